```python
import math
import jax, jax.numpy as jnp
from jax import lax
import numpy as np

D_MODEL = 1024
BATCH = 8
SEQ = 4096
DEPTH = 2

CHUNK = 64
N_MIXERS = 2
N_RWKV = (DEPTH + 1) // 2
N_SSM = DEPTH // 2
RWKV_HEAD = 64
RWKV_HEADS = D_MODEL // RWKV_HEAD
DECAY_LORA = 64
AAA_LORA = 64
GATE_LORA = 160
GN_EPS = 64e-5
SSM_GROUP = 16
SSM_GROUPS = D_MODEL // SSM_GROUP
SSM_STATE = 64
D_FF = 4 * D_MODEL
DN_ALPHA = (2.0 * DEPTH) ** 0.25
DN_BETA = (8.0 * DEPTH) ** -0.25
LN_EPS = 1e-5

kernel_name = "rwkv7_s5_interleaved_deepnorm_trunk"


def layer_norm(x, g, b):
    xf = x.astype(jnp.float32)
    mu = jnp.mean(xf, axis=-1, keepdims=True)
    var = jnp.mean(jnp.square(xf - mu), axis=-1, keepdims=True)
    return ((xf - mu) * lax.rsqrt(var + LN_EPS) * g + b).astype(x.dtype)


def cmul(ar, ai, br, bi):
    return ar * br - ai * bi, ar * bi + ai * br


def rwkv7_time_mix(x, mu, w0, w1, w2, a0, a1, a2, g1, g2, k_k, k_a, r_k,
                   wr, wk, wv, wo, lnx_g, lnx_b):
    bsz, seq, d = x.shape
    f32 = jnp.float32
    xx = jnp.pad(x, ((0, 0), (1, 0), (0, 0)))[:, :-1] - x
    xr = x + xx * mu[0]
    xw = x + xx * mu[1]
    xk = x + xx * mu[2]
    xv = x + xx * mu[3]
    xa = x + xx * mu[4]
    xg = x + xx * mu[5]
    r = xr @ wr
    w_pre = (w0 + jnp.tanh(xw @ w1) @ w2).astype(f32)
    decay = jnp.exp(-jnp.exp(-jax.nn.softplus(-w_pre) - 0.5))
    k = xk @ wk
    v = xv @ wv
    a = jax.nn.sigmoid(a0 + (xa @ a1) @ a2)
    g = jax.nn.sigmoid(xg @ g1) @ g2

    def heads(t):
        return t.astype(f32).reshape(bsz, seq, RWKV_HEADS, RWKV_HEAD)

    kk = heads(k * k_k)
    kk = kk / jnp.maximum(jnp.sqrt(jnp.sum(kk * kk, axis=-1, keepdims=True)), 1e-12)
    k = k * (1.0 + (a - 1.0) * k_a)
    rh, kh, vh, ah, wh = heads(r), heads(k), heads(v), heads(a), heads(decay)
    seq_in = tuple(jnp.moveaxis(t, 1, 0) for t in (rh, wh, kh, vh, -kk, kk * ah))

    def step(state, inp):
        r_t, w_t, k_t, v_t, a_t, b_t = inp
        sa = jnp.einsum('bhvk,bhk->bhv', state, a_t)
        state = (state * w_t[:, :, None, :] + sa[..., None] * b_t[:, :, None, :]
                 + v_t[..., None] * k_t[:, :, None, :])
        return state, jnp.einsum('bhvk,bhk->bhv', state, r_t)

    s0 = jnp.zeros((bsz, RWKV_HEADS, RWKV_HEAD, RWKV_HEAD), f32)
    _, o = lax.scan(step, s0, seq_in)
    o = jnp.moveaxis(o, 0, 1)
    om = jnp.mean(o, axis=-1, keepdims=True)
    ov = jnp.mean(jnp.square(o - om), axis=-1, keepdims=True)
    o = ((o - om) * lax.rsqrt(ov + GN_EPS)).reshape(bsz, seq, d) * lnx_g + lnx_b
    bonus = jnp.sum(rh * kh * r_k.astype(f32), axis=-1, keepdims=True) * vh
    o = o + bonus.reshape(bsz, seq, d)
    return (o.astype(x.dtype) * g) @ wo


def s5_mix(x, a_re, a_im, log_dt, b_re, b_im, c_re, c_im, d_skip, w_glu):
    bsz, seq, d = x.shape
    f32 = jnp.float32
    xf = x.astype(f32)
    dt = jnp.exp(log_dt.astype(f32))[:, None]
    lam_re = jnp.minimum(a_re.astype(f32), -1e-4)
    lam_im = a_im.astype(f32)
    mag = jnp.exp(dt * lam_re)
    abar_re = mag * jnp.cos(dt * lam_im)
    abar_im = mag * jnp.sin(dt * lam_im)
    den = lam_re * lam_re + lam_im * lam_im
    nr, ni = abar_re - 1.0, abar_im
    coef_re = (nr * lam_re + ni * lam_im) / den
    coef_im = (ni * lam_re - nr * lam_im) / den
    bbar_re, bbar_im = cmul(coef_re[..., None], coef_im[..., None],
                            b_re.astype(f32), b_im.astype(f32))
    cr_w, ci_w = c_re.astype(f32), c_im.astype(f32)

    n_chunks = seq // CHUNK
    u = xf.reshape(bsz, n_chunks, CHUNK, SSM_GROUPS, SSM_GROUP).transpose(1, 2, 0, 3, 4)
    a_ch_re = jnp.broadcast_to(abar_re, (CHUNK, 1, SSM_GROUPS, SSM_STATE))
    a_ch_im = jnp.broadcast_to(abar_im, (CHUNK, 1, SSM_GROUPS, SSM_STATE))

    def combine(e1, e2):
        a1r, a1i, b1r, b1i = e1
        a2r, a2i, b2r, b2i = e2
        ar, ai = cmul(a2r, a2i, a1r, a1i)
        br, bi = cmul(a2r, a2i, b1r, b1i)
        return ar, ai, br + b2r, bi + b2i

    def chunk_step(h, u_c):
        h_re, h_im = h
        bu_re = jnp.einsum('tbgc,gpc->tbgp', u_c, bbar_re)
        bu_im = jnp.einsum('tbgc,gpc->tbgp', u_c, bbar_im)
        pr, pi, sr, si = lax.associative_scan(combine, (a_ch_re, a_ch_im, bu_re, bu_im), axis=0)
        carry_re, carry_im = cmul(pr, pi, h_re[None], h_im[None])
        s_re = sr + carry_re
        s_im = si + carry_im
        y = (jnp.einsum('tbgp,gcp->tbgc', s_re, cr_w)
             - jnp.einsum('tbgp,gcp->tbgc', s_im, ci_w))
        return (s_re[-1], s_im[-1]), y

    h0 = jnp.zeros((bsz, SSM_GROUPS, SSM_STATE), f32)
    _, y = lax.scan(chunk_step, (h0, h0), u)
    y = y.transpose(2, 0, 1, 3, 4).reshape(bsz, seq, d) + xf * d_skip.astype(f32)
    y = jax.nn.gelu(y).astype(x.dtype)
    z = y @ w_glu
    return z[..., :d] * jax.nn.sigmoid(z[..., d:])


def sq_relu_mlp(x, w1, w2):
    return jnp.square(jax.nn.relu(x @ w1)) @ w2


def _fwd_setup_inputs(seed: int = 0) -> dict:
    key = jax.random.key(seed)
    ks = list(jax.random.split(key, 40))
    f32 = jnp.float32

    def nrm(i, shape, scale):
        return scale * jax.random.normal(ks[i], shape, f32)

    def uni(i, shape, lo, hi):
        return jax.random.uniform(ks[i], shape, f32, lo, hi)

    C, H, N = D_MODEL, RWKV_HEADS, RWKV_HEAD
    G, P, S = SSM_GROUPS, SSM_STATE, SSM_GROUP
    nr_, ns_ = N_RWKV, N_SSM
    glu = nrm(30, (ns_, C, 2 * C), C ** -0.5)
    glu = glu * jnp.concatenate([jnp.full((C,), DN_BETA, f32), jnp.ones((C,), f32)])
    a_im = jnp.pi * jnp.arange(P, dtype=f32)[None, None, :] + nrm(22, (ns_, G, P), 0.01)
    return {
        "x": nrm(0, (BATCH, SEQ, C), 1.0),
        "ln_g": 1.0 + nrm(1, (2 * DEPTH, C), 0.02),
        "ln_b": nrm(2, (2 * DEPTH, C), 0.02),
        "rw_mu": uni(3, (nr_, 6, C), 0.0, 1.0),
        "rw_w0": uni(4, (nr_, C), -6.0, -1.0),
        "rw_w1": nrm(5, (nr_, C, DECAY_LORA), C ** -0.5),
        "rw_w2": nrm(6, (nr_, DECAY_LORA, C), 0.1 * DECAY_LORA ** -0.5),
        "rw_a0": nrm(7, (nr_, C), 0.1),
        "rw_a1": nrm(8, (nr_, C, AAA_LORA), C ** -0.5),
        "rw_a2": nrm(9, (nr_, AAA_LORA, C), 0.1 * AAA_LORA ** -0.5),
        "rw_g1": nrm(10, (nr_, C, GATE_LORA), C ** -0.5),
        "rw_g2": nrm(11, (nr_, GATE_LORA, C), GATE_LORA ** -0.5),
        "rw_k_k": 0.85 + nrm(12, (nr_, C), 0.02),
        "rw_k_a": 1.0 + nrm(13, (nr_, C), 0.02),
        "rw_r_k": -0.04 + nrm(14, (nr_, H, N), 0.02),
        "rw_wr": nrm(15, (nr_, C, C), C ** -0.5),
        "rw_wk": nrm(16, (nr_, C, C), C ** -0.5),
        "rw_wv": nrm(17, (nr_, C, C), C ** -0.5),
        "rw_wo": nrm(18, (nr_, C, C), DN_BETA * C ** -0.5),
        "rw_lnx_g": 1.0 + nrm(19, (nr_, C), 0.02),
        "rw_lnx_b": nrm(20, (nr_, C), 0.02),
        "s5_a_re": -0.5 + nrm(21, (ns_, G, P), 0.01),
        "s5_a_im": a_im,
        "s5_log_dt": uni(23, (ns_, G), math.log(1e-3), math.log(1e-1)),
        "s5_b_re": nrm(24, (ns_, G, P, S), (2.0 * S) ** -0.5),
        "s5_b_im": nrm(25, (ns_, G, P, S), (2.0 * S) ** -0.5),
        "s5_c_re": nrm(26, (ns_, G, S, P), (2.0 * P) ** -0.5),
        "s5_c_im": nrm(27, (ns_, G, S, P), (2.0 * P) ** -0.5),
        "s5_d": nrm(28, (ns_, C), 1.0),
        "s5_w_glu": glu,
        "mlp_w1": nrm(31, (DEPTH, C, D_FF), C ** -0.5),
        "mlp_w2": nrm(32, (DEPTH, D_FF, C), DN_BETA * D_FF ** -0.5),
    }


def _fwd_reference(x, ln_g, ln_b, rw_mu, rw_w0, rw_w1, rw_w2, rw_a0, rw_a1, rw_a2, rw_g1, rw_g2,
              rw_k_k, rw_k_a, rw_r_k, rw_wr, rw_wk, rw_wv, rw_wo, rw_lnx_g, rw_lnx_b,
              s5_a_re, s5_a_im, s5_log_dt, s5_b_re, s5_b_im, s5_c_re, s5_c_im, s5_d, s5_w_glu,
              mlp_w1, mlp_w2):
    h = x
    for i in range(DEPTH):
        j = i // N_MIXERS
        if i % N_MIXERS == 0:
            mix = rwkv7_time_mix(h, rw_mu[j], rw_w0[j], rw_w1[j], rw_w2[j], rw_a0[j], rw_a1[j],
                                 rw_a2[j], rw_g1[j], rw_g2[j], rw_k_k[j], rw_k_a[j], rw_r_k[j],
                                 rw_wr[j], rw_wk[j], rw_wv[j], rw_wo[j], rw_lnx_g[j], rw_lnx_b[j])
        else:
            mix = s5_mix(h, s5_a_re[j], s5_a_im[j], s5_log_dt[j], s5_b_re[j], s5_b_im[j],
                         s5_c_re[j], s5_c_im[j], s5_d[j], s5_w_glu[j])
        h = layer_norm(DN_ALPHA * h + mix, ln_g[2 * i], ln_b[2 * i])
        h = layer_norm(DN_ALPHA * h + sq_relu_mlp(h, mlp_w1[i], mlp_w2[i]),
                       ln_g[2 * i + 1], ln_b[2 * i + 1])
    return h


import jax as _jax
import jax.numpy as _jnp

TWIN_FORMAT = 'train_step'
FWD_PARAMS = ['x', 'ln_g', 'ln_b', 'rw_mu', 'rw_w0', 'rw_w1', 'rw_w2', 'rw_a0', 'rw_a1', 'rw_a2', 'rw_g1', 'rw_g2', 'rw_k_k', 'rw_k_a', 'rw_r_k', 'rw_wr', 'rw_wk', 'rw_wv', 'rw_wo', 'rw_lnx_g', 'rw_lnx_b', 's5_a_re', 's5_a_im', 's5_log_dt', 's5_b_re', 's5_b_im', 's5_c_re', 's5_c_im', 's5_d', 's5_w_glu', 'mlp_w1', 'mlp_w2']
TWIN_WEIGHTS = ['ln_g', 'ln_b', 'rw_mu', 'rw_w0', 'rw_w1', 'rw_w2', 'rw_a0', 'rw_a1', 'rw_a2', 'rw_g1', 'rw_g2', 'rw_k_k', 'rw_k_a', 'rw_r_k', 'rw_wr', 'rw_wk', 'rw_wv', 'rw_wo', 'rw_lnx_g', 'rw_lnx_b', 's5_a_re', 's5_a_im', 's5_log_dt', 's5_b_re', 's5_b_im', 's5_c_re', 's5_c_im', 's5_d', 's5_w_glu', 'mlp_w1', 'mlp_w2']
TWIN_DIFF_INPUT = 'x'
TWIN_INPUTS = ['x', 'ln_g', 'ln_b', 'rw_mu', 'rw_w0', 'rw_w1', 'rw_w2', 'rw_a0', 'rw_a1', 'rw_a2', 'rw_g1', 'rw_g2', 'rw_k_k', 'rw_k_a', 'rw_r_k', 'rw_wr', 'rw_wk', 'rw_wv', 'rw_wo', 'rw_lnx_g', 'rw_lnx_b', 's5_a_re', 's5_a_im', 's5_log_dt', 's5_b_re', 's5_b_im', 's5_c_re', 's5_c_im', 's5_d', 's5_w_glu', 'mlp_w1', 'mlp_w2', 'loss_target', 'm_ln_g', 'm_ln_b', 'm_rw_mu', 'm_rw_w0', 'm_rw_w1', 'm_rw_w2', 'm_rw_a0', 'm_rw_a1', 'm_rw_a2', 'm_rw_g1', 'm_rw_g2', 'm_rw_k_k', 'm_rw_k_a', 'm_rw_r_k', 'm_rw_wr', 'm_rw_wk', 'm_rw_wv', 'm_rw_wo', 'm_rw_lnx_g', 'm_rw_lnx_b', 'm_s5_a_re', 'm_s5_a_im', 'm_s5_log_dt', 'm_s5_b_re', 'm_s5_b_im', 'm_s5_c_re', 'm_s5_c_im', 'm_s5_d', 'm_s5_w_glu', 'm_mlp_w1', 'm_mlp_w2', 'v_ln_g', 'v_ln_b', 'v_rw_mu', 'v_rw_w0', 'v_rw_w1', 'v_rw_w2', 'v_rw_a0', 'v_rw_a1', 'v_rw_a2', 'v_rw_g1', 'v_rw_g2', 'v_rw_k_k', 'v_rw_k_a', 'v_rw_r_k', 'v_rw_wr', 'v_rw_wk', 'v_rw_wv', 'v_rw_wo', 'v_rw_lnx_g', 'v_rw_lnx_b', 'v_s5_a_re', 'v_s5_a_im', 'v_s5_log_dt', 'v_s5_b_re', 'v_s5_b_im', 'v_s5_c_re', 'v_s5_c_im', 'v_s5_d', 'v_s5_w_glu', 'v_mlp_w1', 'v_mlp_w2']
TWIN_OUTPUTS = ['loss', 'grad_x', 'grad_ln_g', 'grad_ln_b', 'grad_rw_mu', 'grad_rw_w0', 'grad_rw_w1', 'grad_rw_w2', 'grad_rw_a0', 'grad_rw_a1', 'grad_rw_a2', 'grad_rw_g1', 'grad_rw_g2', 'grad_rw_k_k', 'grad_rw_k_a', 'grad_rw_r_k', 'grad_rw_wr', 'grad_rw_wk', 'grad_rw_wv', 'grad_rw_wo', 'grad_rw_lnx_g', 'grad_rw_lnx_b', 'grad_s5_a_re', 'grad_s5_a_im', 'grad_s5_log_dt', 'grad_s5_b_re', 'grad_s5_b_im', 'grad_s5_c_re', 'grad_s5_c_im', 'grad_s5_d', 'grad_s5_w_glu', 'grad_mlp_w1', 'grad_mlp_w2', 'delta_ln_g', 'delta_ln_b', 'delta_rw_mu', 'delta_rw_w0', 'delta_rw_w1', 'delta_rw_w2', 'delta_rw_a0', 'delta_rw_a1', 'delta_rw_a2', 'delta_rw_g1', 'delta_rw_g2', 'delta_rw_k_k', 'delta_rw_k_a', 'delta_rw_r_k', 'delta_rw_wr', 'delta_rw_wk', 'delta_rw_wv', 'delta_rw_wo', 'delta_rw_lnx_g', 'delta_rw_lnx_b', 'delta_s5_a_re', 'delta_s5_a_im', 'delta_s5_log_dt', 'delta_s5_b_re', 'delta_s5_b_im', 'delta_s5_c_re', 'delta_s5_c_im', 'delta_s5_d', 'delta_s5_w_glu', 'delta_mlp_w1', 'delta_mlp_w2', 'new_m_ln_g', 'new_m_ln_b', 'new_m_rw_mu', 'new_m_rw_w0', 'new_m_rw_w1', 'new_m_rw_w2', 'new_m_rw_a0', 'new_m_rw_a1', 'new_m_rw_a2', 'new_m_rw_g1', 'new_m_rw_g2', 'new_m_rw_k_k', 'new_m_rw_k_a', 'new_m_rw_r_k', 'new_m_rw_wr', 'new_m_rw_wk', 'new_m_rw_wv', 'new_m_rw_wo', 'new_m_rw_lnx_g', 'new_m_rw_lnx_b', 'new_m_s5_a_re', 'new_m_s5_a_im', 'new_m_s5_log_dt', 'new_m_s5_b_re', 'new_m_s5_b_im', 'new_m_s5_c_re', 'new_m_s5_c_im', 'new_m_s5_d', 'new_m_s5_w_glu', 'new_m_mlp_w1', 'new_m_mlp_w2', 'new_v_ln_g', 'new_v_ln_b', 'new_v_rw_mu', 'new_v_rw_w0', 'new_v_rw_w1', 'new_v_rw_w2', 'new_v_rw_a0', 'new_v_rw_a1', 'new_v_rw_a2', 'new_v_rw_g1', 'new_v_rw_g2', 'new_v_rw_k_k', 'new_v_rw_k_a', 'new_v_rw_r_k', 'new_v_rw_wr', 'new_v_rw_wk', 'new_v_rw_wv', 'new_v_rw_wo', 'new_v_rw_lnx_g', 'new_v_rw_lnx_b', 'new_v_s5_a_re', 'new_v_s5_a_im', 'new_v_s5_log_dt', 'new_v_s5_b_re', 'new_v_s5_b_im', 'new_v_s5_c_re', 'new_v_s5_c_im', 'new_v_s5_d', 'new_v_s5_w_glu', 'new_v_mlp_w1', 'new_v_mlp_w2']
TWIN_LEAF_KINDS = {'loss': 'loss', 'grad_x': 'grad_x', 'grad_ln_g': 'grad_w', 'grad_ln_b': 'grad_w', 'grad_rw_mu': 'grad_w', 'grad_rw_w0': 'grad_w', 'grad_rw_w1': 'grad_w', 'grad_rw_w2': 'grad_w', 'grad_rw_a0': 'grad_w', 'grad_rw_a1': 'grad_w', 'grad_rw_a2': 'grad_w', 'grad_rw_g1': 'grad_w', 'grad_rw_g2': 'grad_w', 'grad_rw_k_k': 'grad_w', 'grad_rw_k_a': 'grad_w', 'grad_rw_r_k': 'grad_w', 'grad_rw_wr': 'grad_w', 'grad_rw_wk': 'grad_w', 'grad_rw_wv': 'grad_w', 'grad_rw_wo': 'grad_w', 'grad_rw_lnx_g': 'grad_w', 'grad_rw_lnx_b': 'grad_w', 'grad_s5_a_re': 'grad_w', 'grad_s5_a_im': 'grad_w', 'grad_s5_log_dt': 'grad_w', 'grad_s5_b_re': 'grad_w', 'grad_s5_b_im': 'grad_w', 'grad_s5_c_re': 'grad_w', 'grad_s5_c_im': 'grad_w', 'grad_s5_d': 'grad_w', 'grad_s5_w_glu': 'grad_w', 'grad_mlp_w1': 'grad_w', 'grad_mlp_w2': 'grad_w', 'delta_ln_g': 'delta_w', 'delta_ln_b': 'delta_w', 'delta_rw_mu': 'delta_w', 'delta_rw_w0': 'delta_w', 'delta_rw_w1': 'delta_w', 'delta_rw_w2': 'delta_w', 'delta_rw_a0': 'delta_w', 'delta_rw_a1': 'delta_w', 'delta_rw_a2': 'delta_w', 'delta_rw_g1': 'delta_w', 'delta_rw_g2': 'delta_w', 'delta_rw_k_k': 'delta_w', 'delta_rw_k_a': 'delta_w', 'delta_rw_r_k': 'delta_w', 'delta_rw_wr': 'delta_w', 'delta_rw_wk': 'delta_w', 'delta_rw_wv': 'delta_w', 'delta_rw_wo': 'delta_w', 'delta_rw_lnx_g': 'delta_w', 'delta_rw_lnx_b': 'delta_w', 'delta_s5_a_re': 'delta_w', 'delta_s5_a_im': 'delta_w', 'delta_s5_log_dt': 'delta_w', 'delta_s5_b_re': 'delta_w', 'delta_s5_b_im': 'delta_w', 'delta_s5_c_re': 'delta_w', 'delta_s5_c_im': 'delta_w', 'delta_s5_d': 'delta_w', 'delta_s5_w_glu': 'delta_w', 'delta_mlp_w1': 'delta_w', 'delta_mlp_w2': 'delta_w', 'new_m_ln_g': 'new_m', 'new_m_ln_b': 'new_m', 'new_m_rw_mu': 'new_m', 'new_m_rw_w0': 'new_m', 'new_m_rw_w1': 'new_m', 'new_m_rw_w2': 'new_m', 'new_m_rw_a0': 'new_m', 'new_m_rw_a1': 'new_m', 'new_m_rw_a2': 'new_m', 'new_m_rw_g1': 'new_m', 'new_m_rw_g2': 'new_m', 'new_m_rw_k_k': 'new_m', 'new_m_rw_k_a': 'new_m', 'new_m_rw_r_k': 'new_m', 'new_m_rw_wr': 'new_m', 'new_m_rw_wk': 'new_m', 'new_m_rw_wv': 'new_m', 'new_m_rw_wo': 'new_m', 'new_m_rw_lnx_g': 'new_m', 'new_m_rw_lnx_b': 'new_m', 'new_m_s5_a_re': 'new_m', 'new_m_s5_a_im': 'new_m', 'new_m_s5_log_dt': 'new_m', 'new_m_s5_b_re': 'new_m', 'new_m_s5_b_im': 'new_m', 'new_m_s5_c_re': 'new_m', 'new_m_s5_c_im': 'new_m', 'new_m_s5_d': 'new_m', 'new_m_s5_w_glu': 'new_m', 'new_m_mlp_w1': 'new_m', 'new_m_mlp_w2': 'new_m', 'new_v_ln_g': 'new_v', 'new_v_ln_b': 'new_v', 'new_v_rw_mu': 'new_v', 'new_v_rw_w0': 'new_v', 'new_v_rw_w1': 'new_v', 'new_v_rw_w2': 'new_v', 'new_v_rw_a0': 'new_v', 'new_v_rw_a1': 'new_v', 'new_v_rw_a2': 'new_v', 'new_v_rw_g1': 'new_v', 'new_v_rw_g2': 'new_v', 'new_v_rw_k_k': 'new_v', 'new_v_rw_k_a': 'new_v', 'new_v_rw_r_k': 'new_v', 'new_v_rw_wr': 'new_v', 'new_v_rw_wk': 'new_v', 'new_v_rw_wv': 'new_v', 'new_v_rw_wo': 'new_v', 'new_v_rw_lnx_g': 'new_v', 'new_v_rw_lnx_b': 'new_v', 'new_v_s5_a_re': 'new_v', 'new_v_s5_a_im': 'new_v', 'new_v_s5_log_dt': 'new_v', 'new_v_s5_b_re': 'new_v', 'new_v_s5_b_im': 'new_v', 'new_v_s5_c_re': 'new_v', 'new_v_s5_c_im': 'new_v', 'new_v_s5_d': 'new_v', 'new_v_s5_w_glu': 'new_v', 'new_v_mlp_w1': 'new_v', 'new_v_mlp_w2': 'new_v'}


def _forward(args):
    return _fwd_reference(*[args[k] for k in FWD_PARAMS])


def _output_shape():
    def fwd():
        inp = _fwd_setup_inputs(0)
        return _fwd_reference(*[inp[k] for k in FWD_PARAMS])
    out = _jax.eval_shape(fwd)
    return out.shape, out.dtype

N_MICROBATCH = 1
ADAM_LR = 0.001
ADAM_B1 = 0.9
ADAM_B2 = 0.999
ADAM_EPS = 1e-08
ADAM_WD = 0.01
ADAM_STEP = 10
PER_EXAMPLE_BATCH_AXIS = {'x': 0, 'loss_target': 0}
SHARED_INPUTS = []
_WEIGHT_DTYPES = {'ln_g': _jnp.float32, 'ln_b': _jnp.float32, 'rw_mu': _jnp.float32, 'rw_w0': _jnp.float32, 'rw_w1': _jnp.float32, 'rw_w2': _jnp.float32, 'rw_a0': _jnp.float32, 'rw_a1': _jnp.float32, 'rw_a2': _jnp.float32, 'rw_g1': _jnp.float32, 'rw_g2': _jnp.float32, 'rw_k_k': _jnp.float32, 'rw_k_a': _jnp.float32, 'rw_r_k': _jnp.float32, 'rw_wr': _jnp.float32, 'rw_wk': _jnp.float32, 'rw_wv': _jnp.float32, 'rw_wo': _jnp.float32, 'rw_lnx_g': _jnp.float32, 'rw_lnx_b': _jnp.float32, 's5_a_re': _jnp.float32, 's5_a_im': _jnp.float32, 's5_log_dt': _jnp.float32, 's5_b_re': _jnp.float32, 's5_b_im': _jnp.float32, 's5_c_re': _jnp.float32, 's5_c_im': _jnp.float32, 's5_d': _jnp.float32, 's5_w_glu': _jnp.float32, 'mlp_w1': _jnp.float32, 'mlp_w2': _jnp.float32}
MOMENT_SCALE = {'ln_g': 1.608856e+01, 'ln_b': 3.865760e+00, 'rw_mu': 4.145041e-02, 'rw_w0': 1.156324e-02, 'rw_w1': 6.260522e-04, 'rw_w2': 1.524222e-03, 'rw_a0': 1.512022e-02, 'rw_a1': 5.637444e-03, 'rw_a2': 1.421171e-02, 'rw_g1': 2.946198e-02, 'rw_g2': 3.454727e-02, 'rw_k_k': 4.139227e-02, 'rw_k_a': 3.718240e-02, 'rw_r_k': 8.028357e-02, 'rw_wr': 3.521451e-02, 'rw_wk': 3.796385e-02, 'rw_wv': 3.395549e-02, 'rw_wo': 6.721780e-02, 'rw_lnx_g': 3.399640e-02, 'rw_lnx_b': 1.080031e-01, 's5_a_re': 2.282120e-03, 's5_a_im': 1.357330e-03, 's5_log_dt': 1.135149e+00, 's5_b_re': 9.817728e-04, 's5_b_im': 9.629691e-04, 's5_c_re': 2.196442e-03, 's5_c_im': 1.915261e-03, 's5_d': 4.038435e-02, 's5_w_glu': 5.692793e-02, 'mlp_w1': 4.219523e-02, 'mlp_w2': 2.258109e-01}


def _to_microbatches(a, axis):
    t = _jnp.moveaxis(a, axis, 0)
    t = t.reshape((N_MICROBATCH, t.shape[0] // N_MICROBATCH) + t.shape[1:])
    return _jnp.moveaxis(t, 1, axis + 1)


def setup_inputs(seed: int = 0) -> dict:
    inp = _fwd_setup_inputs(seed)
    key = _jax.random.fold_in(_jax.random.key(seed), 7919)
    shape, _ = _output_shape()
    out = dict(inp)
    out["loss_target"] = _jax.random.normal(_jax.random.fold_in(key, 0), shape, _jnp.float32)
    for i, name in enumerate(TWIN_WEIGHTS):
        w = inp[name].astype(_jnp.float32)
        if MOMENT_SCALE is None:
            s = _jnp.sqrt(_jnp.mean(_jnp.square(w)) + 1e-30)
        else:
            s = MOMENT_SCALE[name]
        km, kv = _jax.random.split(_jax.random.fold_in(key, i + 1))
        out[name] = w
        out["m_" + name] = s * _jax.random.normal(km, w.shape, _jnp.float32)
        out["v_" + name] = (s * s) * _jax.random.uniform(kv, w.shape, _jnp.float32, 0.5, 1.5)
    if N_MICROBATCH > 1:
        for name, axis in PER_EXAMPLE_BATCH_AXIS.items():
            out[name] = _to_microbatches(out[name], axis)
    return {'x': out['x'], 'ln_g': out['ln_g'], 'ln_b': out['ln_b'], 'rw_mu': out['rw_mu'], 'rw_w0': out['rw_w0'], 'rw_w1': out['rw_w1'], 'rw_w2': out['rw_w2'], 'rw_a0': out['rw_a0'], 'rw_a1': out['rw_a1'], 'rw_a2': out['rw_a2'], 'rw_g1': out['rw_g1'], 'rw_g2': out['rw_g2'], 'rw_k_k': out['rw_k_k'], 'rw_k_a': out['rw_k_a'], 'rw_r_k': out['rw_r_k'], 'rw_wr': out['rw_wr'], 'rw_wk': out['rw_wk'], 'rw_wv': out['rw_wv'], 'rw_wo': out['rw_wo'], 'rw_lnx_g': out['rw_lnx_g'], 'rw_lnx_b': out['rw_lnx_b'], 's5_a_re': out['s5_a_re'], 's5_a_im': out['s5_a_im'], 's5_log_dt': out['s5_log_dt'], 's5_b_re': out['s5_b_re'], 's5_b_im': out['s5_b_im'], 's5_c_re': out['s5_c_re'], 's5_c_im': out['s5_c_im'], 's5_d': out['s5_d'], 's5_w_glu': out['s5_w_glu'], 'mlp_w1': out['mlp_w1'], 'mlp_w2': out['mlp_w2'], 'loss_target': out['loss_target'], 'm_ln_g': out['m_ln_g'], 'm_ln_b': out['m_ln_b'], 'm_rw_mu': out['m_rw_mu'], 'm_rw_w0': out['m_rw_w0'], 'm_rw_w1': out['m_rw_w1'], 'm_rw_w2': out['m_rw_w2'], 'm_rw_a0': out['m_rw_a0'], 'm_rw_a1': out['m_rw_a1'], 'm_rw_a2': out['m_rw_a2'], 'm_rw_g1': out['m_rw_g1'], 'm_rw_g2': out['m_rw_g2'], 'm_rw_k_k': out['m_rw_k_k'], 'm_rw_k_a': out['m_rw_k_a'], 'm_rw_r_k': out['m_rw_r_k'], 'm_rw_wr': out['m_rw_wr'], 'm_rw_wk': out['m_rw_wk'], 'm_rw_wv': out['m_rw_wv'], 'm_rw_wo': out['m_rw_wo'], 'm_rw_lnx_g': out['m_rw_lnx_g'], 'm_rw_lnx_b': out['m_rw_lnx_b'], 'm_s5_a_re': out['m_s5_a_re'], 'm_s5_a_im': out['m_s5_a_im'], 'm_s5_log_dt': out['m_s5_log_dt'], 'm_s5_b_re': out['m_s5_b_re'], 'm_s5_b_im': out['m_s5_b_im'], 'm_s5_c_re': out['m_s5_c_re'], 'm_s5_c_im': out['m_s5_c_im'], 'm_s5_d': out['m_s5_d'], 'm_s5_w_glu': out['m_s5_w_glu'], 'm_mlp_w1': out['m_mlp_w1'], 'm_mlp_w2': out['m_mlp_w2'], 'v_ln_g': out['v_ln_g'], 'v_ln_b': out['v_ln_b'], 'v_rw_mu': out['v_rw_mu'], 'v_rw_w0': out['v_rw_w0'], 'v_rw_w1': out['v_rw_w1'], 'v_rw_w2': out['v_rw_w2'], 'v_rw_a0': out['v_rw_a0'], 'v_rw_a1': out['v_rw_a1'], 'v_rw_a2': out['v_rw_a2'], 'v_rw_g1': out['v_rw_g1'], 'v_rw_g2': out['v_rw_g2'], 'v_rw_k_k': out['v_rw_k_k'], 'v_rw_k_a': out['v_rw_k_a'], 'v_rw_r_k': out['v_rw_r_k'], 'v_rw_wr': out['v_rw_wr'], 'v_rw_wk': out['v_rw_wk'], 'v_rw_wv': out['v_rw_wv'], 'v_rw_wo': out['v_rw_wo'], 'v_rw_lnx_g': out['v_rw_lnx_g'], 'v_rw_lnx_b': out['v_rw_lnx_b'], 'v_s5_a_re': out['v_s5_a_re'], 'v_s5_a_im': out['v_s5_a_im'], 'v_s5_log_dt': out['v_s5_log_dt'], 'v_s5_b_re': out['v_s5_b_re'], 'v_s5_b_im': out['v_s5_b_im'], 'v_s5_c_re': out['v_s5_c_re'], 'v_s5_c_im': out['v_s5_c_im'], 'v_s5_d': out['v_s5_d'], 'v_s5_w_glu': out['v_s5_w_glu'], 'v_mlp_w1': out['v_mlp_w1'], 'v_mlp_w2': out['v_mlp_w2']}


def _loss(weights, diff, rest, loss_target):
    with _jax.named_scope("forward"):
        args = {**rest, TWIN_DIFF_INPUT: diff, **{k: w.astype(_WEIGHT_DTYPES[k]) for k, w in weights.items()}}
        y = _forward(args)
    with _jax.named_scope("loss_head"):
        err = _jnp.square(y.astype(_jnp.float32) - loss_target)
        return 0.5 * _jnp.sum(_jnp.mean(err, axis=-1)) if err.ndim else 0.5 * err


def _adamw(w, g, m, v):
    m = ADAM_B1 * m + (1.0 - ADAM_B1) * g
    v = ADAM_B2 * v + (1.0 - ADAM_B2) * _jnp.square(g)
    m_hat = m / (1.0 - ADAM_B1 ** ADAM_STEP)
    v_hat = v / (1.0 - ADAM_B2 ** ADAM_STEP)
    delta = -ADAM_LR * (m_hat / (_jnp.sqrt(v_hat) + ADAM_EPS) + ADAM_WD * w)
    return delta, m, v


def reference(x, ln_g, ln_b, rw_mu, rw_w0, rw_w1, rw_w2, rw_a0, rw_a1, rw_a2, rw_g1, rw_g2, rw_k_k, rw_k_a, rw_r_k, rw_wr, rw_wk, rw_wv, rw_wo, rw_lnx_g, rw_lnx_b, s5_a_re, s5_a_im, s5_log_dt, s5_b_re, s5_b_im, s5_c_re, s5_c_im, s5_d, s5_w_glu, mlp_w1, mlp_w2, loss_target, m_ln_g, m_ln_b, m_rw_mu, m_rw_w0, m_rw_w1, m_rw_w2, m_rw_a0, m_rw_a1, m_rw_a2, m_rw_g1, m_rw_g2, m_rw_k_k, m_rw_k_a, m_rw_r_k, m_rw_wr, m_rw_wk, m_rw_wv, m_rw_wo, m_rw_lnx_g, m_rw_lnx_b, m_s5_a_re, m_s5_a_im, m_s5_log_dt, m_s5_b_re, m_s5_b_im, m_s5_c_re, m_s5_c_im, m_s5_d, m_s5_w_glu, m_mlp_w1, m_mlp_w2, v_ln_g, v_ln_b, v_rw_mu, v_rw_w0, v_rw_w1, v_rw_w2, v_rw_a0, v_rw_a1, v_rw_a2, v_rw_g1, v_rw_g2, v_rw_k_k, v_rw_k_a, v_rw_r_k, v_rw_wr, v_rw_wk, v_rw_wv, v_rw_wo, v_rw_lnx_g, v_rw_lnx_b, v_s5_a_re, v_s5_a_im, v_s5_log_dt, v_s5_b_re, v_s5_b_im, v_s5_c_re, v_s5_c_im, v_s5_d, v_s5_w_glu, v_mlp_w1, v_mlp_w2):
    given = dict(x=x, ln_g=ln_g, ln_b=ln_b, rw_mu=rw_mu, rw_w0=rw_w0, rw_w1=rw_w1, rw_w2=rw_w2, rw_a0=rw_a0, rw_a1=rw_a1, rw_a2=rw_a2, rw_g1=rw_g1, rw_g2=rw_g2, rw_k_k=rw_k_k, rw_k_a=rw_k_a, rw_r_k=rw_r_k, rw_wr=rw_wr, rw_wk=rw_wk, rw_wv=rw_wv, rw_wo=rw_wo, rw_lnx_g=rw_lnx_g, rw_lnx_b=rw_lnx_b, s5_a_re=s5_a_re, s5_a_im=s5_a_im, s5_log_dt=s5_log_dt, s5_b_re=s5_b_re, s5_b_im=s5_b_im, s5_c_re=s5_c_re, s5_c_im=s5_c_im, s5_d=s5_d, s5_w_glu=s5_w_glu, mlp_w1=mlp_w1, mlp_w2=mlp_w2, loss_target=loss_target, m_ln_g=m_ln_g, m_ln_b=m_ln_b, m_rw_mu=m_rw_mu, m_rw_w0=m_rw_w0, m_rw_w1=m_rw_w1, m_rw_w2=m_rw_w2, m_rw_a0=m_rw_a0, m_rw_a1=m_rw_a1, m_rw_a2=m_rw_a2, m_rw_g1=m_rw_g1, m_rw_g2=m_rw_g2, m_rw_k_k=m_rw_k_k, m_rw_k_a=m_rw_k_a, m_rw_r_k=m_rw_r_k, m_rw_wr=m_rw_wr, m_rw_wk=m_rw_wk, m_rw_wv=m_rw_wv, m_rw_wo=m_rw_wo, m_rw_lnx_g=m_rw_lnx_g, m_rw_lnx_b=m_rw_lnx_b, m_s5_a_re=m_s5_a_re, m_s5_a_im=m_s5_a_im, m_s5_log_dt=m_s5_log_dt, m_s5_b_re=m_s5_b_re, m_s5_b_im=m_s5_b_im, m_s5_c_re=m_s5_c_re, m_s5_c_im=m_s5_c_im, m_s5_d=m_s5_d, m_s5_w_glu=m_s5_w_glu, m_mlp_w1=m_mlp_w1, m_mlp_w2=m_mlp_w2, v_ln_g=v_ln_g, v_ln_b=v_ln_b, v_rw_mu=v_rw_mu, v_rw_w0=v_rw_w0, v_rw_w1=v_rw_w1, v_rw_w2=v_rw_w2, v_rw_a0=v_rw_a0, v_rw_a1=v_rw_a1, v_rw_a2=v_rw_a2, v_rw_g1=v_rw_g1, v_rw_g2=v_rw_g2, v_rw_k_k=v_rw_k_k, v_rw_k_a=v_rw_k_a, v_rw_r_k=v_rw_r_k, v_rw_wr=v_rw_wr, v_rw_wk=v_rw_wk, v_rw_wv=v_rw_wv, v_rw_wo=v_rw_wo, v_rw_lnx_g=v_rw_lnx_g, v_rw_lnx_b=v_rw_lnx_b, v_s5_a_re=v_s5_a_re, v_s5_a_im=v_s5_a_im, v_s5_log_dt=v_s5_log_dt, v_s5_b_re=v_s5_b_re, v_s5_b_im=v_s5_b_im, v_s5_c_re=v_s5_c_re, v_s5_c_im=v_s5_c_im, v_s5_d=v_s5_d, v_s5_w_glu=v_s5_w_glu, v_mlp_w1=v_mlp_w1, v_mlp_w2=v_mlp_w2)
    weights = {n: given[n] for n in TWIN_WEIGHTS}
    shared = {n: given[n] for n in SHARED_INPUTS}
    per_example = {n: given[n] for n in ['x']}
    grad_fn = _jax.value_and_grad(_loss, argnums=(0, 1))

    def one_microbatch(ex, loss_target):
        ex = dict(ex)
        diff = ex.pop(TWIN_DIFF_INPUT)
        return grad_fn(weights, diff, {**shared, **ex}, loss_target)

    if N_MICROBATCH == 1:
        loss, (grad_w, grad_x) = one_microbatch(per_example, given["loss_target"])
    else:
        def body(carry, xs):
            loss_sum, grad_sum = carry
            l_k, (gw_k, gx_k) = one_microbatch(xs[0], xs[1])
            with _jax.named_scope("update"):
                return (loss_sum + l_k, _jax.tree.map(_jnp.add, grad_sum, gw_k)), gx_k

        init = (_jnp.zeros((), _jnp.float32), _jax.tree.map(_jnp.zeros_like, weights))
        (loss, grad_w), grad_x = _jax.lax.scan(body, init, (per_example, given["loss_target"]))
    with _jax.named_scope("update"):
        delta_w, new_m, new_v = {}, {}, {}
        for n in TWIN_WEIGHTS:
            delta_w[n], new_m[n], new_v[n] = _adamw(weights[n], grad_w[n], given["m_" + n], given["v_" + n])
    return (loss, grad_x, *[grad_w[n] for n in TWIN_WEIGHTS], *[delta_w[n] for n in TWIN_WEIGHTS],
            *[new_m[n] for n in TWIN_WEIGHTS], *[new_v[n] for n in TWIN_WEIGHTS])
```

```python
import functools
import math

import jax
import jax.numpy as jnp
from jax import lax
from jax.experimental import pallas as pl
from jax.experimental.pallas import tpu as pltpu

F32 = jnp.float32
BF16 = jnp.bfloat16
HI = lax.Precision.HIGHEST

D_MODEL = 1024
HEAD = 64
PAIR = 2 * HEAD
N_PAIR = D_MODEL // PAIR
CHUNK = 64
GN_EPS = 64e-5
LN_EPS = 1e-5
SSM_GROUP = 16
SSM_STATE = 64
SSM_GROUPS = D_MODEL // SSM_GROUP
SSM_LANES = SSM_GROUPS * SSM_STATE
SSM_BLK_IN = 128
SSM_BLK_ST = 512
N_SSM_BLK = D_MODEL // SSM_BLK_IN
DEPTH = 2
DN_ALPHA = (2.0 * DEPTH) ** 0.25
ADAM_LR, ADAM_B1, ADAM_B2, ADAM_EPS, ADAM_WD, ADAM_STEP = 0.001, 0.9, 0.999, 1e-08, 0.01, 10
N_DEV = 8
MESH_AXES = ("x", "y", "c")
VMEM_LIMIT = 56 * 1024 * 1024
TOKEN_TILE = 256
PACK_W = 1024

SHARDED = {
    "rw_mu": 2, "rw_w1": 1, "rw_w2": 2, "rw_a1": 1, "rw_a2": 2, "rw_g1": 1, "rw_g2": 2,
    "rw_wr": 1, "rw_wk": 1, "rw_wv": 1, "rw_wo": 1, "s5_d": 1, "s5_w_glu": 2, "mlp_w1": 2, "mlp_w2": 1,
}
WEIGHTS = ['ln_g', 'ln_b', 'rw_mu', 'rw_w0', 'rw_w1', 'rw_w2', 'rw_a0', 'rw_a1', 'rw_a2', 'rw_g1', 'rw_g2', 'rw_k_k',
           'rw_k_a', 'rw_r_k', 'rw_wr', 'rw_wk', 'rw_wv', 'rw_wo', 'rw_lnx_g', 'rw_lnx_b', 's5_a_re', 's5_a_im',
           's5_log_dt', 's5_b_re', 's5_b_im', 's5_c_re', 's5_c_im', 's5_d', 's5_w_glu', 'mlp_w1', 'mlp_w2']
REPLICATED = [n for n in WEIGHTS if n not in SHARDED]


def _cparams(n_grid):
    return pltpu.CompilerParams(dimension_semantics=("arbitrary",) * n_grid, vmem_limit_bytes=VMEM_LIMIT)


@jax.custom_vjp
def _mm(x, w):
    return jnp.dot(x.astype(BF16), w.astype(BF16), preferred_element_type=F32)


def _mm_fwd(x, w):
    return _mm(x, w), (x, w)


def _mm_bwd(res, dy):
    x, w = res
    dyb = dy.astype(BF16)
    dx = lax.dot_general(dyb, w.astype(BF16), (((1,), (1,)), ((), ())), preferred_element_type=F32)
    dw = lax.dot_general(x.astype(BF16), dyb, (((0,), (0,)), ((), ())), preferred_element_type=F32)
    return dx, dw


_mm.defvjp(_mm_fwd, _mm_bwd)


def _hdot(a, b):
    return jnp.dot(a, b, precision=HI, preferred_element_type=F32)


def _hdot_nt(a, b):
    return lax.dot_general(a, b, (((1,), (1,)), ((), ())), precision=HI, preferred_element_type=F32)


def _hdot_tn(a, b):
    return lax.dot_general(a, b, (((0,), (0,)), ((), ())), precision=HI, preferred_element_type=F32)


@functools.partial(jax.custom_vjp, nondiff_argnums=(1,))
def _roll_rows(x, shift):
    return pltpu.roll(x, shift, 0)


def _roll_rows_fwd(x, shift):
    return pltpu.roll(x, shift, 0), None


def _roll_rows_bwd(shift, _, dy):
    n = dy.shape[0]
    return (pltpu.roll(dy, (n - shift) % n, 0),)


_roll_rows.defvjp(_roll_rows_fwd, _roll_rows_bwd)


def _shift_down(x, first_row):
    row = lax.broadcasted_iota(jnp.int32, x.shape, 0)
    return jnp.where(row == 0, first_row, _roll_rows(x, 1))


def _sigmoid(x):
    return 1.0 / (1.0 + jnp.exp(-x))


def _softplus(x):
    return jnp.maximum(x, 0.0) + jnp.log(1.0 + jnp.exp(-jnp.abs(x)))


def _gelu(x):
    return 0.5 * x * (1.0 + jnp.tanh(math.sqrt(2.0 / math.pi) * (x + 0.044715 * (x * x * x))))


def _layer_norm(x, g, b):
    mu = jnp.mean(x, axis=-1, keepdims=True)
    xc = x - mu
    var = jnp.mean(xc * xc, axis=-1, keepdims=True)
    return xc * lax.rsqrt(var + LN_EPS) * g + b


def _spec(block, index_map):
    return pl.BlockSpec(block, index_map)


def _tile(arr, tm):
    return (arr, (tm, arr.shape[1]), lambda i: (i, 0))


def _full(arr):
    nd = arr.ndim
    return (arr, arr.shape, lambda *_: (0,) * nd)


def _fwd_call(name, fn, grid, prims, outs):
    n_in = len(prims)

    def body(*refs):
        vals = [r[...] for r in refs[:n_in]]
        vals = [v.astype(F32) if v.dtype != F32 else v for v in vals]
        res = fn(*vals)
        for o, r in zip(refs[n_in:], res):
            o[...] = r.astype(o.dtype)

    return pl.pallas_call(
        body, name=name, grid=grid,
        in_specs=[_spec(b, m) for _, b, m in prims],
        out_specs=[_spec(b, m) for _, _, b, m in outs],
        out_shape=[jax.ShapeDtypeStruct(s, d) for s, d, _, _ in outs],
        compiler_params=_cparams(len(grid)),
    )(*[a for a, _, _ in prims])


def _bwd_call(name, fn, grid, prims, cts, wants, addends=None):
    addends = addends or {}
    n_in, n_ct = len(prims), len(cts)
    out_place = {}
    wants = list(wants)
    for i, w in enumerate(wants):
        if isinstance(w, tuple):
            wants[i], out_place[i] = w
    diff = [i for i, w in enumerate(wants) if w]
    add_idx = [i for i in diff if i in addends]
    last_axis = len(grid) - 1

    def body(*refs):
        vals = [r[...] for r in refs[:n_in]]
        vals = [v.astype(F32) if v.dtype != F32 else v for v in vals]
        ct_vals = [r[...] for r in refs[n_in:n_in + n_ct]]
        ct_vals = [v.astype(F32) if v.dtype != F32 else v for v in ct_vals]
        add_refs = dict(zip(add_idx, refs[n_in + n_ct:n_in + n_ct + len(add_idx)]))
        out_refs = refs[n_in + n_ct + len(add_idx):]

        def f(*dargs):
            full = list(vals)
            for i, a in zip(diff, dargs):
                full[i] = a
            return tuple(fn(*full))

        _, vjp = jax.vjp(f, *[vals[i] for i in diff])
        grads = vjp(tuple(ct_vals))
        first = pl.program_id(last_axis) == 0
        for o, g, i in zip(out_refs, grads, diff):
            if wants[i] == 'tile':
                if i in add_refs:
                    g = g + add_refs[i][...]
                o[...] = g.astype(o.dtype)
            else:
                @pl.when(first)
                def _(o=o):
                    o[...] = jnp.zeros(o.shape, o.dtype)
                o[...] += g

    ins = list(prims) + list(cts) + [addends[i] for i in add_idx]
    places = [out_place.get(i, (prims[i][0].shape, prims[i][1], prims[i][2])) for i in diff]
    return pl.pallas_call(
        body, name=name, grid=grid,
        in_specs=[_spec(b, m) for _, b, m in ins],
        out_specs=[_spec(b, m) for _, b, m in places],
        out_shape=[jax.ShapeDtypeStruct(s, F32) for s, _, _ in places],
        compiler_params=_cparams(len(grid)),
    )(*[a for a, _, _ in ins])


def _matmul(name, a, b, mode, *, out_dtype=F32, lhs_fn=None, epi=None, extra=None, tm=512, tn=512, tk=512):
    if mode == 'tn':
        kdim, m = a.shape
    else:
        m, kdim = a.shape
    n = b.shape[0] if mode == 'nt' else b.shape[1]
    tm, tn, tk = min(tm, m), min(tn, n), min(tk, kdim)
    nk = kdim // tk
    grid = (m // tm, n // tn, nk)
    a_spec = _spec((tk, tm), lambda i, j, k: (k, i)) if mode == 'tn' else _spec((tm, tk), lambda i, j, k: (i, k))
    b_spec = _spec((tn, tk), lambda i, j, k: (j, k)) if mode == 'nt' else _spec((tk, tn), lambda i, j, k: (k, j))
    o_spec = _spec((tm, tn), lambda i, j, k: (i, j))
    dims = {'nn': (((1,), (0,)), ((), ())), 'nt': (((1,), (1,)), ((), ())), 'tn': (((0,), (0,)), ((), ()))}[mode]

    def body(*refs):
        if extra is None:
            a_ref, b_ref, o_ref, acc = refs
            x_ref = None
        else:
            a_ref, b_ref, x_ref, o_ref, acc = refs
        k = pl.program_id(2)

        @pl.when(k == 0)
        def _():
            acc[...] = jnp.zeros(acc.shape, F32)

        av = a_ref[...]
        if lhs_fn is not None:
            av = lhs_fn(av.astype(F32))
        acc[...] += lax.dot_general(av.astype(BF16), b_ref[...].astype(BF16), dims, preferred_element_type=F32)

        @pl.when(k == nk - 1)
        def _():
            r = acc[...]
            if epi is not None:
                r = epi(r, x_ref[...])
            o_ref[...] = r.astype(o_ref.dtype)

    ops, specs = [a, b], [a_spec, b_spec]
    if extra is not None:
        ops.append(extra)
        specs.append(o_spec)
    return pl.pallas_call(
        body, name=name, grid=grid, in_specs=specs, out_specs=o_spec,
        out_shape=jax.ShapeDtypeStruct((m, n), out_dtype),
        scratch_shapes=[pltpu.VMEM((tm, tn), F32)],
        compiler_params=_cparams(3),
    )(*ops)


def _relu2(x):
    r = jnp.maximum(x, 0.0)
    return r * r


def _mix_fn(x, halo, mu, w0, w1, w2, a0, a1, a2, g1, g2):
    prev_row = jnp.where(pl.program_id(0) == 0, 0.0, halo[7:8, :])
    xx = _shift_down(x, prev_row) - x
    xr, xw, xk, xv, xa, xg = (x + xx * mu[i:i + 1, :] for i in range(6))
    w_pre = w0 + _mm(jnp.tanh(_mm(xw, w1)), w2)
    log_decay = -jnp.exp(-_softplus(-w_pre) - 0.5)
    a = _sigmoid(a0 + _mm(_mm(xa, a1), a2))
    g = _mm(_sigmoid(_mm(xg, g1)), g2)
    return xr, xk, xv, log_decay, a, g


def _chunk_fn(r, lw, k, v, a, k_k, k_a, r_k):
    n = r.shape[0]
    lane = lax.broadcasted_iota(jnp.int32, (1, PAIR), 1)
    m0 = lane < HEAD
    masks = (m0, jnp.logical_not(m0))

    def head_sum(t):
        s0 = jnp.sum(jnp.where(m0, t, 0.0), axis=1, keepdims=True)
        s1 = jnp.sum(jnp.where(m0, 0.0, t), axis=1, keepdims=True)
        return jnp.where(m0, s0, s1)

    kk = k * k_k
    kk = kk / jnp.maximum(jnp.sqrt(head_sum(kk * kk)), 1e-12)
    k2 = k * (1.0 + (a - 1.0) * k_a)
    aa, bb = -kk, kk * a
    bonus = head_sum(r * k2 * r_k) * v

    ti = lax.broadcasted_iota(jnp.int32, (n, n), 0)
    tj = lax.broadcasted_iota(jnp.int32, (n, n), 1)
    incl, strict = ti >= tj, ti > tj
    eye_n = (ti == tj).astype(F32)
    cum = _hdot(incl.astype(F32), lw)
    p_in, p_ex, p_inv = jnp.exp(cum), jnp.exp(cum - lw), jnp.exp(-cum)
    at, rt, kt, bt = aa * p_ex, r * p_in, k2 * p_inv, bb * p_inv

    w_all = jnp.zeros_like(r)
    q_all = jnp.zeros_like(r)
    y_all = jnp.zeros_like(r)
    oi_all = jnp.zeros_like(r)
    for mh in masks:
        ath, rth = jnp.where(mh, at, 0.0), jnp.where(mh, rt, 0.0)
        a_ab = jnp.where(strict, _hdot_nt(ath, bt), 0.0)
        a_ak = jnp.where(strict, _hdot_nt(ath, kt), 0.0)
        a_rb = jnp.where(incl, _hdot_nt(rth, bt), 0.0)
        a_rk = jnp.where(incl, _hdot_nt(rth, kt), 0.0)
        inv, pw = eye_n + a_ab, a_ab
        for _ in range(int(math.log2(n)) - 1):
            pw = _hdot(pw, pw)
            inv = inv + _hdot(inv, pw)
        wh = _hdot(inv, ath)
        yh = _hdot(inv, _hdot(a_ak, v))
        w_all = w_all + wh
        q_all = q_all + rth + _hdot(a_rb, wh)
        y_all = jnp.where(mh, yh, y_all)
        oi_all = jnp.where(mh, _hdot(a_rb, yh) + _hdot(a_rk, v), oi_all)

    mm = jnp.zeros((PAIR, PAIR), F32)
    nn = jnp.zeros((PAIR, PAIR), F32)
    for mh in masks:
        bth, kth = jnp.where(mh, bt, 0.0), jnp.where(mh, kt, 0.0)
        mm = mm + _hdot_tn(bth, jnp.where(mh, w_all, 0.0))
        nn = nn + _hdot_tn(bth, jnp.where(mh, y_all, 0.0)) + _hdot_tn(kth, jnp.where(mh, v, 0.0))
    ei = lax.broadcasted_iota(jnp.int32, (PAIR, PAIR), 0)
    ej = lax.broadcasted_iota(jnp.int32, (PAIR, PAIR), 1)
    eye_p = (ei == ej).astype(F32)
    decay_all = eye_p * p_in[n - 1:n, :]
    return _hdot(decay_all, eye_p + mm)[None, None], _hdot(decay_all, nn)[None, None], q_all, oi_all, bonus


def _gate_fn(o, bonus, g, lnx_g, lnx_b):
    ci = lax.broadcasted_iota(jnp.int32, (D_MODEL, D_MODEL // HEAD), 0) // HEAD
    hi = lax.broadcasted_iota(jnp.int32, (D_MODEL, D_MODEL // HEAD), 1)
    ind = (ci == hi).astype(F32)

    def head_mean(t):
        return _hdot_nt(_hdot(t, ind), ind) * (1.0 / HEAD)

    om = head_mean(o)
    oc = o - om
    ov = head_mean(oc * oc)
    on = oc * lax.rsqrt(ov + GN_EPS) * lnx_g + lnx_b
    return ((on + bonus) * g,)


def _res_ln_fn(h, y, g, b):
    return (_layer_norm(DN_ALPHA * h + y, g, b),)


def _glu_ln_fn(h, z, g, b):
    mix = z[:, :D_MODEL] * _sigmoid(z[:, D_MODEL:])
    return (_layer_norm(DN_ALPHA * h + mix, g, b),)


def _s5_param_fn(a_re, a_im, log_dt, b_re, b_im):
    dt = jnp.exp(log_dt)
    lam_re = jnp.minimum(a_re, -1e-4)
    lam_im = a_im
    mag = jnp.exp(dt * lam_re)
    abar_re = mag * jnp.cos(dt * lam_im)
    abar_im = mag * jnp.sin(dt * lam_im)
    den = lam_re * lam_re + lam_im * lam_im
    nr, ni = abar_re - 1.0, abar_im
    coef_re = (nr * lam_re + ni * lam_im) / den
    coef_im = (ni * lam_re - nr * lam_im) / den
    return abar_re, abar_im, coef_re, coef_im


def _s5_bbar_fn(coef_re, coef_im, b_re, b_im):
    return coef_re * b_re - coef_im * b_im, coef_re * b_im + coef_im * b_re


def _s5_in_fn(u, b_re, b_im):
    return _hdot(u, b_re[0]), _hdot(u, b_im[0])


def _s5_out_fn(s_re, s_im, u, c_re, c_im, d):
    y = _hdot(s_re, c_re[0]) - _hdot(s_im, c_im[0]) + u * d
    return (_gelu(y),)


def _s5_step_fn(a_re, a_im, s_re, s_im, halo_re, halo_im):
    first = pl.program_id(1) == 0
    p_re = _shift_down(s_re, jnp.where(first, 0.0, halo_re[7:8, :]))
    p_im = _shift_down(s_im, jnp.where(first, 0.0, halo_im[7:8, :]))
    return a_re * p_re - a_im * p_im, a_re * p_im + a_im * p_re


def _rwkv_state_fwd(m_all, n_all, q, oi):
    n_chunk = m_all.shape[0]

    def body(m_ref, n_ref, q_ref, oi_ref, o_ref, zs_ref, z):
        @pl.when(pl.program_id(0) == 0)
        def _():
            z[...] = jnp.zeros(z.shape, F32)

        for j in range(N_PAIR):
            sl = slice(j * PAIR, (j + 1) * PAIR)
            zj = z[j]
            zs_ref[0, j] = zj
            o_ref[:, sl] = _hdot(q_ref[:, sl], zj) + oi_ref[:, sl]
            z[j] = _hdot(m_ref[0, j], zj) + n_ref[0, j]

    st = _spec((1, N_PAIR, PAIR, PAIR), lambda c: (c, 0, 0, 0))
    tok = _spec((CHUNK, D_MODEL), lambda c: (c, 0))
    return pl.pallas_call(
        body, name="rwkv_state_fwd", grid=(n_chunk,),
        in_specs=[st, st, tok, tok], out_specs=[tok, st],
        out_shape=[jax.ShapeDtypeStruct(q.shape, F32), jax.ShapeDtypeStruct(m_all.shape, F32)],
        scratch_shapes=[pltpu.VMEM((N_PAIR, PAIR, PAIR), F32)],
        compiler_params=_cparams(1),
    )(m_all, n_all, q, oi)


def _rwkv_state_bwd(m_all, q, zs, d_o):
    n_chunk = m_all.shape[0]

    def body(m_ref, q_ref, zs_ref, do_ref, dq_ref, dm_ref, dn_ref, g):
        @pl.when(pl.program_id(0) == 0)
        def _():
            g[...] = jnp.zeros(g.shape, F32)

        bi = lax.broadcasted_iota(jnp.int32, (PAIR, PAIR), 0) // HEAD
        bj = lax.broadcasted_iota(jnp.int32, (PAIR, PAIR), 1) // HEAD
        same_head = bi == bj
        for j in range(N_PAIR):
            sl = slice(j * PAIR, (j + 1) * PAIR)
            gj, zj, do = g[j], zs_ref[0, j], do_ref[:, sl]
            dn_ref[0, j] = gj
            dm_ref[0, j] = _hdot_nt(gj, zj)
            dq_ref[:, sl] = _hdot_nt(do, zj)
            g[j] = _hdot_tn(m_ref[0, j], gj) + jnp.where(same_head, _hdot_tn(q_ref[:, sl], do), 0.0)

    st = _spec((1, N_PAIR, PAIR, PAIR), lambda c: (n_chunk - 1 - c, 0, 0, 0))
    tok = _spec((CHUNK, D_MODEL), lambda c: (n_chunk - 1 - c, 0))
    return pl.pallas_call(
        body, name="rwkv_state_bwd", grid=(n_chunk,),
        in_specs=[st, tok, st, tok], out_specs=[tok, st, st],
        out_shape=[jax.ShapeDtypeStruct(q.shape, F32), jax.ShapeDtypeStruct(m_all.shape, F32),
                   jax.ShapeDtypeStruct(m_all.shape, F32)],
        scratch_shapes=[pltpu.VMEM((N_PAIR, PAIR, PAIR), F32)],
        compiler_params=_cparams(1),
    )(m_all, q, zs, d_o)


def _s5_scan(name, b_re, b_im, a_re, a_im, reverse):
    t_len, lanes = b_re.shape
    tt, lb, grp = 512, SSM_BLK_ST, 8
    tt = min(tt, t_len)
    n_t, n_grp = t_len // tt, tt // grp

    def cmul(xr, xi, yr, yi):
        return xr * yr - xi * yi, xr * yi + xi * yr

    def body(bre_ref, bim_ref, are_ref, aim_ref, sre_ref, sim_ref, carry):
        @pl.when(pl.program_id(1) == 0)
        def _():
            carry[...] = jnp.zeros(carry.shape, F32)

        a1 = (are_ref[...], aim_ref[...])
        a2 = cmul(*a1, *a1)
        a4 = cmul(*a2, *a2)
        a8 = cmul(*a4, *a4)
        row = lax.broadcasted_iota(jnp.int32, (grp, lb), 0)
        expo = (grp - row) if reverse else (row + 1)
        pw = (jnp.ones((grp, lb), F32), jnp.zeros((grp, lb), F32))
        for bit, ap in ((1, a1), (2, a2), (4, a4), (8, a8)):
            nxt = cmul(*pw, *ap)
            sel = (expo & bit) != 0
            pw = (jnp.where(sel, nxt[0], pw[0]), jnp.where(sel, nxt[1], pw[1]))

        def group(i, c):
            gi = (n_grp - 1 - i) if reverse else i
            rows = pl.ds(pl.multiple_of(gi * grp, grp), grp)
            xr, xi = bre_ref[rows, :], bim_ref[rows, :]
            for d, ad in ((1, a1), (2, a2), (4, a4)):
                if reverse:
                    keep = row < grp - d
                    sr, si = pltpu.roll(xr, grp - d, 0), pltpu.roll(xi, grp - d, 0)
                else:
                    keep = row >= d
                    sr, si = pltpu.roll(xr, d, 0), pltpu.roll(xi, d, 0)
                pr, pi = cmul(*ad, jnp.where(keep, sr, 0.0), jnp.where(keep, si, 0.0))
                xr, xi = xr + pr, xi + pi
            cr, ci = cmul(*pw, c[0], c[1])
            xr, xi = xr + cr, xi + ci
            sre_ref[rows, :] = xr
            sim_ref[rows, :] = xi
            edge = slice(0, 1) if reverse else slice(grp - 1, grp)
            return xr[edge, :], xi[edge, :]

        c = lax.fori_loop(0, n_grp, group, (carry[0:1, :], carry[1:2, :]))
        carry[0:1, :] = c[0]
        carry[1:2, :] = c[1]

    tmap = (lambda l, t: (n_t - 1 - t, l)) if reverse else (lambda l, t: (t, l))
    blk = _spec((tt, lb), tmap)
    par = _spec((1, lb), lambda l, t: (0, l))
    return pl.pallas_call(
        body, name=name, grid=(lanes // lb, n_t),
        in_specs=[blk, blk, par, par], out_specs=[blk, blk],
        out_shape=[jax.ShapeDtypeStruct(b_re.shape, F32)] * 2,
        scratch_shapes=[pltpu.VMEM((2, lb), F32)],
        compiler_params=_cparams(2),
    )(b_re, b_im, a_re, a_im)


def _loss_call(h, target, tm):
    n_tok, d = h.shape

    def body(h_ref, t_ref, acc_ref, dh_ref):
        @pl.when(pl.program_id(0) == 0)
        def _():
            acc_ref[...] = jnp.zeros(acc_ref.shape, F32)

        e = h_ref[...] - t_ref[...]
        dh_ref[...] = e * (1.0 / d)
        acc_ref[...] += jnp.sum(jnp.sum(e * e, axis=1, keepdims=True), axis=0, keepdims=True)

    tok = _spec((tm, d), lambda i: (i, 0))
    return pl.pallas_call(
        body, name="loss", grid=(n_tok // tm,),
        in_specs=[tok, tok], out_specs=[_spec((8, 128), lambda i: (0, 0)), tok],
        out_shape=[jax.ShapeDtypeStruct((8, 128), F32), jax.ShapeDtypeStruct(h.shape, F32)],
        compiler_params=_cparams(1),
    )(h, target)


def _block_diag(t):
    nb, ng, rows, cols = t.shape
    eye = jnp.eye(ng, dtype=t.dtype)
    return jnp.einsum('bgrc,gh->bgrhc', t, eye).reshape(nb, ng * rows, ng * cols)


def _block_diag_t(t, rows, cols):
    nb, ng = t.shape[0], t.shape[1] // rows
    t5 = t.reshape(nb, ng, rows, ng, cols)
    return jnp.einsum('bgrhc,gh->bgrc', t5, jnp.eye(ng, dtype=t.dtype))


def _mlp_fwd(tag, h, w1b, w2b):
    pre = _matmul(f"mlp{tag}_up", h, w1b, 'nn')
    return pre, _matmul(f"mlp{tag}_down", pre, w2b, 'nn', lhs_fn=_relu2)


def _mlp_bwd(tag, h, pre, dy, dh_res, w1b, w2b):
    dpre = _matmul(f"mlp{tag}_down_dx", dy, w2b, 'nt', epi=lambda acc, p: acc * (2.0 * jnp.maximum(p, 0.0)), extra=pre)
    dw2 = _matmul(f"mlp{tag}_down_dw", pre, dy, 'tn', lhs_fn=_relu2)
    dw1 = _matmul(f"mlp{tag}_up_dw", h, dpre, 'tn')
    dh = _matmul(f"mlp{tag}_up_dx", dpre, w1b, 'nt', epi=lambda acc, e: acc + e, extra=dh_res)
    return dh, dw1, dw2


def _local_step(x, target, w):
    n_tok = x.shape[0]
    tm = min(TOKEN_TILE, n_tok)
    n_tile, n_chunk = n_tok // tm, n_tok // CHUNK
    row = lambda v: v.reshape(1, -1)
    big = {n: w[n][0].astype(BF16) for n in ('rw_wr', 'rw_wk', 'rw_wv', 'rw_wo', 's5_w_glu')}
    w1b, w2b = w['mlp_w1'].astype(BF16), w['mlp_w2'].astype(BF16)
    ln_g, ln_b = w['ln_g'], w['ln_b']
    grads = {}

    halo_spec = (8, D_MODEL), lambda i: (jnp.maximum(i * (tm // 8) - 1, 0), 0)
    mix_p = [_tile(x, tm), (x,) + halo_spec, _full(w['rw_mu'][0]), _full(row(w['rw_w0'])), _full(w['rw_w1'][0]),
             _full(w['rw_w2'][0]), _full(row(w['rw_a0'])), _full(w['rw_a1'][0]), _full(w['rw_a2'][0]),
             _full(w['rw_g1'][0]), _full(w['rw_g2'][0])]
    tok_out = lambda dt: ((n_tok, D_MODEL), dt, (tm, D_MODEL), lambda i: (i, 0))
    xr, xk, xv, lw, a, g = _fwd_call("rwkv_mix", _mix_fn, (n_tile,), mix_p,
                                     [tok_out(BF16), tok_out(BF16), tok_out(BF16), tok_out(F32), tok_out(F32), tok_out(F32)])
    r = _matmul("rwkv_r", xr, big['rw_wr'], 'nn')
    k = _matmul("rwkv_k", xk, big['rw_wk'], 'nn')
    v = _matmul("rwkv_v", xv, big['rw_wv'], 'nn')

    slab = lambda arr: (arr, (CHUNK, PAIR), lambda j, c: (c, j))
    pslab = lambda arr: (arr, (1, PAIR), lambda j, c: (0, j))
    chunk_p = [slab(r), slab(lw), slab(k), slab(v), slab(a), pslab(row(w['rw_k_k'])), pslab(row(w['rw_k_a'])),
               pslab(w['rw_r_k'].reshape(1, -1))]
    st_shape = (n_chunk, N_PAIR, PAIR, PAIR)
    st_out = (st_shape, F32, (1, 1, PAIR, PAIR), lambda j, c: (c, j, 0, 0))
    sl_out = ((n_tok, D_MODEL), F32, (CHUNK, PAIR), lambda j, c: (c, j))
    m_all, n_all, q, oi, bonus = _fwd_call("rwkv_chunk", _chunk_fn, (N_PAIR, n_chunk), chunk_p,
                                           [st_out, st_out, sl_out, sl_out, sl_out])
    o, zs = _rwkv_state_fwd(m_all, n_all, q, oi)

    gate_p = [_tile(o, tm), _tile(bonus, tm), _tile(g, tm), _full(row(w['rw_lnx_g'])), _full(row(w['rw_lnx_b']))]
    (og,) = _fwd_call("rwkv_gate", _gate_fn, (n_tile,), gate_p, [tok_out(BF16)])
    y0 = _matmul("rwkv_o", og, big['rw_wo'], 'nn')
    ln0_p = [_tile(x, tm), _tile(y0, tm), _full(ln_g[0:1]), _full(ln_b[0:1])]
    (h1,) = _fwd_call("ln0", _res_ln_fn, (n_tile,), ln0_p, [tok_out(F32)])
    pre0, y1 = _mlp_fwd(0, h1, w1b[0], w2b[0])
    ln1_p = [_tile(h1, tm), _tile(y1, tm), _full(ln_g[1:2]), _full(ln_b[1:2])]
    (h2,) = _fwd_call("ln1", _res_ln_fn, (n_tile,), ln1_p, [tok_out(F32)])

    gp = (SSM_GROUPS, SSM_STATE)
    par_p = [_full(w['s5_a_re'][0]), _full(w['s5_a_im'][0]), _full(w['s5_log_dt'].reshape(SSM_GROUPS, 1))]
    s5_par_fn = lambda ar, ai, ld: _s5_param_fn(ar, ai, ld, None, None)
    gp_out = (gp, F32, gp, lambda i: (0, 0))
    abar_re, abar_im, coef_re, coef_im = _fwd_call("s5_param", s5_par_fn, (1,), par_p, [gp_out] * 4)
    b_flat = [w[n][0].reshape(SSM_LANES, SSM_GROUP) for n in ('s5_b_re', 's5_b_im')]
    bbar_p = [_full(coef_re.reshape(SSM_LANES, 1)), _full(coef_im.reshape(SSM_LANES, 1)), _full(b_flat[0]), _full(b_flat[1])]
    bb_out = ((SSM_LANES, SSM_GROUP), F32, (SSM_LANES, SSM_GROUP), lambda i: (0, 0))
    bbar_re, bbar_im = _fwd_call("s5_bbar", _s5_bbar_fn, (1,), bbar_p, [bb_out] * 2)
    to_in = lambda t: _block_diag(t.reshape(N_SSM_BLK, 8, SSM_STATE, SSM_GROUP).transpose(0, 1, 3, 2))
    to_out = lambda t: _block_diag(t.reshape(N_SSM_BLK, 8, SSM_GROUP, SSM_STATE).transpose(0, 1, 3, 2))
    bblk_re, bblk_im = to_in(bbar_re), to_in(bbar_im)
    cblk_re, cblk_im = to_out(w['s5_c_re'][0]), to_out(w['s5_c_im'][0])
    a_row_re, a_row_im = abar_re.reshape(1, SSM_LANES), abar_im.reshape(1, SSM_LANES)

    u_slab = lambda arr: (arr, (tm, SSM_BLK_IN), lambda l, i: (i, l))
    s_slab = lambda arr: (arr, (tm, SSM_BLK_ST), lambda l, i: (i, l))
    blk3 = lambda arr: (arr, (1,) + arr.shape[1:], lambda l, i: (l, 0, 0))
    in_p = [u_slab(h2), blk3(bblk_re), blk3(bblk_im)]
    st_tok = ((n_tok, SSM_LANES), F32, (tm, SSM_BLK_ST), lambda l, i: (i, l))
    bu_re, bu_im = _fwd_call("s5_in", _s5_in_fn, (N_SSM_BLK, n_tile), in_p, [st_tok, st_tok])
    s_re, s_im = _s5_scan("s5_scan_fwd", bu_re, bu_im, a_row_re, a_row_im, False)
    d_row = row(w['s5_d'])
    out_p = [s_slab(s_re), s_slab(s_im), u_slab(h2), blk3(cblk_re), blk3(cblk_im),
             (d_row, (1, SSM_BLK_IN), lambda l, i: (0, l))]
    u_tok = lambda dt: ((n_tok, D_MODEL), dt, (tm, SSM_BLK_IN), lambda l, i: (i, l))
    (yg,) = _fwd_call("s5_out", _s5_out_fn, (N_SSM_BLK, n_tile), out_p, [u_tok(BF16)])
    z = _matmul("s5_glu", yg, big['s5_w_glu'], 'nn')
    ln2_p = [_tile(h2, tm), _tile(z, tm), _full(ln_g[2:3]), _full(ln_b[2:3])]
    (h3,) = _fwd_call("ln2", _glu_ln_fn, (n_tile,), ln2_p, [tok_out(F32)])
    pre1, y3 = _mlp_fwd(1, h3, w1b[1], w2b[1])
    ln3_p = [_tile(h3, tm), _tile(y3, tm), _full(ln_g[3:4]), _full(ln_b[3:4])]
    (h4,) = _fwd_call("ln3", _res_ln_fn, (n_tile,), ln3_p, [tok_out(F32)])

    loss_acc, dh4 = _loss_call(h4, target, tm)

    ln_w = ['tile', 'tile', 'acc', 'acc']
    dh3_res, dy3, dg3, db3 = _bwd_call("ln3_bwd", _res_ln_fn, (n_tile,), ln3_p, [_tile(dh4, tm)], ln_w)
    dh3, dw1_1, dw2_1 = _mlp_bwd(1, h3, pre1, dy3, dh3_res, w1b[1], w2b[1])
    dh2_res, dz, dg2, db2 = _bwd_call("ln2_bwd", _glu_ln_fn, (n_tile,), ln2_p, [_tile(dh3, tm)], ln_w)
    dyg = _matmul("s5_glu_dx", dz, big['s5_w_glu'], 'nt')
    grads['s5_w_glu'] = _matmul("s5_glu_dw", yg, dz, 'tn')[None]
    ds_re, ds_im, du_out, dcb_re, dcb_im, dd = _bwd_call(
        "s5_out_bwd", _s5_out_fn, (N_SSM_BLK, n_tile), out_p, [u_slab(dyg)],
        ['tile', 'tile', 'tile', 'acc', 'acc', 'acc'], addends={2: u_slab(dh2_res)})
    gs_re, gs_im = _s5_scan("s5_scan_bwd", ds_re, ds_im, a_row_re, -a_row_im, True)
    s_halo = lambda arr: (arr, (8, SSM_BLK_ST), lambda l, i: (jnp.maximum(i * (tm // 8) - 1, 0), l))
    a_par = lambda arr: (arr, (1, SSM_BLK_ST), lambda l, i: (0, l))
    step_p = [a_par(a_row_re), a_par(a_row_im), s_slab(s_re), s_slab(s_im), s_halo(s_re), s_halo(s_im)]
    dabar_re, dabar_im = _bwd_call("s5_abar_bwd", _s5_step_fn, (N_SSM_BLK, n_tile), step_p,
                                   [s_slab(gs_re), s_slab(gs_im)], ['acc', 'acc', None, None, None, None])
    dh2, dbb_re, dbb_im = _bwd_call("s5_in_bwd", _s5_in_fn, (N_SSM_BLK, n_tile), in_p, [s_slab(gs_re), s_slab(gs_im)],
                                    ['tile', 'acc', 'acc'], addends={0: u_slab(du_out)})
    from_in = lambda t: _block_diag_t(t, SSM_GROUP, SSM_STATE).transpose(0, 1, 3, 2).reshape(SSM_LANES, SSM_GROUP)
    from_out = lambda t: _block_diag_t(t, SSM_STATE, SSM_GROUP).transpose(0, 1, 3, 2).reshape(1, SSM_GROUPS, SSM_GROUP, SSM_STATE)
    grads['s5_c_re'], grads['s5_c_im'] = from_out(dcb_re), from_out(dcb_im)
    grads['s5_d'] = dd
    dcoef_re, dcoef_im, db_re, db_im = _bwd_call(
        "s5_bbar_bwd", _s5_bbar_fn, (1,), bbar_p, [_full(from_in(dbb_re)), _full(from_in(dbb_im))], ['acc'] * 4)
    grads['s5_b_re'] = db_re.reshape(w['s5_b_re'].shape)
    grads['s5_b_im'] = db_im.reshape(w['s5_b_im'].shape)
    par_ct = [_full(dabar_re.reshape(gp)), _full(dabar_im.reshape(gp)), _full(dcoef_re.reshape(gp)), _full(dcoef_im.reshape(gp))]
    da_re, da_im, dlog_dt = _bwd_call("s5_param_bwd", s5_par_fn, (1,), par_p, par_ct, ['acc'] * 3)
    grads['s5_a_re'], grads['s5_a_im'], grads['s5_log_dt'] = da_re[None], da_im[None], dlog_dt.reshape(1, SSM_GROUPS)

    dh1_res, dy1, dg1, db1 = _bwd_call("ln1_bwd", _res_ln_fn, (n_tile,), ln1_p, [_tile(dh2, tm)], ln_w)
    dh1, dw1_0, dw2_0 = _mlp_bwd(0, h1, pre0, dy1, dh1_res, w1b[0], w2b[0])
    grads['mlp_w1'], grads['mlp_w2'] = jnp.stack([dw1_0, dw1_1]), jnp.stack([dw2_0, dw2_1])
    dx_res, dy0, dg0, db0 = _bwd_call("ln0_bwd", _res_ln_fn, (n_tile,), ln0_p, [_tile(dh1, tm)], ln_w)
    grads['ln_g'] = jnp.concatenate([dg0, dg1, dg2, dg3])
    grads['ln_b'] = jnp.concatenate([db0, db1, db2, db3])
    dog = _matmul("rwkv_o_dx", dy0, big['rw_wo'], 'nt')
    grads['rw_wo'] = _matmul("rwkv_o_dw", og, dy0, 'tn')[None]
    d_o, dbonus, dgate, grads['rw_lnx_g'], grads['rw_lnx_b'] = _bwd_call(
        "rwkv_gate_bwd", _gate_fn, (n_tile,), gate_p, [_tile(dog, tm)], ['tile', 'tile', 'tile', 'acc', 'acc'])
    dq, dm_all, dn_all = _rwkv_state_bwd(m_all, q, zs, d_o)
    st_ct = lambda arr: (arr, (1, 1, PAIR, PAIR), lambda j, c: (c, j, 0, 0))
    dr, dlw, dk, dv, da, grads['rw_k_k'], grads['rw_k_a'], dr_k = _bwd_call(
        "rwkv_chunk_bwd", _chunk_fn, (N_PAIR, n_chunk), chunk_p,
        [st_ct(dm_all), st_ct(dn_all), slab(dq), slab(d_o), slab(dbonus)],
        ['tile'] * 5 + ['acc'] * 3)
    grads['rw_r_k'] = dr_k.reshape(w['rw_r_k'].shape)
    dxr = _matmul("rwkv_r_dx", dr, big['rw_wr'], 'nt')
    dxk = _matmul("rwkv_k_dx", dk, big['rw_wk'], 'nt')
    dxv = _matmul("rwkv_v_dx", dv, big['rw_wv'], 'nt')
    grads['rw_wr'] = _matmul("rwkv_r_dw", xr, dr, 'tn')[None]
    grads['rw_wk'] = _matmul("rwkv_k_dw", xk, dk, 'tn')[None]
    grads['rw_wv'] = _matmul("rwkv_v_dw", xv, dv, 'tn')[None]
    mix_ct = [_tile(t, tm) for t in (dxr, dxk, dxv, dlw, da, dgate)]
    halo_grad = ('tile', ((n_tile * 8, D_MODEL), (8, D_MODEL), lambda i: (i, 0)))
    res = _bwd_call("rwkv_mix_bwd", _mix_fn, (n_tile,), mix_p, mix_ct, ['tile', halo_grad] + ['acc'] * 9,
                    addends={0: _tile(dx_res, tm)})
    dx, dhalo = res[0], res[1]
    for i in range(1, n_tile):
        last = i * tm - 1
        dx = lax.dynamic_update_slice(dx, dx[last:last + 1] + dhalo[8 * i + 7:8 * i + 8], (last, 0))
    (grads['rw_mu'], grads['rw_w0'], grads['rw_w1'], grads['rw_w2'], grads['rw_a0'], grads['rw_a1'], grads['rw_a2'],
     grads['rw_g1'], grads['rw_g2']) = [t[None] if t.shape[0] != 1 else t for t in res[2:]]
    return loss_acc[0, 0], dx, grads


PACK_ROWS = 128


def _pack(arrs):
    flat = jnp.concatenate([a.reshape(-1).astype(F32) for a in arrs])
    per = PACK_ROWS * PACK_W
    total = -(-flat.shape[0] // per) * per
    return jnp.pad(flat, (0, total - flat.shape[0])).reshape(-1, PACK_W)


def _unpack(packed, shapes, lead=()):
    flat = packed.reshape(lead + (-1,))
    out, off = [], 0
    for s in shapes:
        n = math.prod(s)
        out.append(flat[..., off:off + n].reshape(lead + tuple(s)))
        off += n
    return out


def _mesh_pos():
    return lax.axis_index("x"), lax.axis_index("y"), lax.axis_index("c")


def _flip(v, f):
    return 1 - v if f else v


def _all_gather(name, local):
    rows, width = local.shape

    def body(x_ref, out_ref, send_sems, recv_sems, local_sem):
        x, y, c = _mesh_pos()
        me, sibling = (x, y, c), (x, y, 1 - c)
        chips = [(1 - x, y), (x, 1 - y), (1 - x, 1 - y)]

        def slot(px, py, pc):
            return out_ref.at[4 * px + 2 * py + pc]

        def copy(k, block, to, src=None):
            return pltpu.make_async_remote_copy(
                src_ref=slot(*block) if src is None else src, dst_ref=slot(*block),
                send_sem=send_sems.at[k], recv_sem=recv_sems.at[k],
                device_id=to, device_id_type=pl.DeviceIdType.MESH)

        mine = pltpu.make_async_copy(x_ref, slot(*me), local_sem)
        mine.start()
        first = [copy(0, me, sibling, src=x_ref)]
        first += [copy(1 + j, me, (*chip, c), src=x_ref) for j, chip in enumerate(chips)]
        for cp in first:
            cp.start()
        passed = [copy(4 + j, (*chip, c), sibling) for j, chip in enumerate(chips)]
        for j, chip in enumerate(chips):
            copy(1 + j, (*chip, c), me).wait_recv()
            passed[j].start()
        copy(0, sibling, me).wait_recv()
        for j, chip in enumerate(chips):
            copy(4 + j, (*chip, 1 - c), me).wait_recv()
        for cp in first + passed:
            cp.wait_send()
        mine.wait()

    return pl.pallas_call(
        body, name=name,
        out_shape=jax.ShapeDtypeStruct((N_DEV, rows, width), local.dtype),
        in_specs=[pl.BlockSpec(memory_space=pl.ANY)],
        out_specs=pl.BlockSpec(memory_space=pl.ANY),
        scratch_shapes=[pltpu.SemaphoreType.DMA((7,)), pltpu.SemaphoreType.DMA((7,)), pltpu.SemaphoreType.DMA],
    )(local)


def _all_to_all(name, blocks):
    shape = blocks.shape

    def body(x_ref, out_ref, send_sems, recv_sems, local_sem):
        x, y, c = _mesh_pos()
        my_slot = 4 * x + 2 * y + c
        mine = pltpu.make_async_copy(x_ref.at[my_slot], out_ref.at[my_slot], local_sem)
        mine.start()
        copies = []
        for k in range(1, N_DEV):
            peer = (_flip(x, k & 4), _flip(y, k & 2), _flip(c, k & 1))
            peer_slot = 4 * peer[0] + 2 * peer[1] + peer[2]
            cp = pltpu.make_async_remote_copy(
                src_ref=x_ref.at[peer_slot], dst_ref=out_ref.at[my_slot],
                send_sem=send_sems.at[k - 1], recv_sem=recv_sems.at[k - 1],
                device_id=peer, device_id_type=pl.DeviceIdType.MESH)
            cp.start()
            copies.append((cp, pltpu.make_async_remote_copy(
                src_ref=x_ref.at[my_slot], dst_ref=out_ref.at[peer_slot],
                send_sem=send_sems.at[k - 1], recv_sem=recv_sems.at[k - 1],
                device_id=peer, device_id_type=pl.DeviceIdType.MESH)))
        for sent, landed in copies:
            landed.wait_recv()
            sent.wait_send()
        mine.wait()

    return pl.pallas_call(
        body, name=name,
        out_shape=jax.ShapeDtypeStruct(shape, blocks.dtype),
        in_specs=[pl.BlockSpec(memory_space=pl.ANY)],
        out_specs=pl.BlockSpec(memory_space=pl.ANY),
        scratch_shapes=[pltpu.SemaphoreType.DMA((7,)), pltpu.SemaphoreType.DMA((7,)), pltpu.SemaphoreType.DMA],
    )(blocks)


def _adam_call(name, slots, w, m, v):
    rows, width = w.shape
    c1 = 1.0 / (1.0 - ADAM_B1 ** ADAM_STEP)
    c2 = 1.0 / (1.0 - ADAM_B2 ** ADAM_STEP)

    def body(s_ref, w_ref, m_ref, v_ref, g_ref, d_ref, nm_ref, nv_ref):
        g = s_ref[0]
        for i in range(1, N_DEV):
            g = g + s_ref[i]
        m_new = ADAM_B1 * m_ref[...] + (1.0 - ADAM_B1) * g
        v_new = ADAM_B2 * v_ref[...] + (1.0 - ADAM_B2) * (g * g)
        g_ref[...] = g
        nm_ref[...] = m_new
        nv_ref[...] = v_new
        d_ref[...] = -ADAM_LR * ((m_new * c1) / (jnp.sqrt(v_new * c2) + ADAM_EPS) + ADAM_WD * w_ref[...])

    blk = _spec((PACK_ROWS, width), lambda i: (i, 0))
    return pl.pallas_call(
        body, name=name, grid=(rows // PACK_ROWS,),
        in_specs=[_spec((N_DEV, PACK_ROWS, width), lambda i: (0, i, 0)), blk, blk, blk],
        out_specs=[blk] * 4, out_shape=[jax.ShapeDtypeStruct(w.shape, F32)] * 4,
        compiler_params=_cparams(1),
    )(slots, w, m, v)


def kernel(x, ln_g, ln_b, rw_mu, rw_w0, rw_w1, rw_w2, rw_a0, rw_a1, rw_a2, rw_g1, rw_g2, rw_k_k, rw_k_a, rw_r_k, rw_wr, rw_wk, rw_wv, rw_wo, rw_lnx_g, rw_lnx_b, s5_a_re, s5_a_im, s5_log_dt, s5_b_re, s5_b_im, s5_c_re, s5_c_im, s5_d, s5_w_glu, mlp_w1, mlp_w2, loss_target, m_ln_g, m_ln_b, m_rw_mu, m_rw_w0, m_rw_w1, m_rw_w2, m_rw_a0, m_rw_a1, m_rw_a2, m_rw_g1, m_rw_g2, m_rw_k_k, m_rw_k_a, m_rw_r_k, m_rw_wr, m_rw_wk, m_rw_wv, m_rw_wo, m_rw_lnx_g, m_rw_lnx_b, m_s5_a_re, m_s5_a_im, m_s5_log_dt, m_s5_b_re, m_s5_b_im, m_s5_c_re, m_s5_c_im, m_s5_d, m_s5_w_glu, m_mlp_w1, m_mlp_w2, v_ln_g, v_ln_b, v_rw_mu, v_rw_w0, v_rw_w1, v_rw_w2, v_rw_a0, v_rw_a1, v_rw_a2, v_rw_g1, v_rw_g2, v_rw_k_k, v_rw_k_a, v_rw_r_k, v_rw_wr, v_rw_wk, v_rw_wv, v_rw_wo, v_rw_lnx_g, v_rw_lnx_b, v_s5_a_re, v_s5_a_im, v_s5_log_dt, v_s5_b_re, v_s5_b_im, v_s5_c_re, v_s5_c_im, v_s5_d, v_s5_w_glu, v_mlp_w1, v_mlp_w2):
    given = dict(locals())
    local_w = {n: given[n] for n in WEIGHTS}
    local_m = {n: given["m_" + n] for n in WEIGHTS}
    local_v = {n: given["v_" + n] for n in WEIGHTS}
    sharded = [n for n in WEIGHTS if n in SHARDED]
    sh_shapes = [local_w[n].shape for n in sharded]
    rep_shapes = [local_w[n].shape for n in REPLICATED]

    gathered = _all_gather("weights_all_gather", _pack([local_w[n] for n in sharded]))
    full = dict(local_w)
    for n, blk in zip(sharded, _unpack(gathered, sh_shapes, lead=(N_DEV,))):
        full[n] = jnp.concatenate([blk[i] for i in range(N_DEV)], axis=SHARDED[n])

    loss_sq, dx, grads = _local_step(x[0], loss_target[0], full)
    loss = (0.5 / D_MODEL) * lax.psum(loss_sq, MESH_AXES)

    parts = [jnp.stack(jnp.split(grads[n], N_DEV, axis=SHARDED[n])) for n in sharded]
    send = jnp.stack([_pack([p[i] for p in parts]) for i in range(N_DEV)])
    recv = _all_to_all("grads_all_to_all", send)
    sh_out = _adam_call("adamw_sharded", recv, _pack([local_w[n] for n in sharded]),
                        _pack([local_m[n] for n in sharded]), _pack([local_v[n] for n in sharded]))
    rep_all = _all_gather("grads_all_gather", _pack([grads[n] for n in REPLICATED]))
    rep_out = _adam_call("adamw_replicated", rep_all, _pack([local_w[n] for n in REPLICATED]),
                         _pack([local_m[n] for n in REPLICATED]), _pack([local_v[n] for n in REPLICATED]))

    results = []
    for sh_packed, rep_packed in zip(sh_out, rep_out):
        by_name = dict(zip(sharded, _unpack(sh_packed, sh_shapes)))
        by_name.update(zip(REPLICATED, _unpack(rep_packed, rep_shapes)))
        results += [by_name[n] for n in WEIGHTS]
    return (loss, dx[None], *results)
```

```python
import functools
import math

import jax
import jax.numpy as jnp
from jax import lax
from jax.experimental import pallas as pl
from jax.experimental.pallas import tpu as pltpu

F32 = jnp.float32
BF16 = jnp.bfloat16
HI = lax.Precision.HIGHEST

D_MODEL = 1024
HEAD = 64
PAIR = 2 * HEAD
N_PAIR = D_MODEL // PAIR
CHUNK = 64
GN_EPS = 64e-5
LN_EPS = 1e-5
SSM_GROUP = 16
SSM_STATE = 64
SSM_GROUPS = D_MODEL // SSM_GROUP
SSM_LANES = SSM_GROUPS * SSM_STATE
SSM_BLK_IN = 128
SSM_BLK_ST = 512
N_SSM_BLK = D_MODEL // SSM_BLK_IN
DEPTH = 2
DN_ALPHA = (2.0 * DEPTH) ** 0.25
ADAM_LR, ADAM_B1, ADAM_B2, ADAM_EPS, ADAM_WD, ADAM_STEP = 0.001, 0.9, 0.999, 1e-08, 0.01, 10
N_DEV = 8
MESH_AXES = ("x", "y", "c")
VMEM_LIMIT = 56 * 1024 * 1024
TOKEN_TILE = 256
PACK_W = 1024

SHARDED = {
    "rw_mu": 2, "rw_w1": 1, "rw_w2": 2, "rw_a1": 1, "rw_a2": 2, "rw_g1": 1, "rw_g2": 2,
    "rw_wr": 1, "rw_wk": 1, "rw_wv": 1, "rw_wo": 1, "s5_d": 1, "s5_w_glu": 2, "mlp_w1": 2, "mlp_w2": 1,
}
WEIGHTS = ['ln_g', 'ln_b', 'rw_mu', 'rw_w0', 'rw_w1', 'rw_w2', 'rw_a0', 'rw_a1', 'rw_a2', 'rw_g1', 'rw_g2', 'rw_k_k',
           'rw_k_a', 'rw_r_k', 'rw_wr', 'rw_wk', 'rw_wv', 'rw_wo', 'rw_lnx_g', 'rw_lnx_b', 's5_a_re', 's5_a_im',
           's5_log_dt', 's5_b_re', 's5_b_im', 's5_c_re', 's5_c_im', 's5_d', 's5_w_glu', 'mlp_w1', 'mlp_w2']
REPLICATED = [n for n in WEIGHTS if n not in SHARDED]
BIG = ['rw_wr', 'rw_wk', 'rw_wv', 'rw_wo', 's5_w_glu', 'mlp_w1', 'mlp_w2']


def _cparams(n_grid):
    return pltpu.CompilerParams(dimension_semantics=("arbitrary",) * n_grid, vmem_limit_bytes=VMEM_LIMIT)


@jax.custom_vjp
def _mm(x, w):
    return jnp.dot(x.astype(BF16), w.astype(BF16), preferred_element_type=F32)


def _mm_fwd(x, w):
    return _mm(x, w), (x, w)


def _mm_bwd(res, dy):
    x, w = res
    dyb = dy.astype(BF16)
    dx = lax.dot_general(dyb, w.astype(BF16), (((1,), (1,)), ((), ())), preferred_element_type=F32)
    dw = lax.dot_general(x.astype(BF16), dyb, (((0,), (0,)), ((), ())), preferred_element_type=F32)
    return dx, dw


_mm.defvjp(_mm_fwd, _mm_bwd)


def _hdot(a, b):
    return jnp.dot(a, b, precision=HI, preferred_element_type=F32)


def _hdot_nt(a, b):
    return lax.dot_general(a, b, (((1,), (1,)), ((), ())), precision=HI, preferred_element_type=F32)


def _hdot_tn(a, b):
    return lax.dot_general(a, b, (((0,), (0,)), ((), ())), precision=HI, preferred_element_type=F32)


_DOT_DIMS = {'nn': (((1,), (0,)), ((), ())), 'nt': (((1,), (1,)), ((), ())), 'tn': (((0,), (0,)), ((), ()))}
_BATCH_DOT_DIMS = {'nn': (((2,), (1,)), ((0,), (0,))), 'nt': (((2,), (2,)), ((0,), (0,))), 'tn': (((1,), (1,)), ((0,), (0,)))}
CHUNK_PAIRS = 8
CHUNK_PASSES = 1


def _split_bf16(a):
    hi = a.astype(BF16)
    return hi, (a - hi.astype(F32)).astype(BF16)


def _dot_passes(a, b, mode):
    dims = (_DOT_DIMS if a.ndim == 2 else _BATCH_DOT_DIMS)[mode]
    dot = lambda p, q: lax.dot_general(p, q, dims, preferred_element_type=F32)
    if CHUNK_PASSES == 1:
        return dot(a.astype(BF16), b.astype(BF16))
    (ah, al), (bh, bl) = _split_bf16(a), _split_bf16(b)
    return dot(ah, bh) + (dot(ah, bl) + dot(al, bh))


@functools.partial(jax.custom_vjp, nondiff_argnums=(2,))
def _pdot(a, b, mode):
    return _dot_passes(a, b, mode)


def _pdot_fwd(a, b, mode):
    return _dot_passes(a, b, mode), (a, b)


def _pdot_bwd(mode, res, dy):
    a, b = res
    if mode == 'nn':
        return _dot_passes(dy, b, 'nt'), _dot_passes(a, dy, 'tn')
    if mode == 'nt':
        return _dot_passes(dy, b, 'nn'), _dot_passes(dy, a, 'tn')
    return _dot_passes(b, dy, 'nt'), _dot_passes(a, dy, 'nn')


_pdot.defvjp(_pdot_fwd, _pdot_bwd)


def _tri_sum(x, upper):
    nb, n = x.shape[0], x.shape[1]
    ti = lax.broadcasted_iota(jnp.int32, (nb, n, n), 1)
    tj = lax.broadcasted_iota(jnp.int32, (nb, n, n), 2)
    tri = ((ti <= tj) if upper else (ti >= tj)).astype(BF16)
    hi = x.astype(BF16)
    rest = x - hi.astype(F32)
    mid = rest.astype(BF16)
    lo = (rest - mid.astype(F32)).astype(BF16)
    dot = lambda q: lax.dot_general(tri, q, _BATCH_DOT_DIMS['nn'], preferred_element_type=F32)
    return dot(hi) + (dot(mid) + dot(lo))


@jax.custom_vjp
def _cumsum_rows(x):
    return _tri_sum(x, False)


_cumsum_rows.defvjp(lambda x: (_tri_sum(x, False), None), lambda _, dy: (_tri_sum(dy, True),))


@functools.partial(jax.custom_vjp, nondiff_argnums=(1,))
def _roll_rows(x, shift):
    return pltpu.roll(x, shift, 0)


def _roll_rows_fwd(x, shift):
    return pltpu.roll(x, shift, 0), None


def _roll_rows_bwd(shift, _, dy):
    n = dy.shape[0]
    return (pltpu.roll(dy, (n - shift) % n, 0),)


_roll_rows.defvjp(_roll_rows_fwd, _roll_rows_bwd)


def _shift_down(x, first_row):
    row = lax.broadcasted_iota(jnp.int32, x.shape, 0)
    return jnp.where(row == 0, first_row, _roll_rows(x, 1))


def _sigmoid(x):
    return 1.0 / (1.0 + jnp.exp(-x))


def _softplus(x):
    return jnp.maximum(x, 0.0) + jnp.log(1.0 + jnp.exp(-jnp.abs(x)))


def _gelu(x):
    return 0.5 * x * (1.0 + jnp.tanh(math.sqrt(2.0 / math.pi) * (x + 0.044715 * (x * x * x))))


def _layer_norm(x, g, b):
    mu = jnp.mean(x, axis=-1, keepdims=True)
    xc = x - mu
    var = jnp.mean(xc * xc, axis=-1, keepdims=True)
    return xc * lax.rsqrt(var + LN_EPS) * g + b


def _spec(block, index_map):
    return pl.BlockSpec(block, index_map)


def _tile(arr, tm):
    return (arr, (tm, arr.shape[1]), lambda i: (i, 0))


def _full(arr):
    nd = arr.ndim
    return (arr, arr.shape, lambda *_: (0,) * nd)


def _fwd_call(name, fn, grid, prims, outs):
    n_in = len(prims)

    def body(*refs):
        vals = [r[...] for r in refs[:n_in]]
        vals = [v.astype(F32) if v.dtype != F32 else v for v in vals]
        res = fn(*vals)
        for o, r in zip(refs[n_in:], res):
            o[...] = r.astype(o.dtype)

    return pl.pallas_call(
        body, name=name, grid=grid,
        in_specs=[_spec(b, m) for _, b, m in prims],
        out_specs=[_spec(b, m) for _, _, b, m in outs],
        out_shape=[jax.ShapeDtypeStruct(s, d) for s, d, _, _ in outs],
        compiler_params=_cparams(len(grid)),
    )(*[a for a, _, _ in prims])


def _bwd_call(name, fn, grid, prims, cts, wants, addends=None):
    addends = addends or {}
    n_in, n_ct = len(prims), len(cts)
    out_place = {}
    wants, out_dtype = list(wants), {}
    for i, w in enumerate(wants):
        if isinstance(w, tuple):
            wants[i], out_place[i] = w
        elif w == 'tile_bf16':
            wants[i], out_dtype[i] = 'tile', BF16
    diff = [i for i, w in enumerate(wants) if w]
    add_idx = [i for i in diff if i in addends]
    last_axis = len(grid) - 1

    def body(*refs):
        vals = [r[...] for r in refs[:n_in]]
        vals = [v.astype(F32) if v.dtype != F32 else v for v in vals]
        ct_vals = [r[...] for r in refs[n_in:n_in + n_ct]]
        ct_vals = [v.astype(F32) if v.dtype != F32 else v for v in ct_vals]
        add_refs = dict(zip(add_idx, refs[n_in + n_ct:n_in + n_ct + len(add_idx)]))
        out_refs = refs[n_in + n_ct + len(add_idx):]

        def f(*dargs):
            full = list(vals)
            for i, a in zip(diff, dargs):
                full[i] = a
            return tuple(fn(*full))

        _, vjp = jax.vjp(f, *[vals[i] for i in diff])
        grads = vjp(tuple(ct_vals))
        first = pl.program_id(last_axis) == 0
        for o, g, i in zip(out_refs, grads, diff):
            if wants[i] == 'tile':
                if i in add_refs:
                    g = g + add_refs[i][...]
                o[...] = g.astype(o.dtype)
            else:
                @pl.when(first)
                def _(o=o):
                    o[...] = jnp.zeros(o.shape, o.dtype)
                o[...] += g

    ins = list(prims) + list(cts) + [addends[i] for i in add_idx]
    places = [out_place.get(i, (prims[i][0].shape, prims[i][1], prims[i][2])) for i in diff]
    return pl.pallas_call(
        body, name=name, grid=grid,
        in_specs=[_spec(b, m) for _, b, m in ins],
        out_specs=[_spec(b, m) for _, b, m in places],
        out_shape=[jax.ShapeDtypeStruct(s, out_dtype.get(i, F32)) for i, (s, _, _) in zip(diff, places)],
        compiler_params=_cparams(len(grid)),
    )(*[a for a, _, _ in ins])


def _matmul(name, a, b, mode, *, out_dtype=F32, lhs_fn=None, epi=None, extra=None, tm=1024, tn=1024, tk=512,
            b_cols=None, b_rows=None, out_cols=False):
    if mode == 'tn':
        kdim, m = a.shape
    else:
        m, kdim = a.shape
    if b_rows is not None:
        loc, cols = b.shape[-2], b.shape[-1]
        if mode == 'nn':
            n, tk = cols, loc
        else:
            n, tn = N_DEV * loc, loc
    elif b_cols is None:
        n = b.shape[0] if mode == 'nt' else b.shape[1]
    else:
        rows, loc = b.shape[-2], b.shape[-1]
        if mode == 'nn':
            n, tn = N_DEV * loc, loc
        else:
            n, tk = rows, loc
    if out_cols:
        tn = n // N_DEV
    tm, tn, tk = min(tm, m), min(tn, n), min(tk, kdim)
    nk = kdim // tk
    grid = (m // tm, n // tn, nk)
    a_spec = _spec((tk, tm), lambda i, j, k: (k, i)) if mode == 'tn' else _spec((tm, tk), lambda i, j, k: (i, k))
    if b_rows is not None:
        lead = tuple(b_rows)
        skip = (None,) * (1 + len(lead))
        if mode == 'nn':
            b_spec = _spec(skip + (tk, tn), lambda i, j, k: (k,) + lead + (0, j))
        else:
            b_spec = _spec(skip + (tn, tk), lambda i, j, k: (j,) + lead + (0, k))
    elif b_cols is None:
        b_spec = _spec((tn, tk), lambda i, j, k: (j, k)) if mode == 'nt' else _spec((tk, tn), lambda i, j, k: (k, j))
    else:
        lead = tuple(b_cols)
        skip = (None,) * (1 + len(lead))
        if mode == 'nn':
            b_spec = _spec(skip + (tk, tn), lambda i, j, k: (j,) + lead + (k, 0))
        else:
            b_spec = _spec(skip + (tn, tk), lambda i, j, k: (k,) + lead + (j, 0))
    o_spec = _spec((tm, tn), lambda i, j, k: (i, j))
    if out_cols:
        o_place, o_shape = _spec((None, tm, tn), lambda i, j, k: (j, i, 0)), (N_DEV, m, tn)
    else:
        o_place, o_shape = o_spec, (m, n)
    dims = _DOT_DIMS[mode]

    def body(*refs):
        if extra is None:
            a_ref, b_ref, o_ref, acc = refs
            x_ref = None
        else:
            a_ref, b_ref, x_ref, o_ref, acc = refs
        k = pl.program_id(2)

        @pl.when(k == 0)
        def _():
            acc[...] = jnp.zeros(acc.shape, F32)

        av = a_ref[...]
        if lhs_fn is not None:
            av = lhs_fn(av.astype(F32))
        acc[...] += lax.dot_general(av.astype(BF16), b_ref[...].astype(BF16), dims, preferred_element_type=F32)

        @pl.when(k == nk - 1)
        def _():
            r = acc[...]
            if epi is not None:
                r = epi(r, x_ref[...])
            o_ref[...] = r.astype(o_ref.dtype)

    ops, specs = [a, b], [a_spec, b_spec]
    if extra is not None:
        ops.append(extra)
        specs.append(o_spec)
    return pl.pallas_call(
        body, name=name, grid=grid, in_specs=specs, out_specs=o_place,
        out_shape=jax.ShapeDtypeStruct(o_shape, out_dtype),
        scratch_shapes=[pltpu.VMEM((tm, tn), F32)],
        compiler_params=_cparams(3),
    )(*ops)


def _relu2(x):
    r = jnp.maximum(x, 0.0)
    return r * r


def _mix_fn(x, halo, mu, w0, w1, w2, a0, a1, a2, g1, g2):
    prev_row = jnp.where(pl.program_id(0) == 0, 0.0, halo[7:8, :])
    xx = _shift_down(x, prev_row) - x
    xr, xw, xk, xv, xa, xg = (x + xx * mu[i:i + 1, :] for i in range(6))
    w_pre = w0 + _mm(jnp.tanh(_mm(xw, w1)), w2)
    log_decay = -jnp.exp(-_softplus(-w_pre) - 0.5)
    a = _sigmoid(a0 + _mm(_mm(xa, a1), a2))
    g = _mm(_sigmoid(_mm(xg, g1)), g2)
    return xr, xk, xv, log_decay, a, g


def _chunk_fn(r, lw, k, v, a, k_k, k_a, r_k):
    n, n_pair = r.shape[0], r.shape[1] // PAIR
    to_batch = lambda t: jnp.stack([t[:, p * PAIR:(p + 1) * PAIR] for p in range(n_pair)])
    from_batch = lambda t: jnp.concatenate([t[p] for p in range(n_pair)], axis=1)
    r, lw, k, v, a, k_k, k_a, r_k = (to_batch(t) for t in (r, lw, k, v, a, k_k, k_a, r_k))
    lane = lax.broadcasted_iota(jnp.int32, (1, 1, PAIR), 2)
    m0 = lane < HEAD

    def head_sum(t):
        s0 = jnp.sum(jnp.where(m0, t, 0.0), axis=2, keepdims=True)
        s1 = jnp.sum(jnp.where(m0, 0.0, t), axis=2, keepdims=True)
        return jnp.where(m0, s0, s1)

    kk = k * k_k
    kk = kk / jnp.maximum(jnp.sqrt(head_sum(kk * kk)), 1e-12)
    k2 = k * (1.0 + (a - 1.0) * k_a)
    aa, bb = -kk, kk * a
    bonus = head_sum(r * k2 * r_k) * v

    cum = _cumsum_rows(lw)
    p_in, p_ex, p_inv = jnp.exp(cum), jnp.exp(cum - lw), jnp.exp(-cum)
    at, rt, kt, bt = aa * p_ex, r * p_in, k2 * p_inv, bb * p_inv

    def stack_masked(t):
        return jnp.concatenate([jnp.where(m0, t, 0.0), jnp.where(m0, 0.0, t)], axis=1)

    def unstack_sum(t):
        return t[:, :n] + t[:, n:]

    def unstack_select(t):
        return jnp.where(m0, t[:, :n], t[:, n:])

    ti = lax.broadcasted_iota(jnp.int32, (1, 2 * n, 2 * n), 1)
    tj = lax.broadcasted_iota(jnp.int32, (1, 2 * n, 2 * n), 2)
    same = (ti >= n) == (tj >= n)
    incl, strict = same & (ti >= tj), same & (ti > tj)
    eye_n = (ti == tj).astype(F32)
    at_s, rt_s, kt_s, bt_s = stack_masked(at), stack_masked(rt), stack_masked(kt), stack_masked(bt)
    v_s = jnp.concatenate([v, v], axis=1)
    a_ab = jnp.where(strict, _pdot(at_s, bt_s, 'nt'), 0.0)
    a_ak = jnp.where(strict, _pdot(at_s, kt_s, 'nt'), 0.0)
    a_rb = jnp.where(incl, _pdot(rt_s, bt_s, 'nt'), 0.0)
    a_rk = jnp.where(incl, _pdot(rt_s, kt_s, 'nt'), 0.0)
    inv, pw = eye_n + a_ab, a_ab
    for _ in range(int(math.log2(n)) - 1):
        pw = _pdot(pw, pw, 'nn')
        inv = inv + _pdot(inv, pw, 'nn')
    w_s = _pdot(inv, at_s, 'nn')
    y_s = _pdot(inv, _pdot(a_ak, v_s, 'nn'), 'nn')
    q_all = unstack_sum(rt_s + _pdot(a_rb, w_s, 'nn'))
    oi_all = unstack_select(_pdot(a_rb, y_s, 'nn') + _pdot(a_rk, v_s, 'nn'))
    y_m = stack_masked(unstack_select(y_s))

    mm = _pdot(bt_s, w_s, 'tn')
    nn = _pdot(bt_s, y_m, 'tn') + _pdot(kt_s, stack_masked(v), 'tn')
    ei = lax.broadcasted_iota(jnp.int32, (1, PAIR, PAIR), 1)
    ej = lax.broadcasted_iota(jnp.int32, (1, PAIR, PAIR), 2)
    eye_p = (ei == ej).astype(F32)
    decay_col = jnp.sum(eye_p * p_in[:, n - 1:n, :], axis=2, keepdims=True)
    return ((decay_col * (eye_p + mm))[None], (decay_col * nn)[None], from_batch(q_all), from_batch(oi_all),
            from_batch(bonus))


def _gate_fn(o, bonus, g, lnx_g, lnx_b):
    ci = lax.broadcasted_iota(jnp.int32, (D_MODEL, D_MODEL // HEAD), 0) // HEAD
    hi = lax.broadcasted_iota(jnp.int32, (D_MODEL, D_MODEL // HEAD), 1)
    ind = (ci == hi).astype(F32)

    def head_mean(t):
        return _hdot_nt(_hdot(t, ind), ind) * (1.0 / HEAD)

    om = head_mean(o)
    oc = o - om
    ov = head_mean(oc * oc)
    on = oc * lax.rsqrt(ov + GN_EPS) * lnx_g + lnx_b
    return ((on + bonus) * g,)


def _res_ln_fn(h, y, g, b):
    return (_layer_norm(DN_ALPHA * h + y, g, b),)


def _glu_ln_fn(h, z, g, b):
    mix = z[:, :D_MODEL] * _sigmoid(z[:, D_MODEL:])
    return (_layer_norm(DN_ALPHA * h + mix, g, b),)


def _s5_param_fn(a_re, a_im, log_dt, b_re, b_im):
    dt = jnp.exp(log_dt)
    lam_re = jnp.minimum(a_re, -1e-4)
    lam_im = a_im
    mag = jnp.exp(dt * lam_re)
    abar_re = mag * jnp.cos(dt * lam_im)
    abar_im = mag * jnp.sin(dt * lam_im)
    den = lam_re * lam_re + lam_im * lam_im
    nr, ni = abar_re - 1.0, abar_im
    coef_re = (nr * lam_re + ni * lam_im) / den
    coef_im = (ni * lam_re - nr * lam_im) / den
    return abar_re, abar_im, coef_re, coef_im


def _s5_bbar_fn(coef_re, coef_im, b_re, b_im):
    return coef_re * b_re - coef_im * b_im, coef_re * b_im + coef_im * b_re


def _s5_in_fn(u, b_re, b_im):
    return _hdot(u, b_re[0]), _hdot(u, b_im[0])


def _s5_out_fn(s_re, s_im, u, c_re, c_im, d):
    y = _hdot(s_re, c_re[0]) - _hdot(s_im, c_im[0]) + u * d
    return (_gelu(y),)


def _s5_step_fn(a_re, a_im, s_re, s_im, halo_re, halo_im):
    first = pl.program_id(1) == 0
    p_re = _shift_down(s_re, jnp.where(first, 0.0, halo_re[7:8, :]))
    p_im = _shift_down(s_im, jnp.where(first, 0.0, halo_im[7:8, :]))
    return a_re * p_re - a_im * p_im, a_re * p_im + a_im * p_re


def _rwkv_state_fwd(m_all, n_all, q, oi):
    n_chunk = m_all.shape[0]

    def body(m_ref, n_ref, q_ref, oi_ref, o_ref, zs_ref, z):
        @pl.when(pl.program_id(0) == 0)
        def _():
            z[...] = jnp.zeros(z.shape, F32)

        for j in range(N_PAIR):
            sl = slice(j * PAIR, (j + 1) * PAIR)
            zj = z[j]
            zs_ref[0, j] = zj
            o_ref[:, sl] = _hdot(q_ref[:, sl], zj) + oi_ref[:, sl]
            z[j] = _hdot(m_ref[0, j], zj) + n_ref[0, j]

    st = _spec((1, N_PAIR, PAIR, PAIR), lambda c: (c, 0, 0, 0))
    tok = _spec((CHUNK, D_MODEL), lambda c: (c, 0))
    return pl.pallas_call(
        body, name="rwkv_state_fwd", grid=(n_chunk,),
        in_specs=[st, st, tok, tok], out_specs=[tok, st],
        out_shape=[jax.ShapeDtypeStruct(q.shape, F32), jax.ShapeDtypeStruct(m_all.shape, F32)],
        scratch_shapes=[pltpu.VMEM((N_PAIR, PAIR, PAIR), F32)],
        compiler_params=_cparams(1),
    )(m_all, n_all, q, oi)


def _rwkv_state_bwd(m_all, q, zs, d_o):
    n_chunk = m_all.shape[0]

    def body(m_ref, q_ref, zs_ref, do_ref, dq_ref, dm_ref, dn_ref, g):
        @pl.when(pl.program_id(0) == 0)
        def _():
            g[...] = jnp.zeros(g.shape, F32)

        bi = lax.broadcasted_iota(jnp.int32, (PAIR, PAIR), 0) // HEAD
        bj = lax.broadcasted_iota(jnp.int32, (PAIR, PAIR), 1) // HEAD
        same_head = bi == bj
        for j in range(N_PAIR):
            sl = slice(j * PAIR, (j + 1) * PAIR)
            gj, zj, do = g[j], zs_ref[0, j], do_ref[:, sl]
            dn_ref[0, j] = gj
            dm_ref[0, j] = _hdot_nt(gj, zj)
            dq_ref[:, sl] = _hdot_nt(do, zj)
            g[j] = _hdot_tn(m_ref[0, j], gj) + jnp.where(same_head, _hdot_tn(q_ref[:, sl], do), 0.0)

    st = _spec((1, N_PAIR, PAIR, PAIR), lambda c: (n_chunk - 1 - c, 0, 0, 0))
    tok = _spec((CHUNK, D_MODEL), lambda c: (n_chunk - 1 - c, 0))
    return pl.pallas_call(
        body, name="rwkv_state_bwd", grid=(n_chunk,),
        in_specs=[st, tok, st, tok], out_specs=[tok, st, st],
        out_shape=[jax.ShapeDtypeStruct(q.shape, F32), jax.ShapeDtypeStruct(m_all.shape, F32),
                   jax.ShapeDtypeStruct(m_all.shape, F32)],
        scratch_shapes=[pltpu.VMEM((N_PAIR, PAIR, PAIR), F32)],
        compiler_params=_cparams(1),
    )(m_all, q, zs, d_o)


def _s5_scan(name, b_re, b_im, a_re, a_im, reverse):
    t_len, lanes = b_re.shape
    tt, lb, grp = 512, SSM_BLK_ST, 8
    tt = min(tt, t_len)
    n_t, n_grp = t_len // tt, tt // grp

    def cmul(xr, xi, yr, yi):
        return xr * yr - xi * yi, xr * yi + xi * yr

    def body(bre_ref, bim_ref, are_ref, aim_ref, sre_ref, sim_ref, carry):
        @pl.when(pl.program_id(1) == 0)
        def _():
            carry[...] = jnp.zeros(carry.shape, F32)

        a1 = (are_ref[...], aim_ref[...])
        a2 = cmul(*a1, *a1)
        a4 = cmul(*a2, *a2)
        a8 = cmul(*a4, *a4)
        row = lax.broadcasted_iota(jnp.int32, (grp, lb), 0)
        expo = (grp - row) if reverse else (row + 1)
        pw = (jnp.ones((grp, lb), F32), jnp.zeros((grp, lb), F32))
        for bit, ap in ((1, a1), (2, a2), (4, a4), (8, a8)):
            nxt = cmul(*pw, *ap)
            sel = (expo & bit) != 0
            pw = (jnp.where(sel, nxt[0], pw[0]), jnp.where(sel, nxt[1], pw[1]))

        def group(i, c):
            gi = (n_grp - 1 - i) if reverse else i
            rows = pl.ds(pl.multiple_of(gi * grp, grp), grp)
            xr, xi = bre_ref[rows, :], bim_ref[rows, :]
            for d, ad in ((1, a1), (2, a2), (4, a4)):
                if reverse:
                    keep = row < grp - d
                    sr, si = pltpu.roll(xr, grp - d, 0), pltpu.roll(xi, grp - d, 0)
                else:
                    keep = row >= d
                    sr, si = pltpu.roll(xr, d, 0), pltpu.roll(xi, d, 0)
                pr, pi = cmul(*ad, jnp.where(keep, sr, 0.0), jnp.where(keep, si, 0.0))
                xr, xi = xr + pr, xi + pi
            cr, ci = cmul(*pw, c[0], c[1])
            xr, xi = xr + cr, xi + ci
            sre_ref[rows, :] = xr
            sim_ref[rows, :] = xi
            edge = slice(0, 1) if reverse else slice(grp - 1, grp)
            return xr[edge, :], xi[edge, :]

        c = lax.fori_loop(0, n_grp, group, (carry[0:1, :], carry[1:2, :]))
        carry[0:1, :] = c[0]
        carry[1:2, :] = c[1]

    tmap = (lambda l, t: (n_t - 1 - t, l)) if reverse else (lambda l, t: (t, l))
    blk = _spec((tt, lb), tmap)
    par = _spec((1, lb), lambda l, t: (0, l))
    return pl.pallas_call(
        body, name=name, grid=(lanes // lb, n_t),
        in_specs=[blk, blk, par, par], out_specs=[blk, blk],
        out_shape=[jax.ShapeDtypeStruct(b_re.shape, F32)] * 2,
        scratch_shapes=[pltpu.VMEM((2, lb), F32)],
        compiler_params=_cparams(2),
    )(b_re, b_im, a_re, a_im)


def _loss_call(h, target, tm):
    n_tok, d = h.shape

    def body(h_ref, t_ref, acc_ref, dh_ref):
        @pl.when(pl.program_id(0) == 0)
        def _():
            acc_ref[...] = jnp.zeros(acc_ref.shape, F32)

        e = h_ref[...] - t_ref[...]
        dh_ref[...] = e * (1.0 / d)
        acc_ref[...] += jnp.sum(jnp.sum(e * e, axis=1, keepdims=True), axis=0, keepdims=True)

    tok = _spec((tm, d), lambda i: (i, 0))
    return pl.pallas_call(
        body, name="loss", grid=(n_tok // tm,),
        in_specs=[tok, tok], out_specs=[_spec((8, 128), lambda i: (0, 0)), tok],
        out_shape=[jax.ShapeDtypeStruct((8, 128), F32), jax.ShapeDtypeStruct(h.shape, F32)],
        compiler_params=_cparams(1),
    )(h, target)


def _block_diag(t):
    nb, ng, rows, cols = t.shape
    eye = jnp.eye(ng, dtype=t.dtype)
    return jnp.einsum('bgrc,gh->bgrhc', t, eye).reshape(nb, ng * rows, ng * cols)


def _block_diag_t(t, rows, cols):
    nb, ng = t.shape[0], t.shape[1] // rows
    t5 = t.reshape(nb, ng, rows, ng, cols)
    return jnp.einsum('bgrhc,gh->bgrc', t5, jnp.eye(ng, dtype=t.dtype))


def _mlp_fwd(layer, h, w1g, w2g):
    pre = _matmul(f"mlp{layer}_up", h, w1g, 'nn', b_cols=(layer,))
    return pre, _matmul(f"mlp{layer}_down", pre, w2g, 'nn', lhs_fn=_relu2, b_rows=(layer,))


def _mlp_bwd(layer, h, pre, dy, dh_res, w1g, w2g):
    dpre = _matmul(f"mlp{layer}_down_dx", dy, w2g, 'nt', epi=lambda acc, p: acc * (2.0 * jnp.maximum(p, 0.0)),
                   extra=pre, b_rows=(layer,))
    dw2 = _matmul(f"mlp{layer}_down_dw", pre, dy, 'tn', lhs_fn=_relu2)
    dw1 = _matmul(f"mlp{layer}_up_dw", h, dpre, 'tn', out_cols=True)
    dh = _matmul(f"mlp{layer}_up_dx", dpre, w1g, 'nt', epi=lambda acc, e: acc + e, extra=dh_res, b_cols=(layer,))
    return dh, dw1, dw2


def _local_step(x, target, w, wg):
    n_tok = x.shape[0]
    tm = min(TOKEN_TILE, n_tok)
    n_tile, n_chunk = n_tok // tm, n_tok // CHUNK
    row = lambda v: v.reshape(1, -1)
    big = {n: wg[n].reshape(D_MODEL, D_MODEL) for n in ('rw_wr', 'rw_wk', 'rw_wv', 'rw_wo')}
    w1g, w2g, glu_g = wg['mlp_w1'], wg['mlp_w2'], wg['s5_w_glu']
    ln_g, ln_b = w['ln_g'], w['ln_b']
    grads, big_grads = {}, {}
    to_slots = lambda t: t.reshape(N_DEV, t.shape[0] // N_DEV, t.shape[1])

    halo_spec = (8, D_MODEL), lambda i: (jnp.maximum(i * (tm // 8) - 1, 0), 0)
    mix_p = [_tile(x, tm), (x,) + halo_spec, _full(w['rw_mu'][0]), _full(row(w['rw_w0'])), _full(w['rw_w1'][0]),
             _full(w['rw_w2'][0]), _full(row(w['rw_a0'])), _full(w['rw_a1'][0]), _full(w['rw_a2'][0]),
             _full(w['rw_g1'][0]), _full(w['rw_g2'][0])]
    tok_out = lambda dt: ((n_tok, D_MODEL), dt, (tm, D_MODEL), lambda i: (i, 0))
    xr, xk, xv, lw, a, g = _fwd_call("rwkv_mix", _mix_fn, (n_tile,), mix_p,
                                     [tok_out(BF16), tok_out(BF16), tok_out(BF16), tok_out(F32), tok_out(F32), tok_out(F32)])
    r = _matmul("rwkv_r", xr, big['rw_wr'], 'nn')
    k = _matmul("rwkv_k", xk, big['rw_wk'], 'nn')
    v = _matmul("rwkv_v", xv, big['rw_wv'], 'nn')

    slab_w = CHUNK_PAIRS * PAIR
    slab = lambda arr: (arr, (CHUNK, slab_w), lambda j, c: (c, j))
    pslab = lambda arr: (arr, (1, slab_w), lambda j, c: (0, j))
    chunk_p = [slab(r), slab(lw), slab(k), slab(v), slab(a), pslab(row(w['rw_k_k'])), pslab(row(w['rw_k_a'])),
               pslab(w['rw_r_k'].reshape(1, -1))]
    st_shape = (n_chunk, N_PAIR, PAIR, PAIR)
    st_out = (st_shape, F32, (1, CHUNK_PAIRS, PAIR, PAIR), lambda j, c: (c, j, 0, 0))
    sl_out = ((n_tok, D_MODEL), F32, (CHUNK, slab_w), lambda j, c: (c, j))
    chunk_grid = (N_PAIR // CHUNK_PAIRS, n_chunk)
    m_all, n_all, q, oi, bonus = _fwd_call("rwkv_chunk", _chunk_fn, chunk_grid, chunk_p,
                                           [st_out, st_out, sl_out, sl_out, sl_out])
    o, zs = _rwkv_state_fwd(m_all, n_all, q, oi)

    gate_p = [_tile(o, tm), _tile(bonus, tm), _tile(g, tm), _full(row(w['rw_lnx_g'])), _full(row(w['rw_lnx_b']))]
    (og,) = _fwd_call("rwkv_gate", _gate_fn, (n_tile,), gate_p, [tok_out(BF16)])
    y0 = _matmul("rwkv_o", og, big['rw_wo'], 'nn')
    ln0_p = [_tile(x, tm), _tile(y0, tm), _full(ln_g[0:1]), _full(ln_b[0:1])]
    (h1,) = _fwd_call("ln0", _res_ln_fn, (n_tile,), ln0_p, [tok_out(F32)])
    pre0, y1 = _mlp_fwd(0, h1, w1g, w2g)
    ln1_p = [_tile(h1, tm), _tile(y1, tm), _full(ln_g[1:2]), _full(ln_b[1:2])]
    (h2,) = _fwd_call("ln1", _res_ln_fn, (n_tile,), ln1_p, [tok_out(F32)])

    gp = (SSM_GROUPS, SSM_STATE)
    par_p = [_full(w['s5_a_re'][0]), _full(w['s5_a_im'][0]), _full(w['s5_log_dt'].reshape(SSM_GROUPS, 1))]
    s5_par_fn = lambda ar, ai, ld: _s5_param_fn(ar, ai, ld, None, None)
    gp_out = (gp, F32, gp, lambda i: (0, 0))
    abar_re, abar_im, coef_re, coef_im = _fwd_call("s5_param", s5_par_fn, (1,), par_p, [gp_out] * 4)
    b_flat = [w[n][0].reshape(SSM_LANES, SSM_GROUP) for n in ('s5_b_re', 's5_b_im')]
    bbar_p = [_full(coef_re.reshape(SSM_LANES, 1)), _full(coef_im.reshape(SSM_LANES, 1)), _full(b_flat[0]), _full(b_flat[1])]
    bb_out = ((SSM_LANES, SSM_GROUP), F32, (SSM_LANES, SSM_GROUP), lambda i: (0, 0))
    bbar_re, bbar_im = _fwd_call("s5_bbar", _s5_bbar_fn, (1,), bbar_p, [bb_out] * 2)
    to_in = lambda t: _block_diag(t.reshape(N_SSM_BLK, 8, SSM_STATE, SSM_GROUP).transpose(0, 1, 3, 2))
    to_out = lambda t: _block_diag(t.reshape(N_SSM_BLK, 8, SSM_GROUP, SSM_STATE).transpose(0, 1, 3, 2))
    bblk_re, bblk_im = to_in(bbar_re), to_in(bbar_im)
    cblk_re, cblk_im = to_out(w['s5_c_re'][0]), to_out(w['s5_c_im'][0])
    a_row_re, a_row_im = abar_re.reshape(1, SSM_LANES), abar_im.reshape(1, SSM_LANES)

    u_slab = lambda arr: (arr, (tm, SSM_BLK_IN), lambda l, i: (i, l))
    s_slab = lambda arr: (arr, (tm, SSM_BLK_ST), lambda l, i: (i, l))
    blk3 = lambda arr: (arr, (1,) + arr.shape[1:], lambda l, i: (l, 0, 0))
    in_p = [u_slab(h2), blk3(bblk_re), blk3(bblk_im)]
    st_tok = ((n_tok, SSM_LANES), F32, (tm, SSM_BLK_ST), lambda l, i: (i, l))
    bu_re, bu_im = _fwd_call("s5_in", _s5_in_fn, (N_SSM_BLK, n_tile), in_p, [st_tok, st_tok])
    s_re, s_im = _s5_scan("s5_scan_fwd", bu_re, bu_im, a_row_re, a_row_im, False)
    d_row = row(w['s5_d'])
    out_p = [s_slab(s_re), s_slab(s_im), u_slab(h2), blk3(cblk_re), blk3(cblk_im),
             (d_row, (1, SSM_BLK_IN), lambda l, i: (0, l))]
    u_tok = lambda dt: ((n_tok, D_MODEL), dt, (tm, SSM_BLK_IN), lambda l, i: (i, l))
    (yg,) = _fwd_call("s5_out", _s5_out_fn, (N_SSM_BLK, n_tile), out_p, [u_tok(BF16)])
    z = _matmul("s5_glu", yg, glu_g, 'nn', b_cols=(0,))
    ln2_p = [_tile(h2, tm), _tile(z, tm), _full(ln_g[2:3]), _full(ln_b[2:3])]
    (h3,) = _fwd_call("ln2", _glu_ln_fn, (n_tile,), ln2_p, [tok_out(F32)])
    pre1, y3 = _mlp_fwd(1, h3, w1g, w2g)
    ln3_p = [_tile(h3, tm), _tile(y3, tm), _full(ln_g[3:4]), _full(ln_b[3:4])]
    (h4,) = _fwd_call("ln3", _res_ln_fn, (n_tile,), ln3_p, [tok_out(F32)])

    loss_acc, dh4 = _loss_call(h4, target, tm)

    ln_w = ['tile', 'tile', 'acc', 'acc']
    dh3_res, dy3, dg3, db3 = _bwd_call("ln3_bwd", _res_ln_fn, (n_tile,), ln3_p, [_tile(dh4, tm)], ln_w)
    dh3, dw1_1, dw2_1 = _mlp_bwd(1, h3, pre1, dy3, dh3_res, w1g, w2g)
    dh2_res, dz, dg2, db2 = _bwd_call("ln2_bwd", _glu_ln_fn, (n_tile,), ln2_p, [_tile(dh3, tm)], ln_w)
    dyg = _matmul("s5_glu_dx", dz, glu_g, 'nt', b_cols=(0,))
    big_grads['s5_w_glu'] = [_matmul("s5_glu_dw", yg, dz, 'tn', out_cols=True)]
    ds_re, ds_im, du_out, dcb_re, dcb_im, dd = _bwd_call(
        "s5_out_bwd", _s5_out_fn, (N_SSM_BLK, n_tile), out_p, [u_slab(dyg)],
        ['tile', 'tile', 'tile', 'acc', 'acc', 'acc'], addends={2: u_slab(dh2_res)})
    gs_re, gs_im = _s5_scan("s5_scan_bwd", ds_re, ds_im, a_row_re, -a_row_im, True)
    s_halo = lambda arr: (arr, (8, SSM_BLK_ST), lambda l, i: (jnp.maximum(i * (tm // 8) - 1, 0), l))
    a_par = lambda arr: (arr, (1, SSM_BLK_ST), lambda l, i: (0, l))
    step_p = [a_par(a_row_re), a_par(a_row_im), s_slab(s_re), s_slab(s_im), s_halo(s_re), s_halo(s_im)]
    dabar_re, dabar_im = _bwd_call("s5_abar_bwd", _s5_step_fn, (N_SSM_BLK, n_tile), step_p,
                                   [s_slab(gs_re), s_slab(gs_im)], ['acc', 'acc', None, None, None, None])
    dh2, dbb_re, dbb_im = _bwd_call("s5_in_bwd", _s5_in_fn, (N_SSM_BLK, n_tile), in_p, [s_slab(gs_re), s_slab(gs_im)],
                                    ['tile', 'acc', 'acc'], addends={0: u_slab(du_out)})
    from_in = lambda t: _block_diag_t(t, SSM_GROUP, SSM_STATE).transpose(0, 1, 3, 2).reshape(SSM_LANES, SSM_GROUP)
    from_out = lambda t: _block_diag_t(t, SSM_STATE, SSM_GROUP).transpose(0, 1, 3, 2).reshape(1, SSM_GROUPS, SSM_GROUP, SSM_STATE)
    grads['s5_c_re'], grads['s5_c_im'] = from_out(dcb_re), from_out(dcb_im)
    grads['s5_d'] = dd
    dcoef_re, dcoef_im, db_re, db_im = _bwd_call(
        "s5_bbar_bwd", _s5_bbar_fn, (1,), bbar_p, [_full(from_in(dbb_re)), _full(from_in(dbb_im))], ['acc'] * 4)
    grads['s5_b_re'] = db_re.reshape(w['s5_b_re'].shape)
    grads['s5_b_im'] = db_im.reshape(w['s5_b_im'].shape)
    par_ct = [_full(dabar_re.reshape(gp)), _full(dabar_im.reshape(gp)), _full(dcoef_re.reshape(gp)), _full(dcoef_im.reshape(gp))]
    da_re, da_im, dlog_dt = _bwd_call("s5_param_bwd", s5_par_fn, (1,), par_p, par_ct, ['acc'] * 3)
    grads['s5_a_re'], grads['s5_a_im'], grads['s5_log_dt'] = da_re[None], da_im[None], dlog_dt.reshape(1, SSM_GROUPS)

    dh1_res, dy1, dg1, db1 = _bwd_call("ln1_bwd", _res_ln_fn, (n_tile,), ln1_p, [_tile(dh2, tm)], ln_w)
    dh1, dw1_0, dw2_0 = _mlp_bwd(0, h1, pre0, dy1, dh1_res, w1g, w2g)
    big_grads['mlp_w1'], big_grads['mlp_w2'] = [dw1_0, dw1_1], [to_slots(dw2_0), to_slots(dw2_1)]
    dx_res, dy0, dg0, db0 = _bwd_call("ln0_bwd", _res_ln_fn, (n_tile,), ln0_p, [_tile(dh1, tm)], ln_w)
    grads['ln_g'] = jnp.concatenate([dg0, dg1, dg2, dg3])
    grads['ln_b'] = jnp.concatenate([db0, db1, db2, db3])
    dog = _matmul("rwkv_o_dx", dy0, big['rw_wo'], 'nt')
    big_grads['rw_wo'] = [to_slots(_matmul("rwkv_o_dw", og, dy0, 'tn'))]
    d_o, dbonus, dgate, grads['rw_lnx_g'], grads['rw_lnx_b'] = _bwd_call(
        "rwkv_gate_bwd", _gate_fn, (n_tile,), gate_p, [_tile(dog, tm)], ['tile', 'tile', 'tile', 'acc', 'acc'])
    dq, dm_all, dn_all = _rwkv_state_bwd(m_all, q, zs, d_o)
    st_ct = lambda arr: (arr, (1, CHUNK_PAIRS, PAIR, PAIR), lambda j, c: (c, j, 0, 0))
    dr, dlw, dk, dv, da, grads['rw_k_k'], grads['rw_k_a'], dr_k = _bwd_call(
        "rwkv_chunk_bwd", _chunk_fn, chunk_grid, chunk_p,
        [st_ct(dm_all), st_ct(dn_all), slab(dq), slab(d_o), slab(dbonus)],
        ['tile_bf16', 'tile', 'tile_bf16', 'tile_bf16', 'tile'] + ['acc'] * 3)
    grads['rw_r_k'] = dr_k.reshape(w['rw_r_k'].shape)
    dxr = _matmul("rwkv_r_dx", dr, big['rw_wr'], 'nt')
    dxk = _matmul("rwkv_k_dx", dk, big['rw_wk'], 'nt')
    dxv = _matmul("rwkv_v_dx", dv, big['rw_wv'], 'nt')
    big_grads['rw_wr'] = [to_slots(_matmul("rwkv_r_dw", xr, dr, 'tn'))]
    big_grads['rw_wk'] = [to_slots(_matmul("rwkv_k_dw", xk, dk, 'tn'))]
    big_grads['rw_wv'] = [to_slots(_matmul("rwkv_v_dw", xv, dv, 'tn'))]
    mix_ct = [_tile(t, tm) for t in (dxr, dxk, dxv, dlw, da, dgate)]
    halo_grad = ('tile', ((n_tile * 8, D_MODEL), (8, D_MODEL), lambda i: (i, 0)))
    res = _bwd_call("rwkv_mix_bwd", _mix_fn, (n_tile,), mix_p, mix_ct, ['tile', halo_grad] + ['acc'] * 9,
                    addends={0: _tile(dx_res, tm)})
    corr = res[1].reshape(n_tile, 8, D_MODEL)[1:, 7:8, :]
    corr = jnp.pad(corr, ((0, 1), (tm - 1, 0), (0, 0)))
    dx = (res[0].reshape(n_tile, tm, D_MODEL) + corr).reshape(n_tok, D_MODEL)
    (grads['rw_mu'], grads['rw_w0'], grads['rw_w1'], grads['rw_w2'], grads['rw_a0'], grads['rw_a1'], grads['rw_a2'],
     grads['rw_g1'], grads['rw_g2']) = [t[None] if t.shape[0] != 1 else t for t in res[2:]]
    return loss_acc[0, 0], dx, grads, big_grads


PACK_ROWS = 128


def _pack(arrs):
    flat = jnp.concatenate([a.reshape(-1).astype(F32) for a in arrs])
    per = PACK_ROWS * PACK_W
    total = -(-flat.shape[0] // per) * per
    return jnp.pad(flat, (0, total - flat.shape[0])).reshape(-1, PACK_W)


def _unpack(packed, shapes, lead=()):
    flat = packed.reshape(lead + (-1,))
    out, off = [], 0
    for s in shapes:
        n = math.prod(s)
        out.append(flat[..., off:off + n].reshape(lead + tuple(s)))
        off += n
    return out


def _mesh_pos():
    return lax.axis_index("x"), lax.axis_index("y"), lax.axis_index("c")


def _flip(v, f):
    return 1 - v if f else v


def _hbm_call(name, body, arrays, out_shapes, n_sems):
    n_t = len(arrays)
    hbm = pl.BlockSpec(memory_space=pl.ANY)
    return pl.pallas_call(
        body, name=name, out_shape=out_shapes, in_specs=[hbm] * n_t, out_specs=[hbm] * n_t,
        scratch_shapes=[pltpu.SemaphoreType.DMA((n_t, n_sems)), pltpu.SemaphoreType.DMA((n_t, n_sems)),
                        pltpu.SemaphoreType.DMA((n_t,))],
    )(*arrays)


def _all_gather(name, locals_):
    n_t = len(locals_)

    def body(*refs):
        x_refs, out_refs = refs[:n_t], refs[n_t:2 * n_t]
        send_sems, recv_sems, local_sems = refs[2 * n_t:]
        x, y, c = _mesh_pos()
        me, sibling = (x, y, c), (x, y, 1 - c)
        chips = [(1 - x, y), (x, 1 - y), (1 - x, 1 - y)]

        def copy(t, k, block, to, own=False):
            slot = out_refs[t].at[4 * block[0] + 2 * block[1] + block[2]]
            return pltpu.make_async_remote_copy(
                src_ref=x_refs[t] if own else slot, dst_ref=slot,
                send_sem=send_sems.at[t, k], recv_sem=recv_sems.at[t, k],
                device_id=to, device_id_type=pl.DeviceIdType.MESH)

        mine = [pltpu.make_async_copy(x_refs[t], out_refs[t].at[4 * x + 2 * y + c], local_sems.at[t]) for t in range(n_t)]
        for cp in mine:
            cp.start()
        sent = []
        for t in range(n_t):
            sent.append(copy(t, 0, me, sibling, own=True))
            sent += [copy(t, 1 + j, me, (*chip, c), own=True) for j, chip in enumerate(chips)]
        for cp in sent:
            cp.start()
        for j, chip in enumerate(chips):
            for t in range(n_t):
                copy(t, 1 + j, (*chip, c), me).wait_recv()
                passed = copy(t, 4 + j, (*chip, c), sibling)
                passed.start()
                sent.append(passed)
        for t in range(n_t):
            copy(t, 0, sibling, me).wait_recv()
            for j, chip in enumerate(chips):
                copy(t, 4 + j, (*chip, 1 - c), me).wait_recv()
        for cp in sent:
            cp.wait_send()
        for cp in mine:
            cp.wait()

    outs = [jax.ShapeDtypeStruct((N_DEV,) + a.shape, a.dtype) for a in locals_]
    return _hbm_call(name, body, locals_, outs, 7)


def _all_to_all(name, blocks):
    n_t = len(blocks)

    def body(*refs):
        x_refs, out_refs = refs[:n_t], refs[n_t:2 * n_t]
        send_sems, recv_sems, local_sems = refs[2 * n_t:]
        x, y, c = _mesh_pos()
        my_slot = 4 * x + 2 * y + c
        mine = [pltpu.make_async_copy(x_refs[t].at[my_slot], out_refs[t].at[my_slot], local_sems.at[t]) for t in range(n_t)]
        for cp in mine:
            cp.start()
        copies = []
        for k in range(1, N_DEV):
            peer = (_flip(x, k & 4), _flip(y, k & 2), _flip(c, k & 1))
            peer_slot = 4 * peer[0] + 2 * peer[1] + peer[2]
            for t in range(n_t):
                sems = dict(send_sem=send_sems.at[t, k - 1], recv_sem=recv_sems.at[t, k - 1],
                            device_id=peer, device_id_type=pl.DeviceIdType.MESH)
                cp = pltpu.make_async_remote_copy(src_ref=x_refs[t].at[peer_slot], dst_ref=out_refs[t].at[my_slot], **sems)
                cp.start()
                landed = pltpu.make_async_remote_copy(src_ref=x_refs[t].at[my_slot], dst_ref=out_refs[t].at[peer_slot], **sems)
                copies.append((cp, landed))
        for sent, landed in copies:
            landed.wait_recv()
            sent.wait_send()
        for cp in mine:
            cp.wait()

    outs = [jax.ShapeDtypeStruct(a.shape, a.dtype) for a in blocks]
    return _hbm_call(name, body, blocks, outs, 7)


def _adam_call(name, slots, w, m, v):
    rows, width = w.shape
    tile = min(rows, (PACK_ROWS * PACK_W) // width)
    c1 = 1.0 / (1.0 - ADAM_B1 ** ADAM_STEP)
    c2 = 1.0 / (1.0 - ADAM_B2 ** ADAM_STEP)

    def body(s_ref, w_ref, m_ref, v_ref, g_ref, d_ref, nm_ref, nv_ref):
        g = s_ref[0]
        for i in range(1, N_DEV):
            g = g + s_ref[i]
        m_new = ADAM_B1 * m_ref[...] + (1.0 - ADAM_B1) * g
        v_new = ADAM_B2 * v_ref[...] + (1.0 - ADAM_B2) * (g * g)
        g_ref[...] = g
        nm_ref[...] = m_new
        nv_ref[...] = v_new
        d_ref[...] = -ADAM_LR * ((m_new * c1) / (jnp.sqrt(v_new * c2) + ADAM_EPS) + ADAM_WD * w_ref[...])

    blk = _spec((tile, width), lambda i: (i, 0))
    return pl.pallas_call(
        body, name=name, grid=(rows // tile,),
        in_specs=[_spec((N_DEV, tile, width), lambda i: (0, i, 0)), blk, blk, blk],
        out_specs=[blk] * 4, out_shape=[jax.ShapeDtypeStruct(w.shape, F32)] * 4,
        compiler_params=_cparams(1),
    )(slots, w, m, v)


def kernel(x, ln_g, ln_b, rw_mu, rw_w0, rw_w1, rw_w2, rw_a0, rw_a1, rw_a2, rw_g1, rw_g2, rw_k_k, rw_k_a, rw_r_k, rw_wr, rw_wk, rw_wv, rw_wo, rw_lnx_g, rw_lnx_b, s5_a_re, s5_a_im, s5_log_dt, s5_b_re, s5_b_im, s5_c_re, s5_c_im, s5_d, s5_w_glu, mlp_w1, mlp_w2, loss_target, m_ln_g, m_ln_b, m_rw_mu, m_rw_w0, m_rw_w1, m_rw_w2, m_rw_a0, m_rw_a1, m_rw_a2, m_rw_g1, m_rw_g2, m_rw_k_k, m_rw_k_a, m_rw_r_k, m_rw_wr, m_rw_wk, m_rw_wv, m_rw_wo, m_rw_lnx_g, m_rw_lnx_b, m_s5_a_re, m_s5_a_im, m_s5_log_dt, m_s5_b_re, m_s5_b_im, m_s5_c_re, m_s5_c_im, m_s5_d, m_s5_w_glu, m_mlp_w1, m_mlp_w2, v_ln_g, v_ln_b, v_rw_mu, v_rw_w0, v_rw_w1, v_rw_w2, v_rw_a0, v_rw_a1, v_rw_a2, v_rw_g1, v_rw_g2, v_rw_k_k, v_rw_k_a, v_rw_r_k, v_rw_wr, v_rw_wk, v_rw_wv, v_rw_wo, v_rw_lnx_g, v_rw_lnx_b, v_s5_a_re, v_s5_a_im, v_s5_log_dt, v_s5_b_re, v_s5_b_im, v_s5_c_re, v_s5_c_im, v_s5_d, v_s5_w_glu, v_mlp_w1, v_mlp_w2):
    given = dict(locals())
    local_w = {n: given[n] for n in WEIGHTS}
    local_m = {n: given["m_" + n] for n in WEIGHTS}
    local_v = {n: given["v_" + n] for n in WEIGHTS}
    small = [n for n in WEIGHTS if n in SHARDED and n not in BIG]
    small_shapes = [local_w[n].shape for n in small]
    rep_shapes = [local_w[n].shape for n in REPLICATED]

    gathered = _all_gather("weights_all_gather", [local_w[n].astype(BF16) for n in BIG] + [_pack([local_w[n] for n in small])])
    big_gathered = dict(zip(BIG, gathered))
    full = dict(local_w)
    for n, blk in zip(small, _unpack(gathered[-1], small_shapes, lead=(N_DEV,))):
        full[n] = jnp.concatenate([blk[i] for i in range(N_DEV)], axis=SHARDED[n])

    loss_sq, dx, grads, big_grads = _local_step(x[0], loss_target[0], full, big_gathered)
    loss = (0.5 / D_MODEL) * lax.psum(loss_sq, MESH_AXES)

    parts = [jnp.stack(jnp.split(grads[n], N_DEV, axis=SHARDED[n])) for n in small]
    small_send = jnp.stack([_pack([p[i] for p in parts]) for i in range(N_DEV)])
    big_send = [(n, i, t) for n in BIG for i, t in enumerate(big_grads[n])]
    recv = _all_to_all("grads_all_to_all", [t for _, _, t in big_send] + [small_send])
    by_name = {}
    for (n, i, _), slots in zip(big_send, recv):
        part = lambda d: d[n].reshape((-1,) + slots.shape[1:])[i]
        out = _adam_call(f"adamw_{n}_{i}", slots, part(local_w), part(local_m), part(local_v))
        by_name.setdefault(n, []).append(out)
    for n in BIG:
        by_name[n] = [jnp.stack(ts).reshape(local_w[n].shape) for ts in zip(*by_name[n])]
    small_out = _adam_call("adamw_small", recv[-1], _pack([local_w[n] for n in small]),
                           _pack([local_m[n] for n in small]), _pack([local_v[n] for n in small]))
    (rep_all,) = _all_gather("grads_all_gather", [_pack([grads[n] for n in REPLICATED])])
    rep_out = _adam_call("adamw_replicated", rep_all, _pack([local_w[n] for n in REPLICATED]),
                         _pack([local_m[n] for n in REPLICATED]), _pack([local_v[n] for n in REPLICATED]))
    for k, (small_packed, rep_packed) in enumerate(zip(small_out, rep_out)):
        for n, t in zip(small, _unpack(small_packed, small_shapes)):
            by_name.setdefault(n, [None] * 4)[k] = t
        for n, t in zip(REPLICATED, _unpack(rep_packed, rep_shapes)):
            by_name.setdefault(n, [None] * 4)[k] = t
    results = [by_name[n][k] for k in range(4) for n in WEIGHTS]
    return (loss, dx[None], *results)
```

```python
import functools
import math

import jax
import jax.numpy as jnp
from jax import lax
from jax.experimental import pallas as pl
from jax.experimental.pallas import tpu as pltpu

F32 = jnp.float32
BF16 = jnp.bfloat16
HI = lax.Precision.HIGHEST

D_MODEL = 1024
HEAD = 64
PAIR = 2 * HEAD
N_PAIR = D_MODEL // PAIR
CHUNK = 64
GN_EPS = 64e-5
LN_EPS = 1e-5
SSM_GROUP = 16
SSM_STATE = 64
SSM_GROUPS = D_MODEL // SSM_GROUP
SSM_LANES = SSM_GROUPS * SSM_STATE
SSM_BLK_IN = 128
SSM_BLK_ST = 512
N_SSM_BLK = D_MODEL // SSM_BLK_IN
DEPTH = 2
DN_ALPHA = (2.0 * DEPTH) ** 0.25
ADAM_LR, ADAM_B1, ADAM_B2, ADAM_EPS, ADAM_WD, ADAM_STEP = 0.001, 0.9, 0.999, 1e-08, 0.01, 10
N_DEV = 8
MESH_AXES = ("x", "y", "c")
VMEM_LIMIT = 56 * 1024 * 1024
TOKEN_TILE = 256
PACK_W = 1024

SHARDED = {
    "rw_mu": 2, "rw_w1": 1, "rw_w2": 2, "rw_a1": 1, "rw_a2": 2, "rw_g1": 1, "rw_g2": 2,
    "rw_wr": 1, "rw_wk": 1, "rw_wv": 1, "rw_wo": 1, "s5_d": 1, "s5_w_glu": 2, "mlp_w1": 2, "mlp_w2": 1,
}
WEIGHTS = ['ln_g', 'ln_b', 'rw_mu', 'rw_w0', 'rw_w1', 'rw_w2', 'rw_a0', 'rw_a1', 'rw_a2', 'rw_g1', 'rw_g2', 'rw_k_k',
           'rw_k_a', 'rw_r_k', 'rw_wr', 'rw_wk', 'rw_wv', 'rw_wo', 'rw_lnx_g', 'rw_lnx_b', 's5_a_re', 's5_a_im',
           's5_log_dt', 's5_b_re', 's5_b_im', 's5_c_re', 's5_c_im', 's5_d', 's5_w_glu', 'mlp_w1', 'mlp_w2']
REPLICATED = [n for n in WEIGHTS if n not in SHARDED]
BIG = ['rw_wr', 'rw_wk', 'rw_wv', 'rw_wo', 's5_w_glu', 'mlp_w1', 'mlp_w2']


def _cparams(n_grid):
    return pltpu.CompilerParams(dimension_semantics=("arbitrary",) * n_grid, vmem_limit_bytes=VMEM_LIMIT)


@jax.custom_vjp
def _mm(x, w):
    return jnp.dot(x.astype(BF16), w.astype(BF16), preferred_element_type=F32)


def _mm_fwd(x, w):
    return _mm(x, w), (x, w)


def _mm_bwd(res, dy):
    x, w = res
    dyb = dy.astype(BF16)
    dx = lax.dot_general(dyb, w.astype(BF16), (((1,), (1,)), ((), ())), preferred_element_type=F32)
    dw = lax.dot_general(x.astype(BF16), dyb, (((0,), (0,)), ((), ())), preferred_element_type=F32)
    return dx, dw


_mm.defvjp(_mm_fwd, _mm_bwd)


def _hdot(a, b):
    return jnp.dot(a, b, precision=HI, preferred_element_type=F32)


def _hdot_nt(a, b):
    return lax.dot_general(a, b, (((1,), (1,)), ((), ())), precision=HI, preferred_element_type=F32)


def _hdot_tn(a, b):
    return lax.dot_general(a, b, (((0,), (0,)), ((), ())), precision=HI, preferred_element_type=F32)


_DOT_DIMS = {'nn': (((1,), (0,)), ((), ())), 'nt': (((1,), (1,)), ((), ())), 'tn': (((0,), (0,)), ((), ()))}
_BATCH_DOT_DIMS = {'nn': (((2,), (1,)), ((0,), (0,))), 'nt': (((2,), (2,)), ((0,), (0,))), 'tn': (((1,), (1,)), ((0,), (0,)))}
CHUNK_PAIRS = 8
CHUNK_PASSES = 1


def _split_bf16(a):
    hi = a.astype(BF16)
    return hi, (a - hi.astype(F32)).astype(BF16)


def _dot_passes(a, b, mode, passes=None):
    dims = (_DOT_DIMS if a.ndim == 2 else _BATCH_DOT_DIMS)[mode]
    dot = lambda p, q: lax.dot_general(p, q, dims, preferred_element_type=F32)
    if (CHUNK_PASSES if passes is None else passes) == 1:
        return dot(a.astype(BF16), b.astype(BF16))
    (ah, al), (bh, bl) = _split_bf16(a), _split_bf16(b)
    return dot(ah, bh) + (dot(ah, bl) + dot(al, bh))


@functools.partial(jax.custom_vjp, nondiff_argnums=(2,))
def _pdot(a, b, mode):
    return _dot_passes(a, b, mode)


def _pdot_fwd(a, b, mode):
    return _dot_passes(a, b, mode), (a, b)


def _pdot_bwd(mode, res, dy):
    a, b = res
    if mode == 'nn':
        return _dot_passes(dy, b, 'nt'), _dot_passes(a, dy, 'tn')
    if mode == 'nt':
        return _dot_passes(dy, b, 'nn'), _dot_passes(dy, a, 'tn')
    return _dot_passes(b, dy, 'nt'), _dot_passes(a, dy, 'nn')


_pdot.defvjp(_pdot_fwd, _pdot_bwd)


def _tri_sum(x, upper):
    nb, n = x.shape[0], x.shape[1]
    ti = lax.broadcasted_iota(jnp.int32, (nb, n, n), 1)
    tj = lax.broadcasted_iota(jnp.int32, (nb, n, n), 2)
    tri = ((ti <= tj) if upper else (ti >= tj)).astype(BF16)
    hi = x.astype(BF16)
    rest = x - hi.astype(F32)
    mid = rest.astype(BF16)
    lo = (rest - mid.astype(F32)).astype(BF16)
    dot = lambda q: lax.dot_general(tri, q, _BATCH_DOT_DIMS['nn'], preferred_element_type=F32)
    return dot(hi) + (dot(mid) + dot(lo))


@jax.custom_vjp
def _cumsum_rows(x):
    return _tri_sum(x, False)


_cumsum_rows.defvjp(lambda x: (_tri_sum(x, False), None), lambda _, dy: (_tri_sum(dy, True),))


@functools.partial(jax.custom_vjp, nondiff_argnums=(1,))
def _roll_rows(x, shift):
    return pltpu.roll(x, shift, 0)


def _roll_rows_fwd(x, shift):
    return pltpu.roll(x, shift, 0), None


def _roll_rows_bwd(shift, _, dy):
    n = dy.shape[0]
    return (pltpu.roll(dy, (n - shift) % n, 0),)


_roll_rows.defvjp(_roll_rows_fwd, _roll_rows_bwd)


def _shift_down(x, first_row):
    row = lax.broadcasted_iota(jnp.int32, x.shape, 0)
    return jnp.where(row == 0, first_row, _roll_rows(x, 1))


def _sigmoid(x):
    return 1.0 / (1.0 + jnp.exp(-x))


def _softplus(x):
    return jnp.maximum(x, 0.0) + jnp.log(1.0 + jnp.exp(-jnp.abs(x)))


def _gelu(x):
    return 0.5 * x * (1.0 + jnp.tanh(math.sqrt(2.0 / math.pi) * (x + 0.044715 * (x * x * x))))


def _layer_norm(x, g, b):
    mu = jnp.mean(x, axis=-1, keepdims=True)
    xc = x - mu
    var = jnp.mean(xc * xc, axis=-1, keepdims=True)
    return xc * lax.rsqrt(var + LN_EPS) * g + b


def _spec(block, index_map):
    return pl.BlockSpec(block, index_map)


def _tile(arr, tm):
    return (arr, (tm, arr.shape[1]), lambda i: (i, 0))


def _full(arr):
    nd = arr.ndim
    return (arr, arr.shape, lambda *_: (0,) * nd)


def _fwd_call(name, fn, grid, prims, outs):
    n_in = len(prims)

    def body(*refs):
        vals = [r[...] for r in refs[:n_in]]
        vals = [v.astype(F32) if v.dtype != F32 else v for v in vals]
        res = fn(*vals)
        for o, r in zip(refs[n_in:], res):
            o[...] = r.astype(o.dtype)

    return pl.pallas_call(
        body, name=name, grid=grid,
        in_specs=[_spec(b, m) for _, b, m in prims],
        out_specs=[_spec(b, m) for _, _, b, m in outs],
        out_shape=[jax.ShapeDtypeStruct(s, d) for s, d, _, _ in outs],
        compiler_params=_cparams(len(grid)),
    )(*[a for a, _, _ in prims])


def _bwd_call(name, fn, grid, prims, cts, wants, addends=None):
    addends = addends or {}
    n_in, n_ct = len(prims), len(cts)
    out_place = {}
    wants, out_dtype = list(wants), {}
    for i, w in enumerate(wants):
        if isinstance(w, tuple):
            wants[i], out_place[i] = w
        elif w == 'tile_bf16':
            wants[i], out_dtype[i] = 'tile', BF16
    diff = [i for i, w in enumerate(wants) if w]
    add_idx = [i for i in diff if i in addends]
    last_axis = len(grid) - 1

    def body(*refs):
        vals = [r[...] for r in refs[:n_in]]
        vals = [v.astype(F32) if v.dtype != F32 else v for v in vals]
        ct_vals = [r[...] for r in refs[n_in:n_in + n_ct]]
        ct_vals = [v.astype(F32) if v.dtype != F32 else v for v in ct_vals]
        add_refs = dict(zip(add_idx, refs[n_in + n_ct:n_in + n_ct + len(add_idx)]))
        out_refs = refs[n_in + n_ct + len(add_idx):]

        def f(*dargs):
            full = list(vals)
            for i, a in zip(diff, dargs):
                full[i] = a
            return tuple(fn(*full))

        _, vjp = jax.vjp(f, *[vals[i] for i in diff])
        grads = vjp(tuple(ct_vals))
        first = pl.program_id(last_axis) == 0
        for o, g, i in zip(out_refs, grads, diff):
            if wants[i] == 'tile':
                if i in add_refs:
                    g = g + add_refs[i][...]
                o[...] = g.astype(o.dtype)
            else:
                @pl.when(first)
                def _(o=o):
                    o[...] = jnp.zeros(o.shape, o.dtype)
                o[...] += g

    ins = list(prims) + list(cts) + [addends[i] for i in add_idx]
    places = [out_place.get(i, (prims[i][0].shape, prims[i][1], prims[i][2])) for i in diff]
    return pl.pallas_call(
        body, name=name, grid=grid,
        in_specs=[_spec(b, m) for _, b, m in ins],
        out_specs=[_spec(b, m) for _, b, m in places],
        out_shape=[jax.ShapeDtypeStruct(s, out_dtype.get(i, F32)) for i, (s, _, _) in zip(diff, places)],
        compiler_params=_cparams(len(grid)),
    )(*[a for a, _, _ in ins])


def _matmul(name, a, b, mode, *, out_dtype=F32, lhs_fn=None, epi=None, extra=None, tm=1024, tn=1024, tk=1024,
            b_cols=None, b_rows=None, out_cols=False):
    if mode == 'tn':
        kdim, m = a.shape
    else:
        m, kdim = a.shape
    if b_rows is not None:
        loc, cols = b.shape[-2], b.shape[-1]
        if mode == 'nn':
            n, tk = cols, loc
        else:
            n, tn = N_DEV * loc, loc
    elif b_cols is None:
        n = b.shape[0] if mode == 'nt' else b.shape[1]
    else:
        rows, loc = b.shape[-2], b.shape[-1]
        if mode == 'nn':
            n, tn = N_DEV * loc, loc
        else:
            n, tk = rows, loc
    if out_cols:
        tn = n // N_DEV
    tm, tn, tk = min(tm, m), min(tn, n), min(tk, kdim)
    nk = kdim // tk
    grid = (m // tm, n // tn, nk)
    a_spec = _spec((tk, tm), lambda i, j, k: (k, i)) if mode == 'tn' else _spec((tm, tk), lambda i, j, k: (i, k))
    if b_rows is not None:
        lead = tuple(b_rows)
        skip = (None,) * (1 + len(lead))
        if mode == 'nn':
            b_spec = _spec(skip + (tk, tn), lambda i, j, k: (k,) + lead + (0, j))
        else:
            b_spec = _spec(skip + (tn, tk), lambda i, j, k: (j,) + lead + (0, k))
    elif b_cols is None:
        b_spec = _spec((tn, tk), lambda i, j, k: (j, k)) if mode == 'nt' else _spec((tk, tn), lambda i, j, k: (k, j))
    else:
        lead = tuple(b_cols)
        skip = (None,) * (1 + len(lead))
        if mode == 'nn':
            b_spec = _spec(skip + (tk, tn), lambda i, j, k: (j,) + lead + (k, 0))
        else:
            b_spec = _spec(skip + (tn, tk), lambda i, j, k: (k,) + lead + (j, 0))
    o_spec = _spec((tm, tn), lambda i, j, k: (i, j))
    if out_cols:
        o_place, o_shape = _spec((None, tm, tn), lambda i, j, k: (j, i, 0)), (N_DEV, m, tn)
    else:
        o_place, o_shape = o_spec, (m, n)
    dims = _DOT_DIMS[mode]

    def body(*refs):
        if extra is None:
            a_ref, b_ref, o_ref, acc = refs
            x_ref = None
        else:
            a_ref, b_ref, x_ref, o_ref, acc = refs
        k = pl.program_id(2)
        av = a_ref[...]
        if lhs_fn is not None:
            av = lhs_fn(av.astype(F32))
        part = lax.dot_general(av.astype(BF16), b_ref[...].astype(BF16), dims, preferred_element_type=F32)

        def finish(r):
            if epi is not None:
                r = epi(r, x_ref[...])
            o_ref[...] = r.astype(o_ref.dtype)

        if nk == 1:
            finish(part)
            return

        @pl.when(k == 0)
        def _():
            acc[...] = part

        @pl.when((k > 0) & (k < nk - 1))
        def _():
            acc[...] += part

        @pl.when(k == nk - 1)
        def _():
            finish(acc[...] + part)

    ops, specs = [a, b], [a_spec, b_spec]
    if extra is not None:
        ops.append(extra)
        specs.append(o_spec)
    return pl.pallas_call(
        body, name=name, grid=grid, in_specs=specs, out_specs=o_place,
        out_shape=jax.ShapeDtypeStruct(o_shape, out_dtype),
        scratch_shapes=[pltpu.VMEM((tm, tn), F32)],
        compiler_params=_cparams(3),
    )(*ops)


def _relu2(x):
    r = jnp.maximum(x, 0.0)
    return r * r


def _mix_fn(x, halo, mu, w0, w1, w2, a0, a1, a2, g1, g2):
    prev_row = jnp.where(pl.program_id(0) == 0, 0.0, halo[7:8, :])
    xx = _shift_down(x, prev_row) - x
    xr, xw, xk, xv, xa, xg = (x + xx * mu[i:i + 1, :] for i in range(6))
    w_pre = w0 + _mm(jnp.tanh(_mm(xw, w1)), w2)
    log_decay = -jnp.exp(-_softplus(-w_pre) - 0.5)
    a = _sigmoid(a0 + _mm(_mm(xa, a1), a2))
    g = _mm(_sigmoid(_mm(xg, g1)), g2)
    return xr, xk, xv, log_decay, a, g


def _chunk_fn(r, lw, k, v, a, k_k, k_a, r_k):
    n, n_pair = r.shape[0], r.shape[1] // PAIR
    to_batch = lambda t: jnp.concatenate([t[None, :, p * PAIR:(p + 1) * PAIR] for p in range(n_pair)], axis=0)
    from_batch = lambda t: jnp.concatenate([t[p] for p in range(n_pair)], axis=1)
    r, lw, k, v, a, k_k, k_a, r_k = (to_batch(t) for t in (r, lw, k, v, a, k_k, k_a, r_k))
    lane = lax.broadcasted_iota(jnp.int32, (1, 1, PAIR), 2)
    m0 = lane < HEAD

    def head_sum(t):
        s0 = jnp.sum(jnp.where(m0, t, 0.0), axis=2, keepdims=True)
        s1 = jnp.sum(jnp.where(m0, 0.0, t), axis=2, keepdims=True)
        return jnp.where(m0, s0, s1)

    kk = k * k_k
    kk = kk / jnp.maximum(jnp.sqrt(head_sum(kk * kk)), 1e-12)
    k2 = k * (1.0 + (a - 1.0) * k_a)
    aa, bb = -kk, kk * a
    bonus = head_sum(r * k2 * r_k) * v

    cum = _cumsum_rows(lw)
    p_in, p_ex, p_inv = jnp.exp(cum), jnp.exp(cum - lw), jnp.exp(-cum)
    at, rt, kt, bt = aa * p_ex, r * p_in, k2 * p_inv, bb * p_inv

    def stack_masked(t):
        return jnp.concatenate([jnp.where(m0, t, 0.0), jnp.where(m0, 0.0, t)], axis=1)

    def unstack_sum(t):
        return t[:, :n] + t[:, n:]

    def unstack_select(t):
        return jnp.where(m0, t[:, :n], t[:, n:])

    ti = lax.broadcasted_iota(jnp.int32, (1, 2 * n, 2 * n), 1)
    tj = lax.broadcasted_iota(jnp.int32, (1, 2 * n, 2 * n), 2)
    same = (ti >= n) == (tj >= n)
    incl, strict = same & (ti >= tj), same & (ti > tj)
    eye_n = (ti == tj).astype(F32)
    at_s, rt_s, kt_s, bt_s = stack_masked(at), stack_masked(rt), stack_masked(kt), stack_masked(bt)
    v_s = jnp.concatenate([v, v], axis=1)
    a_ab = jnp.where(strict, _pdot(at_s, bt_s, 'nt'), 0.0)
    a_ak = jnp.where(strict, _pdot(at_s, kt_s, 'nt'), 0.0)
    a_rb = jnp.where(incl, _pdot(rt_s, bt_s, 'nt'), 0.0)
    a_rk = jnp.where(incl, _pdot(rt_s, kt_s, 'nt'), 0.0)
    inv, pw = eye_n + a_ab, a_ab
    for _ in range(int(math.log2(n)) - 1):
        pw = _pdot(pw, pw, 'nn')
        inv = inv + _pdot(inv, pw, 'nn')
    w_s = _pdot(inv, at_s, 'nn')
    y_s = _pdot(inv, _pdot(a_ak, v_s, 'nn'), 'nn')
    q_all = unstack_sum(rt_s + _pdot(a_rb, w_s, 'nn'))
    oi_all = unstack_select(_pdot(a_rb, y_s, 'nn') + _pdot(a_rk, v_s, 'nn'))
    y_m = stack_masked(unstack_select(y_s))

    mm = _pdot(bt_s, w_s, 'tn')
    nn = _pdot(bt_s, y_m, 'tn') + _pdot(kt_s, stack_masked(v), 'tn')
    ei = lax.broadcasted_iota(jnp.int32, (1, PAIR, PAIR), 1)
    ej = lax.broadcasted_iota(jnp.int32, (1, PAIR, PAIR), 2)
    eye_p = (ei == ej).astype(F32)
    decay_col = jnp.sum(eye_p * p_in[:, n - 1:n, :], axis=2, keepdims=True)
    return ((decay_col * (eye_p + mm))[None], (decay_col * nn)[None], from_batch(q_all), from_batch(oi_all),
            from_batch(bonus))


def _gate_fn(o, bonus, g, lnx_g, lnx_b):
    lane = lax.broadcasted_iota(jnp.int32, (1, PAIR), 1)
    m0 = lane < HEAD

    def head_mean(t):
        s0 = jnp.sum(jnp.where(m0, t, 0.0), axis=1, keepdims=True)
        s1 = jnp.sum(jnp.where(m0, 0.0, t), axis=1, keepdims=True)
        return jnp.where(m0, s0, s1) * (1.0 / HEAD)

    outs = []
    for p in range(N_PAIR):
        sl = slice(p * PAIR, (p + 1) * PAIR)
        oc = o[:, sl] - head_mean(o[:, sl])
        on = oc * lax.rsqrt(head_mean(oc * oc) + GN_EPS) * lnx_g[:, sl] + lnx_b[:, sl]
        outs.append((on + bonus[:, sl]) * g[:, sl])
    return (jnp.concatenate(outs, axis=1),)


def _res_ln_fn(h, y, g, b):
    return (_layer_norm(DN_ALPHA * h + y, g, b),)


def _glu_ln_fn(h, z, g, b):
    mix = z[:, :D_MODEL] * _sigmoid(z[:, D_MODEL:])
    return (_layer_norm(DN_ALPHA * h + mix, g, b),)


def _s5_param_fn(a_re, a_im, log_dt, b_re, b_im):
    dt = jnp.exp(log_dt)
    lam_re = jnp.minimum(a_re, -1e-4)
    lam_im = a_im
    mag = jnp.exp(dt * lam_re)
    abar_re = mag * jnp.cos(dt * lam_im)
    abar_im = mag * jnp.sin(dt * lam_im)
    den = lam_re * lam_re + lam_im * lam_im
    nr, ni = abar_re - 1.0, abar_im
    coef_re = (nr * lam_re + ni * lam_im) / den
    coef_im = (ni * lam_re - nr * lam_im) / den
    return abar_re, abar_im, coef_re, coef_im


def _s5_bbar_fn(coef_re, coef_im, b_re, b_im):
    return coef_re * b_re - coef_im * b_im, coef_re * b_im + coef_im * b_re


def _s5_in_fn(u, b_re, b_im):
    return _pdot(u, b_re[0], 'nn'), _pdot(u, b_im[0], 'nn')


def _s5_out_fn(s_re, s_im, u, c_re, c_im, d):
    y = _pdot(s_re, c_re[0], 'nn') - _pdot(s_im, c_im[0], 'nn') + u * d
    return (_gelu(y),)


def _s5_step_fn(a_re, a_im, s_re, s_im, halo_re, halo_im):
    first = pl.program_id(1) == 0
    p_re = _shift_down(s_re, jnp.where(first, 0.0, halo_re[7:8, :]))
    p_im = _shift_down(s_im, jnp.where(first, 0.0, halo_im[7:8, :]))
    return a_re * p_re - a_im * p_im, a_re * p_im + a_im * p_re


def _rwkv_state_fwd(m_all, n_all, q, oi):
    n_chunk = m_all.shape[0]

    def body(m_ref, n_ref, q_ref, oi_ref, o_ref, zs_ref, z):
        @pl.when(pl.program_id(0) == 0)
        def _():
            z[...] = jnp.zeros(z.shape, F32)

        for j in range(N_PAIR):
            sl = slice(j * PAIR, (j + 1) * PAIR)
            zj = z[j]
            zs_ref[0, j] = zj
            o_ref[:, sl] = _dot_passes(q_ref[:, sl], zj, 'nn', 3) + oi_ref[:, sl]
            z[j] = _dot_passes(m_ref[0, j], zj, 'nn', 3) + n_ref[0, j]

    st = _spec((1, N_PAIR, PAIR, PAIR), lambda c: (c, 0, 0, 0))
    tok = _spec((CHUNK, D_MODEL), lambda c: (c, 0))
    return pl.pallas_call(
        body, name="rwkv_state_fwd", grid=(n_chunk,),
        in_specs=[st, st, tok, tok], out_specs=[tok, st],
        out_shape=[jax.ShapeDtypeStruct(q.shape, F32), jax.ShapeDtypeStruct(m_all.shape, F32)],
        scratch_shapes=[pltpu.VMEM((N_PAIR, PAIR, PAIR), F32)],
        compiler_params=_cparams(1),
    )(m_all, n_all, q, oi)


def _rwkv_state_bwd(m_all, q, zs, d_o):
    n_chunk = m_all.shape[0]

    def body(m_ref, q_ref, zs_ref, do_ref, dq_ref, dm_ref, dn_ref, g):
        @pl.when(pl.program_id(0) == 0)
        def _():
            g[...] = jnp.zeros(g.shape, F32)

        bi = lax.broadcasted_iota(jnp.int32, (PAIR, PAIR), 0) // HEAD
        bj = lax.broadcasted_iota(jnp.int32, (PAIR, PAIR), 1) // HEAD
        same_head = bi == bj
        for j in range(N_PAIR):
            sl = slice(j * PAIR, (j + 1) * PAIR)
            gj, zj, do = g[j], zs_ref[0, j], do_ref[:, sl]
            dn_ref[0, j] = gj
            dm_ref[0, j] = _dot_passes(gj, zj, 'nt', 3)
            dq_ref[:, sl] = _dot_passes(do, zj, 'nt', 3)
            g[j] = _dot_passes(m_ref[0, j], gj, 'tn', 3) + jnp.where(same_head, _dot_passes(q_ref[:, sl], do, 'tn', 3), 0.0)

    st = _spec((1, N_PAIR, PAIR, PAIR), lambda c: (n_chunk - 1 - c, 0, 0, 0))
    tok = _spec((CHUNK, D_MODEL), lambda c: (n_chunk - 1 - c, 0))
    return pl.pallas_call(
        body, name="rwkv_state_bwd", grid=(n_chunk,),
        in_specs=[st, tok, st, tok], out_specs=[tok, st, st],
        out_shape=[jax.ShapeDtypeStruct(q.shape, F32), jax.ShapeDtypeStruct(m_all.shape, F32),
                   jax.ShapeDtypeStruct(m_all.shape, F32)],
        scratch_shapes=[pltpu.VMEM((N_PAIR, PAIR, PAIR), F32)],
        compiler_params=_cparams(1),
    )(m_all, q, zs, d_o)


def _s5_scan(name, b_re, b_im, a_re, a_im, reverse):
    t_len, lanes = b_re.shape
    tt, lb, grp = 512, SSM_BLK_ST, 8
    tt = min(tt, t_len)
    n_t, n_grp = t_len // tt, tt // grp

    def cmul(xr, xi, yr, yi):
        return xr * yr - xi * yi, xr * yi + xi * yr

    def body(bre_ref, bim_ref, are_ref, aim_ref, sre_ref, sim_ref, carry):
        @pl.when(pl.program_id(1) == 0)
        def _():
            carry[...] = jnp.zeros(carry.shape, F32)

        a1 = (are_ref[...], aim_ref[...])
        a2 = cmul(*a1, *a1)
        a4 = cmul(*a2, *a2)
        a8 = cmul(*a4, *a4)
        row = lax.broadcasted_iota(jnp.int32, (grp, lb), 0)
        expo = (grp - row) if reverse else (row + 1)
        pw = (jnp.ones((grp, lb), F32), jnp.zeros((grp, lb), F32))
        for bit, ap in ((1, a1), (2, a2), (4, a4), (8, a8)):
            nxt = cmul(*pw, *ap)
            sel = (expo & bit) != 0
            pw = (jnp.where(sel, nxt[0], pw[0]), jnp.where(sel, nxt[1], pw[1]))

        def group(i, c):
            gi = (n_grp - 1 - i) if reverse else i
            rows = pl.ds(pl.multiple_of(gi * grp, grp), grp)
            xr, xi = bre_ref[rows, :], bim_ref[rows, :]
            for d, ad in ((1, a1), (2, a2), (4, a4)):
                if reverse:
                    keep = row < grp - d
                    sr, si = pltpu.roll(xr, grp - d, 0), pltpu.roll(xi, grp - d, 0)
                else:
                    keep = row >= d
                    sr, si = pltpu.roll(xr, d, 0), pltpu.roll(xi, d, 0)
                pr, pi = cmul(*ad, jnp.where(keep, sr, 0.0), jnp.where(keep, si, 0.0))
                xr, xi = xr + pr, xi + pi
            cr, ci = cmul(*pw, c[0], c[1])
            xr, xi = xr + cr, xi + ci
            sre_ref[rows, :] = xr
            sim_ref[rows, :] = xi
            edge = slice(0, 1) if reverse else slice(grp - 1, grp)
            return xr[edge, :], xi[edge, :]

        c = lax.fori_loop(0, n_grp, group, (carry[0:1, :], carry[1:2, :]))
        carry[0:1, :] = c[0]
        carry[1:2, :] = c[1]

    tmap = (lambda l, t: (n_t - 1 - t, l)) if reverse else (lambda l, t: (t, l))
    blk = _spec((tt, lb), tmap)
    par = _spec((1, lb), lambda l, t: (0, l))
    return pl.pallas_call(
        body, name=name, grid=(lanes // lb, n_t),
        in_specs=[blk, blk, par, par], out_specs=[blk, blk],
        out_shape=[jax.ShapeDtypeStruct(b_re.shape, F32)] * 2,
        scratch_shapes=[pltpu.VMEM((2, lb), F32)],
        compiler_params=_cparams(2),
    )(b_re, b_im, a_re, a_im)


def _loss_call(h, target, tm):
    n_tok, d = h.shape

    def body(h_ref, t_ref, acc_ref, dh_ref):
        @pl.when(pl.program_id(0) == 0)
        def _():
            acc_ref[...] = jnp.zeros(acc_ref.shape, F32)

        e = h_ref[...] - t_ref[...]
        dh_ref[...] = e * (1.0 / d)
        acc_ref[...] += jnp.sum(jnp.sum(e * e, axis=1, keepdims=True), axis=0, keepdims=True)

    tok = _spec((tm, d), lambda i: (i, 0))
    return pl.pallas_call(
        body, name="loss", grid=(n_tok // tm,),
        in_specs=[tok, tok], out_specs=[_spec((8, 128), lambda i: (0, 0)), tok],
        out_shape=[jax.ShapeDtypeStruct((8, 128), F32), jax.ShapeDtypeStruct(h.shape, F32)],
        compiler_params=_cparams(1),
    )(h, target)


def _block_diag(t):
    nb, ng, rows, cols = t.shape
    eye = jnp.eye(ng, dtype=t.dtype)
    return jnp.einsum('bgrc,gh->bgrhc', t, eye).reshape(nb, ng * rows, ng * cols)


def _block_diag_t(t, rows, cols):
    nb, ng = t.shape[0], t.shape[1] // rows
    t5 = t.reshape(nb, ng, rows, ng, cols)
    return jnp.einsum('bgrhc,gh->bgrc', t5, jnp.eye(ng, dtype=t.dtype))


def _mlp_fwd(layer, h, w1g, w2g):
    pre = _matmul(f"mlp{layer}_up", h, w1g, 'nn', b_cols=(layer,))
    return pre, _matmul(f"mlp{layer}_down", pre, w2g, 'nn', lhs_fn=_relu2, b_rows=(layer,))


def _mlp_bwd(layer, h, pre, dy, dh_res, w1g, w2g):
    dpre = _matmul(f"mlp{layer}_down_dx", dy, w2g, 'nt', epi=lambda acc, p: acc * (2.0 * jnp.maximum(p, 0.0)),
                   extra=pre, b_rows=(layer,))
    dw2 = _matmul(f"mlp{layer}_down_dw", pre, dy, 'tn', lhs_fn=_relu2)
    dw1 = _matmul(f"mlp{layer}_up_dw", h, dpre, 'tn', out_cols=True)
    dh = _matmul(f"mlp{layer}_up_dx", dpre, w1g, 'nt', epi=lambda acc, e: acc + e, extra=dh_res, b_cols=(layer,))
    return dh, dw1, dw2


def _local_step(x, target, w, wg):
    n_tok = x.shape[0]
    tm = min(TOKEN_TILE, n_tok)
    n_tile, n_chunk = n_tok // tm, n_tok // CHUNK
    row = lambda v: v.reshape(1, -1)
    big = {n: wg[n].reshape(D_MODEL, D_MODEL) for n in ('rw_wr', 'rw_wk', 'rw_wv', 'rw_wo')}
    w1g, w2g, glu_g = wg['mlp_w1'], wg['mlp_w2'], wg['s5_w_glu']
    ln_g, ln_b = w['ln_g'], w['ln_b']
    grads, big_grads = {}, {}
    to_slots = lambda t: t.reshape(N_DEV, t.shape[0] // N_DEV, t.shape[1])

    halo_spec = (8, D_MODEL), lambda i: (jnp.maximum(i * (tm // 8) - 1, 0), 0)
    mix_p = [_tile(x, tm), (x,) + halo_spec, _full(w['rw_mu'][0]), _full(row(w['rw_w0'])), _full(w['rw_w1'][0]),
             _full(w['rw_w2'][0]), _full(row(w['rw_a0'])), _full(w['rw_a1'][0]), _full(w['rw_a2'][0]),
             _full(w['rw_g1'][0]), _full(w['rw_g2'][0])]
    tok_out = lambda dt: ((n_tok, D_MODEL), dt, (tm, D_MODEL), lambda i: (i, 0))
    xr, xk, xv, lw, a, g = _fwd_call("rwkv_mix", _mix_fn, (n_tile,), mix_p,
                                     [tok_out(BF16), tok_out(BF16), tok_out(BF16), tok_out(F32), tok_out(F32), tok_out(F32)])
    r = _matmul("rwkv_r", xr, big['rw_wr'], 'nn')
    k = _matmul("rwkv_k", xk, big['rw_wk'], 'nn')
    v = _matmul("rwkv_v", xv, big['rw_wv'], 'nn')

    slab_w = CHUNK_PAIRS * PAIR
    slab = lambda arr: (arr, (CHUNK, slab_w), lambda j, c: (c, j))
    pslab = lambda arr: (arr, (1, slab_w), lambda j, c: (0, j))
    chunk_p = [slab(r), slab(lw), slab(k), slab(v), slab(a), pslab(row(w['rw_k_k'])), pslab(row(w['rw_k_a'])),
               pslab(w['rw_r_k'].reshape(1, -1))]
    st_shape = (n_chunk, N_PAIR, PAIR, PAIR)
    st_out = (st_shape, F32, (1, CHUNK_PAIRS, PAIR, PAIR), lambda j, c: (c, j, 0, 0))
    sl_out = ((n_tok, D_MODEL), F32, (CHUNK, slab_w), lambda j, c: (c, j))
    chunk_grid = (N_PAIR // CHUNK_PAIRS, n_chunk)
    m_all, n_all, q, oi, bonus = _fwd_call("rwkv_chunk", _chunk_fn, chunk_grid, chunk_p,
                                           [st_out, st_out, sl_out, sl_out, sl_out])
    o, zs = _rwkv_state_fwd(m_all, n_all, q, oi)

    gate_p = [_tile(o, tm), _tile(bonus, tm), _tile(g, tm), _full(row(w['rw_lnx_g'])), _full(row(w['rw_lnx_b']))]
    (og,) = _fwd_call("rwkv_gate", _gate_fn, (n_tile,), gate_p, [tok_out(BF16)])
    y0 = _matmul("rwkv_o", og, big['rw_wo'], 'nn')
    ln0_p = [_tile(x, tm), _tile(y0, tm), _full(ln_g[0:1]), _full(ln_b[0:1])]
    (h1,) = _fwd_call("ln0", _res_ln_fn, (n_tile,), ln0_p, [tok_out(F32)])
    pre0, y1 = _mlp_fwd(0, h1, w1g, w2g)
    ln1_p = [_tile(h1, tm), _tile(y1, tm), _full(ln_g[1:2]), _full(ln_b[1:2])]
    (h2,) = _fwd_call("ln1", _res_ln_fn, (n_tile,), ln1_p, [tok_out(F32)])

    gp = (SSM_GROUPS, SSM_STATE)
    par_p = [_full(w['s5_a_re'][0]), _full(w['s5_a_im'][0]), _full(w['s5_log_dt'].reshape(SSM_GROUPS, 1))]
    s5_par_fn = lambda ar, ai, ld: _s5_param_fn(ar, ai, ld, None, None)
    gp_out = (gp, F32, gp, lambda i: (0, 0))
    abar_re, abar_im, coef_re, coef_im = _fwd_call("s5_param", s5_par_fn, (1,), par_p, [gp_out] * 4)
    b_flat = [w[n][0].reshape(SSM_LANES, SSM_GROUP) for n in ('s5_b_re', 's5_b_im')]
    bbar_p = [_full(coef_re.reshape(SSM_LANES, 1)), _full(coef_im.reshape(SSM_LANES, 1)), _full(b_flat[0]), _full(b_flat[1])]
    bb_out = ((SSM_LANES, SSM_GROUP), F32, (SSM_LANES, SSM_GROUP), lambda i: (0, 0))
    bbar_re, bbar_im = _fwd_call("s5_bbar", _s5_bbar_fn, (1,), bbar_p, [bb_out] * 2)
    to_in = lambda t: _block_diag(t.reshape(N_SSM_BLK, 8, SSM_STATE, SSM_GROUP).transpose(0, 1, 3, 2))
    to_out = lambda t: _block_diag(t.reshape(N_SSM_BLK, 8, SSM_GROUP, SSM_STATE).transpose(0, 1, 3, 2))
    bblk_re, bblk_im = to_in(bbar_re), to_in(bbar_im)
    cblk_re, cblk_im = to_out(w['s5_c_re'][0]), to_out(w['s5_c_im'][0])
    a_row_re, a_row_im = abar_re.reshape(1, SSM_LANES), abar_im.reshape(1, SSM_LANES)

    u_slab = lambda arr: (arr, (tm, SSM_BLK_IN), lambda l, i: (i, l))
    s_slab = lambda arr: (arr, (tm, SSM_BLK_ST), lambda l, i: (i, l))
    blk3 = lambda arr: (arr, (1,) + arr.shape[1:], lambda l, i: (l, 0, 0))
    in_p = [u_slab(h2), blk3(bblk_re), blk3(bblk_im)]
    st_tok = ((n_tok, SSM_LANES), F32, (tm, SSM_BLK_ST), lambda l, i: (i, l))
    bu_re, bu_im = _fwd_call("s5_in", _s5_in_fn, (N_SSM_BLK, n_tile), in_p, [st_tok, st_tok])
    s_re, s_im = _s5_scan("s5_scan_fwd", bu_re, bu_im, a_row_re, a_row_im, False)
    d_row = row(w['s5_d'])
    out_p = [s_slab(s_re), s_slab(s_im), u_slab(h2), blk3(cblk_re), blk3(cblk_im),
             (d_row, (1, SSM_BLK_IN), lambda l, i: (0, l))]
    u_tok = lambda dt: ((n_tok, D_MODEL), dt, (tm, SSM_BLK_IN), lambda l, i: (i, l))
    (yg,) = _fwd_call("s5_out", _s5_out_fn, (N_SSM_BLK, n_tile), out_p, [u_tok(BF16)])
    z = _matmul("s5_glu", yg, glu_g, 'nn', b_cols=(0,))
    ln2_p = [_tile(h2, tm), _tile(z, tm), _full(ln_g[2:3]), _full(ln_b[2:3])]
    (h3,) = _fwd_call("ln2", _glu_ln_fn, (n_tile,), ln2_p, [tok_out(F32)])
    pre1, y3 = _mlp_fwd(1, h3, w1g, w2g)
    ln3_p = [_tile(h3, tm), _tile(y3, tm), _full(ln_g[3:4]), _full(ln_b[3:4])]
    (h4,) = _fwd_call("ln3", _res_ln_fn, (n_tile,), ln3_p, [tok_out(F32)])

    loss_acc, dh4 = _loss_call(h4, target, tm)

    ln_w = ['tile', 'tile', 'acc', 'acc']
    dh3_res, dy3, dg3, db3 = _bwd_call("ln3_bwd", _res_ln_fn, (n_tile,), ln3_p, [_tile(dh4, tm)], ln_w)
    dh3, dw1_1, dw2_1 = _mlp_bwd(1, h3, pre1, dy3, dh3_res, w1g, w2g)
    dh2_res, dz, dg2, db2 = _bwd_call("ln2_bwd", _glu_ln_fn, (n_tile,), ln2_p, [_tile(dh3, tm)], ln_w)
    dyg = _matmul("s5_glu_dx", dz, glu_g, 'nt', b_cols=(0,))
    big_grads['s5_w_glu'] = [_matmul("s5_glu_dw", yg, dz, 'tn', out_cols=True)]
    ds_re, ds_im, du_out, dcb_re, dcb_im, dd = _bwd_call(
        "s5_out_bwd", _s5_out_fn, (N_SSM_BLK, n_tile), out_p, [u_slab(dyg)],
        ['tile', 'tile', 'tile', 'acc', 'acc', 'acc'], addends={2: u_slab(dh2_res)})
    gs_re, gs_im = _s5_scan("s5_scan_bwd", ds_re, ds_im, a_row_re, -a_row_im, True)
    s_halo = lambda arr: (arr, (8, SSM_BLK_ST), lambda l, i: (jnp.maximum(i * (tm // 8) - 1, 0), l))
    a_par = lambda arr: (arr, (1, SSM_BLK_ST), lambda l, i: (0, l))
    step_p = [a_par(a_row_re), a_par(a_row_im), s_slab(s_re), s_slab(s_im), s_halo(s_re), s_halo(s_im)]
    dabar_re, dabar_im = _bwd_call("s5_abar_bwd", _s5_step_fn, (N_SSM_BLK, n_tile), step_p,
                                   [s_slab(gs_re), s_slab(gs_im)], ['acc', 'acc', None, None, None, None])
    dh2, dbb_re, dbb_im = _bwd_call("s5_in_bwd", _s5_in_fn, (N_SSM_BLK, n_tile), in_p, [s_slab(gs_re), s_slab(gs_im)],
                                    ['tile', 'acc', 'acc'], addends={0: u_slab(du_out)})
    from_in = lambda t: _block_diag_t(t, SSM_GROUP, SSM_STATE).transpose(0, 1, 3, 2).reshape(SSM_LANES, SSM_GROUP)
    from_out = lambda t: _block_diag_t(t, SSM_STATE, SSM_GROUP).transpose(0, 1, 3, 2).reshape(1, SSM_GROUPS, SSM_GROUP, SSM_STATE)
    grads['s5_c_re'], grads['s5_c_im'] = from_out(dcb_re), from_out(dcb_im)
    grads['s5_d'] = dd
    dcoef_re, dcoef_im, db_re, db_im = _bwd_call(
        "s5_bbar_bwd", _s5_bbar_fn, (1,), bbar_p, [_full(from_in(dbb_re)), _full(from_in(dbb_im))], ['acc'] * 4)
    grads['s5_b_re'] = db_re.reshape(w['s5_b_re'].shape)
    grads['s5_b_im'] = db_im.reshape(w['s5_b_im'].shape)
    par_ct = [_full(dabar_re.reshape(gp)), _full(dabar_im.reshape(gp)), _full(dcoef_re.reshape(gp)), _full(dcoef_im.reshape(gp))]
    da_re, da_im, dlog_dt = _bwd_call("s5_param_bwd", s5_par_fn, (1,), par_p, par_ct, ['acc'] * 3)
    grads['s5_a_re'], grads['s5_a_im'], grads['s5_log_dt'] = da_re[None], da_im[None], dlog_dt.reshape(1, SSM_GROUPS)

    dh1_res, dy1, dg1, db1 = _bwd_call("ln1_bwd", _res_ln_fn, (n_tile,), ln1_p, [_tile(dh2, tm)], ln_w)
    dh1, dw1_0, dw2_0 = _mlp_bwd(0, h1, pre0, dy1, dh1_res, w1g, w2g)
    big_grads['mlp_w1'], big_grads['mlp_w2'] = [dw1_0, dw1_1], [to_slots(dw2_0), to_slots(dw2_1)]
    dx_res, dy0, dg0, db0 = _bwd_call("ln0_bwd", _res_ln_fn, (n_tile,), ln0_p, [_tile(dh1, tm)], ln_w)
    grads['ln_g'] = jnp.concatenate([dg0, dg1, dg2, dg3])
    grads['ln_b'] = jnp.concatenate([db0, db1, db2, db3])
    dog = _matmul("rwkv_o_dx", dy0, big['rw_wo'], 'nt')
    big_grads['rw_wo'] = [to_slots(_matmul("rwkv_o_dw", og, dy0, 'tn'))]
    d_o, dbonus, dgate, grads['rw_lnx_g'], grads['rw_lnx_b'] = _bwd_call(
        "rwkv_gate_bwd", _gate_fn, (n_tile,), gate_p, [_tile(dog, tm)], ['tile', 'tile', 'tile', 'acc', 'acc'])
    dq, dm_all, dn_all = _rwkv_state_bwd(m_all, q, zs, d_o)
    st_ct = lambda arr: (arr, (1, CHUNK_PAIRS, PAIR, PAIR), lambda j, c: (c, j, 0, 0))
    dr, dlw, dk, dv, da, grads['rw_k_k'], grads['rw_k_a'], dr_k = _bwd_call(
        "rwkv_chunk_bwd", _chunk_fn, chunk_grid, chunk_p,
        [st_ct(dm_all), st_ct(dn_all), slab(dq), slab(d_o), slab(dbonus)],
        ['tile_bf16', 'tile', 'tile_bf16', 'tile_bf16', 'tile'] + ['acc'] * 3)
    grads['rw_r_k'] = dr_k.reshape(w['rw_r_k'].shape)
    dxr = _matmul("rwkv_r_dx", dr, big['rw_wr'], 'nt')
    dxk = _matmul("rwkv_k_dx", dk, big['rw_wk'], 'nt')
    dxv = _matmul("rwkv_v_dx", dv, big['rw_wv'], 'nt')
    big_grads['rw_wr'] = [to_slots(_matmul("rwkv_r_dw", xr, dr, 'tn'))]
    big_grads['rw_wk'] = [to_slots(_matmul("rwkv_k_dw", xk, dk, 'tn'))]
    big_grads['rw_wv'] = [to_slots(_matmul("rwkv_v_dw", xv, dv, 'tn'))]
    mix_ct = [_tile(t, tm) for t in (dxr, dxk, dxv, dlw, da, dgate)]
    halo_grad = ('tile', ((n_tile * 8, D_MODEL), (8, D_MODEL), lambda i: (i, 0)))
    res = _bwd_call("rwkv_mix_bwd", _mix_fn, (n_tile,), mix_p, mix_ct, ['tile', halo_grad] + ['acc'] * 9,
                    addends={0: _tile(dx_res, tm)})
    corr = res[1].reshape(n_tile, 8, D_MODEL)[1:, 7:8, :]
    corr = jnp.pad(corr, ((0, 1), (tm - 1, 0), (0, 0)))
    dx = (res[0].reshape(n_tile, tm, D_MODEL) + corr).reshape(n_tok, D_MODEL)
    (grads['rw_mu'], grads['rw_w0'], grads['rw_w1'], grads['rw_w2'], grads['rw_a0'], grads['rw_a1'], grads['rw_a2'],
     grads['rw_g1'], grads['rw_g2']) = [t[None] if t.shape[0] != 1 else t for t in res[2:]]
    return loss_acc[0, 0], dx, grads, big_grads


PACK_ROWS = 128


def _pack(arrs):
    flat = jnp.concatenate([a.reshape(-1).astype(F32) for a in arrs])
    per = PACK_ROWS * PACK_W
    total = -(-flat.shape[0] // per) * per
    return jnp.pad(flat, (0, total - flat.shape[0])).reshape(-1, PACK_W)


def _unpack(packed, shapes, lead=()):
    flat = packed.reshape(lead + (-1,))
    out, off = [], 0
    for s in shapes:
        n = math.prod(s)
        out.append(flat[..., off:off + n].reshape(lead + tuple(s)))
        off += n
    return out


def _mesh_pos():
    return lax.axis_index("x"), lax.axis_index("y"), lax.axis_index("c")


def _flip(v, f):
    return 1 - v if f else v


def _hbm_call(name, body, arrays, out_shapes, n_sems):
    n_t = len(arrays)
    hbm = pl.BlockSpec(memory_space=pl.ANY)
    return pl.pallas_call(
        body, name=name, out_shape=out_shapes, in_specs=[hbm] * n_t, out_specs=[hbm] * n_t,
        scratch_shapes=[pltpu.SemaphoreType.DMA((n_t, n_sems)), pltpu.SemaphoreType.DMA((n_t, n_sems)),
                        pltpu.SemaphoreType.DMA((n_t,))],
    )(*arrays)


def _all_gather(name, locals_):
    n_t = len(locals_)

    def body(*refs):
        x_refs, out_refs = refs[:n_t], refs[n_t:2 * n_t]
        send_sems, recv_sems, local_sems = refs[2 * n_t:]
        x, y, c = _mesh_pos()
        me, sibling = (x, y, c), (x, y, 1 - c)
        chips = [(1 - x, y), (x, 1 - y), (1 - x, 1 - y)]

        def copy(t, k, block, to, own=False):
            slot = out_refs[t].at[4 * block[0] + 2 * block[1] + block[2]]
            return pltpu.make_async_remote_copy(
                src_ref=x_refs[t] if own else slot, dst_ref=slot,
                send_sem=send_sems.at[t, k], recv_sem=recv_sems.at[t, k],
                device_id=to, device_id_type=pl.DeviceIdType.MESH)

        mine = [pltpu.make_async_copy(x_refs[t], out_refs[t].at[4 * x + 2 * y + c], local_sems.at[t]) for t in range(n_t)]
        for cp in mine:
            cp.start()
        sent = []
        for t in range(n_t):
            sent.append(copy(t, 0, me, sibling, own=True))
            sent += [copy(t, 1 + j, me, (*chip, c), own=True) for j, chip in enumerate(chips)]
        for cp in sent:
            cp.start()
        for j, chip in enumerate(chips):
            for t in range(n_t):
                copy(t, 1 + j, (*chip, c), me).wait_recv()
                passed = copy(t, 4 + j, (*chip, c), sibling)
                passed.start()
                sent.append(passed)
        for t in range(n_t):
            copy(t, 0, sibling, me).wait_recv()
            for j, chip in enumerate(chips):
                copy(t, 4 + j, (*chip, 1 - c), me).wait_recv()
        for cp in sent:
            cp.wait_send()
        for cp in mine:
            cp.wait()

    outs = [jax.ShapeDtypeStruct((N_DEV,) + a.shape, a.dtype) for a in locals_]
    return _hbm_call(name, body, locals_, outs, 7)


def _all_to_all(name, blocks):
    n_t = len(blocks)

    def body(*refs):
        x_refs, out_refs = refs[:n_t], refs[n_t:2 * n_t]
        send_sems, recv_sems, local_sems = refs[2 * n_t:]
        x, y, c = _mesh_pos()
        my_slot = 4 * x + 2 * y + c
        mine = [pltpu.make_async_copy(x_refs[t].at[my_slot], out_refs[t].at[my_slot], local_sems.at[t]) for t in range(n_t)]
        for cp in mine:
            cp.start()
        copies = []
        for k in range(1, N_DEV):
            peer = (_flip(x, k & 4), _flip(y, k & 2), _flip(c, k & 1))
            peer_slot = 4 * peer[0] + 2 * peer[1] + peer[2]
            for t in range(n_t):
                sems = dict(send_sem=send_sems.at[t, k - 1], recv_sem=recv_sems.at[t, k - 1],
                            device_id=peer, device_id_type=pl.DeviceIdType.MESH)
                cp = pltpu.make_async_remote_copy(src_ref=x_refs[t].at[peer_slot], dst_ref=out_refs[t].at[my_slot], **sems)
                cp.start()
                landed = pltpu.make_async_remote_copy(src_ref=x_refs[t].at[my_slot], dst_ref=out_refs[t].at[peer_slot], **sems)
                copies.append((cp, landed))
        for sent, landed in copies:
            landed.wait_recv()
            sent.wait_send()
        for cp in mine:
            cp.wait()

    outs = [jax.ShapeDtypeStruct(a.shape, a.dtype) for a in blocks]
    return _hbm_call(name, body, blocks, outs, 7)


def _adam_call(name, slots, w, m, v):
    rows, width = w.shape
    tile = min(rows, (PACK_ROWS * PACK_W) // width)
    c1 = 1.0 / (1.0 - ADAM_B1 ** ADAM_STEP)
    c2 = 1.0 / (1.0 - ADAM_B2 ** ADAM_STEP)

    def body(s_ref, w_ref, m_ref, v_ref, g_ref, d_ref, nm_ref, nv_ref):
        g = s_ref[0]
        for i in range(1, N_DEV):
            g = g + s_ref[i]
        m_new = ADAM_B1 * m_ref[...] + (1.0 - ADAM_B1) * g
        v_new = ADAM_B2 * v_ref[...] + (1.0 - ADAM_B2) * (g * g)
        g_ref[...] = g
        nm_ref[...] = m_new
        nv_ref[...] = v_new
        d_ref[...] = -ADAM_LR * ((m_new * c1) / (jnp.sqrt(v_new * c2) + ADAM_EPS) + ADAM_WD * w_ref[...])

    blk = _spec((tile, width), lambda i: (i, 0))
    return pl.pallas_call(
        body, name=name, grid=(rows // tile,),
        in_specs=[_spec((N_DEV, tile, width), lambda i: (0, i, 0)), blk, blk, blk],
        out_specs=[blk] * 4, out_shape=[jax.ShapeDtypeStruct(w.shape, F32)] * 4,
        compiler_params=_cparams(1),
    )(slots, w, m, v)


def kernel(x, ln_g, ln_b, rw_mu, rw_w0, rw_w1, rw_w2, rw_a0, rw_a1, rw_a2, rw_g1, rw_g2, rw_k_k, rw_k_a, rw_r_k, rw_wr, rw_wk, rw_wv, rw_wo, rw_lnx_g, rw_lnx_b, s5_a_re, s5_a_im, s5_log_dt, s5_b_re, s5_b_im, s5_c_re, s5_c_im, s5_d, s5_w_glu, mlp_w1, mlp_w2, loss_target, m_ln_g, m_ln_b, m_rw_mu, m_rw_w0, m_rw_w1, m_rw_w2, m_rw_a0, m_rw_a1, m_rw_a2, m_rw_g1, m_rw_g2, m_rw_k_k, m_rw_k_a, m_rw_r_k, m_rw_wr, m_rw_wk, m_rw_wv, m_rw_wo, m_rw_lnx_g, m_rw_lnx_b, m_s5_a_re, m_s5_a_im, m_s5_log_dt, m_s5_b_re, m_s5_b_im, m_s5_c_re, m_s5_c_im, m_s5_d, m_s5_w_glu, m_mlp_w1, m_mlp_w2, v_ln_g, v_ln_b, v_rw_mu, v_rw_w0, v_rw_w1, v_rw_w2, v_rw_a0, v_rw_a1, v_rw_a2, v_rw_g1, v_rw_g2, v_rw_k_k, v_rw_k_a, v_rw_r_k, v_rw_wr, v_rw_wk, v_rw_wv, v_rw_wo, v_rw_lnx_g, v_rw_lnx_b, v_s5_a_re, v_s5_a_im, v_s5_log_dt, v_s5_b_re, v_s5_b_im, v_s5_c_re, v_s5_c_im, v_s5_d, v_s5_w_glu, v_mlp_w1, v_mlp_w2):
    given = dict(locals())
    local_w = {n: given[n] for n in WEIGHTS}
    local_m = {n: given["m_" + n] for n in WEIGHTS}
    local_v = {n: given["v_" + n] for n in WEIGHTS}
    small = [n for n in WEIGHTS if n in SHARDED and n not in BIG]
    small_shapes = [local_w[n].shape for n in small]
    rep_shapes = [local_w[n].shape for n in REPLICATED]

    gathered = _all_gather("weights_all_gather", [local_w[n].astype(BF16) for n in BIG] + [_pack([local_w[n] for n in small])])
    big_gathered = dict(zip(BIG, gathered))
    full = dict(local_w)
    for n, blk in zip(small, _unpack(gathered[-1], small_shapes, lead=(N_DEV,))):
        full[n] = jnp.concatenate([blk[i] for i in range(N_DEV)], axis=SHARDED[n])

    loss_sq, dx, grads, big_grads = _local_step(x[0], loss_target[0], full, big_gathered)
    loss = (0.5 / D_MODEL) * lax.psum(loss_sq, MESH_AXES)

    parts = [jnp.stack(jnp.split(grads[n], N_DEV, axis=SHARDED[n])) for n in small]
    small_send = jnp.stack([_pack([p[i] for p in parts]) for i in range(N_DEV)])
    big_send = [(n, i, t) for n in BIG for i, t in enumerate(big_grads[n])]
    recv = _all_to_all("grads_all_to_all", [t for _, _, t in big_send] + [small_send])
    by_name = {}
    for (n, i, _), slots in zip(big_send, recv):
        part = lambda d: d[n].reshape((-1,) + slots.shape[1:])[i]
        out = _adam_call(f"adamw_{n}_{i}", slots, part(local_w), part(local_m), part(local_v))
        by_name.setdefault(n, []).append(out)
    for n in BIG:
        by_name[n] = [jnp.stack(ts).reshape(local_w[n].shape) for ts in zip(*by_name[n])]
    small_out = _adam_call("adamw_small", recv[-1], _pack([local_w[n] for n in small]),
                           _pack([local_m[n] for n in small]), _pack([local_v[n] for n in small]))
    (rep_all,) = _all_gather("grads_all_gather", [_pack([grads[n] for n in REPLICATED])])
    rep_out = _adam_call("adamw_replicated", rep_all, _pack([local_w[n] for n in REPLICATED]),
                         _pack([local_m[n] for n in REPLICATED]), _pack([local_v[n] for n in REPLICATED]))
    for k, (small_packed, rep_packed) in enumerate(zip(small_out, rep_out)):
        for n, t in zip(small, _unpack(small_packed, small_shapes)):
            by_name.setdefault(n, [None] * 4)[k] = t
        for n, t in zip(REPLICATED, _unpack(rep_packed, rep_shapes)):
            by_name.setdefault(n, [None] * 4)[k] = t
    results = [by_name[n][k] for k in range(4) for n in WEIGHTS]
    return (loss, dx[None], *results)
```

```python
import functools
import math

import jax
import jax.numpy as jnp
from jax import lax
from jax.experimental import pallas as pl
from jax.experimental.pallas import tpu as pltpu

F32 = jnp.float32
BF16 = jnp.bfloat16
HI = lax.Precision.HIGHEST

D_MODEL = 1024
HEAD = 64
PAIR = 2 * HEAD
N_PAIR = D_MODEL // PAIR
CHUNK = 64
GN_EPS = 64e-5
LN_EPS = 1e-5
SSM_GROUP = 16
SSM_STATE = 64
SSM_GROUPS = D_MODEL // SSM_GROUP
SSM_LANES = SSM_GROUPS * SSM_STATE
SSM_BLK_IN = 128
SSM_BLK_ST = 512
N_SSM_BLK = D_MODEL // SSM_BLK_IN
DEPTH = 2
DN_ALPHA = (2.0 * DEPTH) ** 0.25
ADAM_LR, ADAM_B1, ADAM_B2, ADAM_EPS, ADAM_WD, ADAM_STEP = 0.001, 0.9, 0.999, 1e-08, 0.01, 10
N_DEV = 8
MESH_AXES = ("x", "y", "c")
VMEM_LIMIT = 56 * 1024 * 1024
TOKEN_TILE = 256
PACK_W = 1024

SHARDED = {
    "rw_mu": 2, "rw_w1": 1, "rw_w2": 2, "rw_a1": 1, "rw_a2": 2, "rw_g1": 1, "rw_g2": 2,
    "rw_wr": 1, "rw_wk": 1, "rw_wv": 1, "rw_wo": 1, "s5_d": 1, "s5_w_glu": 2, "mlp_w1": 2, "mlp_w2": 1,
}
WEIGHTS = ['ln_g', 'ln_b', 'rw_mu', 'rw_w0', 'rw_w1', 'rw_w2', 'rw_a0', 'rw_a1', 'rw_a2', 'rw_g1', 'rw_g2', 'rw_k_k',
           'rw_k_a', 'rw_r_k', 'rw_wr', 'rw_wk', 'rw_wv', 'rw_wo', 'rw_lnx_g', 'rw_lnx_b', 's5_a_re', 's5_a_im',
           's5_log_dt', 's5_b_re', 's5_b_im', 's5_c_re', 's5_c_im', 's5_d', 's5_w_glu', 'mlp_w1', 'mlp_w2']
REPLICATED = [n for n in WEIGHTS if n not in SHARDED]
BIG = ['rw_wr', 'rw_wk', 'rw_wv', 'rw_wo', 's5_w_glu', 'mlp_w1', 'mlp_w2']
LATE = ['s5_w_glu', 'mlp_w1', 'mlp_w2']


def _cparams(n_grid):
    return pltpu.CompilerParams(dimension_semantics=("arbitrary",) * n_grid, vmem_limit_bytes=VMEM_LIMIT)


@jax.custom_vjp
def _mm(x, w):
    return jnp.dot(x.astype(BF16), w.astype(BF16), preferred_element_type=F32)


def _mm_fwd(x, w):
    return _mm(x, w), (x, w)


def _mm_bwd(res, dy):
    x, w = res
    dyb = dy.astype(BF16)
    dx = lax.dot_general(dyb, w.astype(BF16), (((1,), (1,)), ((), ())), preferred_element_type=F32)
    dw = lax.dot_general(x.astype(BF16), dyb, (((0,), (0,)), ((), ())), preferred_element_type=F32)
    return dx, dw


_mm.defvjp(_mm_fwd, _mm_bwd)


def _hdot(a, b):
    return jnp.dot(a, b, precision=HI, preferred_element_type=F32)


def _hdot_nt(a, b):
    return lax.dot_general(a, b, (((1,), (1,)), ((), ())), precision=HI, preferred_element_type=F32)


def _hdot_tn(a, b):
    return lax.dot_general(a, b, (((0,), (0,)), ((), ())), precision=HI, preferred_element_type=F32)


_DOT_DIMS = {'nn': (((1,), (0,)), ((), ())), 'nt': (((1,), (1,)), ((), ())), 'tn': (((0,), (0,)), ((), ()))}
_BATCH_DOT_DIMS = {'nn': (((2,), (1,)), ((0,), (0,))), 'nt': (((2,), (2,)), ((0,), (0,))), 'tn': (((1,), (1,)), ((0,), (0,)))}
CHUNK_PAIRS = 8
CHUNK_PASSES = 1


def _split_bf16(a):
    hi = a.astype(BF16)
    return hi, (a - hi.astype(F32)).astype(BF16)


def _dot_passes(a, b, mode, passes=None):
    dims = (_DOT_DIMS if a.ndim == 2 else _BATCH_DOT_DIMS)[mode]
    dot = lambda p, q: lax.dot_general(p, q, dims, preferred_element_type=F32)
    if (CHUNK_PASSES if passes is None else passes) == 1:
        return dot(a.astype(BF16), b.astype(BF16))
    (ah, al), (bh, bl) = _split_bf16(a), _split_bf16(b)
    return dot(ah, bh) + (dot(ah, bl) + dot(al, bh))


@functools.partial(jax.custom_vjp, nondiff_argnums=(2,))
def _pdot(a, b, mode):
    return _dot_passes(a, b, mode)


def _pdot_fwd(a, b, mode):
    return _dot_passes(a, b, mode), (a, b)


def _pdot_bwd(mode, res, dy):
    a, b = res
    if mode == 'nn':
        return _dot_passes(dy, b, 'nt'), _dot_passes(a, dy, 'tn')
    if mode == 'nt':
        return _dot_passes(dy, b, 'nn'), _dot_passes(dy, a, 'tn')
    return _dot_passes(b, dy, 'nt'), _dot_passes(a, dy, 'nn')


_pdot.defvjp(_pdot_fwd, _pdot_bwd)


def _tri_sum(x, upper):
    nb, n = x.shape[0], x.shape[1]
    ti = lax.broadcasted_iota(jnp.int32, (nb, n, n), 1)
    tj = lax.broadcasted_iota(jnp.int32, (nb, n, n), 2)
    tri = ((ti <= tj) if upper else (ti >= tj)).astype(BF16)
    hi = x.astype(BF16)
    rest = x - hi.astype(F32)
    mid = rest.astype(BF16)
    lo = (rest - mid.astype(F32)).astype(BF16)
    dot = lambda q: lax.dot_general(tri, q, _BATCH_DOT_DIMS['nn'], preferred_element_type=F32)
    return dot(hi) + (dot(mid) + dot(lo))


@jax.custom_vjp
def _cumsum_rows(x):
    return _tri_sum(x, False)


_cumsum_rows.defvjp(lambda x: (_tri_sum(x, False), None), lambda _, dy: (_tri_sum(dy, True),))


@functools.partial(jax.custom_vjp, nondiff_argnums=(1,))
def _roll_rows(x, shift):
    return pltpu.roll(x, shift, 0)


def _roll_rows_fwd(x, shift):
    return pltpu.roll(x, shift, 0), None


def _roll_rows_bwd(shift, _, dy):
    n = dy.shape[0]
    return (pltpu.roll(dy, (n - shift) % n, 0),)


_roll_rows.defvjp(_roll_rows_fwd, _roll_rows_bwd)


def _shift_down(x, first_row):
    row = lax.broadcasted_iota(jnp.int32, x.shape, 0)
    return jnp.where(row == 0, first_row, _roll_rows(x, 1))


def _sigmoid(x):
    return 1.0 / (1.0 + jnp.exp(-x))


def _softplus(x):
    return jnp.maximum(x, 0.0) + jnp.log(1.0 + jnp.exp(-jnp.abs(x)))


def _gelu(x):
    return 0.5 * x * (1.0 + jnp.tanh(math.sqrt(2.0 / math.pi) * (x + 0.044715 * (x * x * x))))


def _layer_norm(x, g, b):
    mu = jnp.mean(x, axis=-1, keepdims=True)
    xc = x - mu
    var = jnp.mean(xc * xc, axis=-1, keepdims=True)
    return xc * lax.rsqrt(var + LN_EPS) * g + b


def _spec(block, index_map):
    return pl.BlockSpec(block, index_map)


def _tile(arr, tm):
    return (arr, (tm, arr.shape[1]), lambda i: (i, 0))


def _full(arr):
    nd = arr.ndim
    return (arr, arr.shape, lambda *_: (0,) * nd)


def _fwd_call(name, fn, grid, prims, outs):
    n_in = len(prims)

    def body(*refs):
        vals = [r[...] for r in refs[:n_in]]
        vals = [v.astype(F32) if v.dtype != F32 else v for v in vals]
        res = fn(*vals)
        for o, r in zip(refs[n_in:], res):
            o[...] = r.astype(o.dtype)

    return pl.pallas_call(
        body, name=name, grid=grid,
        in_specs=[_spec(b, m) for _, b, m in prims],
        out_specs=[_spec(b, m) for _, _, b, m in outs],
        out_shape=[jax.ShapeDtypeStruct(s, d) for s, d, _, _ in outs],
        compiler_params=_cparams(len(grid)),
    )(*[a for a, _, _ in prims])


def _bwd_call(name, fn, grid, prims, cts, wants, addends=None):
    addends = addends or {}
    n_in, n_ct = len(prims), len(cts)
    out_place = {}
    wants, out_dtype = list(wants), {}
    for i, w in enumerate(wants):
        if isinstance(w, tuple):
            wants[i], out_place[i] = w
        elif w == 'tile_bf16':
            wants[i], out_dtype[i] = 'tile', BF16
    diff = [i for i, w in enumerate(wants) if w]
    add_idx = [i for i in diff if i in addends]
    last_axis = len(grid) - 1

    def body(*refs):
        vals = [r[...] for r in refs[:n_in]]
        vals = [v.astype(F32) if v.dtype != F32 else v for v in vals]
        ct_vals = [r[...] for r in refs[n_in:n_in + n_ct]]
        ct_vals = [v.astype(F32) if v.dtype != F32 else v for v in ct_vals]
        add_refs = dict(zip(add_idx, refs[n_in + n_ct:n_in + n_ct + len(add_idx)]))
        out_refs = refs[n_in + n_ct + len(add_idx):]

        def f(*dargs):
            full = list(vals)
            for i, a in zip(diff, dargs):
                full[i] = a
            return tuple(fn(*full))

        _, vjp = jax.vjp(f, *[vals[i] for i in diff])
        grads = vjp(tuple(ct_vals))
        first = pl.program_id(last_axis) == 0
        for o, g, i in zip(out_refs, grads, diff):
            if wants[i] == 'tile':
                if i in add_refs:
                    g = g + add_refs[i][...]
                o[...] = g.astype(o.dtype)
            else:
                @pl.when(first)
                def _(o=o):
                    o[...] = jnp.zeros(o.shape, o.dtype)
                o[...] += g

    ins = list(prims) + list(cts) + [addends[i] for i in add_idx]
    places = [out_place.get(i, (prims[i][0].shape, prims[i][1], prims[i][2])) for i in diff]
    return pl.pallas_call(
        body, name=name, grid=grid,
        in_specs=[_spec(b, m) for _, b, m in ins],
        out_specs=[_spec(b, m) for _, b, m in places],
        out_shape=[jax.ShapeDtypeStruct(s, out_dtype.get(i, F32)) for i, (s, _, _) in zip(diff, places)],
        compiler_params=_cparams(len(grid)),
    )(*[a for a, _, _ in ins])


def _matmul(name, a, b, mode, *, out_dtype=F32, lhs_fn=None, epi=None, extra=None, tm=1024, tn=1024, tk=1024,
            b_cols=None, b_rows=None, out_cols=False):
    if mode == 'tn':
        kdim, m = a.shape
    else:
        m, kdim = a.shape
    if b_rows is not None:
        loc, cols = b.shape[-2], b.shape[-1]
        if mode == 'nn':
            n, tk = cols, loc
        else:
            n, tn = N_DEV * loc, loc
    elif b_cols is None:
        n = b.shape[0] if mode == 'nt' else b.shape[1]
    else:
        rows, loc = b.shape[-2], b.shape[-1]
        if mode == 'nn':
            n, tn = N_DEV * loc, loc
        else:
            n, tk = rows, loc
    if out_cols:
        tn = n // N_DEV
    tm, tn, tk = min(tm, m), min(tn, n), min(tk, kdim)
    nk = kdim // tk
    grid = (m // tm, n // tn, nk)
    a_spec = _spec((tk, tm), lambda i, j, k: (k, i)) if mode == 'tn' else _spec((tm, tk), lambda i, j, k: (i, k))
    if b_rows is not None:
        lead = tuple(b_rows)
        skip = (None,) * (1 + len(lead))
        if mode == 'nn':
            b_spec = _spec(skip + (tk, tn), lambda i, j, k: (k,) + lead + (0, j))
        else:
            b_spec = _spec(skip + (tn, tk), lambda i, j, k: (j,) + lead + (0, k))
    elif b_cols is None:
        b_spec = _spec((tn, tk), lambda i, j, k: (j, k)) if mode == 'nt' else _spec((tk, tn), lambda i, j, k: (k, j))
    else:
        lead = tuple(b_cols)
        skip = (None,) * (1 + len(lead))
        if mode == 'nn':
            b_spec = _spec(skip + (tk, tn), lambda i, j, k: (j,) + lead + (k, 0))
        else:
            b_spec = _spec(skip + (tn, tk), lambda i, j, k: (k,) + lead + (j, 0))
    o_spec = _spec((tm, tn), lambda i, j, k: (i, j))
    if out_cols:
        o_place, o_shape = _spec((None, tm, tn), lambda i, j, k: (j, i, 0)), (N_DEV, m, tn)
    else:
        o_place, o_shape = o_spec, (m, n)
    dims = _DOT_DIMS[mode]

    def body(*refs):
        if extra is None:
            a_ref, b_ref, o_ref, acc = refs
            x_ref = None
        else:
            a_ref, b_ref, x_ref, o_ref, acc = refs
        k = pl.program_id(2)
        av = a_ref[...]
        if lhs_fn is not None:
            av = lhs_fn(av.astype(F32))
        part = lax.dot_general(av.astype(BF16), b_ref[...].astype(BF16), dims, preferred_element_type=F32)

        def finish(r):
            if epi is not None:
                r = epi(r, x_ref[...])
            o_ref[...] = r.astype(o_ref.dtype)

        if nk == 1:
            finish(part)
            return

        @pl.when(k == 0)
        def _():
            acc[...] = part

        @pl.when((k > 0) & (k < nk - 1))
        def _():
            acc[...] += part

        @pl.when(k == nk - 1)
        def _():
            finish(acc[...] + part)

    ops, specs = [a, b], [a_spec, b_spec]
    if extra is not None:
        ops.append(extra)
        specs.append(o_spec)
    return pl.pallas_call(
        body, name=name, grid=grid, in_specs=specs, out_specs=o_place,
        out_shape=jax.ShapeDtypeStruct(o_shape, out_dtype),
        scratch_shapes=[pltpu.VMEM((tm, tn), F32)],
        compiler_params=_cparams(3),
    )(*ops)


def _relu2(x):
    r = jnp.maximum(x, 0.0)
    return r * r


def _mix_fn(x, halo, mu, w0, w1, w2, a0, a1, a2, g1, g2):
    prev_row = jnp.where(pl.program_id(0) == 0, 0.0, halo[7:8, :])
    xx = _shift_down(x, prev_row) - x
    xr, xw, xk, xv, xa, xg = (x + xx * mu[i:i + 1, :] for i in range(6))
    w_pre = w0 + _mm(jnp.tanh(_mm(xw, w1)), w2)
    log_decay = -jnp.exp(-_softplus(-w_pre) - 0.5)
    a = _sigmoid(a0 + _mm(_mm(xa, a1), a2))
    g = _mm(_sigmoid(_mm(xg, g1)), g2)
    return xr, xk, xv, log_decay, a, g


def _chunk_fn(r, lw, k, v, a, k_k, k_a, r_k):
    n, n_pair = r.shape[0], r.shape[1] // PAIR
    to_batch = lambda t: jnp.concatenate([t[None, :, p * PAIR:(p + 1) * PAIR] for p in range(n_pair)], axis=0)
    from_batch = lambda t: jnp.concatenate([t[p] for p in range(n_pair)], axis=1)
    r, lw, k, v, a, k_k, k_a, r_k = (to_batch(t) for t in (r, lw, k, v, a, k_k, k_a, r_k))
    lane = lax.broadcasted_iota(jnp.int32, (1, 1, PAIR), 2)
    m0 = lane < HEAD

    def head_sum(t):
        s0 = jnp.sum(jnp.where(m0, t, 0.0), axis=2, keepdims=True)
        s1 = jnp.sum(jnp.where(m0, 0.0, t), axis=2, keepdims=True)
        return jnp.where(m0, s0, s1)

    kk = k * k_k
    kk = kk / jnp.maximum(jnp.sqrt(head_sum(kk * kk)), 1e-12)
    k2 = k * (1.0 + (a - 1.0) * k_a)
    aa, bb = -kk, kk * a
    bonus = head_sum(r * k2 * r_k) * v

    cum = _cumsum_rows(lw)
    p_in, p_ex, p_inv = jnp.exp(cum), jnp.exp(cum - lw), jnp.exp(-cum)
    at, rt, kt, bt = aa * p_ex, r * p_in, k2 * p_inv, bb * p_inv

    def stack_masked(t):
        return jnp.concatenate([jnp.where(m0, t, 0.0), jnp.where(m0, 0.0, t)], axis=1)

    def unstack_sum(t):
        return t[:, :n] + t[:, n:]

    def unstack_select(t):
        return jnp.where(m0, t[:, :n], t[:, n:])

    ti = lax.broadcasted_iota(jnp.int32, (1, 2 * n, 2 * n), 1)
    tj = lax.broadcasted_iota(jnp.int32, (1, 2 * n, 2 * n), 2)
    same = (ti >= n) == (tj >= n)
    incl, strict = same & (ti >= tj), same & (ti > tj)
    eye_n = (ti == tj).astype(F32)
    at_s, rt_s, kt_s, bt_s = stack_masked(at), stack_masked(rt), stack_masked(kt), stack_masked(bt)
    v_s = jnp.concatenate([v, v], axis=1)
    a_ab = jnp.where(strict, _pdot(at_s, bt_s, 'nt'), 0.0)
    a_ak = jnp.where(strict, _pdot(at_s, kt_s, 'nt'), 0.0)
    a_rb = jnp.where(incl, _pdot(rt_s, bt_s, 'nt'), 0.0)
    a_rk = jnp.where(incl, _pdot(rt_s, kt_s, 'nt'), 0.0)
    inv, pw = eye_n + a_ab, a_ab
    for _ in range(int(math.log2(n)) - 1):
        pw = _pdot(pw, pw, 'nn')
        inv = inv + _pdot(inv, pw, 'nn')
    w_s = _pdot(inv, at_s, 'nn')
    y_s = _pdot(inv, _pdot(a_ak, v_s, 'nn'), 'nn')
    q_all = unstack_sum(rt_s + _pdot(a_rb, w_s, 'nn'))
    oi_all = unstack_select(_pdot(a_rb, y_s, 'nn') + _pdot(a_rk, v_s, 'nn'))
    y_m = stack_masked(unstack_select(y_s))

    mm = _pdot(bt_s, w_s, 'tn')
    nn = _pdot(bt_s, y_m, 'tn') + _pdot(kt_s, stack_masked(v), 'tn')
    ei = lax.broadcasted_iota(jnp.int32, (1, PAIR, PAIR), 1)
    ej = lax.broadcasted_iota(jnp.int32, (1, PAIR, PAIR), 2)
    eye_p = (ei == ej).astype(F32)
    decay_col = jnp.sum(eye_p * p_in[:, n - 1:n, :], axis=2, keepdims=True)
    return ((decay_col * (eye_p + mm))[None], (decay_col * nn)[None], from_batch(q_all), from_batch(oi_all),
            from_batch(bonus))


def _gate_fn(o, bonus, g, lnx_g, lnx_b):
    lane = lax.broadcasted_iota(jnp.int32, (1, PAIR), 1)
    m0 = lane < HEAD

    def head_mean(t):
        s0 = jnp.sum(jnp.where(m0, t, 0.0), axis=1, keepdims=True)
        s1 = jnp.sum(jnp.where(m0, 0.0, t), axis=1, keepdims=True)
        return jnp.where(m0, s0, s1) * (1.0 / HEAD)

    outs = []
    for p in range(N_PAIR):
        sl = slice(p * PAIR, (p + 1) * PAIR)
        oc = o[:, sl] - head_mean(o[:, sl])
        on = oc * lax.rsqrt(head_mean(oc * oc) + GN_EPS) * lnx_g[:, sl] + lnx_b[:, sl]
        outs.append((on + bonus[:, sl]) * g[:, sl])
    return (jnp.concatenate(outs, axis=1),)


def _res_ln_fn(h, y, g, b):
    return (_layer_norm(DN_ALPHA * h + y, g, b),)


def _glu_ln_fn(h, z, g, b):
    mix = z[:, :D_MODEL] * _sigmoid(z[:, D_MODEL:])
    return (_layer_norm(DN_ALPHA * h + mix, g, b),)


def _s5_param_fn(a_re, a_im, log_dt, b_re, b_im):
    dt = jnp.exp(log_dt)
    lam_re = jnp.minimum(a_re, -1e-4)
    lam_im = a_im
    mag = jnp.exp(dt * lam_re)
    abar_re = mag * jnp.cos(dt * lam_im)
    abar_im = mag * jnp.sin(dt * lam_im)
    den = lam_re * lam_re + lam_im * lam_im
    nr, ni = abar_re - 1.0, abar_im
    coef_re = (nr * lam_re + ni * lam_im) / den
    coef_im = (ni * lam_re - nr * lam_im) / den
    return abar_re, abar_im, coef_re, coef_im


def _s5_bbar_fn(coef_re, coef_im, b_re, b_im):
    return coef_re * b_re - coef_im * b_im, coef_re * b_im + coef_im * b_re


def _s5_in_fn(u, b_re, b_im):
    return _pdot(u, b_re[0], 'nn'), _pdot(u, b_im[0], 'nn')


def _s5_out_fn(s_re, s_im, u, c_re, c_im, d):
    y = _pdot(s_re, c_re[0], 'nn') - _pdot(s_im, c_im[0], 'nn') + u * d
    return (_gelu(y),)


def _s5_step_fn(a_re, a_im, s_re, s_im, halo_re, halo_im):
    first = pl.program_id(1) == 0
    p_re = _shift_down(s_re, jnp.where(first, 0.0, halo_re[7:8, :]))
    p_im = _shift_down(s_im, jnp.where(first, 0.0, halo_im[7:8, :]))
    return a_re * p_re - a_im * p_im, a_re * p_im + a_im * p_re


def _rwkv_state_fwd(m_all, n_all, q, oi):
    n_chunk = m_all.shape[0]

    def body(m_ref, n_ref, q_ref, oi_ref, o_ref, zs_ref, z):
        @pl.when(pl.program_id(0) == 0)
        def _():
            z[...] = jnp.zeros(z.shape, F32)

        for j in range(N_PAIR):
            sl = slice(j * PAIR, (j + 1) * PAIR)
            zj = z[j]
            zs_ref[0, j] = zj
            o_ref[:, sl] = _dot_passes(q_ref[:, sl], zj, 'nn', 3) + oi_ref[:, sl]
            z[j] = _dot_passes(m_ref[0, j], zj, 'nn', 3) + n_ref[0, j]

    st = _spec((1, N_PAIR, PAIR, PAIR), lambda c: (c, 0, 0, 0))
    tok = _spec((CHUNK, D_MODEL), lambda c: (c, 0))
    return pl.pallas_call(
        body, name="rwkv_state_fwd", grid=(n_chunk,),
        in_specs=[st, st, tok, tok], out_specs=[tok, st],
        out_shape=[jax.ShapeDtypeStruct(q.shape, F32), jax.ShapeDtypeStruct(m_all.shape, F32)],
        scratch_shapes=[pltpu.VMEM((N_PAIR, PAIR, PAIR), F32)],
        compiler_params=_cparams(1),
    )(m_all, n_all, q, oi)


def _rwkv_state_bwd(m_all, q, zs, d_o):
    n_chunk = m_all.shape[0]

    def body(m_ref, q_ref, zs_ref, do_ref, dq_ref, dm_ref, dn_ref, g):
        @pl.when(pl.program_id(0) == 0)
        def _():
            g[...] = jnp.zeros(g.shape, F32)

        bi = lax.broadcasted_iota(jnp.int32, (PAIR, PAIR), 0) // HEAD
        bj = lax.broadcasted_iota(jnp.int32, (PAIR, PAIR), 1) // HEAD
        same_head = bi == bj
        for j in range(N_PAIR):
            sl = slice(j * PAIR, (j + 1) * PAIR)
            gj, zj, do = g[j], zs_ref[0, j], do_ref[:, sl]
            dn_ref[0, j] = gj
            dm_ref[0, j] = _dot_passes(gj, zj, 'nt', 3)
            dq_ref[:, sl] = _dot_passes(do, zj, 'nt', 3)
            g[j] = _dot_passes(m_ref[0, j], gj, 'tn', 3) + jnp.where(same_head, _dot_passes(q_ref[:, sl], do, 'tn', 3), 0.0)

    st = _spec((1, N_PAIR, PAIR, PAIR), lambda c: (n_chunk - 1 - c, 0, 0, 0))
    tok = _spec((CHUNK, D_MODEL), lambda c: (n_chunk - 1 - c, 0))
    return pl.pallas_call(
        body, name="rwkv_state_bwd", grid=(n_chunk,),
        in_specs=[st, tok, st, tok], out_specs=[tok, st, st],
        out_shape=[jax.ShapeDtypeStruct(q.shape, F32), jax.ShapeDtypeStruct(m_all.shape, F32),
                   jax.ShapeDtypeStruct(m_all.shape, F32)],
        scratch_shapes=[pltpu.VMEM((N_PAIR, PAIR, PAIR), F32)],
        compiler_params=_cparams(1),
    )(m_all, q, zs, d_o)


def _s5_scan(name, b_re, b_im, a_re, a_im, reverse):
    t_len, lanes = b_re.shape
    tt, lb, grp = 512, SSM_BLK_ST, 8
    tt = min(tt, t_len)
    n_t, n_grp = t_len // tt, tt // grp

    def cmul(xr, xi, yr, yi):
        return xr * yr - xi * yi, xr * yi + xi * yr

    def body(bre_ref, bim_ref, are_ref, aim_ref, sre_ref, sim_ref, carry):
        @pl.when(pl.program_id(1) == 0)
        def _():
            carry[...] = jnp.zeros(carry.shape, F32)

        a1 = (are_ref[...], aim_ref[...])
        a2 = cmul(*a1, *a1)
        a4 = cmul(*a2, *a2)
        a8 = cmul(*a4, *a4)
        row = lax.broadcasted_iota(jnp.int32, (grp, lb), 0)
        expo = (grp - row) if reverse else (row + 1)
        pw = (jnp.ones((grp, lb), F32), jnp.zeros((grp, lb), F32))
        for bit, ap in ((1, a1), (2, a2), (4, a4), (8, a8)):
            nxt = cmul(*pw, *ap)
            sel = (expo & bit) != 0
            pw = (jnp.where(sel, nxt[0], pw[0]), jnp.where(sel, nxt[1], pw[1]))

        def group(i, c):
            gi = (n_grp - 1 - i) if reverse else i
            rows = pl.ds(pl.multiple_of(gi * grp, grp), grp)
            xr, xi = bre_ref[rows, :], bim_ref[rows, :]
            for d, ad in ((1, a1), (2, a2), (4, a4)):
                if reverse:
                    keep = row < grp - d
                    sr, si = pltpu.roll(xr, grp - d, 0), pltpu.roll(xi, grp - d, 0)
                else:
                    keep = row >= d
                    sr, si = pltpu.roll(xr, d, 0), pltpu.roll(xi, d, 0)
                pr, pi = cmul(*ad, jnp.where(keep, sr, 0.0), jnp.where(keep, si, 0.0))
                xr, xi = xr + pr, xi + pi
            cr, ci = cmul(*pw, c[0], c[1])
            xr, xi = xr + cr, xi + ci
            sre_ref[rows, :] = xr
            sim_ref[rows, :] = xi
            edge = slice(0, 1) if reverse else slice(grp - 1, grp)
            return xr[edge, :], xi[edge, :]

        c = lax.fori_loop(0, n_grp, group, (carry[0:1, :], carry[1:2, :]))
        carry[0:1, :] = c[0]
        carry[1:2, :] = c[1]

    tmap = (lambda l, t: (n_t - 1 - t, l)) if reverse else (lambda l, t: (t, l))
    blk = _spec((tt, lb), tmap)
    par = _spec((1, lb), lambda l, t: (0, l))
    return pl.pallas_call(
        body, name=name, grid=(lanes // lb, n_t),
        in_specs=[blk, blk, par, par], out_specs=[blk, blk],
        out_shape=[jax.ShapeDtypeStruct(b_re.shape, F32)] * 2,
        scratch_shapes=[pltpu.VMEM((2, lb), F32)],
        compiler_params=_cparams(2),
    )(b_re, b_im, a_re, a_im)


def _loss_call(h, target, tm):
    n_tok, d = h.shape

    def body(h_ref, t_ref, acc_ref, dh_ref):
        @pl.when(pl.program_id(0) == 0)
        def _():
            acc_ref[...] = jnp.zeros(acc_ref.shape, F32)

        e = h_ref[...] - t_ref[...]
        dh_ref[...] = e * (1.0 / d)
        acc_ref[...] += jnp.sum(jnp.sum(e * e, axis=1, keepdims=True), axis=0, keepdims=True)

    tok = _spec((tm, d), lambda i: (i, 0))
    return pl.pallas_call(
        body, name="loss", grid=(n_tok // tm,),
        in_specs=[tok, tok], out_specs=[_spec((8, 128), lambda i: (0, 0)), tok],
        out_shape=[jax.ShapeDtypeStruct((8, 128), F32), jax.ShapeDtypeStruct(h.shape, F32)],
        compiler_params=_cparams(1),
    )(h, target)


def _block_diag(t):
    nb, ng, rows, cols = t.shape
    eye = jnp.eye(ng, dtype=t.dtype)
    return jnp.einsum('bgrc,gh->bgrhc', t, eye).reshape(nb, ng * rows, ng * cols)


def _block_diag_t(t, rows, cols):
    nb, ng = t.shape[0], t.shape[1] // rows
    t5 = t.reshape(nb, ng, rows, ng, cols)
    return jnp.einsum('bgrhc,gh->bgrc', t5, jnp.eye(ng, dtype=t.dtype))


def _mlp_fwd(layer, h, w1g, w2g):
    pre = _matmul(f"mlp{layer}_up", h, w1g, 'nn', b_cols=(layer,))
    return pre, _matmul(f"mlp{layer}_down", pre, w2g, 'nn', lhs_fn=_relu2, b_rows=(layer,))


def _mlp_bwd(layer, h, pre, dy, dh_res, w1g, w2g):
    dpre = _matmul(f"mlp{layer}_down_dx", dy, w2g, 'nt', epi=lambda acc, p: acc * (2.0 * jnp.maximum(p, 0.0)),
                   extra=pre, b_rows=(layer,))
    dw2 = _matmul(f"mlp{layer}_down_dw", pre, dy, 'tn', lhs_fn=_relu2)
    dw1 = _matmul(f"mlp{layer}_up_dw", h, dpre, 'tn', out_cols=True)
    dh = _matmul(f"mlp{layer}_up_dx", dpre, w1g, 'nt', epi=lambda acc, e: acc + e, extra=dh_res, b_cols=(layer,))
    return dh, dw1, dw2


def _local_step(x, target, w, wg, late_weights, exchange):
    n_tok = x.shape[0]
    tm = min(TOKEN_TILE, n_tok)
    n_tile, n_chunk = n_tok // tm, n_tok // CHUNK
    row = lambda v: v.reshape(1, -1)
    big = {n: wg[n].reshape(D_MODEL, D_MODEL) for n in ('rw_wr', 'rw_wk', 'rw_wv', 'rw_wo')}
    ln_g, ln_b = w['ln_g'], w['ln_b']
    grads = {}
    to_slots = lambda t: t.reshape(N_DEV, t.shape[0] // N_DEV, t.shape[1])

    halo_spec = (8, D_MODEL), lambda i: (jnp.maximum(i * (tm // 8) - 1, 0), 0)
    mix_p = [_tile(x, tm), (x,) + halo_spec, _full(w['rw_mu'][0]), _full(row(w['rw_w0'])), _full(w['rw_w1'][0]),
             _full(w['rw_w2'][0]), _full(row(w['rw_a0'])), _full(w['rw_a1'][0]), _full(w['rw_a2'][0]),
             _full(w['rw_g1'][0]), _full(w['rw_g2'][0])]
    tok_out = lambda dt: ((n_tok, D_MODEL), dt, (tm, D_MODEL), lambda i: (i, 0))
    xr, xk, xv, lw, a, g = _fwd_call("rwkv_mix", _mix_fn, (n_tile,), mix_p,
                                     [tok_out(BF16), tok_out(BF16), tok_out(BF16), tok_out(F32), tok_out(F32), tok_out(F32)])
    r = _matmul("rwkv_r", xr, big['rw_wr'], 'nn')
    k = _matmul("rwkv_k", xk, big['rw_wk'], 'nn')
    v = _matmul("rwkv_v", xv, big['rw_wv'], 'nn')

    slab_w = CHUNK_PAIRS * PAIR
    slab = lambda arr: (arr, (CHUNK, slab_w), lambda j, c: (c, j))
    pslab = lambda arr: (arr, (1, slab_w), lambda j, c: (0, j))
    chunk_p = [slab(r), slab(lw), slab(k), slab(v), slab(a), pslab(row(w['rw_k_k'])), pslab(row(w['rw_k_a'])),
               pslab(w['rw_r_k'].reshape(1, -1))]
    st_shape = (n_chunk, N_PAIR, PAIR, PAIR)
    st_out = (st_shape, F32, (1, CHUNK_PAIRS, PAIR, PAIR), lambda j, c: (c, j, 0, 0))
    sl_out = ((n_tok, D_MODEL), F32, (CHUNK, slab_w), lambda j, c: (c, j))
    chunk_grid = (N_PAIR // CHUNK_PAIRS, n_chunk)
    m_all, n_all, q, oi, bonus = _fwd_call("rwkv_chunk", _chunk_fn, chunk_grid, chunk_p,
                                           [st_out, st_out, sl_out, sl_out, sl_out])
    o, zs = _rwkv_state_fwd(m_all, n_all, q, oi)

    gate_p = [_tile(o, tm), _tile(bonus, tm), _tile(g, tm), _full(row(w['rw_lnx_g'])), _full(row(w['rw_lnx_b']))]
    (og,) = _fwd_call("rwkv_gate", _gate_fn, (n_tile,), gate_p, [tok_out(BF16)])
    y0 = _matmul("rwkv_o", og, big['rw_wo'], 'nn')
    ln0_p = [_tile(x, tm), _tile(y0, tm), _full(ln_g[0:1]), _full(ln_b[0:1])]
    (h1,) = _fwd_call("ln0", _res_ln_fn, (n_tile,), ln0_p, [tok_out(F32)])
    late = late_weights(h1)
    w1g, w2g, glu_g = late['mlp_w1'], late['mlp_w2'], late['s5_w_glu']
    pre0, y1 = _mlp_fwd(0, h1, w1g, w2g)
    ln1_p = [_tile(h1, tm), _tile(y1, tm), _full(ln_g[1:2]), _full(ln_b[1:2])]
    (h2,) = _fwd_call("ln1", _res_ln_fn, (n_tile,), ln1_p, [tok_out(F32)])

    gp = (SSM_GROUPS, SSM_STATE)
    par_p = [_full(w['s5_a_re'][0]), _full(w['s5_a_im'][0]), _full(w['s5_log_dt'].reshape(SSM_GROUPS, 1))]
    s5_par_fn = lambda ar, ai, ld: _s5_param_fn(ar, ai, ld, None, None)
    gp_out = (gp, F32, gp, lambda i: (0, 0))
    abar_re, abar_im, coef_re, coef_im = _fwd_call("s5_param", s5_par_fn, (1,), par_p, [gp_out] * 4)
    b_flat = [w[n][0].reshape(SSM_LANES, SSM_GROUP) for n in ('s5_b_re', 's5_b_im')]
    bbar_p = [_full(coef_re.reshape(SSM_LANES, 1)), _full(coef_im.reshape(SSM_LANES, 1)), _full(b_flat[0]), _full(b_flat[1])]
    bb_out = ((SSM_LANES, SSM_GROUP), F32, (SSM_LANES, SSM_GROUP), lambda i: (0, 0))
    bbar_re, bbar_im = _fwd_call("s5_bbar", _s5_bbar_fn, (1,), bbar_p, [bb_out] * 2)
    to_in = lambda t: _block_diag(t.reshape(N_SSM_BLK, 8, SSM_STATE, SSM_GROUP).transpose(0, 1, 3, 2))
    to_out = lambda t: _block_diag(t.reshape(N_SSM_BLK, 8, SSM_GROUP, SSM_STATE).transpose(0, 1, 3, 2))
    bblk_re, bblk_im = to_in(bbar_re), to_in(bbar_im)
    cblk_re, cblk_im = to_out(w['s5_c_re'][0]), to_out(w['s5_c_im'][0])
    a_row_re, a_row_im = abar_re.reshape(1, SSM_LANES), abar_im.reshape(1, SSM_LANES)

    u_slab = lambda arr: (arr, (tm, SSM_BLK_IN), lambda l, i: (i, l))
    s_slab = lambda arr: (arr, (tm, SSM_BLK_ST), lambda l, i: (i, l))
    blk3 = lambda arr: (arr, (1,) + arr.shape[1:], lambda l, i: (l, 0, 0))
    in_p = [u_slab(h2), blk3(bblk_re), blk3(bblk_im)]
    st_tok = ((n_tok, SSM_LANES), F32, (tm, SSM_BLK_ST), lambda l, i: (i, l))
    bu_re, bu_im = _fwd_call("s5_in", _s5_in_fn, (N_SSM_BLK, n_tile), in_p, [st_tok, st_tok])
    s_re, s_im = _s5_scan("s5_scan_fwd", bu_re, bu_im, a_row_re, a_row_im, False)
    d_row = row(w['s5_d'])
    out_p = [s_slab(s_re), s_slab(s_im), u_slab(h2), blk3(cblk_re), blk3(cblk_im),
             (d_row, (1, SSM_BLK_IN), lambda l, i: (0, l))]
    u_tok = lambda dt: ((n_tok, D_MODEL), dt, (tm, SSM_BLK_IN), lambda l, i: (i, l))
    (yg,) = _fwd_call("s5_out", _s5_out_fn, (N_SSM_BLK, n_tile), out_p, [u_tok(BF16)])
    z = _matmul("s5_glu", yg, glu_g, 'nn', b_cols=(0,))
    ln2_p = [_tile(h2, tm), _tile(z, tm), _full(ln_g[2:3]), _full(ln_b[2:3])]
    (h3,) = _fwd_call("ln2", _glu_ln_fn, (n_tile,), ln2_p, [tok_out(F32)])
    pre1, y3 = _mlp_fwd(1, h3, w1g, w2g)
    ln3_p = [_tile(h3, tm), _tile(y3, tm), _full(ln_g[3:4]), _full(ln_b[3:4])]
    (h4,) = _fwd_call("ln3", _res_ln_fn, (n_tile,), ln3_p, [tok_out(F32)])

    loss_acc, dh4 = _loss_call(h4, target, tm)

    ln_w = ['tile', 'tile', 'acc', 'acc']
    dh3_res, dy3, dg3, db3 = _bwd_call("ln3_bwd", _res_ln_fn, (n_tile,), ln3_p, [_tile(dh4, tm)], ln_w)
    dh3, dw1_1, dw2_1 = _mlp_bwd(1, h3, pre1, dy3, dh3_res, w1g, w2g)
    zero = exchange([('mlp_w1', 1, dw1_1), ('mlp_w2', 1, to_slots(dw2_1))])
    ln2_p[2] = _full(ln_g[2:3] + zero)
    dh2_res, dz, dg2, db2 = _bwd_call("ln2_bwd", _glu_ln_fn, (n_tile,), ln2_p, [_tile(dh3, tm)], ln_w)
    dyg = _matmul("s5_glu_dx", dz, glu_g, 'nt', b_cols=(0,))
    dw_glu = _matmul("s5_glu_dw", yg, dz, 'tn', out_cols=True)
    ds_re, ds_im, du_out, dcb_re, dcb_im, dd = _bwd_call(
        "s5_out_bwd", _s5_out_fn, (N_SSM_BLK, n_tile), out_p, [u_slab(dyg)],
        ['tile', 'tile', 'tile', 'acc', 'acc', 'acc'], addends={2: u_slab(dh2_res)})
    gs_re, gs_im = _s5_scan("s5_scan_bwd", ds_re, ds_im, a_row_re, -a_row_im, True)
    s_halo = lambda arr: (arr, (8, SSM_BLK_ST), lambda l, i: (jnp.maximum(i * (tm // 8) - 1, 0), l))
    a_par = lambda arr: (arr, (1, SSM_BLK_ST), lambda l, i: (0, l))
    step_p = [a_par(a_row_re), a_par(a_row_im), s_slab(s_re), s_slab(s_im), s_halo(s_re), s_halo(s_im)]
    dabar_re, dabar_im = _bwd_call("s5_abar_bwd", _s5_step_fn, (N_SSM_BLK, n_tile), step_p,
                                   [s_slab(gs_re), s_slab(gs_im)], ['acc', 'acc', None, None, None, None])
    dh2, dbb_re, dbb_im = _bwd_call("s5_in_bwd", _s5_in_fn, (N_SSM_BLK, n_tile), in_p, [s_slab(gs_re), s_slab(gs_im)],
                                    ['tile', 'acc', 'acc'], addends={0: u_slab(du_out)})
    from_in = lambda t: _block_diag_t(t, SSM_GROUP, SSM_STATE).transpose(0, 1, 3, 2).reshape(SSM_LANES, SSM_GROUP)
    from_out = lambda t: _block_diag_t(t, SSM_STATE, SSM_GROUP).transpose(0, 1, 3, 2).reshape(1, SSM_GROUPS, SSM_GROUP, SSM_STATE)
    grads['s5_c_re'], grads['s5_c_im'] = from_out(dcb_re), from_out(dcb_im)
    grads['s5_d'] = dd
    dcoef_re, dcoef_im, db_re, db_im = _bwd_call(
        "s5_bbar_bwd", _s5_bbar_fn, (1,), bbar_p, [_full(from_in(dbb_re)), _full(from_in(dbb_im))], ['acc'] * 4)
    grads['s5_b_re'] = db_re.reshape(w['s5_b_re'].shape)
    grads['s5_b_im'] = db_im.reshape(w['s5_b_im'].shape)
    par_ct = [_full(dabar_re.reshape(gp)), _full(dabar_im.reshape(gp)), _full(dcoef_re.reshape(gp)), _full(dcoef_im.reshape(gp))]
    da_re, da_im, dlog_dt = _bwd_call("s5_param_bwd", s5_par_fn, (1,), par_p, par_ct, ['acc'] * 3)
    grads['s5_a_re'], grads['s5_a_im'], grads['s5_log_dt'] = da_re[None], da_im[None], dlog_dt.reshape(1, SSM_GROUPS)

    dh1_res, dy1, dg1, db1 = _bwd_call("ln1_bwd", _res_ln_fn, (n_tile,), ln1_p, [_tile(dh2, tm)], ln_w)
    dh1, dw1_0, dw2_0 = _mlp_bwd(0, h1, pre0, dy1, dh1_res, w1g, w2g)
    dx_res, dy0, dg0, db0 = _bwd_call("ln0_bwd", _res_ln_fn, (n_tile,), ln0_p, [_tile(dh1, tm)], ln_w)
    grads['ln_g'] = jnp.concatenate([dg0, dg1, dg2, dg3])
    grads['ln_b'] = jnp.concatenate([db0, db1, db2, db3])
    dog = _matmul("rwkv_o_dx", dy0, big['rw_wo'], 'nt')
    dw_o = to_slots(_matmul("rwkv_o_dw", og, dy0, 'tn'))
    zero = exchange([('s5_w_glu', 0, dw_glu), ('mlp_w1', 0, dw1_0), ('mlp_w2', 0, to_slots(dw2_0)), ('rw_wo', 0, dw_o)])
    gate_p[3] = _full(row(w['rw_lnx_g']) + zero)
    d_o, dbonus, dgate, grads['rw_lnx_g'], grads['rw_lnx_b'] = _bwd_call(
        "rwkv_gate_bwd", _gate_fn, (n_tile,), gate_p, [_tile(dog, tm)], ['tile', 'tile', 'tile', 'acc', 'acc'])
    dq, dm_all, dn_all = _rwkv_state_bwd(m_all, q, zs, d_o)
    st_ct = lambda arr: (arr, (1, CHUNK_PAIRS, PAIR, PAIR), lambda j, c: (c, j, 0, 0))
    dr, dlw, dk, dv, da, grads['rw_k_k'], grads['rw_k_a'], dr_k = _bwd_call(
        "rwkv_chunk_bwd", _chunk_fn, chunk_grid, chunk_p,
        [st_ct(dm_all), st_ct(dn_all), slab(dq), slab(d_o), slab(dbonus)],
        ['tile_bf16', 'tile', 'tile_bf16', 'tile_bf16', 'tile'] + ['acc'] * 3)
    grads['rw_r_k'] = dr_k.reshape(w['rw_r_k'].shape)
    dw_r = to_slots(_matmul("rwkv_r_dw", xr, dr, 'tn'))
    dw_k = to_slots(_matmul("rwkv_k_dw", xk, dk, 'tn'))
    dw_v = to_slots(_matmul("rwkv_v_dw", xv, dv, 'tn'))
    zero = exchange([('rw_wr', 0, dw_r), ('rw_wk', 0, dw_k), ('rw_wv', 0, dw_v)])
    mix_p[3] = _full(row(w['rw_w0']) + zero)
    dxr = _matmul("rwkv_r_dx", dr, big['rw_wr'], 'nt')
    dxk = _matmul("rwkv_k_dx", dk, big['rw_wk'], 'nt')
    dxv = _matmul("rwkv_v_dx", dv, big['rw_wv'], 'nt')
    mix_ct = [_tile(t, tm) for t in (dxr, dxk, dxv, dlw, da, dgate)]
    halo_grad = ('tile', ((n_tile * 8, D_MODEL), (8, D_MODEL), lambda i: (i, 0)))
    res = _bwd_call("rwkv_mix_bwd", _mix_fn, (n_tile,), mix_p, mix_ct, ['tile', halo_grad] + ['acc'] * 9,
                    addends={0: _tile(dx_res, tm)})
    corr = res[1].reshape(n_tile, 8, D_MODEL)[1:, 7:8, :]
    corr = jnp.pad(corr, ((0, 1), (tm - 1, 0), (0, 0)))
    dx = (res[0].reshape(n_tile, tm, D_MODEL) + corr).reshape(n_tok, D_MODEL)
    (grads['rw_mu'], grads['rw_w0'], grads['rw_w1'], grads['rw_w2'], grads['rw_a0'], grads['rw_a1'], grads['rw_a2'],
     grads['rw_g1'], grads['rw_g2']) = [t[None] if t.shape[0] != 1 else t for t in res[2:]]
    return loss_acc[0, 0], dx, grads


PACK_ROWS = 128


def _pack(arrs):
    flat = jnp.concatenate([a.reshape(-1).astype(F32) for a in arrs])
    per = PACK_ROWS * PACK_W
    total = -(-flat.shape[0] // per) * per
    return jnp.pad(flat, (0, total - flat.shape[0])).reshape(-1, PACK_W)


def _unpack(packed, shapes, lead=()):
    flat = packed.reshape(lead + (-1,))
    out, off = [], 0
    for s in shapes:
        n = math.prod(s)
        out.append(flat[..., off:off + n].reshape(lead + tuple(s)))
        off += n
    return out


def _mesh_pos():
    return lax.axis_index("x"), lax.axis_index("y"), lax.axis_index("c")


def _flip(v, f):
    return 1 - v if f else v


def _hbm_call(name, body, arrays, out_shapes, n_sems):
    n_t = len(arrays)
    hbm = pl.BlockSpec(memory_space=pl.ANY)
    return pl.pallas_call(
        body, name=name, out_shape=out_shapes, in_specs=[hbm] * n_t, out_specs=[hbm] * n_t,
        scratch_shapes=[pltpu.SemaphoreType.DMA((n_t, n_sems)), pltpu.SemaphoreType.DMA((n_t, n_sems)),
                        pltpu.SemaphoreType.DMA((n_t,))],
    )(*arrays)


def _all_gather(name, locals_):
    n_t = len(locals_)

    def body(*refs):
        x_refs, out_refs = refs[:n_t], refs[n_t:2 * n_t]
        send_sems, recv_sems, local_sems = refs[2 * n_t:]
        x, y, c = _mesh_pos()
        me, sibling = (x, y, c), (x, y, 1 - c)
        chips = [(1 - x, y), (x, 1 - y), (1 - x, 1 - y)]

        def copy(t, k, block, to, own=False):
            slot = out_refs[t].at[4 * block[0] + 2 * block[1] + block[2]]
            return pltpu.make_async_remote_copy(
                src_ref=x_refs[t] if own else slot, dst_ref=slot,
                send_sem=send_sems.at[t, k], recv_sem=recv_sems.at[t, k],
                device_id=to, device_id_type=pl.DeviceIdType.MESH)

        mine = [pltpu.make_async_copy(x_refs[t], out_refs[t].at[4 * x + 2 * y + c], local_sems.at[t]) for t in range(n_t)]
        for cp in mine:
            cp.start()
        sent = []
        for t in range(n_t):
            sent.append(copy(t, 0, me, sibling, own=True))
            sent += [copy(t, 1 + j, me, (*chip, c), own=True) for j, chip in enumerate(chips)]
        for cp in sent:
            cp.start()
        for j, chip in enumerate(chips):
            for t in range(n_t):
                copy(t, 1 + j, (*chip, c), me).wait_recv()
                passed = copy(t, 4 + j, (*chip, c), sibling)
                passed.start()
                sent.append(passed)
        for t in range(n_t):
            copy(t, 0, sibling, me).wait_recv()
            for j, chip in enumerate(chips):
                copy(t, 4 + j, (*chip, 1 - c), me).wait_recv()
        for cp in sent:
            cp.wait_send()
        for cp in mine:
            cp.wait()

    outs = [jax.ShapeDtypeStruct((N_DEV,) + a.shape, a.dtype) for a in locals_]
    return _hbm_call(name, body, locals_, outs, 7)


def _all_to_all(name, blocks):
    n_t = len(blocks)

    def body(*refs):
        x_refs, out_refs = refs[:n_t], refs[n_t:2 * n_t]
        send_sems, recv_sems, local_sems = refs[2 * n_t:]
        x, y, c = _mesh_pos()
        my_slot = 4 * x + 2 * y + c
        mine = [pltpu.make_async_copy(x_refs[t].at[my_slot], out_refs[t].at[my_slot], local_sems.at[t]) for t in range(n_t)]
        for cp in mine:
            cp.start()
        copies = []
        for k in range(1, N_DEV):
            peer = (_flip(x, k & 4), _flip(y, k & 2), _flip(c, k & 1))
            peer_slot = 4 * peer[0] + 2 * peer[1] + peer[2]
            for t in range(n_t):
                sems = dict(send_sem=send_sems.at[t, k - 1], recv_sem=recv_sems.at[t, k - 1],
                            device_id=peer, device_id_type=pl.DeviceIdType.MESH)
                cp = pltpu.make_async_remote_copy(src_ref=x_refs[t].at[peer_slot], dst_ref=out_refs[t].at[my_slot], **sems)
                cp.start()
                landed = pltpu.make_async_remote_copy(src_ref=x_refs[t].at[my_slot], dst_ref=out_refs[t].at[peer_slot], **sems)
                copies.append((cp, landed))
        for sent, landed in copies:
            landed.wait_recv()
            sent.wait_send()
        for cp in mine:
            cp.wait()

    outs = [jax.ShapeDtypeStruct(a.shape, a.dtype) for a in blocks]
    return _hbm_call(name, body, blocks, outs, 7)


_HBM = pl.BlockSpec(memory_space=pltpu.HBM)
_SEM = pl.BlockSpec(memory_space=pltpu.SEMAPHORE)
_EFFECT = pltpu.SideEffectType.DATAFLOW_SIDE_EFFECTING


def _peer_copies(x_refs, land_refs, send_sems, recv_sems, gather, mine):
    x, y, c = _mesh_pos()
    my_slot = 4 * x + 2 * y + c
    copies = []
    for k in range(1, N_DEV):
        peer = (_flip(x, k & 4), _flip(y, k & 2), _flip(c, k & 1))
        peer_slot = 4 * peer[0] + 2 * peer[1] + peer[2]
        for t, (x_ref, land) in enumerate(zip(x_refs, land_refs)):
            sem = t * (N_DEV - 1) + k - 1
            copies.append(pltpu.make_async_remote_copy(
                src_ref=x_ref if gather else x_ref.at[peer_slot], dst_ref=land.at[my_slot if mine else peer_slot],
                send_sem=send_sems.at[sem], recv_sem=recv_sems.at[sem], device_id=peer, device_id_type=pl.DeviceIdType.MESH))
    return copies


def _push_start(name, arrays, gather):
    n_t = len(arrays)
    lands = [lax.empty((N_DEV,) + a.shape if gather else a.shape, a.dtype) for a in arrays]

    def body(*refs):
        x_refs, land_refs = refs[:n_t], refs[n_t:2 * n_t]
        send_sems, recv_sems = refs[2 * n_t], refs[2 * n_t + 1]
        for sent in _peer_copies(x_refs, land_refs, send_sems, recv_sems, gather, True):
            sent.start()
        refs[-1][...] = jnp.zeros((8, 128), F32)

    held = list(arrays) + lands
    sems = pltpu.SemaphoreType.DMA((n_t * (N_DEV - 1),))
    out = pl.pallas_call(
        body, name=name,
        out_shape=(sems, sems, *[pltpu.HBM(a.shape, a.dtype) for a in held], jax.ShapeDtypeStruct((8, 128), F32)),
        in_specs=[_HBM] * (2 * n_t),
        out_specs=(_SEM, _SEM, *[_HBM] * (2 * n_t), pl.BlockSpec(memory_space=pltpu.VMEM)),
        input_output_aliases={i: 2 + i for i in range(2 * n_t)},
        compiler_params=pltpu.CompilerParams(has_side_effects=_EFFECT),
    )(*[pltpu.with_memory_space_constraint(a, pltpu.HBM) for a in held])
    return dict(send=out[0], recv=out[1], held=out[2:2 + 2 * n_t], zero=out[-1][0, 0], gather=gather)


def _push_wait(name, started, after):
    held, gather = started['held'], started['gather']
    n_t = len(held) // 2

    def body(*refs):
        x_refs, land_refs = refs[:n_t], refs[n_t:2 * n_t]
        send_sems, recv_sems = refs[2 * n_t], refs[2 * n_t + 1]
        for sent in _peer_copies(x_refs, land_refs, send_sems, recv_sems, gather, True):
            sent.wait_send()
        for landed in _peer_copies(x_refs, land_refs, send_sems, recv_sems, gather, False):
            landed.wait_recv()

    out = pl.pallas_call(
        body, name=name,
        out_shape=[pltpu.HBM(a.shape, a.dtype) for a in held],
        in_specs=[_HBM] * (2 * n_t) + [_SEM, _SEM, pl.BlockSpec(memory_space=pl.ANY)],
        out_specs=[_HBM] * (2 * n_t),
        input_output_aliases={i: i for i in range(2 * n_t)},
        compiler_params=pltpu.CompilerParams(has_side_effects=_EFFECT),
    )(*held, started['send'], started['recv'], after)
    return out[:n_t], out[n_t:]


def _adam_call(name, slots, w, m, v, own=None):
    rows, width = w.shape
    tile = min(rows, (PACK_ROWS * PACK_W) // width)
    c1 = 1.0 / (1.0 - ADAM_B1 ** ADAM_STEP)
    c2 = 1.0 / (1.0 - ADAM_B2 ** ADAM_STEP)

    def body(*refs):
        if own is None:
            s_ref, w_ref, m_ref, v_ref, g_ref, d_ref, nm_ref, nv_ref = refs
            part = lambda i: s_ref[i]
        else:
            s_ref, own_ref, w_ref, m_ref, v_ref, g_ref, d_ref, nm_ref, nv_ref = refs
            x, y, c = _mesh_pos()
            part = lambda i: jnp.where(4 * x + 2 * y + c == i, own_ref[...], s_ref[i])
        g = part(0)
        for i in range(1, N_DEV):
            g = g + part(i)
        m_new = ADAM_B1 * m_ref[...] + (1.0 - ADAM_B1) * g
        v_new = ADAM_B2 * v_ref[...] + (1.0 - ADAM_B2) * (g * g)
        g_ref[...] = g
        nm_ref[...] = m_new
        nv_ref[...] = v_new
        d_ref[...] = -ADAM_LR * ((m_new * c1) / (jnp.sqrt(v_new * c2) + ADAM_EPS) + ADAM_WD * w_ref[...])

    blk = _spec((tile, width), lambda i: (i, 0))
    ops = [slots] + ([] if own is None else [own]) + [w, m, v]
    return pl.pallas_call(
        body, name=name, grid=(rows // tile,),
        in_specs=[_spec((N_DEV, tile, width), lambda i: (0, i, 0))] + [blk] * (len(ops) - 1),
        out_specs=[blk] * 4, out_shape=[jax.ShapeDtypeStruct(w.shape, F32)] * 4,
        compiler_params=_cparams(1),
    )(*ops)


def kernel(x, ln_g, ln_b, rw_mu, rw_w0, rw_w1, rw_w2, rw_a0, rw_a1, rw_a2, rw_g1, rw_g2, rw_k_k, rw_k_a, rw_r_k, rw_wr, rw_wk, rw_wv, rw_wo, rw_lnx_g, rw_lnx_b, s5_a_re, s5_a_im, s5_log_dt, s5_b_re, s5_b_im, s5_c_re, s5_c_im, s5_d, s5_w_glu, mlp_w1, mlp_w2, loss_target, m_ln_g, m_ln_b, m_rw_mu, m_rw_w0, m_rw_w1, m_rw_w2, m_rw_a0, m_rw_a1, m_rw_a2, m_rw_g1, m_rw_g2, m_rw_k_k, m_rw_k_a, m_rw_r_k, m_rw_wr, m_rw_wk, m_rw_wv, m_rw_wo, m_rw_lnx_g, m_rw_lnx_b, m_s5_a_re, m_s5_a_im, m_s5_log_dt, m_s5_b_re, m_s5_b_im, m_s5_c_re, m_s5_c_im, m_s5_d, m_s5_w_glu, m_mlp_w1, m_mlp_w2, v_ln_g, v_ln_b, v_rw_mu, v_rw_w0, v_rw_w1, v_rw_w2, v_rw_a0, v_rw_a1, v_rw_a2, v_rw_g1, v_rw_g2, v_rw_k_k, v_rw_k_a, v_rw_r_k, v_rw_wr, v_rw_wk, v_rw_wv, v_rw_wo, v_rw_lnx_g, v_rw_lnx_b, v_s5_a_re, v_s5_a_im, v_s5_log_dt, v_s5_b_re, v_s5_b_im, v_s5_c_re, v_s5_c_im, v_s5_d, v_s5_w_glu, v_mlp_w1, v_mlp_w2):
    given = dict(locals())
    local_w = {n: given[n] for n in WEIGHTS}
    local_m = {n: given["m_" + n] for n in WEIGHTS}
    local_v = {n: given["v_" + n] for n in WEIGHTS}
    small = [n for n in WEIGHTS if n in SHARDED and n not in BIG]
    small_shapes = [local_w[n].shape for n in small]
    rep_shapes = [local_w[n].shape for n in REPLICATED]

    my_slot = 4 * lax.axis_index("x") + 2 * lax.axis_index("y") + lax.axis_index("c")
    as_bf16 = {n: local_w[n].astype(BF16) for n in BIG}

    late_started = _push_start("weights_late_start", [as_bf16[n] for n in LATE], gather=True)
    early = [n for n in BIG if n not in LATE]
    gathered = _all_gather("weights_all_gather", [as_bf16[n] for n in early] + [_pack([local_w[n] for n in small])])
    early_gathered = dict(zip(early, gathered))
    full = dict(local_w)
    for n, blk in zip(small, _unpack(gathered[-1], small_shapes, lead=(N_DEV,))):
        full[n] = jnp.concatenate([blk[i] for i in range(N_DEV)], axis=SHARDED[n])
    full['rw_w0'] = full['rw_w0'] + late_started['zero']

    def late_weights(after):
        _, lands = _push_wait("weights_late_wait", late_started, after)
        own = lambda n: as_bf16[n][None]
        return {n: lax.dynamic_update_slice(land, own(n), (my_slot,) + (0,) * as_bf16[n].ndim) for n, land in zip(LATE, lands)}

    exchanges = []

    def exchange(entries):
        started = _push_start(f"grads_start_{len(exchanges)}", [t for _, _, t in entries], gather=False)
        exchanges.append((entries, started))
        return started['zero']

    loss_sq, dx, grads = _local_step(x[0], loss_target[0], full, early_gathered, late_weights, exchange)
    loss = (0.5 / D_MODEL) * lax.psum(loss_sq, MESH_AXES)

    by_name = {}
    for e, (entries, started) in enumerate(exchanges):
        sources, lands = _push_wait(f"grads_wait_{e}", started, dx)
        for (n, i, _), src, slots in zip(entries, sources, lands):
            part = lambda d: d[n].reshape((-1,) + slots.shape[1:])[i]
            own = lax.dynamic_index_in_dim(src, my_slot, 0, keepdims=False)
            out = _adam_call(f"adamw_{n}_{i}", slots, part(local_w), part(local_m), part(local_v), own=own)
            by_name.setdefault(n, {})[i] = out
    for n in BIG:
        by_name[n] = [by_name[n][i] for i in sorted(by_name[n])]
    parts = [jnp.stack(jnp.split(grads[n], N_DEV, axis=SHARDED[n])) for n in small]
    small_send = jnp.stack([_pack([p[i] for p in parts]) for i in range(N_DEV)])
    recv = _all_to_all("grads_all_to_all", [small_send])
    for n in BIG:
        by_name[n] = [jnp.stack(ts).reshape(local_w[n].shape) for ts in zip(*by_name[n])]
    small_out = _adam_call("adamw_small", recv[-1], _pack([local_w[n] for n in small]),
                           _pack([local_m[n] for n in small]), _pack([local_v[n] for n in small]))
    (rep_all,) = _all_gather("grads_all_gather", [_pack([grads[n] for n in REPLICATED])])
    rep_out = _adam_call("adamw_replicated", rep_all, _pack([local_w[n] for n in REPLICATED]),
                         _pack([local_m[n] for n in REPLICATED]), _pack([local_v[n] for n in REPLICATED]))
    for k, (small_packed, rep_packed) in enumerate(zip(small_out, rep_out)):
        for n, t in zip(small, _unpack(small_packed, small_shapes)):
            by_name.setdefault(n, [None] * 4)[k] = t
        for n, t in zip(REPLICATED, _unpack(rep_packed, rep_shapes)):
            by_name.setdefault(n, [None] * 4)[k] = t
    results = [by_name[n][k] for k in range(4) for n in WEIGHTS]
    return (loss, dx[None], *results)
```

```python
import functools
import math

import jax
import jax.numpy as jnp
from jax import lax
from jax.experimental import pallas as pl
from jax.experimental.pallas import tpu as pltpu

F32 = jnp.float32
BF16 = jnp.bfloat16
HI = lax.Precision.HIGHEST

D_MODEL = 1024
HEAD = 64
PAIR = 2 * HEAD
N_PAIR = D_MODEL // PAIR
CHUNK = 64
GN_EPS = 64e-5
LN_EPS = 1e-5
SSM_GROUP = 16
SSM_STATE = 64
SSM_GROUPS = D_MODEL // SSM_GROUP
SSM_LANES = SSM_GROUPS * SSM_STATE
SSM_BLK_IN = 128
SSM_BLK_ST = 512
N_SSM_BLK = D_MODEL // SSM_BLK_IN
DEPTH = 2
DN_ALPHA = (2.0 * DEPTH) ** 0.25
ADAM_LR, ADAM_B1, ADAM_B2, ADAM_EPS, ADAM_WD, ADAM_STEP = 0.001, 0.9, 0.999, 1e-08, 0.01, 10
N_DEV = 8
MESH_AXES = ("x", "y", "c")
VMEM_LIMIT = 56 * 1024 * 1024
TOKEN_TILE = 256
PACK_W = 1024

SHARDED = {
    "rw_mu": 2, "rw_w1": 1, "rw_w2": 2, "rw_a1": 1, "rw_a2": 2, "rw_g1": 1, "rw_g2": 2,
    "rw_wr": 1, "rw_wk": 1, "rw_wv": 1, "rw_wo": 1, "s5_d": 1, "s5_w_glu": 2, "mlp_w1": 2, "mlp_w2": 1,
}
WEIGHTS = ['ln_g', 'ln_b', 'rw_mu', 'rw_w0', 'rw_w1', 'rw_w2', 'rw_a0', 'rw_a1', 'rw_a2', 'rw_g1', 'rw_g2', 'rw_k_k',
           'rw_k_a', 'rw_r_k', 'rw_wr', 'rw_wk', 'rw_wv', 'rw_wo', 'rw_lnx_g', 'rw_lnx_b', 's5_a_re', 's5_a_im',
           's5_log_dt', 's5_b_re', 's5_b_im', 's5_c_re', 's5_c_im', 's5_d', 's5_w_glu', 'mlp_w1', 'mlp_w2']
REPLICATED = [n for n in WEIGHTS if n not in SHARDED]
BIG = ['rw_wr', 'rw_wk', 'rw_wv', 'rw_wo', 's5_w_glu', 'mlp_w1', 'mlp_w2']
LATE = ['s5_w_glu', 'mlp_w1', 'mlp_w2']


def _cparams(n_grid):
    return pltpu.CompilerParams(dimension_semantics=("arbitrary",) * n_grid, vmem_limit_bytes=VMEM_LIMIT)


@jax.custom_vjp
def _mm(x, w):
    return jnp.dot(x.astype(BF16), w.astype(BF16), preferred_element_type=F32)


def _mm_fwd(x, w):
    return _mm(x, w), (x, w)


def _mm_bwd(res, dy):
    x, w = res
    dyb = dy.astype(BF16)
    dx = lax.dot_general(dyb, w.astype(BF16), (((1,), (1,)), ((), ())), preferred_element_type=F32)
    dw = lax.dot_general(x.astype(BF16), dyb, (((0,), (0,)), ((), ())), preferred_element_type=F32)
    return dx, dw


_mm.defvjp(_mm_fwd, _mm_bwd)


_DOT_DIMS = {'nn': (((1,), (0,)), ((), ())), 'nt': (((1,), (1,)), ((), ())), 'tn': (((0,), (0,)), ((), ()))}
_BATCH_DOT_DIMS = {'nn': (((2,), (1,)), ((0,), (0,))), 'nt': (((2,), (2,)), ((0,), (0,))), 'tn': (((1,), (1,)), ((0,), (0,)))}
CHUNK_PAIRS = 8
CHUNK_PASSES = 1


def _split_bf16(a):
    hi = a.astype(BF16)
    return hi, (a - hi.astype(F32)).astype(BF16)


def _dot_passes(a, b, mode, passes=None):
    dims = (_DOT_DIMS if a.ndim == 2 else _BATCH_DOT_DIMS)[mode]
    dot = lambda p, q: lax.dot_general(p, q, dims, preferred_element_type=F32)
    if (CHUNK_PASSES if passes is None else passes) == 1:
        return dot(a.astype(BF16), b.astype(BF16))
    (ah, al), (bh, bl) = _split_bf16(a), _split_bf16(b)
    return dot(ah, bh) + (dot(ah, bl) + dot(al, bh))


@functools.partial(jax.custom_vjp, nondiff_argnums=(2,))
def _pdot(a, b, mode):
    return _dot_passes(a, b, mode)


def _pdot_fwd(a, b, mode):
    return _dot_passes(a, b, mode), (a, b)


def _pdot_bwd(mode, res, dy):
    a, b = res
    if mode == 'nn':
        return _dot_passes(dy, b, 'nt'), _dot_passes(a, dy, 'tn')
    if mode == 'nt':
        return _dot_passes(dy, b, 'nn'), _dot_passes(dy, a, 'tn')
    return _dot_passes(b, dy, 'nt'), _dot_passes(a, dy, 'nn')


_pdot.defvjp(_pdot_fwd, _pdot_bwd)


def _tri_sum(x, upper):
    nb, n = x.shape[0], x.shape[1]
    ti = lax.broadcasted_iota(jnp.int32, (nb, n, n), 1)
    tj = lax.broadcasted_iota(jnp.int32, (nb, n, n), 2)
    tri = ((ti <= tj) if upper else (ti >= tj)).astype(BF16)
    hi = x.astype(BF16)
    rest = x - hi.astype(F32)
    mid = rest.astype(BF16)
    lo = (rest - mid.astype(F32)).astype(BF16)
    dot = lambda q: lax.dot_general(tri, q, _BATCH_DOT_DIMS['nn'], preferred_element_type=F32)
    return dot(hi) + (dot(mid) + dot(lo))


@jax.custom_vjp
def _cumsum_rows(x):
    return _tri_sum(x, False)


_cumsum_rows.defvjp(lambda x: (_tri_sum(x, False), None), lambda _, dy: (_tri_sum(dy, True),))


@functools.partial(jax.custom_vjp, nondiff_argnums=(1,))
def _roll_rows(x, shift):
    return pltpu.roll(x, shift, 0)


def _roll_rows_fwd(x, shift):
    return pltpu.roll(x, shift, 0), None


def _roll_rows_bwd(shift, _, dy):
    n = dy.shape[0]
    return (pltpu.roll(dy, (n - shift) % n, 0),)


_roll_rows.defvjp(_roll_rows_fwd, _roll_rows_bwd)


def _shift_down(x, first_row):
    row = lax.broadcasted_iota(jnp.int32, x.shape, 0)
    return jnp.where(row == 0, first_row, _roll_rows(x, 1))


def _sigmoid(x):
    return 1.0 / (1.0 + jnp.exp(-x))


def _softplus(x):
    return jnp.maximum(x, 0.0) + jnp.log(1.0 + jnp.exp(-jnp.abs(x)))


def _gelu(x):
    return 0.5 * x * (1.0 + jnp.tanh(math.sqrt(2.0 / math.pi) * (x + 0.044715 * (x * x * x))))


def _layer_norm(x, g, b):
    mu = jnp.mean(x, axis=-1, keepdims=True)
    xc = x - mu
    var = jnp.mean(xc * xc, axis=-1, keepdims=True)
    return xc * lax.rsqrt(var + LN_EPS) * g + b


def _spec(block, index_map):
    return pl.BlockSpec(block, index_map)


def _tile(arr, tm):
    return (arr, (tm, arr.shape[1]), lambda i: (i, 0))


def _full(arr):
    nd = arr.ndim
    return (arr, arr.shape, lambda *_: (0,) * nd)


def _fwd_call(name, fn, grid, prims, outs):
    n_in = len(prims)

    def body(*refs):
        vals = [r[...] for r in refs[:n_in]]
        vals = [v.astype(F32) if v.dtype != F32 else v for v in vals]
        res = fn(*vals)
        for o, r in zip(refs[n_in:], res):
            o[...] = r.astype(o.dtype)

    return pl.pallas_call(
        body, name=name, grid=grid,
        in_specs=[_spec(b, m) for _, b, m in prims],
        out_specs=[_spec(b, m) for _, _, b, m in outs],
        out_shape=[jax.ShapeDtypeStruct(s, d) for s, d, _, _ in outs],
        compiler_params=_cparams(len(grid)),
    )(*[a for a, _, _ in prims])


def _bwd_call(name, fn, grid, prims, cts, wants, addends=None):
    addends = addends or {}
    n_in, n_ct = len(prims), len(cts)
    out_place = {}
    wants, out_dtype = list(wants), {}
    for i, w in enumerate(wants):
        if isinstance(w, tuple):
            wants[i], out_place[i] = w
        elif w == 'tile_bf16':
            wants[i], out_dtype[i] = 'tile', BF16
    diff = [i for i, w in enumerate(wants) if w]
    add_idx = [i for i in diff if i in addends]
    last_axis = len(grid) - 1

    def body(*refs):
        vals = [r[...] for r in refs[:n_in]]
        vals = [v.astype(F32) if v.dtype != F32 else v for v in vals]
        ct_vals = [r[...] for r in refs[n_in:n_in + n_ct]]
        ct_vals = [v.astype(F32) if v.dtype != F32 else v for v in ct_vals]
        add_refs = dict(zip(add_idx, refs[n_in + n_ct:n_in + n_ct + len(add_idx)]))
        out_refs = refs[n_in + n_ct + len(add_idx):]

        def f(*dargs):
            full = list(vals)
            for i, a in zip(diff, dargs):
                full[i] = a
            return tuple(fn(*full))

        _, vjp = jax.vjp(f, *[vals[i] for i in diff])
        grads = vjp(tuple(ct_vals))
        first = pl.program_id(last_axis) == 0
        for o, g, i in zip(out_refs, grads, diff):
            if wants[i] == 'tile':
                if i in add_refs:
                    g = g + add_refs[i][...]
                o[...] = g.astype(o.dtype)
            else:
                @pl.when(first)
                def _(o=o):
                    o[...] = jnp.zeros(o.shape, o.dtype)
                o[...] += g

    ins = list(prims) + list(cts) + [addends[i] for i in add_idx]
    places = [out_place.get(i, (prims[i][0].shape, prims[i][1], prims[i][2])) for i in diff]
    return pl.pallas_call(
        body, name=name, grid=grid,
        in_specs=[_spec(b, m) for _, b, m in ins],
        out_specs=[_spec(b, m) for _, b, m in places],
        out_shape=[jax.ShapeDtypeStruct(s, out_dtype.get(i, F32)) for i, (s, _, _) in zip(diff, places)],
        compiler_params=_cparams(len(grid)),
    )(*[a for a, _, _ in ins])


def _matmul(name, a, b, mode, *, out_dtype=F32, lhs_fn=None, epi=None, extra=None, tm=1024, tn=1024, tk=1024,
            b_cols=None, b_rows=None, out_cols=False):
    if mode == 'tn':
        kdim, m = a.shape
    else:
        m, kdim = a.shape
    if b_rows is not None:
        loc, cols = b.shape[-2], b.shape[-1]
        if mode == 'nn':
            n, tk = cols, loc
        else:
            n, tn = N_DEV * loc, loc
    elif b_cols is None:
        n = b.shape[0] if mode == 'nt' else b.shape[1]
    else:
        rows, loc = b.shape[-2], b.shape[-1]
        if mode == 'nn':
            n, tn = N_DEV * loc, loc
        else:
            n, tk = rows, loc
    if out_cols:
        tn = n // N_DEV
    tm, tn, tk = min(tm, m), min(tn, n), min(tk, kdim)
    nk = kdim // tk
    grid = (m // tm, n // tn, nk)
    a_spec = _spec((tk, tm), lambda i, j, k: (k, i)) if mode == 'tn' else _spec((tm, tk), lambda i, j, k: (i, k))
    if b_rows is not None:
        lead = tuple(b_rows)
        skip = (None,) * (1 + len(lead))
        if mode == 'nn':
            b_spec = _spec(skip + (tk, tn), lambda i, j, k: (k,) + lead + (0, j))
        else:
            b_spec = _spec(skip + (tn, tk), lambda i, j, k: (j,) + lead + (0, k))
    elif b_cols is None:
        b_spec = _spec((tn, tk), lambda i, j, k: (j, k)) if mode == 'nt' else _spec((tk, tn), lambda i, j, k: (k, j))
    else:
        lead = tuple(b_cols)
        skip = (None,) * (1 + len(lead))
        if mode == 'nn':
            b_spec = _spec(skip + (tk, tn), lambda i, j, k: (j,) + lead + (k, 0))
        else:
            b_spec = _spec(skip + (tn, tk), lambda i, j, k: (k,) + lead + (j, 0))
    o_spec = _spec((tm, tn), lambda i, j, k: (i, j))
    if out_cols:
        o_place, o_shape = _spec((None, tm, tn), lambda i, j, k: (j, i, 0)), (N_DEV, m, tn)
    else:
        o_place, o_shape = o_spec, (m, n)
    dims = _DOT_DIMS[mode]

    def body(*refs):
        if extra is None:
            a_ref, b_ref, o_ref, acc = refs
            x_ref = None
        else:
            a_ref, b_ref, x_ref, o_ref, acc = refs
        k = pl.program_id(2)
        av = a_ref[...]
        if lhs_fn is not None:
            av = lhs_fn(av.astype(F32))
        part = lax.dot_general(av.astype(BF16), b_ref[...].astype(BF16), dims, preferred_element_type=F32)

        def finish(r):
            if epi is not None:
                r = epi(r, x_ref[...])
            o_ref[...] = r.astype(o_ref.dtype)

        if nk == 1:
            finish(part)
            return

        @pl.when(k == 0)
        def _():
            acc[...] = part

        @pl.when((k > 0) & (k < nk - 1))
        def _():
            acc[...] += part

        @pl.when(k == nk - 1)
        def _():
            finish(acc[...] + part)

    ops, specs = [a, b], [a_spec, b_spec]
    if extra is not None:
        ops.append(extra)
        specs.append(o_spec)
    return pl.pallas_call(
        body, name=name, grid=grid, in_specs=specs, out_specs=o_place,
        out_shape=jax.ShapeDtypeStruct(o_shape, out_dtype),
        scratch_shapes=[pltpu.VMEM((tm, tn), F32)],
        compiler_params=_cparams(3),
    )(*ops)


def _relu2(x):
    r = jnp.maximum(x, 0.0)
    return r * r


def _mix_fn(x, halo, mu, w0, w1, w2, a0, a1, a2, g1, g2):
    prev_row = jnp.where(pl.program_id(0) == 0, 0.0, halo[7:8, :])
    xx = _shift_down(x, prev_row) - x
    xr, xw, xk, xv, xa, xg = (x + xx * mu[i:i + 1, :] for i in range(6))
    w_pre = w0 + _mm(jnp.tanh(_mm(xw, w1)), w2)
    log_decay = -jnp.exp(-_softplus(-w_pre) - 0.5)
    a = _sigmoid(a0 + _mm(_mm(xa, a1), a2))
    g = _mm(_sigmoid(_mm(xg, g1)), g2)
    return xr, xk, xv, log_decay, a, g


def _chunk_fn(r, lw, k, v, a, k_k, k_a, r_k):
    n, n_pair = r.shape[0], r.shape[1] // PAIR
    to_batch = lambda t: jnp.concatenate([t[None, :, p * PAIR:(p + 1) * PAIR] for p in range(n_pair)], axis=0)
    from_batch = lambda t: jnp.concatenate([t[p] for p in range(n_pair)], axis=1)
    r, lw, k, v, a, k_k, k_a, r_k = (to_batch(t) for t in (r, lw, k, v, a, k_k, k_a, r_k))
    lane = lax.broadcasted_iota(jnp.int32, (1, 1, PAIR), 2)
    m0 = lane < HEAD

    def head_sum(t):
        s0 = jnp.sum(jnp.where(m0, t, 0.0), axis=2, keepdims=True)
        s1 = jnp.sum(jnp.where(m0, 0.0, t), axis=2, keepdims=True)
        return jnp.where(m0, s0, s1)

    kk = k * k_k
    kk = kk / jnp.maximum(jnp.sqrt(head_sum(kk * kk)), 1e-12)
    k2 = k * (1.0 + (a - 1.0) * k_a)
    aa, bb = -kk, kk * a
    bonus = head_sum(r * k2 * r_k) * v

    cum = _cumsum_rows(lw)
    p_in, p_ex, p_inv = jnp.exp(cum), jnp.exp(cum - lw), jnp.exp(-cum)
    at, rt, kt, bt = aa * p_ex, r * p_in, k2 * p_inv, bb * p_inv

    def stack_masked(t):
        return jnp.concatenate([jnp.where(m0, t, 0.0), jnp.where(m0, 0.0, t)], axis=1)

    def unstack_sum(t):
        return t[:, :n] + t[:, n:]

    def unstack_select(t):
        return jnp.where(m0, t[:, :n], t[:, n:])

    ti = lax.broadcasted_iota(jnp.int32, (1, 2 * n, 2 * n), 1)
    tj = lax.broadcasted_iota(jnp.int32, (1, 2 * n, 2 * n), 2)
    same = (ti >= n) == (tj >= n)
    incl, strict = same & (ti >= tj), same & (ti > tj)
    eye_n = (ti == tj).astype(F32)
    at_s, rt_s, kt_s, bt_s = stack_masked(at), stack_masked(rt), stack_masked(kt), stack_masked(bt)
    v_s = jnp.concatenate([v, v], axis=1)
    a_ab = jnp.where(strict, _pdot(at_s, bt_s, 'nt'), 0.0)
    a_ak = jnp.where(strict, _pdot(at_s, kt_s, 'nt'), 0.0)
    a_rb = jnp.where(incl, _pdot(rt_s, bt_s, 'nt'), 0.0)
    a_rk = jnp.where(incl, _pdot(rt_s, kt_s, 'nt'), 0.0)
    inv, pw = eye_n + a_ab, a_ab
    for _ in range(int(math.log2(n)) - 1):
        pw = _pdot(pw, pw, 'nn')
        inv = inv + _pdot(inv, pw, 'nn')
    w_s = _pdot(inv, at_s, 'nn')
    y_s = _pdot(inv, _pdot(a_ak, v_s, 'nn'), 'nn')
    q_all = unstack_sum(rt_s + _pdot(a_rb, w_s, 'nn'))
    oi_all = unstack_select(_pdot(a_rb, y_s, 'nn') + _pdot(a_rk, v_s, 'nn'))
    y_m = stack_masked(unstack_select(y_s))

    mm = _pdot(bt_s, w_s, 'tn')
    nn = _pdot(bt_s, y_m, 'tn') + _pdot(kt_s, stack_masked(v), 'tn')
    ei = lax.broadcasted_iota(jnp.int32, (1, PAIR, PAIR), 1)
    ej = lax.broadcasted_iota(jnp.int32, (1, PAIR, PAIR), 2)
    eye_p = (ei == ej).astype(F32)
    decay_col = jnp.sum(eye_p * p_in[:, n - 1:n, :], axis=2, keepdims=True)
    return ((decay_col * (eye_p + mm))[None], (decay_col * nn)[None], from_batch(q_all), from_batch(oi_all),
            from_batch(bonus))


def _gate_fn(o, bonus, g, lnx_g, lnx_b):
    lane = lax.broadcasted_iota(jnp.int32, (1, PAIR), 1)
    m0 = lane < HEAD

    def head_mean(t):
        s0 = jnp.sum(jnp.where(m0, t, 0.0), axis=1, keepdims=True)
        s1 = jnp.sum(jnp.where(m0, 0.0, t), axis=1, keepdims=True)
        return jnp.where(m0, s0, s1) * (1.0 / HEAD)

    outs = []
    for p in range(N_PAIR):
        sl = slice(p * PAIR, (p + 1) * PAIR)
        oc = o[:, sl] - head_mean(o[:, sl])
        on = oc * lax.rsqrt(head_mean(oc * oc) + GN_EPS) * lnx_g[:, sl] + lnx_b[:, sl]
        outs.append((on + bonus[:, sl]) * g[:, sl])
    return (jnp.concatenate(outs, axis=1),)


def _res_ln_fn(h, y, g, b):
    return (_layer_norm(DN_ALPHA * h + y, g, b),)


def _glu_ln_fn(h, z, g, b):
    mix = z[:, :D_MODEL] * _sigmoid(z[:, D_MODEL:])
    return (_layer_norm(DN_ALPHA * h + mix, g, b),)


def _s5_param_fn(a_re, a_im, log_dt, b_re, b_im):
    dt = jnp.exp(log_dt)
    lam_re = jnp.minimum(a_re, -1e-4)
    lam_im = a_im
    mag = jnp.exp(dt * lam_re)
    abar_re = mag * jnp.cos(dt * lam_im)
    abar_im = mag * jnp.sin(dt * lam_im)
    den = lam_re * lam_re + lam_im * lam_im
    nr, ni = abar_re - 1.0, abar_im
    coef_re = (nr * lam_re + ni * lam_im) / den
    coef_im = (ni * lam_re - nr * lam_im) / den
    return abar_re, abar_im, coef_re, coef_im


def _s5_bbar_fn(coef_re, coef_im, b_re, b_im):
    return coef_re * b_re - coef_im * b_im, coef_re * b_im + coef_im * b_re


def _s5_in_fn(u, b_re, b_im):
    return _pdot(u, b_re[0], 'nn'), _pdot(u, b_im[0], 'nn')


def _s5_out_fn(s_re, s_im, u, c_re, c_im, d):
    y = _pdot(s_re, c_re[0], 'nn') - _pdot(s_im, c_im[0], 'nn') + u * d
    return (_gelu(y),)


def _rwkv_state_fwd(m_all, n_all, q, oi):
    n_chunk = m_all.shape[0]

    def body(m_ref, n_ref, q_ref, oi_ref, o_ref, zs_ref, z):
        @pl.when(pl.program_id(0) == 0)
        def _():
            z[...] = jnp.zeros(z.shape, F32)

        for j in range(N_PAIR):
            sl = slice(j * PAIR, (j + 1) * PAIR)
            zj = z[j]
            zs_ref[0, j] = zj
            o_ref[:, sl] = _dot_passes(q_ref[:, sl], zj, 'nn', 3) + oi_ref[:, sl]
            z[j] = _dot_passes(m_ref[0, j], zj, 'nn', 3) + n_ref[0, j]

    st = _spec((1, N_PAIR, PAIR, PAIR), lambda c: (c, 0, 0, 0))
    tok = _spec((CHUNK, D_MODEL), lambda c: (c, 0))
    return pl.pallas_call(
        body, name="rwkv_state_fwd", grid=(n_chunk,),
        in_specs=[st, st, tok, tok], out_specs=[tok, st],
        out_shape=[jax.ShapeDtypeStruct(q.shape, F32), jax.ShapeDtypeStruct(m_all.shape, F32)],
        scratch_shapes=[pltpu.VMEM((N_PAIR, PAIR, PAIR), F32)],
        compiler_params=_cparams(1),
    )(m_all, n_all, q, oi)


def _rwkv_state_bwd(m_all, q, zs, d_o):
    n_chunk = m_all.shape[0]

    def body(m_ref, q_ref, zs_ref, do_ref, dq_ref, dm_ref, dn_ref, g):
        @pl.when(pl.program_id(0) == 0)
        def _():
            g[...] = jnp.zeros(g.shape, F32)

        bi = lax.broadcasted_iota(jnp.int32, (PAIR, PAIR), 0) // HEAD
        bj = lax.broadcasted_iota(jnp.int32, (PAIR, PAIR), 1) // HEAD
        same_head = bi == bj
        for j in range(N_PAIR):
            sl = slice(j * PAIR, (j + 1) * PAIR)
            gj, zj, do = g[j], zs_ref[0, j], do_ref[:, sl]
            dn_ref[0, j] = gj
            dm_ref[0, j] = _dot_passes(gj, zj, 'nt', 3)
            dq_ref[:, sl] = _dot_passes(do, zj, 'nt', 3)
            g[j] = _dot_passes(m_ref[0, j], gj, 'tn', 3) + jnp.where(same_head, _dot_passes(q_ref[:, sl], do, 'tn', 3), 0.0)

    st = _spec((1, N_PAIR, PAIR, PAIR), lambda c: (n_chunk - 1 - c, 0, 0, 0))
    tok = _spec((CHUNK, D_MODEL), lambda c: (n_chunk - 1 - c, 0))
    return pl.pallas_call(
        body, name="rwkv_state_bwd", grid=(n_chunk,),
        in_specs=[st, tok, st, tok], out_specs=[tok, st, st],
        out_shape=[jax.ShapeDtypeStruct(q.shape, F32), jax.ShapeDtypeStruct(m_all.shape, F32),
                   jax.ShapeDtypeStruct(m_all.shape, F32)],
        scratch_shapes=[pltpu.VMEM((N_PAIR, PAIR, PAIR), F32)],
        compiler_params=_cparams(1),
    )(m_all, q, zs, d_o)


S5_TIME_TILE = 512
S5_GROUP = 8


def _scan_rows(re_ref, im_ref, a1, carry, reverse):
    lb = re_ref.shape[1]
    grp, n_grp = S5_GROUP, re_ref.shape[0] // S5_GROUP

    def cmul(xr, xi, yr, yi):
        return xr * yr - xi * yi, xr * yi + xi * yr

    a2 = cmul(*a1, *a1)
    a4 = cmul(*a2, *a2)
    a8 = cmul(*a4, *a4)
    row = lax.broadcasted_iota(jnp.int32, (grp, lb), 0)
    expo = (grp - row) if reverse else (row + 1)
    pw = (jnp.ones((grp, lb), F32), jnp.zeros((grp, lb), F32))
    for bit, ap in ((1, a1), (2, a2), (4, a4), (8, a8)):
        nxt = cmul(*pw, *ap)
        sel = (expo & bit) != 0
        pw = (jnp.where(sel, nxt[0], pw[0]), jnp.where(sel, nxt[1], pw[1]))

    def group(i, c):
        gi = (n_grp - 1 - i) if reverse else i
        rows = pl.ds(pl.multiple_of(gi * grp, grp), grp)
        xr, xi = re_ref[rows, :], im_ref[rows, :]
        for d, ad in ((1, a1), (2, a2), (4, a4)):
            if reverse:
                keep = row < grp - d
                sr, si = pltpu.roll(xr, grp - d, 0), pltpu.roll(xi, grp - d, 0)
            else:
                keep = row >= d
                sr, si = pltpu.roll(xr, d, 0), pltpu.roll(xi, d, 0)
            pr, pi = cmul(*ad, jnp.where(keep, sr, 0.0), jnp.where(keep, si, 0.0))
            xr, xi = xr + pr, xi + pi
        cr, ci = cmul(*pw, c[0], c[1])
        xr, xi = xr + cr, xi + ci
        re_ref[rows, :] = xr
        im_ref[rows, :] = xi
        edge = slice(0, 1) if reverse else slice(grp - 1, grp)
        return xr[edge, :], xi[edge, :]

    return lax.fori_loop(0, n_grp, group, carry)


def _s5_fwd(u, b_re, b_im, a_re, a_im, c_re, c_im, d_row):
    n_tok = u.shape[0]
    tt = min(S5_TIME_TILE, n_tok)

    def body(u_ref, bre_ref, bim_ref, are_ref, aim_ref, cre_ref, cim_ref, d_ref, sre_ref, sim_ref, yg_ref, carry):
        @pl.when(pl.program_id(1) == 0)
        def _():
            carry[...] = jnp.zeros(carry.shape, F32)

        uv = u_ref[...]
        sre_ref[...], sim_ref[...] = _s5_in_fn(uv, bre_ref[...], bim_ref[...])
        c = _scan_rows(sre_ref, sim_ref, (are_ref[...], aim_ref[...]), (carry[0:1, :], carry[1:2, :]), False)
        carry[0:1, :] = c[0]
        carry[1:2, :] = c[1]
        (yg,) = _s5_out_fn(sre_ref[...], sim_ref[...], uv, cre_ref[...], cim_ref[...], d_ref[...])
        yg_ref[...] = yg.astype(yg_ref.dtype)

    u_blk = _spec((tt, SSM_BLK_IN), lambda l, t: (t, l))
    s_blk = _spec((tt, SSM_BLK_ST), lambda l, t: (t, l))
    blk3 = lambda arr: _spec((1,) + arr.shape[1:], lambda l, t: (l, 0, 0))
    par = lambda width: _spec((1, width), lambda l, t: (0, l))
    return pl.pallas_call(
        body, name="s5_fwd", grid=(N_SSM_BLK, n_tok // tt),
        in_specs=[u_blk, blk3(b_re), blk3(b_im), par(SSM_BLK_ST), par(SSM_BLK_ST), blk3(c_re), blk3(c_im), par(SSM_BLK_IN)],
        out_specs=[s_blk, s_blk, u_blk],
        out_shape=[jax.ShapeDtypeStruct((n_tok, SSM_LANES), F32)] * 2 + [jax.ShapeDtypeStruct((n_tok, D_MODEL), BF16)],
        scratch_shapes=[pltpu.VMEM((2, SSM_BLK_ST), F32)],
        compiler_params=_cparams(2),
    )(u, b_re, b_im, a_re, a_im, c_re, c_im, d_row)


def _s5_bwd(dyg, u, s_re, s_im, dh_res, b_re, b_im, a_re, a_im, c_re, c_im, d_row):
    n_tok = u.shape[0]
    tt = min(S5_TIME_TILE, n_tok)
    n_t = n_tok // tt

    def body(dyg_ref, u_ref, sre_ref, sim_ref, hre_ref, him_ref, res_ref, bre_ref, bim_ref, are_ref, aim_ref,
             cre_ref, cim_ref, d_ref, du_ref, dbre_ref, dbim_ref, dcre_ref, dcim_ref, dd_ref, dare_ref, daim_ref,
             carry, gre, gim):
        i = pl.program_id(1)
        sums = (dbre_ref, dbim_ref, dcre_ref, dcim_ref, dd_ref, dare_ref, daim_ref)

        @pl.when(i == 0)
        def _():
            carry[...] = jnp.zeros(carry.shape, F32)
            for r in sums:
                r[...] = jnp.zeros(r.shape, F32)

        uv, sre, sim = u_ref[...], sre_ref[...], sim_ref[...]
        _, out_vjp = jax.vjp(_s5_out_fn, sre, sim, uv, cre_ref[...], cim_ref[...], d_ref[...])
        gre[...], gim[...], du_out, dcre, dcim, dd = out_vjp((dyg_ref[...],))
        c = _scan_rows(gre, gim, (are_ref[...], -aim_ref[...]), (carry[0:1, :], carry[1:2, :]), True)
        carry[0:1, :] = c[0]
        carry[1:2, :] = c[1]
        g_re, g_im = gre[...], gim[...]
        at_start = i == n_t - 1
        p_re = _shift_down(sre, jnp.where(at_start, 0.0, hre_ref[7:8, :]))
        p_im = _shift_down(sim, jnp.where(at_start, 0.0, him_ref[7:8, :]))
        _, in_vjp = jax.vjp(_s5_in_fn, uv, bre_ref[...], bim_ref[...])
        du_in, dbre, dbim = in_vjp((g_re, g_im))
        du_ref[...] = du_out + du_in + res_ref[...]
        dare = jnp.sum(g_re * p_re + g_im * p_im, axis=0, keepdims=True)
        daim = jnp.sum(g_im * p_re - g_re * p_im, axis=0, keepdims=True)
        for r, val in zip(sums, (dbre, dbim, dcre, dcim, dd, dare, daim)):
            r[...] += val

    u_blk = _spec((tt, SSM_BLK_IN), lambda l, t: (n_t - 1 - t, l))
    s_blk = _spec((tt, SSM_BLK_ST), lambda l, t: (n_t - 1 - t, l))
    halo = _spec((8, SSM_BLK_ST), lambda l, t: (jnp.maximum((n_t - 1 - t) * (tt // 8) - 1, 0), l))
    blk3 = lambda arr: _spec((1,) + arr.shape[1:], lambda l, t: (l, 0, 0))
    par = lambda width: _spec((1, width), lambda l, t: (0, l))
    params = [blk3(b_re), blk3(b_im), par(SSM_BLK_ST), par(SSM_BLK_ST), blk3(c_re), blk3(c_im), par(SSM_BLK_IN)]
    shape = lambda arr: jax.ShapeDtypeStruct(arr.shape, F32)
    return pl.pallas_call(
        body, name="s5_bwd", grid=(N_SSM_BLK, n_t),
        in_specs=[u_blk, u_blk, s_blk, s_blk, halo, halo, u_blk] + params,
        out_specs=[u_blk, blk3(b_re), blk3(b_im), blk3(c_re), blk3(c_im), par(SSM_BLK_IN), par(SSM_BLK_ST), par(SSM_BLK_ST)],
        out_shape=[shape(u), shape(b_re), shape(b_im), shape(c_re), shape(c_im), shape(d_row), shape(a_re), shape(a_im)],
        scratch_shapes=[pltpu.VMEM((2, SSM_BLK_ST), F32), pltpu.VMEM((tt, SSM_BLK_ST), F32), pltpu.VMEM((tt, SSM_BLK_ST), F32)],
        compiler_params=_cparams(2),
    )(dyg, u, s_re, s_im, s_re, s_im, dh_res, b_re, b_im, a_re, a_im, c_re, c_im, d_row)


def _loss_call(h, target, tm):
    n_tok, d = h.shape

    def body(h_ref, t_ref, acc_ref, dh_ref):
        @pl.when(pl.program_id(0) == 0)
        def _():
            acc_ref[...] = jnp.zeros(acc_ref.shape, F32)

        e = h_ref[...] - t_ref[...]
        dh_ref[...] = e * (1.0 / d)
        acc_ref[...] += jnp.sum(jnp.sum(e * e, axis=1, keepdims=True), axis=0, keepdims=True)

    tok = _spec((tm, d), lambda i: (i, 0))
    return pl.pallas_call(
        body, name="loss", grid=(n_tok // tm,),
        in_specs=[tok, tok], out_specs=[_spec((8, 128), lambda i: (0, 0)), tok],
        out_shape=[jax.ShapeDtypeStruct((8, 128), F32), jax.ShapeDtypeStruct(h.shape, F32)],
        compiler_params=_cparams(1),
    )(h, target)


def _block_diag(t):
    nb, ng, rows, cols = t.shape
    eye = jnp.eye(ng, dtype=t.dtype)
    return jnp.einsum('bgrc,gh->bgrhc', t, eye).reshape(nb, ng * rows, ng * cols)


def _block_diag_t(t, rows, cols):
    nb, ng = t.shape[0], t.shape[1] // rows
    t5 = t.reshape(nb, ng, rows, ng, cols)
    return jnp.einsum('bgrhc,gh->bgrc', t5, jnp.eye(ng, dtype=t.dtype))


def _mlp_fwd(layer, h, w1g, w2g):
    pre = _matmul(f"mlp{layer}_up", h, w1g, 'nn', b_cols=(layer,))
    return pre, _matmul(f"mlp{layer}_down", pre, w2g, 'nn', lhs_fn=_relu2, b_rows=(layer,))


def _mlp_bwd(layer, h, pre, dy, dh_res, w1g, w2g):
    dpre = _matmul(f"mlp{layer}_down_dx", dy, w2g, 'nt', epi=lambda acc, p: acc * (2.0 * jnp.maximum(p, 0.0)),
                   extra=pre, b_rows=(layer,))
    dw2 = _matmul(f"mlp{layer}_down_dw", pre, dy, 'tn', lhs_fn=_relu2)
    dw1 = _matmul(f"mlp{layer}_up_dw", h, dpre, 'tn', out_cols=True)
    dh = _matmul(f"mlp{layer}_up_dx", dpre, w1g, 'nt', epi=lambda acc, e: acc + e, extra=dh_res, b_cols=(layer,))
    return dh, dw1, dw2


def _local_step(x, target, w, wg, late_weights, exchange):
    n_tok = x.shape[0]
    tm = min(TOKEN_TILE, n_tok)
    n_tile, n_chunk = n_tok // tm, n_tok // CHUNK
    row = lambda v: v.reshape(1, -1)
    big = {n: wg[n].reshape(D_MODEL, D_MODEL) for n in ('rw_wr', 'rw_wk', 'rw_wv', 'rw_wo')}
    ln_g, ln_b = w['ln_g'], w['ln_b']
    grads = {}
    to_slots = lambda t: t.reshape(N_DEV, t.shape[0] // N_DEV, t.shape[1])

    halo_spec = (8, D_MODEL), lambda i: (jnp.maximum(i * (tm // 8) - 1, 0), 0)
    mix_p = [_tile(x, tm), (x,) + halo_spec, _full(w['rw_mu'][0]), _full(row(w['rw_w0'])), _full(w['rw_w1'][0]),
             _full(w['rw_w2'][0]), _full(row(w['rw_a0'])), _full(w['rw_a1'][0]), _full(w['rw_a2'][0]),
             _full(w['rw_g1'][0]), _full(w['rw_g2'][0])]
    tok_out = lambda dt: ((n_tok, D_MODEL), dt, (tm, D_MODEL), lambda i: (i, 0))
    xr, xk, xv, lw, a, g = _fwd_call("rwkv_mix", _mix_fn, (n_tile,), mix_p,
                                     [tok_out(BF16), tok_out(BF16), tok_out(BF16), tok_out(F32), tok_out(F32), tok_out(F32)])
    r = _matmul("rwkv_r", xr, big['rw_wr'], 'nn')
    k = _matmul("rwkv_k", xk, big['rw_wk'], 'nn')
    v = _matmul("rwkv_v", xv, big['rw_wv'], 'nn')

    slab_w = CHUNK_PAIRS * PAIR
    slab = lambda arr: (arr, (CHUNK, slab_w), lambda j, c: (c, j))
    pslab = lambda arr: (arr, (1, slab_w), lambda j, c: (0, j))
    chunk_p = [slab(r), slab(lw), slab(k), slab(v), slab(a), pslab(row(w['rw_k_k'])), pslab(row(w['rw_k_a'])),
               pslab(w['rw_r_k'].reshape(1, -1))]
    st_shape = (n_chunk, N_PAIR, PAIR, PAIR)
    st_out = (st_shape, F32, (1, CHUNK_PAIRS, PAIR, PAIR), lambda j, c: (c, j, 0, 0))
    sl_out = ((n_tok, D_MODEL), F32, (CHUNK, slab_w), lambda j, c: (c, j))
    chunk_grid = (N_PAIR // CHUNK_PAIRS, n_chunk)
    m_all, n_all, q, oi, bonus = _fwd_call("rwkv_chunk", _chunk_fn, chunk_grid, chunk_p,
                                           [st_out, st_out, sl_out, sl_out, sl_out])
    o, zs = _rwkv_state_fwd(m_all, n_all, q, oi)

    gate_p = [_tile(o, tm), _tile(bonus, tm), _tile(g, tm), _full(row(w['rw_lnx_g'])), _full(row(w['rw_lnx_b']))]
    (og,) = _fwd_call("rwkv_gate", _gate_fn, (n_tile,), gate_p, [tok_out(BF16)])
    y0 = _matmul("rwkv_o", og, big['rw_wo'], 'nn')
    ln0_p = [_tile(x, tm), _tile(y0, tm), _full(ln_g[0:1]), _full(ln_b[0:1])]
    (h1,) = _fwd_call("ln0", _res_ln_fn, (n_tile,), ln0_p, [tok_out(F32)])
    late = late_weights(h1)
    w1g, w2g, glu_g = late['mlp_w1'], late['mlp_w2'], late['s5_w_glu']
    pre0, y1 = _mlp_fwd(0, h1, w1g, w2g)
    ln1_p = [_tile(h1, tm), _tile(y1, tm), _full(ln_g[1:2]), _full(ln_b[1:2])]
    (h2,) = _fwd_call("ln1", _res_ln_fn, (n_tile,), ln1_p, [tok_out(F32)])

    gp = (SSM_GROUPS, SSM_STATE)
    par_p = [_full(w['s5_a_re'][0]), _full(w['s5_a_im'][0]), _full(w['s5_log_dt'].reshape(SSM_GROUPS, 1))]
    s5_par_fn = lambda ar, ai, ld: _s5_param_fn(ar, ai, ld, None, None)
    gp_out = (gp, F32, gp, lambda i: (0, 0))
    abar_re, abar_im, coef_re, coef_im = _fwd_call("s5_param", s5_par_fn, (1,), par_p, [gp_out] * 4)
    b_flat = [w[n][0].reshape(SSM_LANES, SSM_GROUP) for n in ('s5_b_re', 's5_b_im')]
    bbar_p = [_full(coef_re.reshape(SSM_LANES, 1)), _full(coef_im.reshape(SSM_LANES, 1)), _full(b_flat[0]), _full(b_flat[1])]
    bb_out = ((SSM_LANES, SSM_GROUP), F32, (SSM_LANES, SSM_GROUP), lambda i: (0, 0))
    bbar_re, bbar_im = _fwd_call("s5_bbar", _s5_bbar_fn, (1,), bbar_p, [bb_out] * 2)
    to_in = lambda t: _block_diag(t.reshape(N_SSM_BLK, 8, SSM_STATE, SSM_GROUP).transpose(0, 1, 3, 2))
    to_out = lambda t: _block_diag(t.reshape(N_SSM_BLK, 8, SSM_GROUP, SSM_STATE).transpose(0, 1, 3, 2))
    bblk_re, bblk_im = to_in(bbar_re), to_in(bbar_im)
    cblk_re, cblk_im = to_out(w['s5_c_re'][0]), to_out(w['s5_c_im'][0])
    a_row_re, a_row_im = abar_re.reshape(1, SSM_LANES), abar_im.reshape(1, SSM_LANES)

    d_row = row(w['s5_d'])
    s5_params = (bblk_re, bblk_im, a_row_re, a_row_im, cblk_re, cblk_im, d_row)
    s_re, s_im, yg = _s5_fwd(h2, *s5_params)
    z = _matmul("s5_glu", yg, glu_g, 'nn', b_cols=(0,))
    ln2_p = [_tile(h2, tm), _tile(z, tm), _full(ln_g[2:3]), _full(ln_b[2:3])]
    (h3,) = _fwd_call("ln2", _glu_ln_fn, (n_tile,), ln2_p, [tok_out(F32)])
    pre1, y3 = _mlp_fwd(1, h3, w1g, w2g)
    ln3_p = [_tile(h3, tm), _tile(y3, tm), _full(ln_g[3:4]), _full(ln_b[3:4])]
    (h4,) = _fwd_call("ln3", _res_ln_fn, (n_tile,), ln3_p, [tok_out(F32)])

    loss_acc, dh4 = _loss_call(h4, target, tm)

    ln_w = ['tile', 'tile', 'acc', 'acc']
    dh3_res, dy3, dg3, db3 = _bwd_call("ln3_bwd", _res_ln_fn, (n_tile,), ln3_p, [_tile(dh4, tm)], ln_w)
    dh3, dw1_1, dw2_1 = _mlp_bwd(1, h3, pre1, dy3, dh3_res, w1g, w2g)
    zero = exchange([('mlp_w1', 1, dw1_1), ('mlp_w2', 1, to_slots(dw2_1))])
    ln2_p[2] = _full(ln_g[2:3] + zero)
    dh2_res, dz, dg2, db2 = _bwd_call("ln2_bwd", _glu_ln_fn, (n_tile,), ln2_p, [_tile(dh3, tm)], ln_w)
    dyg = _matmul("s5_glu_dx", dz, glu_g, 'nt', b_cols=(0,))
    dw_glu = _matmul("s5_glu_dw", yg, dz, 'tn', out_cols=True)
    dh2, dbb_re, dbb_im, dcb_re, dcb_im, dd, dabar_re, dabar_im = _s5_bwd(dyg, h2, s_re, s_im, dh2_res, *s5_params)
    from_in = lambda t: _block_diag_t(t, SSM_GROUP, SSM_STATE).transpose(0, 1, 3, 2).reshape(SSM_LANES, SSM_GROUP)
    from_out = lambda t: _block_diag_t(t, SSM_STATE, SSM_GROUP).transpose(0, 1, 3, 2).reshape(1, SSM_GROUPS, SSM_GROUP, SSM_STATE)
    grads['s5_c_re'], grads['s5_c_im'] = from_out(dcb_re), from_out(dcb_im)
    grads['s5_d'] = dd
    dcoef_re, dcoef_im, db_re, db_im = _bwd_call(
        "s5_bbar_bwd", _s5_bbar_fn, (1,), bbar_p, [_full(from_in(dbb_re)), _full(from_in(dbb_im))], ['acc'] * 4)
    grads['s5_b_re'] = db_re.reshape(w['s5_b_re'].shape)
    grads['s5_b_im'] = db_im.reshape(w['s5_b_im'].shape)
    par_ct = [_full(dabar_re.reshape(gp)), _full(dabar_im.reshape(gp)), _full(dcoef_re.reshape(gp)), _full(dcoef_im.reshape(gp))]
    da_re, da_im, dlog_dt = _bwd_call("s5_param_bwd", s5_par_fn, (1,), par_p, par_ct, ['acc'] * 3)
    grads['s5_a_re'], grads['s5_a_im'], grads['s5_log_dt'] = da_re[None], da_im[None], dlog_dt.reshape(1, SSM_GROUPS)

    dh1_res, dy1, dg1, db1 = _bwd_call("ln1_bwd", _res_ln_fn, (n_tile,), ln1_p, [_tile(dh2, tm)], ln_w)
    dh1, dw1_0, dw2_0 = _mlp_bwd(0, h1, pre0, dy1, dh1_res, w1g, w2g)
    dx_res, dy0, dg0, db0 = _bwd_call("ln0_bwd", _res_ln_fn, (n_tile,), ln0_p, [_tile(dh1, tm)], ln_w)
    grads['ln_g'] = jnp.concatenate([dg0, dg1, dg2, dg3])
    grads['ln_b'] = jnp.concatenate([db0, db1, db2, db3])
    dog = _matmul("rwkv_o_dx", dy0, big['rw_wo'], 'nt')
    dw_o = to_slots(_matmul("rwkv_o_dw", og, dy0, 'tn'))
    zero = exchange([('s5_w_glu', 0, dw_glu), ('mlp_w1', 0, dw1_0), ('mlp_w2', 0, to_slots(dw2_0)), ('rw_wo', 0, dw_o)])
    gate_p[3] = _full(row(w['rw_lnx_g']) + zero)
    d_o, dbonus, dgate, grads['rw_lnx_g'], grads['rw_lnx_b'] = _bwd_call(
        "rwkv_gate_bwd", _gate_fn, (n_tile,), gate_p, [_tile(dog, tm)], ['tile', 'tile', 'tile', 'acc', 'acc'])
    dq, dm_all, dn_all = _rwkv_state_bwd(m_all, q, zs, d_o)
    st_ct = lambda arr: (arr, (1, CHUNK_PAIRS, PAIR, PAIR), lambda j, c: (c, j, 0, 0))
    dr, dlw, dk, dv, da, grads['rw_k_k'], grads['rw_k_a'], dr_k = _bwd_call(
        "rwkv_chunk_bwd", _chunk_fn, chunk_grid, chunk_p,
        [st_ct(dm_all), st_ct(dn_all), slab(dq), slab(d_o), slab(dbonus)],
        ['tile_bf16', 'tile', 'tile_bf16', 'tile_bf16', 'tile'] + ['acc'] * 3)
    grads['rw_r_k'] = dr_k.reshape(w['rw_r_k'].shape)
    dw_r = to_slots(_matmul("rwkv_r_dw", xr, dr, 'tn'))
    dw_k = to_slots(_matmul("rwkv_k_dw", xk, dk, 'tn'))
    dw_v = to_slots(_matmul("rwkv_v_dw", xv, dv, 'tn'))
    zero = exchange([('rw_wr', 0, dw_r), ('rw_wk', 0, dw_k), ('rw_wv', 0, dw_v)])
    mix_p[3] = _full(row(w['rw_w0']) + zero)
    dxr = _matmul("rwkv_r_dx", dr, big['rw_wr'], 'nt')
    dxk = _matmul("rwkv_k_dx", dk, big['rw_wk'], 'nt')
    dxv = _matmul("rwkv_v_dx", dv, big['rw_wv'], 'nt')
    mix_ct = [_tile(t, tm) for t in (dxr, dxk, dxv, dlw, da, dgate)]
    halo_grad = ('tile', ((n_tile * 8, D_MODEL), (8, D_MODEL), lambda i: (i, 0)))
    res = _bwd_call("rwkv_mix_bwd", _mix_fn, (n_tile,), mix_p, mix_ct, ['tile', halo_grad] + ['acc'] * 9,
                    addends={0: _tile(dx_res, tm)})
    corr = res[1].reshape(n_tile, 8, D_MODEL)[1:, 7:8, :]
    corr = jnp.pad(corr, ((0, 1), (tm - 1, 0), (0, 0)))
    dx = (res[0].reshape(n_tile, tm, D_MODEL) + corr).reshape(n_tok, D_MODEL)
    (grads['rw_mu'], grads['rw_w0'], grads['rw_w1'], grads['rw_w2'], grads['rw_a0'], grads['rw_a1'], grads['rw_a2'],
     grads['rw_g1'], grads['rw_g2']) = [t[None] if t.shape[0] != 1 else t for t in res[2:]]
    return loss_acc[0, 0], dx, grads


PACK_ROWS = 128


def _pack(arrs):
    flat = jnp.concatenate([a.reshape(-1).astype(F32) for a in arrs])
    per = PACK_ROWS * PACK_W
    total = -(-flat.shape[0] // per) * per
    return jnp.pad(flat, (0, total - flat.shape[0])).reshape(-1, PACK_W)


def _unpack(packed, shapes, lead=()):
    flat = packed.reshape(lead + (-1,))
    out, off = [], 0
    for s in shapes:
        n = math.prod(s)
        out.append(flat[..., off:off + n].reshape(lead + tuple(s)))
        off += n
    return out


def _mesh_pos():
    return lax.axis_index("x"), lax.axis_index("y"), lax.axis_index("c")


def _flip(v, f):
    return 1 - v if f else v


def _hbm_call(name, body, arrays, out_shapes, n_sems):
    n_t = len(arrays)
    hbm = pl.BlockSpec(memory_space=pl.ANY)
    return pl.pallas_call(
        body, name=name, out_shape=out_shapes, in_specs=[hbm] * n_t, out_specs=[hbm] * n_t,
        scratch_shapes=[pltpu.SemaphoreType.DMA((n_t, n_sems)), pltpu.SemaphoreType.DMA((n_t, n_sems)),
                        pltpu.SemaphoreType.DMA((n_t,))],
    )(*arrays)


def _all_gather(name, locals_):
    n_t = len(locals_)

    def body(*refs):
        x_refs, out_refs = refs[:n_t], refs[n_t:2 * n_t]
        send_sems, recv_sems, local_sems = refs[2 * n_t:]
        x, y, c = _mesh_pos()
        me, sibling = (x, y, c), (x, y, 1 - c)
        chips = [(1 - x, y), (x, 1 - y), (1 - x, 1 - y)]

        def copy(t, k, block, to, own=False):
            slot = out_refs[t].at[4 * block[0] + 2 * block[1] + block[2]]
            return pltpu.make_async_remote_copy(
                src_ref=x_refs[t] if own else slot, dst_ref=slot,
                send_sem=send_sems.at[t, k], recv_sem=recv_sems.at[t, k],
                device_id=to, device_id_type=pl.DeviceIdType.MESH)

        mine = [pltpu.make_async_copy(x_refs[t], out_refs[t].at[4 * x + 2 * y + c], local_sems.at[t]) for t in range(n_t)]
        for cp in mine:
            cp.start()
        sent = []
        for t in range(n_t):
            sent.append(copy(t, 0, me, sibling, own=True))
            sent += [copy(t, 1 + j, me, (*chip, c), own=True) for j, chip in enumerate(chips)]
        for cp in sent:
            cp.start()
        for j, chip in enumerate(chips):
            for t in range(n_t):
                copy(t, 1 + j, (*chip, c), me).wait_recv()
                passed = copy(t, 4 + j, (*chip, c), sibling)
                passed.start()
                sent.append(passed)
        for t in range(n_t):
            copy(t, 0, sibling, me).wait_recv()
            for j, chip in enumerate(chips):
                copy(t, 4 + j, (*chip, 1 - c), me).wait_recv()
        for cp in sent:
            cp.wait_send()
        for cp in mine:
            cp.wait()

    outs = [jax.ShapeDtypeStruct((N_DEV,) + a.shape, a.dtype) for a in locals_]
    return _hbm_call(name, body, locals_, outs, 7)


def _all_to_all(name, blocks):
    n_t = len(blocks)

    def body(*refs):
        x_refs, out_refs = refs[:n_t], refs[n_t:2 * n_t]
        send_sems, recv_sems, local_sems = refs[2 * n_t:]
        x, y, c = _mesh_pos()
        my_slot = 4 * x + 2 * y + c
        mine = [pltpu.make_async_copy(x_refs[t].at[my_slot], out_refs[t].at[my_slot], local_sems.at[t]) for t in range(n_t)]
        for cp in mine:
            cp.start()
        copies = []
        for k in range(1, N_DEV):
            peer = (_flip(x, k & 4), _flip(y, k & 2), _flip(c, k & 1))
            peer_slot = 4 * peer[0] + 2 * peer[1] + peer[2]
            for t in range(n_t):
                sems = dict(send_sem=send_sems.at[t, k - 1], recv_sem=recv_sems.at[t, k - 1],
                            device_id=peer, device_id_type=pl.DeviceIdType.MESH)
                cp = pltpu.make_async_remote_copy(src_ref=x_refs[t].at[peer_slot], dst_ref=out_refs[t].at[my_slot], **sems)
                cp.start()
                landed = pltpu.make_async_remote_copy(src_ref=x_refs[t].at[my_slot], dst_ref=out_refs[t].at[peer_slot], **sems)
                copies.append((cp, landed))
        for sent, landed in copies:
            landed.wait_recv()
            sent.wait_send()
        for cp in mine:
            cp.wait()

    outs = [jax.ShapeDtypeStruct(a.shape, a.dtype) for a in blocks]
    return _hbm_call(name, body, blocks, outs, 7)


_HBM = pl.BlockSpec(memory_space=pltpu.HBM)
_SEM = pl.BlockSpec(memory_space=pltpu.SEMAPHORE)
_EFFECT = pltpu.SideEffectType.DATAFLOW_SIDE_EFFECTING


def _peer_copies(x_refs, land_refs, send_sems, recv_sems, gather, mine):
    x, y, c = _mesh_pos()
    my_slot = 4 * x + 2 * y + c
    copies = []
    for k in range(1, N_DEV):
        peer = (_flip(x, k & 4), _flip(y, k & 2), _flip(c, k & 1))
        peer_slot = 4 * peer[0] + 2 * peer[1] + peer[2]
        for t, (x_ref, land) in enumerate(zip(x_refs, land_refs)):
            sem = t * (N_DEV - 1) + k - 1
            copies.append(pltpu.make_async_remote_copy(
                src_ref=x_ref if gather else x_ref.at[peer_slot], dst_ref=land.at[my_slot if mine else peer_slot],
                send_sem=send_sems.at[sem], recv_sem=recv_sems.at[sem], device_id=peer, device_id_type=pl.DeviceIdType.MESH))
    return copies


def _push_start(name, arrays, gather):
    n_t = len(arrays)
    lands = [lax.empty((N_DEV,) + a.shape if gather else a.shape, a.dtype) for a in arrays]

    def body(*refs):
        x_refs, land_refs = refs[:n_t], refs[n_t:2 * n_t]
        send_sems, recv_sems = refs[2 * n_t], refs[2 * n_t + 1]
        for sent in _peer_copies(x_refs, land_refs, send_sems, recv_sems, gather, True):
            sent.start()
        refs[-1][...] = jnp.zeros((8, 128), F32)

    held = list(arrays) + lands
    sems = pltpu.SemaphoreType.DMA((n_t * (N_DEV - 1),))
    out = pl.pallas_call(
        body, name=name,
        out_shape=(sems, sems, *[pltpu.HBM(a.shape, a.dtype) for a in held], jax.ShapeDtypeStruct((8, 128), F32)),
        in_specs=[_HBM] * (2 * n_t),
        out_specs=(_SEM, _SEM, *[_HBM] * (2 * n_t), pl.BlockSpec(memory_space=pltpu.VMEM)),
        input_output_aliases={i: 2 + i for i in range(2 * n_t)},
        compiler_params=pltpu.CompilerParams(has_side_effects=_EFFECT),
    )(*[pltpu.with_memory_space_constraint(a, pltpu.HBM) for a in held])
    return dict(send=out[0], recv=out[1], held=out[2:2 + 2 * n_t], zero=out[-1][0, 0], gather=gather)


def _push_wait(name, started, after):
    held, gather = started['held'], started['gather']
    n_t = len(held) // 2

    def body(*refs):
        x_refs, land_refs = refs[:n_t], refs[n_t:2 * n_t]
        send_sems, recv_sems = refs[2 * n_t], refs[2 * n_t + 1]
        for sent in _peer_copies(x_refs, land_refs, send_sems, recv_sems, gather, True):
            sent.wait_send()
        for landed in _peer_copies(x_refs, land_refs, send_sems, recv_sems, gather, False):
            landed.wait_recv()

    out = pl.pallas_call(
        body, name=name,
        out_shape=[pltpu.HBM(a.shape, a.dtype) for a in held],
        in_specs=[_HBM] * (2 * n_t) + [_SEM, _SEM, pl.BlockSpec(memory_space=pl.ANY)],
        out_specs=[_HBM] * (2 * n_t),
        input_output_aliases={i: i for i in range(2 * n_t)},
        compiler_params=pltpu.CompilerParams(has_side_effects=_EFFECT),
    )(*held, started['send'], started['recv'], after)
    return out[:n_t], out[n_t:]


def _adam_call(name, slots, w, m, v, own=None):
    rows, width = w.shape
    tile = min(rows, (PACK_ROWS * PACK_W) // width)
    c1 = 1.0 / (1.0 - ADAM_B1 ** ADAM_STEP)
    c2 = 1.0 / (1.0 - ADAM_B2 ** ADAM_STEP)

    def body(*refs):
        if own is None:
            s_ref, w_ref, m_ref, v_ref, g_ref, d_ref, nm_ref, nv_ref = refs
            part = lambda i: s_ref[i]
        else:
            s_ref, own_ref, w_ref, m_ref, v_ref, g_ref, d_ref, nm_ref, nv_ref = refs
            x, y, c = _mesh_pos()
            part = lambda i: jnp.where(4 * x + 2 * y + c == i, own_ref[...], s_ref[i])
        g = part(0)
        for i in range(1, N_DEV):
            g = g + part(i)
        m_new = ADAM_B1 * m_ref[...] + (1.0 - ADAM_B1) * g
        v_new = ADAM_B2 * v_ref[...] + (1.0 - ADAM_B2) * (g * g)
        g_ref[...] = g
        nm_ref[...] = m_new
        nv_ref[...] = v_new
        d_ref[...] = -ADAM_LR * ((m_new * c1) / (jnp.sqrt(v_new * c2) + ADAM_EPS) + ADAM_WD * w_ref[...])

    blk = _spec((tile, width), lambda i: (i, 0))
    ops = [slots] + ([] if own is None else [own]) + [w, m, v]
    return pl.pallas_call(
        body, name=name, grid=(rows // tile,),
        in_specs=[_spec((N_DEV, tile, width), lambda i: (0, i, 0))] + [blk] * (len(ops) - 1),
        out_specs=[blk] * 4, out_shape=[jax.ShapeDtypeStruct(w.shape, F32)] * 4,
        compiler_params=_cparams(1),
    )(*ops)


def kernel(x, ln_g, ln_b, rw_mu, rw_w0, rw_w1, rw_w2, rw_a0, rw_a1, rw_a2, rw_g1, rw_g2, rw_k_k, rw_k_a, rw_r_k, rw_wr, rw_wk, rw_wv, rw_wo, rw_lnx_g, rw_lnx_b, s5_a_re, s5_a_im, s5_log_dt, s5_b_re, s5_b_im, s5_c_re, s5_c_im, s5_d, s5_w_glu, mlp_w1, mlp_w2, loss_target, m_ln_g, m_ln_b, m_rw_mu, m_rw_w0, m_rw_w1, m_rw_w2, m_rw_a0, m_rw_a1, m_rw_a2, m_rw_g1, m_rw_g2, m_rw_k_k, m_rw_k_a, m_rw_r_k, m_rw_wr, m_rw_wk, m_rw_wv, m_rw_wo, m_rw_lnx_g, m_rw_lnx_b, m_s5_a_re, m_s5_a_im, m_s5_log_dt, m_s5_b_re, m_s5_b_im, m_s5_c_re, m_s5_c_im, m_s5_d, m_s5_w_glu, m_mlp_w1, m_mlp_w2, v_ln_g, v_ln_b, v_rw_mu, v_rw_w0, v_rw_w1, v_rw_w2, v_rw_a0, v_rw_a1, v_rw_a2, v_rw_g1, v_rw_g2, v_rw_k_k, v_rw_k_a, v_rw_r_k, v_rw_wr, v_rw_wk, v_rw_wv, v_rw_wo, v_rw_lnx_g, v_rw_lnx_b, v_s5_a_re, v_s5_a_im, v_s5_log_dt, v_s5_b_re, v_s5_b_im, v_s5_c_re, v_s5_c_im, v_s5_d, v_s5_w_glu, v_mlp_w1, v_mlp_w2):
    given = dict(locals())
    local_w = {n: given[n] for n in WEIGHTS}
    local_m = {n: given["m_" + n] for n in WEIGHTS}
    local_v = {n: given["v_" + n] for n in WEIGHTS}
    small = [n for n in WEIGHTS if n in SHARDED and n not in BIG]
    small_shapes = [local_w[n].shape for n in small]
    rep_shapes = [local_w[n].shape for n in REPLICATED]

    my_slot = 4 * lax.axis_index("x") + 2 * lax.axis_index("y") + lax.axis_index("c")
    as_bf16 = {n: local_w[n].astype(BF16) for n in BIG}

    late_started = _push_start("weights_late_start", [as_bf16[n] for n in LATE], gather=True)
    early = [n for n in BIG if n not in LATE]
    gathered = _all_gather("weights_all_gather", [as_bf16[n] for n in early] + [_pack([local_w[n] for n in small])])
    early_gathered = dict(zip(early, gathered))
    full = dict(local_w)
    for n, blk in zip(small, _unpack(gathered[-1], small_shapes, lead=(N_DEV,))):
        full[n] = jnp.concatenate([blk[i] for i in range(N_DEV)], axis=SHARDED[n])
    full['rw_w0'] = full['rw_w0'] + late_started['zero']

    def late_weights(after):
        _, lands = _push_wait("weights_late_wait", late_started, after)
        own = lambda n: as_bf16[n][None]
        return {n: lax.dynamic_update_slice(land, own(n), (my_slot,) + (0,) * as_bf16[n].ndim) for n, land in zip(LATE, lands)}

    exchanges = []

    def exchange(entries):
        started = _push_start(f"grads_start_{len(exchanges)}", [t for _, _, t in entries], gather=False)
        exchanges.append((entries, started))
        return started['zero']

    loss_sq, dx, grads = _local_step(x[0], loss_target[0], full, early_gathered, late_weights, exchange)
    loss = (0.5 / D_MODEL) * lax.psum(loss_sq, MESH_AXES)

    by_name = {}
    for e, (entries, started) in enumerate(exchanges):
        sources, lands = _push_wait(f"grads_wait_{e}", started, dx)
        for (n, i, _), src, slots in zip(entries, sources, lands):
            part = lambda d: d[n].reshape((-1,) + slots.shape[1:])[i]
            own = lax.dynamic_index_in_dim(src, my_slot, 0, keepdims=False)
            out = _adam_call(f"adamw_{n}_{i}", slots, part(local_w), part(local_m), part(local_v), own=own)
            by_name.setdefault(n, {})[i] = out
    for n in BIG:
        by_name[n] = [by_name[n][i] for i in sorted(by_name[n])]
    parts = [jnp.stack(jnp.split(grads[n], N_DEV, axis=SHARDED[n])) for n in small]
    small_send = jnp.stack([_pack([p[i] for p in parts]) for i in range(N_DEV)])
    recv = _all_to_all("grads_all_to_all", [small_send])
    for n in BIG:
        by_name[n] = [jnp.stack(ts).reshape(local_w[n].shape) for ts in zip(*by_name[n])]
    small_out = _adam_call("adamw_small", recv[-1], _pack([local_w[n] for n in small]),
                           _pack([local_m[n] for n in small]), _pack([local_v[n] for n in small]))
    (rep_all,) = _all_gather("grads_all_gather", [_pack([grads[n] for n in REPLICATED])])
    rep_out = _adam_call("adamw_replicated", rep_all, _pack([local_w[n] for n in REPLICATED]),
                         _pack([local_m[n] for n in REPLICATED]), _pack([local_v[n] for n in REPLICATED]))
    for k, (small_packed, rep_packed) in enumerate(zip(small_out, rep_out)):
        for n, t in zip(small, _unpack(small_packed, small_shapes)):
            by_name.setdefault(n, [None] * 4)[k] = t
        for n, t in zip(REPLICATED, _unpack(rep_packed, rep_shapes)):
            by_name.setdefault(n, [None] * 4)[k] = t
    results = [by_name[n][k] for k in range(4) for n in WEIGHTS]
    return (loss, dx[None], *results)
```

```python
import functools
import math

import jax
import jax.numpy as jnp
from jax import lax
from jax.experimental import pallas as pl
from jax.experimental.pallas import tpu as pltpu

F32 = jnp.float32
BF16 = jnp.bfloat16
HI = lax.Precision.HIGHEST

D_MODEL = 1024
HEAD = 64
PAIR = 2 * HEAD
N_PAIR = D_MODEL // PAIR
CHUNK = 64
GN_EPS = 64e-5
LN_EPS = 1e-5
SSM_GROUP = 16
SSM_STATE = 64
SSM_GROUPS = D_MODEL // SSM_GROUP
SSM_LANES = SSM_GROUPS * SSM_STATE
SSM_BLK_IN = 128
SSM_BLK_ST = 512
N_SSM_BLK = D_MODEL // SSM_BLK_IN
DEPTH = 2
DN_ALPHA = (2.0 * DEPTH) ** 0.25
ADAM_LR, ADAM_B1, ADAM_B2, ADAM_EPS, ADAM_WD, ADAM_STEP = 0.001, 0.9, 0.999, 1e-08, 0.01, 10
N_DEV = 8
MESH_AXES = ("x", "y", "c")
VMEM_LIMIT = 56 * 1024 * 1024
TOKEN_TILE = 256
PACK_W = 1024

SHARDED = {
    "rw_mu": 2, "rw_w1": 1, "rw_w2": 2, "rw_a1": 1, "rw_a2": 2, "rw_g1": 1, "rw_g2": 2,
    "rw_wr": 1, "rw_wk": 1, "rw_wv": 1, "rw_wo": 1, "s5_d": 1, "s5_w_glu": 2, "mlp_w1": 2, "mlp_w2": 1,
}
WEIGHTS = ['ln_g', 'ln_b', 'rw_mu', 'rw_w0', 'rw_w1', 'rw_w2', 'rw_a0', 'rw_a1', 'rw_a2', 'rw_g1', 'rw_g2', 'rw_k_k',
           'rw_k_a', 'rw_r_k', 'rw_wr', 'rw_wk', 'rw_wv', 'rw_wo', 'rw_lnx_g', 'rw_lnx_b', 's5_a_re', 's5_a_im',
           's5_log_dt', 's5_b_re', 's5_b_im', 's5_c_re', 's5_c_im', 's5_d', 's5_w_glu', 'mlp_w1', 'mlp_w2']
REPLICATED = [n for n in WEIGHTS if n not in SHARDED]
BIG = ['rw_wr', 'rw_wk', 'rw_wv', 'rw_wo', 's5_w_glu', 'mlp_w1', 'mlp_w2']
LATE = ['s5_w_glu', 'mlp_w1', 'mlp_w2']


def _cparams(n_grid):
    return pltpu.CompilerParams(dimension_semantics=("arbitrary",) * n_grid, vmem_limit_bytes=VMEM_LIMIT)


@jax.custom_vjp
def _mm(x, w):
    return jnp.dot(x.astype(BF16), w.astype(BF16), preferred_element_type=F32)


def _mm_fwd(x, w):
    return _mm(x, w), (x, w)


def _mm_bwd(res, dy):
    x, w = res
    dyb = dy.astype(BF16)
    dx = lax.dot_general(dyb, w.astype(BF16), (((1,), (1,)), ((), ())), preferred_element_type=F32)
    dw = lax.dot_general(x.astype(BF16), dyb, (((0,), (0,)), ((), ())), preferred_element_type=F32)
    return dx, dw


_mm.defvjp(_mm_fwd, _mm_bwd)


_DOT_DIMS = {'nn': (((1,), (0,)), ((), ())), 'nt': (((1,), (1,)), ((), ())), 'tn': (((0,), (0,)), ((), ()))}
_BATCH_DOT_DIMS = {'nn': (((2,), (1,)), ((0,), (0,))), 'nt': (((2,), (2,)), ((0,), (0,))), 'tn': (((1,), (1,)), ((0,), (0,)))}
CHUNK_PAIRS = 8
CHUNK_PASSES = 1


def _split_bf16(a):
    hi = a.astype(BF16)
    return hi, (a - hi.astype(F32)).astype(BF16)


def _dot_passes(a, b, mode, passes=None):
    dims = (_DOT_DIMS if a.ndim == 2 else _BATCH_DOT_DIMS)[mode]
    dot = lambda p, q: lax.dot_general(p, q, dims, preferred_element_type=F32)
    if (CHUNK_PASSES if passes is None else passes) == 1:
        return dot(a.astype(BF16), b.astype(BF16))
    (ah, al), (bh, bl) = _split_bf16(a), _split_bf16(b)
    return dot(ah, bh) + (dot(ah, bl) + dot(al, bh))


@functools.partial(jax.custom_vjp, nondiff_argnums=(2,))
def _pdot(a, b, mode):
    return _dot_passes(a, b, mode)


def _pdot_fwd(a, b, mode):
    return _dot_passes(a, b, mode), (a, b)


def _pdot_bwd(mode, res, dy):
    a, b = res
    if mode == 'nn':
        return _dot_passes(dy, b, 'nt'), _dot_passes(a, dy, 'tn')
    if mode == 'nt':
        return _dot_passes(dy, b, 'nn'), _dot_passes(dy, a, 'tn')
    return _dot_passes(b, dy, 'nt'), _dot_passes(a, dy, 'nn')


_pdot.defvjp(_pdot_fwd, _pdot_bwd)


def _tri_sum(x, upper):
    nb, n = x.shape[0], x.shape[1]
    ti = lax.broadcasted_iota(jnp.int32, (nb, n, n), 1)
    tj = lax.broadcasted_iota(jnp.int32, (nb, n, n), 2)
    tri = ((ti <= tj) if upper else (ti >= tj)).astype(BF16)
    hi = x.astype(BF16)
    rest = x - hi.astype(F32)
    mid = rest.astype(BF16)
    lo = (rest - mid.astype(F32)).astype(BF16)
    dot = lambda q: lax.dot_general(tri, q, _BATCH_DOT_DIMS['nn'], preferred_element_type=F32)
    return dot(hi) + (dot(mid) + dot(lo))


@jax.custom_vjp
def _cumsum_rows(x):
    return _tri_sum(x, False)


_cumsum_rows.defvjp(lambda x: (_tri_sum(x, False), None), lambda _, dy: (_tri_sum(dy, True),))


@functools.partial(jax.custom_vjp, nondiff_argnums=(1,))
def _roll_rows(x, shift):
    return pltpu.roll(x, shift, 0)


def _roll_rows_fwd(x, shift):
    return pltpu.roll(x, shift, 0), None


def _roll_rows_bwd(shift, _, dy):
    n = dy.shape[0]
    return (pltpu.roll(dy, (n - shift) % n, 0),)


_roll_rows.defvjp(_roll_rows_fwd, _roll_rows_bwd)


def _shift_down(x, first_row):
    row = lax.broadcasted_iota(jnp.int32, x.shape, 0)
    return jnp.where(row == 0, first_row, _roll_rows(x, 1))


def _sigmoid(x):
    return 1.0 / (1.0 + jnp.exp(-x))


def _softplus(x):
    return jnp.maximum(x, 0.0) + jnp.log(1.0 + jnp.exp(-jnp.abs(x)))


def _gelu(x):
    return 0.5 * x * (1.0 + jnp.tanh(math.sqrt(2.0 / math.pi) * (x + 0.044715 * (x * x * x))))


def _layer_norm(x, g, b):
    mu = jnp.mean(x, axis=-1, keepdims=True)
    xc = x - mu
    var = jnp.mean(xc * xc, axis=-1, keepdims=True)
    return xc * lax.rsqrt(var + LN_EPS) * g + b


def _spec(block, index_map):
    return pl.BlockSpec(block, index_map)


def _tile(arr, tm):
    return (arr, (tm, arr.shape[1]), lambda i: (i, 0))


def _full(arr):
    nd = arr.ndim
    return (arr, arr.shape, lambda *_: (0,) * nd)


def _fwd_call(name, fn, grid, prims, outs):
    n_in = len(prims)

    def body(*refs):
        vals = [r[...] for r in refs[:n_in]]
        vals = [v.astype(F32) if v.dtype != F32 else v for v in vals]
        res = fn(*vals)
        for o, r in zip(refs[n_in:], res):
            o[...] = r.astype(o.dtype)

    return pl.pallas_call(
        body, name=name, grid=grid,
        in_specs=[_spec(b, m) for _, b, m in prims],
        out_specs=[_spec(b, m) for _, _, b, m in outs],
        out_shape=[jax.ShapeDtypeStruct(s, d) for s, d, _, _ in outs],
        compiler_params=_cparams(len(grid)),
    )(*[a for a, _, _ in prims])


def _bwd_call(name, fn, grid, prims, cts, wants, addends=None):
    addends = addends or {}
    n_in, n_ct = len(prims), len(cts)
    out_place = {}
    wants, out_dtype = list(wants), {}
    for i, w in enumerate(wants):
        if isinstance(w, tuple):
            wants[i], out_place[i] = w
        elif w == 'tile_bf16':
            wants[i], out_dtype[i] = 'tile', BF16
    diff = [i for i, w in enumerate(wants) if w]
    add_idx = [i for i in diff if i in addends]
    last_axis = len(grid) - 1

    def body(*refs):
        vals = [r[...] for r in refs[:n_in]]
        vals = [v.astype(F32) if v.dtype != F32 else v for v in vals]
        ct_vals = [r[...] for r in refs[n_in:n_in + n_ct]]
        ct_vals = [v.astype(F32) if v.dtype != F32 else v for v in ct_vals]
        add_refs = dict(zip(add_idx, refs[n_in + n_ct:n_in + n_ct + len(add_idx)]))
        out_refs = refs[n_in + n_ct + len(add_idx):]

        def f(*dargs):
            full = list(vals)
            for i, a in zip(diff, dargs):
                full[i] = a
            return tuple(fn(*full))

        _, vjp = jax.vjp(f, *[vals[i] for i in diff])
        grads = vjp(tuple(ct_vals))
        first = pl.program_id(last_axis) == 0
        for o, g, i in zip(out_refs, grads, diff):
            if wants[i] == 'tile':
                if i in add_refs:
                    g = g + add_refs[i][...]
                o[...] = g.astype(o.dtype)
            else:
                @pl.when(first)
                def _(o=o):
                    o[...] = jnp.zeros(o.shape, o.dtype)
                o[...] += g

    ins = list(prims) + list(cts) + [addends[i] for i in add_idx]
    places = [out_place.get(i, (prims[i][0].shape, prims[i][1], prims[i][2])) for i in diff]
    return pl.pallas_call(
        body, name=name, grid=grid,
        in_specs=[_spec(b, m) for _, b, m in ins],
        out_specs=[_spec(b, m) for _, b, m in places],
        out_shape=[jax.ShapeDtypeStruct(s, out_dtype.get(i, F32)) for i, (s, _, _) in zip(diff, places)],
        compiler_params=_cparams(len(grid)),
    )(*[a for a, _, _ in ins])


def _matmul(name, a, b, mode, *, out_dtype=F32, lhs_fn=None, epi=None, extra=None, tm=1024, tn=1024, tk=1024,
            b_cols=None, b_rows=None, out_cols=False):
    if mode == 'tn':
        kdim, m = a.shape
    else:
        m, kdim = a.shape
    if b_rows is not None:
        loc, cols = b.shape[-2], b.shape[-1]
        if mode == 'nn':
            n, k_shards = cols, max(1, tk // loc)
            tk = k_shards * loc
        else:
            n, tn = N_DEV * loc, loc
    elif b_cols is None:
        n = b.shape[0] if mode == 'nt' else b.shape[1]
    else:
        rows, loc = b.shape[-2], b.shape[-1]
        if mode == 'nn':
            n, tn = N_DEV * loc, loc
        else:
            n, k_shards = rows, max(1, tk // loc)
            tk = k_shards * loc
    if out_cols:
        tn = n // N_DEV
    tm, tn, tk = min(tm, m), min(tn, n), min(tk, kdim)
    nk = kdim // tk
    grid = (m // tm, n // tn, nk)
    a_spec = _spec((tk, tm), lambda i, j, k: (k, i)) if mode == 'tn' else _spec((tm, tk), lambda i, j, k: (i, k))
    if b_rows is not None:
        lead = tuple(b_rows)
        skip = (None,) * (1 + len(lead))
        if mode == 'nn':
            b_spec = _spec((k_shards,) + skip[1:] + (tk // k_shards, tn), lambda i, j, k: (k,) + lead + (0, j))
        else:
            b_spec = _spec(skip + (tn, tk), lambda i, j, k: (j,) + lead + (0, k))
    elif b_cols is None:
        b_spec = _spec((tn, tk), lambda i, j, k: (j, k)) if mode == 'nt' else _spec((tk, tn), lambda i, j, k: (k, j))
    else:
        lead = tuple(b_cols)
        skip = (None,) * (1 + len(lead))
        if mode == 'nn':
            b_spec = _spec(skip + (tk, tn), lambda i, j, k: (j,) + lead + (k, 0))
        else:
            b_spec = _spec((k_shards,) + skip[1:] + (tn, tk // k_shards), lambda i, j, k: (k,) + lead + (j, 0))
    o_spec = _spec((tm, tn), lambda i, j, k: (i, j))
    if out_cols:
        o_place, o_shape = _spec((None, tm, tn), lambda i, j, k: (j, i, 0)), (N_DEV, m, tn)
    else:
        o_place, o_shape = o_spec, (m, n)
    dims = _DOT_DIMS[mode]

    def body(*refs):
        if extra is None:
            a_ref, b_ref, o_ref, acc = refs
            x_ref = None
        else:
            a_ref, b_ref, x_ref, o_ref, acc = refs
        k = pl.program_id(2)
        av = a_ref[...]
        if lhs_fn is not None:
            av = lhs_fn(av.astype(F32))
        bv = b_ref[...]
        if bv.ndim == 3:
            bv = bv.reshape(tk, tn) if mode == 'nn' else jnp.concatenate([bv[g] for g in range(bv.shape[0])], axis=1)
        part = lax.dot_general(av.astype(BF16), bv.astype(BF16), dims, preferred_element_type=F32)

        def finish(r):
            if epi is not None:
                r = epi(r, x_ref[...])
            o_ref[...] = r.astype(o_ref.dtype)

        if nk == 1:
            finish(part)
            return

        @pl.when(k == 0)
        def _():
            acc[...] = part

        @pl.when((k > 0) & (k < nk - 1))
        def _():
            acc[...] += part

        @pl.when(k == nk - 1)
        def _():
            finish(acc[...] + part)

    ops, specs = [a, b], [a_spec, b_spec]
    if extra is not None:
        ops.append(extra)
        specs.append(o_spec)
    return pl.pallas_call(
        body, name=name, grid=grid, in_specs=specs, out_specs=o_place,
        out_shape=jax.ShapeDtypeStruct(o_shape, out_dtype),
        scratch_shapes=[pltpu.VMEM((tm, tn), F32)],
        compiler_params=_cparams(3),
    )(*ops)


def _relu2(x):
    r = jnp.maximum(x, 0.0)
    return r * r


def _mix_fn(x, halo, mu, w0, w1, w2, a0, a1, a2, g1, g2):
    prev_row = jnp.where(pl.program_id(0) == 0, 0.0, halo[7:8, :])
    xx = _shift_down(x, prev_row) - x
    xr, xw, xk, xv, xa, xg = (x + xx * mu[i:i + 1, :] for i in range(6))
    w_pre = w0 + _mm(jnp.tanh(_mm(xw, w1)), w2)
    log_decay = -jnp.exp(-_softplus(-w_pre) - 0.5)
    a = _sigmoid(a0 + _mm(_mm(xa, a1), a2))
    g = _mm(_sigmoid(_mm(xg, g1)), g2)
    return xr, xk, xv, log_decay, a, g


def _chunk_fn(r, lw, k, v, a, k_k, k_a, r_k):
    n, n_pair = r.shape[0], r.shape[1] // PAIR
    to_batch = lambda t: jnp.concatenate([t[None, :, p * PAIR:(p + 1) * PAIR] for p in range(n_pair)], axis=0)
    from_batch = lambda t: jnp.concatenate([t[p] for p in range(n_pair)], axis=1)
    r, lw, k, v, a, k_k, k_a, r_k = (to_batch(t) for t in (r, lw, k, v, a, k_k, k_a, r_k))
    lane = lax.broadcasted_iota(jnp.int32, (1, 1, PAIR), 2)
    m0 = lane < HEAD

    def head_sum(t):
        s0 = jnp.sum(jnp.where(m0, t, 0.0), axis=2, keepdims=True)
        s1 = jnp.sum(jnp.where(m0, 0.0, t), axis=2, keepdims=True)
        return jnp.where(m0, s0, s1)

    kk = k * k_k
    kk = kk / jnp.maximum(jnp.sqrt(head_sum(kk * kk)), 1e-12)
    k2 = k * (1.0 + (a - 1.0) * k_a)
    aa, bb = -kk, kk * a
    bonus = head_sum(r * k2 * r_k) * v

    cum = _cumsum_rows(lw)
    p_in, p_ex, p_inv = jnp.exp(cum), jnp.exp(cum - lw), jnp.exp(-cum)
    at, rt, kt, bt = aa * p_ex, r * p_in, k2 * p_inv, bb * p_inv

    def stack_masked(t):
        return jnp.concatenate([jnp.where(m0, t, 0.0), jnp.where(m0, 0.0, t)], axis=1)

    def unstack_sum(t):
        return t[:, :n] + t[:, n:]

    def unstack_select(t):
        return jnp.where(m0, t[:, :n], t[:, n:])

    ti = lax.broadcasted_iota(jnp.int32, (1, 2 * n, 2 * n), 1)
    tj = lax.broadcasted_iota(jnp.int32, (1, 2 * n, 2 * n), 2)
    same = (ti >= n) == (tj >= n)
    incl, strict = same & (ti >= tj), same & (ti > tj)
    eye_n = (ti == tj).astype(F32)
    m = 2 * n
    rows = lambda *ts: jnp.concatenate(ts, axis=1)
    cols = lambda *ts: jnp.concatenate(ts, axis=2)
    at_s, rt_s, kt_s, bt_s = stack_masked(at), stack_masked(rt), stack_masked(kt), stack_masked(bt)
    v_s, zeros = rows(v, v), jnp.zeros_like(at_s)
    bk_s = rows(bt_s, kt_s)
    gram = _pdot(rows(at_s, rt_s), bk_s, 'nt')
    a_ab = jnp.where(strict, gram[:, :m, :m], 0.0)
    a_ak = jnp.where(strict, gram[:, :m, m:], 0.0)
    a_rb = jnp.where(incl, gram[:, m:, :m], 0.0)
    a_rk = jnp.where(incl, gram[:, m:, m:], 0.0)
    pw, inv = _pdot(a_ab, a_ab, 'nn'), eye_n + a_ab
    for _ in range(int(math.log2(n)) - 2):
        both = _pdot(pw, cols(pw, inv), 'nn')
        pw, inv = both[:, :, :m], inv + both[:, :, m:]
    inv = inv + _pdot(pw, inv, 'nn')
    wy_s = _pdot(inv, cols(at_s, _pdot(a_ak, v_s, 'nn')), 'nn')
    q_oi = _pdot(cols(a_rb, a_rk), rows(wy_s, cols(zeros, v_s)), 'nn')
    q_all = unstack_sum(rt_s + q_oi[:, :, :PAIR])
    oi_all = unstack_select(q_oi[:, :, PAIR:])
    own_head = (lax.broadcasted_iota(jnp.int32, (1, m, PAIR), 1) >= n) == (lax.broadcasted_iota(jnp.int32, (1, m, PAIR), 2) >= HEAD)
    wy_m = cols(wy_s[:, :, :PAIR], jnp.where(own_head, wy_s[:, :, PAIR:], 0.0))
    mm_nn = _pdot(bk_s, rows(wy_m, cols(zeros, stack_masked(v))), 'tn')
    mm, nn = mm_nn[:, :, :PAIR], mm_nn[:, :, PAIR:]
    ei = lax.broadcasted_iota(jnp.int32, (1, PAIR, PAIR), 1)
    ej = lax.broadcasted_iota(jnp.int32, (1, PAIR, PAIR), 2)
    eye_p = (ei == ej).astype(F32)
    decay_col = jnp.sum(eye_p * p_in[:, n - 1:n, :], axis=2, keepdims=True)
    return ((decay_col * (eye_p + mm))[None], (decay_col * nn)[None], from_batch(q_all), from_batch(oi_all),
            from_batch(bonus))


def _gate_fn(o, bonus, g, lnx_g, lnx_b):
    lane = lax.broadcasted_iota(jnp.int32, (1, PAIR), 1)
    m0 = lane < HEAD

    def head_mean(t):
        s0 = jnp.sum(jnp.where(m0, t, 0.0), axis=1, keepdims=True)
        s1 = jnp.sum(jnp.where(m0, 0.0, t), axis=1, keepdims=True)
        return jnp.where(m0, s0, s1) * (1.0 / HEAD)

    outs = []
    for p in range(N_PAIR):
        sl = slice(p * PAIR, (p + 1) * PAIR)
        oc = o[:, sl] - head_mean(o[:, sl])
        on = oc * lax.rsqrt(head_mean(oc * oc) + GN_EPS) * lnx_g[:, sl] + lnx_b[:, sl]
        outs.append((on + bonus[:, sl]) * g[:, sl])
    return (jnp.concatenate(outs, axis=1),)


def _res_ln_fn(h, y, g, b):
    return (_layer_norm(DN_ALPHA * h + y, g, b),)


def _glu_ln_fn(h, z, g, b):
    mix = z[:, :D_MODEL] * _sigmoid(z[:, D_MODEL:])
    return (_layer_norm(DN_ALPHA * h + mix, g, b),)


def _s5_param_fn(a_re, a_im, log_dt, b_re, b_im):
    dt = jnp.exp(log_dt)
    lam_re = jnp.minimum(a_re, -1e-4)
    lam_im = a_im
    mag = jnp.exp(dt * lam_re)
    abar_re = mag * jnp.cos(dt * lam_im)
    abar_im = mag * jnp.sin(dt * lam_im)
    den = lam_re * lam_re + lam_im * lam_im
    nr, ni = abar_re - 1.0, abar_im
    coef_re = (nr * lam_re + ni * lam_im) / den
    coef_im = (ni * lam_re - nr * lam_im) / den
    return abar_re, abar_im, coef_re, coef_im


def _s5_bbar_fn(coef_re, coef_im, b_re, b_im):
    return coef_re * b_re - coef_im * b_im, coef_re * b_im + coef_im * b_re


def _s5_in_fn(u, b_re, b_im):
    return _pdot(u, b_re[0], 'nn'), _pdot(u, b_im[0], 'nn')


def _s5_out_fn(s_re, s_im, u, c_re, c_im, d):
    y = _pdot(s_re, c_re[0], 'nn') - _pdot(s_im, c_im[0], 'nn') + u * d
    return (_gelu(y),)


def _rwkv_state_fwd(m_all, n_all, q, oi):
    n_chunk = m_all.shape[0]

    def body(m_ref, n_ref, q_ref, oi_ref, o_ref, zs_ref, z):
        @pl.when(pl.program_id(0) == 0)
        def _():
            z[...] = jnp.zeros(z.shape, F32)

        for j in range(N_PAIR):
            sl = slice(j * PAIR, (j + 1) * PAIR)
            zj = z[j]
            zs_ref[0, j] = zj
            o_ref[:, sl] = _dot_passes(q_ref[:, sl], zj, 'nn', 3) + oi_ref[:, sl]
            z[j] = _dot_passes(m_ref[0, j], zj, 'nn', 3) + n_ref[0, j]

    st = _spec((1, N_PAIR, PAIR, PAIR), lambda c: (c, 0, 0, 0))
    tok = _spec((CHUNK, D_MODEL), lambda c: (c, 0))
    return pl.pallas_call(
        body, name="rwkv_state_fwd", grid=(n_chunk,),
        in_specs=[st, st, tok, tok], out_specs=[tok, st],
        out_shape=[jax.ShapeDtypeStruct(q.shape, F32), jax.ShapeDtypeStruct(m_all.shape, F32)],
        scratch_shapes=[pltpu.VMEM((N_PAIR, PAIR, PAIR), F32)],
        compiler_params=_cparams(1),
    )(m_all, n_all, q, oi)


def _rwkv_state_bwd(m_all, q, zs, d_o):
    n_chunk = m_all.shape[0]

    def body(m_ref, q_ref, zs_ref, do_ref, dq_ref, dm_ref, dn_ref, g):
        @pl.when(pl.program_id(0) == 0)
        def _():
            g[...] = jnp.zeros(g.shape, F32)

        bi = lax.broadcasted_iota(jnp.int32, (PAIR, PAIR), 0) // HEAD
        bj = lax.broadcasted_iota(jnp.int32, (PAIR, PAIR), 1) // HEAD
        same_head = bi == bj
        for j in range(N_PAIR):
            sl = slice(j * PAIR, (j + 1) * PAIR)
            gj, zj, do = g[j], zs_ref[0, j], do_ref[:, sl]
            dn_ref[0, j] = gj
            dm_ref[0, j] = _dot_passes(gj, zj, 'nt', 3)
            dq_ref[:, sl] = _dot_passes(do, zj, 'nt', 3)
            g[j] = _dot_passes(m_ref[0, j], gj, 'tn', 3) + jnp.where(same_head, _dot_passes(q_ref[:, sl], do, 'tn', 3), 0.0)

    st = _spec((1, N_PAIR, PAIR, PAIR), lambda c: (n_chunk - 1 - c, 0, 0, 0))
    tok = _spec((CHUNK, D_MODEL), lambda c: (n_chunk - 1 - c, 0))
    return pl.pallas_call(
        body, name="rwkv_state_bwd", grid=(n_chunk,),
        in_specs=[st, tok, st, tok], out_specs=[tok, st, st],
        out_shape=[jax.ShapeDtypeStruct(q.shape, F32), jax.ShapeDtypeStruct(m_all.shape, F32),
                   jax.ShapeDtypeStruct(m_all.shape, F32)],
        scratch_shapes=[pltpu.VMEM((N_PAIR, PAIR, PAIR), F32)],
        compiler_params=_cparams(1),
    )(m_all, q, zs, d_o)


S5_TIME_TILE = 512
S5_GROUP = 8


def _scan_rows(re_ref, im_ref, a1, carry, reverse):
    lb = re_ref.shape[1]
    grp, n_grp = S5_GROUP, re_ref.shape[0] // S5_GROUP

    def cmul(xr, xi, yr, yi):
        return xr * yr - xi * yi, xr * yi + xi * yr

    a2 = cmul(*a1, *a1)
    a4 = cmul(*a2, *a2)
    a8 = cmul(*a4, *a4)
    row = lax.broadcasted_iota(jnp.int32, (grp, lb), 0)
    expo = (grp - row) if reverse else (row + 1)
    pw = (jnp.ones((grp, lb), F32), jnp.zeros((grp, lb), F32))
    for bit, ap in ((1, a1), (2, a2), (4, a4), (8, a8)):
        nxt = cmul(*pw, *ap)
        sel = (expo & bit) != 0
        pw = (jnp.where(sel, nxt[0], pw[0]), jnp.where(sel, nxt[1], pw[1]))

    steps = []
    for d, ad in ((1, a1), (2, a2), (4, a4)):
        keep = (row < grp - d) if reverse else (row >= d)
        steps.append(((grp - d) if reverse else d, jnp.where(keep, ad[0], 0.0), jnp.where(keep, ad[1], 0.0)))

    def group(i, c):
        gi = (n_grp - 1 - i) if reverse else i
        rows = pl.ds(pl.multiple_of(gi * grp, grp), grp)
        xr, xi = re_ref[rows, :], im_ref[rows, :]
        for shift, mr, mi in steps:
            pr, pi = cmul(mr, mi, pltpu.roll(xr, shift, 0), pltpu.roll(xi, shift, 0))
            xr, xi = xr + pr, xi + pi
        cr, ci = cmul(*pw, c[0], c[1])
        xr, xi = xr + cr, xi + ci
        re_ref[rows, :] = xr
        im_ref[rows, :] = xi
        edge = slice(0, 1) if reverse else slice(grp - 1, grp)
        return xr[edge, :], xi[edge, :]

    return lax.fori_loop(0, n_grp, group, carry)


def _s5_fwd(u, b_re, b_im, a_re, a_im, c_re, c_im, d_row):
    n_tok = u.shape[0]
    tt = min(S5_TIME_TILE, n_tok)

    def body(u_ref, bre_ref, bim_ref, are_ref, aim_ref, cre_ref, cim_ref, d_ref, sre_ref, sim_ref, yg_ref, carry):
        @pl.when(pl.program_id(1) == 0)
        def _():
            carry[...] = jnp.zeros(carry.shape, F32)

        uv = u_ref[...]
        sre_ref[...], sim_ref[...] = _s5_in_fn(uv, bre_ref[...], bim_ref[...])
        c = _scan_rows(sre_ref, sim_ref, (are_ref[...], aim_ref[...]), (carry[0:1, :], carry[1:2, :]), False)
        carry[0:1, :] = c[0]
        carry[1:2, :] = c[1]
        (yg,) = _s5_out_fn(sre_ref[...], sim_ref[...], uv, cre_ref[...], cim_ref[...], d_ref[...])
        yg_ref[...] = yg.astype(yg_ref.dtype)

    u_blk = _spec((tt, SSM_BLK_IN), lambda l, t: (t, l))
    s_blk = _spec((tt, SSM_BLK_ST), lambda l, t: (t, l))
    blk3 = lambda arr: _spec((1,) + arr.shape[1:], lambda l, t: (l, 0, 0))
    par = lambda width: _spec((1, width), lambda l, t: (0, l))
    return pl.pallas_call(
        body, name="s5_fwd", grid=(N_SSM_BLK, n_tok // tt),
        in_specs=[u_blk, blk3(b_re), blk3(b_im), par(SSM_BLK_ST), par(SSM_BLK_ST), blk3(c_re), blk3(c_im), par(SSM_BLK_IN)],
        out_specs=[s_blk, s_blk, u_blk],
        out_shape=[jax.ShapeDtypeStruct((n_tok, SSM_LANES), F32)] * 2 + [jax.ShapeDtypeStruct((n_tok, D_MODEL), BF16)],
        scratch_shapes=[pltpu.VMEM((2, SSM_BLK_ST), F32)],
        compiler_params=_cparams(2),
    )(u, b_re, b_im, a_re, a_im, c_re, c_im, d_row)


def _s5_bwd(dyg, u, s_re, s_im, dh_res, b_re, b_im, a_re, a_im, c_re, c_im, d_row):
    n_tok = u.shape[0]
    tt = min(S5_TIME_TILE, n_tok)
    n_t = n_tok // tt

    def body(dyg_ref, u_ref, sre_ref, sim_ref, hre_ref, him_ref, res_ref, bre_ref, bim_ref, are_ref, aim_ref,
             cre_ref, cim_ref, d_ref, du_ref, dbre_ref, dbim_ref, dcre_ref, dcim_ref, dd_ref, dare_ref, daim_ref,
             carry, gre, gim):
        i = pl.program_id(1)
        sums = (dbre_ref, dbim_ref, dcre_ref, dcim_ref, dd_ref, dare_ref, daim_ref)

        @pl.when(i == 0)
        def _():
            carry[...] = jnp.zeros(carry.shape, F32)
            for r in sums:
                r[...] = jnp.zeros(r.shape, F32)

        uv, sre, sim = u_ref[...], sre_ref[...], sim_ref[...]
        _, out_vjp = jax.vjp(_s5_out_fn, sre, sim, uv, cre_ref[...], cim_ref[...], d_ref[...])
        gre[...], gim[...], du_out, dcre, dcim, dd = out_vjp((dyg_ref[...],))
        c = _scan_rows(gre, gim, (are_ref[...], -aim_ref[...]), (carry[0:1, :], carry[1:2, :]), True)
        carry[0:1, :] = c[0]
        carry[1:2, :] = c[1]
        g_re, g_im = gre[...], gim[...]
        at_start = i == n_t - 1
        p_re = _shift_down(sre, jnp.where(at_start, 0.0, hre_ref[7:8, :]))
        p_im = _shift_down(sim, jnp.where(at_start, 0.0, him_ref[7:8, :]))
        _, in_vjp = jax.vjp(_s5_in_fn, uv, bre_ref[...], bim_ref[...])
        du_in, dbre, dbim = in_vjp((g_re, g_im))
        du_ref[...] = du_out + du_in + res_ref[...]
        dare = jnp.sum(g_re * p_re + g_im * p_im, axis=0, keepdims=True)
        daim = jnp.sum(g_im * p_re - g_re * p_im, axis=0, keepdims=True)
        for r, val in zip(sums, (dbre, dbim, dcre, dcim, dd, dare, daim)):
            r[...] += val

    u_blk = _spec((tt, SSM_BLK_IN), lambda l, t: (n_t - 1 - t, l))
    s_blk = _spec((tt, SSM_BLK_ST), lambda l, t: (n_t - 1 - t, l))
    halo = _spec((8, SSM_BLK_ST), lambda l, t: (jnp.maximum((n_t - 1 - t) * (tt // 8) - 1, 0), l))
    blk3 = lambda arr: _spec((1,) + arr.shape[1:], lambda l, t: (l, 0, 0))
    par = lambda width: _spec((1, width), lambda l, t: (0, l))
    params = [blk3(b_re), blk3(b_im), par(SSM_BLK_ST), par(SSM_BLK_ST), blk3(c_re), blk3(c_im), par(SSM_BLK_IN)]
    shape = lambda arr: jax.ShapeDtypeStruct(arr.shape, F32)
    return pl.pallas_call(
        body, name="s5_bwd", grid=(N_SSM_BLK, n_t),
        in_specs=[u_blk, u_blk, s_blk, s_blk, halo, halo, u_blk] + params,
        out_specs=[u_blk, blk3(b_re), blk3(b_im), blk3(c_re), blk3(c_im), par(SSM_BLK_IN), par(SSM_BLK_ST), par(SSM_BLK_ST)],
        out_shape=[shape(u), shape(b_re), shape(b_im), shape(c_re), shape(c_im), shape(d_row), shape(a_re), shape(a_im)],
        scratch_shapes=[pltpu.VMEM((2, SSM_BLK_ST), F32), pltpu.VMEM((tt, SSM_BLK_ST), F32), pltpu.VMEM((tt, SSM_BLK_ST), F32)],
        compiler_params=_cparams(2),
    )(dyg, u, s_re, s_im, s_re, s_im, dh_res, b_re, b_im, a_re, a_im, c_re, c_im, d_row)


def _loss_call(h, target, tm):
    n_tok, d = h.shape

    def body(h_ref, t_ref, acc_ref, dh_ref):
        @pl.when(pl.program_id(0) == 0)
        def _():
            acc_ref[...] = jnp.zeros(acc_ref.shape, F32)

        e = h_ref[...] - t_ref[...]
        dh_ref[...] = e * (1.0 / d)
        acc_ref[...] += jnp.sum(jnp.sum(e * e, axis=1, keepdims=True), axis=0, keepdims=True)

    tok = _spec((tm, d), lambda i: (i, 0))
    return pl.pallas_call(
        body, name="loss", grid=(n_tok // tm,),
        in_specs=[tok, tok], out_specs=[_spec((8, 128), lambda i: (0, 0)), tok],
        out_shape=[jax.ShapeDtypeStruct((8, 128), F32), jax.ShapeDtypeStruct(h.shape, F32)],
        compiler_params=_cparams(1),
    )(h, target)


def _block_diag(t):
    nb, ng, rows, cols = t.shape
    eye = jnp.eye(ng, dtype=t.dtype)
    return jnp.einsum('bgrc,gh->bgrhc', t, eye).reshape(nb, ng * rows, ng * cols)


def _block_diag_t(t, rows, cols):
    nb, ng = t.shape[0], t.shape[1] // rows
    t5 = t.reshape(nb, ng, rows, ng, cols)
    return jnp.einsum('bgrhc,gh->bgrc', t5, jnp.eye(ng, dtype=t.dtype))


def _mlp_fwd(layer, h, w1g, w2g):
    pre = _matmul(f"mlp{layer}_up", h, w1g, 'nn', b_cols=(layer,), out_dtype=BF16)
    return pre, _matmul(f"mlp{layer}_down", pre, w2g, 'nn', lhs_fn=_relu2, b_rows=(layer,))


def _mlp_bwd(layer, h, pre, dy, dh_res, w1g, w2g):
    dpre = _matmul(f"mlp{layer}_down_dx", dy, w2g, 'nt', epi=lambda acc, p: acc * (2.0 * jnp.maximum(p, 0.0)),
                   extra=pre, b_rows=(layer,), out_dtype=BF16)
    dw2 = _matmul(f"mlp{layer}_down_dw", pre, dy, 'tn', lhs_fn=_relu2)
    dw1 = _matmul(f"mlp{layer}_up_dw", h, dpre, 'tn', out_cols=True)
    dh = _matmul(f"mlp{layer}_up_dx", dpre, w1g, 'nt', epi=lambda acc, e: acc + e, extra=dh_res, b_cols=(layer,))
    return dh, dw1, dw2


def _local_step(x, target, w, wg, late_weights, exchange):
    n_tok = x.shape[0]
    tm = min(TOKEN_TILE, n_tok)
    n_tile, n_chunk = n_tok // tm, n_tok // CHUNK
    row = lambda v: v.reshape(1, -1)
    big = {n: wg[n].reshape(D_MODEL, D_MODEL) for n in ('rw_wr', 'rw_wk', 'rw_wv', 'rw_wo')}
    ln_g, ln_b = w['ln_g'], w['ln_b']
    grads = {}
    to_slots = lambda t: t.reshape(N_DEV, t.shape[0] // N_DEV, t.shape[1])

    halo_spec = (8, D_MODEL), lambda i: (jnp.maximum(i * (tm // 8) - 1, 0), 0)
    mix_p = [_tile(x, tm), (x,) + halo_spec, _full(w['rw_mu'][0]), _full(row(w['rw_w0'])), _full(w['rw_w1'][0]),
             _full(w['rw_w2'][0]), _full(row(w['rw_a0'])), _full(w['rw_a1'][0]), _full(w['rw_a2'][0]),
             _full(w['rw_g1'][0]), _full(w['rw_g2'][0])]
    tok_out = lambda dt: ((n_tok, D_MODEL), dt, (tm, D_MODEL), lambda i: (i, 0))
    xr, xk, xv, lw, a, g = _fwd_call("rwkv_mix", _mix_fn, (n_tile,), mix_p,
                                     [tok_out(BF16), tok_out(BF16), tok_out(BF16), tok_out(F32), tok_out(F32), tok_out(F32)])
    r = _matmul("rwkv_r", xr, big['rw_wr'], 'nn')
    k = _matmul("rwkv_k", xk, big['rw_wk'], 'nn')
    v = _matmul("rwkv_v", xv, big['rw_wv'], 'nn')

    slab_w = CHUNK_PAIRS * PAIR
    slab = lambda arr: (arr, (CHUNK, slab_w), lambda j, c: (c, j))
    pslab = lambda arr: (arr, (1, slab_w), lambda j, c: (0, j))
    chunk_p = [slab(r), slab(lw), slab(k), slab(v), slab(a), pslab(row(w['rw_k_k'])), pslab(row(w['rw_k_a'])),
               pslab(w['rw_r_k'].reshape(1, -1))]
    st_shape = (n_chunk, N_PAIR, PAIR, PAIR)
    st_out = (st_shape, F32, (1, CHUNK_PAIRS, PAIR, PAIR), lambda j, c: (c, j, 0, 0))
    sl_out = ((n_tok, D_MODEL), F32, (CHUNK, slab_w), lambda j, c: (c, j))
    chunk_grid = (N_PAIR // CHUNK_PAIRS, n_chunk)
    m_all, n_all, q, oi, bonus = _fwd_call("rwkv_chunk", _chunk_fn, chunk_grid, chunk_p,
                                           [st_out, st_out, sl_out, sl_out, sl_out])
    o, zs = _rwkv_state_fwd(m_all, n_all, q, oi)

    gate_p = [_tile(o, tm), _tile(bonus, tm), _tile(g, tm), _full(row(w['rw_lnx_g'])), _full(row(w['rw_lnx_b']))]
    (og,) = _fwd_call("rwkv_gate", _gate_fn, (n_tile,), gate_p, [tok_out(BF16)])
    y0 = _matmul("rwkv_o", og, big['rw_wo'], 'nn')
    ln0_p = [_tile(x, tm), _tile(y0, tm), _full(ln_g[0:1]), _full(ln_b[0:1])]
    (h1,) = _fwd_call("ln0", _res_ln_fn, (n_tile,), ln0_p, [tok_out(F32)])
    late = late_weights(h1)
    w1g, w2g, glu_g = late['mlp_w1'], late['mlp_w2'], late['s5_w_glu']
    pre0, y1 = _mlp_fwd(0, h1, w1g, w2g)
    ln1_p = [_tile(h1, tm), _tile(y1, tm), _full(ln_g[1:2]), _full(ln_b[1:2])]
    (h2,) = _fwd_call("ln1", _res_ln_fn, (n_tile,), ln1_p, [tok_out(F32)])

    gp = (SSM_GROUPS, SSM_STATE)
    par_p = [_full(w['s5_a_re'][0]), _full(w['s5_a_im'][0]), _full(w['s5_log_dt'].reshape(SSM_GROUPS, 1))]
    s5_par_fn = lambda ar, ai, ld: _s5_param_fn(ar, ai, ld, None, None)
    gp_out = (gp, F32, gp, lambda i: (0, 0))
    abar_re, abar_im, coef_re, coef_im = _fwd_call("s5_param", s5_par_fn, (1,), par_p, [gp_out] * 4)
    b_flat = [w[n][0].reshape(SSM_LANES, SSM_GROUP) for n in ('s5_b_re', 's5_b_im')]
    bbar_p = [_full(coef_re.reshape(SSM_LANES, 1)), _full(coef_im.reshape(SSM_LANES, 1)), _full(b_flat[0]), _full(b_flat[1])]
    bb_out = ((SSM_LANES, SSM_GROUP), F32, (SSM_LANES, SSM_GROUP), lambda i: (0, 0))
    bbar_re, bbar_im = _fwd_call("s5_bbar", _s5_bbar_fn, (1,), bbar_p, [bb_out] * 2)
    to_in = lambda t: _block_diag(t.reshape(N_SSM_BLK, 8, SSM_STATE, SSM_GROUP).transpose(0, 1, 3, 2))
    to_out = lambda t: _block_diag(t.reshape(N_SSM_BLK, 8, SSM_GROUP, SSM_STATE).transpose(0, 1, 3, 2))
    bblk_re, bblk_im = to_in(bbar_re), to_in(bbar_im)
    cblk_re, cblk_im = to_out(w['s5_c_re'][0]), to_out(w['s5_c_im'][0])
    a_row_re, a_row_im = abar_re.reshape(1, SSM_LANES), abar_im.reshape(1, SSM_LANES)

    d_row = row(w['s5_d'])
    s5_params = (bblk_re, bblk_im, a_row_re, a_row_im, cblk_re, cblk_im, d_row)
    s_re, s_im, yg = _s5_fwd(h2, *s5_params)
    z = _matmul("s5_glu", yg, glu_g, 'nn', b_cols=(0,))
    ln2_p = [_tile(h2, tm), _tile(z, tm), _full(ln_g[2:3]), _full(ln_b[2:3])]
    (h3,) = _fwd_call("ln2", _glu_ln_fn, (n_tile,), ln2_p, [tok_out(F32)])
    pre1, y3 = _mlp_fwd(1, h3, w1g, w2g)
    ln3_p = [_tile(h3, tm), _tile(y3, tm), _full(ln_g[3:4]), _full(ln_b[3:4])]
    (h4,) = _fwd_call("ln3", _res_ln_fn, (n_tile,), ln3_p, [tok_out(F32)])

    loss_acc, dh4 = _loss_call(h4, target, tm)

    ln_w = ['tile', 'tile_bf16', 'acc', 'acc']
    dh3_res, dy3, dg3, db3 = _bwd_call("ln3_bwd", _res_ln_fn, (n_tile,), ln3_p, [_tile(dh4, tm)], ln_w)
    dh3, dw1_1, dw2_1 = _mlp_bwd(1, h3, pre1, dy3, dh3_res, w1g, w2g)
    zero = exchange([('mlp_w1', 1, dw1_1), ('mlp_w2', 1, to_slots(dw2_1))])
    ln2_p[2] = _full(ln_g[2:3] + zero)
    dh2_res, dz, dg2, db2 = _bwd_call("ln2_bwd", _glu_ln_fn, (n_tile,), ln2_p, [_tile(dh3, tm)], ln_w)
    dyg = _matmul("s5_glu_dx", dz, glu_g, 'nt', b_cols=(0,))
    dw_glu = _matmul("s5_glu_dw", yg, dz, 'tn', out_cols=True)
    dh2, dbb_re, dbb_im, dcb_re, dcb_im, dd, dabar_re, dabar_im = _s5_bwd(dyg, h2, s_re, s_im, dh2_res, *s5_params)
    from_in = lambda t: _block_diag_t(t, SSM_GROUP, SSM_STATE).transpose(0, 1, 3, 2).reshape(SSM_LANES, SSM_GROUP)
    from_out = lambda t: _block_diag_t(t, SSM_STATE, SSM_GROUP).transpose(0, 1, 3, 2).reshape(1, SSM_GROUPS, SSM_GROUP, SSM_STATE)
    grads['s5_c_re'], grads['s5_c_im'] = from_out(dcb_re), from_out(dcb_im)
    grads['s5_d'] = dd
    dcoef_re, dcoef_im, db_re, db_im = _bwd_call(
        "s5_bbar_bwd", _s5_bbar_fn, (1,), bbar_p, [_full(from_in(dbb_re)), _full(from_in(dbb_im))], ['acc'] * 4)
    grads['s5_b_re'] = db_re.reshape(w['s5_b_re'].shape)
    grads['s5_b_im'] = db_im.reshape(w['s5_b_im'].shape)
    par_ct = [_full(dabar_re.reshape(gp)), _full(dabar_im.reshape(gp)), _full(dcoef_re.reshape(gp)), _full(dcoef_im.reshape(gp))]
    da_re, da_im, dlog_dt = _bwd_call("s5_param_bwd", s5_par_fn, (1,), par_p, par_ct, ['acc'] * 3)
    grads['s5_a_re'], grads['s5_a_im'], grads['s5_log_dt'] = da_re[None], da_im[None], dlog_dt.reshape(1, SSM_GROUPS)

    dh1_res, dy1, dg1, db1 = _bwd_call("ln1_bwd", _res_ln_fn, (n_tile,), ln1_p, [_tile(dh2, tm)], ln_w)
    dh1, dw1_0, dw2_0 = _mlp_bwd(0, h1, pre0, dy1, dh1_res, w1g, w2g)
    dx_res, dy0, dg0, db0 = _bwd_call("ln0_bwd", _res_ln_fn, (n_tile,), ln0_p, [_tile(dh1, tm)], ln_w)
    grads['ln_g'] = jnp.concatenate([dg0, dg1, dg2, dg3])
    grads['ln_b'] = jnp.concatenate([db0, db1, db2, db3])
    dog = _matmul("rwkv_o_dx", dy0, big['rw_wo'], 'nt')
    dw_o = to_slots(_matmul("rwkv_o_dw", og, dy0, 'tn'))
    zero = exchange([('s5_w_glu', 0, dw_glu), ('mlp_w1', 0, dw1_0), ('mlp_w2', 0, to_slots(dw2_0)), ('rw_wo', 0, dw_o)])
    gate_p[3] = _full(row(w['rw_lnx_g']) + zero)
    d_o, dbonus, dgate, grads['rw_lnx_g'], grads['rw_lnx_b'] = _bwd_call(
        "rwkv_gate_bwd", _gate_fn, (n_tile,), gate_p, [_tile(dog, tm)], ['tile', 'tile', 'tile', 'acc', 'acc'])
    dq, dm_all, dn_all = _rwkv_state_bwd(m_all, q, zs, d_o)
    st_ct = lambda arr: (arr, (1, CHUNK_PAIRS, PAIR, PAIR), lambda j, c: (c, j, 0, 0))
    dr, dlw, dk, dv, da, grads['rw_k_k'], grads['rw_k_a'], dr_k = _bwd_call(
        "rwkv_chunk_bwd", _chunk_fn, chunk_grid, chunk_p,
        [st_ct(dm_all), st_ct(dn_all), slab(dq), slab(d_o), slab(dbonus)],
        ['tile_bf16', 'tile', 'tile_bf16', 'tile_bf16', 'tile'] + ['acc'] * 3)
    grads['rw_r_k'] = dr_k.reshape(w['rw_r_k'].shape)
    dw_r = to_slots(_matmul("rwkv_r_dw", xr, dr, 'tn'))
    dw_k = to_slots(_matmul("rwkv_k_dw", xk, dk, 'tn'))
    dw_v = to_slots(_matmul("rwkv_v_dw", xv, dv, 'tn'))
    zero = exchange([('rw_wr', 0, dw_r), ('rw_wk', 0, dw_k), ('rw_wv', 0, dw_v)])
    mix_p[3] = _full(row(w['rw_w0']) + zero)
    dxr = _matmul("rwkv_r_dx", dr, big['rw_wr'], 'nt')
    dxk = _matmul("rwkv_k_dx", dk, big['rw_wk'], 'nt')
    dxv = _matmul("rwkv_v_dx", dv, big['rw_wv'], 'nt')
    mix_ct = [_tile(t, tm) for t in (dxr, dxk, dxv, dlw, da, dgate)]
    halo_grad = ('tile', ((n_tile * 8, D_MODEL), (8, D_MODEL), lambda i: (i, 0)))
    res = _bwd_call("rwkv_mix_bwd", _mix_fn, (n_tile,), mix_p, mix_ct, ['tile', halo_grad] + ['acc'] * 9,
                    addends={0: _tile(dx_res, tm)})
    corr = res[1].reshape(n_tile, 8, D_MODEL)[1:, 7:8, :]
    corr = jnp.pad(corr, ((0, 1), (tm - 1, 0), (0, 0)))
    dx = (res[0].reshape(n_tile, tm, D_MODEL) + corr).reshape(n_tok, D_MODEL)
    (grads['rw_mu'], grads['rw_w0'], grads['rw_w1'], grads['rw_w2'], grads['rw_a0'], grads['rw_a1'], grads['rw_a2'],
     grads['rw_g1'], grads['rw_g2']) = [t[None] if t.shape[0] != 1 else t for t in res[2:]]
    return loss_acc[0, 0], dx, grads


PACK_ROWS = 128


def _pack(arrs):
    flat = jnp.concatenate([a.reshape(-1).astype(F32) for a in arrs])
    per = PACK_ROWS * PACK_W
    total = -(-flat.shape[0] // per) * per
    return jnp.pad(flat, (0, total - flat.shape[0])).reshape(-1, PACK_W)


def _unpack(packed, shapes, lead=()):
    flat = packed.reshape(lead + (-1,))
    out, off = [], 0
    for s in shapes:
        n = math.prod(s)
        out.append(flat[..., off:off + n].reshape(lead + tuple(s)))
        off += n
    return out


def _mesh_pos():
    return lax.axis_index("x"), lax.axis_index("y"), lax.axis_index("c")


def _flip(v, f):
    return 1 - v if f else v


def _hbm_call(name, body, arrays, out_shapes, n_sems):
    n_t = len(arrays)
    hbm = pl.BlockSpec(memory_space=pl.ANY)
    return pl.pallas_call(
        body, name=name, out_shape=out_shapes, in_specs=[hbm] * n_t, out_specs=[hbm] * n_t,
        scratch_shapes=[pltpu.SemaphoreType.DMA((n_t, n_sems)), pltpu.SemaphoreType.DMA((n_t, n_sems)),
                        pltpu.SemaphoreType.DMA((n_t,))],
    )(*arrays)


def _all_gather(name, locals_):
    n_t = len(locals_)

    def body(*refs):
        x_refs, out_refs = refs[:n_t], refs[n_t:2 * n_t]
        send_sems, recv_sems, local_sems = refs[2 * n_t:]
        x, y, c = _mesh_pos()
        me, sibling = (x, y, c), (x, y, 1 - c)
        chips = [(1 - x, y), (x, 1 - y), (1 - x, 1 - y)]

        def copy(t, k, block, to, own=False):
            slot = out_refs[t].at[4 * block[0] + 2 * block[1] + block[2]]
            return pltpu.make_async_remote_copy(
                src_ref=x_refs[t] if own else slot, dst_ref=slot,
                send_sem=send_sems.at[t, k], recv_sem=recv_sems.at[t, k],
                device_id=to, device_id_type=pl.DeviceIdType.MESH)

        mine = [pltpu.make_async_copy(x_refs[t], out_refs[t].at[4 * x + 2 * y + c], local_sems.at[t]) for t in range(n_t)]
        for cp in mine:
            cp.start()
        sent = []
        for t in range(n_t):
            sent.append(copy(t, 0, me, sibling, own=True))
            sent += [copy(t, 1 + j, me, (*chip, c), own=True) for j, chip in enumerate(chips)]
        for cp in sent:
            cp.start()
        for j, chip in enumerate(chips):
            for t in range(n_t):
                copy(t, 1 + j, (*chip, c), me).wait_recv()
                passed = copy(t, 4 + j, (*chip, c), sibling)
                passed.start()
                sent.append(passed)
        for t in range(n_t):
            copy(t, 0, sibling, me).wait_recv()
            for j, chip in enumerate(chips):
                copy(t, 4 + j, (*chip, 1 - c), me).wait_recv()
        for cp in sent:
            cp.wait_send()
        for cp in mine:
            cp.wait()

    outs = [jax.ShapeDtypeStruct((N_DEV,) + a.shape, a.dtype) for a in locals_]
    return _hbm_call(name, body, locals_, outs, 7)


def _all_to_all(name, blocks):
    n_t = len(blocks)

    def body(*refs):
        x_refs, out_refs = refs[:n_t], refs[n_t:2 * n_t]
        send_sems, recv_sems, local_sems = refs[2 * n_t:]
        x, y, c = _mesh_pos()
        my_slot = 4 * x + 2 * y + c
        mine = [pltpu.make_async_copy(x_refs[t].at[my_slot], out_refs[t].at[my_slot], local_sems.at[t]) for t in range(n_t)]
        for cp in mine:
            cp.start()
        copies = []
        for k in range(1, N_DEV):
            peer = (_flip(x, k & 4), _flip(y, k & 2), _flip(c, k & 1))
            peer_slot = 4 * peer[0] + 2 * peer[1] + peer[2]
            for t in range(n_t):
                sems = dict(send_sem=send_sems.at[t, k - 1], recv_sem=recv_sems.at[t, k - 1],
                            device_id=peer, device_id_type=pl.DeviceIdType.MESH)
                cp = pltpu.make_async_remote_copy(src_ref=x_refs[t].at[peer_slot], dst_ref=out_refs[t].at[my_slot], **sems)
                cp.start()
                landed = pltpu.make_async_remote_copy(src_ref=x_refs[t].at[my_slot], dst_ref=out_refs[t].at[peer_slot], **sems)
                copies.append((cp, landed))
        for sent, landed in copies:
            landed.wait_recv()
            sent.wait_send()
        for cp in mine:
            cp.wait()

    outs = [jax.ShapeDtypeStruct(a.shape, a.dtype) for a in blocks]
    return _hbm_call(name, body, blocks, outs, 7)


_HBM = pl.BlockSpec(memory_space=pltpu.HBM)
_SEM = pl.BlockSpec(memory_space=pltpu.SEMAPHORE)
_EFFECT = pltpu.SideEffectType.DATAFLOW_SIDE_EFFECTING


def _peer_copies(x_refs, land_refs, send_sems, recv_sems, gather, mine):
    x, y, c = _mesh_pos()
    my_slot = 4 * x + 2 * y + c
    copies = []
    for k in range(1, N_DEV):
        peer = (_flip(x, k & 4), _flip(y, k & 2), _flip(c, k & 1))
        peer_slot = 4 * peer[0] + 2 * peer[1] + peer[2]
        for t, (x_ref, land) in enumerate(zip(x_refs, land_refs)):
            sem = t * (N_DEV - 1) + k - 1
            copies.append(pltpu.make_async_remote_copy(
                src_ref=x_ref if gather else x_ref.at[peer_slot], dst_ref=land.at[my_slot if mine else peer_slot],
                send_sem=send_sems.at[sem], recv_sem=recv_sems.at[sem], device_id=peer, device_id_type=pl.DeviceIdType.MESH))
    return copies


def _push_start(name, arrays, gather):
    n_t = len(arrays)
    lands = [lax.empty((N_DEV,) + a.shape if gather else a.shape, a.dtype) for a in arrays]

    def body(*refs):
        x_refs, land_refs = refs[:n_t], refs[n_t:2 * n_t]
        send_sems, recv_sems = refs[2 * n_t], refs[2 * n_t + 1]
        for sent in _peer_copies(x_refs, land_refs, send_sems, recv_sems, gather, True):
            sent.start()
        refs[-1][...] = jnp.zeros((8, 128), F32)

    held = list(arrays) + lands
    sems = pltpu.SemaphoreType.DMA((n_t * (N_DEV - 1),))
    out = pl.pallas_call(
        body, name=name,
        out_shape=(sems, sems, *[pltpu.HBM(a.shape, a.dtype) for a in held], jax.ShapeDtypeStruct((8, 128), F32)),
        in_specs=[_HBM] * (2 * n_t),
        out_specs=(_SEM, _SEM, *[_HBM] * (2 * n_t), pl.BlockSpec(memory_space=pltpu.VMEM)),
        input_output_aliases={i: 2 + i for i in range(2 * n_t)},
        compiler_params=pltpu.CompilerParams(has_side_effects=_EFFECT),
    )(*[pltpu.with_memory_space_constraint(a, pltpu.HBM) for a in held])
    return dict(send=out[0], recv=out[1], held=out[2:2 + 2 * n_t], zero=out[-1][0, 0], gather=gather)


def _push_wait(name, started, after):
    held, gather = started['held'], started['gather']
    n_t = len(held) // 2

    def body(*refs):
        x_refs, land_refs = refs[:n_t], refs[n_t:2 * n_t]
        send_sems, recv_sems = refs[2 * n_t], refs[2 * n_t + 1]
        for sent in _peer_copies(x_refs, land_refs, send_sems, recv_sems, gather, True):
            sent.wait_send()
        for landed in _peer_copies(x_refs, land_refs, send_sems, recv_sems, gather, False):
            landed.wait_recv()

    out = pl.pallas_call(
        body, name=name,
        out_shape=[pltpu.HBM(a.shape, a.dtype) for a in held],
        in_specs=[_HBM] * (2 * n_t) + [_SEM, _SEM, pl.BlockSpec(memory_space=pl.ANY)],
        out_specs=[_HBM] * (2 * n_t),
        input_output_aliases={i: i for i in range(2 * n_t)},
        compiler_params=pltpu.CompilerParams(has_side_effects=_EFFECT),
    )(*held, started['send'], started['recv'], after)
    return out[:n_t], out[n_t:]


def _adam_call(name, slots, w, m, v, own=None):
    rows, width = w.shape
    tile = min(rows, (PACK_ROWS * PACK_W) // width)
    c1 = 1.0 / (1.0 - ADAM_B1 ** ADAM_STEP)
    c2 = 1.0 / (1.0 - ADAM_B2 ** ADAM_STEP)

    def body(*refs):
        if own is None:
            s_ref, w_ref, m_ref, v_ref, g_ref, d_ref, nm_ref, nv_ref = refs
            part = lambda i: s_ref[i]
        else:
            s_ref, own_ref, w_ref, m_ref, v_ref, g_ref, d_ref, nm_ref, nv_ref = refs
            x, y, c = _mesh_pos()
            part = lambda i: jnp.where(4 * x + 2 * y + c == i, own_ref[...], s_ref[i])
        g = part(0)
        for i in range(1, N_DEV):
            g = g + part(i)
        m_new = ADAM_B1 * m_ref[...] + (1.0 - ADAM_B1) * g
        v_new = ADAM_B2 * v_ref[...] + (1.0 - ADAM_B2) * (g * g)
        g_ref[...] = g
        nm_ref[...] = m_new
        nv_ref[...] = v_new
        d_ref[...] = -ADAM_LR * ((m_new * c1) / (jnp.sqrt(v_new * c2) + ADAM_EPS) + ADAM_WD * w_ref[...])

    blk = _spec((tile, width), lambda i: (i, 0))
    ops = [slots] + ([] if own is None else [own]) + [w, m, v]
    return pl.pallas_call(
        body, name=name, grid=(rows // tile,),
        in_specs=[_spec((N_DEV, tile, width), lambda i: (0, i, 0))] + [blk] * (len(ops) - 1),
        out_specs=[blk] * 4, out_shape=[jax.ShapeDtypeStruct(w.shape, F32)] * 4,
        compiler_params=_cparams(1),
    )(*ops)


def kernel(x, ln_g, ln_b, rw_mu, rw_w0, rw_w1, rw_w2, rw_a0, rw_a1, rw_a2, rw_g1, rw_g2, rw_k_k, rw_k_a, rw_r_k, rw_wr, rw_wk, rw_wv, rw_wo, rw_lnx_g, rw_lnx_b, s5_a_re, s5_a_im, s5_log_dt, s5_b_re, s5_b_im, s5_c_re, s5_c_im, s5_d, s5_w_glu, mlp_w1, mlp_w2, loss_target, m_ln_g, m_ln_b, m_rw_mu, m_rw_w0, m_rw_w1, m_rw_w2, m_rw_a0, m_rw_a1, m_rw_a2, m_rw_g1, m_rw_g2, m_rw_k_k, m_rw_k_a, m_rw_r_k, m_rw_wr, m_rw_wk, m_rw_wv, m_rw_wo, m_rw_lnx_g, m_rw_lnx_b, m_s5_a_re, m_s5_a_im, m_s5_log_dt, m_s5_b_re, m_s5_b_im, m_s5_c_re, m_s5_c_im, m_s5_d, m_s5_w_glu, m_mlp_w1, m_mlp_w2, v_ln_g, v_ln_b, v_rw_mu, v_rw_w0, v_rw_w1, v_rw_w2, v_rw_a0, v_rw_a1, v_rw_a2, v_rw_g1, v_rw_g2, v_rw_k_k, v_rw_k_a, v_rw_r_k, v_rw_wr, v_rw_wk, v_rw_wv, v_rw_wo, v_rw_lnx_g, v_rw_lnx_b, v_s5_a_re, v_s5_a_im, v_s5_log_dt, v_s5_b_re, v_s5_b_im, v_s5_c_re, v_s5_c_im, v_s5_d, v_s5_w_glu, v_mlp_w1, v_mlp_w2):
    given = dict(locals())
    local_w = {n: given[n] for n in WEIGHTS}
    local_m = {n: given["m_" + n] for n in WEIGHTS}
    local_v = {n: given["v_" + n] for n in WEIGHTS}
    small = [n for n in WEIGHTS if n in SHARDED and n not in BIG]
    small_shapes = [local_w[n].shape for n in small]
    rep_shapes = [local_w[n].shape for n in REPLICATED]

    my_slot = 4 * lax.axis_index("x") + 2 * lax.axis_index("y") + lax.axis_index("c")
    as_bf16 = {n: local_w[n].astype(BF16) for n in BIG}

    late_started = _push_start("weights_late_start", [as_bf16[n] for n in LATE], gather=True)
    early = [n for n in BIG if n not in LATE]
    gathered = _all_gather("weights_all_gather", [as_bf16[n] for n in early] + [_pack([local_w[n] for n in small])])
    early_gathered = dict(zip(early, gathered))
    full = dict(local_w)
    for n, blk in zip(small, _unpack(gathered[-1], small_shapes, lead=(N_DEV,))):
        full[n] = jnp.concatenate([blk[i] for i in range(N_DEV)], axis=SHARDED[n])
    full['rw_w0'] = full['rw_w0'] + late_started['zero']

    def late_weights(after):
        _, lands = _push_wait("weights_late_wait", late_started, after)
        own = lambda n: as_bf16[n][None]
        return {n: lax.dynamic_update_slice(land, own(n), (my_slot,) + (0,) * as_bf16[n].ndim) for n, land in zip(LATE, lands)}

    exchanges = []

    def exchange(entries):
        started = _push_start(f"grads_start_{len(exchanges)}", [t for _, _, t in entries], gather=False)
        exchanges.append((entries, started))
        return started['zero']

    loss_sq, dx, grads = _local_step(x[0], loss_target[0], full, early_gathered, late_weights, exchange)
    loss = (0.5 / D_MODEL) * lax.psum(loss_sq, MESH_AXES)

    by_name = {}
    for e, (entries, started) in enumerate(exchanges):
        sources, lands = _push_wait(f"grads_wait_{e}", started, dx)
        for (n, i, _), src, slots in zip(entries, sources, lands):
            part = lambda d: d[n].reshape((-1,) + slots.shape[1:])[i]
            own = lax.dynamic_index_in_dim(src, my_slot, 0, keepdims=False)
            out = _adam_call(f"adamw_{n}_{i}", slots, part(local_w), part(local_m), part(local_v), own=own)
            by_name.setdefault(n, {})[i] = out
    for n in BIG:
        by_name[n] = [by_name[n][i] for i in sorted(by_name[n])]
    parts = [jnp.stack(jnp.split(grads[n], N_DEV, axis=SHARDED[n])) for n in small]
    small_send = jnp.stack([_pack([p[i] for p in parts]) for i in range(N_DEV)])
    recv = _all_to_all("grads_all_to_all", [small_send])
    for n in BIG:
        by_name[n] = [jnp.stack(ts).reshape(local_w[n].shape) for ts in zip(*by_name[n])]
    small_out = _adam_call("adamw_small", recv[-1], _pack([local_w[n] for n in small]),
                           _pack([local_m[n] for n in small]), _pack([local_v[n] for n in small]))
    (rep_all,) = _all_gather("grads_all_gather", [_pack([grads[n] for n in REPLICATED])])
    rep_out = _adam_call("adamw_replicated", rep_all, _pack([local_w[n] for n in REPLICATED]),
                         _pack([local_m[n] for n in REPLICATED]), _pack([local_v[n] for n in REPLICATED]))
    for k, (small_packed, rep_packed) in enumerate(zip(small_out, rep_out)):
        for n, t in zip(small, _unpack(small_packed, small_shapes)):
            by_name.setdefault(n, [None] * 4)[k] = t
        for n, t in zip(REPLICATED, _unpack(rep_packed, rep_shapes)):
            by_name.setdefault(n, [None] * 4)[k] = t
    results = [by_name[n][k] for k in range(4) for n in WEIGHTS]
    return (loss, dx[None], *results)
```

```python
import functools
import math

import jax
import jax.numpy as jnp
from jax import lax
from jax.experimental import pallas as pl
from jax.experimental.pallas import tpu as pltpu

F32 = jnp.float32
BF16 = jnp.bfloat16
HI = lax.Precision.HIGHEST

D_MODEL = 1024
HEAD = 64
PAIR = 2 * HEAD
N_PAIR = D_MODEL // PAIR
CHUNK = 64
GN_EPS = 64e-5
LN_EPS = 1e-5
SSM_GROUP = 16
SSM_STATE = 64
SSM_GROUPS = D_MODEL // SSM_GROUP
SSM_LANES = SSM_GROUPS * SSM_STATE
SSM_BLK_IN = 128
SSM_BLK_ST = 512
N_SSM_BLK = D_MODEL // SSM_BLK_IN
DEPTH = 2
DN_ALPHA = (2.0 * DEPTH) ** 0.25
ADAM_LR, ADAM_B1, ADAM_B2, ADAM_EPS, ADAM_WD, ADAM_STEP = 0.001, 0.9, 0.999, 1e-08, 0.01, 10
N_DEV = 8
MESH_AXES = ("x", "y", "c")
VMEM_LIMIT = 56 * 1024 * 1024
TOKEN_TILE = 256

SHARDED = {
    "rw_mu": 2, "rw_w1": 1, "rw_w2": 2, "rw_a1": 1, "rw_a2": 2, "rw_g1": 1, "rw_g2": 2,
    "rw_wr": 1, "rw_wk": 1, "rw_wv": 1, "rw_wo": 1, "s5_d": 1, "s5_w_glu": 2, "mlp_w1": 2, "mlp_w2": 1,
}
WEIGHTS = ['ln_g', 'ln_b', 'rw_mu', 'rw_w0', 'rw_w1', 'rw_w2', 'rw_a0', 'rw_a1', 'rw_a2', 'rw_g1', 'rw_g2', 'rw_k_k',
           'rw_k_a', 'rw_r_k', 'rw_wr', 'rw_wk', 'rw_wv', 'rw_wo', 'rw_lnx_g', 'rw_lnx_b', 's5_a_re', 's5_a_im',
           's5_log_dt', 's5_b_re', 's5_b_im', 's5_c_re', 's5_c_im', 's5_d', 's5_w_glu', 'mlp_w1', 'mlp_w2']
REPLICATED = [n for n in WEIGHTS if n not in SHARDED]
BIG = ['rw_wr', 'rw_wk', 'rw_wv', 'rw_wo', 's5_w_glu', 'mlp_w1', 'mlp_w2']
LATE = ['s5_w_glu', 'mlp_w1', 'mlp_w2']


def _cparams(n_grid):
    return pltpu.CompilerParams(dimension_semantics=("arbitrary",) * n_grid, vmem_limit_bytes=VMEM_LIMIT)


@jax.custom_vjp
def _mm(x, w):
    return jnp.dot(x.astype(BF16), w.astype(BF16), preferred_element_type=F32)


def _mm_fwd(x, w):
    return _mm(x, w), (x, w)


def _mm_bwd(res, dy):
    x, w = res
    dyb = dy.astype(BF16)
    dx = lax.dot_general(dyb, w.astype(BF16), (((1,), (1,)), ((), ())), preferred_element_type=F32)
    dw = lax.dot_general(x.astype(BF16), dyb, (((0,), (0,)), ((), ())), preferred_element_type=F32)
    return dx, dw


_mm.defvjp(_mm_fwd, _mm_bwd)


_DOT_DIMS = {'nn': (((1,), (0,)), ((), ())), 'nt': (((1,), (1,)), ((), ())), 'tn': (((0,), (0,)), ((), ()))}
_BATCH_DOT_DIMS = {'nn': (((2,), (1,)), ((0,), (0,))), 'nt': (((2,), (2,)), ((0,), (0,))), 'tn': (((1,), (1,)), ((0,), (0,)))}
CHUNK_PAIRS = 8
CHUNK_PASSES = 1


def _split_bf16(a):
    hi = a.astype(BF16)
    return hi, (a - hi.astype(F32)).astype(BF16)


def _dot_passes(a, b, mode, passes=None):
    dims = (_DOT_DIMS if a.ndim == 2 else _BATCH_DOT_DIMS)[mode]
    dot = lambda p, q: lax.dot_general(p, q, dims, preferred_element_type=F32)
    if (CHUNK_PASSES if passes is None else passes) == 1:
        return dot(a.astype(BF16), b.astype(BF16))
    (ah, al), (bh, bl) = _split_bf16(a), _split_bf16(b)
    return dot(ah, bh) + (dot(ah, bl) + dot(al, bh))


@functools.partial(jax.custom_vjp, nondiff_argnums=(2,))
def _pdot(a, b, mode):
    return _dot_passes(a, b, mode)


def _pdot_fwd(a, b, mode):
    return _dot_passes(a, b, mode), (a, b)


def _pdot_bwd(mode, res, dy):
    a, b = res
    if mode == 'nn':
        return _dot_passes(dy, b, 'nt'), _dot_passes(a, dy, 'tn')
    if mode == 'nt':
        return _dot_passes(dy, b, 'nn'), _dot_passes(dy, a, 'tn')
    return _dot_passes(b, dy, 'nt'), _dot_passes(a, dy, 'nn')


_pdot.defvjp(_pdot_fwd, _pdot_bwd)


def _tri_sum(x, upper):
    nb, n = x.shape[0], x.shape[1]
    ti = lax.broadcasted_iota(jnp.int32, (nb, n, n), 1)
    tj = lax.broadcasted_iota(jnp.int32, (nb, n, n), 2)
    tri = ((ti <= tj) if upper else (ti >= tj)).astype(BF16)
    hi = x.astype(BF16)
    rest = x - hi.astype(F32)
    mid = rest.astype(BF16)
    lo = (rest - mid.astype(F32)).astype(BF16)
    dot = lambda q: lax.dot_general(tri, q, _BATCH_DOT_DIMS['nn'], preferred_element_type=F32)
    return dot(hi) + (dot(mid) + dot(lo))


@jax.custom_vjp
def _cumsum_rows(x):
    return _tri_sum(x, False)


_cumsum_rows.defvjp(lambda x: (_tri_sum(x, False), None), lambda _, dy: (_tri_sum(dy, True),))


@functools.partial(jax.custom_vjp, nondiff_argnums=(1,))
def _roll_rows(x, shift):
    return pltpu.roll(x, shift, 0)


def _roll_rows_fwd(x, shift):
    return pltpu.roll(x, shift, 0), None


def _roll_rows_bwd(shift, _, dy):
    n = dy.shape[0]
    return (pltpu.roll(dy, (n - shift) % n, 0),)


_roll_rows.defvjp(_roll_rows_fwd, _roll_rows_bwd)


def _shift_down(x, first_row):
    row = lax.broadcasted_iota(jnp.int32, x.shape, 0)
    return jnp.where(row == 0, first_row, _roll_rows(x, 1))


def _sigmoid(x):
    return 1.0 / (1.0 + jnp.exp(-x))


def _softplus(x):
    return jnp.maximum(x, 0.0) + jnp.log(1.0 + jnp.exp(-jnp.abs(x)))


def _gelu(x):
    return 0.5 * x * (1.0 + jnp.tanh(math.sqrt(2.0 / math.pi) * (x + 0.044715 * (x * x * x))))


def _layer_norm(x, g, b):
    mu = jnp.mean(x, axis=-1, keepdims=True)
    xc = x - mu
    var = jnp.mean(xc * xc, axis=-1, keepdims=True)
    return xc * lax.rsqrt(var + LN_EPS) * g + b


def _spec(block, index_map):
    return pl.BlockSpec(block, index_map)


def _tile(arr, tm):
    return (arr, (tm, arr.shape[1]), lambda i: (i, 0))


def _full(arr):
    nd = arr.ndim
    return (arr, arr.shape, lambda *_: (0,) * nd)


def _fwd_call(name, fn, grid, prims, outs):
    n_in = len(prims)

    def body(*refs):
        vals = [r[...] for r in refs[:n_in]]
        vals = [v.astype(F32) if v.dtype != F32 else v for v in vals]
        res = fn(*vals)
        for o, r in zip(refs[n_in:], res):
            o[...] = r.astype(o.dtype)

    return pl.pallas_call(
        body, name=name, grid=grid,
        in_specs=[_spec(b, m) for _, b, m in prims],
        out_specs=[_spec(b, m) for _, _, b, m in outs],
        out_shape=[jax.ShapeDtypeStruct(s, d) for s, d, _, _ in outs],
        compiler_params=_cparams(len(grid)),
    )(*[a for a, _, _ in prims])


def _bwd_call(name, fn, grid, prims, cts, wants, addends=None):
    addends = addends or {}
    n_in, n_ct = len(prims), len(cts)
    out_place = {}
    wants, out_dtype = list(wants), {}
    for i, w in enumerate(wants):
        if isinstance(w, tuple):
            wants[i], out_place[i] = w
        elif w == 'tile_bf16':
            wants[i], out_dtype[i] = 'tile', BF16
    diff = [i for i, w in enumerate(wants) if w]
    add_idx = [i for i in diff if i in addends]
    last_axis = len(grid) - 1

    def body(*refs):
        vals = [r[...] for r in refs[:n_in]]
        vals = [v.astype(F32) if v.dtype != F32 else v for v in vals]
        ct_vals = [r[...] for r in refs[n_in:n_in + n_ct]]
        ct_vals = [v.astype(F32) if v.dtype != F32 else v for v in ct_vals]
        add_refs = dict(zip(add_idx, refs[n_in + n_ct:n_in + n_ct + len(add_idx)]))
        out_refs = refs[n_in + n_ct + len(add_idx):]

        def f(*dargs):
            full = list(vals)
            for i, a in zip(diff, dargs):
                full[i] = a
            return tuple(fn(*full))

        _, vjp = jax.vjp(f, *[vals[i] for i in diff])
        grads = vjp(tuple(ct_vals))
        first = pl.program_id(last_axis) == 0
        for o, g, i in zip(out_refs, grads, diff):
            if wants[i] == 'tile':
                if i in add_refs:
                    g = g + add_refs[i][...]
                o[...] = g.astype(o.dtype)
            else:
                @pl.when(first)
                def _(o=o):
                    o[...] = jnp.zeros(o.shape, o.dtype)
                o[...] += g

    ins = list(prims) + list(cts) + [addends[i] for i in add_idx]
    places = [out_place.get(i, (prims[i][0].shape, prims[i][1], prims[i][2])) for i in diff]
    return pl.pallas_call(
        body, name=name, grid=grid,
        in_specs=[_spec(b, m) for _, b, m in ins],
        out_specs=[_spec(b, m) for _, b, m in places],
        out_shape=[jax.ShapeDtypeStruct(s, out_dtype.get(i, F32)) for i, (s, _, _) in zip(diff, places)],
        compiler_params=_cparams(len(grid)),
    )(*[a for a, _, _ in ins])


def _matmul(name, a, b, mode, *, out_dtype=F32, lhs_fn=None, epi=None, extra=None, tm=1024, tn=1024, tk=1024,
            b_cols=None, b_rows=None, out_cols=False):
    if mode == 'tn':
        kdim, m = a.shape
    else:
        m, kdim = a.shape
    if b_rows is not None:
        loc, cols = b.shape[-2], b.shape[-1]
        if mode == 'nn':
            n, k_shards = cols, max(1, tk // loc)
            tk = k_shards * loc
        else:
            n, tn = N_DEV * loc, loc
    elif b_cols is None:
        n = b.shape[0] if mode == 'nt' else b.shape[1]
    else:
        rows, loc = b.shape[-2], b.shape[-1]
        if mode == 'nn':
            n, tn = N_DEV * loc, loc
        else:
            n, k_shards = rows, max(1, tk // loc)
            tk = k_shards * loc
    if out_cols:
        tn = n // N_DEV
    tm, tn, tk = min(tm, m), min(tn, n), min(tk, kdim)
    nk = kdim // tk
    grid = (m // tm, n // tn, nk)
    a_spec = _spec((tk, tm), lambda i, j, k: (k, i)) if mode == 'tn' else _spec((tm, tk), lambda i, j, k: (i, k))
    if b_rows is not None:
        lead = tuple(b_rows)
        skip = (None,) * (1 + len(lead))
        if mode == 'nn':
            b_spec = _spec((k_shards,) + skip[1:] + (tk // k_shards, tn), lambda i, j, k: (k,) + lead + (0, j))
        else:
            b_spec = _spec(skip + (tn, tk), lambda i, j, k: (j,) + lead + (0, k))
    elif b_cols is None:
        b_spec = _spec((tn, tk), lambda i, j, k: (j, k)) if mode == 'nt' else _spec((tk, tn), lambda i, j, k: (k, j))
    else:
        lead = tuple(b_cols)
        skip = (None,) * (1 + len(lead))
        if mode == 'nn':
            b_spec = _spec(skip + (tk, tn), lambda i, j, k: (j,) + lead + (k, 0))
        else:
            b_spec = _spec((k_shards,) + skip[1:] + (tn, tk // k_shards), lambda i, j, k: (k,) + lead + (j, 0))
    o_spec = _spec((tm, tn), lambda i, j, k: (i, j))
    if out_cols:
        o_place, o_shape = _spec((None, tm, tn), lambda i, j, k: (j, i, 0)), (N_DEV, m, tn)
    else:
        o_place, o_shape = o_spec, (m, n)
    dims = _DOT_DIMS[mode]

    def body(*refs):
        if extra is None:
            a_ref, b_ref, o_ref, acc = refs
            x_ref = None
        else:
            a_ref, b_ref, x_ref, o_ref, acc = refs
        k = pl.program_id(2)
        av = a_ref[...]
        if lhs_fn is not None:
            av = lhs_fn(av.astype(F32))
        bv = b_ref[...]
        if bv.ndim == 3:
            bv = bv.reshape(tk, tn) if mode == 'nn' else jnp.concatenate([bv[g] for g in range(bv.shape[0])], axis=1)
        part = lax.dot_general(av.astype(BF16), bv.astype(BF16), dims, preferred_element_type=F32)

        def finish(r):
            if epi is not None:
                r = epi(r, x_ref[...])
            o_ref[...] = r.astype(o_ref.dtype)

        if nk == 1:
            finish(part)
            return

        @pl.when(k == 0)
        def _():
            acc[...] = part

        @pl.when((k > 0) & (k < nk - 1))
        def _():
            acc[...] += part

        @pl.when(k == nk - 1)
        def _():
            finish(acc[...] + part)

    ops, specs = [a, b], [a_spec, b_spec]
    if extra is not None:
        ops.append(extra)
        specs.append(o_spec)
    return pl.pallas_call(
        body, name=name, grid=grid, in_specs=specs, out_specs=o_place,
        out_shape=jax.ShapeDtypeStruct(o_shape, out_dtype),
        scratch_shapes=[pltpu.VMEM((tm, tn), F32)],
        compiler_params=_cparams(3),
    )(*ops)


def _relu2(x):
    r = jnp.maximum(x, 0.0)
    return r * r


def _mix_fn(x, halo, mu, w0, w1, w2, a0, a1, a2, g1, g2):
    prev_row = jnp.where(pl.program_id(0) == 0, 0.0, halo[7:8, :])
    xx = _shift_down(x, prev_row) - x
    xr, xw, xk, xv, xa, xg = (x + xx * mu[i:i + 1, :] for i in range(6))
    w_pre = w0 + _mm(jnp.tanh(_mm(xw, w1)), w2)
    log_decay = -jnp.exp(-_softplus(-w_pre) - 0.5)
    a = _sigmoid(a0 + _mm(_mm(xa, a1), a2))
    g = _mm(_sigmoid(_mm(xg, g1)), g2)
    return xr, xk, xv, log_decay, a, g


def _chunk_fn(r, lw, k, v, a, k_k, k_a, r_k):
    n, n_pair = r.shape[0], r.shape[1] // PAIR
    to_batch = lambda t: jnp.concatenate([t[None, :, p * PAIR:(p + 1) * PAIR] for p in range(n_pair)], axis=0)
    from_batch = lambda t: jnp.concatenate([t[p] for p in range(n_pair)], axis=1)
    r, lw, k, v, a, k_k, k_a, r_k = (to_batch(t) for t in (r, lw, k, v, a, k_k, k_a, r_k))
    lane = lax.broadcasted_iota(jnp.int32, (1, 1, PAIR), 2)
    m0 = lane < HEAD

    def head_sum(t):
        s0 = jnp.sum(jnp.where(m0, t, 0.0), axis=2, keepdims=True)
        s1 = jnp.sum(jnp.where(m0, 0.0, t), axis=2, keepdims=True)
        return jnp.where(m0, s0, s1)

    kk = k * k_k
    kk = kk / jnp.maximum(jnp.sqrt(head_sum(kk * kk)), 1e-12)
    k2 = k * (1.0 + (a - 1.0) * k_a)
    aa, bb = -kk, kk * a
    bonus = head_sum(r * k2 * r_k) * v

    cum = _cumsum_rows(lw)
    p_in, p_ex, p_inv = jnp.exp(cum), jnp.exp(cum - lw), jnp.exp(-cum)
    at, rt, kt, bt = aa * p_ex, r * p_in, k2 * p_inv, bb * p_inv

    def stack_masked(t):
        return jnp.concatenate([jnp.where(m0, t, 0.0), jnp.where(m0, 0.0, t)], axis=1)

    def unstack_sum(t):
        return t[:, :n] + t[:, n:]

    def unstack_select(t):
        return jnp.where(m0, t[:, :n], t[:, n:])

    ti = lax.broadcasted_iota(jnp.int32, (1, 2 * n, 2 * n), 1)
    tj = lax.broadcasted_iota(jnp.int32, (1, 2 * n, 2 * n), 2)
    same = (ti >= n) == (tj >= n)
    incl, strict = same & (ti >= tj), same & (ti > tj)
    eye_n = (ti == tj).astype(F32)
    m = 2 * n
    rows = lambda *ts: jnp.concatenate(ts, axis=1)
    cols = lambda *ts: jnp.concatenate(ts, axis=2)
    at_s, rt_s, kt_s, bt_s = stack_masked(at), stack_masked(rt), stack_masked(kt), stack_masked(bt)
    v_s, zeros = rows(v, v), jnp.zeros_like(at_s)
    bk_s = rows(bt_s, kt_s)
    gram = _pdot(rows(at_s, rt_s), bk_s, 'nt')
    a_ab = jnp.where(strict, gram[:, :m, :m], 0.0)
    a_ak = jnp.where(strict, gram[:, :m, m:], 0.0)
    a_rb = jnp.where(incl, gram[:, m:, :m], 0.0)
    a_rk = jnp.where(incl, gram[:, m:, m:], 0.0)
    pw, inv = _pdot(a_ab, a_ab, 'nn'), eye_n + a_ab
    for _ in range(int(math.log2(n)) - 2):
        both = _pdot(pw, cols(pw, inv), 'nn')
        pw, inv = both[:, :, :m], inv + both[:, :, m:]
    inv = inv + _pdot(pw, inv, 'nn')
    wy_s = _pdot(inv, cols(at_s, _pdot(a_ak, v_s, 'nn')), 'nn')
    q_oi = _pdot(cols(a_rb, a_rk), rows(wy_s, cols(zeros, v_s)), 'nn')
    q_all = unstack_sum(rt_s + q_oi[:, :, :PAIR])
    oi_all = unstack_select(q_oi[:, :, PAIR:])
    own_head = (lax.broadcasted_iota(jnp.int32, (1, m, PAIR), 1) >= n) == (lax.broadcasted_iota(jnp.int32, (1, m, PAIR), 2) >= HEAD)
    wy_m = cols(wy_s[:, :, :PAIR], jnp.where(own_head, wy_s[:, :, PAIR:], 0.0))
    mm_nn = _pdot(bk_s, rows(wy_m, cols(zeros, stack_masked(v))), 'tn')
    mm, nn = mm_nn[:, :, :PAIR], mm_nn[:, :, PAIR:]
    ei = lax.broadcasted_iota(jnp.int32, (1, PAIR, PAIR), 1)
    ej = lax.broadcasted_iota(jnp.int32, (1, PAIR, PAIR), 2)
    eye_p = (ei == ej).astype(F32)
    decay_col = jnp.sum(eye_p * p_in[:, n - 1:n, :], axis=2, keepdims=True)
    return ((decay_col * (eye_p + mm))[None], (decay_col * nn)[None], from_batch(q_all), from_batch(oi_all),
            from_batch(bonus))


def _gate_fn(o, bonus, g, lnx_g, lnx_b):
    lane = lax.broadcasted_iota(jnp.int32, (1, PAIR), 1)
    m0 = lane < HEAD

    def head_mean(t):
        s0 = jnp.sum(jnp.where(m0, t, 0.0), axis=1, keepdims=True)
        s1 = jnp.sum(jnp.where(m0, 0.0, t), axis=1, keepdims=True)
        return jnp.where(m0, s0, s1) * (1.0 / HEAD)

    outs = []
    for p in range(N_PAIR):
        sl = slice(p * PAIR, (p + 1) * PAIR)
        oc = o[:, sl] - head_mean(o[:, sl])
        on = oc * lax.rsqrt(head_mean(oc * oc) + GN_EPS) * lnx_g[:, sl] + lnx_b[:, sl]
        outs.append((on + bonus[:, sl]) * g[:, sl])
    return (jnp.concatenate(outs, axis=1),)


def _res_ln_fn(h, y, g, b):
    return (_layer_norm(DN_ALPHA * h + y, g, b),)


def _glu_ln_fn(h, z, g, b):
    mix = z[:, :D_MODEL] * _sigmoid(z[:, D_MODEL:])
    return (_layer_norm(DN_ALPHA * h + mix, g, b),)


def _s5_param_fn(a_re, a_im, log_dt, b_re, b_im):
    dt = jnp.exp(log_dt)
    lam_re = jnp.minimum(a_re, -1e-4)
    lam_im = a_im
    mag = jnp.exp(dt * lam_re)
    abar_re = mag * jnp.cos(dt * lam_im)
    abar_im = mag * jnp.sin(dt * lam_im)
    den = lam_re * lam_re + lam_im * lam_im
    nr, ni = abar_re - 1.0, abar_im
    coef_re = (nr * lam_re + ni * lam_im) / den
    coef_im = (ni * lam_re - nr * lam_im) / den
    return abar_re, abar_im, coef_re, coef_im


def _s5_bbar_fn(coef_re, coef_im, b_re, b_im):
    return coef_re * b_re - coef_im * b_im, coef_re * b_im + coef_im * b_re


def _s5_in_fn(u, b_re, b_im):
    return _pdot(u, b_re[0], 'nn'), _pdot(u, b_im[0], 'nn')


def _s5_out_fn(s_re, s_im, u, c_re, c_im, d):
    y = _pdot(s_re, c_re[0], 'nn') - _pdot(s_im, c_im[0], 'nn') + u * d
    return (_gelu(y),)


def _pairs_to_batch(t):
    return jnp.concatenate([t[None, :, p * PAIR:(p + 1) * PAIR] for p in range(N_PAIR)], axis=0)


def _batch_to_pairs(t):
    return jnp.concatenate([t[p] for p in range(N_PAIR)], axis=1)


STATE_PASSES = 3


def _rwkv_scan_fwd(r, lw, k, v, a, k_k, k_a, r_k):
    n_tok = r.shape[0]
    n_chunk = n_tok // CHUNK

    def body(r_ref, lw_ref, k_ref, v_ref, a_ref, kk_ref, ka_ref, rk_ref, o_ref, bonus_ref, zs_ref, z):
        @pl.when(pl.program_id(0) == 0)
        def _():
            z[...] = jnp.zeros(z.shape, F32)

        m_all, n_all, q, oi, bonus = _chunk_fn(r_ref[...], lw_ref[...], k_ref[...], v_ref[...], a_ref[...],
                                               kk_ref[...], ka_ref[...], rk_ref[...])
        zb = z[...]
        zs_ref[0] = zb
        bonus_ref[...] = bonus
        o_ref[...] = _batch_to_pairs(_dot_passes(_pairs_to_batch(q), zb, 'nn', STATE_PASSES)) + oi
        z[...] = _dot_passes(m_all[0], zb, 'nn', STATE_PASSES) + n_all[0]

    tok = _spec((CHUNK, D_MODEL), lambda c: (c, 0))
    par = _spec((1, D_MODEL), lambda c: (0, 0))
    st = _spec((1, N_PAIR, PAIR, PAIR), lambda c: (c, 0, 0, 0))
    return pl.pallas_call(
        body, name="rwkv_scan_fwd", grid=(n_chunk,),
        in_specs=[tok] * 5 + [par] * 3, out_specs=[tok, tok, st],
        out_shape=[jax.ShapeDtypeStruct((n_tok, D_MODEL), F32)] * 2 + [jax.ShapeDtypeStruct((n_chunk, N_PAIR, PAIR, PAIR), F32)],
        scratch_shapes=[pltpu.VMEM((N_PAIR, PAIR, PAIR), F32)],
        compiler_params=_cparams(1),
    )(r, lw, k, v, a, k_k, k_a, r_k)


def _rwkv_scan_bwd(r, lw, k, v, a, k_k, k_a, r_k, zs, d_o, d_bonus):
    n_tok = r.shape[0]
    n_chunk = n_tok // CHUNK

    def body(r_ref, lw_ref, k_ref, v_ref, a_ref, kk_ref, ka_ref, rk_ref, zs_ref, do_ref, db_ref,
             dr_ref, dlw_ref, dk_ref, dv_ref, da_ref, dkk_ref, dka_ref, drk_ref, g):
        sums = (dkk_ref, dka_ref, drk_ref)

        @pl.when(pl.program_id(0) == 0)
        def _():
            g[...] = jnp.zeros(g.shape, F32)
            for s_ref in sums:
                s_ref[...] = jnp.zeros(s_ref.shape, F32)

        prim = (r_ref[...], lw_ref[...], k_ref[...], v_ref[...], a_ref[...], kk_ref[...], ka_ref[...], rk_ref[...])
        (m_all, _, q, _, _), chunk_vjp = jax.vjp(_chunk_fn, *prim)
        gb, zb, d_out = g[...], zs_ref[0], do_ref[...]
        dob = _pairs_to_batch(d_out)
        d_m = _dot_passes(gb, zb, 'nt', STATE_PASSES)
        d_q = _batch_to_pairs(_dot_passes(dob, zb, 'nt', STATE_PASSES))
        grads = chunk_vjp((d_m[None], gb[None], d_q, d_out, db_ref[...]))
        for out_ref, val in zip((dr_ref, dlw_ref, dk_ref, dv_ref, da_ref), grads[:5]):
            out_ref[...] = val.astype(out_ref.dtype)
        for s_ref, val in zip(sums, grads[5:]):
            s_ref[...] += val
        same_head = (lax.broadcasted_iota(jnp.int32, (1, PAIR, PAIR), 1) // HEAD
                     == lax.broadcasted_iota(jnp.int32, (1, PAIR, PAIR), 2) // HEAD)
        g[...] = (_dot_passes(m_all[0], gb, 'tn', STATE_PASSES)
                  + jnp.where(same_head, _dot_passes(_pairs_to_batch(q), dob, 'tn', STATE_PASSES), 0.0))

    tok = _spec((CHUNK, D_MODEL), lambda c: (n_chunk - 1 - c, 0))
    par = _spec((1, D_MODEL), lambda c: (0, 0))
    st = _spec((1, N_PAIR, PAIR, PAIR), lambda c: (n_chunk - 1 - c, 0, 0, 0))
    tok_shape = lambda dt: jax.ShapeDtypeStruct((n_tok, D_MODEL), dt)
    return pl.pallas_call(
        body, name="rwkv_scan_bwd", grid=(n_chunk,),
        in_specs=[tok] * 5 + [par] * 3 + [st, tok, tok], out_specs=[tok] * 5 + [par] * 3,
        out_shape=[tok_shape(BF16), tok_shape(F32), tok_shape(BF16), tok_shape(BF16), tok_shape(F32)]
        + [jax.ShapeDtypeStruct((1, D_MODEL), F32)] * 3,
        scratch_shapes=[pltpu.VMEM((N_PAIR, PAIR, PAIR), F32)],
        compiler_params=_cparams(1),
    )(r, lw, k, v, a, k_k, k_a, r_k, zs, d_o, d_bonus)


S5_TIME_TILE = 512
S5_GROUP = 8


def _scan_rows(re_ref, im_ref, a1, carry, reverse):
    lb = re_ref.shape[1]
    grp, n_grp = S5_GROUP, re_ref.shape[0] // S5_GROUP

    def cmul(xr, xi, yr, yi):
        return xr * yr - xi * yi, xr * yi + xi * yr

    a2 = cmul(*a1, *a1)
    a4 = cmul(*a2, *a2)
    a8 = cmul(*a4, *a4)
    row = lax.broadcasted_iota(jnp.int32, (grp, lb), 0)
    expo = (grp - row) if reverse else (row + 1)
    pw = (jnp.ones((grp, lb), F32), jnp.zeros((grp, lb), F32))
    for bit, ap in ((1, a1), (2, a2), (4, a4), (8, a8)):
        nxt = cmul(*pw, *ap)
        sel = (expo & bit) != 0
        pw = (jnp.where(sel, nxt[0], pw[0]), jnp.where(sel, nxt[1], pw[1]))

    steps = []
    for d, ad in ((1, a1), (2, a2), (4, a4)):
        keep = (row < grp - d) if reverse else (row >= d)
        steps.append(((grp - d) if reverse else d, jnp.where(keep, ad[0], 0.0), jnp.where(keep, ad[1], 0.0)))

    def group(i, c):
        gi = (n_grp - 1 - i) if reverse else i
        rows = pl.ds(pl.multiple_of(gi * grp, grp), grp)
        xr, xi = re_ref[rows, :], im_ref[rows, :]
        for shift, mr, mi in steps:
            pr, pi = cmul(mr, mi, pltpu.roll(xr, shift, 0), pltpu.roll(xi, shift, 0))
            xr, xi = xr + pr, xi + pi
        cr, ci = cmul(*pw, c[0], c[1])
        xr, xi = xr + cr, xi + ci
        re_ref[rows, :] = xr
        im_ref[rows, :] = xi
        edge = slice(0, 1) if reverse else slice(grp - 1, grp)
        return xr[edge, :], xi[edge, :]

    return lax.fori_loop(0, n_grp, group, carry)


def _s5_fwd(u, b_re, b_im, a_re, a_im, c_re, c_im, d_row):
    n_tok = u.shape[0]
    tt = min(S5_TIME_TILE, n_tok)

    def body(u_ref, bre_ref, bim_ref, are_ref, aim_ref, cre_ref, cim_ref, d_ref, sre_ref, sim_ref, yg_ref, carry):
        @pl.when(pl.program_id(1) == 0)
        def _():
            carry[...] = jnp.zeros(carry.shape, F32)

        uv = u_ref[...]
        sre_ref[...], sim_ref[...] = _s5_in_fn(uv, bre_ref[...], bim_ref[...])
        c = _scan_rows(sre_ref, sim_ref, (are_ref[...], aim_ref[...]), (carry[0:1, :], carry[1:2, :]), False)
        carry[0:1, :] = c[0]
        carry[1:2, :] = c[1]
        (yg,) = _s5_out_fn(sre_ref[...], sim_ref[...], uv, cre_ref[...], cim_ref[...], d_ref[...])
        yg_ref[...] = yg.astype(yg_ref.dtype)

    u_blk = _spec((tt, SSM_BLK_IN), lambda l, t: (t, l))
    s_blk = _spec((tt, SSM_BLK_ST), lambda l, t: (t, l))
    blk3 = lambda arr: _spec((1,) + arr.shape[1:], lambda l, t: (l, 0, 0))
    par = lambda width: _spec((1, width), lambda l, t: (0, l))
    return pl.pallas_call(
        body, name="s5_fwd", grid=(N_SSM_BLK, n_tok // tt),
        in_specs=[u_blk, blk3(b_re), blk3(b_im), par(SSM_BLK_ST), par(SSM_BLK_ST), blk3(c_re), blk3(c_im), par(SSM_BLK_IN)],
        out_specs=[s_blk, s_blk, u_blk],
        out_shape=[jax.ShapeDtypeStruct((n_tok, SSM_LANES), F32)] * 2 + [jax.ShapeDtypeStruct((n_tok, D_MODEL), BF16)],
        scratch_shapes=[pltpu.VMEM((2, SSM_BLK_ST), F32)],
        compiler_params=_cparams(2),
    )(u, b_re, b_im, a_re, a_im, c_re, c_im, d_row)


def _s5_bwd(dyg, u, s_re, s_im, dh_res, b_re, b_im, a_re, a_im, c_re, c_im, d_row):
    n_tok = u.shape[0]
    tt = min(S5_TIME_TILE, n_tok)
    n_t = n_tok // tt

    def body(dyg_ref, u_ref, sre_ref, sim_ref, hre_ref, him_ref, res_ref, bre_ref, bim_ref, are_ref, aim_ref,
             cre_ref, cim_ref, d_ref, du_ref, dbre_ref, dbim_ref, dcre_ref, dcim_ref, dd_ref, dare_ref, daim_ref,
             carry, gre, gim):
        i = pl.program_id(1)
        sums = (dbre_ref, dbim_ref, dcre_ref, dcim_ref, dd_ref, dare_ref, daim_ref)

        @pl.when(i == 0)
        def _():
            carry[...] = jnp.zeros(carry.shape, F32)
            for r in sums:
                r[...] = jnp.zeros(r.shape, F32)

        uv, sre, sim = u_ref[...], sre_ref[...], sim_ref[...]
        _, out_vjp = jax.vjp(_s5_out_fn, sre, sim, uv, cre_ref[...], cim_ref[...], d_ref[...])
        gre[...], gim[...], du_out, dcre, dcim, dd = out_vjp((dyg_ref[...],))
        c = _scan_rows(gre, gim, (are_ref[...], -aim_ref[...]), (carry[0:1, :], carry[1:2, :]), True)
        carry[0:1, :] = c[0]
        carry[1:2, :] = c[1]
        g_re, g_im = gre[...], gim[...]
        at_start = i == n_t - 1
        p_re = _shift_down(sre, jnp.where(at_start, 0.0, hre_ref[7:8, :]))
        p_im = _shift_down(sim, jnp.where(at_start, 0.0, him_ref[7:8, :]))
        _, in_vjp = jax.vjp(_s5_in_fn, uv, bre_ref[...], bim_ref[...])
        du_in, dbre, dbim = in_vjp((g_re, g_im))
        du_ref[...] = du_out + du_in + res_ref[...]
        dare = jnp.sum(g_re * p_re + g_im * p_im, axis=0, keepdims=True)
        daim = jnp.sum(g_im * p_re - g_re * p_im, axis=0, keepdims=True)
        for r, val in zip(sums, (dbre, dbim, dcre, dcim, dd, dare, daim)):
            r[...] += val

    u_blk = _spec((tt, SSM_BLK_IN), lambda l, t: (n_t - 1 - t, l))
    s_blk = _spec((tt, SSM_BLK_ST), lambda l, t: (n_t - 1 - t, l))
    halo = _spec((8, SSM_BLK_ST), lambda l, t: (jnp.maximum((n_t - 1 - t) * (tt // 8) - 1, 0), l))
    blk3 = lambda arr: _spec((1,) + arr.shape[1:], lambda l, t: (l, 0, 0))
    par = lambda width: _spec((1, width), lambda l, t: (0, l))
    params = [blk3(b_re), blk3(b_im), par(SSM_BLK_ST), par(SSM_BLK_ST), blk3(c_re), blk3(c_im), par(SSM_BLK_IN)]
    shape = lambda arr: jax.ShapeDtypeStruct(arr.shape, F32)
    return pl.pallas_call(
        body, name="s5_bwd", grid=(N_SSM_BLK, n_t),
        in_specs=[u_blk, u_blk, s_blk, s_blk, halo, halo, u_blk] + params,
        out_specs=[u_blk, blk3(b_re), blk3(b_im), blk3(c_re), blk3(c_im), par(SSM_BLK_IN), par(SSM_BLK_ST), par(SSM_BLK_ST)],
        out_shape=[shape(u), shape(b_re), shape(b_im), shape(c_re), shape(c_im), shape(d_row), shape(a_re), shape(a_im)],
        scratch_shapes=[pltpu.VMEM((2, SSM_BLK_ST), F32), pltpu.VMEM((tt, SSM_BLK_ST), F32), pltpu.VMEM((tt, SSM_BLK_ST), F32)],
        compiler_params=_cparams(2),
    )(dyg, u, s_re, s_im, s_re, s_im, dh_res, b_re, b_im, a_re, a_im, c_re, c_im, d_row)


def _loss_call(h, target, tm):
    n_tok, d = h.shape

    def body(h_ref, t_ref, acc_ref, dh_ref):
        @pl.when(pl.program_id(0) == 0)
        def _():
            acc_ref[...] = jnp.zeros(acc_ref.shape, F32)

        e = h_ref[...] - t_ref[...]
        dh_ref[...] = e * (1.0 / d)
        acc_ref[...] += jnp.sum(jnp.sum(e * e, axis=1, keepdims=True), axis=0, keepdims=True)

    tok = _spec((tm, d), lambda i: (i, 0))
    return pl.pallas_call(
        body, name="loss", grid=(n_tok // tm,),
        in_specs=[tok, tok], out_specs=[_spec((8, 128), lambda i: (0, 0)), tok],
        out_shape=[jax.ShapeDtypeStruct((8, 128), F32), jax.ShapeDtypeStruct(h.shape, F32)],
        compiler_params=_cparams(1),
    )(h, target)


def _block_diag(t):
    nb, ng, rows, cols = t.shape
    eye = jnp.eye(ng, dtype=t.dtype)
    return jnp.einsum('bgrc,gh->bgrhc', t, eye).reshape(nb, ng * rows, ng * cols)


def _block_diag_t(t, rows, cols):
    nb, ng = t.shape[0], t.shape[1] // rows
    t5 = t.reshape(nb, ng, rows, ng, cols)
    return jnp.einsum('bgrhc,gh->bgrc', t5, jnp.eye(ng, dtype=t.dtype))


def _mlp_fwd(layer, h, w1g, w2g):
    pre = _matmul(f"mlp{layer}_up", h, w1g, 'nn', b_cols=(layer,), out_dtype=BF16)
    return pre, _matmul(f"mlp{layer}_down", pre, w2g, 'nn', lhs_fn=_relu2, b_rows=(layer,))


def _mlp_bwd(layer, h, pre, dy, dh_res, w1g, w2g):
    dpre = _matmul(f"mlp{layer}_down_dx", dy, w2g, 'nt', epi=lambda acc, p: acc * (2.0 * jnp.maximum(p, 0.0)),
                   extra=pre, b_rows=(layer,), out_dtype=BF16)
    dw2 = _matmul(f"mlp{layer}_down_dw", pre, dy, 'tn', lhs_fn=_relu2)
    dw1 = _matmul(f"mlp{layer}_up_dw", h, dpre, 'tn', out_cols=True)
    dh = _matmul(f"mlp{layer}_up_dx", dpre, w1g, 'nt', epi=lambda acc, e: acc + e, extra=dh_res, b_cols=(layer,))
    return dh, dw1, dw2


def _local_step(x, target, w, wg, late_weights, exchange):
    n_tok = x.shape[0]
    tm = min(TOKEN_TILE, n_tok)
    n_tile, n_chunk = n_tok // tm, n_tok // CHUNK
    row = lambda v: v.reshape(1, -1)
    big = {n: wg[n].reshape(D_MODEL, D_MODEL) for n in ('rw_wr', 'rw_wk', 'rw_wv', 'rw_wo')}
    ln_g, ln_b = w['ln_g'], w['ln_b']
    grads = {}
    to_slots = lambda t: t.reshape(N_DEV, t.shape[0] // N_DEV, t.shape[1])

    halo_spec = (8, D_MODEL), lambda i: (jnp.maximum(i * (tm // 8) - 1, 0), 0)
    mix_p = [_tile(x, tm), (x,) + halo_spec, _full(w['rw_mu'][0]), _full(row(w['rw_w0'])), _full(w['rw_w1'][0]),
             _full(w['rw_w2'][0]), _full(row(w['rw_a0'])), _full(w['rw_a1'][0]), _full(w['rw_a2'][0]),
             _full(w['rw_g1'][0]), _full(w['rw_g2'][0])]
    tok_out = lambda dt: ((n_tok, D_MODEL), dt, (tm, D_MODEL), lambda i: (i, 0))
    xr, xk, xv, lw, a, g = _fwd_call("rwkv_mix", _mix_fn, (n_tile,), mix_p,
                                     [tok_out(BF16), tok_out(BF16), tok_out(BF16), tok_out(F32), tok_out(F32), tok_out(F32)])
    r = _matmul("rwkv_r", xr, big['rw_wr'], 'nn')
    k = _matmul("rwkv_k", xk, big['rw_wk'], 'nn')
    v = _matmul("rwkv_v", xv, big['rw_wv'], 'nn')

    scan_in = (r, lw, k, v, a, row(w['rw_k_k']), row(w['rw_k_a']), w['rw_r_k'].reshape(1, -1))
    o, bonus, zs = _rwkv_scan_fwd(*scan_in)

    gate_p = [_tile(o, tm), _tile(bonus, tm), _tile(g, tm), _full(row(w['rw_lnx_g'])), _full(row(w['rw_lnx_b']))]
    (og,) = _fwd_call("rwkv_gate", _gate_fn, (n_tile,), gate_p, [tok_out(BF16)])
    y0 = _matmul("rwkv_o", og, big['rw_wo'], 'nn')
    ln0_p = [_tile(x, tm), _tile(y0, tm), _full(ln_g[0:1]), _full(ln_b[0:1])]
    (h1,) = _fwd_call("ln0", _res_ln_fn, (n_tile,), ln0_p, [tok_out(F32)])
    late = late_weights(h1)
    w1g, w2g, glu_g = late['mlp_w1'], late['mlp_w2'], late['s5_w_glu']
    pre0, y1 = _mlp_fwd(0, h1, w1g, w2g)
    ln1_p = [_tile(h1, tm), _tile(y1, tm), _full(ln_g[1:2]), _full(ln_b[1:2])]
    (h2,) = _fwd_call("ln1", _res_ln_fn, (n_tile,), ln1_p, [tok_out(F32)])

    gp = (SSM_GROUPS, SSM_STATE)
    par_p = [_full(w['s5_a_re'][0]), _full(w['s5_a_im'][0]), _full(w['s5_log_dt'].reshape(SSM_GROUPS, 1))]
    s5_par_fn = lambda ar, ai, ld: _s5_param_fn(ar, ai, ld, None, None)
    gp_out = (gp, F32, gp, lambda i: (0, 0))
    abar_re, abar_im, coef_re, coef_im = _fwd_call("s5_param", s5_par_fn, (1,), par_p, [gp_out] * 4)
    b_flat = [w[n][0].reshape(SSM_LANES, SSM_GROUP) for n in ('s5_b_re', 's5_b_im')]
    bbar_p = [_full(coef_re.reshape(SSM_LANES, 1)), _full(coef_im.reshape(SSM_LANES, 1)), _full(b_flat[0]), _full(b_flat[1])]
    bb_out = ((SSM_LANES, SSM_GROUP), F32, (SSM_LANES, SSM_GROUP), lambda i: (0, 0))
    bbar_re, bbar_im = _fwd_call("s5_bbar", _s5_bbar_fn, (1,), bbar_p, [bb_out] * 2)
    to_in = lambda t: _block_diag(t.reshape(N_SSM_BLK, 8, SSM_STATE, SSM_GROUP).transpose(0, 1, 3, 2))
    to_out = lambda t: _block_diag(t.reshape(N_SSM_BLK, 8, SSM_GROUP, SSM_STATE).transpose(0, 1, 3, 2))
    bblk_re, bblk_im = to_in(bbar_re), to_in(bbar_im)
    cblk_re, cblk_im = to_out(w['s5_c_re'][0]), to_out(w['s5_c_im'][0])
    a_row_re, a_row_im = abar_re.reshape(1, SSM_LANES), abar_im.reshape(1, SSM_LANES)

    d_row = row(w['s5_d'])
    s5_params = (bblk_re, bblk_im, a_row_re, a_row_im, cblk_re, cblk_im, d_row)
    s_re, s_im, yg = _s5_fwd(h2, *s5_params)
    z = _matmul("s5_glu", yg, glu_g, 'nn', b_cols=(0,))
    ln2_p = [_tile(h2, tm), _tile(z, tm), _full(ln_g[2:3]), _full(ln_b[2:3])]
    (h3,) = _fwd_call("ln2", _glu_ln_fn, (n_tile,), ln2_p, [tok_out(F32)])
    pre1, y3 = _mlp_fwd(1, h3, w1g, w2g)
    ln3_p = [_tile(h3, tm), _tile(y3, tm), _full(ln_g[3:4]), _full(ln_b[3:4])]
    (h4,) = _fwd_call("ln3", _res_ln_fn, (n_tile,), ln3_p, [tok_out(F32)])

    loss_acc, dh4 = _loss_call(h4, target, tm)

    ln_w = ['tile', 'tile_bf16', 'acc', 'acc']
    dh3_res, dy3, dg3, db3 = _bwd_call("ln3_bwd", _res_ln_fn, (n_tile,), ln3_p, [_tile(dh4, tm)], ln_w)
    dh3, dw1_1, dw2_1 = _mlp_bwd(1, h3, pre1, dy3, dh3_res, w1g, w2g)
    zero = exchange([('mlp_w1', 1, dw1_1), ('mlp_w2', 1, to_slots(dw2_1))])
    ln2_p[2] = _full(ln_g[2:3] + zero)
    dh2_res, dz, dg2, db2 = _bwd_call("ln2_bwd", _glu_ln_fn, (n_tile,), ln2_p, [_tile(dh3, tm)], ln_w)
    dyg = _matmul("s5_glu_dx", dz, glu_g, 'nt', b_cols=(0,))
    dw_glu = _matmul("s5_glu_dw", yg, dz, 'tn', out_cols=True)
    dh2, dbb_re, dbb_im, dcb_re, dcb_im, dd, dabar_re, dabar_im = _s5_bwd(dyg, h2, s_re, s_im, dh2_res, *s5_params)
    from_in = lambda t: _block_diag_t(t, SSM_GROUP, SSM_STATE).transpose(0, 1, 3, 2).reshape(SSM_LANES, SSM_GROUP)
    from_out = lambda t: _block_diag_t(t, SSM_STATE, SSM_GROUP).transpose(0, 1, 3, 2).reshape(1, SSM_GROUPS, SSM_GROUP, SSM_STATE)
    grads['s5_c_re'], grads['s5_c_im'] = from_out(dcb_re), from_out(dcb_im)
    grads['s5_d'] = dd
    dcoef_re, dcoef_im, db_re, db_im = _bwd_call(
        "s5_bbar_bwd", _s5_bbar_fn, (1,), bbar_p, [_full(from_in(dbb_re)), _full(from_in(dbb_im))], ['acc'] * 4)
    grads['s5_b_re'] = db_re.reshape(w['s5_b_re'].shape)
    grads['s5_b_im'] = db_im.reshape(w['s5_b_im'].shape)
    par_ct = [_full(dabar_re.reshape(gp)), _full(dabar_im.reshape(gp)), _full(dcoef_re.reshape(gp)), _full(dcoef_im.reshape(gp))]
    da_re, da_im, dlog_dt = _bwd_call("s5_param_bwd", s5_par_fn, (1,), par_p, par_ct, ['acc'] * 3)
    grads['s5_a_re'], grads['s5_a_im'], grads['s5_log_dt'] = da_re[None], da_im[None], dlog_dt.reshape(1, SSM_GROUPS)

    dh1_res, dy1, dg1, db1 = _bwd_call("ln1_bwd", _res_ln_fn, (n_tile,), ln1_p, [_tile(dh2, tm)], ln_w)
    dh1, dw1_0, dw2_0 = _mlp_bwd(0, h1, pre0, dy1, dh1_res, w1g, w2g)
    dx_res, dy0, dg0, db0 = _bwd_call("ln0_bwd", _res_ln_fn, (n_tile,), ln0_p, [_tile(dh1, tm)], ln_w)
    grads['ln_g'] = jnp.concatenate([dg0, dg1, dg2, dg3])
    grads['ln_b'] = jnp.concatenate([db0, db1, db2, db3])
    dog = _matmul("rwkv_o_dx", dy0, big['rw_wo'], 'nt')
    dw_o = to_slots(_matmul("rwkv_o_dw", og, dy0, 'tn'))
    zero = exchange([('s5_w_glu', 0, dw_glu), ('mlp_w1', 0, dw1_0), ('mlp_w2', 0, to_slots(dw2_0)), ('rw_wo', 0, dw_o)])
    gate_p[3] = _full(row(w['rw_lnx_g']) + zero)
    d_o, dbonus, dgate, grads['rw_lnx_g'], grads['rw_lnx_b'] = _bwd_call(
        "rwkv_gate_bwd", _gate_fn, (n_tile,), gate_p, [_tile(dog, tm)], ['tile', 'tile', 'tile', 'acc', 'acc'])
    dr, dlw, dk, dv, da, grads['rw_k_k'], grads['rw_k_a'], dr_k = _rwkv_scan_bwd(*scan_in, zs, d_o, dbonus)
    grads['rw_r_k'] = dr_k.reshape(w['rw_r_k'].shape)
    dw_r = to_slots(_matmul("rwkv_r_dw", xr, dr, 'tn'))
    dw_k = to_slots(_matmul("rwkv_k_dw", xk, dk, 'tn'))
    dw_v = to_slots(_matmul("rwkv_v_dw", xv, dv, 'tn'))
    zero = exchange([('rw_wr', 0, dw_r), ('rw_wk', 0, dw_k), ('rw_wv', 0, dw_v)])
    mix_p[3] = _full(row(w['rw_w0']) + zero)
    dxr = _matmul("rwkv_r_dx", dr, big['rw_wr'], 'nt')
    dxk = _matmul("rwkv_k_dx", dk, big['rw_wk'], 'nt')
    dxv = _matmul("rwkv_v_dx", dv, big['rw_wv'], 'nt')
    mix_ct = [_tile(t, tm) for t in (dxr, dxk, dxv, dlw, da, dgate)]
    halo_grad = ('tile', ((n_tile * 8, D_MODEL), (8, D_MODEL), lambda i: (i, 0)))
    res = _bwd_call("rwkv_mix_bwd", _mix_fn, (n_tile,), mix_p, mix_ct, ['tile', halo_grad] + ['acc'] * 9,
                    addends={0: _tile(dx_res, tm)})
    corr = res[1].reshape(n_tile, 8, D_MODEL)[1:, 7:8, :]
    corr = jnp.pad(corr, ((0, 1), (tm - 1, 0), (0, 0)))
    dx = (res[0].reshape(n_tile, tm, D_MODEL) + corr).reshape(n_tok, D_MODEL)
    (grads['rw_mu'], grads['rw_w0'], grads['rw_w1'], grads['rw_w2'], grads['rw_a0'], grads['rw_a1'], grads['rw_a2'],
     grads['rw_g1'], grads['rw_g2']) = [t[None] if t.shape[0] != 1 else t for t in res[2:]]
    return loss_acc[0, 0], dx, grads


def _full_shape(local_shape, axis):
    return local_shape[:axis] + (N_DEV * local_shape[axis],) + local_shape[axis + 1:]


def _split_shape(local_shape, axis):
    return local_shape[:axis] + (N_DEV, local_shape[axis]) + local_shape[axis + 1:]


def _mesh_pos():
    return lax.axis_index("x"), lax.axis_index("y"), lax.axis_index("c")


def _flip(v, f):
    return 1 - v if f else v


def _hbm_call(name, body, arrays, out_shapes, n_sems):
    n_t = len(arrays)
    hbm = pl.BlockSpec(memory_space=pl.ANY)
    return pl.pallas_call(
        body, name=name, out_shape=out_shapes, in_specs=[hbm] * n_t, out_specs=[hbm] * n_t,
        scratch_shapes=[pltpu.SemaphoreType.DMA((n_t, n_sems)), pltpu.SemaphoreType.DMA((n_t, n_sems)),
                        pltpu.SemaphoreType.DMA((n_t,))],
    )(*arrays)


def _all_gather(name, locals_):
    n_t = len(locals_)

    def body(*refs):
        x_refs, out_refs = refs[:n_t], refs[n_t:2 * n_t]
        send_sems, recv_sems, local_sems = refs[2 * n_t:]
        x, y, c = _mesh_pos()
        me, sibling = (x, y, c), (x, y, 1 - c)
        chips = [(1 - x, y), (x, 1 - y), (1 - x, 1 - y)]

        def copy(t, k, block, to, own=False):
            slot = out_refs[t].at[4 * block[0] + 2 * block[1] + block[2]]
            return pltpu.make_async_remote_copy(
                src_ref=x_refs[t] if own else slot, dst_ref=slot,
                send_sem=send_sems.at[t, k], recv_sem=recv_sems.at[t, k],
                device_id=to, device_id_type=pl.DeviceIdType.MESH)

        mine = [pltpu.make_async_copy(x_refs[t], out_refs[t].at[4 * x + 2 * y + c], local_sems.at[t]) for t in range(n_t)]
        for cp in mine:
            cp.start()
        sent = []
        for t in range(n_t):
            sent.append(copy(t, 0, me, sibling, own=True))
            sent += [copy(t, 1 + j, me, (*chip, c), own=True) for j, chip in enumerate(chips)]
        for cp in sent:
            cp.start()
        for j, chip in enumerate(chips):
            for t in range(n_t):
                copy(t, 1 + j, (*chip, c), me).wait_recv()
                passed = copy(t, 4 + j, (*chip, c), sibling)
                passed.start()
                sent.append(passed)
        for t in range(n_t):
            copy(t, 0, sibling, me).wait_recv()
            for j, chip in enumerate(chips):
                copy(t, 4 + j, (*chip, 1 - c), me).wait_recv()
        for cp in sent:
            cp.wait_send()
        for cp in mine:
            cp.wait()

    outs = [jax.ShapeDtypeStruct((N_DEV,) + a.shape, a.dtype) for a in locals_]
    return _hbm_call(name, body, locals_, outs, 7)


def _all_to_all(name, blocks):
    n_t = len(blocks)

    def body(*refs):
        x_refs, out_refs = refs[:n_t], refs[n_t:2 * n_t]
        send_sems, recv_sems, local_sems = refs[2 * n_t:]
        x, y, c = _mesh_pos()
        my_slot = 4 * x + 2 * y + c
        mine = [pltpu.make_async_copy(x_refs[t].at[my_slot], out_refs[t].at[my_slot], local_sems.at[t]) for t in range(n_t)]
        for cp in mine:
            cp.start()
        copies = []
        for k in range(1, N_DEV):
            peer = (_flip(x, k & 4), _flip(y, k & 2), _flip(c, k & 1))
            peer_slot = 4 * peer[0] + 2 * peer[1] + peer[2]
            for t in range(n_t):
                sems = dict(send_sem=send_sems.at[t, k - 1], recv_sem=recv_sems.at[t, k - 1],
                            device_id=peer, device_id_type=pl.DeviceIdType.MESH)
                cp = pltpu.make_async_remote_copy(src_ref=x_refs[t].at[peer_slot], dst_ref=out_refs[t].at[my_slot], **sems)
                cp.start()
                landed = pltpu.make_async_remote_copy(src_ref=x_refs[t].at[my_slot], dst_ref=out_refs[t].at[peer_slot], **sems)
                copies.append((cp, landed))
        for sent, landed in copies:
            landed.wait_recv()
            sent.wait_send()
        for cp in mine:
            cp.wait()

    outs = [jax.ShapeDtypeStruct(a.shape, a.dtype) for a in blocks]
    return _hbm_call(name, body, blocks, outs, 7)


_HBM = pl.BlockSpec(memory_space=pltpu.HBM)
_SEM = pl.BlockSpec(memory_space=pltpu.SEMAPHORE)
_EFFECT = pltpu.SideEffectType.DATAFLOW_SIDE_EFFECTING


def _peer_copies(x_refs, land_refs, send_sems, recv_sems, gather, mine):
    x, y, c = _mesh_pos()
    my_slot = 4 * x + 2 * y + c
    copies = []
    for k in range(1, N_DEV):
        peer = (_flip(x, k & 4), _flip(y, k & 2), _flip(c, k & 1))
        peer_slot = 4 * peer[0] + 2 * peer[1] + peer[2]
        for t, (x_ref, land) in enumerate(zip(x_refs, land_refs)):
            sem = t * (N_DEV - 1) + k - 1
            copies.append(pltpu.make_async_remote_copy(
                src_ref=x_ref if gather else x_ref.at[peer_slot], dst_ref=land.at[my_slot if mine else peer_slot],
                send_sem=send_sems.at[sem], recv_sem=recv_sems.at[sem], device_id=peer, device_id_type=pl.DeviceIdType.MESH))
    return copies


def _push_start(name, arrays, gather):
    n_t = len(arrays)
    lands = [lax.empty((N_DEV,) + a.shape if gather else a.shape, a.dtype) for a in arrays]

    def body(*refs):
        x_refs, land_refs = refs[:n_t], refs[n_t:2 * n_t]
        send_sems, recv_sems = refs[2 * n_t], refs[2 * n_t + 1]
        for sent in _peer_copies(x_refs, land_refs, send_sems, recv_sems, gather, True):
            sent.start()
        refs[-1][...] = jnp.zeros((8, 128), F32)

    held = list(arrays) + lands
    sems = pltpu.SemaphoreType.DMA((n_t * (N_DEV - 1),))
    out = pl.pallas_call(
        body, name=name,
        out_shape=(sems, sems, *[pltpu.HBM(a.shape, a.dtype) for a in held], jax.ShapeDtypeStruct((8, 128), F32)),
        in_specs=[_HBM] * (2 * n_t),
        out_specs=(_SEM, _SEM, *[_HBM] * (2 * n_t), pl.BlockSpec(memory_space=pltpu.VMEM)),
        input_output_aliases={i: 2 + i for i in range(2 * n_t)},
        compiler_params=pltpu.CompilerParams(has_side_effects=_EFFECT),
    )(*[pltpu.with_memory_space_constraint(a, pltpu.HBM) for a in held])
    return dict(send=out[0], recv=out[1], held=out[2:2 + 2 * n_t], zero=out[-1][0, 0], gather=gather)


def _push_wait(name, started, after):
    held, gather = started['held'], started['gather']
    n_t = len(held) // 2

    def body(*refs):
        x_refs, land_refs = refs[:n_t], refs[n_t:2 * n_t]
        send_sems, recv_sems = refs[2 * n_t], refs[2 * n_t + 1]
        for sent in _peer_copies(x_refs, land_refs, send_sems, recv_sems, gather, True):
            sent.wait_send()
        for landed in _peer_copies(x_refs, land_refs, send_sems, recv_sems, gather, False):
            landed.wait_recv()

    out = pl.pallas_call(
        body, name=name,
        out_shape=[pltpu.HBM(a.shape, a.dtype) for a in held],
        in_specs=[_HBM] * (2 * n_t) + [_SEM, _SEM, pl.BlockSpec(memory_space=pl.ANY)],
        out_specs=[_HBM] * (2 * n_t),
        input_output_aliases={i: i for i in range(2 * n_t)},
        compiler_params=pltpu.CompilerParams(has_side_effects=_EFFECT),
    )(*held, started['send'], started['recv'], after)
    return out[:n_t], out[n_t:]


ADAM_BLOCK = 128 * 1024


def _adam_update(part, w_ref, m_ref, v_ref, g_ref, d_ref, nm_ref, nv_ref):
    g = part(0)
    for i in range(1, N_DEV):
        g = g + part(i)
    m_new = ADAM_B1 * m_ref[...] + (1.0 - ADAM_B1) * g
    v_new = ADAM_B2 * v_ref[...] + (1.0 - ADAM_B2) * (g * g)
    m_hat = m_new * (1.0 / (1.0 - ADAM_B1 ** ADAM_STEP))
    v_hat = v_new * (1.0 / (1.0 - ADAM_B2 ** ADAM_STEP))
    g_ref[...] = g
    nm_ref[...] = m_new
    nv_ref[...] = v_new
    d_ref[...] = -ADAM_LR * (m_hat / (jnp.sqrt(v_hat) + ADAM_EPS) + ADAM_WD * w_ref[...])


def _adam_many(name, items):
    n_t = len(items)

    def body(*refs):
        ins, outs = refs[:4 * n_t], refs[4 * n_t:]
        for t in range(n_t):
            s_ref, w_ref, m_ref, v_ref = ins[4 * t:4 * t + 4]
            _adam_update(lambda i: s_ref[i], w_ref, m_ref, v_ref, *outs[4 * t:4 * t + 4])

    vmem = pl.BlockSpec(memory_space=pltpu.VMEM)
    out = pl.pallas_call(
        body, name=name, in_specs=[vmem] * (4 * n_t), out_specs=[vmem] * (4 * n_t),
        out_shape=[jax.ShapeDtypeStruct(w.shape, F32) for _, w, _, _ in items for _ in range(4)],
        compiler_params=pltpu.CompilerParams(vmem_limit_bytes=VMEM_LIMIT),
    )(*[a for item in items for a in item])
    return [out[4 * t:4 * t + 4] for t in range(n_t)]


def _adam_call(name, slots, w, m, v, own=None):
    rows, width = w.shape
    tile = min(rows, ADAM_BLOCK // max(width, 128))

    def body(*refs):
        if own is None:
            s_ref, w_ref, m_ref, v_ref = refs[:4]
            part = lambda i: s_ref[i]
        else:
            s_ref, own_ref, w_ref, m_ref, v_ref = refs[:5]
            x, y, c = _mesh_pos()
            part = lambda i: jnp.where(4 * x + 2 * y + c == i, own_ref[...], s_ref[i])
        _adam_update(part, w_ref, m_ref, v_ref, *refs[-4:])

    blk = _spec((tile, width), lambda i: (i, 0))
    ops = [slots] + ([] if own is None else [own]) + [w, m, v]
    return pl.pallas_call(
        body, name=name, grid=(rows // tile,),
        in_specs=[_spec((N_DEV, tile, width), lambda i: (0, i, 0))] + [blk] * (len(ops) - 1),
        out_specs=[blk] * 4, out_shape=[jax.ShapeDtypeStruct(w.shape, F32)] * 4,
        compiler_params=_cparams(1),
    )(*ops)


def kernel(x, ln_g, ln_b, rw_mu, rw_w0, rw_w1, rw_w2, rw_a0, rw_a1, rw_a2, rw_g1, rw_g2, rw_k_k, rw_k_a, rw_r_k, rw_wr, rw_wk, rw_wv, rw_wo, rw_lnx_g, rw_lnx_b, s5_a_re, s5_a_im, s5_log_dt, s5_b_re, s5_b_im, s5_c_re, s5_c_im, s5_d, s5_w_glu, mlp_w1, mlp_w2, loss_target, m_ln_g, m_ln_b, m_rw_mu, m_rw_w0, m_rw_w1, m_rw_w2, m_rw_a0, m_rw_a1, m_rw_a2, m_rw_g1, m_rw_g2, m_rw_k_k, m_rw_k_a, m_rw_r_k, m_rw_wr, m_rw_wk, m_rw_wv, m_rw_wo, m_rw_lnx_g, m_rw_lnx_b, m_s5_a_re, m_s5_a_im, m_s5_log_dt, m_s5_b_re, m_s5_b_im, m_s5_c_re, m_s5_c_im, m_s5_d, m_s5_w_glu, m_mlp_w1, m_mlp_w2, v_ln_g, v_ln_b, v_rw_mu, v_rw_w0, v_rw_w1, v_rw_w2, v_rw_a0, v_rw_a1, v_rw_a2, v_rw_g1, v_rw_g2, v_rw_k_k, v_rw_k_a, v_rw_r_k, v_rw_wr, v_rw_wk, v_rw_wv, v_rw_wo, v_rw_lnx_g, v_rw_lnx_b, v_s5_a_re, v_s5_a_im, v_s5_log_dt, v_s5_b_re, v_s5_b_im, v_s5_c_re, v_s5_c_im, v_s5_d, v_s5_w_glu, v_mlp_w1, v_mlp_w2):
    given = dict(locals())
    local_w = {n: given[n] for n in WEIGHTS}
    local_m = {n: given["m_" + n] for n in WEIGHTS}
    local_v = {n: given["v_" + n] for n in WEIGHTS}
    small = [n for n in WEIGHTS if n in SHARDED and n not in BIG]

    my_slot = 4 * lax.axis_index("x") + 2 * lax.axis_index("y") + lax.axis_index("c")
    as_bf16 = {n: local_w[n].astype(BF16) for n in BIG}

    late_started = _push_start("weights_late_start", [as_bf16[n] for n in LATE], gather=True)
    early = [n for n in BIG if n not in LATE]
    gathered = _all_gather("weights_all_gather", [as_bf16[n] for n in early] + [local_w[n] for n in small])
    early_gathered = dict(zip(early, gathered))
    full = dict(local_w)
    for n, blk in zip(small, gathered[len(early):]):
        full[n] = jnp.moveaxis(blk, 0, SHARDED[n]).reshape(_full_shape(local_w[n].shape, SHARDED[n]))
    full['rw_w0'] = full['rw_w0'] + late_started['zero']

    def late_weights(after):
        _, lands = _push_wait("weights_late_wait", late_started, after)
        own = lambda n: as_bf16[n][None]
        return {n: lax.dynamic_update_slice(land, own(n), (my_slot,) + (0,) * as_bf16[n].ndim) for n, land in zip(LATE, lands)}

    exchanges = []

    def exchange(entries):
        started = _push_start(f"grads_start_{len(exchanges)}", [t for _, _, t in entries], gather=False)
        exchanges.append((entries, started))
        return started['zero']

    loss_sq, dx, grads = _local_step(x[0], loss_target[0], full, early_gathered, late_weights, exchange)
    loss = (0.5 / D_MODEL) * lax.psum(loss_sq, MESH_AXES)

    by_name = {}
    for e, (entries, started) in enumerate(exchanges):
        sources, lands = _push_wait(f"grads_wait_{e}", started, dx)
        for (n, i, _), src, slots in zip(entries, sources, lands):
            part = lambda d: d[n].reshape((-1,) + slots.shape[1:])[i]
            own = lax.dynamic_index_in_dim(src, my_slot, 0, keepdims=False)
            out = _adam_call(f"adamw_{n}_{i}", slots, part(local_w), part(local_m), part(local_v), own=own)
            by_name.setdefault(n, {})[i] = out
    for n in BIG:
        per_layer = [by_name[n][i] for i in sorted(by_name[n])]
        by_name[n] = [jnp.stack(ts).reshape(local_w[n].shape) for ts in zip(*per_layer)]
    split = lambda n: jnp.moveaxis(grads[n].reshape(_split_shape(local_w[n].shape, SHARDED[n])), SHARDED[n], 0)
    slots = dict(zip(small, _all_to_all("grads_all_to_all", [split(n) for n in small])))
    slots.update(zip(REPLICATED, _all_gather("grads_all_gather", [grads[n] for n in REPLICATED])))
    wide = {'s5_b_re': (SSM_LANES, SSM_GROUP), 's5_b_im': (SSM_LANES, SSM_GROUP),
            's5_c_re': (D_MODEL, SSM_STATE), 's5_c_im': (D_MODEL, SSM_STATE)}
    for n, view in wide.items():
        as_view = lambda t: t.reshape(view)
        out = _adam_call(f"adamw_{n}", slots[n].reshape((N_DEV,) + view), as_view(local_w[n]), as_view(local_m[n]), as_view(local_v[n]))
        by_name[n] = [t.reshape(local_w[n].shape) for t in out]
    rest = [n for n in small + REPLICATED if n not in wide]
    for n, out in zip(rest, _adam_many("adamw_small", [(slots[n], local_w[n], local_m[n], local_v[n]) for n in rest])):
        by_name[n] = out
    results = [by_name[n][k] for k in range(4) for n in WEIGHTS]
    return (loss, dx[None], *results)
```

```python
import functools
import math

import jax
import jax.numpy as jnp
from jax import lax
from jax.experimental import pallas as pl
from jax.experimental.pallas import tpu as pltpu

F32 = jnp.float32
BF16 = jnp.bfloat16
HI = lax.Precision.HIGHEST

D_MODEL = 1024
HEAD = 64
PAIR = 2 * HEAD
N_PAIR = D_MODEL // PAIR
CHUNK = 64
GN_EPS = 64e-5
LN_EPS = 1e-5
SSM_GROUP = 16
SSM_STATE = 64
SSM_GROUPS = D_MODEL // SSM_GROUP
SSM_LANES = SSM_GROUPS * SSM_STATE
SSM_BLK_IN = 128
SSM_BLK_ST = 512
N_SSM_BLK = D_MODEL // SSM_BLK_IN
DEPTH = 2
DN_ALPHA = (2.0 * DEPTH) ** 0.25
ADAM_LR, ADAM_B1, ADAM_B2, ADAM_EPS, ADAM_WD, ADAM_STEP = 0.001, 0.9, 0.999, 1e-08, 0.01, 10
N_DEV = 8
MESH_AXES = ("x", "y", "c")
VMEM_LIMIT = 56 * 1024 * 1024
TOKEN_TILE = 256

SHARDED = {
    "rw_mu": 2, "rw_w1": 1, "rw_w2": 2, "rw_a1": 1, "rw_a2": 2, "rw_g1": 1, "rw_g2": 2,
    "rw_wr": 1, "rw_wk": 1, "rw_wv": 1, "rw_wo": 1, "s5_d": 1, "s5_w_glu": 2, "mlp_w1": 2, "mlp_w2": 1,
}
WEIGHTS = ['ln_g', 'ln_b', 'rw_mu', 'rw_w0', 'rw_w1', 'rw_w2', 'rw_a0', 'rw_a1', 'rw_a2', 'rw_g1', 'rw_g2', 'rw_k_k',
           'rw_k_a', 'rw_r_k', 'rw_wr', 'rw_wk', 'rw_wv', 'rw_wo', 'rw_lnx_g', 'rw_lnx_b', 's5_a_re', 's5_a_im',
           's5_log_dt', 's5_b_re', 's5_b_im', 's5_c_re', 's5_c_im', 's5_d', 's5_w_glu', 'mlp_w1', 'mlp_w2']
REPLICATED = [n for n in WEIGHTS if n not in SHARDED]
BIG = ['rw_wr', 'rw_wk', 'rw_wv', 'rw_wo', 's5_w_glu', 'mlp_w1', 'mlp_w2']
LATE = ['s5_w_glu', 'mlp_w1', 'mlp_w2']


def _cparams(n_grid):
    return pltpu.CompilerParams(dimension_semantics=("arbitrary",) * n_grid, vmem_limit_bytes=VMEM_LIMIT)


@jax.custom_vjp
def _mm(x, w):
    return jnp.dot(x.astype(BF16), w.astype(BF16), preferred_element_type=F32)


def _mm_fwd(x, w):
    return _mm(x, w), (x, w)


def _mm_bwd(res, dy):
    x, w = res
    dyb = dy.astype(BF16)
    dx = lax.dot_general(dyb, w.astype(BF16), (((1,), (1,)), ((), ())), preferred_element_type=F32)
    dw = lax.dot_general(x.astype(BF16), dyb, (((0,), (0,)), ((), ())), preferred_element_type=F32)
    return dx, dw


_mm.defvjp(_mm_fwd, _mm_bwd)


_DOT_DIMS = {'nn': (((1,), (0,)), ((), ())), 'nt': (((1,), (1,)), ((), ())), 'tn': (((0,), (0,)), ((), ()))}
_BATCH_DOT_DIMS = {'nn': (((2,), (1,)), ((0,), (0,))), 'nt': (((2,), (2,)), ((0,), (0,))), 'tn': (((1,), (1,)), ((0,), (0,)))}
CHUNK_PAIRS = 8
CHUNK_PASSES = 1


def _split_bf16(a):
    hi = a.astype(BF16)
    return hi, (a - hi.astype(F32)).astype(BF16)


def _dot_passes(a, b, mode, passes=None):
    dims = (_DOT_DIMS if a.ndim == 2 else _BATCH_DOT_DIMS)[mode]
    dot = lambda p, q: lax.dot_general(p, q, dims, preferred_element_type=F32)
    if (CHUNK_PASSES if passes is None else passes) == 1:
        return dot(a.astype(BF16), b.astype(BF16))
    (ah, al), (bh, bl) = _split_bf16(a), _split_bf16(b)
    return dot(ah, bh) + (dot(ah, bl) + dot(al, bh))


@functools.partial(jax.custom_vjp, nondiff_argnums=(2,))
def _pdot(a, b, mode):
    return _dot_passes(a, b, mode)


def _pdot_fwd(a, b, mode):
    return _dot_passes(a, b, mode), (a, b)


def _pdot_bwd(mode, res, dy):
    a, b = res
    if mode == 'nn':
        return _dot_passes(dy, b, 'nt'), _dot_passes(a, dy, 'tn')
    if mode == 'nt':
        return _dot_passes(dy, b, 'nn'), _dot_passes(dy, a, 'tn')
    return _dot_passes(b, dy, 'nt'), _dot_passes(a, dy, 'nn')


_pdot.defvjp(_pdot_fwd, _pdot_bwd)


def _tri_sum(x, upper):
    nb, n = x.shape[0], x.shape[1]
    ti = lax.broadcasted_iota(jnp.int32, (nb, n, n), 1)
    tj = lax.broadcasted_iota(jnp.int32, (nb, n, n), 2)
    tri = ((ti <= tj) if upper else (ti >= tj)).astype(BF16)
    hi = x.astype(BF16)
    rest = x - hi.astype(F32)
    mid = rest.astype(BF16)
    lo = (rest - mid.astype(F32)).astype(BF16)
    dot = lambda q: lax.dot_general(tri, q, _BATCH_DOT_DIMS['nn'], preferred_element_type=F32)
    return dot(hi) + (dot(mid) + dot(lo))


@jax.custom_vjp
def _cumsum_rows(x):
    return _tri_sum(x, False)


_cumsum_rows.defvjp(lambda x: (_tri_sum(x, False), None), lambda _, dy: (_tri_sum(dy, True),))


@functools.partial(jax.custom_vjp, nondiff_argnums=(1,))
def _roll_rows(x, shift):
    return pltpu.roll(x, shift, 0)


def _roll_rows_fwd(x, shift):
    return pltpu.roll(x, shift, 0), None


def _roll_rows_bwd(shift, _, dy):
    n = dy.shape[0]
    return (pltpu.roll(dy, (n - shift) % n, 0),)


_roll_rows.defvjp(_roll_rows_fwd, _roll_rows_bwd)


def _shift_down(x, first_row):
    row = lax.broadcasted_iota(jnp.int32, x.shape, 0)
    return jnp.where(row == 0, first_row, _roll_rows(x, 1))


def _sigmoid(x):
    return 1.0 / (1.0 + jnp.exp(-x))


def _softplus(x):
    return jnp.maximum(x, 0.0) + jnp.log(1.0 + jnp.exp(-jnp.abs(x)))


def _gelu(x):
    return 0.5 * x * (1.0 + jnp.tanh(math.sqrt(2.0 / math.pi) * (x + 0.044715 * (x * x * x))))


def _layer_norm(x, g, b):
    mu = jnp.mean(x, axis=-1, keepdims=True)
    xc = x - mu
    var = jnp.mean(xc * xc, axis=-1, keepdims=True)
    return xc * lax.rsqrt(var + LN_EPS) * g + b


def _spec(block, index_map):
    return pl.BlockSpec(block, index_map)


def _tile(arr, tm):
    return (arr, (tm, arr.shape[1]), lambda i: (i, 0))


def _full(arr):
    nd = arr.ndim
    return (arr, arr.shape, lambda *_: (0,) * nd)


def _fwd_call(name, fn, grid, prims, outs):
    n_in = len(prims)

    def body(*refs):
        vals = [r[...] for r in refs[:n_in]]
        vals = [v.astype(F32) if v.dtype != F32 else v for v in vals]
        res = fn(*vals)
        for o, r in zip(refs[n_in:], res):
            o[...] = r.astype(o.dtype)

    return pl.pallas_call(
        body, name=name, grid=grid,
        in_specs=[_spec(b, m) for _, b, m in prims],
        out_specs=[_spec(b, m) for _, _, b, m in outs],
        out_shape=[jax.ShapeDtypeStruct(s, d) for s, d, _, _ in outs],
        compiler_params=_cparams(len(grid)),
    )(*[a for a, _, _ in prims])


def _bwd_call(name, fn, grid, prims, cts, wants, addends=None):
    addends = addends or {}
    n_in, n_ct = len(prims), len(cts)
    out_place = {}
    wants, out_dtype = list(wants), {}
    for i, w in enumerate(wants):
        if isinstance(w, tuple):
            wants[i], out_place[i] = w
        elif w == 'tile_bf16':
            wants[i], out_dtype[i] = 'tile', BF16
    diff = [i for i, w in enumerate(wants) if w]
    add_idx = [i for i in diff if i in addends]
    last_axis = len(grid) - 1

    def body(*refs):
        vals = [r[...] for r in refs[:n_in]]
        vals = [v.astype(F32) if v.dtype != F32 else v for v in vals]
        ct_vals = [r[...] for r in refs[n_in:n_in + n_ct]]
        ct_vals = [v.astype(F32) if v.dtype != F32 else v for v in ct_vals]
        add_refs = dict(zip(add_idx, refs[n_in + n_ct:n_in + n_ct + len(add_idx)]))
        out_refs = refs[n_in + n_ct + len(add_idx):]

        def f(*dargs):
            full = list(vals)
            for i, a in zip(diff, dargs):
                full[i] = a
            return tuple(fn(*full))

        _, vjp = jax.vjp(f, *[vals[i] for i in diff])
        grads = vjp(tuple(ct_vals))
        first = pl.program_id(last_axis) == 0
        for o, g, i in zip(out_refs, grads, diff):
            if wants[i] == 'tile':
                if i in add_refs:
                    g = g + add_refs[i][...]
                o[...] = g.astype(o.dtype)
            else:
                @pl.when(first)
                def _(o=o):
                    o[...] = jnp.zeros(o.shape, o.dtype)
                o[...] += g

    ins = list(prims) + list(cts) + [addends[i] for i in add_idx]
    places = [out_place.get(i, (prims[i][0].shape, prims[i][1], prims[i][2])) for i in diff]
    return pl.pallas_call(
        body, name=name, grid=grid,
        in_specs=[_spec(b, m) for _, b, m in ins],
        out_specs=[_spec(b, m) for _, b, m in places],
        out_shape=[jax.ShapeDtypeStruct(s, out_dtype.get(i, F32)) for i, (s, _, _) in zip(diff, places)],
        compiler_params=_cparams(len(grid)),
    )(*[a for a, _, _ in ins])


def _matmul(name, a, b, mode, *, out_dtype=F32, lhs_fn=None, epi=None, extra=None, tm=1024, tn=1024, tk=1024,
            b_cols=None, b_rows=None, out_cols=False):
    if mode == 'tn':
        kdim, m = a.shape
    else:
        m, kdim = a.shape
    if b_rows is not None:
        loc, cols = b.shape[-2], b.shape[-1]
        if mode == 'nn':
            n, k_shards = cols, max(1, tk // loc)
            tk = k_shards * loc
        else:
            n, tn = N_DEV * loc, loc
    elif b_cols is None:
        n = b.shape[0] if mode == 'nt' else b.shape[1]
    else:
        rows, loc = b.shape[-2], b.shape[-1]
        if mode == 'nn':
            n, tn = N_DEV * loc, loc
        else:
            n, k_shards = rows, max(1, tk // loc)
            tk = k_shards * loc
    if out_cols:
        tn = n // N_DEV
    tm, tn, tk = min(tm, m), min(tn, n), min(tk, kdim)
    nk = kdim // tk
    grid = (m // tm, n // tn, nk)
    a_spec = _spec((tk, tm), lambda i, j, k: (k, i)) if mode == 'tn' else _spec((tm, tk), lambda i, j, k: (i, k))
    if b_rows is not None:
        lead = tuple(b_rows)
        skip = (None,) * (1 + len(lead))
        if mode == 'nn':
            b_spec = _spec((k_shards,) + skip[1:] + (tk // k_shards, tn), lambda i, j, k: (k,) + lead + (0, j))
        else:
            b_spec = _spec(skip + (tn, tk), lambda i, j, k: (j,) + lead + (0, k))
    elif b_cols is None:
        b_spec = _spec((tn, tk), lambda i, j, k: (j, k)) if mode == 'nt' else _spec((tk, tn), lambda i, j, k: (k, j))
    else:
        lead = tuple(b_cols)
        skip = (None,) * (1 + len(lead))
        if mode == 'nn':
            b_spec = _spec(skip + (tk, tn), lambda i, j, k: (j,) + lead + (k, 0))
        else:
            b_spec = _spec((k_shards,) + skip[1:] + (tn, tk // k_shards), lambda i, j, k: (k,) + lead + (j, 0))
    o_spec = _spec((tm, tn), lambda i, j, k: (i, j))
    if out_cols:
        o_place, o_shape = _spec((None, tm, tn), lambda i, j, k: (j, i, 0)), (N_DEV, m, tn)
    else:
        o_place, o_shape = o_spec, (m, n)
    dims = _DOT_DIMS[mode]

    def body(*refs):
        if extra is None:
            a_ref, b_ref, o_ref, acc = refs
            x_ref = None
        else:
            a_ref, b_ref, x_ref, o_ref, acc = refs
        k = pl.program_id(2)
        av = a_ref[...]
        if lhs_fn is not None:
            av = lhs_fn(av.astype(F32))
        bv = b_ref[...]
        if bv.ndim == 3:
            bv = bv.reshape(tk, tn) if mode == 'nn' else jnp.concatenate([bv[g] for g in range(bv.shape[0])], axis=1)
        part = lax.dot_general(av.astype(BF16), bv.astype(BF16), dims, preferred_element_type=F32)

        def finish(r):
            if epi is not None:
                r = epi(r, x_ref[...])
            o_ref[...] = r.astype(o_ref.dtype)

        if nk == 1:
            finish(part)
            return

        @pl.when(k == 0)
        def _():
            acc[...] = part

        @pl.when((k > 0) & (k < nk - 1))
        def _():
            acc[...] += part

        @pl.when(k == nk - 1)
        def _():
            finish(acc[...] + part)

    ops, specs = [a, b], [a_spec, b_spec]
    if extra is not None:
        ops.append(extra)
        specs.append(o_spec)
    return pl.pallas_call(
        body, name=name, grid=grid, in_specs=specs, out_specs=o_place,
        out_shape=jax.ShapeDtypeStruct(o_shape, out_dtype),
        scratch_shapes=[pltpu.VMEM((tm, tn), F32)],
        compiler_params=_cparams(3),
    )(*ops)


def _relu2(x):
    r = jnp.maximum(x, 0.0)
    return r * r


def _mix_fn(x, halo, mu, w0, w1, w2, a0, a1, a2, g1, g2):
    prev_row = jnp.where(pl.program_id(0) == 0, 0.0, halo[7:8, :])
    xx = _shift_down(x, prev_row) - x
    xr, xw, xk, xv, xa, xg = (x + xx * mu[i:i + 1, :] for i in range(6))
    w_pre = w0 + _mm(jnp.tanh(_mm(xw, w1)), w2)
    log_decay = -jnp.exp(-_softplus(-w_pre) - 0.5)
    a = _sigmoid(a0 + _mm(_mm(xa, a1), a2))
    g = _mm(_sigmoid(_mm(xg, g1)), g2)
    return xr, xk, xv, log_decay, a, g


def _chunk_fn(r, lw, k, v, a, k_k, k_a, r_k):
    n, n_pair = r.shape[0], r.shape[1] // PAIR
    to_batch = lambda t: jnp.concatenate([t[None, :, p * PAIR:(p + 1) * PAIR] for p in range(n_pair)], axis=0)
    from_batch = lambda t: jnp.concatenate([t[p] for p in range(n_pair)], axis=1)
    r, lw, k, v, a, k_k, k_a, r_k = (to_batch(t) for t in (r, lw, k, v, a, k_k, k_a, r_k))
    lane = lax.broadcasted_iota(jnp.int32, (1, 1, PAIR), 2)
    m0 = lane < HEAD

    def head_sum(t):
        s0 = jnp.sum(jnp.where(m0, t, 0.0), axis=2, keepdims=True)
        s1 = jnp.sum(jnp.where(m0, 0.0, t), axis=2, keepdims=True)
        return jnp.where(m0, s0, s1)

    kk = k * k_k
    kk = kk / jnp.maximum(jnp.sqrt(head_sum(kk * kk)), 1e-12)
    k2 = k * (1.0 + (a - 1.0) * k_a)
    aa, bb = -kk, kk * a
    bonus = head_sum(r * k2 * r_k) * v

    cum = _cumsum_rows(lw)
    p_in, p_ex, p_inv = jnp.exp(cum), jnp.exp(cum - lw), jnp.exp(-cum)
    at, rt, kt, bt = aa * p_ex, r * p_in, k2 * p_inv, bb * p_inv

    def stack_masked(t):
        return jnp.concatenate([jnp.where(m0, t, 0.0), jnp.where(m0, 0.0, t)], axis=1)

    def unstack_sum(t):
        return t[:, :n] + t[:, n:]

    def unstack_select(t):
        return jnp.where(m0, t[:, :n], t[:, n:])

    ti = lax.broadcasted_iota(jnp.int32, (1, 2 * n, 2 * n), 1)
    tj = lax.broadcasted_iota(jnp.int32, (1, 2 * n, 2 * n), 2)
    same = (ti >= n) == (tj >= n)
    incl, strict = same & (ti >= tj), same & (ti > tj)
    eye_n = (ti == tj).astype(F32)
    m = 2 * n
    rows = lambda *ts: jnp.concatenate(ts, axis=1)
    cols = lambda *ts: jnp.concatenate(ts, axis=2)
    at_s, rt_s, kt_s, bt_s = stack_masked(at), stack_masked(rt), stack_masked(kt), stack_masked(bt)
    v_s, zeros = rows(v, v), jnp.zeros_like(at_s)
    bk_s = rows(bt_s, kt_s)
    gram = _pdot(rows(at_s, rt_s), bk_s, 'nt')
    a_ab = jnp.where(strict, gram[:, :m, :m], 0.0)
    a_ak = jnp.where(strict, gram[:, :m, m:], 0.0)
    a_rb = jnp.where(incl, gram[:, m:, :m], 0.0)
    a_rk = jnp.where(incl, gram[:, m:, m:], 0.0)
    pw, inv = _pdot(a_ab, a_ab, 'nn'), eye_n + a_ab
    for _ in range(int(math.log2(n)) - 2):
        both = _pdot(pw, cols(pw, inv), 'nn')
        pw, inv = both[:, :, :m], inv + both[:, :, m:]
    inv = inv + _pdot(pw, inv, 'nn')
    wy_s = _pdot(inv, cols(at_s, _pdot(a_ak, v_s, 'nn')), 'nn')
    q_oi = _pdot(cols(a_rb, a_rk), rows(wy_s, cols(zeros, v_s)), 'nn')
    q_all = unstack_sum(rt_s + q_oi[:, :, :PAIR])
    oi_all = unstack_select(q_oi[:, :, PAIR:])
    own_head = (lax.broadcasted_iota(jnp.int32, (1, m, PAIR), 1) >= n) == (lax.broadcasted_iota(jnp.int32, (1, m, PAIR), 2) >= HEAD)
    wy_m = cols(wy_s[:, :, :PAIR], jnp.where(own_head, wy_s[:, :, PAIR:], 0.0))
    mm_nn = _pdot(bk_s, rows(wy_m, cols(zeros, stack_masked(v))), 'tn')
    mm, nn = mm_nn[:, :, :PAIR], mm_nn[:, :, PAIR:]
    ei = lax.broadcasted_iota(jnp.int32, (1, PAIR, PAIR), 1)
    ej = lax.broadcasted_iota(jnp.int32, (1, PAIR, PAIR), 2)
    eye_p = (ei == ej).astype(F32)
    decay_col = jnp.sum(eye_p * p_in[:, n - 1:n, :], axis=2, keepdims=True)
    return ((decay_col * (eye_p + mm))[None], (decay_col * nn)[None], from_batch(q_all), from_batch(oi_all),
            from_batch(bonus))


def _gate_fn(o, bonus, g, lnx_g, lnx_b):
    lane = lax.broadcasted_iota(jnp.int32, (1, PAIR), 1)
    m0 = lane < HEAD

    def head_mean(t):
        s0 = jnp.sum(jnp.where(m0, t, 0.0), axis=1, keepdims=True)
        s1 = jnp.sum(jnp.where(m0, 0.0, t), axis=1, keepdims=True)
        return jnp.where(m0, s0, s1) * (1.0 / HEAD)

    outs = []
    for p in range(N_PAIR):
        sl = slice(p * PAIR, (p + 1) * PAIR)
        oc = o[:, sl] - head_mean(o[:, sl])
        on = oc * lax.rsqrt(head_mean(oc * oc) + GN_EPS) * lnx_g[:, sl] + lnx_b[:, sl]
        outs.append((on + bonus[:, sl]) * g[:, sl])
    return (jnp.concatenate(outs, axis=1),)


def _res_ln_fn(h, y, g, b):
    return (_layer_norm(DN_ALPHA * h + y, g, b),)


def _glu_ln_fn(h, z, g, b):
    mix = z[:, :D_MODEL] * _sigmoid(z[:, D_MODEL:])
    return (_layer_norm(DN_ALPHA * h + mix, g, b),)


def _s5_param_fn(a_re, a_im, log_dt, b_re, b_im):
    dt = jnp.exp(log_dt)
    lam_re = jnp.minimum(a_re, -1e-4)
    lam_im = a_im
    mag = jnp.exp(dt * lam_re)
    abar_re = mag * jnp.cos(dt * lam_im)
    abar_im = mag * jnp.sin(dt * lam_im)
    den = lam_re * lam_re + lam_im * lam_im
    nr, ni = abar_re - 1.0, abar_im
    coef_re = (nr * lam_re + ni * lam_im) / den
    coef_im = (ni * lam_re - nr * lam_im) / den
    return abar_re, abar_im, coef_re, coef_im


def _s5_bbar_fn(coef_re, coef_im, b_re, b_im):
    return coef_re * b_re - coef_im * b_im, coef_re * b_im + coef_im * b_re


def _s5_in_fn(u, b_re, b_im):
    return _pdot(u, b_re[0], 'nn'), _pdot(u, b_im[0], 'nn')


def _s5_out_fn(s_re, s_im, u, c_re, c_im, d):
    y = _pdot(s_re, c_re[0], 'nn') - _pdot(s_im, c_im[0], 'nn') + u * d
    return (_gelu(y),)


def _pairs_to_batch(t):
    return jnp.concatenate([t[None, :, p * PAIR:(p + 1) * PAIR] for p in range(N_PAIR)], axis=0)


def _batch_to_pairs(t):
    return jnp.concatenate([t[p] for p in range(N_PAIR)], axis=1)


STATE_PASSES = 3


def _rwkv_scan_fwd(r, lw, k, v, a, k_k, k_a, r_k):
    n_tok = r.shape[0]
    n_chunk = n_tok // CHUNK

    def body(r_ref, lw_ref, k_ref, v_ref, a_ref, kk_ref, ka_ref, rk_ref, o_ref, bonus_ref, zs_ref, z):
        @pl.when(pl.program_id(0) == 0)
        def _():
            z[...] = jnp.zeros(z.shape, F32)

        m_all, n_all, q, oi, bonus = _chunk_fn(r_ref[...], lw_ref[...], k_ref[...], v_ref[...], a_ref[...],
                                               kk_ref[...], ka_ref[...], rk_ref[...])
        zb = z[...]
        zs_ref[0] = zb
        bonus_ref[...] = bonus
        o_ref[...] = _batch_to_pairs(_dot_passes(_pairs_to_batch(q), zb, 'nn', STATE_PASSES)) + oi
        z[...] = _dot_passes(m_all[0], zb, 'nn', STATE_PASSES) + n_all[0]

    tok = _spec((CHUNK, D_MODEL), lambda c: (c, 0))
    par = _spec((1, D_MODEL), lambda c: (0, 0))
    st = _spec((1, N_PAIR, PAIR, PAIR), lambda c: (c, 0, 0, 0))
    return pl.pallas_call(
        body, name="rwkv_scan_fwd", grid=(n_chunk,),
        in_specs=[tok] * 5 + [par] * 3, out_specs=[tok, tok, st],
        out_shape=[jax.ShapeDtypeStruct((n_tok, D_MODEL), F32)] * 2 + [jax.ShapeDtypeStruct((n_chunk, N_PAIR, PAIR, PAIR), F32)],
        scratch_shapes=[pltpu.VMEM((N_PAIR, PAIR, PAIR), F32)],
        compiler_params=_cparams(1),
    )(r, lw, k, v, a, k_k, k_a, r_k)


def _rwkv_scan_bwd(r, lw, k, v, a, k_k, k_a, r_k, zs, d_o, d_bonus):
    n_tok = r.shape[0]
    n_chunk = n_tok // CHUNK

    def body(r_ref, lw_ref, k_ref, v_ref, a_ref, kk_ref, ka_ref, rk_ref, zs_ref, do_ref, db_ref,
             dr_ref, dlw_ref, dk_ref, dv_ref, da_ref, dkk_ref, dka_ref, drk_ref, g):
        sums = (dkk_ref, dka_ref, drk_ref)

        @pl.when(pl.program_id(0) == 0)
        def _():
            g[...] = jnp.zeros(g.shape, F32)
            for s_ref in sums:
                s_ref[...] = jnp.zeros(s_ref.shape, F32)

        prim = (r_ref[...], lw_ref[...], k_ref[...], v_ref[...], a_ref[...], kk_ref[...], ka_ref[...], rk_ref[...])
        (m_all, _, q, _, _), chunk_vjp = jax.vjp(_chunk_fn, *prim)
        gb, zb, d_out = g[...], zs_ref[0], do_ref[...]
        dob = _pairs_to_batch(d_out)
        d_m = _dot_passes(gb, zb, 'nt', STATE_PASSES)
        d_q = _batch_to_pairs(_dot_passes(dob, zb, 'nt', STATE_PASSES))
        grads = chunk_vjp((d_m[None], gb[None], d_q, d_out, db_ref[...]))
        for out_ref, val in zip((dr_ref, dlw_ref, dk_ref, dv_ref, da_ref), grads[:5]):
            out_ref[...] = val.astype(out_ref.dtype)
        for s_ref, val in zip(sums, grads[5:]):
            s_ref[...] += val
        same_head = (lax.broadcasted_iota(jnp.int32, (1, PAIR, PAIR), 1) // HEAD
                     == lax.broadcasted_iota(jnp.int32, (1, PAIR, PAIR), 2) // HEAD)
        g[...] = (_dot_passes(m_all[0], gb, 'tn', STATE_PASSES)
                  + jnp.where(same_head, _dot_passes(_pairs_to_batch(q), dob, 'tn', STATE_PASSES), 0.0))

    tok = _spec((CHUNK, D_MODEL), lambda c: (n_chunk - 1 - c, 0))
    par = _spec((1, D_MODEL), lambda c: (0, 0))
    st = _spec((1, N_PAIR, PAIR, PAIR), lambda c: (n_chunk - 1 - c, 0, 0, 0))
    tok_shape = lambda dt: jax.ShapeDtypeStruct((n_tok, D_MODEL), dt)
    return pl.pallas_call(
        body, name="rwkv_scan_bwd", grid=(n_chunk,),
        in_specs=[tok] * 5 + [par] * 3 + [st, tok, tok], out_specs=[tok] * 5 + [par] * 3,
        out_shape=[tok_shape(BF16), tok_shape(F32), tok_shape(BF16), tok_shape(BF16), tok_shape(F32)]
        + [jax.ShapeDtypeStruct((1, D_MODEL), F32)] * 3,
        scratch_shapes=[pltpu.VMEM((N_PAIR, PAIR, PAIR), F32)],
        compiler_params=_cparams(1),
    )(r, lw, k, v, a, k_k, k_a, r_k, zs, d_o, d_bonus)


S5_TIME_TILE = 512
S5_GROUP = 8


def _scan_rows(re_ref, im_ref, a1, carry, reverse):
    lb = re_ref.shape[1]
    grp, n_grp = S5_GROUP, re_ref.shape[0] // S5_GROUP

    def cmul(xr, xi, yr, yi):
        return xr * yr - xi * yi, xr * yi + xi * yr

    a2 = cmul(*a1, *a1)
    a4 = cmul(*a2, *a2)
    a8 = cmul(*a4, *a4)
    row = lax.broadcasted_iota(jnp.int32, (grp, lb), 0)
    expo = (grp - row) if reverse else (row + 1)
    pw = (jnp.ones((grp, lb), F32), jnp.zeros((grp, lb), F32))
    for bit, ap in ((1, a1), (2, a2), (4, a4), (8, a8)):
        nxt = cmul(*pw, *ap)
        sel = (expo & bit) != 0
        pw = (jnp.where(sel, nxt[0], pw[0]), jnp.where(sel, nxt[1], pw[1]))

    steps = []
    for d, ad in ((1, a1), (2, a2), (4, a4)):
        keep = (row < grp - d) if reverse else (row >= d)
        steps.append(((grp - d) if reverse else d, jnp.where(keep, ad[0], 0.0), jnp.where(keep, ad[1], 0.0)))

    def group(i, c):
        gi = (n_grp - 1 - i) if reverse else i
        rows = pl.ds(pl.multiple_of(gi * grp, grp), grp)
        xr, xi = re_ref[rows, :], im_ref[rows, :]
        for shift, mr, mi in steps:
            pr, pi = cmul(mr, mi, pltpu.roll(xr, shift, 0), pltpu.roll(xi, shift, 0))
            xr, xi = xr + pr, xi + pi
        cr, ci = cmul(*pw, c[0], c[1])
        xr, xi = xr + cr, xi + ci
        re_ref[rows, :] = xr
        im_ref[rows, :] = xi
        edge = slice(0, 1) if reverse else slice(grp - 1, grp)
        return xr[edge, :], xi[edge, :]

    return lax.fori_loop(0, n_grp, group, carry)


def _s5_fwd(u, b_re, b_im, a_re, a_im, c_re, c_im, d_row):
    n_tok = u.shape[0]
    tt = min(S5_TIME_TILE, n_tok)

    def body(u_ref, bre_ref, bim_ref, are_ref, aim_ref, cre_ref, cim_ref, d_ref, sre_ref, sim_ref, yg_ref, carry):
        @pl.when(pl.program_id(1) == 0)
        def _():
            carry[...] = jnp.zeros(carry.shape, F32)

        uv = u_ref[...]
        sre_ref[...], sim_ref[...] = _s5_in_fn(uv, bre_ref[...], bim_ref[...])
        c = _scan_rows(sre_ref, sim_ref, (are_ref[...], aim_ref[...]), (carry[0:1, :], carry[1:2, :]), False)
        carry[0:1, :] = c[0]
        carry[1:2, :] = c[1]
        (yg,) = _s5_out_fn(sre_ref[...], sim_ref[...], uv, cre_ref[...], cim_ref[...], d_ref[...])
        yg_ref[...] = yg.astype(yg_ref.dtype)

    u_blk = _spec((tt, SSM_BLK_IN), lambda l, t: (t, l))
    s_blk = _spec((tt, SSM_BLK_ST), lambda l, t: (t, l))
    blk3 = lambda arr: _spec((1,) + arr.shape[1:], lambda l, t: (l, 0, 0))
    par = lambda width: _spec((1, width), lambda l, t: (0, l))
    return pl.pallas_call(
        body, name="s5_fwd", grid=(N_SSM_BLK, n_tok // tt),
        in_specs=[u_blk, blk3(b_re), blk3(b_im), par(SSM_BLK_ST), par(SSM_BLK_ST), blk3(c_re), blk3(c_im), par(SSM_BLK_IN)],
        out_specs=[s_blk, s_blk, u_blk],
        out_shape=[jax.ShapeDtypeStruct((n_tok, SSM_LANES), F32)] * 2 + [jax.ShapeDtypeStruct((n_tok, D_MODEL), BF16)],
        scratch_shapes=[pltpu.VMEM((2, SSM_BLK_ST), F32)],
        compiler_params=_cparams(2),
    )(u, b_re, b_im, a_re, a_im, c_re, c_im, d_row)


def _s5_bwd(dyg, u, s_re, s_im, dh_res, b_re, b_im, a_re, a_im, c_re, c_im, d_row):
    n_tok = u.shape[0]
    tt = min(S5_TIME_TILE, n_tok)
    n_t = n_tok // tt

    def body(dyg_ref, u_ref, sre_ref, sim_ref, hre_ref, him_ref, res_ref, bre_ref, bim_ref, are_ref, aim_ref,
             cre_ref, cim_ref, d_ref, du_ref, dbre_ref, dbim_ref, dcre_ref, dcim_ref, dd_ref, dare_ref, daim_ref,
             carry, gre, gim):
        i = pl.program_id(1)
        sums = (dbre_ref, dbim_ref, dcre_ref, dcim_ref, dd_ref, dare_ref, daim_ref)

        @pl.when(i == 0)
        def _():
            carry[...] = jnp.zeros(carry.shape, F32)
            for r in sums:
                r[...] = jnp.zeros(r.shape, F32)

        uv, sre, sim = u_ref[...], sre_ref[...], sim_ref[...]
        _, out_vjp = jax.vjp(_s5_out_fn, sre, sim, uv, cre_ref[...], cim_ref[...], d_ref[...])
        gre[...], gim[...], du_out, dcre, dcim, dd = out_vjp((dyg_ref[...],))
        c = _scan_rows(gre, gim, (are_ref[...], -aim_ref[...]), (carry[0:1, :], carry[1:2, :]), True)
        carry[0:1, :] = c[0]
        carry[1:2, :] = c[1]
        g_re, g_im = gre[...], gim[...]
        at_start = i == n_t - 1
        p_re = _shift_down(sre, jnp.where(at_start, 0.0, hre_ref[7:8, :]))
        p_im = _shift_down(sim, jnp.where(at_start, 0.0, him_ref[7:8, :]))
        _, in_vjp = jax.vjp(_s5_in_fn, uv, bre_ref[...], bim_ref[...])
        du_in, dbre, dbim = in_vjp((g_re, g_im))
        du_ref[...] = du_out + du_in + res_ref[...]
        dare = jnp.sum(g_re * p_re + g_im * p_im, axis=0, keepdims=True)
        daim = jnp.sum(g_im * p_re - g_re * p_im, axis=0, keepdims=True)
        for r, val in zip(sums, (dbre, dbim, dcre, dcim, dd, dare, daim)):
            r[...] += val

    u_blk = _spec((tt, SSM_BLK_IN), lambda l, t: (n_t - 1 - t, l))
    s_blk = _spec((tt, SSM_BLK_ST), lambda l, t: (n_t - 1 - t, l))
    halo = _spec((8, SSM_BLK_ST), lambda l, t: (jnp.maximum((n_t - 1 - t) * (tt // 8) - 1, 0), l))
    blk3 = lambda arr: _spec((1,) + arr.shape[1:], lambda l, t: (l, 0, 0))
    par = lambda width: _spec((1, width), lambda l, t: (0, l))
    params = [blk3(b_re), blk3(b_im), par(SSM_BLK_ST), par(SSM_BLK_ST), blk3(c_re), blk3(c_im), par(SSM_BLK_IN)]
    shape = lambda arr: jax.ShapeDtypeStruct(arr.shape, F32)
    return pl.pallas_call(
        body, name="s5_bwd", grid=(N_SSM_BLK, n_t),
        in_specs=[u_blk, u_blk, s_blk, s_blk, halo, halo, u_blk] + params,
        out_specs=[u_blk, blk3(b_re), blk3(b_im), blk3(c_re), blk3(c_im), par(SSM_BLK_IN), par(SSM_BLK_ST), par(SSM_BLK_ST)],
        out_shape=[shape(u), shape(b_re), shape(b_im), shape(c_re), shape(c_im), shape(d_row), shape(a_re), shape(a_im)],
        scratch_shapes=[pltpu.VMEM((2, SSM_BLK_ST), F32), pltpu.VMEM((tt, SSM_BLK_ST), F32), pltpu.VMEM((tt, SSM_BLK_ST), F32)],
        compiler_params=_cparams(2),
    )(dyg, u, s_re, s_im, s_re, s_im, dh_res, b_re, b_im, a_re, a_im, c_re, c_im, d_row)


def _loss_call(h, target, tm):
    n_tok, d = h.shape

    def body(h_ref, t_ref, acc_ref, dh_ref):
        @pl.when(pl.program_id(0) == 0)
        def _():
            acc_ref[...] = jnp.zeros(acc_ref.shape, F32)

        e = h_ref[...] - t_ref[...]
        dh_ref[...] = e * (1.0 / d)
        acc_ref[...] += jnp.sum(jnp.sum(e * e, axis=1, keepdims=True), axis=0, keepdims=True)

    tok = _spec((tm, d), lambda i: (i, 0))
    return pl.pallas_call(
        body, name="loss", grid=(n_tok // tm,),
        in_specs=[tok, tok], out_specs=[_spec((8, 128), lambda i: (0, 0)), tok],
        out_shape=[jax.ShapeDtypeStruct((8, 128), F32), jax.ShapeDtypeStruct(h.shape, F32)],
        compiler_params=_cparams(1),
    )(h, target)


def _block_diag(t):
    nb, ng, rows, cols = t.shape
    eye = jnp.eye(ng, dtype=t.dtype)
    return jnp.einsum('bgrc,gh->bgrhc', t, eye).reshape(nb, ng * rows, ng * cols)


def _block_diag_t(t, rows, cols):
    nb, ng = t.shape[0], t.shape[1] // rows
    t5 = t.reshape(nb, ng, rows, ng, cols)
    return jnp.einsum('bgrhc,gh->bgrc', t5, jnp.eye(ng, dtype=t.dtype))


def _mlp_fwd(layer, h, w1g, w2g):
    pre = _matmul(f"mlp{layer}_up", h, w1g, 'nn', b_cols=(layer,), out_dtype=BF16)
    return pre, _matmul(f"mlp{layer}_down", pre, w2g, 'nn', lhs_fn=_relu2, b_rows=(layer,))


def _mlp_bwd(layer, h, pre, dy, dh_res, w1g, w2g):
    dpre = _matmul(f"mlp{layer}_down_dx", dy, w2g, 'nt', epi=lambda acc, p: acc * (2.0 * jnp.maximum(p, 0.0)),
                   extra=pre, b_rows=(layer,), out_dtype=BF16)
    dw2 = _matmul(f"mlp{layer}_down_dw", pre, dy, 'tn', lhs_fn=_relu2)
    dw1 = _matmul(f"mlp{layer}_up_dw", h, dpre, 'tn', out_cols=True)
    dh = _matmul(f"mlp{layer}_up_dx", dpre, w1g, 'nt', epi=lambda acc, e: acc + e, extra=dh_res, b_cols=(layer,))
    return dh, dw1, dw2


def _local_step(x, target, w, wg, late_weights, exchange):
    n_tok = x.shape[0]
    tm = min(TOKEN_TILE, n_tok)
    n_tile, n_chunk = n_tok // tm, n_tok // CHUNK
    row = lambda v: v.reshape(1, -1)
    big = {n: wg[n].reshape(D_MODEL, D_MODEL) for n in ('rw_wr', 'rw_wk', 'rw_wv', 'rw_wo')}
    ln_g, ln_b = w['ln_g'], w['ln_b']
    grads = {}
    to_slots = lambda t: t.reshape(N_DEV, t.shape[0] // N_DEV, t.shape[1])

    halo_spec = (8, D_MODEL), lambda i: (jnp.maximum(i * (tm // 8) - 1, 0), 0)
    mix_p = [_tile(x, tm), (x,) + halo_spec, _full(w['rw_mu'][0]), _full(row(w['rw_w0'])), _full(w['rw_w1'][0]),
             _full(w['rw_w2'][0]), _full(row(w['rw_a0'])), _full(w['rw_a1'][0]), _full(w['rw_a2'][0]),
             _full(w['rw_g1'][0]), _full(w['rw_g2'][0])]
    tok_out = lambda dt: ((n_tok, D_MODEL), dt, (tm, D_MODEL), lambda i: (i, 0))
    xr, xk, xv, lw, a, g = _fwd_call("rwkv_mix", _mix_fn, (n_tile,), mix_p,
                                     [tok_out(BF16), tok_out(BF16), tok_out(BF16), tok_out(F32), tok_out(F32), tok_out(F32)])
    r = _matmul("rwkv_r", xr, big['rw_wr'], 'nn')
    k = _matmul("rwkv_k", xk, big['rw_wk'], 'nn')
    v = _matmul("rwkv_v", xv, big['rw_wv'], 'nn')

    scan_in = (r, lw, k, v, a, row(w['rw_k_k']), row(w['rw_k_a']), w['rw_r_k'].reshape(1, -1))
    o, bonus, zs = _rwkv_scan_fwd(*scan_in)

    gate_p = [_tile(o, tm), _tile(bonus, tm), _tile(g, tm), _full(row(w['rw_lnx_g'])), _full(row(w['rw_lnx_b']))]
    (og,) = _fwd_call("rwkv_gate", _gate_fn, (n_tile,), gate_p, [tok_out(BF16)])
    y0 = _matmul("rwkv_o", og, big['rw_wo'], 'nn')
    ln0_p = [_tile(x, tm), _tile(y0, tm), _full(ln_g[0:1]), _full(ln_b[0:1])]
    (h1,) = _fwd_call("ln0", _res_ln_fn, (n_tile,), ln0_p, [tok_out(F32)])
    late = late_weights(h1)
    w1g, w2g, glu_g = late['mlp_w1'], late['mlp_w2'], late['s5_w_glu']
    pre0, y1 = _mlp_fwd(0, h1, w1g, w2g)
    ln1_p = [_tile(h1, tm), _tile(y1, tm), _full(ln_g[1:2]), _full(ln_b[1:2])]
    (h2,) = _fwd_call("ln1", _res_ln_fn, (n_tile,), ln1_p, [tok_out(F32)])

    gp = (SSM_GROUPS, SSM_STATE)
    par_p = [_full(w['s5_a_re'][0]), _full(w['s5_a_im'][0]), _full(w['s5_log_dt'].reshape(SSM_GROUPS, 1))]
    s5_par_fn = lambda ar, ai, ld: _s5_param_fn(ar, ai, ld, None, None)
    gp_out = (gp, F32, gp, lambda i: (0, 0))
    abar_re, abar_im, coef_re, coef_im = _fwd_call("s5_param", s5_par_fn, (1,), par_p, [gp_out] * 4)
    b_flat = [w[n][0].reshape(SSM_LANES, SSM_GROUP) for n in ('s5_b_re', 's5_b_im')]
    bbar_p = [_full(coef_re.reshape(SSM_LANES, 1)), _full(coef_im.reshape(SSM_LANES, 1)), _full(b_flat[0]), _full(b_flat[1])]
    bb_out = ((SSM_LANES, SSM_GROUP), F32, (SSM_LANES, SSM_GROUP), lambda i: (0, 0))
    bbar_re, bbar_im = _fwd_call("s5_bbar", _s5_bbar_fn, (1,), bbar_p, [bb_out] * 2)
    to_in = lambda t: _block_diag(t.reshape(N_SSM_BLK, 8, SSM_STATE, SSM_GROUP).transpose(0, 1, 3, 2))
    to_out = lambda t: _block_diag(t.reshape(N_SSM_BLK, 8, SSM_GROUP, SSM_STATE).transpose(0, 1, 3, 2))
    bblk_re, bblk_im = to_in(bbar_re), to_in(bbar_im)
    cblk_re, cblk_im = to_out(w['s5_c_re'][0]), to_out(w['s5_c_im'][0])
    a_row_re, a_row_im = abar_re.reshape(1, SSM_LANES), abar_im.reshape(1, SSM_LANES)

    d_row = row(w['s5_d'])
    s5_params = (bblk_re, bblk_im, a_row_re, a_row_im, cblk_re, cblk_im, d_row)
    s_re, s_im, yg = _s5_fwd(h2, *s5_params)
    z = _matmul("s5_glu", yg, glu_g, 'nn', b_cols=(0,))
    ln2_p = [_tile(h2, tm), _tile(z, tm), _full(ln_g[2:3]), _full(ln_b[2:3])]
    (h3,) = _fwd_call("ln2", _glu_ln_fn, (n_tile,), ln2_p, [tok_out(F32)])
    pre1, y3 = _mlp_fwd(1, h3, w1g, w2g)
    ln3_p = [_tile(h3, tm), _tile(y3, tm), _full(ln_g[3:4]), _full(ln_b[3:4])]
    (h4,) = _fwd_call("ln3", _res_ln_fn, (n_tile,), ln3_p, [tok_out(F32)])

    loss_acc, dh4 = _loss_call(h4, target, tm)

    ln_w = ['tile', 'tile_bf16', 'acc', 'acc']
    dh3_res, dy3, dg3, db3 = _bwd_call("ln3_bwd", _res_ln_fn, (n_tile,), ln3_p, [_tile(dh4, tm)], ln_w)
    dh3, dw1_1, dw2_1 = _mlp_bwd(1, h3, pre1, dy3, dh3_res, w1g, w2g)
    zero = exchange([('mlp_w1', 1, dw1_1), ('mlp_w2', 1, to_slots(dw2_1))])
    ln2_p[2] = _full(ln_g[2:3] + zero)
    dh2_res, dz, dg2, db2 = _bwd_call("ln2_bwd", _glu_ln_fn, (n_tile,), ln2_p, [_tile(dh3, tm)], ln_w)
    dyg = _matmul("s5_glu_dx", dz, glu_g, 'nt', b_cols=(0,))
    dw_glu = _matmul("s5_glu_dw", yg, dz, 'tn', out_cols=True)
    dh2, dbb_re, dbb_im, dcb_re, dcb_im, dd, dabar_re, dabar_im = _s5_bwd(dyg, h2, s_re, s_im, dh2_res, *s5_params)
    from_in = lambda t: _block_diag_t(t, SSM_GROUP, SSM_STATE).transpose(0, 1, 3, 2).reshape(SSM_LANES, SSM_GROUP)
    from_out = lambda t: _block_diag_t(t, SSM_STATE, SSM_GROUP).transpose(0, 1, 3, 2).reshape(1, SSM_GROUPS, SSM_GROUP, SSM_STATE)
    grads['s5_c_re'], grads['s5_c_im'] = from_out(dcb_re), from_out(dcb_im)
    grads['s5_d'] = dd
    dcoef_re, dcoef_im, db_re, db_im = _bwd_call(
        "s5_bbar_bwd", _s5_bbar_fn, (1,), bbar_p, [_full(from_in(dbb_re)), _full(from_in(dbb_im))], ['acc'] * 4)
    grads['s5_b_re'] = db_re.reshape(w['s5_b_re'].shape)
    grads['s5_b_im'] = db_im.reshape(w['s5_b_im'].shape)
    par_ct = [_full(dabar_re.reshape(gp)), _full(dabar_im.reshape(gp)), _full(dcoef_re.reshape(gp)), _full(dcoef_im.reshape(gp))]
    da_re, da_im, dlog_dt = _bwd_call("s5_param_bwd", s5_par_fn, (1,), par_p, par_ct, ['acc'] * 3)
    grads['s5_a_re'], grads['s5_a_im'], grads['s5_log_dt'] = da_re[None], da_im[None], dlog_dt.reshape(1, SSM_GROUPS)
    zero = exchange([(n, 0, grads[n]) for n in ('s5_a_re', 's5_a_im', 's5_log_dt', 's5_b_re', 's5_b_im', 's5_c_re', 's5_c_im')])
    ln1_p[2] = _full(ln_g[1:2] + zero)

    dh1_res, dy1, dg1, db1 = _bwd_call("ln1_bwd", _res_ln_fn, (n_tile,), ln1_p, [_tile(dh2, tm)], ln_w)
    dh1, dw1_0, dw2_0 = _mlp_bwd(0, h1, pre0, dy1, dh1_res, w1g, w2g)
    dx_res, dy0, dg0, db0 = _bwd_call("ln0_bwd", _res_ln_fn, (n_tile,), ln0_p, [_tile(dh1, tm)], ln_w)
    grads['ln_g'] = jnp.concatenate([dg0, dg1, dg2, dg3])
    grads['ln_b'] = jnp.concatenate([db0, db1, db2, db3])
    dog = _matmul("rwkv_o_dx", dy0, big['rw_wo'], 'nt')
    dw_o = to_slots(_matmul("rwkv_o_dw", og, dy0, 'tn'))
    zero = exchange([('s5_w_glu', 0, dw_glu), ('mlp_w1', 0, dw1_0), ('mlp_w2', 0, to_slots(dw2_0)), ('rw_wo', 0, dw_o)])
    gate_p[3] = _full(row(w['rw_lnx_g']) + zero)
    d_o, dbonus, dgate, grads['rw_lnx_g'], grads['rw_lnx_b'] = _bwd_call(
        "rwkv_gate_bwd", _gate_fn, (n_tile,), gate_p, [_tile(dog, tm)], ['tile', 'tile', 'tile', 'acc', 'acc'])
    dr, dlw, dk, dv, da, grads['rw_k_k'], grads['rw_k_a'], dr_k = _rwkv_scan_bwd(*scan_in, zs, d_o, dbonus)
    grads['rw_r_k'] = dr_k.reshape(w['rw_r_k'].shape)
    dw_r = to_slots(_matmul("rwkv_r_dw", xr, dr, 'tn'))
    dw_k = to_slots(_matmul("rwkv_k_dw", xk, dk, 'tn'))
    dw_v = to_slots(_matmul("rwkv_v_dw", xv, dv, 'tn'))
    done = ('ln_g', 'ln_b', 'rw_lnx_g', 'rw_lnx_b', 'rw_k_k', 'rw_k_a', 'rw_r_k')
    zero = exchange([('rw_wr', 0, dw_r), ('rw_wk', 0, dw_k), ('rw_wv', 0, dw_v)] + [(n, 0, grads[n]) for n in done])
    mix_p[3] = _full(row(w['rw_w0']) + zero)
    dxr = _matmul("rwkv_r_dx", dr, big['rw_wr'], 'nt')
    dxk = _matmul("rwkv_k_dx", dk, big['rw_wk'], 'nt')
    dxv = _matmul("rwkv_v_dx", dv, big['rw_wv'], 'nt')
    mix_ct = [_tile(t, tm) for t in (dxr, dxk, dxv, dlw, da, dgate)]
    halo_grad = ('tile', ((n_tile * 8, D_MODEL), (8, D_MODEL), lambda i: (i, 0)))
    res = _bwd_call("rwkv_mix_bwd", _mix_fn, (n_tile,), mix_p, mix_ct, ['tile', halo_grad] + ['acc'] * 9,
                    addends={0: _tile(dx_res, tm)})
    corr = res[1].reshape(n_tile, 8, D_MODEL)[1:, 7:8, :]
    corr = jnp.pad(corr, ((0, 1), (tm - 1, 0), (0, 0)))
    dx = (res[0].reshape(n_tile, tm, D_MODEL) + corr).reshape(n_tok, D_MODEL)
    (grads['rw_mu'], grads['rw_w0'], grads['rw_w1'], grads['rw_w2'], grads['rw_a0'], grads['rw_a1'], grads['rw_a2'],
     grads['rw_g1'], grads['rw_g2']) = [t[None] if t.shape[0] != 1 else t for t in res[2:]]
    return loss_acc[0, 0], dx, grads


def _full_shape(local_shape, axis):
    return local_shape[:axis] + (N_DEV * local_shape[axis],) + local_shape[axis + 1:]


def _split_shape(local_shape, axis):
    return local_shape[:axis] + (N_DEV, local_shape[axis]) + local_shape[axis + 1:]


def _mesh_pos():
    return lax.axis_index("x"), lax.axis_index("y"), lax.axis_index("c")


def _flip(v, f):
    return 1 - v if f else v


def _hbm_call(name, body, arrays, out_shapes, n_sems):
    n_t = len(arrays)
    hbm = pl.BlockSpec(memory_space=pl.ANY)
    return pl.pallas_call(
        body, name=name, out_shape=out_shapes, in_specs=[hbm] * n_t, out_specs=[hbm] * n_t,
        scratch_shapes=[pltpu.SemaphoreType.DMA((n_t, n_sems)), pltpu.SemaphoreType.DMA((n_t, n_sems)),
                        pltpu.SemaphoreType.DMA((n_t,))],
    )(*arrays)


def _all_gather(name, locals_):
    n_t = len(locals_)

    def body(*refs):
        x_refs, out_refs = refs[:n_t], refs[n_t:2 * n_t]
        send_sems, recv_sems, local_sems = refs[2 * n_t:]
        x, y, c = _mesh_pos()
        me, sibling = (x, y, c), (x, y, 1 - c)
        chips = [(1 - x, y), (x, 1 - y), (1 - x, 1 - y)]

        def copy(t, k, block, to, own=False):
            slot = out_refs[t].at[4 * block[0] + 2 * block[1] + block[2]]
            return pltpu.make_async_remote_copy(
                src_ref=x_refs[t] if own else slot, dst_ref=slot,
                send_sem=send_sems.at[t, k], recv_sem=recv_sems.at[t, k],
                device_id=to, device_id_type=pl.DeviceIdType.MESH)

        mine = [pltpu.make_async_copy(x_refs[t], out_refs[t].at[4 * x + 2 * y + c], local_sems.at[t]) for t in range(n_t)]
        for cp in mine:
            cp.start()
        sent = []
        for t in range(n_t):
            sent.append(copy(t, 0, me, sibling, own=True))
            sent += [copy(t, 1 + j, me, (*chip, c), own=True) for j, chip in enumerate(chips)]
        for cp in sent:
            cp.start()
        for j, chip in enumerate(chips):
            for t in range(n_t):
                copy(t, 1 + j, (*chip, c), me).wait_recv()
                passed = copy(t, 4 + j, (*chip, c), sibling)
                passed.start()
                sent.append(passed)
        for t in range(n_t):
            copy(t, 0, sibling, me).wait_recv()
            for j, chip in enumerate(chips):
                copy(t, 4 + j, (*chip, 1 - c), me).wait_recv()
        for cp in sent:
            cp.wait_send()
        for cp in mine:
            cp.wait()

    outs = [jax.ShapeDtypeStruct((N_DEV,) + a.shape, a.dtype) for a in locals_]
    return _hbm_call(name, body, locals_, outs, 7)


def _all_to_all(name, blocks):
    n_t = len(blocks)

    def body(*refs):
        x_refs, out_refs = refs[:n_t], refs[n_t:2 * n_t]
        send_sems, recv_sems, local_sems = refs[2 * n_t:]
        x, y, c = _mesh_pos()
        my_slot = 4 * x + 2 * y + c
        mine = [pltpu.make_async_copy(x_refs[t].at[my_slot], out_refs[t].at[my_slot], local_sems.at[t]) for t in range(n_t)]
        for cp in mine:
            cp.start()
        copies = []
        for k in range(1, N_DEV):
            peer = (_flip(x, k & 4), _flip(y, k & 2), _flip(c, k & 1))
            peer_slot = 4 * peer[0] + 2 * peer[1] + peer[2]
            for t in range(n_t):
                sems = dict(send_sem=send_sems.at[t, k - 1], recv_sem=recv_sems.at[t, k - 1],
                            device_id=peer, device_id_type=pl.DeviceIdType.MESH)
                cp = pltpu.make_async_remote_copy(src_ref=x_refs[t].at[peer_slot], dst_ref=out_refs[t].at[my_slot], **sems)
                cp.start()
                landed = pltpu.make_async_remote_copy(src_ref=x_refs[t].at[my_slot], dst_ref=out_refs[t].at[peer_slot], **sems)
                copies.append((cp, landed))
        for sent, landed in copies:
            landed.wait_recv()
            sent.wait_send()
        for cp in mine:
            cp.wait()

    outs = [jax.ShapeDtypeStruct(a.shape, a.dtype) for a in blocks]
    return _hbm_call(name, body, blocks, outs, 7)


_HBM = pl.BlockSpec(memory_space=pltpu.HBM)
_SEM = pl.BlockSpec(memory_space=pltpu.SEMAPHORE)
_EFFECT = pltpu.SideEffectType.DATAFLOW_SIDE_EFFECTING


def _peer_copies(x_refs, land_refs, send_sems, recv_sems, gather, mine):
    x, y, c = _mesh_pos()
    my_slot = 4 * x + 2 * y + c
    copies = []
    for k in range(1, N_DEV):
        peer = (_flip(x, k & 4), _flip(y, k & 2), _flip(c, k & 1))
        peer_slot = 4 * peer[0] + 2 * peer[1] + peer[2]
        for t, (x_ref, land) in enumerate(zip(x_refs, land_refs)):
            sem = t * (N_DEV - 1) + k - 1
            copies.append(pltpu.make_async_remote_copy(
                src_ref=x_ref if gather[t] else x_ref.at[peer_slot], dst_ref=land.at[my_slot if mine else peer_slot],
                send_sem=send_sems.at[sem], recv_sem=recv_sems.at[sem], device_id=peer, device_id_type=pl.DeviceIdType.MESH))
    return copies


def _push_start(name, arrays, gather):
    n_t = len(arrays)
    lands = [lax.empty((N_DEV,) + a.shape if whole else a.shape, a.dtype) for a, whole in zip(arrays, gather)]

    def body(*refs):
        x_refs, land_refs = refs[:n_t], refs[n_t:2 * n_t]
        send_sems, recv_sems = refs[2 * n_t], refs[2 * n_t + 1]
        for sent in _peer_copies(x_refs, land_refs, send_sems, recv_sems, gather, True):
            sent.start()
        refs[-1][...] = jnp.zeros((8, 128), F32)

    held = list(arrays) + lands
    sems = pltpu.SemaphoreType.DMA((n_t * (N_DEV - 1),))
    out = pl.pallas_call(
        body, name=name,
        out_shape=(sems, sems, *[pltpu.HBM(a.shape, a.dtype) for a in held], jax.ShapeDtypeStruct((8, 128), F32)),
        in_specs=[_HBM] * (2 * n_t),
        out_specs=(_SEM, _SEM, *[_HBM] * (2 * n_t), pl.BlockSpec(memory_space=pltpu.VMEM)),
        input_output_aliases={i: 2 + i for i in range(2 * n_t)},
        compiler_params=pltpu.CompilerParams(has_side_effects=_EFFECT),
    )(*[pltpu.with_memory_space_constraint(a, pltpu.HBM) for a in held])
    return dict(send=out[0], recv=out[1], held=out[2:2 + 2 * n_t], zero=out[-1][0, 0], gather=gather)


def _push_wait(name, started, after):
    held, gather = started['held'], started['gather']
    n_t = len(held) // 2

    def body(*refs):
        x_refs, land_refs = refs[:n_t], refs[n_t:2 * n_t]
        send_sems, recv_sems = refs[2 * n_t], refs[2 * n_t + 1]
        for sent in _peer_copies(x_refs, land_refs, send_sems, recv_sems, gather, True):
            sent.wait_send()
        for landed in _peer_copies(x_refs, land_refs, send_sems, recv_sems, gather, False):
            landed.wait_recv()

    out = pl.pallas_call(
        body, name=name,
        out_shape=[pltpu.HBM(a.shape, a.dtype) for a in held],
        in_specs=[_HBM] * (2 * n_t) + [_SEM, _SEM, pl.BlockSpec(memory_space=pl.ANY)],
        out_specs=[_HBM] * (2 * n_t),
        input_output_aliases={i: i for i in range(2 * n_t)},
        compiler_params=pltpu.CompilerParams(has_side_effects=_EFFECT),
    )(*held, started['send'], started['recv'], after)
    return out[:n_t], out[n_t:]


ADAM_BLOCK = 128 * 1024


def _adam_update(part, w_ref, m_ref, v_ref, g_ref, d_ref, nm_ref, nv_ref):
    g = part(0)
    for i in range(1, N_DEV):
        g = g + part(i)
    m_new = ADAM_B1 * m_ref[...] + (1.0 - ADAM_B1) * g
    v_new = ADAM_B2 * v_ref[...] + (1.0 - ADAM_B2) * (g * g)
    m_hat = m_new * (1.0 / (1.0 - ADAM_B1 ** ADAM_STEP))
    v_hat = v_new * (1.0 / (1.0 - ADAM_B2 ** ADAM_STEP))
    g_ref[...] = g
    nm_ref[...] = m_new
    nv_ref[...] = v_new
    d_ref[...] = -ADAM_LR * (m_hat / (jnp.sqrt(v_hat) + ADAM_EPS) + ADAM_WD * w_ref[...])


def _adam_many(name, items):
    n_t = len(items)
    n_in = [4 if own is None else 5 for *_, own in items]
    first = [sum(n_in[:t]) for t in range(n_t)]

    def body(*refs):
        ins, outs = refs[:sum(n_in)], refs[sum(n_in):]
        x, y, c = _mesh_pos()
        my_slot = 4 * x + 2 * y + c
        for t in range(n_t):
            s_ref, w_ref, m_ref, v_ref = ins[first[t]:first[t] + 4]
            if n_in[t] == 4:
                part = lambda i: s_ref[i]
            else:
                own_ref = ins[first[t] + 4]
                part = lambda i: jnp.where(my_slot == i, own_ref[...], s_ref[i])
            _adam_update(part, w_ref, m_ref, v_ref, *outs[4 * t:4 * t + 4])

    vmem = pl.BlockSpec(memory_space=pltpu.VMEM)
    out = pl.pallas_call(
        body, name=name, in_specs=[vmem] * sum(n_in), out_specs=[vmem] * (4 * n_t),
        out_shape=[jax.ShapeDtypeStruct(item[1].shape, F32) for item in items for _ in range(4)],
        compiler_params=pltpu.CompilerParams(vmem_limit_bytes=VMEM_LIMIT),
    )(*[a for item in items for a in item if a is not None])
    return [out[4 * t:4 * t + 4] for t in range(n_t)]


def _adam_call(name, slots, w, m, v, own=None):
    rows, width = w.shape
    tile = min(rows, ADAM_BLOCK // max(width, 128))

    def body(*refs):
        if own is None:
            s_ref, w_ref, m_ref, v_ref = refs[:4]
            part = lambda i: s_ref[i]
        else:
            s_ref, own_ref, w_ref, m_ref, v_ref = refs[:5]
            x, y, c = _mesh_pos()
            part = lambda i: jnp.where(4 * x + 2 * y + c == i, own_ref[...], s_ref[i])
        _adam_update(part, w_ref, m_ref, v_ref, *refs[-4:])

    blk = _spec((tile, width), lambda i: (i, 0))
    ops = [slots] + ([] if own is None else [own]) + [w, m, v]
    return pl.pallas_call(
        body, name=name, grid=(rows // tile,),
        in_specs=[_spec((N_DEV, tile, width), lambda i: (0, i, 0))] + [blk] * (len(ops) - 1),
        out_specs=[blk] * 4, out_shape=[jax.ShapeDtypeStruct(w.shape, F32)] * 4,
        compiler_params=_cparams(1),
    )(*ops)


def kernel(x, ln_g, ln_b, rw_mu, rw_w0, rw_w1, rw_w2, rw_a0, rw_a1, rw_a2, rw_g1, rw_g2, rw_k_k, rw_k_a, rw_r_k, rw_wr, rw_wk, rw_wv, rw_wo, rw_lnx_g, rw_lnx_b, s5_a_re, s5_a_im, s5_log_dt, s5_b_re, s5_b_im, s5_c_re, s5_c_im, s5_d, s5_w_glu, mlp_w1, mlp_w2, loss_target, m_ln_g, m_ln_b, m_rw_mu, m_rw_w0, m_rw_w1, m_rw_w2, m_rw_a0, m_rw_a1, m_rw_a2, m_rw_g1, m_rw_g2, m_rw_k_k, m_rw_k_a, m_rw_r_k, m_rw_wr, m_rw_wk, m_rw_wv, m_rw_wo, m_rw_lnx_g, m_rw_lnx_b, m_s5_a_re, m_s5_a_im, m_s5_log_dt, m_s5_b_re, m_s5_b_im, m_s5_c_re, m_s5_c_im, m_s5_d, m_s5_w_glu, m_mlp_w1, m_mlp_w2, v_ln_g, v_ln_b, v_rw_mu, v_rw_w0, v_rw_w1, v_rw_w2, v_rw_a0, v_rw_a1, v_rw_a2, v_rw_g1, v_rw_g2, v_rw_k_k, v_rw_k_a, v_rw_r_k, v_rw_wr, v_rw_wk, v_rw_wv, v_rw_wo, v_rw_lnx_g, v_rw_lnx_b, v_s5_a_re, v_s5_a_im, v_s5_log_dt, v_s5_b_re, v_s5_b_im, v_s5_c_re, v_s5_c_im, v_s5_d, v_s5_w_glu, v_mlp_w1, v_mlp_w2):
    given = dict(locals())
    local_w = {n: given[n] for n in WEIGHTS}
    local_m = {n: given["m_" + n] for n in WEIGHTS}
    local_v = {n: given["v_" + n] for n in WEIGHTS}
    small = [n for n in WEIGHTS if n in SHARDED and n not in BIG]

    my_slot = 4 * lax.axis_index("x") + 2 * lax.axis_index("y") + lax.axis_index("c")
    as_bf16 = {n: local_w[n].astype(BF16) for n in BIG}

    late_started = _push_start("weights_late_start", [as_bf16[n] for n in LATE], [True] * len(LATE))
    early = [n for n in BIG if n not in LATE]
    gathered = _all_gather("weights_all_gather", [as_bf16[n] for n in early] + [local_w[n] for n in small])
    early_gathered = dict(zip(early, gathered))
    full = dict(local_w)
    for n, blk in zip(small, gathered[len(early):]):
        full[n] = jnp.moveaxis(blk, 0, SHARDED[n]).reshape(_full_shape(local_w[n].shape, SHARDED[n]))
    full['rw_w0'] = full['rw_w0'] + late_started['zero']

    def late_weights(after):
        _, lands = _push_wait("weights_late_wait", late_started, after)
        own = lambda n: as_bf16[n][None]
        return {n: lax.dynamic_update_slice(land, own(n), (my_slot,) + (0,) * as_bf16[n].ndim) for n, land in zip(LATE, lands)}

    exchanges = []

    def exchange(entries):
        started = _push_start(f"grads_start_{len(exchanges)}", [t for _, _, t in entries], [n in REPLICATED for n, _, _ in entries])
        exchanges.append((entries, started))
        return started['zero']

    loss_sq, dx, grads = _local_step(x[0], loss_target[0], full, early_gathered, late_weights, exchange)
    loss = (0.5 / D_MODEL) * lax.psum(loss_sq, MESH_AXES)

    wide = {'s5_b_re': (SSM_LANES, SSM_GROUP), 's5_b_im': (SSM_LANES, SSM_GROUP),
            's5_c_re': (D_MODEL, SSM_STATE), 's5_c_im': (D_MODEL, SSM_STATE)}
    by_name, many = {}, []
    for e, (entries, started) in enumerate(exchanges):
        sources, lands = _push_wait(f"grads_wait_{e}", started, dx)
        for (n, i, _), src, slots in zip(entries, sources, lands):
            if n in wide:
                as_view = lambda t: t.reshape(wide[n])
                out = _adam_call(f"adamw_{n}", slots.reshape((N_DEV,) + wide[n]), as_view(local_w[n]), as_view(local_m[n]),
                                 as_view(local_v[n]), own=as_view(src))
                by_name[n] = [t.reshape(local_w[n].shape) for t in out]
            elif n in REPLICATED:
                many.append((n, (slots, local_w[n], local_m[n], local_v[n], src)))
            else:
                part = lambda d: d[n].reshape((-1,) + slots.shape[1:])[i]
                own = lax.dynamic_index_in_dim(src, my_slot, 0, keepdims=False)
                out = _adam_call(f"adamw_{n}_{i}", slots, part(local_w), part(local_m), part(local_v), own=own)
                by_name.setdefault(n, {})[i] = out
    for n in BIG:
        per_layer = [by_name[n][i] for i in sorted(by_name[n])]
        by_name[n] = [jnp.stack(ts).reshape(local_w[n].shape) for ts in zip(*per_layer)]
    split = lambda n: jnp.moveaxis(grads[n].reshape(_split_shape(local_w[n].shape, SHARDED[n])), SHARDED[n], 0)
    for n, slots in zip(small, _all_to_all("grads_all_to_all", [split(n) for n in small])):
        many.append((n, (slots, local_w[n], local_m[n], local_v[n], None)))
    last = [n for n in REPLICATED if n not in by_name and n not in dict(many)]
    for n, slots in zip(last, _all_gather("grads_all_gather", [grads[n] for n in last])):
        many.append((n, (slots, local_w[n], local_m[n], local_v[n], None)))
    for (n, _), out in zip(many, _adam_many("adamw_small", [item for _, item in many])):
        by_name[n] = out
    results = [by_name[n][k] for k in range(4) for n in WEIGHTS]
    return (loss, dx[None], *results)
```

```python
import functools
import math

import jax
import jax.numpy as jnp
from jax import lax
from jax.experimental import pallas as pl
from jax.experimental.pallas import tpu as pltpu

F32 = jnp.float32
BF16 = jnp.bfloat16
HI = lax.Precision.HIGHEST

D_MODEL = 1024
HEAD = 64
PAIR = 2 * HEAD
N_PAIR = D_MODEL // PAIR
CHUNK = 64
GN_EPS = 64e-5
LN_EPS = 1e-5
SSM_GROUP = 16
SSM_STATE = 64
SSM_GROUPS = D_MODEL // SSM_GROUP
SSM_LANES = SSM_GROUPS * SSM_STATE
SSM_BLK_IN = 128
SSM_BLK_ST = 512
N_SSM_BLK = D_MODEL // SSM_BLK_IN
DEPTH = 2
DN_ALPHA = (2.0 * DEPTH) ** 0.25
ADAM_LR, ADAM_B1, ADAM_B2, ADAM_EPS, ADAM_WD, ADAM_STEP = 0.001, 0.9, 0.999, 1e-08, 0.01, 10
N_DEV = 8
MESH_AXES = ("x", "y", "c")
VMEM_LIMIT = 56 * 1024 * 1024
TOKEN_TILE = 256

SHARDED = {
    "rw_mu": 2, "rw_w1": 1, "rw_w2": 2, "rw_a1": 1, "rw_a2": 2, "rw_g1": 1, "rw_g2": 2,
    "rw_wr": 1, "rw_wk": 1, "rw_wv": 1, "rw_wo": 1, "s5_d": 1, "s5_w_glu": 2, "mlp_w1": 2, "mlp_w2": 1,
}
WEIGHTS = ['ln_g', 'ln_b', 'rw_mu', 'rw_w0', 'rw_w1', 'rw_w2', 'rw_a0', 'rw_a1', 'rw_a2', 'rw_g1', 'rw_g2', 'rw_k_k',
           'rw_k_a', 'rw_r_k', 'rw_wr', 'rw_wk', 'rw_wv', 'rw_wo', 'rw_lnx_g', 'rw_lnx_b', 's5_a_re', 's5_a_im',
           's5_log_dt', 's5_b_re', 's5_b_im', 's5_c_re', 's5_c_im', 's5_d', 's5_w_glu', 'mlp_w1', 'mlp_w2']
REPLICATED = [n for n in WEIGHTS if n not in SHARDED]
BIG = ['rw_wr', 'rw_wk', 'rw_wv', 'rw_wo', 's5_w_glu', 'mlp_w1', 'mlp_w2']
LATE = ['s5_w_glu', 'mlp_w1', 'mlp_w2']
GRAD_DTYPE = BF16


def _cparams(n_grid):
    return pltpu.CompilerParams(dimension_semantics=("arbitrary",) * n_grid, vmem_limit_bytes=VMEM_LIMIT)


@jax.custom_vjp
def _mm(x, w):
    return jnp.dot(x.astype(BF16), w.astype(BF16), preferred_element_type=F32)


def _mm_fwd(x, w):
    return _mm(x, w), (x, w)


def _mm_bwd(res, dy):
    x, w = res
    dyb = dy.astype(BF16)
    dx = lax.dot_general(dyb, w.astype(BF16), (((1,), (1,)), ((), ())), preferred_element_type=F32)
    dw = lax.dot_general(x.astype(BF16), dyb, (((0,), (0,)), ((), ())), preferred_element_type=F32)
    return dx, dw


_mm.defvjp(_mm_fwd, _mm_bwd)


_DOT_DIMS = {'nn': (((1,), (0,)), ((), ())), 'nt': (((1,), (1,)), ((), ())), 'tn': (((0,), (0,)), ((), ()))}
_BATCH_DOT_DIMS = {'nn': (((2,), (1,)), ((0,), (0,))), 'nt': (((2,), (2,)), ((0,), (0,))), 'tn': (((1,), (1,)), ((0,), (0,)))}
CHUNK_PAIRS = 8
CHUNK_PASSES = 1


def _split_bf16(a):
    hi = a.astype(BF16)
    return hi, (a - hi.astype(F32)).astype(BF16)


def _dot_passes(a, b, mode, passes=None):
    dims = (_DOT_DIMS if a.ndim == 2 else _BATCH_DOT_DIMS)[mode]
    dot = lambda p, q: lax.dot_general(p, q, dims, preferred_element_type=F32)
    if (CHUNK_PASSES if passes is None else passes) == 1:
        return dot(a.astype(BF16), b.astype(BF16))
    (ah, al), (bh, bl) = _split_bf16(a), _split_bf16(b)
    return dot(ah, bh) + (dot(ah, bl) + dot(al, bh))


@functools.partial(jax.custom_vjp, nondiff_argnums=(2,))
def _pdot(a, b, mode):
    return _dot_passes(a, b, mode)


def _pdot_fwd(a, b, mode):
    return _dot_passes(a, b, mode), (a, b)


def _pdot_bwd(mode, res, dy):
    a, b = res
    if mode == 'nn':
        return _dot_passes(dy, b, 'nt'), _dot_passes(a, dy, 'tn')
    if mode == 'nt':
        return _dot_passes(dy, b, 'nn'), _dot_passes(dy, a, 'tn')
    return _dot_passes(b, dy, 'nt'), _dot_passes(a, dy, 'nn')


_pdot.defvjp(_pdot_fwd, _pdot_bwd)


def _tri_sum(x, upper):
    nb, n = x.shape[0], x.shape[1]
    ti = lax.broadcasted_iota(jnp.int32, (nb, n, n), 1)
    tj = lax.broadcasted_iota(jnp.int32, (nb, n, n), 2)
    tri = ((ti <= tj) if upper else (ti >= tj)).astype(BF16)
    hi = x.astype(BF16)
    rest = x - hi.astype(F32)
    mid = rest.astype(BF16)
    lo = (rest - mid.astype(F32)).astype(BF16)
    dot = lambda q: lax.dot_general(tri, q, _BATCH_DOT_DIMS['nn'], preferred_element_type=F32)
    return dot(hi) + (dot(mid) + dot(lo))


@jax.custom_vjp
def _cumsum_rows(x):
    return _tri_sum(x, False)


_cumsum_rows.defvjp(lambda x: (_tri_sum(x, False), None), lambda _, dy: (_tri_sum(dy, True),))


@functools.partial(jax.custom_vjp, nondiff_argnums=(1,))
def _roll_rows(x, shift):
    return pltpu.roll(x, shift, 0)


def _roll_rows_fwd(x, shift):
    return pltpu.roll(x, shift, 0), None


def _roll_rows_bwd(shift, _, dy):
    n = dy.shape[0]
    return (pltpu.roll(dy, (n - shift) % n, 0),)


_roll_rows.defvjp(_roll_rows_fwd, _roll_rows_bwd)


def _shift_down(x, first_row):
    row = lax.broadcasted_iota(jnp.int32, x.shape, 0)
    return jnp.where(row == 0, first_row, _roll_rows(x, 1))


def _sigmoid(x):
    return 1.0 / (1.0 + jnp.exp(-x))


def _softplus(x):
    return jnp.maximum(x, 0.0) + jnp.log(1.0 + jnp.exp(-jnp.abs(x)))


def _gelu(x):
    return 0.5 * x * (1.0 + jnp.tanh(math.sqrt(2.0 / math.pi) * (x + 0.044715 * (x * x * x))))


def _layer_norm(x, g, b):
    mu = jnp.mean(x, axis=-1, keepdims=True)
    xc = x - mu
    var = jnp.mean(xc * xc, axis=-1, keepdims=True)
    return xc * lax.rsqrt(var + LN_EPS) * g + b


def _spec(block, index_map):
    return pl.BlockSpec(block, index_map)


def _tile(arr, tm):
    return (arr, (tm, arr.shape[1]), lambda i: (i, 0))


def _full(arr):
    nd = arr.ndim
    return (arr, arr.shape, lambda *_: (0,) * nd)


def _fwd_call(name, fn, grid, prims, outs):
    n_in = len(prims)

    def body(*refs):
        vals = [r[...] for r in refs[:n_in]]
        vals = [v.astype(F32) if v.dtype != F32 else v for v in vals]
        res = fn(*vals)
        for o, r in zip(refs[n_in:], res):
            o[...] = r.astype(o.dtype)

    return pl.pallas_call(
        body, name=name, grid=grid,
        in_specs=[_spec(b, m) for _, b, m in prims],
        out_specs=[_spec(b, m) for _, _, b, m in outs],
        out_shape=[jax.ShapeDtypeStruct(s, d) for s, d, _, _ in outs],
        compiler_params=_cparams(len(grid)),
    )(*[a for a, _, _ in prims])


def _bwd_call(name, fn, grid, prims, cts, wants, addends=None):
    addends = addends or {}
    n_in, n_ct = len(prims), len(cts)
    out_place = {}
    wants, out_dtype = list(wants), {}
    for i, w in enumerate(wants):
        if isinstance(w, tuple):
            wants[i], out_place[i] = w
        elif w == 'tile_bf16':
            wants[i], out_dtype[i] = 'tile', BF16
    diff = [i for i, w in enumerate(wants) if w]
    add_idx = [i for i in diff if i in addends]
    last_axis = len(grid) - 1

    def body(*refs):
        vals = [r[...] for r in refs[:n_in]]
        vals = [v.astype(F32) if v.dtype != F32 else v for v in vals]
        ct_vals = [r[...] for r in refs[n_in:n_in + n_ct]]
        ct_vals = [v.astype(F32) if v.dtype != F32 else v for v in ct_vals]
        add_refs = dict(zip(add_idx, refs[n_in + n_ct:n_in + n_ct + len(add_idx)]))
        out_refs = refs[n_in + n_ct + len(add_idx):]

        def f(*dargs):
            full = list(vals)
            for i, a in zip(diff, dargs):
                full[i] = a
            return tuple(fn(*full))

        _, vjp = jax.vjp(f, *[vals[i] for i in diff])
        grads = vjp(tuple(ct_vals))
        first = pl.program_id(last_axis) == 0
        for o, g, i in zip(out_refs, grads, diff):
            if wants[i] == 'tile':
                if i in add_refs:
                    g = g + add_refs[i][...]
                o[...] = g.astype(o.dtype)
            else:
                @pl.when(first)
                def _(o=o):
                    o[...] = jnp.zeros(o.shape, o.dtype)
                o[...] += g

    ins = list(prims) + list(cts) + [addends[i] for i in add_idx]
    places = [out_place.get(i, (prims[i][0].shape, prims[i][1], prims[i][2])) for i in diff]
    return pl.pallas_call(
        body, name=name, grid=grid,
        in_specs=[_spec(b, m) for _, b, m in ins],
        out_specs=[_spec(b, m) for _, b, m in places],
        out_shape=[jax.ShapeDtypeStruct(s, out_dtype.get(i, F32)) for i, (s, _, _) in zip(diff, places)],
        compiler_params=_cparams(len(grid)),
    )(*[a for a, _, _ in ins])


def _matmul(name, a, b, mode, *, out_dtype=F32, lhs_fn=None, epi=None, extra=None, tm=1024, tn=1024, tk=1024,
            b_cols=None, b_rows=None, out_cols=False):
    if mode == 'tn':
        kdim, m = a.shape
    else:
        m, kdim = a.shape
    if b_rows is not None:
        loc, cols = b.shape[-2], b.shape[-1]
        if mode == 'nn':
            n, k_shards = cols, max(1, tk // loc)
            tk = k_shards * loc
        else:
            n, tn = N_DEV * loc, loc
    elif b_cols is None:
        n = b.shape[0] if mode == 'nt' else b.shape[1]
    else:
        rows, loc = b.shape[-2], b.shape[-1]
        if mode == 'nn':
            n, n_shards = N_DEV * loc, max(1, tn // loc)
            tn = n_shards * loc
        else:
            n, k_shards = rows, max(1, tk // loc)
            tk = k_shards * loc
    if out_cols:
        o_shards = max(1, min(tn, n) // (n // N_DEV))
        tn = o_shards * (n // N_DEV)
    tm, tn, tk = min(tm, m), min(tn, n), min(tk, kdim)
    nk = kdim // tk
    grid = (m // tm, n // tn, nk)
    a_spec = _spec((tk, tm), lambda i, j, k: (k, i)) if mode == 'tn' else _spec((tm, tk), lambda i, j, k: (i, k))
    if b_rows is not None:
        lead = tuple(b_rows)
        skip = (None,) * (1 + len(lead))
        if mode == 'nn':
            b_spec = _spec((k_shards,) + skip[1:] + (tk // k_shards, tn), lambda i, j, k: (k,) + lead + (0, j))
        else:
            b_spec = _spec(skip + (tn, tk), lambda i, j, k: (j,) + lead + (0, k))
    elif b_cols is None:
        b_spec = _spec((tn, tk), lambda i, j, k: (j, k)) if mode == 'nt' else _spec((tk, tn), lambda i, j, k: (k, j))
    else:
        lead = tuple(b_cols)
        skip = (None,) * (1 + len(lead))
        if mode == 'nn':
            b_spec = _spec((n_shards,) + skip[1:] + (tk, tn // n_shards), lambda i, j, k: (j,) + lead + (k, 0))
        else:
            b_spec = _spec((k_shards,) + skip[1:] + (tn, tk // k_shards), lambda i, j, k: (k,) + lead + (j, 0))
    o_spec = _spec((tm, tn), lambda i, j, k: (i, j))
    if out_cols:
        o_place, o_shape = _spec((o_shards, tm, tn // o_shards), lambda i, j, k: (j, i, 0)), (N_DEV, m, n // N_DEV)
    else:
        o_place, o_shape = o_spec, (m, n)
    dims = _DOT_DIMS[mode]

    def body(*refs):
        if extra is None:
            a_ref, b_ref, o_ref, acc = refs
            x_ref = None
        else:
            a_ref, b_ref, x_ref, o_ref, acc = refs
        k = pl.program_id(2)
        av = a_ref[...]
        if lhs_fn is not None:
            av = lhs_fn(av.astype(F32))
        bv = b_ref[...]
        if bv.ndim == 3:
            bv = bv.reshape(tk, tn) if b_rows is not None else jnp.concatenate([bv[g] for g in range(bv.shape[0])], axis=1)
        part = lax.dot_general(av.astype(BF16), bv.astype(BF16), dims, preferred_element_type=F32)

        def finish(r):
            if epi is not None:
                r = epi(r, x_ref[...])
            if out_cols:
                width = tn // o_shards
                for g in range(o_shards):
                    o_ref[g] = r[:, g * width:(g + 1) * width].astype(o_ref.dtype)
            else:
                o_ref[...] = r.astype(o_ref.dtype)

        if nk == 1:
            finish(part)
            return

        @pl.when(k == 0)
        def _():
            acc[...] = part

        @pl.when((k > 0) & (k < nk - 1))
        def _():
            acc[...] += part

        @pl.when(k == nk - 1)
        def _():
            finish(acc[...] + part)

    ops, specs = [a, b], [a_spec, b_spec]
    if extra is not None:
        ops.append(extra)
        specs.append(o_spec)
    return pl.pallas_call(
        body, name=name, grid=grid, in_specs=specs, out_specs=o_place,
        out_shape=jax.ShapeDtypeStruct(o_shape, out_dtype),
        scratch_shapes=[pltpu.VMEM((tm, tn), F32)],
        compiler_params=_cparams(3),
    )(*ops)


def _relu2(x):
    r = jnp.maximum(x, 0.0)
    return r * r


def _mix_fn(x, halo, mu, w0, w1, w2, a0, a1, a2, g1, g2):
    prev_row = jnp.where(pl.program_id(0) == 0, 0.0, halo[7:8, :])
    xx = _shift_down(x, prev_row) - x
    xr, xw, xk, xv, xa, xg = (x + xx * mu[i:i + 1, :] for i in range(6))
    w_pre = w0 + _mm(jnp.tanh(_mm(xw, w1)), w2)
    log_decay = -jnp.exp(-_softplus(-w_pre) - 0.5)
    a = _sigmoid(a0 + _mm(_mm(xa, a1), a2))
    g = _mm(_sigmoid(_mm(xg, g1)), g2)
    return xr, xk, xv, log_decay, a, g


def _chunk_fn(r, lw, k, v, a, k_k, k_a, r_k):
    n, n_pair = r.shape[0], r.shape[1] // PAIR
    to_batch = lambda t: jnp.concatenate([t[None, :, p * PAIR:(p + 1) * PAIR] for p in range(n_pair)], axis=0)
    from_batch = lambda t: jnp.concatenate([t[p] for p in range(n_pair)], axis=1)
    r, lw, k, v, a, k_k, k_a, r_k = (to_batch(t) for t in (r, lw, k, v, a, k_k, k_a, r_k))
    lane = lax.broadcasted_iota(jnp.int32, (1, 1, PAIR), 2)
    m0 = lane < HEAD

    def head_sum(t):
        s0 = jnp.sum(jnp.where(m0, t, 0.0), axis=2, keepdims=True)
        s1 = jnp.sum(jnp.where(m0, 0.0, t), axis=2, keepdims=True)
        return jnp.where(m0, s0, s1)

    kk = k * k_k
    kk = kk / jnp.maximum(jnp.sqrt(head_sum(kk * kk)), 1e-12)
    k2 = k * (1.0 + (a - 1.0) * k_a)
    aa, bb = -kk, kk * a
    bonus = head_sum(r * k2 * r_k) * v

    cum = _cumsum_rows(lw)
    p_in, p_ex, p_inv = jnp.exp(cum), jnp.exp(cum - lw), jnp.exp(-cum)
    at, rt, kt, bt = aa * p_ex, r * p_in, k2 * p_inv, bb * p_inv

    def stack_masked(t):
        return jnp.concatenate([jnp.where(m0, t, 0.0), jnp.where(m0, 0.0, t)], axis=1)

    def unstack_sum(t):
        return t[:, :n] + t[:, n:]

    def unstack_select(t):
        return jnp.where(m0, t[:, :n], t[:, n:])

    ti = lax.broadcasted_iota(jnp.int32, (1, 2 * n, 2 * n), 1)
    tj = lax.broadcasted_iota(jnp.int32, (1, 2 * n, 2 * n), 2)
    same = (ti >= n) == (tj >= n)
    incl, strict = same & (ti >= tj), same & (ti > tj)
    eye_n = (ti == tj).astype(F32)
    m = 2 * n
    rows = lambda *ts: jnp.concatenate(ts, axis=1)
    cols = lambda *ts: jnp.concatenate(ts, axis=2)
    at_s, rt_s, kt_s, bt_s = stack_masked(at), stack_masked(rt), stack_masked(kt), stack_masked(bt)
    v_s, zeros = rows(v, v), jnp.zeros_like(at_s)
    bk_s = rows(bt_s, kt_s)
    gram = _pdot(rows(at_s, rt_s), bk_s, 'nt')
    a_ab = jnp.where(strict, gram[:, :m, :m], 0.0)
    a_ak = jnp.where(strict, gram[:, :m, m:], 0.0)
    a_rb = jnp.where(incl, gram[:, m:, :m], 0.0)
    a_rk = jnp.where(incl, gram[:, m:, m:], 0.0)
    pw, inv = _pdot(a_ab, a_ab, 'nn'), eye_n + a_ab
    for _ in range(int(math.log2(n)) - 2):
        both = _pdot(pw, cols(pw, inv), 'nn')
        pw, inv = both[:, :, :m], inv + both[:, :, m:]
    inv = inv + _pdot(pw, inv, 'nn')
    wy_s = _pdot(inv, cols(at_s, _pdot(a_ak, v_s, 'nn')), 'nn')
    q_oi = _pdot(cols(a_rb, a_rk), rows(wy_s, cols(zeros, v_s)), 'nn')
    q_all = unstack_sum(rt_s + q_oi[:, :, :PAIR])
    oi_all = unstack_select(q_oi[:, :, PAIR:])
    own_head = (lax.broadcasted_iota(jnp.int32, (1, m, PAIR), 1) >= n) == (lax.broadcasted_iota(jnp.int32, (1, m, PAIR), 2) >= HEAD)
    wy_m = cols(wy_s[:, :, :PAIR], jnp.where(own_head, wy_s[:, :, PAIR:], 0.0))
    mm_nn = _pdot(bk_s, rows(wy_m, cols(zeros, stack_masked(v))), 'tn')
    mm, nn = mm_nn[:, :, :PAIR], mm_nn[:, :, PAIR:]
    ei = lax.broadcasted_iota(jnp.int32, (1, PAIR, PAIR), 1)
    ej = lax.broadcasted_iota(jnp.int32, (1, PAIR, PAIR), 2)
    eye_p = (ei == ej).astype(F32)
    decay_col = jnp.sum(eye_p * p_in[:, n - 1:n, :], axis=2, keepdims=True)
    return ((decay_col * (eye_p + mm))[None], (decay_col * nn)[None], from_batch(q_all), from_batch(oi_all),
            from_batch(bonus))


def _gate_fn(o, bonus, g, lnx_g, lnx_b):
    lane = lax.broadcasted_iota(jnp.int32, (1, PAIR), 1)
    m0 = lane < HEAD

    def head_mean(t):
        s0 = jnp.sum(jnp.where(m0, t, 0.0), axis=1, keepdims=True)
        s1 = jnp.sum(jnp.where(m0, 0.0, t), axis=1, keepdims=True)
        return jnp.where(m0, s0, s1) * (1.0 / HEAD)

    outs = []
    for p in range(N_PAIR):
        sl = slice(p * PAIR, (p + 1) * PAIR)
        oc = o[:, sl] - head_mean(o[:, sl])
        on = oc * lax.rsqrt(head_mean(oc * oc) + GN_EPS) * lnx_g[:, sl] + lnx_b[:, sl]
        outs.append((on + bonus[:, sl]) * g[:, sl])
    return (jnp.concatenate(outs, axis=1),)


def _res_ln_fn(h, y, g, b):
    return (_layer_norm(DN_ALPHA * h + y, g, b),)


def _glu_ln_fn(h, z, g, b):
    mix = z[:, :D_MODEL] * _sigmoid(z[:, D_MODEL:])
    return (_layer_norm(DN_ALPHA * h + mix, g, b),)


def _s5_param_fn(a_re, a_im, log_dt, b_re, b_im):
    dt = jnp.exp(log_dt)
    lam_re = jnp.minimum(a_re, -1e-4)
    lam_im = a_im
    mag = jnp.exp(dt * lam_re)
    abar_re = mag * jnp.cos(dt * lam_im)
    abar_im = mag * jnp.sin(dt * lam_im)
    den = lam_re * lam_re + lam_im * lam_im
    nr, ni = abar_re - 1.0, abar_im
    coef_re = (nr * lam_re + ni * lam_im) / den
    coef_im = (ni * lam_re - nr * lam_im) / den
    return abar_re, abar_im, coef_re, coef_im


def _s5_bbar_fn(coef_re, coef_im, b_re, b_im):
    return coef_re * b_re - coef_im * b_im, coef_re * b_im + coef_im * b_re


def _s5_in_fn(u, b_re, b_im):
    return _pdot(u, b_re[0], 'nn'), _pdot(u, b_im[0], 'nn')


def _s5_out_fn(s_re, s_im, u, c_re, c_im, d):
    y = _pdot(s_re, c_re[0], 'nn') - _pdot(s_im, c_im[0], 'nn') + u * d
    return (_gelu(y),)


def _pairs_to_batch(t):
    return jnp.concatenate([t[None, :, p * PAIR:(p + 1) * PAIR] for p in range(N_PAIR)], axis=0)


def _batch_to_pairs(t):
    return jnp.concatenate([t[p] for p in range(N_PAIR)], axis=1)


STATE_PASSES = 3


def _rwkv_scan_fwd(r, lw, k, v, a, k_k, k_a, r_k):
    n_tok = r.shape[0]
    n_chunk = n_tok // CHUNK

    def body(r_ref, lw_ref, k_ref, v_ref, a_ref, kk_ref, ka_ref, rk_ref, o_ref, bonus_ref, zs_ref, z):
        @pl.when(pl.program_id(0) == 0)
        def _():
            z[...] = jnp.zeros(z.shape, F32)

        m_all, n_all, q, oi, bonus = _chunk_fn(r_ref[...], lw_ref[...], k_ref[...], v_ref[...], a_ref[...],
                                               kk_ref[...], ka_ref[...], rk_ref[...])
        zb = z[...]
        zs_ref[0] = zb
        bonus_ref[...] = bonus
        o_ref[...] = _batch_to_pairs(_dot_passes(_pairs_to_batch(q), zb, 'nn', STATE_PASSES)) + oi
        z[...] = _dot_passes(m_all[0], zb, 'nn', STATE_PASSES) + n_all[0]

    tok = _spec((CHUNK, D_MODEL), lambda c: (c, 0))
    par = _spec((1, D_MODEL), lambda c: (0, 0))
    st = _spec((1, N_PAIR, PAIR, PAIR), lambda c: (c, 0, 0, 0))
    return pl.pallas_call(
        body, name="rwkv_scan_fwd", grid=(n_chunk,),
        in_specs=[tok] * 5 + [par] * 3, out_specs=[tok, tok, st],
        out_shape=[jax.ShapeDtypeStruct((n_tok, D_MODEL), F32)] * 2 + [jax.ShapeDtypeStruct((n_chunk, N_PAIR, PAIR, PAIR), F32)],
        scratch_shapes=[pltpu.VMEM((N_PAIR, PAIR, PAIR), F32)],
        compiler_params=_cparams(1),
    )(r, lw, k, v, a, k_k, k_a, r_k)


def _rwkv_scan_bwd(r, lw, k, v, a, k_k, k_a, r_k, zs, d_o, d_bonus):
    n_tok = r.shape[0]
    n_chunk = n_tok // CHUNK

    def body(r_ref, lw_ref, k_ref, v_ref, a_ref, kk_ref, ka_ref, rk_ref, zs_ref, do_ref, db_ref,
             dr_ref, dlw_ref, dk_ref, dv_ref, da_ref, dkk_ref, dka_ref, drk_ref, g):
        sums = (dkk_ref, dka_ref, drk_ref)

        @pl.when(pl.program_id(0) == 0)
        def _():
            g[...] = jnp.zeros(g.shape, F32)
            for s_ref in sums:
                s_ref[...] = jnp.zeros(s_ref.shape, F32)

        prim = (r_ref[...], lw_ref[...], k_ref[...], v_ref[...], a_ref[...], kk_ref[...], ka_ref[...], rk_ref[...])
        (m_all, _, q, _, _), chunk_vjp = jax.vjp(_chunk_fn, *prim)
        gb, zb, d_out = g[...], zs_ref[0], do_ref[...]
        dob = _pairs_to_batch(d_out)
        d_m = _dot_passes(gb, zb, 'nt', STATE_PASSES)
        d_q = _batch_to_pairs(_dot_passes(dob, zb, 'nt', STATE_PASSES))
        grads = chunk_vjp((d_m[None], gb[None], d_q, d_out, db_ref[...]))
        for out_ref, val in zip((dr_ref, dlw_ref, dk_ref, dv_ref, da_ref), grads[:5]):
            out_ref[...] = val.astype(out_ref.dtype)
        for s_ref, val in zip(sums, grads[5:]):
            s_ref[...] += val
        same_head = (lax.broadcasted_iota(jnp.int32, (1, PAIR, PAIR), 1) // HEAD
                     == lax.broadcasted_iota(jnp.int32, (1, PAIR, PAIR), 2) // HEAD)
        g[...] = (_dot_passes(m_all[0], gb, 'tn', STATE_PASSES)
                  + jnp.where(same_head, _dot_passes(_pairs_to_batch(q), dob, 'tn', STATE_PASSES), 0.0))

    tok = _spec((CHUNK, D_MODEL), lambda c: (n_chunk - 1 - c, 0))
    par = _spec((1, D_MODEL), lambda c: (0, 0))
    st = _spec((1, N_PAIR, PAIR, PAIR), lambda c: (n_chunk - 1 - c, 0, 0, 0))
    tok_shape = lambda dt: jax.ShapeDtypeStruct((n_tok, D_MODEL), dt)
    return pl.pallas_call(
        body, name="rwkv_scan_bwd", grid=(n_chunk,),
        in_specs=[tok] * 5 + [par] * 3 + [st, tok, tok], out_specs=[tok] * 5 + [par] * 3,
        out_shape=[tok_shape(BF16), tok_shape(F32), tok_shape(BF16), tok_shape(BF16), tok_shape(F32)]
        + [jax.ShapeDtypeStruct((1, D_MODEL), F32)] * 3,
        scratch_shapes=[pltpu.VMEM((N_PAIR, PAIR, PAIR), F32)],
        compiler_params=_cparams(1),
    )(r, lw, k, v, a, k_k, k_a, r_k, zs, d_o, d_bonus)


S5_TIME_TILE = 512
S5_GROUP = 8


def _scan_rows(re_ref, im_ref, a1, carry, reverse):
    lb = re_ref.shape[1]
    grp, n_grp = S5_GROUP, re_ref.shape[0] // S5_GROUP

    def cmul(xr, xi, yr, yi):
        return xr * yr - xi * yi, xr * yi + xi * yr

    a2 = cmul(*a1, *a1)
    a4 = cmul(*a2, *a2)
    a8 = cmul(*a4, *a4)
    row = lax.broadcasted_iota(jnp.int32, (grp, lb), 0)
    expo = (grp - row) if reverse else (row + 1)
    pw = (jnp.ones((grp, lb), F32), jnp.zeros((grp, lb), F32))
    for bit, ap in ((1, a1), (2, a2), (4, a4), (8, a8)):
        nxt = cmul(*pw, *ap)
        sel = (expo & bit) != 0
        pw = (jnp.where(sel, nxt[0], pw[0]), jnp.where(sel, nxt[1], pw[1]))

    steps = []
    for d, ad in ((1, a1), (2, a2), (4, a4)):
        keep = (row < grp - d) if reverse else (row >= d)
        steps.append(((grp - d) if reverse else d, jnp.where(keep, ad[0], 0.0), jnp.where(keep, ad[1], 0.0)))

    def group(i, c):
        gi = (n_grp - 1 - i) if reverse else i
        rows = pl.ds(pl.multiple_of(gi * grp, grp), grp)
        xr, xi = re_ref[rows, :], im_ref[rows, :]
        for shift, mr, mi in steps:
            pr, pi = cmul(mr, mi, pltpu.roll(xr, shift, 0), pltpu.roll(xi, shift, 0))
            xr, xi = xr + pr, xi + pi
        cr, ci = cmul(*pw, c[0], c[1])
        xr, xi = xr + cr, xi + ci
        re_ref[rows, :] = xr
        im_ref[rows, :] = xi
        edge = slice(0, 1) if reverse else slice(grp - 1, grp)
        return xr[edge, :], xi[edge, :]

    return lax.fori_loop(0, n_grp, group, carry)


def _s5_fwd(u, b_re, b_im, a_re, a_im, c_re, c_im, d_row):
    n_tok = u.shape[0]
    tt = min(S5_TIME_TILE, n_tok)

    def body(u_ref, bre_ref, bim_ref, are_ref, aim_ref, cre_ref, cim_ref, d_ref, sre_ref, sim_ref, yg_ref, carry):
        @pl.when(pl.program_id(1) == 0)
        def _():
            carry[...] = jnp.zeros(carry.shape, F32)

        uv = u_ref[...]
        sre_ref[...], sim_ref[...] = _s5_in_fn(uv, bre_ref[...], bim_ref[...])
        c = _scan_rows(sre_ref, sim_ref, (are_ref[...], aim_ref[...]), (carry[0:1, :], carry[1:2, :]), False)
        carry[0:1, :] = c[0]
        carry[1:2, :] = c[1]
        (yg,) = _s5_out_fn(sre_ref[...], sim_ref[...], uv, cre_ref[...], cim_ref[...], d_ref[...])
        yg_ref[...] = yg.astype(yg_ref.dtype)

    u_blk = _spec((tt, SSM_BLK_IN), lambda l, t: (t, l))
    s_blk = _spec((tt, SSM_BLK_ST), lambda l, t: (t, l))
    blk3 = lambda arr: _spec((1,) + arr.shape[1:], lambda l, t: (l, 0, 0))
    par = lambda width: _spec((1, width), lambda l, t: (0, l))
    return pl.pallas_call(
        body, name="s5_fwd", grid=(N_SSM_BLK, n_tok // tt),
        in_specs=[u_blk, blk3(b_re), blk3(b_im), par(SSM_BLK_ST), par(SSM_BLK_ST), blk3(c_re), blk3(c_im), par(SSM_BLK_IN)],
        out_specs=[s_blk, s_blk, u_blk],
        out_shape=[jax.ShapeDtypeStruct((n_tok, SSM_LANES), F32)] * 2 + [jax.ShapeDtypeStruct((n_tok, D_MODEL), BF16)],
        scratch_shapes=[pltpu.VMEM((2, SSM_BLK_ST), F32)],
        compiler_params=_cparams(2),
    )(u, b_re, b_im, a_re, a_im, c_re, c_im, d_row)


def _s5_bwd(dyg, u, s_re, s_im, dh_res, b_re, b_im, a_re, a_im, c_re, c_im, d_row):
    n_tok = u.shape[0]
    tt = min(S5_TIME_TILE, n_tok)
    n_t = n_tok // tt

    def body(dyg_ref, u_ref, sre_ref, sim_ref, hre_ref, him_ref, res_ref, bre_ref, bim_ref, are_ref, aim_ref,
             cre_ref, cim_ref, d_ref, du_ref, dbre_ref, dbim_ref, dcre_ref, dcim_ref, dd_ref, dare_ref, daim_ref,
             carry, gre, gim):
        i = pl.program_id(1)
        sums = (dbre_ref, dbim_ref, dcre_ref, dcim_ref, dd_ref, dare_ref, daim_ref)

        @pl.when(i == 0)
        def _():
            carry[...] = jnp.zeros(carry.shape, F32)
            for r in sums:
                r[...] = jnp.zeros(r.shape, F32)

        uv, sre, sim = u_ref[...], sre_ref[...], sim_ref[...]
        _, out_vjp = jax.vjp(_s5_out_fn, sre, sim, uv, cre_ref[...], cim_ref[...], d_ref[...])
        gre[...], gim[...], du_out, dcre, dcim, dd = out_vjp((dyg_ref[...],))
        c = _scan_rows(gre, gim, (are_ref[...], -aim_ref[...]), (carry[0:1, :], carry[1:2, :]), True)
        carry[0:1, :] = c[0]
        carry[1:2, :] = c[1]
        g_re, g_im = gre[...], gim[...]
        at_start = i == n_t - 1
        p_re = _shift_down(sre, jnp.where(at_start, 0.0, hre_ref[7:8, :]))
        p_im = _shift_down(sim, jnp.where(at_start, 0.0, him_ref[7:8, :]))
        _, in_vjp = jax.vjp(_s5_in_fn, uv, bre_ref[...], bim_ref[...])
        du_in, dbre, dbim = in_vjp((g_re, g_im))
        du_ref[...] = du_out + du_in + res_ref[...]
        dare = jnp.sum(g_re * p_re + g_im * p_im, axis=0, keepdims=True)
        daim = jnp.sum(g_im * p_re - g_re * p_im, axis=0, keepdims=True)
        for r, val in zip(sums, (dbre, dbim, dcre, dcim, dd, dare, daim)):
            r[...] += val

    u_blk = _spec((tt, SSM_BLK_IN), lambda l, t: (n_t - 1 - t, l))
    s_blk = _spec((tt, SSM_BLK_ST), lambda l, t: (n_t - 1 - t, l))
    halo = _spec((8, SSM_BLK_ST), lambda l, t: (jnp.maximum((n_t - 1 - t) * (tt // 8) - 1, 0), l))
    blk3 = lambda arr: _spec((1,) + arr.shape[1:], lambda l, t: (l, 0, 0))
    par = lambda width: _spec((1, width), lambda l, t: (0, l))
    params = [blk3(b_re), blk3(b_im), par(SSM_BLK_ST), par(SSM_BLK_ST), blk3(c_re), blk3(c_im), par(SSM_BLK_IN)]
    shape = lambda arr: jax.ShapeDtypeStruct(arr.shape, F32)
    return pl.pallas_call(
        body, name="s5_bwd", grid=(N_SSM_BLK, n_t),
        in_specs=[u_blk, u_blk, s_blk, s_blk, halo, halo, u_blk] + params,
        out_specs=[u_blk, blk3(b_re), blk3(b_im), blk3(c_re), blk3(c_im), par(SSM_BLK_IN), par(SSM_BLK_ST), par(SSM_BLK_ST)],
        out_shape=[shape(u), shape(b_re), shape(b_im), shape(c_re), shape(c_im), shape(d_row), shape(a_re), shape(a_im)],
        scratch_shapes=[pltpu.VMEM((2, SSM_BLK_ST), F32), pltpu.VMEM((tt, SSM_BLK_ST), F32), pltpu.VMEM((tt, SSM_BLK_ST), F32)],
        compiler_params=_cparams(2),
    )(dyg, u, s_re, s_im, s_re, s_im, dh_res, b_re, b_im, a_re, a_im, c_re, c_im, d_row)


def _loss_call(h, target, tm):
    n_tok, d = h.shape

    def body(h_ref, t_ref, acc_ref, dh_ref):
        @pl.when(pl.program_id(0) == 0)
        def _():
            acc_ref[...] = jnp.zeros(acc_ref.shape, F32)

        e = h_ref[...] - t_ref[...]
        dh_ref[...] = e * (1.0 / d)
        acc_ref[...] += jnp.sum(jnp.sum(e * e, axis=1, keepdims=True), axis=0, keepdims=True)

    tok = _spec((tm, d), lambda i: (i, 0))
    return pl.pallas_call(
        body, name="loss", grid=(n_tok // tm,),
        in_specs=[tok, tok], out_specs=[_spec((8, 128), lambda i: (0, 0)), tok],
        out_shape=[jax.ShapeDtypeStruct((8, 128), F32), jax.ShapeDtypeStruct(h.shape, F32)],
        compiler_params=_cparams(1),
    )(h, target)


def _block_diag(t):
    nb, ng, rows, cols = t.shape
    eye = jnp.eye(ng, dtype=t.dtype)
    return jnp.einsum('bgrc,gh->bgrhc', t, eye).reshape(nb, ng * rows, ng * cols)


def _block_diag_t(t, rows, cols):
    nb, ng = t.shape[0], t.shape[1] // rows
    t5 = t.reshape(nb, ng, rows, ng, cols)
    return jnp.einsum('bgrhc,gh->bgrc', t5, jnp.eye(ng, dtype=t.dtype))


def _mlp_fwd(layer, h, w1g, w2g):
    pre = _matmul(f"mlp{layer}_up", h, w1g, 'nn', b_cols=(layer,), out_dtype=BF16)
    return pre, _matmul(f"mlp{layer}_down", pre, w2g, 'nn', lhs_fn=_relu2, b_rows=(layer,))


def _mlp_bwd(layer, h, pre, dy, dh_res, w1g, w2g):
    dpre = _matmul(f"mlp{layer}_down_dx", dy, w2g, 'nt', epi=lambda acc, p: acc * (2.0 * jnp.maximum(p, 0.0)),
                   extra=pre, b_rows=(layer,), out_dtype=BF16)
    dw2 = _matmul(f"mlp{layer}_down_dw", pre, dy, 'tn', lhs_fn=_relu2, out_dtype=GRAD_DTYPE)
    dw1 = _matmul(f"mlp{layer}_up_dw", h, dpre, 'tn', out_cols=True, out_dtype=GRAD_DTYPE)
    dh = _matmul(f"mlp{layer}_up_dx", dpre, w1g, 'nt', epi=lambda acc, e: acc + e, extra=dh_res, b_cols=(layer,))
    return dh, dw1, dw2


def _local_step(x, target, w, wg, late_weights, exchange):
    n_tok = x.shape[0]
    tm = min(TOKEN_TILE, n_tok)
    n_tile, n_chunk = n_tok // tm, n_tok // CHUNK
    row = lambda v: v.reshape(1, -1)
    big = {n: wg[n].reshape(D_MODEL, D_MODEL) for n in ('rw_wr', 'rw_wk', 'rw_wv', 'rw_wo')}
    ln_g, ln_b = w['ln_g'], w['ln_b']
    grads = {}
    to_slots = lambda t: t.reshape(N_DEV, t.shape[0] // N_DEV, t.shape[1])

    halo_spec = (8, D_MODEL), lambda i: (jnp.maximum(i * (tm // 8) - 1, 0), 0)
    mix_p = [_tile(x, tm), (x,) + halo_spec, _full(w['rw_mu'][0]), _full(row(w['rw_w0'])), _full(w['rw_w1'][0]),
             _full(w['rw_w2'][0]), _full(row(w['rw_a0'])), _full(w['rw_a1'][0]), _full(w['rw_a2'][0]),
             _full(w['rw_g1'][0]), _full(w['rw_g2'][0])]
    tok_out = lambda dt: ((n_tok, D_MODEL), dt, (tm, D_MODEL), lambda i: (i, 0))
    xr, xk, xv, lw, a, g = _fwd_call("rwkv_mix", _mix_fn, (n_tile,), mix_p,
                                     [tok_out(BF16), tok_out(BF16), tok_out(BF16), tok_out(F32), tok_out(F32), tok_out(F32)])
    r = _matmul("rwkv_r", xr, big['rw_wr'], 'nn')
    k = _matmul("rwkv_k", xk, big['rw_wk'], 'nn')
    v = _matmul("rwkv_v", xv, big['rw_wv'], 'nn')

    scan_in = (r, lw, k, v, a, row(w['rw_k_k']), row(w['rw_k_a']), w['rw_r_k'].reshape(1, -1))
    o, bonus, zs = _rwkv_scan_fwd(*scan_in)

    gate_p = [_tile(o, tm), _tile(bonus, tm), _tile(g, tm), _full(row(w['rw_lnx_g'])), _full(row(w['rw_lnx_b']))]
    (og,) = _fwd_call("rwkv_gate", _gate_fn, (n_tile,), gate_p, [tok_out(BF16)])
    y0 = _matmul("rwkv_o", og, big['rw_wo'], 'nn')
    ln0_p = [_tile(x, tm), _tile(y0, tm), _full(ln_g[0:1]), _full(ln_b[0:1])]
    twice = lambda fn: lambda *args: fn(*args) * 2
    h1, h1_b = _fwd_call("ln0", twice(_res_ln_fn), (n_tile,), ln0_p, [tok_out(F32), tok_out(BF16)])
    late = late_weights(h1)
    w1g, w2g, glu_g = late['mlp_w1'], late['mlp_w2'], late['s5_w_glu']
    pre0, y1 = _mlp_fwd(0, h1_b, w1g, w2g)
    ln1_p = [_tile(h1, tm), _tile(y1, tm), _full(ln_g[1:2]), _full(ln_b[1:2])]
    (h2,) = _fwd_call("ln1", _res_ln_fn, (n_tile,), ln1_p, [tok_out(F32)])

    gp = (SSM_GROUPS, SSM_STATE)
    par_p = [_full(w['s5_a_re'][0]), _full(w['s5_a_im'][0]), _full(w['s5_log_dt'].reshape(SSM_GROUPS, 1))]
    s5_par_fn = lambda ar, ai, ld: _s5_param_fn(ar, ai, ld, None, None)
    gp_out = (gp, F32, gp, lambda i: (0, 0))
    abar_re, abar_im, coef_re, coef_im = _fwd_call("s5_param", s5_par_fn, (1,), par_p, [gp_out] * 4)
    b_flat = [w[n][0].reshape(SSM_LANES, SSM_GROUP) for n in ('s5_b_re', 's5_b_im')]
    bbar_p = [_full(coef_re.reshape(SSM_LANES, 1)), _full(coef_im.reshape(SSM_LANES, 1)), _full(b_flat[0]), _full(b_flat[1])]
    bb_out = ((SSM_LANES, SSM_GROUP), F32, (SSM_LANES, SSM_GROUP), lambda i: (0, 0))
    bbar_re, bbar_im = _fwd_call("s5_bbar", _s5_bbar_fn, (1,), bbar_p, [bb_out] * 2)
    to_in = lambda t: _block_diag(t.reshape(N_SSM_BLK, 8, SSM_STATE, SSM_GROUP).transpose(0, 1, 3, 2))
    to_out = lambda t: _block_diag(t.reshape(N_SSM_BLK, 8, SSM_GROUP, SSM_STATE).transpose(0, 1, 3, 2))
    bblk_re, bblk_im = to_in(bbar_re), to_in(bbar_im)
    cblk_re, cblk_im = to_out(w['s5_c_re'][0]), to_out(w['s5_c_im'][0])
    a_row_re, a_row_im = abar_re.reshape(1, SSM_LANES), abar_im.reshape(1, SSM_LANES)

    d_row = row(w['s5_d'])
    s5_params = (bblk_re, bblk_im, a_row_re, a_row_im, cblk_re, cblk_im, d_row)
    s_re, s_im, yg = _s5_fwd(h2, *s5_params)
    z = _matmul("s5_glu", yg, glu_g, 'nn', b_cols=(0,))
    ln2_p = [_tile(h2, tm), _tile(z, tm), _full(ln_g[2:3]), _full(ln_b[2:3])]
    h3, h3_b = _fwd_call("ln2", twice(_glu_ln_fn), (n_tile,), ln2_p, [tok_out(F32), tok_out(BF16)])
    pre1, y3 = _mlp_fwd(1, h3_b, w1g, w2g)
    ln3_p = [_tile(h3, tm), _tile(y3, tm), _full(ln_g[3:4]), _full(ln_b[3:4])]
    (h4,) = _fwd_call("ln3", _res_ln_fn, (n_tile,), ln3_p, [tok_out(F32)])

    loss_acc, dh4 = _loss_call(h4, target, tm)

    ln_w = ['tile', 'tile_bf16', 'acc', 'acc']
    dh3_res, dy3, dg3, db3 = _bwd_call("ln3_bwd", _res_ln_fn, (n_tile,), ln3_p, [_tile(dh4, tm)], ln_w)
    dh3, dw1_1, dw2_1 = _mlp_bwd(1, h3_b, pre1, dy3, dh3_res, w1g, w2g)
    zero = exchange([('mlp_w1', 1, dw1_1), ('mlp_w2', 1, to_slots(dw2_1))])
    ln2_p[2] = _full(ln_g[2:3] + zero)
    dh2_res, dz, dg2, db2 = _bwd_call("ln2_bwd", _glu_ln_fn, (n_tile,), ln2_p, [_tile(dh3, tm)], ln_w)
    dyg = _matmul("s5_glu_dx", dz, glu_g, 'nt', b_cols=(0,))
    dw_glu = _matmul("s5_glu_dw", yg, dz, 'tn', out_cols=True, out_dtype=GRAD_DTYPE)
    dh2, dbb_re, dbb_im, dcb_re, dcb_im, dd, dabar_re, dabar_im = _s5_bwd(dyg, h2, s_re, s_im, dh2_res, *s5_params)
    from_in = lambda t: _block_diag_t(t, SSM_GROUP, SSM_STATE).transpose(0, 1, 3, 2).reshape(SSM_LANES, SSM_GROUP)
    from_out = lambda t: _block_diag_t(t, SSM_STATE, SSM_GROUP).transpose(0, 1, 3, 2).reshape(1, SSM_GROUPS, SSM_GROUP, SSM_STATE)
    grads['s5_c_re'], grads['s5_c_im'] = from_out(dcb_re), from_out(dcb_im)
    grads['s5_d'] = dd
    dcoef_re, dcoef_im, db_re, db_im = _bwd_call(
        "s5_bbar_bwd", _s5_bbar_fn, (1,), bbar_p, [_full(from_in(dbb_re)), _full(from_in(dbb_im))], ['acc'] * 4)
    grads['s5_b_re'] = db_re.reshape(w['s5_b_re'].shape)
    grads['s5_b_im'] = db_im.reshape(w['s5_b_im'].shape)
    par_ct = [_full(dabar_re.reshape(gp)), _full(dabar_im.reshape(gp)), _full(dcoef_re.reshape(gp)), _full(dcoef_im.reshape(gp))]
    da_re, da_im, dlog_dt = _bwd_call("s5_param_bwd", s5_par_fn, (1,), par_p, par_ct, ['acc'] * 3)
    grads['s5_a_re'], grads['s5_a_im'], grads['s5_log_dt'] = da_re[None], da_im[None], dlog_dt.reshape(1, SSM_GROUPS)
    zero = exchange([(n, 0, grads[n]) for n in ('s5_a_re', 's5_a_im', 's5_log_dt', 's5_b_re', 's5_b_im', 's5_c_re', 's5_c_im')])
    ln1_p[2] = _full(ln_g[1:2] + zero)

    dh1_res, dy1, dg1, db1 = _bwd_call("ln1_bwd", _res_ln_fn, (n_tile,), ln1_p, [_tile(dh2, tm)], ln_w)
    dh1, dw1_0, dw2_0 = _mlp_bwd(0, h1_b, pre0, dy1, dh1_res, w1g, w2g)
    dx_res, dy0, dg0, db0 = _bwd_call("ln0_bwd", _res_ln_fn, (n_tile,), ln0_p, [_tile(dh1, tm)], ln_w)
    grads['ln_g'] = jnp.concatenate([dg0, dg1, dg2, dg3])
    grads['ln_b'] = jnp.concatenate([db0, db1, db2, db3])
    dog = _matmul("rwkv_o_dx", dy0, big['rw_wo'], 'nt')
    dw_o = to_slots(_matmul("rwkv_o_dw", og, dy0, 'tn', out_dtype=GRAD_DTYPE))
    zero = exchange([('s5_w_glu', 0, dw_glu), ('mlp_w1', 0, dw1_0), ('mlp_w2', 0, to_slots(dw2_0)), ('rw_wo', 0, dw_o)])
    gate_p[3] = _full(row(w['rw_lnx_g']) + zero)
    d_o, dbonus, dgate, grads['rw_lnx_g'], grads['rw_lnx_b'] = _bwd_call(
        "rwkv_gate_bwd", _gate_fn, (n_tile,), gate_p, [_tile(dog, tm)], ['tile', 'tile', 'tile', 'acc', 'acc'])
    dr, dlw, dk, dv, da, grads['rw_k_k'], grads['rw_k_a'], dr_k = _rwkv_scan_bwd(*scan_in, zs, d_o, dbonus)
    grads['rw_r_k'] = dr_k.reshape(w['rw_r_k'].shape)
    dw_r = to_slots(_matmul("rwkv_r_dw", xr, dr, 'tn', out_dtype=GRAD_DTYPE))
    dw_k = to_slots(_matmul("rwkv_k_dw", xk, dk, 'tn', out_dtype=GRAD_DTYPE))
    dw_v = to_slots(_matmul("rwkv_v_dw", xv, dv, 'tn', out_dtype=GRAD_DTYPE))
    done = ('ln_g', 'ln_b', 'rw_lnx_g', 'rw_lnx_b', 'rw_k_k', 'rw_k_a', 'rw_r_k')
    zero = exchange([('rw_wr', 0, dw_r), ('rw_wk', 0, dw_k), ('rw_wv', 0, dw_v)] + [(n, 0, grads[n]) for n in done])
    mix_p[3] = _full(row(w['rw_w0']) + zero)
    dxr = _matmul("rwkv_r_dx", dr, big['rw_wr'], 'nt')
    dxk = _matmul("rwkv_k_dx", dk, big['rw_wk'], 'nt')
    dxv = _matmul("rwkv_v_dx", dv, big['rw_wv'], 'nt')
    mix_ct = [_tile(t, tm) for t in (dxr, dxk, dxv, dlw, da, dgate)]
    halo_grad = ('tile', ((n_tile * 8, D_MODEL), (8, D_MODEL), lambda i: (i, 0)))
    res = _bwd_call("rwkv_mix_bwd", _mix_fn, (n_tile,), mix_p, mix_ct, ['tile', halo_grad] + ['acc'] * 9,
                    addends={0: _tile(dx_res, tm)})
    corr = res[1].reshape(n_tile, 8, D_MODEL)[1:, 7:8, :]
    corr = jnp.pad(corr, ((0, 1), (tm - 1, 0), (0, 0)))
    dx = (res[0].reshape(n_tile, tm, D_MODEL) + corr).reshape(n_tok, D_MODEL)
    (grads['rw_mu'], grads['rw_w0'], grads['rw_w1'], grads['rw_w2'], grads['rw_a0'], grads['rw_a1'], grads['rw_a2'],
     grads['rw_g1'], grads['rw_g2']) = [t[None] if t.shape[0] != 1 else t for t in res[2:]]
    return loss_acc[0, 0], dx, grads


def _full_shape(local_shape, axis):
    return local_shape[:axis] + (N_DEV * local_shape[axis],) + local_shape[axis + 1:]


def _split_shape(local_shape, axis):
    return local_shape[:axis] + (N_DEV, local_shape[axis]) + local_shape[axis + 1:]


def _mesh_pos():
    return lax.axis_index("x"), lax.axis_index("y"), lax.axis_index("c")


def _flip(v, f):
    return 1 - v if f else v


def _hbm_call(name, body, arrays, out_shapes, n_sems):
    n_t = len(arrays)
    hbm = pl.BlockSpec(memory_space=pl.ANY)
    return pl.pallas_call(
        body, name=name, out_shape=out_shapes, in_specs=[hbm] * n_t, out_specs=[hbm] * n_t,
        scratch_shapes=[pltpu.SemaphoreType.DMA((n_t, n_sems)), pltpu.SemaphoreType.DMA((n_t, n_sems)),
                        pltpu.SemaphoreType.DMA((n_t,))],
    )(*arrays)


def _all_gather(name, locals_):
    n_t = len(locals_)

    def body(*refs):
        x_refs, out_refs = refs[:n_t], refs[n_t:2 * n_t]
        send_sems, recv_sems, local_sems = refs[2 * n_t:]
        x, y, c = _mesh_pos()
        me, sibling = (x, y, c), (x, y, 1 - c)
        chips = [(1 - x, y), (x, 1 - y), (1 - x, 1 - y)]

        def copy(t, k, block, to, own=False):
            slot = out_refs[t].at[4 * block[0] + 2 * block[1] + block[2]]
            return pltpu.make_async_remote_copy(
                src_ref=x_refs[t] if own else slot, dst_ref=slot,
                send_sem=send_sems.at[t, k], recv_sem=recv_sems.at[t, k],
                device_id=to, device_id_type=pl.DeviceIdType.MESH)

        mine = [pltpu.make_async_copy(x_refs[t], out_refs[t].at[4 * x + 2 * y + c], local_sems.at[t]) for t in range(n_t)]
        for cp in mine:
            cp.start()
        sent = []
        for t in range(n_t):
            sent.append(copy(t, 0, me, sibling, own=True))
            sent += [copy(t, 1 + j, me, (*chip, c), own=True) for j, chip in enumerate(chips)]
        for cp in sent:
            cp.start()
        for j, chip in enumerate(chips):
            for t in range(n_t):
                copy(t, 1 + j, (*chip, c), me).wait_recv()
                passed = copy(t, 4 + j, (*chip, c), sibling)
                passed.start()
                sent.append(passed)
        for t in range(n_t):
            copy(t, 0, sibling, me).wait_recv()
            for j, chip in enumerate(chips):
                copy(t, 4 + j, (*chip, 1 - c), me).wait_recv()
        for cp in sent:
            cp.wait_send()
        for cp in mine:
            cp.wait()

    outs = [jax.ShapeDtypeStruct((N_DEV,) + a.shape, a.dtype) for a in locals_]
    return _hbm_call(name, body, locals_, outs, 7)


def _all_to_all(name, blocks):
    n_t = len(blocks)

    def body(*refs):
        x_refs, out_refs = refs[:n_t], refs[n_t:2 * n_t]
        send_sems, recv_sems, local_sems = refs[2 * n_t:]
        x, y, c = _mesh_pos()
        my_slot = 4 * x + 2 * y + c
        mine = [pltpu.make_async_copy(x_refs[t].at[my_slot], out_refs[t].at[my_slot], local_sems.at[t]) for t in range(n_t)]
        for cp in mine:
            cp.start()
        copies = []
        for k in range(1, N_DEV):
            peer = (_flip(x, k & 4), _flip(y, k & 2), _flip(c, k & 1))
            peer_slot = 4 * peer[0] + 2 * peer[1] + peer[2]
            for t in range(n_t):
                sems = dict(send_sem=send_sems.at[t, k - 1], recv_sem=recv_sems.at[t, k - 1],
                            device_id=peer, device_id_type=pl.DeviceIdType.MESH)
                cp = pltpu.make_async_remote_copy(src_ref=x_refs[t].at[peer_slot], dst_ref=out_refs[t].at[my_slot], **sems)
                cp.start()
                landed = pltpu.make_async_remote_copy(src_ref=x_refs[t].at[my_slot], dst_ref=out_refs[t].at[peer_slot], **sems)
                copies.append((cp, landed))
        for sent, landed in copies:
            landed.wait_recv()
            sent.wait_send()
        for cp in mine:
            cp.wait()

    outs = [jax.ShapeDtypeStruct(a.shape, a.dtype) for a in blocks]
    return _hbm_call(name, body, blocks, outs, 7)


_HBM = pl.BlockSpec(memory_space=pltpu.HBM)
_SEM = pl.BlockSpec(memory_space=pltpu.SEMAPHORE)
_EFFECT = pltpu.SideEffectType.DATAFLOW_SIDE_EFFECTING


def _peer_copies(x_refs, land_refs, send_sems, recv_sems, gather, mine):
    x, y, c = _mesh_pos()
    my_slot = 4 * x + 2 * y + c
    copies = []
    for k in range(1, N_DEV):
        peer = (_flip(x, k & 4), _flip(y, k & 2), _flip(c, k & 1))
        peer_slot = 4 * peer[0] + 2 * peer[1] + peer[2]
        for t, (x_ref, land) in enumerate(zip(x_refs, land_refs)):
            sem = t * (N_DEV - 1) + k - 1
            copies.append(pltpu.make_async_remote_copy(
                src_ref=x_ref if gather[t] else x_ref.at[peer_slot], dst_ref=land.at[my_slot if mine else peer_slot],
                send_sem=send_sems.at[sem], recv_sem=recv_sems.at[sem], device_id=peer, device_id_type=pl.DeviceIdType.MESH))
    return copies


def _push_start(name, arrays, gather):
    n_t = len(arrays)
    lands = [lax.empty((N_DEV,) + a.shape if whole else a.shape, a.dtype) for a, whole in zip(arrays, gather)]

    def body(*refs):
        x_refs, land_refs = refs[:n_t], refs[n_t:2 * n_t]
        send_sems, recv_sems = refs[2 * n_t], refs[2 * n_t + 1]
        for sent in _peer_copies(x_refs, land_refs, send_sems, recv_sems, gather, True):
            sent.start()
        refs[-1][...] = jnp.zeros((8, 128), F32)

    held = list(arrays) + lands
    sems = pltpu.SemaphoreType.DMA((n_t * (N_DEV - 1),))
    out = pl.pallas_call(
        body, name=name,
        out_shape=(sems, sems, *[pltpu.HBM(a.shape, a.dtype) for a in held], jax.ShapeDtypeStruct((8, 128), F32)),
        in_specs=[_HBM] * (2 * n_t),
        out_specs=(_SEM, _SEM, *[_HBM] * (2 * n_t), pl.BlockSpec(memory_space=pltpu.VMEM)),
        input_output_aliases={i: 2 + i for i in range(2 * n_t)},
        compiler_params=pltpu.CompilerParams(has_side_effects=_EFFECT),
    )(*[pltpu.with_memory_space_constraint(a, pltpu.HBM) for a in held])
    return dict(send=out[0], recv=out[1], held=out[2:2 + 2 * n_t], zero=out[-1][0, 0], gather=gather)


def _push_wait(name, started, after):
    held, gather = started['held'], started['gather']
    n_t = len(held) // 2

    def body(*refs):
        x_refs, land_refs = refs[:n_t], refs[n_t:2 * n_t]
        send_sems, recv_sems = refs[2 * n_t], refs[2 * n_t + 1]
        for sent in _peer_copies(x_refs, land_refs, send_sems, recv_sems, gather, True):
            sent.wait_send()
        for landed in _peer_copies(x_refs, land_refs, send_sems, recv_sems, gather, False):
            landed.wait_recv()

    out = pl.pallas_call(
        body, name=name,
        out_shape=[pltpu.HBM(a.shape, a.dtype) for a in held],
        in_specs=[_HBM] * (2 * n_t) + [_SEM, _SEM, pl.BlockSpec(memory_space=pl.ANY)],
        out_specs=[_HBM] * (2 * n_t),
        input_output_aliases={i: i for i in range(2 * n_t)},
        compiler_params=pltpu.CompilerParams(has_side_effects=_EFFECT),
    )(*held, started['send'], started['recv'], after)
    return out[:n_t], out[n_t:]


ADAM_BLOCK = 128 * 1024


def _adam_update(part, w_ref, m_ref, v_ref, g_ref, d_ref, nm_ref, nv_ref):
    g = part(0)
    for i in range(1, N_DEV):
        g = g + part(i)
    m_new = ADAM_B1 * m_ref[...] + (1.0 - ADAM_B1) * g
    v_new = ADAM_B2 * v_ref[...] + (1.0 - ADAM_B2) * (g * g)
    m_hat = m_new * (1.0 / (1.0 - ADAM_B1 ** ADAM_STEP))
    v_hat = v_new * (1.0 / (1.0 - ADAM_B2 ** ADAM_STEP))
    g_ref[...] = g
    nm_ref[...] = m_new
    nv_ref[...] = v_new
    d_ref[...] = -ADAM_LR * (m_hat / (jnp.sqrt(v_hat) + ADAM_EPS) + ADAM_WD * w_ref[...])


def _adam_many(name, items):
    n_t = len(items)
    n_in = [4 if own is None else 5 for *_, own in items]
    first = [sum(n_in[:t]) for t in range(n_t)]

    def body(*refs):
        ins, outs = refs[:sum(n_in)], refs[sum(n_in):]
        x, y, c = _mesh_pos()
        my_slot = 4 * x + 2 * y + c
        for t in range(n_t):
            s_ref, w_ref, m_ref, v_ref = ins[first[t]:first[t] + 4]
            if n_in[t] == 4:
                part = lambda i: s_ref[i]
            else:
                own_ref = ins[first[t] + 4]
                part = lambda i: jnp.where(my_slot == i, own_ref[...], s_ref[i])
            _adam_update(part, w_ref, m_ref, v_ref, *outs[4 * t:4 * t + 4])

    vmem = pl.BlockSpec(memory_space=pltpu.VMEM)
    out = pl.pallas_call(
        body, name=name, in_specs=[vmem] * sum(n_in), out_specs=[vmem] * (4 * n_t),
        out_shape=[jax.ShapeDtypeStruct(item[1].shape, F32) for item in items for _ in range(4)],
        compiler_params=pltpu.CompilerParams(vmem_limit_bytes=VMEM_LIMIT),
    )(*[a for item in items for a in item if a is not None])
    return [out[4 * t:4 * t + 4] for t in range(n_t)]


def _adam_call(name, slots, w, m, v, own=None):
    rows, width = w.shape
    tile = min(rows, ADAM_BLOCK // max(width, 128))

    def body(*refs):
        if own is None:
            s_ref, w_ref, m_ref, v_ref = refs[:4]
            part = lambda i: s_ref[i].astype(F32)
        else:
            s_ref, own_ref, w_ref, m_ref, v_ref = refs[:5]
            x, y, c = _mesh_pos()
            part = lambda i: jnp.where(4 * x + 2 * y + c == i, own_ref[...], s_ref[i]).astype(F32)
        _adam_update(part, w_ref, m_ref, v_ref, *refs[-4:])

    blk = _spec((tile, width), lambda i: (i, 0))
    ops = [slots] + ([] if own is None else [own]) + [w, m, v]
    return pl.pallas_call(
        body, name=name, grid=(rows // tile,),
        in_specs=[_spec((N_DEV, tile, width), lambda i: (0, i, 0))] + [blk] * (len(ops) - 1),
        out_specs=[blk] * 4, out_shape=[jax.ShapeDtypeStruct(w.shape, F32)] * 4,
        compiler_params=_cparams(1),
    )(*ops)


def kernel(x, ln_g, ln_b, rw_mu, rw_w0, rw_w1, rw_w2, rw_a0, rw_a1, rw_a2, rw_g1, rw_g2, rw_k_k, rw_k_a, rw_r_k, rw_wr, rw_wk, rw_wv, rw_wo, rw_lnx_g, rw_lnx_b, s5_a_re, s5_a_im, s5_log_dt, s5_b_re, s5_b_im, s5_c_re, s5_c_im, s5_d, s5_w_glu, mlp_w1, mlp_w2, loss_target, m_ln_g, m_ln_b, m_rw_mu, m_rw_w0, m_rw_w1, m_rw_w2, m_rw_a0, m_rw_a1, m_rw_a2, m_rw_g1, m_rw_g2, m_rw_k_k, m_rw_k_a, m_rw_r_k, m_rw_wr, m_rw_wk, m_rw_wv, m_rw_wo, m_rw_lnx_g, m_rw_lnx_b, m_s5_a_re, m_s5_a_im, m_s5_log_dt, m_s5_b_re, m_s5_b_im, m_s5_c_re, m_s5_c_im, m_s5_d, m_s5_w_glu, m_mlp_w1, m_mlp_w2, v_ln_g, v_ln_b, v_rw_mu, v_rw_w0, v_rw_w1, v_rw_w2, v_rw_a0, v_rw_a1, v_rw_a2, v_rw_g1, v_rw_g2, v_rw_k_k, v_rw_k_a, v_rw_r_k, v_rw_wr, v_rw_wk, v_rw_wv, v_rw_wo, v_rw_lnx_g, v_rw_lnx_b, v_s5_a_re, v_s5_a_im, v_s5_log_dt, v_s5_b_re, v_s5_b_im, v_s5_c_re, v_s5_c_im, v_s5_d, v_s5_w_glu, v_mlp_w1, v_mlp_w2):
    given = dict(locals())
    local_w = {n: given[n] for n in WEIGHTS}
    local_m = {n: given["m_" + n] for n in WEIGHTS}
    local_v = {n: given["v_" + n] for n in WEIGHTS}
    small = [n for n in WEIGHTS if n in SHARDED and n not in BIG]

    my_slot = 4 * lax.axis_index("x") + 2 * lax.axis_index("y") + lax.axis_index("c")
    as_bf16 = {n: local_w[n].astype(BF16) for n in BIG}

    late_started = _push_start("weights_late_start", [as_bf16[n] for n in LATE], [True] * len(LATE))
    early = [n for n in BIG if n not in LATE]
    gathered = _all_gather("weights_all_gather", [as_bf16[n] for n in early] + [local_w[n] for n in small])
    early_gathered = dict(zip(early, gathered))
    full = dict(local_w)
    for n, blk in zip(small, gathered[len(early):]):
        full[n] = jnp.moveaxis(blk, 0, SHARDED[n]).reshape(_full_shape(local_w[n].shape, SHARDED[n]))
    full['rw_w0'] = full['rw_w0'] + late_started['zero']

    def late_weights(after):
        _, lands = _push_wait("weights_late_wait", late_started, after)
        own = lambda n: as_bf16[n][None]
        return {n: lax.dynamic_update_slice(land, own(n), (my_slot,) + (0,) * as_bf16[n].ndim) for n, land in zip(LATE, lands)}

    exchanges = []

    def exchange(entries):
        started = _push_start(f"grads_start_{len(exchanges)}", [t for _, _, t in entries], [n in REPLICATED for n, _, _ in entries])
        exchanges.append((entries, started))
        return started['zero']

    loss_sq, dx, grads = _local_step(x[0], loss_target[0], full, early_gathered, late_weights, exchange)
    loss = (0.5 / D_MODEL) * lax.psum(loss_sq, MESH_AXES)

    wide = {'s5_b_re': (SSM_LANES, SSM_GROUP), 's5_b_im': (SSM_LANES, SSM_GROUP),
            's5_c_re': (D_MODEL, SSM_STATE), 's5_c_im': (D_MODEL, SSM_STATE)}
    by_name, many = {}, []
    for e, (entries, started) in enumerate(exchanges):
        sources, lands = _push_wait(f"grads_wait_{e}", started, dx)
        for (n, i, _), src, slots in zip(entries, sources, lands):
            if n in wide:
                as_view = lambda t: t.reshape(wide[n])
                out = _adam_call(f"adamw_{n}", slots.reshape((N_DEV,) + wide[n]), as_view(local_w[n]), as_view(local_m[n]),
                                 as_view(local_v[n]), own=as_view(src))
                by_name[n] = [t.reshape(local_w[n].shape) for t in out]
            elif n in REPLICATED:
                many.append((n, (slots, local_w[n], local_m[n], local_v[n], src)))
            else:
                part = lambda d: d[n].reshape((-1,) + slots.shape[1:])[i]
                own = lax.dynamic_index_in_dim(src, my_slot, 0, keepdims=False)
                out = _adam_call(f"adamw_{n}_{i}", slots, part(local_w), part(local_m), part(local_v), own=own)
                by_name.setdefault(n, {})[i] = out
    for n in BIG:
        per_layer = [by_name[n][i] for i in sorted(by_name[n])]
        by_name[n] = [jnp.stack(ts).reshape(local_w[n].shape) for ts in zip(*per_layer)]
    split = lambda n: jnp.moveaxis(grads[n].reshape(_split_shape(local_w[n].shape, SHARDED[n])), SHARDED[n], 0)
    for n, slots in zip(small, _all_to_all("grads_all_to_all", [split(n) for n in small])):
        many.append((n, (slots, local_w[n], local_m[n], local_v[n], None)))
    last = [n for n in REPLICATED if n not in by_name and n not in dict(many)]
    for n, slots in zip(last, _all_gather("grads_all_gather", [grads[n] for n in last])):
        many.append((n, (slots, local_w[n], local_m[n], local_v[n], None)))
    for (n, _), out in zip(many, _adam_many("adamw_small", [item for _, item in many])):
        by_name[n] = out
    results = [by_name[n][k] for k in range(4) for n in WEIGHTS]
    return (loss, dx[None], *results)
```

```python
import functools
import math

import jax
import jax.numpy as jnp
from jax import lax
from jax.experimental import pallas as pl
from jax.experimental.pallas import tpu as pltpu

F32 = jnp.float32
BF16 = jnp.bfloat16
HI = lax.Precision.HIGHEST

D_MODEL = 1024
HEAD = 64
PAIR = 2 * HEAD
N_PAIR = D_MODEL // PAIR
CHUNK = 64
GN_EPS = 64e-5
LN_EPS = 1e-5
SSM_GROUP = 16
SSM_STATE = 64
SSM_GROUPS = D_MODEL // SSM_GROUP
SSM_LANES = SSM_GROUPS * SSM_STATE
SSM_BLK_IN = 128
SSM_BLK_ST = 512
N_SSM_BLK = D_MODEL // SSM_BLK_IN
DEPTH = 2
DN_ALPHA = (2.0 * DEPTH) ** 0.25
ADAM_LR, ADAM_B1, ADAM_B2, ADAM_EPS, ADAM_WD, ADAM_STEP = 0.001, 0.9, 0.999, 1e-08, 0.01, 10
N_DEV = 8
MESH_AXES = ("x", "y", "c")
VMEM_LIMIT = 56 * 1024 * 1024
TOKEN_TILE = 256

SHARDED = {
    "rw_mu": 2, "rw_w1": 1, "rw_w2": 2, "rw_a1": 1, "rw_a2": 2, "rw_g1": 1, "rw_g2": 2,
    "rw_wr": 1, "rw_wk": 1, "rw_wv": 1, "rw_wo": 1, "s5_d": 1, "s5_w_glu": 2, "mlp_w1": 2, "mlp_w2": 1,
}
WEIGHTS = ['ln_g', 'ln_b', 'rw_mu', 'rw_w0', 'rw_w1', 'rw_w2', 'rw_a0', 'rw_a1', 'rw_a2', 'rw_g1', 'rw_g2', 'rw_k_k',
           'rw_k_a', 'rw_r_k', 'rw_wr', 'rw_wk', 'rw_wv', 'rw_wo', 'rw_lnx_g', 'rw_lnx_b', 's5_a_re', 's5_a_im',
           's5_log_dt', 's5_b_re', 's5_b_im', 's5_c_re', 's5_c_im', 's5_d', 's5_w_glu', 'mlp_w1', 'mlp_w2']
REPLICATED = [n for n in WEIGHTS if n not in SHARDED]
BIG = ['rw_wr', 'rw_wk', 'rw_wv', 'rw_wo', 's5_w_glu', 'mlp_w1', 'mlp_w2']
LATE = ['s5_w_glu', 'mlp_w1', 'mlp_w2']
GRAD_DTYPE = BF16


def _cparams(n_grid):
    return pltpu.CompilerParams(dimension_semantics=("arbitrary",) * n_grid, vmem_limit_bytes=VMEM_LIMIT)


@jax.custom_vjp
def _mm(x, w):
    return jnp.dot(x.astype(BF16), w.astype(BF16), preferred_element_type=F32)


def _mm_fwd(x, w):
    return _mm(x, w), (x, w)


def _mm_bwd(res, dy):
    x, w = res
    dyb = dy.astype(BF16)
    dx = lax.dot_general(dyb, w.astype(BF16), (((1,), (1,)), ((), ())), preferred_element_type=F32)
    dw = lax.dot_general(x.astype(BF16), dyb, (((0,), (0,)), ((), ())), preferred_element_type=F32)
    return dx, dw


_mm.defvjp(_mm_fwd, _mm_bwd)


_DOT_DIMS = {'nn': (((1,), (0,)), ((), ())), 'nt': (((1,), (1,)), ((), ())), 'tn': (((0,), (0,)), ((), ()))}
_BATCH_DOT_DIMS = {'nn': (((2,), (1,)), ((0,), (0,))), 'nt': (((2,), (2,)), ((0,), (0,))), 'tn': (((1,), (1,)), ((0,), (0,)))}
CHUNK_PAIRS = 8
CHUNK_PASSES = 1


def _split_bf16(a):
    hi = a.astype(BF16)
    return hi, (a - hi.astype(F32)).astype(BF16)


def _dot_passes(a, b, mode, passes=None):
    dims = (_DOT_DIMS if a.ndim == 2 else _BATCH_DOT_DIMS)[mode]
    dot = lambda p, q: lax.dot_general(p, q, dims, preferred_element_type=F32)
    if (CHUNK_PASSES if passes is None else passes) == 1:
        return dot(a.astype(BF16), b.astype(BF16))
    (ah, al), (bh, bl) = _split_bf16(a), _split_bf16(b)
    return dot(ah, bh) + (dot(ah, bl) + dot(al, bh))


@functools.partial(jax.custom_vjp, nondiff_argnums=(2,))
def _pdot(a, b, mode):
    return _dot_passes(a, b, mode)


def _pdot_fwd(a, b, mode):
    return _dot_passes(a, b, mode), (a, b)


def _pdot_bwd(mode, res, dy):
    a, b = res
    if mode == 'nn':
        return _dot_passes(dy, b, 'nt'), _dot_passes(a, dy, 'tn')
    if mode == 'nt':
        return _dot_passes(dy, b, 'nn'), _dot_passes(dy, a, 'tn')
    return _dot_passes(b, dy, 'nt'), _dot_passes(a, dy, 'nn')


_pdot.defvjp(_pdot_fwd, _pdot_bwd)


def _tri_sum(x, upper):
    nb, n = x.shape[0], x.shape[1]
    ti = lax.broadcasted_iota(jnp.int32, (nb, n, n), 1)
    tj = lax.broadcasted_iota(jnp.int32, (nb, n, n), 2)
    tri = ((ti <= tj) if upper else (ti >= tj)).astype(BF16)
    hi = x.astype(BF16)
    rest = x - hi.astype(F32)
    mid = rest.astype(BF16)
    lo = (rest - mid.astype(F32)).astype(BF16)
    dot = lambda q: lax.dot_general(tri, q, _BATCH_DOT_DIMS['nn'], preferred_element_type=F32)
    return dot(hi) + (dot(mid) + dot(lo))


@jax.custom_vjp
def _cumsum_rows(x):
    return _tri_sum(x, False)


_cumsum_rows.defvjp(lambda x: (_tri_sum(x, False), None), lambda _, dy: (_tri_sum(dy, True),))


def _power_sum(a, order):
    m = a.shape[-1]
    ti = lax.broadcasted_iota(jnp.int32, (1, m, m), 1)
    tj = lax.broadcasted_iota(jnp.int32, (1, m, m), 2)
    pw, total = _dot_passes(a, a, 'nn'), (ti == tj).astype(F32) + a
    for _ in range(int(math.log2(order)) - 2):
        both = _dot_passes(pw, jnp.concatenate([pw, total], axis=2), 'nn')
        pw, total = both[:, :, :m], total + both[:, :, m:]
    return total + _dot_passes(pw, total, 'nn')


@functools.partial(jax.custom_vjp, nondiff_argnums=(1,))
def _nilpotent_inverse(a, order):
    return _power_sum(a, order)


def _nilpotent_inverse_fwd(a, order):
    inv = _power_sum(a, order)
    return inv, inv


def _nilpotent_inverse_bwd(order, inv, g):
    return (_dot_passes(_dot_passes(inv, g, 'tn'), inv, 'nt'),)


_nilpotent_inverse.defvjp(_nilpotent_inverse_fwd, _nilpotent_inverse_bwd)


@functools.partial(jax.custom_vjp, nondiff_argnums=(1,))
def _roll_rows(x, shift):
    return pltpu.roll(x, shift, 0)


def _roll_rows_fwd(x, shift):
    return pltpu.roll(x, shift, 0), None


def _roll_rows_bwd(shift, _, dy):
    n = dy.shape[0]
    return (pltpu.roll(dy, (n - shift) % n, 0),)


_roll_rows.defvjp(_roll_rows_fwd, _roll_rows_bwd)


def _shift_down(x, first_row):
    row = lax.broadcasted_iota(jnp.int32, x.shape, 0)
    return jnp.where(row == 0, first_row, _roll_rows(x, 1))


def _sigmoid(x):
    return 1.0 / (1.0 + jnp.exp(-x))


def _softplus(x):
    return jnp.maximum(x, 0.0) + jnp.log(1.0 + jnp.exp(-jnp.abs(x)))


def _gelu(x):
    return 0.5 * x * (1.0 + jnp.tanh(math.sqrt(2.0 / math.pi) * (x + 0.044715 * (x * x * x))))


def _layer_norm(x, g, b):
    mu = jnp.mean(x, axis=-1, keepdims=True)
    xc = x - mu
    var = jnp.mean(xc * xc, axis=-1, keepdims=True)
    return xc * lax.rsqrt(var + LN_EPS) * g + b


def _spec(block, index_map):
    return pl.BlockSpec(block, index_map)


def _tile(arr, tm):
    return (arr, (tm, arr.shape[1]), lambda i: (i, 0))


def _full(arr):
    nd = arr.ndim
    return (arr, arr.shape, lambda *_: (0,) * nd)


def _fwd_call(name, fn, grid, prims, outs):
    n_in = len(prims)

    def body(*refs):
        vals = [r[...] for r in refs[:n_in]]
        vals = [v.astype(F32) if v.dtype != F32 else v for v in vals]
        res = fn(*vals)
        for o, r in zip(refs[n_in:], res):
            o[...] = r.astype(o.dtype)

    return pl.pallas_call(
        body, name=name, grid=grid,
        in_specs=[_spec(b, m) for _, b, m in prims],
        out_specs=[_spec(b, m) for _, _, b, m in outs],
        out_shape=[jax.ShapeDtypeStruct(s, d) for s, d, _, _ in outs],
        compiler_params=_cparams(len(grid)),
    )(*[a for a, _, _ in prims])


def _bwd_call(name, fn, grid, prims, cts, wants, addends=None):
    addends = addends or {}
    n_in, n_ct = len(prims), len(cts)
    out_place = {}
    wants, out_dtype = list(wants), {}
    for i, w in enumerate(wants):
        if isinstance(w, tuple):
            wants[i], out_place[i] = w
        elif w == 'tile_bf16':
            wants[i], out_dtype[i] = 'tile', BF16
    diff = [i for i, w in enumerate(wants) if w]
    add_idx = [i for i in diff if i in addends]
    last_axis = len(grid) - 1

    def body(*refs):
        vals = [r[...] for r in refs[:n_in]]
        vals = [v.astype(F32) if v.dtype != F32 else v for v in vals]
        ct_vals = [r[...] for r in refs[n_in:n_in + n_ct]]
        ct_vals = [v.astype(F32) if v.dtype != F32 else v for v in ct_vals]
        add_refs = dict(zip(add_idx, refs[n_in + n_ct:n_in + n_ct + len(add_idx)]))
        out_refs = refs[n_in + n_ct + len(add_idx):]

        def f(*dargs):
            full = list(vals)
            for i, a in zip(diff, dargs):
                full[i] = a
            return tuple(fn(*full))

        _, vjp = jax.vjp(f, *[vals[i] for i in diff])
        grads = vjp(tuple(ct_vals))
        first = pl.program_id(last_axis) == 0
        for o, g, i in zip(out_refs, grads, diff):
            if wants[i] == 'tile':
                if i in add_refs:
                    g = g + add_refs[i][...]
                o[...] = g.astype(o.dtype)
            else:
                @pl.when(first)
                def _(o=o):
                    o[...] = jnp.zeros(o.shape, o.dtype)
                o[...] += g

    ins = list(prims) + list(cts) + [addends[i] for i in add_idx]
    places = [out_place.get(i, (prims[i][0].shape, prims[i][1], prims[i][2])) for i in diff]
    return pl.pallas_call(
        body, name=name, grid=grid,
        in_specs=[_spec(b, m) for _, b, m in ins],
        out_specs=[_spec(b, m) for _, b, m in places],
        out_shape=[jax.ShapeDtypeStruct(s, out_dtype.get(i, F32)) for i, (s, _, _) in zip(diff, places)],
        compiler_params=_cparams(len(grid)),
    )(*[a for a, _, _ in ins])


def _matmul(name, a, b, mode, *, out_dtype=F32, lhs_fn=None, epi=None, extra=None, tm=1024, tn=1024, tk=1024,
            b_cols=None, b_rows=None, out_cols=False):
    if mode == 'tn':
        kdim, m = a.shape
    else:
        m, kdim = a.shape
    if b_rows is not None:
        loc, cols = b.shape[-2], b.shape[-1]
        if mode == 'nn':
            n, k_shards = cols, max(1, tk // loc)
            tk = k_shards * loc
        else:
            n, tn = N_DEV * loc, loc
    elif b_cols is None:
        n = b.shape[0] if mode == 'nt' else b.shape[1]
    else:
        rows, loc = b.shape[-2], b.shape[-1]
        if mode == 'nn':
            n, n_shards = N_DEV * loc, max(1, tn // loc)
            tn = n_shards * loc
        else:
            n, k_shards = rows, max(1, tk // loc)
            tk = k_shards * loc
    if out_cols:
        o_shards = max(1, min(tn, n) // (n // N_DEV))
        tn = o_shards * (n // N_DEV)
    tm, tn, tk = min(tm, m), min(tn, n), min(tk, kdim)
    nk = kdim // tk
    grid = (m // tm, n // tn, nk)
    a_spec = _spec((tk, tm), lambda i, j, k: (k, i)) if mode == 'tn' else _spec((tm, tk), lambda i, j, k: (i, k))
    if b_rows is not None:
        lead = tuple(b_rows)
        skip = (None,) * (1 + len(lead))
        if mode == 'nn':
            b_spec = _spec((k_shards,) + skip[1:] + (tk // k_shards, tn), lambda i, j, k: (k,) + lead + (0, j))
        else:
            b_spec = _spec(skip + (tn, tk), lambda i, j, k: (j,) + lead + (0, k))
    elif b_cols is None:
        b_spec = _spec((tn, tk), lambda i, j, k: (j, k)) if mode == 'nt' else _spec((tk, tn), lambda i, j, k: (k, j))
    else:
        lead = tuple(b_cols)
        skip = (None,) * (1 + len(lead))
        if mode == 'nn':
            b_spec = _spec((n_shards,) + skip[1:] + (tk, tn // n_shards), lambda i, j, k: (j,) + lead + (k, 0))
        else:
            b_spec = _spec((k_shards,) + skip[1:] + (tn, tk // k_shards), lambda i, j, k: (k,) + lead + (j, 0))
    o_spec = _spec((tm, tn), lambda i, j, k: (i, j))
    if out_cols:
        o_place, o_shape = _spec((o_shards, tm, tn // o_shards), lambda i, j, k: (j, i, 0)), (N_DEV, m, n // N_DEV)
    else:
        o_place, o_shape = o_spec, (m, n)
    dims = _DOT_DIMS[mode]

    def body(*refs):
        if extra is None:
            a_ref, b_ref, o_ref, acc = refs
            x_ref = None
        else:
            a_ref, b_ref, x_ref, o_ref, acc = refs
        k = pl.program_id(2)
        av = a_ref[...]
        if lhs_fn is not None:
            av = lhs_fn(av.astype(F32))
        bv = b_ref[...]
        if bv.ndim == 3:
            bv = bv.reshape(tk, tn) if b_rows is not None else jnp.concatenate([bv[g] for g in range(bv.shape[0])], axis=1)
        part = lax.dot_general(av.astype(BF16), bv.astype(BF16), dims, preferred_element_type=F32)

        def finish(r):
            if epi is not None:
                r = epi(r, x_ref[...])
            if out_cols:
                width = tn // o_shards
                for g in range(o_shards):
                    o_ref[g] = r[:, g * width:(g + 1) * width].astype(o_ref.dtype)
            else:
                o_ref[...] = r.astype(o_ref.dtype)

        if nk == 1:
            finish(part)
            return

        @pl.when(k == 0)
        def _():
            acc[...] = part

        @pl.when((k > 0) & (k < nk - 1))
        def _():
            acc[...] += part

        @pl.when(k == nk - 1)
        def _():
            finish(acc[...] + part)

    ops, specs = [a, b], [a_spec, b_spec]
    if extra is not None:
        ops.append(extra)
        specs.append(o_spec)
    return pl.pallas_call(
        body, name=name, grid=grid, in_specs=specs, out_specs=o_place,
        out_shape=jax.ShapeDtypeStruct(o_shape, out_dtype),
        scratch_shapes=[pltpu.VMEM((tm, tn), F32)],
        compiler_params=_cparams(3),
    )(*ops)


def _relu2(x):
    r = jnp.maximum(x, 0.0)
    return r * r


def _mix_fn(x, halo, mu, w0, w1, w2, a0, a1, a2, g1, g2):
    prev_row = jnp.where(pl.program_id(0) == 0, 0.0, halo[7:8, :])
    xx = _shift_down(x, prev_row) - x
    xr, xw, xk, xv, xa, xg = (x + xx * mu[i:i + 1, :] for i in range(6))
    w_pre = w0 + _mm(jnp.tanh(_mm(xw, w1)), w2)
    log_decay = -jnp.exp(-_softplus(-w_pre) - 0.5)
    a = _sigmoid(a0 + _mm(_mm(xa, a1), a2))
    g = _mm(_sigmoid(_mm(xg, g1)), g2)
    return xr, xk, xv, log_decay, a, g


def _chunk_fn(r, lw, k, v, a, k_k, k_a, r_k):
    n, n_pair = r.shape[0], r.shape[1] // PAIR
    to_batch = lambda t: jnp.concatenate([t[None, :, p * PAIR:(p + 1) * PAIR] for p in range(n_pair)], axis=0)
    from_batch = lambda t: jnp.concatenate([t[p] for p in range(n_pair)], axis=1)
    r, lw, k, v, a, k_k, k_a, r_k = (to_batch(t) for t in (r, lw, k, v, a, k_k, k_a, r_k))
    lane = lax.broadcasted_iota(jnp.int32, (1, 1, PAIR), 2)
    m0 = lane < HEAD

    def head_sum(t):
        s0 = jnp.sum(jnp.where(m0, t, 0.0), axis=2, keepdims=True)
        s1 = jnp.sum(jnp.where(m0, 0.0, t), axis=2, keepdims=True)
        return jnp.where(m0, s0, s1)

    kk = k * k_k
    kk = kk / jnp.maximum(jnp.sqrt(head_sum(kk * kk)), 1e-12)
    k2 = k * (1.0 + (a - 1.0) * k_a)
    aa, bb = -kk, kk * a
    bonus = head_sum(r * k2 * r_k) * v

    cum = _cumsum_rows(lw)
    p_in, p_ex, p_inv = jnp.exp(cum), jnp.exp(cum - lw), jnp.exp(-cum)
    at, rt, kt, bt = aa * p_ex, r * p_in, k2 * p_inv, bb * p_inv

    def stack_masked(t):
        return jnp.concatenate([jnp.where(m0, t, 0.0), jnp.where(m0, 0.0, t)], axis=1)

    def unstack_sum(t):
        return t[:, :n] + t[:, n:]

    def unstack_select(t):
        return jnp.where(m0, t[:, :n], t[:, n:])

    ti = lax.broadcasted_iota(jnp.int32, (1, 2 * n, 2 * n), 1)
    tj = lax.broadcasted_iota(jnp.int32, (1, 2 * n, 2 * n), 2)
    same = (ti >= n) == (tj >= n)
    incl, strict = same & (ti >= tj), same & (ti > tj)
    m = 2 * n
    rows = lambda *ts: jnp.concatenate(ts, axis=1)
    cols = lambda *ts: jnp.concatenate(ts, axis=2)
    at_s, rt_s, kt_s, bt_s = stack_masked(at), stack_masked(rt), stack_masked(kt), stack_masked(bt)
    v_s, zeros = rows(v, v), jnp.zeros_like(at_s)
    bk_s = rows(bt_s, kt_s)
    gram = _pdot(rows(at_s, rt_s), bk_s, 'nt')
    a_ab = jnp.where(strict, gram[:, :m, :m], 0.0)
    a_ak = jnp.where(strict, gram[:, :m, m:], 0.0)
    a_rb = jnp.where(incl, gram[:, m:, :m], 0.0)
    a_rk = jnp.where(incl, gram[:, m:, m:], 0.0)
    inv = _nilpotent_inverse(a_ab, n)
    wy_s = _pdot(inv, cols(at_s, _pdot(a_ak, v_s, 'nn')), 'nn')
    q_oi = _pdot(cols(a_rb, a_rk), rows(wy_s, cols(zeros, v_s)), 'nn')
    q_all = unstack_sum(rt_s + q_oi[:, :, :PAIR])
    oi_all = unstack_select(q_oi[:, :, PAIR:])
    own_head = (lax.broadcasted_iota(jnp.int32, (1, m, PAIR), 1) >= n) == (lax.broadcasted_iota(jnp.int32, (1, m, PAIR), 2) >= HEAD)
    wy_m = cols(wy_s[:, :, :PAIR], jnp.where(own_head, wy_s[:, :, PAIR:], 0.0))
    mm_nn = _pdot(bk_s, rows(wy_m, cols(zeros, stack_masked(v))), 'tn')
    mm, nn = mm_nn[:, :, :PAIR], mm_nn[:, :, PAIR:]
    ei = lax.broadcasted_iota(jnp.int32, (1, PAIR, PAIR), 1)
    ej = lax.broadcasted_iota(jnp.int32, (1, PAIR, PAIR), 2)
    eye_p = (ei == ej).astype(F32)
    decay_col = jnp.sum(eye_p * p_in[:, n - 1:n, :], axis=2, keepdims=True)
    return ((decay_col * (eye_p + mm))[None], (decay_col * nn)[None], from_batch(q_all), from_batch(oi_all),
            from_batch(bonus))


def _gate_fn(o, bonus, g, lnx_g, lnx_b):
    lane = lax.broadcasted_iota(jnp.int32, (1, PAIR), 1)
    m0 = lane < HEAD

    def head_mean(t):
        s0 = jnp.sum(jnp.where(m0, t, 0.0), axis=1, keepdims=True)
        s1 = jnp.sum(jnp.where(m0, 0.0, t), axis=1, keepdims=True)
        return jnp.where(m0, s0, s1) * (1.0 / HEAD)

    outs = []
    for p in range(N_PAIR):
        sl = slice(p * PAIR, (p + 1) * PAIR)
        oc = o[:, sl] - head_mean(o[:, sl])
        on = oc * lax.rsqrt(head_mean(oc * oc) + GN_EPS) * lnx_g[:, sl] + lnx_b[:, sl]
        outs.append((on + bonus[:, sl]) * g[:, sl])
    return (jnp.concatenate(outs, axis=1),)


def _res_ln_fn(h, y, g, b):
    return (_layer_norm(DN_ALPHA * h + y, g, b),)


def _glu_ln_fn(h, z, g, b):
    mix = z[:, :D_MODEL] * _sigmoid(z[:, D_MODEL:])
    return (_layer_norm(DN_ALPHA * h + mix, g, b),)


def _s5_param_fn(a_re, a_im, log_dt, b_re, b_im):
    dt = jnp.exp(log_dt)
    lam_re = jnp.minimum(a_re, -1e-4)
    lam_im = a_im
    mag = jnp.exp(dt * lam_re)
    abar_re = mag * jnp.cos(dt * lam_im)
    abar_im = mag * jnp.sin(dt * lam_im)
    den = lam_re * lam_re + lam_im * lam_im
    nr, ni = abar_re - 1.0, abar_im
    coef_re = (nr * lam_re + ni * lam_im) / den
    coef_im = (ni * lam_re - nr * lam_im) / den
    return abar_re, abar_im, coef_re, coef_im


def _s5_bbar_fn(coef_re, coef_im, b_re, b_im):
    return coef_re * b_re - coef_im * b_im, coef_re * b_im + coef_im * b_re


def _s5_in_fn(u, b_re, b_im):
    return _pdot(u, b_re[0], 'nn'), _pdot(u, b_im[0], 'nn')


def _s5_out_fn(s_re, s_im, u, c_re, c_im, d):
    y = _pdot(s_re, c_re[0], 'nn') - _pdot(s_im, c_im[0], 'nn') + u * d
    return (_gelu(y),)


def _pairs_to_batch(t):
    return jnp.concatenate([t[None, :, p * PAIR:(p + 1) * PAIR] for p in range(N_PAIR)], axis=0)


def _batch_to_pairs(t):
    return jnp.concatenate([t[p] for p in range(N_PAIR)], axis=1)


STATE_PASSES = 3


def _rwkv_scan_fwd(r, lw, k, v, a, k_k, k_a, r_k):
    n_tok = r.shape[0]
    n_chunk = n_tok // CHUNK

    def body(r_ref, lw_ref, k_ref, v_ref, a_ref, kk_ref, ka_ref, rk_ref, o_ref, bonus_ref, zs_ref, z):
        @pl.when(pl.program_id(0) == 0)
        def _():
            z[...] = jnp.zeros(z.shape, F32)

        m_all, n_all, q, oi, bonus = _chunk_fn(r_ref[...], lw_ref[...], k_ref[...], v_ref[...], a_ref[...],
                                               kk_ref[...], ka_ref[...], rk_ref[...])
        zb = z[...]
        zs_ref[0] = zb
        bonus_ref[...] = bonus
        o_ref[...] = _batch_to_pairs(_dot_passes(_pairs_to_batch(q), zb, 'nn', STATE_PASSES)) + oi
        z[...] = _dot_passes(m_all[0], zb, 'nn', STATE_PASSES) + n_all[0]

    tok = _spec((CHUNK, D_MODEL), lambda c: (c, 0))
    par = _spec((1, D_MODEL), lambda c: (0, 0))
    st = _spec((1, N_PAIR, PAIR, PAIR), lambda c: (c, 0, 0, 0))
    return pl.pallas_call(
        body, name="rwkv_scan_fwd", grid=(n_chunk,),
        in_specs=[tok] * 5 + [par] * 3, out_specs=[tok, tok, st],
        out_shape=[jax.ShapeDtypeStruct((n_tok, D_MODEL), F32)] * 2 + [jax.ShapeDtypeStruct((n_chunk, N_PAIR, PAIR, PAIR), F32)],
        scratch_shapes=[pltpu.VMEM((N_PAIR, PAIR, PAIR), F32)],
        compiler_params=_cparams(1),
    )(r, lw, k, v, a, k_k, k_a, r_k)


def _rwkv_scan_bwd(r, lw, k, v, a, k_k, k_a, r_k, zs, d_o, d_bonus):
    n_tok = r.shape[0]
    n_chunk = n_tok // CHUNK

    def body(r_ref, lw_ref, k_ref, v_ref, a_ref, kk_ref, ka_ref, rk_ref, zs_ref, do_ref, db_ref,
             dr_ref, dlw_ref, dk_ref, dv_ref, da_ref, dkk_ref, dka_ref, drk_ref, g):
        sums = (dkk_ref, dka_ref, drk_ref)

        @pl.when(pl.program_id(0) == 0)
        def _():
            g[...] = jnp.zeros(g.shape, F32)
            for s_ref in sums:
                s_ref[...] = jnp.zeros(s_ref.shape, F32)

        prim = (r_ref[...], lw_ref[...], k_ref[...], v_ref[...], a_ref[...], kk_ref[...], ka_ref[...], rk_ref[...])
        (m_all, _, q, _, _), chunk_vjp = jax.vjp(_chunk_fn, *prim)
        gb, zb, d_out = g[...], zs_ref[0], do_ref[...]
        dob = _pairs_to_batch(d_out)
        d_m = _dot_passes(gb, zb, 'nt', STATE_PASSES)
        d_q = _batch_to_pairs(_dot_passes(dob, zb, 'nt', STATE_PASSES))
        grads = chunk_vjp((d_m[None], gb[None], d_q, d_out, db_ref[...]))
        for out_ref, val in zip((dr_ref, dlw_ref, dk_ref, dv_ref, da_ref), grads[:5]):
            out_ref[...] = val.astype(out_ref.dtype)
        for s_ref, val in zip(sums, grads[5:]):
            s_ref[...] += val
        same_head = (lax.broadcasted_iota(jnp.int32, (1, PAIR, PAIR), 1) // HEAD
                     == lax.broadcasted_iota(jnp.int32, (1, PAIR, PAIR), 2) // HEAD)
        g[...] = (_dot_passes(m_all[0], gb, 'tn', STATE_PASSES)
                  + jnp.where(same_head, _dot_passes(_pairs_to_batch(q), dob, 'tn', STATE_PASSES), 0.0))

    tok = _spec((CHUNK, D_MODEL), lambda c: (n_chunk - 1 - c, 0))
    par = _spec((1, D_MODEL), lambda c: (0, 0))
    st = _spec((1, N_PAIR, PAIR, PAIR), lambda c: (n_chunk - 1 - c, 0, 0, 0))
    tok_shape = lambda dt: jax.ShapeDtypeStruct((n_tok, D_MODEL), dt)
    return pl.pallas_call(
        body, name="rwkv_scan_bwd", grid=(n_chunk,),
        in_specs=[tok] * 5 + [par] * 3 + [st, tok, tok], out_specs=[tok] * 5 + [par] * 3,
        out_shape=[tok_shape(BF16), tok_shape(F32), tok_shape(BF16), tok_shape(BF16), tok_shape(F32)]
        + [jax.ShapeDtypeStruct((1, D_MODEL), F32)] * 3,
        scratch_shapes=[pltpu.VMEM((N_PAIR, PAIR, PAIR), F32)],
        compiler_params=_cparams(1),
    )(r, lw, k, v, a, k_k, k_a, r_k, zs, d_o, d_bonus)


S5_TIME_TILE = 512
S5_GROUP = 8


def _scan_rows(re_ref, im_ref, a1, carry, reverse):
    lb = re_ref.shape[1]
    grp, n_grp = S5_GROUP, re_ref.shape[0] // S5_GROUP

    def cmul(xr, xi, yr, yi):
        return xr * yr - xi * yi, xr * yi + xi * yr

    a2 = cmul(*a1, *a1)
    a4 = cmul(*a2, *a2)
    a8 = cmul(*a4, *a4)
    row = lax.broadcasted_iota(jnp.int32, (grp, lb), 0)
    expo = (grp - row) if reverse else (row + 1)
    pw = (jnp.ones((grp, lb), F32), jnp.zeros((grp, lb), F32))
    for bit, ap in ((1, a1), (2, a2), (4, a4), (8, a8)):
        nxt = cmul(*pw, *ap)
        sel = (expo & bit) != 0
        pw = (jnp.where(sel, nxt[0], pw[0]), jnp.where(sel, nxt[1], pw[1]))

    steps = []
    for d, ad in ((1, a1), (2, a2), (4, a4)):
        keep = (row < grp - d) if reverse else (row >= d)
        steps.append(((grp - d) if reverse else d, jnp.where(keep, ad[0], 0.0), jnp.where(keep, ad[1], 0.0)))

    def group(i, c):
        gi = (n_grp - 1 - i) if reverse else i
        rows = pl.ds(pl.multiple_of(gi * grp, grp), grp)
        xr, xi = re_ref[rows, :], im_ref[rows, :]
        for shift, mr, mi in steps:
            pr, pi = cmul(mr, mi, pltpu.roll(xr, shift, 0), pltpu.roll(xi, shift, 0))
            xr, xi = xr + pr, xi + pi
        cr, ci = cmul(*pw, c[0], c[1])
        xr, xi = xr + cr, xi + ci
        re_ref[rows, :] = xr
        im_ref[rows, :] = xi
        edge = slice(0, 1) if reverse else slice(grp - 1, grp)
        return xr[edge, :], xi[edge, :]

    return lax.fori_loop(0, n_grp, group, carry)


def _s5_fwd(u, b_re, b_im, a_re, a_im, c_re, c_im, d_row):
    n_tok = u.shape[0]
    tt = min(S5_TIME_TILE, n_tok)

    def body(u_ref, bre_ref, bim_ref, are_ref, aim_ref, cre_ref, cim_ref, d_ref, sre_ref, sim_ref, yg_ref, carry):
        @pl.when(pl.program_id(1) == 0)
        def _():
            carry[...] = jnp.zeros(carry.shape, F32)

        uv = u_ref[...]
        sre_ref[...], sim_ref[...] = _s5_in_fn(uv, bre_ref[...], bim_ref[...])
        c = _scan_rows(sre_ref, sim_ref, (are_ref[...], aim_ref[...]), (carry[0:1, :], carry[1:2, :]), False)
        carry[0:1, :] = c[0]
        carry[1:2, :] = c[1]
        (yg,) = _s5_out_fn(sre_ref[...], sim_ref[...], uv, cre_ref[...], cim_ref[...], d_ref[...])
        yg_ref[...] = yg.astype(yg_ref.dtype)

    u_blk = _spec((tt, SSM_BLK_IN), lambda l, t: (t, l))
    s_blk = _spec((tt, SSM_BLK_ST), lambda l, t: (t, l))
    blk3 = lambda arr: _spec((1,) + arr.shape[1:], lambda l, t: (l, 0, 0))
    par = lambda width: _spec((1, width), lambda l, t: (0, l))
    return pl.pallas_call(
        body, name="s5_fwd", grid=(N_SSM_BLK, n_tok // tt),
        in_specs=[u_blk, blk3(b_re), blk3(b_im), par(SSM_BLK_ST), par(SSM_BLK_ST), blk3(c_re), blk3(c_im), par(SSM_BLK_IN)],
        out_specs=[s_blk, s_blk, u_blk],
        out_shape=[jax.ShapeDtypeStruct((n_tok, SSM_LANES), F32)] * 2 + [jax.ShapeDtypeStruct((n_tok, D_MODEL), BF16)],
        scratch_shapes=[pltpu.VMEM((2, SSM_BLK_ST), F32)],
        compiler_params=_cparams(2),
    )(u, b_re, b_im, a_re, a_im, c_re, c_im, d_row)


def _s5_bwd(dyg, u, s_re, s_im, dh_res, b_re, b_im, a_re, a_im, c_re, c_im, d_row):
    n_tok = u.shape[0]
    tt = min(S5_TIME_TILE, n_tok)
    n_t = n_tok // tt

    def body(dyg_ref, u_ref, sre_ref, sim_ref, hre_ref, him_ref, res_ref, bre_ref, bim_ref, are_ref, aim_ref,
             cre_ref, cim_ref, d_ref, du_ref, dbre_ref, dbim_ref, dcre_ref, dcim_ref, dd_ref, dare_ref, daim_ref,
             carry, gre, gim):
        i = pl.program_id(1)
        sums = (dbre_ref, dbim_ref, dcre_ref, dcim_ref, dd_ref, dare_ref, daim_ref)

        @pl.when(i == 0)
        def _():
            carry[...] = jnp.zeros(carry.shape, F32)
            for r in sums:
                r[...] = jnp.zeros(r.shape, F32)

        uv, sre, sim = u_ref[...], sre_ref[...], sim_ref[...]
        _, out_vjp = jax.vjp(_s5_out_fn, sre, sim, uv, cre_ref[...], cim_ref[...], d_ref[...])
        gre[...], gim[...], du_out, dcre, dcim, dd = out_vjp((dyg_ref[...],))
        c = _scan_rows(gre, gim, (are_ref[...], -aim_ref[...]), (carry[0:1, :], carry[1:2, :]), True)
        carry[0:1, :] = c[0]
        carry[1:2, :] = c[1]
        g_re, g_im = gre[...], gim[...]
        at_start = i == n_t - 1
        p_re = _shift_down(sre, jnp.where(at_start, 0.0, hre_ref[7:8, :]))
        p_im = _shift_down(sim, jnp.where(at_start, 0.0, him_ref[7:8, :]))
        _, in_vjp = jax.vjp(_s5_in_fn, uv, bre_ref[...], bim_ref[...])
        du_in, dbre, dbim = in_vjp((g_re, g_im))
        du_ref[...] = du_out + du_in + res_ref[...]
        dare = jnp.sum(g_re * p_re + g_im * p_im, axis=0, keepdims=True)
        daim = jnp.sum(g_im * p_re - g_re * p_im, axis=0, keepdims=True)
        for r, val in zip(sums, (dbre, dbim, dcre, dcim, dd, dare, daim)):
            r[...] += val

    u_blk = _spec((tt, SSM_BLK_IN), lambda l, t: (n_t - 1 - t, l))
    s_blk = _spec((tt, SSM_BLK_ST), lambda l, t: (n_t - 1 - t, l))
    halo = _spec((8, SSM_BLK_ST), lambda l, t: (jnp.maximum((n_t - 1 - t) * (tt // 8) - 1, 0), l))
    blk3 = lambda arr: _spec((1,) + arr.shape[1:], lambda l, t: (l, 0, 0))
    par = lambda width: _spec((1, width), lambda l, t: (0, l))
    params = [blk3(b_re), blk3(b_im), par(SSM_BLK_ST), par(SSM_BLK_ST), blk3(c_re), blk3(c_im), par(SSM_BLK_IN)]
    shape = lambda arr: jax.ShapeDtypeStruct(arr.shape, F32)
    return pl.pallas_call(
        body, name="s5_bwd", grid=(N_SSM_BLK, n_t),
        in_specs=[u_blk, u_blk, s_blk, s_blk, halo, halo, u_blk] + params,
        out_specs=[u_blk, blk3(b_re), blk3(b_im), blk3(c_re), blk3(c_im), par(SSM_BLK_IN), par(SSM_BLK_ST), par(SSM_BLK_ST)],
        out_shape=[shape(u), shape(b_re), shape(b_im), shape(c_re), shape(c_im), shape(d_row), shape(a_re), shape(a_im)],
        scratch_shapes=[pltpu.VMEM((2, SSM_BLK_ST), F32), pltpu.VMEM((tt, SSM_BLK_ST), F32), pltpu.VMEM((tt, SSM_BLK_ST), F32)],
        compiler_params=_cparams(2),
    )(dyg, u, s_re, s_im, s_re, s_im, dh_res, b_re, b_im, a_re, a_im, c_re, c_im, d_row)


def _loss_call(h, target, tm):
    n_tok, d = h.shape

    def body(h_ref, t_ref, acc_ref, dh_ref):
        @pl.when(pl.program_id(0) == 0)
        def _():
            acc_ref[...] = jnp.zeros(acc_ref.shape, F32)

        e = h_ref[...] - t_ref[...]
        dh_ref[...] = e * (1.0 / d)
        acc_ref[...] += jnp.sum(jnp.sum(e * e, axis=1, keepdims=True), axis=0, keepdims=True)

    tok = _spec((tm, d), lambda i: (i, 0))
    return pl.pallas_call(
        body, name="loss", grid=(n_tok // tm,),
        in_specs=[tok, tok], out_specs=[_spec((8, 128), lambda i: (0, 0)), tok],
        out_shape=[jax.ShapeDtypeStruct((8, 128), F32), jax.ShapeDtypeStruct(h.shape, F32)],
        compiler_params=_cparams(1),
    )(h, target)


def _block_diag(t):
    nb, ng, rows, cols = t.shape
    eye = jnp.eye(ng, dtype=t.dtype)
    return jnp.einsum('bgrc,gh->bgrhc', t, eye).reshape(nb, ng * rows, ng * cols)


def _block_diag_t(t, rows, cols):
    nb, ng = t.shape[0], t.shape[1] // rows
    t5 = t.reshape(nb, ng, rows, ng, cols)
    return jnp.einsum('bgrhc,gh->bgrc', t5, jnp.eye(ng, dtype=t.dtype))


def _mlp_fwd(layer, h, w1g, w2g):
    pre = _matmul(f"mlp{layer}_up", h, w1g, 'nn', b_cols=(layer,), out_dtype=BF16)
    return pre, _matmul(f"mlp{layer}_down", pre, w2g, 'nn', lhs_fn=_relu2, b_rows=(layer,))


def _mlp_bwd(layer, h, pre, dy, dh_res, w1g, w2g):
    dpre = _matmul(f"mlp{layer}_down_dx", dy, w2g, 'nt', epi=lambda acc, p: acc * (2.0 * jnp.maximum(p, 0.0)),
                   extra=pre, b_rows=(layer,), out_dtype=BF16)
    dw2 = _matmul(f"mlp{layer}_down_dw", pre, dy, 'tn', lhs_fn=_relu2, out_dtype=GRAD_DTYPE)
    dw1 = _matmul(f"mlp{layer}_up_dw", h, dpre, 'tn', out_cols=True, out_dtype=GRAD_DTYPE)
    dh = _matmul(f"mlp{layer}_up_dx", dpre, w1g, 'nt', epi=lambda acc, e: acc + e, extra=dh_res, b_cols=(layer,))
    return dh, dw1, dw2


def _local_step(x, target, w, wg, late_weights, exchange):
    n_tok = x.shape[0]
    tm = min(TOKEN_TILE, n_tok)
    n_tile, n_chunk = n_tok // tm, n_tok // CHUNK
    row = lambda v: v.reshape(1, -1)
    big = {n: wg[n].reshape(D_MODEL, D_MODEL) for n in ('rw_wr', 'rw_wk', 'rw_wv', 'rw_wo')}
    ln_g, ln_b = w['ln_g'], w['ln_b']
    grads = {}
    to_slots = lambda t: t.reshape(N_DEV, t.shape[0] // N_DEV, t.shape[1])

    halo_spec = (8, D_MODEL), lambda i: (jnp.maximum(i * (tm // 8) - 1, 0), 0)
    mix_p = [_tile(x, tm), (x,) + halo_spec, _full(w['rw_mu'][0]), _full(row(w['rw_w0'])), _full(w['rw_w1'][0]),
             _full(w['rw_w2'][0]), _full(row(w['rw_a0'])), _full(w['rw_a1'][0]), _full(w['rw_a2'][0]),
             _full(w['rw_g1'][0]), _full(w['rw_g2'][0])]
    tok_out = lambda dt: ((n_tok, D_MODEL), dt, (tm, D_MODEL), lambda i: (i, 0))
    xr, xk, xv, lw, a, g = _fwd_call("rwkv_mix", _mix_fn, (n_tile,), mix_p,
                                     [tok_out(BF16), tok_out(BF16), tok_out(BF16), tok_out(F32), tok_out(F32), tok_out(F32)])
    r = _matmul("rwkv_r", xr, big['rw_wr'], 'nn')
    k = _matmul("rwkv_k", xk, big['rw_wk'], 'nn')
    v = _matmul("rwkv_v", xv, big['rw_wv'], 'nn')

    scan_in = (r, lw, k, v, a, row(w['rw_k_k']), row(w['rw_k_a']), w['rw_r_k'].reshape(1, -1))
    o, bonus, zs = _rwkv_scan_fwd(*scan_in)

    gate_p = [_tile(o, tm), _tile(bonus, tm), _tile(g, tm), _full(row(w['rw_lnx_g'])), _full(row(w['rw_lnx_b']))]
    (og,) = _fwd_call("rwkv_gate", _gate_fn, (n_tile,), gate_p, [tok_out(BF16)])
    y0 = _matmul("rwkv_o", og, big['rw_wo'], 'nn')
    ln0_p = [_tile(x, tm), _tile(y0, tm), _full(ln_g[0:1]), _full(ln_b[0:1])]
    twice = lambda fn: lambda *args: fn(*args) * 2
    h1, h1_b = _fwd_call("ln0", twice(_res_ln_fn), (n_tile,), ln0_p, [tok_out(F32), tok_out(BF16)])
    late = late_weights(h1)
    w1g, w2g, glu_g = late['mlp_w1'], late['mlp_w2'], late['s5_w_glu']
    pre0, y1 = _mlp_fwd(0, h1_b, w1g, w2g)
    ln1_p = [_tile(h1, tm), _tile(y1, tm), _full(ln_g[1:2]), _full(ln_b[1:2])]
    (h2,) = _fwd_call("ln1", _res_ln_fn, (n_tile,), ln1_p, [tok_out(F32)])

    gp = (SSM_GROUPS, SSM_STATE)
    par_p = [_full(w['s5_a_re'][0]), _full(w['s5_a_im'][0]), _full(w['s5_log_dt'].reshape(SSM_GROUPS, 1))]
    s5_par_fn = lambda ar, ai, ld: _s5_param_fn(ar, ai, ld, None, None)
    gp_out = (gp, F32, gp, lambda i: (0, 0))
    abar_re, abar_im, coef_re, coef_im = _fwd_call("s5_param", s5_par_fn, (1,), par_p, [gp_out] * 4)
    b_flat = [w[n][0].reshape(SSM_LANES, SSM_GROUP) for n in ('s5_b_re', 's5_b_im')]
    bbar_p = [_full(coef_re.reshape(SSM_LANES, 1)), _full(coef_im.reshape(SSM_LANES, 1)), _full(b_flat[0]), _full(b_flat[1])]
    bb_out = ((SSM_LANES, SSM_GROUP), F32, (SSM_LANES, SSM_GROUP), lambda i: (0, 0))
    bbar_re, bbar_im = _fwd_call("s5_bbar", _s5_bbar_fn, (1,), bbar_p, [bb_out] * 2)
    to_in = lambda t: _block_diag(t.reshape(N_SSM_BLK, 8, SSM_STATE, SSM_GROUP).transpose(0, 1, 3, 2))
    to_out = lambda t: _block_diag(t.reshape(N_SSM_BLK, 8, SSM_GROUP, SSM_STATE).transpose(0, 1, 3, 2))
    bblk_re, bblk_im = to_in(bbar_re), to_in(bbar_im)
    cblk_re, cblk_im = to_out(w['s5_c_re'][0]), to_out(w['s5_c_im'][0])
    a_row_re, a_row_im = abar_re.reshape(1, SSM_LANES), abar_im.reshape(1, SSM_LANES)

    d_row = row(w['s5_d'])
    s5_params = (bblk_re, bblk_im, a_row_re, a_row_im, cblk_re, cblk_im, d_row)
    s_re, s_im, yg = _s5_fwd(h2, *s5_params)
    z = _matmul("s5_glu", yg, glu_g, 'nn', b_cols=(0,))
    ln2_p = [_tile(h2, tm), _tile(z, tm), _full(ln_g[2:3]), _full(ln_b[2:3])]
    h3, h3_b = _fwd_call("ln2", twice(_glu_ln_fn), (n_tile,), ln2_p, [tok_out(F32), tok_out(BF16)])
    pre1, y3 = _mlp_fwd(1, h3_b, w1g, w2g)
    ln3_p = [_tile(h3, tm), _tile(y3, tm), _full(ln_g[3:4]), _full(ln_b[3:4])]
    (h4,) = _fwd_call("ln3", _res_ln_fn, (n_tile,), ln3_p, [tok_out(F32)])

    loss_acc, dh4 = _loss_call(h4, target, tm)

    ln_w = ['tile', 'tile_bf16', 'acc', 'acc']
    dh3_res, dy3, dg3, db3 = _bwd_call("ln3_bwd", _res_ln_fn, (n_tile,), ln3_p, [_tile(dh4, tm)], ln_w)
    dh3, dw1_1, dw2_1 = _mlp_bwd(1, h3_b, pre1, dy3, dh3_res, w1g, w2g)
    zero = exchange([('mlp_w1', 1, dw1_1), ('mlp_w2', 1, to_slots(dw2_1))])
    ln2_p[2] = _full(ln_g[2:3] + zero)
    dh2_res, dz, dg2, db2 = _bwd_call("ln2_bwd", _glu_ln_fn, (n_tile,), ln2_p, [_tile(dh3, tm)], ln_w)
    dyg = _matmul("s5_glu_dx", dz, glu_g, 'nt', b_cols=(0,))
    dw_glu = _matmul("s5_glu_dw", yg, dz, 'tn', out_cols=True, out_dtype=GRAD_DTYPE)
    dh2, dbb_re, dbb_im, dcb_re, dcb_im, dd, dabar_re, dabar_im = _s5_bwd(dyg, h2, s_re, s_im, dh2_res, *s5_params)
    from_in = lambda t: _block_diag_t(t, SSM_GROUP, SSM_STATE).transpose(0, 1, 3, 2).reshape(SSM_LANES, SSM_GROUP)
    from_out = lambda t: _block_diag_t(t, SSM_STATE, SSM_GROUP).transpose(0, 1, 3, 2).reshape(1, SSM_GROUPS, SSM_GROUP, SSM_STATE)
    grads['s5_c_re'], grads['s5_c_im'] = from_out(dcb_re), from_out(dcb_im)
    grads['s5_d'] = dd
    dcoef_re, dcoef_im, db_re, db_im = _bwd_call(
        "s5_bbar_bwd", _s5_bbar_fn, (1,), bbar_p, [_full(from_in(dbb_re)), _full(from_in(dbb_im))], ['acc'] * 4)
    grads['s5_b_re'] = db_re.reshape(w['s5_b_re'].shape)
    grads['s5_b_im'] = db_im.reshape(w['s5_b_im'].shape)
    par_ct = [_full(dabar_re.reshape(gp)), _full(dabar_im.reshape(gp)), _full(dcoef_re.reshape(gp)), _full(dcoef_im.reshape(gp))]
    da_re, da_im, dlog_dt = _bwd_call("s5_param_bwd", s5_par_fn, (1,), par_p, par_ct, ['acc'] * 3)
    grads['s5_a_re'], grads['s5_a_im'], grads['s5_log_dt'] = da_re[None], da_im[None], dlog_dt.reshape(1, SSM_GROUPS)
    zero = exchange([(n, 0, grads[n]) for n in ('s5_a_re', 's5_a_im', 's5_log_dt', 's5_b_re', 's5_b_im', 's5_c_re', 's5_c_im')])
    ln1_p[2] = _full(ln_g[1:2] + zero)

    dh1_res, dy1, dg1, db1 = _bwd_call("ln1_bwd", _res_ln_fn, (n_tile,), ln1_p, [_tile(dh2, tm)], ln_w)
    dh1, dw1_0, dw2_0 = _mlp_bwd(0, h1_b, pre0, dy1, dh1_res, w1g, w2g)
    dx_res, dy0, dg0, db0 = _bwd_call("ln0_bwd", _res_ln_fn, (n_tile,), ln0_p, [_tile(dh1, tm)], ln_w)
    grads['ln_g'] = jnp.concatenate([dg0, dg1, dg2, dg3])
    grads['ln_b'] = jnp.concatenate([db0, db1, db2, db3])
    dog = _matmul("rwkv_o_dx", dy0, big['rw_wo'], 'nt')
    dw_o = to_slots(_matmul("rwkv_o_dw", og, dy0, 'tn', out_dtype=GRAD_DTYPE))
    zero = exchange([('s5_w_glu', 0, dw_glu), ('mlp_w1', 0, dw1_0), ('mlp_w2', 0, to_slots(dw2_0)), ('rw_wo', 0, dw_o)])
    gate_p[3] = _full(row(w['rw_lnx_g']) + zero)
    d_o, dbonus, dgate, grads['rw_lnx_g'], grads['rw_lnx_b'] = _bwd_call(
        "rwkv_gate_bwd", _gate_fn, (n_tile,), gate_p, [_tile(dog, tm)], ['tile', 'tile', 'tile', 'acc', 'acc'])
    dr, dlw, dk, dv, da, grads['rw_k_k'], grads['rw_k_a'], dr_k = _rwkv_scan_bwd(*scan_in, zs, d_o, dbonus)
    grads['rw_r_k'] = dr_k.reshape(w['rw_r_k'].shape)
    dw_r = to_slots(_matmul("rwkv_r_dw", xr, dr, 'tn', out_dtype=GRAD_DTYPE))
    dw_k = to_slots(_matmul("rwkv_k_dw", xk, dk, 'tn', out_dtype=GRAD_DTYPE))
    dw_v = to_slots(_matmul("rwkv_v_dw", xv, dv, 'tn', out_dtype=GRAD_DTYPE))
    done = ('ln_g', 'ln_b', 'rw_lnx_g', 'rw_lnx_b', 'rw_k_k', 'rw_k_a', 'rw_r_k')
    zero = exchange([('rw_wr', 0, dw_r), ('rw_wk', 0, dw_k), ('rw_wv', 0, dw_v)] + [(n, 0, grads[n]) for n in done])
    mix_p[3] = _full(row(w['rw_w0']) + zero)
    dxr = _matmul("rwkv_r_dx", dr, big['rw_wr'], 'nt')
    dxk = _matmul("rwkv_k_dx", dk, big['rw_wk'], 'nt')
    dxv = _matmul("rwkv_v_dx", dv, big['rw_wv'], 'nt')
    mix_ct = [_tile(t, tm) for t in (dxr, dxk, dxv, dlw, da, dgate)]
    halo_grad = ('tile', ((n_tile * 8, D_MODEL), (8, D_MODEL), lambda i: (i, 0)))
    res = _bwd_call("rwkv_mix_bwd", _mix_fn, (n_tile,), mix_p, mix_ct, ['tile', halo_grad] + ['acc'] * 9,
                    addends={0: _tile(dx_res, tm)})
    corr = res[1].reshape(n_tile, 8, D_MODEL)[1:, 7:8, :]
    corr = jnp.pad(corr, ((0, 1), (tm - 1, 0), (0, 0)))
    dx = (res[0].reshape(n_tile, tm, D_MODEL) + corr).reshape(n_tok, D_MODEL)
    (grads['rw_mu'], grads['rw_w0'], grads['rw_w1'], grads['rw_w2'], grads['rw_a0'], grads['rw_a1'], grads['rw_a2'],
     grads['rw_g1'], grads['rw_g2']) = [t[None] if t.shape[0] != 1 else t for t in res[2:]]
    return loss_acc[0, 0], dx, grads


def _full_shape(local_shape, axis):
    return local_shape[:axis] + (N_DEV * local_shape[axis],) + local_shape[axis + 1:]


def _split_shape(local_shape, axis):
    return local_shape[:axis] + (N_DEV, local_shape[axis]) + local_shape[axis + 1:]


def _mesh_pos():
    return lax.axis_index("x"), lax.axis_index("y"), lax.axis_index("c")


def _flip(v, f):
    return 1 - v if f else v


def _hbm_call(name, body, arrays, out_shapes, n_sems):
    n_t = len(arrays)
    hbm = pl.BlockSpec(memory_space=pl.ANY)
    return pl.pallas_call(
        body, name=name, out_shape=out_shapes, in_specs=[hbm] * n_t, out_specs=[hbm] * n_t,
        scratch_shapes=[pltpu.SemaphoreType.DMA((n_t, n_sems)), pltpu.SemaphoreType.DMA((n_t, n_sems)),
                        pltpu.SemaphoreType.DMA((n_t,))],
    )(*arrays)


def _all_gather(name, locals_):
    n_t = len(locals_)

    def body(*refs):
        x_refs, out_refs = refs[:n_t], refs[n_t:2 * n_t]
        send_sems, recv_sems, local_sems = refs[2 * n_t:]
        x, y, c = _mesh_pos()
        me, sibling = (x, y, c), (x, y, 1 - c)
        chips = [(1 - x, y), (x, 1 - y), (1 - x, 1 - y)]

        def copy(t, k, block, to, own=False):
            slot = out_refs[t].at[4 * block[0] + 2 * block[1] + block[2]]
            return pltpu.make_async_remote_copy(
                src_ref=x_refs[t] if own else slot, dst_ref=slot,
                send_sem=send_sems.at[t, k], recv_sem=recv_sems.at[t, k],
                device_id=to, device_id_type=pl.DeviceIdType.MESH)

        mine = [pltpu.make_async_copy(x_refs[t], out_refs[t].at[4 * x + 2 * y + c], local_sems.at[t]) for t in range(n_t)]
        for cp in mine:
            cp.start()
        sent = []
        for t in range(n_t):
            sent.append(copy(t, 0, me, sibling, own=True))
            sent += [copy(t, 1 + j, me, (*chip, c), own=True) for j, chip in enumerate(chips)]
        for cp in sent:
            cp.start()
        for j, chip in enumerate(chips):
            for t in range(n_t):
                copy(t, 1 + j, (*chip, c), me).wait_recv()
                passed = copy(t, 4 + j, (*chip, c), sibling)
                passed.start()
                sent.append(passed)
        for t in range(n_t):
            copy(t, 0, sibling, me).wait_recv()
            for j, chip in enumerate(chips):
                copy(t, 4 + j, (*chip, 1 - c), me).wait_recv()
        for cp in sent:
            cp.wait_send()
        for cp in mine:
            cp.wait()

    outs = [jax.ShapeDtypeStruct((N_DEV,) + a.shape, a.dtype) for a in locals_]
    return _hbm_call(name, body, locals_, outs, 7)


def _all_to_all(name, blocks):
    n_t = len(blocks)

    def body(*refs):
        x_refs, out_refs = refs[:n_t], refs[n_t:2 * n_t]
        send_sems, recv_sems, local_sems = refs[2 * n_t:]
        x, y, c = _mesh_pos()
        my_slot = 4 * x + 2 * y + c
        mine = [pltpu.make_async_copy(x_refs[t].at[my_slot], out_refs[t].at[my_slot], local_sems.at[t]) for t in range(n_t)]
        for cp in mine:
            cp.start()
        copies = []
        for k in range(1, N_DEV):
            peer = (_flip(x, k & 4), _flip(y, k & 2), _flip(c, k & 1))
            peer_slot = 4 * peer[0] + 2 * peer[1] + peer[2]
            for t in range(n_t):
                sems = dict(send_sem=send_sems.at[t, k - 1], recv_sem=recv_sems.at[t, k - 1],
                            device_id=peer, device_id_type=pl.DeviceIdType.MESH)
                cp = pltpu.make_async_remote_copy(src_ref=x_refs[t].at[peer_slot], dst_ref=out_refs[t].at[my_slot], **sems)
                cp.start()
                landed = pltpu.make_async_remote_copy(src_ref=x_refs[t].at[my_slot], dst_ref=out_refs[t].at[peer_slot], **sems)
                copies.append((cp, landed))
        for sent, landed in copies:
            landed.wait_recv()
            sent.wait_send()
        for cp in mine:
            cp.wait()

    outs = [jax.ShapeDtypeStruct(a.shape, a.dtype) for a in blocks]
    return _hbm_call(name, body, blocks, outs, 7)


_HBM = pl.BlockSpec(memory_space=pltpu.HBM)
_SEM = pl.BlockSpec(memory_space=pltpu.SEMAPHORE)
_EFFECT = pltpu.SideEffectType.DATAFLOW_SIDE_EFFECTING


def _peer_copies(x_refs, land_refs, send_sems, recv_sems, gather, mine):
    x, y, c = _mesh_pos()
    my_slot = 4 * x + 2 * y + c
    copies = []
    for k in range(1, N_DEV):
        peer = (_flip(x, k & 4), _flip(y, k & 2), _flip(c, k & 1))
        peer_slot = 4 * peer[0] + 2 * peer[1] + peer[2]
        for t, (x_ref, land) in enumerate(zip(x_refs, land_refs)):
            sem = t * (N_DEV - 1) + k - 1
            copies.append(pltpu.make_async_remote_copy(
                src_ref=x_ref if gather[t] else x_ref.at[peer_slot], dst_ref=land.at[my_slot if mine else peer_slot],
                send_sem=send_sems.at[sem], recv_sem=recv_sems.at[sem], device_id=peer, device_id_type=pl.DeviceIdType.MESH))
    return copies


def _push_start(name, arrays, gather):
    n_t = len(arrays)
    lands = [lax.empty((N_DEV,) + a.shape if whole else a.shape, a.dtype) for a, whole in zip(arrays, gather)]

    def body(*refs):
        x_refs, land_refs = refs[:n_t], refs[n_t:2 * n_t]
        send_sems, recv_sems = refs[2 * n_t], refs[2 * n_t + 1]
        for sent in _peer_copies(x_refs, land_refs, send_sems, recv_sems, gather, True):
            sent.start()
        refs[-1][...] = jnp.zeros((8, 128), F32)

    held = list(arrays) + lands
    sems = pltpu.SemaphoreType.DMA((n_t * (N_DEV - 1),))
    out = pl.pallas_call(
        body, name=name,
        out_shape=(sems, sems, *[pltpu.HBM(a.shape, a.dtype) for a in held], jax.ShapeDtypeStruct((8, 128), F32)),
        in_specs=[_HBM] * (2 * n_t),
        out_specs=(_SEM, _SEM, *[_HBM] * (2 * n_t), pl.BlockSpec(memory_space=pltpu.VMEM)),
        input_output_aliases={i: 2 + i for i in range(2 * n_t)},
        compiler_params=pltpu.CompilerParams(has_side_effects=_EFFECT),
    )(*[pltpu.with_memory_space_constraint(a, pltpu.HBM) for a in held])
    return dict(send=out[0], recv=out[1], held=out[2:2 + 2 * n_t], zero=out[-1][0, 0], gather=gather)


def _push_wait(name, started, after):
    held, gather = started['held'], started['gather']
    n_t = len(held) // 2

    def body(*refs):
        x_refs, land_refs = refs[:n_t], refs[n_t:2 * n_t]
        send_sems, recv_sems = refs[2 * n_t], refs[2 * n_t + 1]
        for sent in _peer_copies(x_refs, land_refs, send_sems, recv_sems, gather, True):
            sent.wait_send()
        for landed in _peer_copies(x_refs, land_refs, send_sems, recv_sems, gather, False):
            landed.wait_recv()

    out = pl.pallas_call(
        body, name=name,
        out_shape=[pltpu.HBM(a.shape, a.dtype) for a in held],
        in_specs=[_HBM] * (2 * n_t) + [_SEM, _SEM, pl.BlockSpec(memory_space=pl.ANY)],
        out_specs=[_HBM] * (2 * n_t),
        input_output_aliases={i: i for i in range(2 * n_t)},
        compiler_params=pltpu.CompilerParams(has_side_effects=_EFFECT),
    )(*held, started['send'], started['recv'], after)
    return out[:n_t], out[n_t:]


ADAM_BLOCK = 128 * 1024


def _adam_update(part, w_ref, m_ref, v_ref, g_ref, d_ref, nm_ref, nv_ref):
    g = part(0)
    for i in range(1, N_DEV):
        g = g + part(i)
    m_new = ADAM_B1 * m_ref[...] + (1.0 - ADAM_B1) * g
    v_new = ADAM_B2 * v_ref[...] + (1.0 - ADAM_B2) * (g * g)
    m_hat = m_new * (1.0 / (1.0 - ADAM_B1 ** ADAM_STEP))
    v_hat = v_new * (1.0 / (1.0 - ADAM_B2 ** ADAM_STEP))
    g_ref[...] = g
    nm_ref[...] = m_new
    nv_ref[...] = v_new
    d_ref[...] = -ADAM_LR * (m_hat / (jnp.sqrt(v_hat) + ADAM_EPS) + ADAM_WD * w_ref[...])


def _adam_many(name, items):
    n_t = len(items)
    n_in = [4 if own is None else 5 for *_, own in items]
    first = [sum(n_in[:t]) for t in range(n_t)]

    def body(*refs):
        ins, outs = refs[:sum(n_in)], refs[sum(n_in):]
        x, y, c = _mesh_pos()
        my_slot = 4 * x + 2 * y + c
        for t in range(n_t):
            s_ref, w_ref, m_ref, v_ref = ins[first[t]:first[t] + 4]
            if n_in[t] == 4:
                part = lambda i: s_ref[i]
            else:
                own_ref = ins[first[t] + 4]
                part = lambda i: jnp.where(my_slot == i, own_ref[...], s_ref[i])
            _adam_update(part, w_ref, m_ref, v_ref, *outs[4 * t:4 * t + 4])

    vmem = pl.BlockSpec(memory_space=pltpu.VMEM)
    out = pl.pallas_call(
        body, name=name, in_specs=[vmem] * sum(n_in), out_specs=[vmem] * (4 * n_t),
        out_shape=[jax.ShapeDtypeStruct(item[1].shape, F32) for item in items for _ in range(4)],
        compiler_params=pltpu.CompilerParams(vmem_limit_bytes=VMEM_LIMIT),
    )(*[a for item in items for a in item if a is not None])
    return [out[4 * t:4 * t + 4] for t in range(n_t)]


def _adam_call(name, slots, w, m, v, own=None):
    n_layer, rows, width = w.shape
    tile = min(rows, ADAM_BLOCK // max(width, 128))

    def body(*refs):
        s_refs = refs[:n_layer]
        own_refs = refs[n_layer:2 * n_layer] if own is not None else None
        w_ref, m_ref, v_ref = refs[-7:-4]
        layer = pl.program_id(0)
        x, y, c = _mesh_pos()
        my_slot = 4 * x + 2 * y + c
        for l in range(n_layer):
            @pl.when(layer == l)
            def _(l=l):
                if own is None:
                    part = lambda i: s_refs[l][i].astype(F32)
                else:
                    part = lambda i: jnp.where(my_slot == i, own_refs[l][...], s_refs[l][i]).astype(F32)
                _adam_update(part, w_ref, m_ref, v_ref, *refs[-4:])

    slot_spec = lambda k: _spec((N_DEV, tile, width), lambda l, i: (0, jnp.where(l == k, i, 0), 0))
    own_spec = lambda k: _spec((tile, width), lambda l, i: (jnp.where(l == k, i, 0), 0))
    blk = _spec((None, tile, width), lambda l, i: (l, i, 0))
    specs = [slot_spec(k) for k in range(n_layer)] + ([own_spec(k) for k in range(n_layer)] if own is not None else [])
    return pl.pallas_call(
        body, name=name, grid=(n_layer, rows // tile),
        in_specs=specs + [blk] * 3, out_specs=[blk] * 4, out_shape=[jax.ShapeDtypeStruct(w.shape, F32)] * 4,
        compiler_params=_cparams(2),
    )(*slots, *(own or []), w, m, v)


def kernel(x, ln_g, ln_b, rw_mu, rw_w0, rw_w1, rw_w2, rw_a0, rw_a1, rw_a2, rw_g1, rw_g2, rw_k_k, rw_k_a, rw_r_k, rw_wr, rw_wk, rw_wv, rw_wo, rw_lnx_g, rw_lnx_b, s5_a_re, s5_a_im, s5_log_dt, s5_b_re, s5_b_im, s5_c_re, s5_c_im, s5_d, s5_w_glu, mlp_w1, mlp_w2, loss_target, m_ln_g, m_ln_b, m_rw_mu, m_rw_w0, m_rw_w1, m_rw_w2, m_rw_a0, m_rw_a1, m_rw_a2, m_rw_g1, m_rw_g2, m_rw_k_k, m_rw_k_a, m_rw_r_k, m_rw_wr, m_rw_wk, m_rw_wv, m_rw_wo, m_rw_lnx_g, m_rw_lnx_b, m_s5_a_re, m_s5_a_im, m_s5_log_dt, m_s5_b_re, m_s5_b_im, m_s5_c_re, m_s5_c_im, m_s5_d, m_s5_w_glu, m_mlp_w1, m_mlp_w2, v_ln_g, v_ln_b, v_rw_mu, v_rw_w0, v_rw_w1, v_rw_w2, v_rw_a0, v_rw_a1, v_rw_a2, v_rw_g1, v_rw_g2, v_rw_k_k, v_rw_k_a, v_rw_r_k, v_rw_wr, v_rw_wk, v_rw_wv, v_rw_wo, v_rw_lnx_g, v_rw_lnx_b, v_s5_a_re, v_s5_a_im, v_s5_log_dt, v_s5_b_re, v_s5_b_im, v_s5_c_re, v_s5_c_im, v_s5_d, v_s5_w_glu, v_mlp_w1, v_mlp_w2):
    given = dict(locals())
    local_w = {n: given[n] for n in WEIGHTS}
    local_m = {n: given["m_" + n] for n in WEIGHTS}
    local_v = {n: given["v_" + n] for n in WEIGHTS}
    small = [n for n in WEIGHTS if n in SHARDED and n not in BIG]

    my_slot = 4 * lax.axis_index("x") + 2 * lax.axis_index("y") + lax.axis_index("c")
    as_bf16 = {n: local_w[n].astype(BF16) for n in BIG}

    late_started = _push_start("weights_late_start", [as_bf16[n] for n in LATE], [True] * len(LATE))
    early = [n for n in BIG if n not in LATE]
    gathered = _all_gather("weights_all_gather", [as_bf16[n] for n in early] + [local_w[n] for n in small])
    early_gathered = dict(zip(early, gathered))
    full = dict(local_w)
    for n, blk in zip(small, gathered[len(early):]):
        full[n] = jnp.moveaxis(blk, 0, SHARDED[n]).reshape(_full_shape(local_w[n].shape, SHARDED[n]))
    full['rw_w0'] = full['rw_w0'] + late_started['zero']

    def late_weights(after):
        _, lands = _push_wait("weights_late_wait", late_started, after)
        own = lambda n: as_bf16[n][None]
        return {n: lax.dynamic_update_slice(land, own(n), (my_slot,) + (0,) * as_bf16[n].ndim) for n, land in zip(LATE, lands)}

    exchanges = []

    def exchange(entries):
        started = _push_start(f"grads_start_{len(exchanges)}", [t for _, _, t in entries], [n in REPLICATED for n, _, _ in entries])
        exchanges.append((entries, started))
        return started['zero']

    loss_sq, dx, grads = _local_step(x[0], loss_target[0], full, early_gathered, late_weights, exchange)
    loss = (0.5 / D_MODEL) * lax.psum(loss_sq, MESH_AXES)

    wide = {'s5_b_re': (SSM_LANES, SSM_GROUP), 's5_b_im': (SSM_LANES, SSM_GROUP),
            's5_c_re': (D_MODEL, SSM_STATE), 's5_c_im': (D_MODEL, SSM_STATE)}
    by_name, many = {}, []
    landed = {n: {} for n in BIG}
    for e, (entries, started) in enumerate(exchanges):
        sources, lands = _push_wait(f"grads_wait_{e}", started, dx)
        for (n, i, _), src, slots in zip(entries, sources, lands):
            if n in wide:
                as_view = lambda t: t.reshape((1,) + wide[n])
                out = _adam_call(f"adamw_{n}", [slots.reshape((N_DEV,) + wide[n])], as_view(local_w[n]), as_view(local_m[n]),
                                 as_view(local_v[n]), own=[src.reshape(wide[n])])
                by_name[n] = [t.reshape(local_w[n].shape) for t in out]
            elif n in REPLICATED:
                many.append((n, (slots, local_w[n], local_m[n], local_v[n], src)))
            else:
                landed[n][i] = (slots, lax.dynamic_index_in_dim(src, my_slot, 0, keepdims=False))
    for n in BIG:
        layers = [landed[n][i] for i in sorted(landed[n])]
        by_name[n] = _adam_call(f"adamw_{n}", [s for s, _ in layers], local_w[n], local_m[n], local_v[n], own=[o for _, o in layers])
    split = lambda n: jnp.moveaxis(grads[n].reshape(_split_shape(local_w[n].shape, SHARDED[n])), SHARDED[n], 0)
    for n, slots in zip(small, _all_to_all("grads_all_to_all", [split(n) for n in small])):
        many.append((n, (slots, local_w[n], local_m[n], local_v[n], None)))
    last = [n for n in REPLICATED if n not in by_name and n not in dict(many)]
    for n, slots in zip(last, _all_gather("grads_all_gather", [grads[n] for n in last])):
        many.append((n, (slots, local_w[n], local_m[n], local_v[n], None)))
    for (n, _), out in zip(many, _adam_many("adamw_small", [item for _, item in many])):
        by_name[n] = out
    results = [by_name[n][k] for k in range(4) for n in WEIGHTS]
    return (loss, dx[None], *results)
```

```python
import functools
import math

import jax
import jax.numpy as jnp
from jax import lax
from jax.experimental import pallas as pl
from jax.experimental.pallas import tpu as pltpu

F32 = jnp.float32
BF16 = jnp.bfloat16
HI = lax.Precision.HIGHEST

D_MODEL = 1024
HEAD = 64
PAIR = 2 * HEAD
N_PAIR = D_MODEL // PAIR
CHUNK = 64
GN_EPS = 64e-5
LN_EPS = 1e-5
SSM_GROUP = 16
SSM_STATE = 64
SSM_GROUPS = D_MODEL // SSM_GROUP
SSM_LANES = SSM_GROUPS * SSM_STATE
SSM_BLK_IN = 128
SSM_BLK_ST = 512
N_SSM_BLK = D_MODEL // SSM_BLK_IN
DEPTH = 2
DN_ALPHA = (2.0 * DEPTH) ** 0.25
ADAM_LR, ADAM_B1, ADAM_B2, ADAM_EPS, ADAM_WD, ADAM_STEP = 0.001, 0.9, 0.999, 1e-08, 0.01, 10
N_DEV = 8
MESH_AXES = ("x", "y", "c")
VMEM_LIMIT = 56 * 1024 * 1024
TOKEN_TILE = 256

SHARDED = {
    "rw_mu": 2, "rw_w1": 1, "rw_w2": 2, "rw_a1": 1, "rw_a2": 2, "rw_g1": 1, "rw_g2": 2,
    "rw_wr": 1, "rw_wk": 1, "rw_wv": 1, "rw_wo": 1, "s5_d": 1, "s5_w_glu": 2, "mlp_w1": 2, "mlp_w2": 1,
}
WEIGHTS = ['ln_g', 'ln_b', 'rw_mu', 'rw_w0', 'rw_w1', 'rw_w2', 'rw_a0', 'rw_a1', 'rw_a2', 'rw_g1', 'rw_g2', 'rw_k_k',
           'rw_k_a', 'rw_r_k', 'rw_wr', 'rw_wk', 'rw_wv', 'rw_wo', 'rw_lnx_g', 'rw_lnx_b', 's5_a_re', 's5_a_im',
           's5_log_dt', 's5_b_re', 's5_b_im', 's5_c_re', 's5_c_im', 's5_d', 's5_w_glu', 'mlp_w1', 'mlp_w2']
REPLICATED = [n for n in WEIGHTS if n not in SHARDED]
BIG = ['rw_wr', 'rw_wk', 'rw_wv', 'rw_wo', 's5_w_glu', 'mlp_w1', 'mlp_w2']
LATE = ['s5_w_glu', 'mlp_w1', 'mlp_w2']
GRAD_DTYPE = BF16


def _cparams(n_grid):
    return pltpu.CompilerParams(dimension_semantics=("arbitrary",) * n_grid, vmem_limit_bytes=VMEM_LIMIT)


@jax.custom_vjp
def _mm(x, w):
    return jnp.dot(x.astype(BF16), w.astype(BF16), preferred_element_type=F32)


def _mm_fwd(x, w):
    return _mm(x, w), (x, w)


def _mm_bwd(res, dy):
    x, w = res
    dyb = dy.astype(BF16)
    dx = lax.dot_general(dyb, w.astype(BF16), (((1,), (1,)), ((), ())), preferred_element_type=F32)
    dw = lax.dot_general(x.astype(BF16), dyb, (((0,), (0,)), ((), ())), preferred_element_type=F32)
    return dx, dw


_mm.defvjp(_mm_fwd, _mm_bwd)


_DOT_DIMS = {'nn': (((1,), (0,)), ((), ())), 'nt': (((1,), (1,)), ((), ())), 'tn': (((0,), (0,)), ((), ()))}
_BATCH_DOT_DIMS = {'nn': (((2,), (1,)), ((0,), (0,))), 'nt': (((2,), (2,)), ((0,), (0,))), 'tn': (((1,), (1,)), ((0,), (0,)))}
CHUNK_PAIRS = 8
CHUNK_PASSES = 1


def _split_bf16(a):
    hi = a.astype(BF16)
    return hi, (a - hi.astype(F32)).astype(BF16)


def _dot_passes(a, b, mode, passes=None):
    dims = (_DOT_DIMS if a.ndim == 2 else _BATCH_DOT_DIMS)[mode]
    dot = lambda p, q: lax.dot_general(p, q, dims, preferred_element_type=F32)
    if (CHUNK_PASSES if passes is None else passes) == 1:
        return dot(a.astype(BF16), b.astype(BF16))
    (ah, al), (bh, bl) = _split_bf16(a), _split_bf16(b)
    return dot(ah, bh) + (dot(ah, bl) + dot(al, bh))


@functools.partial(jax.custom_vjp, nondiff_argnums=(2,))
def _pdot(a, b, mode):
    return _dot_passes(a, b, mode)


def _pdot_fwd(a, b, mode):
    return _dot_passes(a, b, mode), (a, b)


def _pdot_bwd(mode, res, dy):
    a, b = res
    if mode == 'nn':
        return _dot_passes(dy, b, 'nt'), _dot_passes(a, dy, 'tn')
    if mode == 'nt':
        return _dot_passes(dy, b, 'nn'), _dot_passes(dy, a, 'tn')
    return _dot_passes(b, dy, 'nt'), _dot_passes(a, dy, 'nn')


_pdot.defvjp(_pdot_fwd, _pdot_bwd)


def _tri_sum(x, upper):
    nb, n = x.shape[0], x.shape[1]
    ti = lax.broadcasted_iota(jnp.int32, (nb, n, n), 1)
    tj = lax.broadcasted_iota(jnp.int32, (nb, n, n), 2)
    tri = ((ti <= tj) if upper else (ti >= tj)).astype(BF16)
    hi = x.astype(BF16)
    rest = x - hi.astype(F32)
    mid = rest.astype(BF16)
    lo = (rest - mid.astype(F32)).astype(BF16)
    dot = lambda q: lax.dot_general(tri, q, _BATCH_DOT_DIMS['nn'], preferred_element_type=F32)
    return dot(hi) + (dot(mid) + dot(lo))


@jax.custom_vjp
def _cumsum_rows(x):
    return _tri_sum(x, False)


_cumsum_rows.defvjp(lambda x: (_tri_sum(x, False), None), lambda _, dy: (_tri_sum(dy, True),))


def _power_sum(a, order):
    m = a.shape[-1]
    ti = lax.broadcasted_iota(jnp.int32, (1, m, m), 1)
    tj = lax.broadcasted_iota(jnp.int32, (1, m, m), 2)
    pw, total = _dot_passes(a, a, 'nn'), (ti == tj).astype(F32) + a
    for _ in range(int(math.log2(order)) - 2):
        both = _dot_passes(pw, jnp.concatenate([pw, total], axis=2), 'nn')
        pw, total = both[:, :, :m], total + both[:, :, m:]
    return total + _dot_passes(pw, total, 'nn')


@functools.partial(jax.custom_vjp, nondiff_argnums=(1,))
def _nilpotent_inverse(a, order):
    return _power_sum(a, order)


def _nilpotent_inverse_fwd(a, order):
    inv = _power_sum(a, order)
    return inv, inv


def _nilpotent_inverse_bwd(order, inv, g):
    return (_dot_passes(_dot_passes(inv, g, 'tn'), inv, 'nt'),)


_nilpotent_inverse.defvjp(_nilpotent_inverse_fwd, _nilpotent_inverse_bwd)


@jax.custom_vjp
def _known_inverse(a, inv):
    return inv


_known_inverse.defvjp(lambda a, inv: (inv, inv),
                      lambda inv, g: (_dot_passes(_dot_passes(inv, g, 'tn'), inv, 'nt'), jnp.zeros_like(inv)))


@functools.partial(jax.custom_vjp, nondiff_argnums=(1,))
def _roll_rows(x, shift):
    return pltpu.roll(x, shift, 0)


def _roll_rows_fwd(x, shift):
    return pltpu.roll(x, shift, 0), None


def _roll_rows_bwd(shift, _, dy):
    n = dy.shape[0]
    return (pltpu.roll(dy, (n - shift) % n, 0),)


_roll_rows.defvjp(_roll_rows_fwd, _roll_rows_bwd)


def _shift_down(x, first_row):
    row = lax.broadcasted_iota(jnp.int32, x.shape, 0)
    return jnp.where(row == 0, first_row, _roll_rows(x, 1))


def _sigmoid(x):
    return 1.0 / (1.0 + jnp.exp(-x))


def _softplus(x):
    return jnp.maximum(x, 0.0) + jnp.log(1.0 + jnp.exp(-jnp.abs(x)))


def _gelu(x):
    return 0.5 * x * (1.0 + jnp.tanh(math.sqrt(2.0 / math.pi) * (x + 0.044715 * (x * x * x))))


def _layer_norm(x, g, b):
    mu = jnp.mean(x, axis=-1, keepdims=True)
    xc = x - mu
    var = jnp.mean(xc * xc, axis=-1, keepdims=True)
    return xc * lax.rsqrt(var + LN_EPS) * g + b


def _spec(block, index_map):
    return pl.BlockSpec(block, index_map)


def _tile(arr, tm):
    return (arr, (tm, arr.shape[1]), lambda i: (i, 0))


def _full(arr):
    nd = arr.ndim
    return (arr, arr.shape, lambda *_: (0,) * nd)


def _fwd_call(name, fn, grid, prims, outs):
    n_in = len(prims)

    def body(*refs):
        vals = [r[...] for r in refs[:n_in]]
        vals = [v.astype(F32) if v.dtype != F32 else v for v in vals]
        res = fn(*vals)
        for o, r in zip(refs[n_in:], res):
            o[...] = r.astype(o.dtype)

    return pl.pallas_call(
        body, name=name, grid=grid,
        in_specs=[_spec(b, m) for _, b, m in prims],
        out_specs=[_spec(b, m) for _, _, b, m in outs],
        out_shape=[jax.ShapeDtypeStruct(s, d) for s, d, _, _ in outs],
        compiler_params=_cparams(len(grid)),
    )(*[a for a, _, _ in prims])


def _bwd_call(name, fn, grid, prims, cts, wants, addends=None):
    addends = addends or {}
    n_in, n_ct = len(prims), len(cts)
    out_place = {}
    wants, out_dtype = list(wants), {}
    for i, w in enumerate(wants):
        if isinstance(w, tuple):
            wants[i], out_place[i] = w
        elif w == 'tile_bf16':
            wants[i], out_dtype[i] = 'tile', BF16
    diff = [i for i, w in enumerate(wants) if w]
    add_idx = [i for i in diff if i in addends]
    last_axis = len(grid) - 1

    def body(*refs):
        vals = [r[...] for r in refs[:n_in]]
        vals = [v.astype(F32) if v.dtype != F32 else v for v in vals]
        ct_vals = [r[...] for r in refs[n_in:n_in + n_ct]]
        ct_vals = [v.astype(F32) if v.dtype != F32 else v for v in ct_vals]
        add_refs = dict(zip(add_idx, refs[n_in + n_ct:n_in + n_ct + len(add_idx)]))
        out_refs = refs[n_in + n_ct + len(add_idx):]

        def f(*dargs):
            full = list(vals)
            for i, a in zip(diff, dargs):
                full[i] = a
            return tuple(fn(*full))

        _, vjp = jax.vjp(f, *[vals[i] for i in diff])
        grads = vjp(tuple(ct_vals))
        first = pl.program_id(last_axis) == 0
        for o, g, i in zip(out_refs, grads, diff):
            if wants[i] == 'tile':
                if i in add_refs:
                    g = g + add_refs[i][...]
                o[...] = g.astype(o.dtype)
            else:
                @pl.when(first)
                def _(o=o):
                    o[...] = jnp.zeros(o.shape, o.dtype)
                o[...] += g

    ins = list(prims) + list(cts) + [addends[i] for i in add_idx]
    places = [out_place.get(i, (prims[i][0].shape, prims[i][1], prims[i][2])) for i in diff]
    return pl.pallas_call(
        body, name=name, grid=grid,
        in_specs=[_spec(b, m) for _, b, m in ins],
        out_specs=[_spec(b, m) for _, b, m in places],
        out_shape=[jax.ShapeDtypeStruct(s, out_dtype.get(i, F32)) for i, (s, _, _) in zip(diff, places)],
        compiler_params=_cparams(len(grid)),
    )(*[a for a, _, _ in ins])


def _matmul(name, a, b, mode, *, out_dtype=F32, lhs_fn=None, epi=None, extra=None, tm=1024, tn=1024, tk=1024,
            b_cols=None, b_rows=None, out_cols=False):
    if mode == 'tn':
        kdim, m = a.shape
    else:
        m, kdim = a.shape
    if b_rows is not None:
        loc, cols = b.shape[-2], b.shape[-1]
        if mode == 'nn':
            n, k_shards = cols, max(1, tk // loc)
            tk = k_shards * loc
        else:
            n, tn = N_DEV * loc, loc
    elif b_cols is None:
        n = b.shape[0] if mode == 'nt' else b.shape[1]
    else:
        rows, loc = b.shape[-2], b.shape[-1]
        if mode == 'nn':
            n, n_shards = N_DEV * loc, max(1, tn // loc)
            tn = n_shards * loc
        else:
            n, k_shards = rows, max(1, tk // loc)
            tk = k_shards * loc
    if out_cols:
        o_shards = max(1, min(tn, n) // (n // N_DEV))
        tn = o_shards * (n // N_DEV)
    tm, tn, tk = min(tm, m), min(tn, n), min(tk, kdim)
    nk = kdim // tk
    grid = (m // tm, n // tn, nk)
    a_spec = _spec((tk, tm), lambda i, j, k: (k, i)) if mode == 'tn' else _spec((tm, tk), lambda i, j, k: (i, k))
    if b_rows is not None:
        lead = tuple(b_rows)
        skip = (None,) * (1 + len(lead))
        if mode == 'nn':
            b_spec = _spec((k_shards,) + skip[1:] + (tk // k_shards, tn), lambda i, j, k: (k,) + lead + (0, j))
        else:
            b_spec = _spec(skip + (tn, tk), lambda i, j, k: (j,) + lead + (0, k))
    elif b_cols is None:
        b_spec = _spec((tn, tk), lambda i, j, k: (j, k)) if mode == 'nt' else _spec((tk, tn), lambda i, j, k: (k, j))
    else:
        lead = tuple(b_cols)
        skip = (None,) * (1 + len(lead))
        if mode == 'nn':
            b_spec = _spec((n_shards,) + skip[1:] + (tk, tn // n_shards), lambda i, j, k: (j,) + lead + (k, 0))
        else:
            b_spec = _spec((k_shards,) + skip[1:] + (tn, tk // k_shards), lambda i, j, k: (k,) + lead + (j, 0))
    o_spec = _spec((tm, tn), lambda i, j, k: (i, j))
    if out_cols:
        o_place, o_shape = _spec((o_shards, tm, tn // o_shards), lambda i, j, k: (j, i, 0)), (N_DEV, m, n // N_DEV)
    else:
        o_place, o_shape = o_spec, (m, n)
    dims = _DOT_DIMS[mode]

    def body(*refs):
        if extra is None:
            a_ref, b_ref, o_ref, acc = refs
            x_ref = None
        else:
            a_ref, b_ref, x_ref, o_ref, acc = refs
        k = pl.program_id(2)
        av = a_ref[...]
        if lhs_fn is not None:
            av = lhs_fn(av.astype(F32))
        bv = b_ref[...]
        if bv.ndim == 3:
            bv = bv.reshape(tk, tn) if b_rows is not None else jnp.concatenate([bv[g] for g in range(bv.shape[0])], axis=1)
        part = lax.dot_general(av.astype(BF16), bv.astype(BF16), dims, preferred_element_type=F32)

        def finish(r):
            if epi is not None:
                r = epi(r, x_ref[...])
            if out_cols:
                width = tn // o_shards
                for g in range(o_shards):
                    o_ref[g] = r[:, g * width:(g + 1) * width].astype(o_ref.dtype)
            else:
                o_ref[...] = r.astype(o_ref.dtype)

        if nk == 1:
            finish(part)
            return

        @pl.when(k == 0)
        def _():
            acc[...] = part

        @pl.when((k > 0) & (k < nk - 1))
        def _():
            acc[...] += part

        @pl.when(k == nk - 1)
        def _():
            finish(acc[...] + part)

    ops, specs = [a, b], [a_spec, b_spec]
    if extra is not None:
        ops.append(extra)
        specs.append(o_spec)
    return pl.pallas_call(
        body, name=name, grid=grid, in_specs=specs, out_specs=o_place,
        out_shape=jax.ShapeDtypeStruct(o_shape, out_dtype),
        scratch_shapes=[pltpu.VMEM((tm, tn), F32)],
        compiler_params=_cparams(3),
    )(*ops)


def _relu2(x):
    r = jnp.maximum(x, 0.0)
    return r * r


def _mix_fn(x, halo, mu, w0, w1, w2, a0, a1, a2, g1, g2):
    prev_row = jnp.where(pl.program_id(0) == 0, 0.0, halo[7:8, :])
    xx = _shift_down(x, prev_row) - x
    xr, xw, xk, xv, xa, xg = (x + xx * mu[i:i + 1, :] for i in range(6))
    w_pre = w0 + _mm(jnp.tanh(_mm(xw, w1)), w2)
    log_decay = -jnp.exp(-_softplus(-w_pre) - 0.5)
    a = _sigmoid(a0 + _mm(_mm(xa, a1), a2))
    g = _mm(_sigmoid(_mm(xg, g1)), g2)
    return xr, xk, xv, log_decay, a, g


def _chunk_fn(r, lw, k, v, a, k_k, k_a, r_k, inverse=None, with_inverse=False):
    n, n_pair = r.shape[0], r.shape[1] // PAIR
    to_batch = lambda t: jnp.concatenate([t[None, :, p * PAIR:(p + 1) * PAIR] for p in range(n_pair)], axis=0)
    from_batch = lambda t: jnp.concatenate([t[p] for p in range(n_pair)], axis=1)
    r, lw, k, v, a, k_k, k_a, r_k = (to_batch(t) for t in (r, lw, k, v, a, k_k, k_a, r_k))
    lane = lax.broadcasted_iota(jnp.int32, (1, 1, PAIR), 2)
    m0 = lane < HEAD

    def head_sum(t):
        s0 = jnp.sum(jnp.where(m0, t, 0.0), axis=2, keepdims=True)
        s1 = jnp.sum(jnp.where(m0, 0.0, t), axis=2, keepdims=True)
        return jnp.where(m0, s0, s1)

    kk = k * k_k
    kk = kk / jnp.maximum(jnp.sqrt(head_sum(kk * kk)), 1e-12)
    k2 = k * (1.0 + (a - 1.0) * k_a)
    aa, bb = -kk, kk * a
    bonus = head_sum(r * k2 * r_k) * v

    cum = _cumsum_rows(lw)
    p_in, p_ex, p_inv = jnp.exp(cum), jnp.exp(cum - lw), jnp.exp(-cum)
    at, rt, kt, bt = aa * p_ex, r * p_in, k2 * p_inv, bb * p_inv

    def stack_masked(t):
        return jnp.concatenate([jnp.where(m0, t, 0.0), jnp.where(m0, 0.0, t)], axis=1)

    def unstack_sum(t):
        return t[:, :n] + t[:, n:]

    def unstack_select(t):
        return jnp.where(m0, t[:, :n], t[:, n:])

    ti = lax.broadcasted_iota(jnp.int32, (1, 2 * n, 2 * n), 1)
    tj = lax.broadcasted_iota(jnp.int32, (1, 2 * n, 2 * n), 2)
    same = (ti >= n) == (tj >= n)
    incl, strict = same & (ti >= tj), same & (ti > tj)
    m = 2 * n
    rows = lambda *ts: jnp.concatenate(ts, axis=1)
    cols = lambda *ts: jnp.concatenate(ts, axis=2)
    at_s, rt_s, kt_s, bt_s = stack_masked(at), stack_masked(rt), stack_masked(kt), stack_masked(bt)
    v_s, zeros = rows(v, v), jnp.zeros_like(at_s)
    bk_s = rows(bt_s, kt_s)
    gram = _pdot(rows(at_s, rt_s), bk_s, 'nt')
    a_ab = jnp.where(strict, gram[:, :m, :m], 0.0)
    a_ak = jnp.where(strict, gram[:, :m, m:], 0.0)
    a_rb = jnp.where(incl, gram[:, m:, :m], 0.0)
    a_rk = jnp.where(incl, gram[:, m:, m:], 0.0)
    inv = _nilpotent_inverse(a_ab, n) if inverse is None else _known_inverse(a_ab, inverse)
    wy_s = _pdot(inv, cols(at_s, _pdot(a_ak, v_s, 'nn')), 'nn')
    q_oi = _pdot(cols(a_rb, a_rk), rows(wy_s, cols(zeros, v_s)), 'nn')
    q_all = unstack_sum(rt_s + q_oi[:, :, :PAIR])
    oi_all = unstack_select(q_oi[:, :, PAIR:])
    own_head = (lax.broadcasted_iota(jnp.int32, (1, m, PAIR), 1) >= n) == (lax.broadcasted_iota(jnp.int32, (1, m, PAIR), 2) >= HEAD)
    wy_m = cols(wy_s[:, :, :PAIR], jnp.where(own_head, wy_s[:, :, PAIR:], 0.0))
    mm_nn = _pdot(bk_s, rows(wy_m, cols(zeros, stack_masked(v))), 'tn')
    mm, nn = mm_nn[:, :, :PAIR], mm_nn[:, :, PAIR:]
    ei = lax.broadcasted_iota(jnp.int32, (1, PAIR, PAIR), 1)
    ej = lax.broadcasted_iota(jnp.int32, (1, PAIR, PAIR), 2)
    eye_p = (ei == ej).astype(F32)
    decay_col = jnp.sum(eye_p * p_in[:, n - 1:n, :], axis=2, keepdims=True)
    maps = ((decay_col * (eye_p + mm))[None], (decay_col * nn)[None], from_batch(q_all), from_batch(oi_all),
            from_batch(bonus))
    return maps + (inv,) if with_inverse else maps


def _gate_fn(o, bonus, g, lnx_g, lnx_b):
    lane = lax.broadcasted_iota(jnp.int32, (1, PAIR), 1)
    m0 = lane < HEAD

    def head_mean(t):
        s0 = jnp.sum(jnp.where(m0, t, 0.0), axis=1, keepdims=True)
        s1 = jnp.sum(jnp.where(m0, 0.0, t), axis=1, keepdims=True)
        return jnp.where(m0, s0, s1) * (1.0 / HEAD)

    outs = []
    for p in range(N_PAIR):
        sl = slice(p * PAIR, (p + 1) * PAIR)
        oc = o[:, sl] - head_mean(o[:, sl])
        on = oc * lax.rsqrt(head_mean(oc * oc) + GN_EPS) * lnx_g[:, sl] + lnx_b[:, sl]
        outs.append((on + bonus[:, sl]) * g[:, sl])
    return (jnp.concatenate(outs, axis=1),)


def _res_ln_fn(h, y, g, b):
    return (_layer_norm(DN_ALPHA * h + y, g, b),)


def _glu_ln_fn(h, z, g, b):
    mix = z[:, :D_MODEL] * _sigmoid(z[:, D_MODEL:])
    return (_layer_norm(DN_ALPHA * h + mix, g, b),)


def _s5_param_fn(a_re, a_im, log_dt, b_re, b_im):
    dt = jnp.exp(log_dt)
    lam_re = jnp.minimum(a_re, -1e-4)
    lam_im = a_im
    mag = jnp.exp(dt * lam_re)
    abar_re = mag * jnp.cos(dt * lam_im)
    abar_im = mag * jnp.sin(dt * lam_im)
    den = lam_re * lam_re + lam_im * lam_im
    nr, ni = abar_re - 1.0, abar_im
    coef_re = (nr * lam_re + ni * lam_im) / den
    coef_im = (ni * lam_re - nr * lam_im) / den
    return abar_re, abar_im, coef_re, coef_im


def _s5_bbar_fn(coef_re, coef_im, b_re, b_im):
    return coef_re * b_re - coef_im * b_im, coef_re * b_im + coef_im * b_re


def _s5_in_fn(u, b_re, b_im):
    return _pdot(u, b_re[0], 'nn'), _pdot(u, b_im[0], 'nn')


def _s5_out_fn(s_re, s_im, u, c_re, c_im, d):
    y = _pdot(s_re, c_re[0], 'nn') - _pdot(s_im, c_im[0], 'nn') + u * d
    return (_gelu(y),)


def _pairs_to_batch(t):
    return jnp.concatenate([t[None, :, p * PAIR:(p + 1) * PAIR] for p in range(N_PAIR)], axis=0)


def _batch_to_pairs(t):
    return jnp.concatenate([t[p] for p in range(N_PAIR)], axis=1)


STATE_PASSES = 3


def _rwkv_scan_fwd(r, lw, k, v, a, k_k, k_a, r_k):
    n_tok = r.shape[0]
    n_chunk = n_tok // CHUNK

    def body(r_ref, lw_ref, k_ref, v_ref, a_ref, kk_ref, ka_ref, rk_ref, o_ref, bonus_ref, zs_ref, inv_ref, z):
        @pl.when(pl.program_id(0) == 0)
        def _():
            z[...] = jnp.zeros(z.shape, F32)

        m_all, n_all, q, oi, bonus, inv = _chunk_fn(r_ref[...], lw_ref[...], k_ref[...], v_ref[...], a_ref[...],
                                                    kk_ref[...], ka_ref[...], rk_ref[...], with_inverse=True)
        zb = z[...]
        zs_ref[0] = zb
        inv_ref[0] = inv.astype(inv_ref.dtype)
        bonus_ref[...] = bonus
        o_ref[...] = _batch_to_pairs(_dot_passes(_pairs_to_batch(q), zb, 'nn', STATE_PASSES)) + oi
        z[...] = _dot_passes(m_all[0], zb, 'nn', STATE_PASSES) + n_all[0]

    tok = _spec((CHUNK, D_MODEL), lambda c: (c, 0))
    par = _spec((1, D_MODEL), lambda c: (0, 0))
    st = _spec((1, N_PAIR, PAIR, PAIR), lambda c: (c, 0, 0, 0))
    return pl.pallas_call(
        body, name="rwkv_scan_fwd", grid=(n_chunk,),
        in_specs=[tok] * 5 + [par] * 3, out_specs=[tok, tok, st, st],
        out_shape=[jax.ShapeDtypeStruct((n_tok, D_MODEL), F32)] * 2
        + [jax.ShapeDtypeStruct((n_chunk, N_PAIR, PAIR, PAIR), F32), jax.ShapeDtypeStruct((n_chunk, N_PAIR, PAIR, PAIR), BF16)],
        scratch_shapes=[pltpu.VMEM((N_PAIR, PAIR, PAIR), F32)],
        compiler_params=_cparams(1),
    )(r, lw, k, v, a, k_k, k_a, r_k)


def _rwkv_scan_bwd(r, lw, k, v, a, k_k, k_a, r_k, zs, invs, d_o, d_bonus):
    n_tok = r.shape[0]
    n_chunk = n_tok // CHUNK

    def body(r_ref, lw_ref, k_ref, v_ref, a_ref, kk_ref, ka_ref, rk_ref, zs_ref, inv_ref, do_ref, db_ref,
             dr_ref, dlw_ref, dk_ref, dv_ref, da_ref, dkk_ref, dka_ref, drk_ref, g):
        sums = (dkk_ref, dka_ref, drk_ref)

        @pl.when(pl.program_id(0) == 0)
        def _():
            g[...] = jnp.zeros(g.shape, F32)
            for s_ref in sums:
                s_ref[...] = jnp.zeros(s_ref.shape, F32)

        prim = (r_ref[...], lw_ref[...], k_ref[...], v_ref[...], a_ref[...], kk_ref[...], ka_ref[...], rk_ref[...])
        known = inv_ref[0].astype(F32)
        (m_all, _, q, _, _), chunk_vjp = jax.vjp(lambda *p: _chunk_fn(*p, inverse=known), *prim)
        gb, zb, d_out = g[...], zs_ref[0], do_ref[...]
        dob = _pairs_to_batch(d_out)
        d_m = _dot_passes(gb, zb, 'nt', STATE_PASSES)
        d_q = _batch_to_pairs(_dot_passes(dob, zb, 'nt', STATE_PASSES))
        grads = chunk_vjp((d_m[None], gb[None], d_q, d_out, db_ref[...]))
        for out_ref, val in zip((dr_ref, dlw_ref, dk_ref, dv_ref, da_ref), grads[:5]):
            out_ref[...] = val.astype(out_ref.dtype)
        for s_ref, val in zip(sums, grads[5:]):
            s_ref[...] += val
        same_head = (lax.broadcasted_iota(jnp.int32, (1, PAIR, PAIR), 1) // HEAD
                     == lax.broadcasted_iota(jnp.int32, (1, PAIR, PAIR), 2) // HEAD)
        g[...] = (_dot_passes(m_all[0], gb, 'tn', STATE_PASSES)
                  + jnp.where(same_head, _dot_passes(_pairs_to_batch(q), dob, 'tn', STATE_PASSES), 0.0))

    tok = _spec((CHUNK, D_MODEL), lambda c: (n_chunk - 1 - c, 0))
    par = _spec((1, D_MODEL), lambda c: (0, 0))
    st = _spec((1, N_PAIR, PAIR, PAIR), lambda c: (n_chunk - 1 - c, 0, 0, 0))
    tok_shape = lambda dt: jax.ShapeDtypeStruct((n_tok, D_MODEL), dt)
    return pl.pallas_call(
        body, name="rwkv_scan_bwd", grid=(n_chunk,),
        in_specs=[tok] * 5 + [par] * 3 + [st, st, tok, tok], out_specs=[tok] * 5 + [par] * 3,
        out_shape=[tok_shape(BF16), tok_shape(F32), tok_shape(BF16), tok_shape(BF16), tok_shape(F32)]
        + [jax.ShapeDtypeStruct((1, D_MODEL), F32)] * 3,
        scratch_shapes=[pltpu.VMEM((N_PAIR, PAIR, PAIR), F32)],
        compiler_params=_cparams(1),
    )(r, lw, k, v, a, k_k, k_a, r_k, zs, invs, d_o, d_bonus)


S5_TIME_TILE = 1024
S5_GROUP = 8


def _scan_rows(re_ref, im_ref, a1, carry, reverse):
    lb = re_ref.shape[1]
    grp, n_grp = S5_GROUP, re_ref.shape[0] // S5_GROUP

    def cmul(xr, xi, yr, yi):
        return xr * yr - xi * yi, xr * yi + xi * yr

    a2 = cmul(*a1, *a1)
    a4 = cmul(*a2, *a2)
    a8 = cmul(*a4, *a4)
    row = lax.broadcasted_iota(jnp.int32, (grp, lb), 0)
    expo = (grp - row) if reverse else (row + 1)
    pw = (jnp.ones((grp, lb), F32), jnp.zeros((grp, lb), F32))
    for bit, ap in ((1, a1), (2, a2), (4, a4), (8, a8)):
        nxt = cmul(*pw, *ap)
        sel = (expo & bit) != 0
        pw = (jnp.where(sel, nxt[0], pw[0]), jnp.where(sel, nxt[1], pw[1]))

    steps = []
    for d, ad in ((1, a1), (2, a2), (4, a4)):
        keep = (row < grp - d) if reverse else (row >= d)
        steps.append(((grp - d) if reverse else d, jnp.where(keep, ad[0], 0.0), jnp.where(keep, ad[1], 0.0)))

    def group(i, c):
        gi = (n_grp - 1 - i) if reverse else i
        rows = pl.ds(pl.multiple_of(gi * grp, grp), grp)
        xr, xi = re_ref[rows, :], im_ref[rows, :]
        for shift, mr, mi in steps:
            pr, pi = cmul(mr, mi, pltpu.roll(xr, shift, 0), pltpu.roll(xi, shift, 0))
            xr, xi = xr + pr, xi + pi
        cr, ci = cmul(*pw, c[0], c[1])
        xr, xi = xr + cr, xi + ci
        re_ref[rows, :] = xr
        im_ref[rows, :] = xi
        edge = slice(0, 1) if reverse else slice(grp - 1, grp)
        return xr[edge, :], xi[edge, :]

    return lax.fori_loop(0, n_grp, group, carry)


def _s5_fwd(u, b_re, b_im, a_re, a_im, c_re, c_im, d_row):
    n_tok = u.shape[0]
    tt = min(S5_TIME_TILE, n_tok)

    def body(u_ref, bre_ref, bim_ref, are_ref, aim_ref, cre_ref, cim_ref, d_ref, sre_ref, sim_ref, yg_ref, carry):
        @pl.when(pl.program_id(1) == 0)
        def _():
            carry[...] = jnp.zeros(carry.shape, F32)

        uv = u_ref[...]
        sre_ref[...], sim_ref[...] = _s5_in_fn(uv, bre_ref[...], bim_ref[...])
        c = _scan_rows(sre_ref, sim_ref, (are_ref[...], aim_ref[...]), (carry[0:1, :], carry[1:2, :]), False)
        carry[0:1, :] = c[0]
        carry[1:2, :] = c[1]
        (yg,) = _s5_out_fn(sre_ref[...], sim_ref[...], uv, cre_ref[...], cim_ref[...], d_ref[...])
        yg_ref[...] = yg.astype(yg_ref.dtype)

    u_blk = _spec((tt, SSM_BLK_IN), lambda l, t: (t, l))
    s_blk = _spec((tt, SSM_BLK_ST), lambda l, t: (t, l))
    blk3 = lambda arr: _spec((1,) + arr.shape[1:], lambda l, t: (l, 0, 0))
    par = lambda width: _spec((1, width), lambda l, t: (0, l))
    return pl.pallas_call(
        body, name="s5_fwd", grid=(N_SSM_BLK, n_tok // tt),
        in_specs=[u_blk, blk3(b_re), blk3(b_im), par(SSM_BLK_ST), par(SSM_BLK_ST), blk3(c_re), blk3(c_im), par(SSM_BLK_IN)],
        out_specs=[s_blk, s_blk, u_blk],
        out_shape=[jax.ShapeDtypeStruct((n_tok, SSM_LANES), F32)] * 2 + [jax.ShapeDtypeStruct((n_tok, D_MODEL), BF16)],
        scratch_shapes=[pltpu.VMEM((2, SSM_BLK_ST), F32)],
        compiler_params=_cparams(2),
    )(u, b_re, b_im, a_re, a_im, c_re, c_im, d_row)


def _s5_bwd(dyg, u, s_re, s_im, dh_res, b_re, b_im, a_re, a_im, c_re, c_im, d_row):
    n_tok = u.shape[0]
    tt = min(S5_TIME_TILE, n_tok)
    n_t = n_tok // tt

    def body(dyg_ref, u_ref, sre_ref, sim_ref, hre_ref, him_ref, res_ref, bre_ref, bim_ref, are_ref, aim_ref,
             cre_ref, cim_ref, d_ref, du_ref, dbre_ref, dbim_ref, dcre_ref, dcim_ref, dd_ref, dare_ref, daim_ref,
             carry, gre, gim):
        i = pl.program_id(1)
        sums = (dbre_ref, dbim_ref, dcre_ref, dcim_ref, dd_ref, dare_ref, daim_ref)

        @pl.when(i == 0)
        def _():
            carry[...] = jnp.zeros(carry.shape, F32)
            for r in sums:
                r[...] = jnp.zeros(r.shape, F32)

        uv, sre, sim = u_ref[...], sre_ref[...], sim_ref[...]
        _, out_vjp = jax.vjp(_s5_out_fn, sre, sim, uv, cre_ref[...], cim_ref[...], d_ref[...])
        gre[...], gim[...], du_out, dcre, dcim, dd = out_vjp((dyg_ref[...],))
        c = _scan_rows(gre, gim, (are_ref[...], -aim_ref[...]), (carry[0:1, :], carry[1:2, :]), True)
        carry[0:1, :] = c[0]
        carry[1:2, :] = c[1]
        g_re, g_im = gre[...], gim[...]
        at_start = i == n_t - 1
        p_re = _shift_down(sre, jnp.where(at_start, 0.0, hre_ref[7:8, :]))
        p_im = _shift_down(sim, jnp.where(at_start, 0.0, him_ref[7:8, :]))
        _, in_vjp = jax.vjp(_s5_in_fn, uv, bre_ref[...], bim_ref[...])
        du_in, dbre, dbim = in_vjp((g_re, g_im))
        du_ref[...] = du_out + du_in + res_ref[...]
        dare = jnp.sum(g_re * p_re + g_im * p_im, axis=0, keepdims=True)
        daim = jnp.sum(g_im * p_re - g_re * p_im, axis=0, keepdims=True)
        for r, val in zip(sums, (dbre, dbim, dcre, dcim, dd, dare, daim)):
            r[...] += val

    u_blk = _spec((tt, SSM_BLK_IN), lambda l, t: (n_t - 1 - t, l))
    s_blk = _spec((tt, SSM_BLK_ST), lambda l, t: (n_t - 1 - t, l))
    halo = _spec((8, SSM_BLK_ST), lambda l, t: (jnp.maximum((n_t - 1 - t) * (tt // 8) - 1, 0), l))
    blk3 = lambda arr: _spec((1,) + arr.shape[1:], lambda l, t: (l, 0, 0))
    par = lambda width: _spec((1, width), lambda l, t: (0, l))
    params = [blk3(b_re), blk3(b_im), par(SSM_BLK_ST), par(SSM_BLK_ST), blk3(c_re), blk3(c_im), par(SSM_BLK_IN)]
    shape = lambda arr: jax.ShapeDtypeStruct(arr.shape, F32)
    return pl.pallas_call(
        body, name="s5_bwd", grid=(N_SSM_BLK, n_t),
        in_specs=[u_blk, u_blk, s_blk, s_blk, halo, halo, u_blk] + params,
        out_specs=[u_blk, blk3(b_re), blk3(b_im), blk3(c_re), blk3(c_im), par(SSM_BLK_IN), par(SSM_BLK_ST), par(SSM_BLK_ST)],
        out_shape=[shape(u), shape(b_re), shape(b_im), shape(c_re), shape(c_im), shape(d_row), shape(a_re), shape(a_im)],
        scratch_shapes=[pltpu.VMEM((2, SSM_BLK_ST), F32), pltpu.VMEM((tt, SSM_BLK_ST), F32), pltpu.VMEM((tt, SSM_BLK_ST), F32)],
        compiler_params=_cparams(2),
    )(dyg, u, s_re, s_im, s_re, s_im, dh_res, b_re, b_im, a_re, a_im, c_re, c_im, d_row)


def _loss_call(h, target, tm):
    n_tok, d = h.shape

    def body(h_ref, t_ref, acc_ref, dh_ref):
        @pl.when(pl.program_id(0) == 0)
        def _():
            acc_ref[...] = jnp.zeros(acc_ref.shape, F32)

        e = h_ref[...] - t_ref[...]
        dh_ref[...] = e * (1.0 / d)
        acc_ref[...] += jnp.sum(jnp.sum(e * e, axis=1, keepdims=True), axis=0, keepdims=True)

    tok = _spec((tm, d), lambda i: (i, 0))
    return pl.pallas_call(
        body, name="loss", grid=(n_tok // tm,),
        in_specs=[tok, tok], out_specs=[_spec((8, 128), lambda i: (0, 0)), tok],
        out_shape=[jax.ShapeDtypeStruct((8, 128), F32), jax.ShapeDtypeStruct(h.shape, F32)],
        compiler_params=_cparams(1),
    )(h, target)


def _block_diag(t):
    nb, ng, rows, cols = t.shape
    eye = jnp.eye(ng, dtype=t.dtype)
    return jnp.einsum('bgrc,gh->bgrhc', t, eye).reshape(nb, ng * rows, ng * cols)


def _block_diag_t(t, rows, cols):
    nb, ng = t.shape[0], t.shape[1] // rows
    t5 = t.reshape(nb, ng, rows, ng, cols)
    return jnp.einsum('bgrhc,gh->bgrc', t5, jnp.eye(ng, dtype=t.dtype))


def _mlp_fwd(layer, h, w1g, w2g):
    pre = _matmul(f"mlp{layer}_up", h, w1g, 'nn', b_cols=(layer,), out_dtype=BF16)
    return pre, _matmul(f"mlp{layer}_down", pre, w2g, 'nn', lhs_fn=_relu2, b_rows=(layer,))


def _mlp_bwd(layer, h, pre, dy, dh_res, w1g, w2g):
    dpre = _matmul(f"mlp{layer}_down_dx", dy, w2g, 'nt', epi=lambda acc, p: acc * (2.0 * jnp.maximum(p, 0.0)),
                   extra=pre, b_rows=(layer,), out_dtype=BF16)
    dw2 = _matmul(f"mlp{layer}_down_dw", pre, dy, 'tn', lhs_fn=_relu2, out_dtype=GRAD_DTYPE)
    dw1 = _matmul(f"mlp{layer}_up_dw", h, dpre, 'tn', out_cols=True, out_dtype=GRAD_DTYPE)
    dh = _matmul(f"mlp{layer}_up_dx", dpre, w1g, 'nt', epi=lambda acc, e: acc + e, extra=dh_res, b_cols=(layer,))
    return dh, dw1, dw2


def _local_step(x, target, w, wg, late_weights, exchange):
    n_tok = x.shape[0]
    tm = min(TOKEN_TILE, n_tok)
    n_tile, n_chunk = n_tok // tm, n_tok // CHUNK
    row = lambda v: v.reshape(1, -1)
    big = {n: wg[n].reshape(D_MODEL, D_MODEL) for n in ('rw_wr', 'rw_wk', 'rw_wv', 'rw_wo')}
    ln_g, ln_b = w['ln_g'], w['ln_b']
    grads = {}
    to_slots = lambda t: t.reshape(N_DEV, t.shape[0] // N_DEV, t.shape[1])

    halo_spec = (8, D_MODEL), lambda i: (jnp.maximum(i * (tm // 8) - 1, 0), 0)
    mix_p = [_tile(x, tm), (x,) + halo_spec, _full(w['rw_mu'][0]), _full(row(w['rw_w0'])), _full(w['rw_w1'][0]),
             _full(w['rw_w2'][0]), _full(row(w['rw_a0'])), _full(w['rw_a1'][0]), _full(w['rw_a2'][0]),
             _full(w['rw_g1'][0]), _full(w['rw_g2'][0])]
    tok_out = lambda dt: ((n_tok, D_MODEL), dt, (tm, D_MODEL), lambda i: (i, 0))
    xr, xk, xv, lw, a, g = _fwd_call("rwkv_mix", _mix_fn, (n_tile,), mix_p,
                                     [tok_out(BF16), tok_out(BF16), tok_out(BF16), tok_out(F32), tok_out(F32), tok_out(F32)])
    r = _matmul("rwkv_r", xr, big['rw_wr'], 'nn')
    k = _matmul("rwkv_k", xk, big['rw_wk'], 'nn')
    v = _matmul("rwkv_v", xv, big['rw_wv'], 'nn')

    scan_in = (r, lw, k, v, a, row(w['rw_k_k']), row(w['rw_k_a']), w['rw_r_k'].reshape(1, -1))
    o, bonus, zs, invs = _rwkv_scan_fwd(*scan_in)

    gate_p = [_tile(o, tm), _tile(bonus, tm), _tile(g, tm), _full(row(w['rw_lnx_g'])), _full(row(w['rw_lnx_b']))]
    (og,) = _fwd_call("rwkv_gate", _gate_fn, (n_tile,), gate_p, [tok_out(BF16)])
    y0 = _matmul("rwkv_o", og, big['rw_wo'], 'nn')
    ln0_p = [_tile(x, tm), _tile(y0, tm), _full(ln_g[0:1]), _full(ln_b[0:1])]
    twice = lambda fn: lambda *args: fn(*args) * 2
    h1, h1_b = _fwd_call("ln0", twice(_res_ln_fn), (n_tile,), ln0_p, [tok_out(F32), tok_out(BF16)])
    late = late_weights(h1)
    w1g, w2g, glu_g = late['mlp_w1'], late['mlp_w2'], late['s5_w_glu']
    pre0, y1 = _mlp_fwd(0, h1_b, w1g, w2g)
    ln1_p = [_tile(h1, tm), _tile(y1, tm), _full(ln_g[1:2]), _full(ln_b[1:2])]
    (h2,) = _fwd_call("ln1", _res_ln_fn, (n_tile,), ln1_p, [tok_out(F32)])

    gp = (SSM_GROUPS, SSM_STATE)
    par_p = [_full(w['s5_a_re'][0]), _full(w['s5_a_im'][0]), _full(w['s5_log_dt'].reshape(SSM_GROUPS, 1))]
    s5_par_fn = lambda ar, ai, ld: _s5_param_fn(ar, ai, ld, None, None)
    gp_out = (gp, F32, gp, lambda i: (0, 0))
    abar_re, abar_im, coef_re, coef_im = _fwd_call("s5_param", s5_par_fn, (1,), par_p, [gp_out] * 4)
    b_flat = [w[n][0].reshape(SSM_LANES, SSM_GROUP) for n in ('s5_b_re', 's5_b_im')]
    bbar_p = [_full(coef_re.reshape(SSM_LANES, 1)), _full(coef_im.reshape(SSM_LANES, 1)), _full(b_flat[0]), _full(b_flat[1])]
    bb_out = ((SSM_LANES, SSM_GROUP), F32, (SSM_LANES, SSM_GROUP), lambda i: (0, 0))
    bbar_re, bbar_im = _fwd_call("s5_bbar", _s5_bbar_fn, (1,), bbar_p, [bb_out] * 2)
    to_in = lambda t: _block_diag(t.reshape(N_SSM_BLK, 8, SSM_STATE, SSM_GROUP).transpose(0, 1, 3, 2))
    to_out = lambda t: _block_diag(t.reshape(N_SSM_BLK, 8, SSM_GROUP, SSM_STATE).transpose(0, 1, 3, 2))
    bblk_re, bblk_im = to_in(bbar_re), to_in(bbar_im)
    cblk_re, cblk_im = to_out(w['s5_c_re'][0]), to_out(w['s5_c_im'][0])
    a_row_re, a_row_im = abar_re.reshape(1, SSM_LANES), abar_im.reshape(1, SSM_LANES)

    d_row = row(w['s5_d'])
    s5_params = (bblk_re, bblk_im, a_row_re, a_row_im, cblk_re, cblk_im, d_row)
    s_re, s_im, yg = _s5_fwd(h2, *s5_params)
    z = _matmul("s5_glu", yg, glu_g, 'nn', b_cols=(0,))
    ln2_p = [_tile(h2, tm), _tile(z, tm), _full(ln_g[2:3]), _full(ln_b[2:3])]
    h3, h3_b = _fwd_call("ln2", twice(_glu_ln_fn), (n_tile,), ln2_p, [tok_out(F32), tok_out(BF16)])
    pre1, y3 = _mlp_fwd(1, h3_b, w1g, w2g)
    ln3_p = [_tile(h3, tm), _tile(y3, tm), _full(ln_g[3:4]), _full(ln_b[3:4])]
    (h4,) = _fwd_call("ln3", _res_ln_fn, (n_tile,), ln3_p, [tok_out(F32)])

    loss_acc, dh4 = _loss_call(h4, target, tm)

    ln_w = ['tile', 'tile_bf16', 'acc', 'acc']
    dh3_res, dy3, dg3, db3 = _bwd_call("ln3_bwd", _res_ln_fn, (n_tile,), ln3_p, [_tile(dh4, tm)], ln_w)
    dh3, dw1_1, dw2_1 = _mlp_bwd(1, h3_b, pre1, dy3, dh3_res, w1g, w2g)
    zero = exchange([('mlp_w1', 1, dw1_1), ('mlp_w2', 1, to_slots(dw2_1))])
    ln2_p[2] = _full(ln_g[2:3] + zero)
    dh2_res, dz, dg2, db2 = _bwd_call("ln2_bwd", _glu_ln_fn, (n_tile,), ln2_p, [_tile(dh3, tm)], ln_w)
    dyg = _matmul("s5_glu_dx", dz, glu_g, 'nt', b_cols=(0,))
    dw_glu = _matmul("s5_glu_dw", yg, dz, 'tn', out_cols=True, out_dtype=GRAD_DTYPE)
    dh2, dbb_re, dbb_im, dcb_re, dcb_im, dd, dabar_re, dabar_im = _s5_bwd(dyg, h2, s_re, s_im, dh2_res, *s5_params)
    from_in = lambda t: _block_diag_t(t, SSM_GROUP, SSM_STATE).transpose(0, 1, 3, 2).reshape(SSM_LANES, SSM_GROUP)
    from_out = lambda t: _block_diag_t(t, SSM_STATE, SSM_GROUP).transpose(0, 1, 3, 2).reshape(1, SSM_GROUPS, SSM_GROUP, SSM_STATE)
    grads['s5_c_re'], grads['s5_c_im'] = from_out(dcb_re), from_out(dcb_im)
    grads['s5_d'] = dd
    dcoef_re, dcoef_im, db_re, db_im = _bwd_call(
        "s5_bbar_bwd", _s5_bbar_fn, (1,), bbar_p, [_full(from_in(dbb_re)), _full(from_in(dbb_im))], ['acc'] * 4)
    grads['s5_b_re'] = db_re.reshape(w['s5_b_re'].shape)
    grads['s5_b_im'] = db_im.reshape(w['s5_b_im'].shape)
    par_ct = [_full(dabar_re.reshape(gp)), _full(dabar_im.reshape(gp)), _full(dcoef_re.reshape(gp)), _full(dcoef_im.reshape(gp))]
    da_re, da_im, dlog_dt = _bwd_call("s5_param_bwd", s5_par_fn, (1,), par_p, par_ct, ['acc'] * 3)
    grads['s5_a_re'], grads['s5_a_im'], grads['s5_log_dt'] = da_re[None], da_im[None], dlog_dt.reshape(1, SSM_GROUPS)
    zero = exchange([(n, 0, grads[n]) for n in ('s5_a_re', 's5_a_im', 's5_log_dt', 's5_b_re', 's5_b_im', 's5_c_re', 's5_c_im')])
    ln1_p[2] = _full(ln_g[1:2] + zero)

    dh1_res, dy1, dg1, db1 = _bwd_call("ln1_bwd", _res_ln_fn, (n_tile,), ln1_p, [_tile(dh2, tm)], ln_w)
    dh1, dw1_0, dw2_0 = _mlp_bwd(0, h1_b, pre0, dy1, dh1_res, w1g, w2g)
    dx_res, dy0, dg0, db0 = _bwd_call("ln0_bwd", _res_ln_fn, (n_tile,), ln0_p, [_tile(dh1, tm)], ln_w)
    grads['ln_g'] = jnp.concatenate([dg0, dg1, dg2, dg3])
    grads['ln_b'] = jnp.concatenate([db0, db1, db2, db3])
    dog = _matmul("rwkv_o_dx", dy0, big['rw_wo'], 'nt')
    dw_o = to_slots(_matmul("rwkv_o_dw", og, dy0, 'tn', out_dtype=GRAD_DTYPE))
    zero = exchange([('s5_w_glu', 0, dw_glu), ('mlp_w1', 0, dw1_0), ('mlp_w2', 0, to_slots(dw2_0)), ('rw_wo', 0, dw_o)])
    gate_p[3] = _full(row(w['rw_lnx_g']) + zero)
    d_o, dbonus, dgate, grads['rw_lnx_g'], grads['rw_lnx_b'] = _bwd_call(
        "rwkv_gate_bwd", _gate_fn, (n_tile,), gate_p, [_tile(dog, tm)], ['tile', 'tile', 'tile', 'acc', 'acc'])
    dr, dlw, dk, dv, da, grads['rw_k_k'], grads['rw_k_a'], dr_k = _rwkv_scan_bwd(*scan_in, zs, invs, d_o, dbonus)
    grads['rw_r_k'] = dr_k.reshape(w['rw_r_k'].shape)
    dw_r = to_slots(_matmul("rwkv_r_dw", xr, dr, 'tn', out_dtype=GRAD_DTYPE))
    dw_k = to_slots(_matmul("rwkv_k_dw", xk, dk, 'tn', out_dtype=GRAD_DTYPE))
    dw_v = to_slots(_matmul("rwkv_v_dw", xv, dv, 'tn', out_dtype=GRAD_DTYPE))
    done = ('ln_g', 'ln_b', 'rw_lnx_g', 'rw_lnx_b', 'rw_k_k', 'rw_k_a', 'rw_r_k')
    zero = exchange([('rw_wr', 0, dw_r), ('rw_wk', 0, dw_k), ('rw_wv', 0, dw_v)] + [(n, 0, grads[n]) for n in done])
    mix_p[3] = _full(row(w['rw_w0']) + zero)
    dxr = _matmul("rwkv_r_dx", dr, big['rw_wr'], 'nt')
    dxk = _matmul("rwkv_k_dx", dk, big['rw_wk'], 'nt')
    dxv = _matmul("rwkv_v_dx", dv, big['rw_wv'], 'nt')
    mix_ct = [_tile(t, tm) for t in (dxr, dxk, dxv, dlw, da, dgate)]
    halo_grad = ('tile', ((n_tile * 8, D_MODEL), (8, D_MODEL), lambda i: (i, 0)))
    res = _bwd_call("rwkv_mix_bwd", _mix_fn, (n_tile,), mix_p, mix_ct, ['tile', halo_grad] + ['acc'] * 9,
                    addends={0: _tile(dx_res, tm)})
    corr = res[1].reshape(n_tile, 8, D_MODEL)[1:, 7:8, :]
    corr = jnp.pad(corr, ((0, 1), (tm - 1, 0), (0, 0)))
    dx = (res[0].reshape(n_tile, tm, D_MODEL) + corr).reshape(n_tok, D_MODEL)
    (grads['rw_mu'], grads['rw_w0'], grads['rw_w1'], grads['rw_w2'], grads['rw_a0'], grads['rw_a1'], grads['rw_a2'],
     grads['rw_g1'], grads['rw_g2']) = [t[None] if t.shape[0] != 1 else t for t in res[2:]]
    return loss_acc[0, 0], dx, grads


def _full_shape(local_shape, axis):
    return local_shape[:axis] + (N_DEV * local_shape[axis],) + local_shape[axis + 1:]


def _split_shape(local_shape, axis):
    return local_shape[:axis] + (N_DEV, local_shape[axis]) + local_shape[axis + 1:]


def _mesh_pos():
    return lax.axis_index("x"), lax.axis_index("y"), lax.axis_index("c")


def _flip(v, f):
    return 1 - v if f else v


def _hbm_call(name, body, arrays, out_shapes, n_sems):
    n_t = len(arrays)
    hbm = pl.BlockSpec(memory_space=pl.ANY)
    return pl.pallas_call(
        body, name=name, out_shape=out_shapes, in_specs=[hbm] * n_t, out_specs=[hbm] * n_t,
        scratch_shapes=[pltpu.SemaphoreType.DMA((n_t, n_sems)), pltpu.SemaphoreType.DMA((n_t, n_sems)),
                        pltpu.SemaphoreType.DMA((n_t,))],
    )(*arrays)


def _all_gather(name, locals_):
    n_t = len(locals_)

    def body(*refs):
        x_refs, out_refs = refs[:n_t], refs[n_t:2 * n_t]
        send_sems, recv_sems, local_sems = refs[2 * n_t:]
        x, y, c = _mesh_pos()
        me, sibling = (x, y, c), (x, y, 1 - c)
        chips = [(1 - x, y), (x, 1 - y), (1 - x, 1 - y)]

        def copy(t, k, block, to, own=False):
            slot = out_refs[t].at[4 * block[0] + 2 * block[1] + block[2]]
            return pltpu.make_async_remote_copy(
                src_ref=x_refs[t] if own else slot, dst_ref=slot,
                send_sem=send_sems.at[t, k], recv_sem=recv_sems.at[t, k],
                device_id=to, device_id_type=pl.DeviceIdType.MESH)

        mine = [pltpu.make_async_copy(x_refs[t], out_refs[t].at[4 * x + 2 * y + c], local_sems.at[t]) for t in range(n_t)]
        for cp in mine:
            cp.start()
        sent = []
        for t in range(n_t):
            sent.append(copy(t, 0, me, sibling, own=True))
            sent += [copy(t, 1 + j, me, (*chip, c), own=True) for j, chip in enumerate(chips)]
        for cp in sent:
            cp.start()
        for j, chip in enumerate(chips):
            for t in range(n_t):
                copy(t, 1 + j, (*chip, c), me).wait_recv()
                passed = copy(t, 4 + j, (*chip, c), sibling)
                passed.start()
                sent.append(passed)
        for t in range(n_t):
            copy(t, 0, sibling, me).wait_recv()
            for j, chip in enumerate(chips):
                copy(t, 4 + j, (*chip, 1 - c), me).wait_recv()
        for cp in sent:
            cp.wait_send()
        for cp in mine:
            cp.wait()

    outs = [jax.ShapeDtypeStruct((N_DEV,) + a.shape, a.dtype) for a in locals_]
    return _hbm_call(name, body, locals_, outs, 7)


def _all_to_all(name, blocks):
    n_t = len(blocks)

    def body(*refs):
        x_refs, out_refs = refs[:n_t], refs[n_t:2 * n_t]
        send_sems, recv_sems, local_sems = refs[2 * n_t:]
        x, y, c = _mesh_pos()
        my_slot = 4 * x + 2 * y + c
        mine = [pltpu.make_async_copy(x_refs[t].at[my_slot], out_refs[t].at[my_slot], local_sems.at[t]) for t in range(n_t)]
        for cp in mine:
            cp.start()
        copies = []
        for k in range(1, N_DEV):
            peer = (_flip(x, k & 4), _flip(y, k & 2), _flip(c, k & 1))
            peer_slot = 4 * peer[0] + 2 * peer[1] + peer[2]
            for t in range(n_t):
                sems = dict(send_sem=send_sems.at[t, k - 1], recv_sem=recv_sems.at[t, k - 1],
                            device_id=peer, device_id_type=pl.DeviceIdType.MESH)
                cp = pltpu.make_async_remote_copy(src_ref=x_refs[t].at[peer_slot], dst_ref=out_refs[t].at[my_slot], **sems)
                cp.start()
                landed = pltpu.make_async_remote_copy(src_ref=x_refs[t].at[my_slot], dst_ref=out_refs[t].at[peer_slot], **sems)
                copies.append((cp, landed))
        for sent, landed in copies:
            landed.wait_recv()
            sent.wait_send()
        for cp in mine:
            cp.wait()

    outs = [jax.ShapeDtypeStruct(a.shape, a.dtype) for a in blocks]
    return _hbm_call(name, body, blocks, outs, 7)


_HBM = pl.BlockSpec(memory_space=pltpu.HBM)
_SEM = pl.BlockSpec(memory_space=pltpu.SEMAPHORE)
_EFFECT = pltpu.SideEffectType.DATAFLOW_SIDE_EFFECTING


def _peer_copies(x_refs, land_refs, send_sems, recv_sems, gather, mine):
    x, y, c = _mesh_pos()
    my_slot = 4 * x + 2 * y + c
    copies = []
    for k in range(1, N_DEV):
        peer = (_flip(x, k & 4), _flip(y, k & 2), _flip(c, k & 1))
        peer_slot = 4 * peer[0] + 2 * peer[1] + peer[2]
        for t, (x_ref, land) in enumerate(zip(x_refs, land_refs)):
            sem = t * (N_DEV - 1) + k - 1
            copies.append(pltpu.make_async_remote_copy(
                src_ref=x_ref if gather[t] else x_ref.at[peer_slot], dst_ref=land.at[my_slot if mine else peer_slot],
                send_sem=send_sems.at[sem], recv_sem=recv_sems.at[sem], device_id=peer, device_id_type=pl.DeviceIdType.MESH))
    return copies


def _push_start(name, arrays, gather):
    n_t = len(arrays)
    lands = [lax.empty((N_DEV,) + a.shape if whole else a.shape, a.dtype) for a, whole in zip(arrays, gather)]

    def body(*refs):
        x_refs, land_refs = refs[:n_t], refs[n_t:2 * n_t]
        send_sems, recv_sems = refs[2 * n_t], refs[2 * n_t + 1]
        for sent in _peer_copies(x_refs, land_refs, send_sems, recv_sems, gather, True):
            sent.start()
        refs[-1][...] = jnp.zeros((8, 128), F32)

    held = list(arrays) + lands
    sems = pltpu.SemaphoreType.DMA((n_t * (N_DEV - 1),))
    out = pl.pallas_call(
        body, name=name,
        out_shape=(sems, sems, *[pltpu.HBM(a.shape, a.dtype) for a in held], jax.ShapeDtypeStruct((8, 128), F32)),
        in_specs=[_HBM] * (2 * n_t),
        out_specs=(_SEM, _SEM, *[_HBM] * (2 * n_t), pl.BlockSpec(memory_space=pltpu.VMEM)),
        input_output_aliases={i: 2 + i for i in range(2 * n_t)},
        compiler_params=pltpu.CompilerParams(has_side_effects=_EFFECT),
    )(*[pltpu.with_memory_space_constraint(a, pltpu.HBM) for a in held])
    return dict(send=out[0], recv=out[1], held=out[2:2 + 2 * n_t], zero=out[-1][0, 0], gather=gather)


def _push_wait(name, started, after):
    held, gather = started['held'], started['gather']
    n_t = len(held) // 2

    def body(*refs):
        x_refs, land_refs = refs[:n_t], refs[n_t:2 * n_t]
        send_sems, recv_sems = refs[2 * n_t], refs[2 * n_t + 1]
        for sent in _peer_copies(x_refs, land_refs, send_sems, recv_sems, gather, True):
            sent.wait_send()
        for landed in _peer_copies(x_refs, land_refs, send_sems, recv_sems, gather, False):
            landed.wait_recv()

    out = pl.pallas_call(
        body, name=name,
        out_shape=[pltpu.HBM(a.shape, a.dtype) for a in held],
        in_specs=[_HBM] * (2 * n_t) + [_SEM, _SEM, pl.BlockSpec(memory_space=pl.ANY)],
        out_specs=[_HBM] * (2 * n_t),
        input_output_aliases={i: i for i in range(2 * n_t)},
        compiler_params=pltpu.CompilerParams(has_side_effects=_EFFECT),
    )(*held, started['send'], started['recv'], after)
    return out[:n_t], out[n_t:]


ADAM_BLOCK = 128 * 1024


def _adam_update(part, w_ref, m_ref, v_ref, g_ref, d_ref, nm_ref, nv_ref):
    g = part(0)
    for i in range(1, N_DEV):
        g = g + part(i)
    m_new = ADAM_B1 * m_ref[...] + (1.0 - ADAM_B1) * g
    v_new = ADAM_B2 * v_ref[...] + (1.0 - ADAM_B2) * (g * g)
    m_hat = m_new * (1.0 / (1.0 - ADAM_B1 ** ADAM_STEP))
    v_hat = v_new * (1.0 / (1.0 - ADAM_B2 ** ADAM_STEP))
    g_ref[...] = g
    nm_ref[...] = m_new
    nv_ref[...] = v_new
    d_ref[...] = -ADAM_LR * (m_hat / (jnp.sqrt(v_hat) + ADAM_EPS) + ADAM_WD * w_ref[...])


def _adam_many(name, items):
    n_t = len(items)
    n_in = [4 if own is None else 5 for *_, own in items]
    first = [sum(n_in[:t]) for t in range(n_t)]

    def body(*refs):
        ins, outs = refs[:sum(n_in)], refs[sum(n_in):]
        x, y, c = _mesh_pos()
        my_slot = 4 * x + 2 * y + c
        for t in range(n_t):
            s_ref, w_ref, m_ref, v_ref = ins[first[t]:first[t] + 4]
            if n_in[t] == 4:
                part = lambda i: s_ref[i]
            else:
                own_ref = ins[first[t] + 4]
                part = lambda i: jnp.where(my_slot == i, own_ref[...], s_ref[i])
            _adam_update(part, w_ref, m_ref, v_ref, *outs[4 * t:4 * t + 4])

    vmem = pl.BlockSpec(memory_space=pltpu.VMEM)
    out = pl.pallas_call(
        body, name=name, in_specs=[vmem] * sum(n_in), out_specs=[vmem] * (4 * n_t),
        out_shape=[jax.ShapeDtypeStruct(item[1].shape, F32) for item in items for _ in range(4)],
        compiler_params=pltpu.CompilerParams(vmem_limit_bytes=VMEM_LIMIT),
    )(*[a for item in items for a in item if a is not None])
    return [out[4 * t:4 * t + 4] for t in range(n_t)]


def _adam_call(name, slots, w, m, v, own=None):
    n_layer, rows, width = w.shape
    tile = min(rows, ADAM_BLOCK // max(width, 128))

    def body(*refs):
        s_refs = refs[:n_layer]
        own_refs = refs[n_layer:2 * n_layer] if own is not None else None
        w_ref, m_ref, v_ref = refs[-7:-4]
        layer = pl.program_id(0)
        x, y, c = _mesh_pos()
        my_slot = 4 * x + 2 * y + c
        for l in range(n_layer):
            @pl.when(layer == l)
            def _(l=l):
                if own is None:
                    part = lambda i: s_refs[l][i].astype(F32)
                else:
                    part = lambda i: jnp.where(my_slot == i, own_refs[l][...], s_refs[l][i]).astype(F32)
                _adam_update(part, w_ref, m_ref, v_ref, *refs[-4:])

    slot_spec = lambda k: _spec((N_DEV, tile, width), lambda l, i: (0, jnp.where(l == k, i, 0), 0))
    own_spec = lambda k: _spec((tile, width), lambda l, i: (jnp.where(l == k, i, 0), 0))
    blk = _spec((None, tile, width), lambda l, i: (l, i, 0))
    specs = [slot_spec(k) for k in range(n_layer)] + ([own_spec(k) for k in range(n_layer)] if own is not None else [])
    return pl.pallas_call(
        body, name=name, grid=(n_layer, rows // tile),
        in_specs=specs + [blk] * 3, out_specs=[blk] * 4, out_shape=[jax.ShapeDtypeStruct(w.shape, F32)] * 4,
        compiler_params=_cparams(2),
    )(*slots, *(own or []), w, m, v)


def kernel(x, ln_g, ln_b, rw_mu, rw_w0, rw_w1, rw_w2, rw_a0, rw_a1, rw_a2, rw_g1, rw_g2, rw_k_k, rw_k_a, rw_r_k, rw_wr, rw_wk, rw_wv, rw_wo, rw_lnx_g, rw_lnx_b, s5_a_re, s5_a_im, s5_log_dt, s5_b_re, s5_b_im, s5_c_re, s5_c_im, s5_d, s5_w_glu, mlp_w1, mlp_w2, loss_target, m_ln_g, m_ln_b, m_rw_mu, m_rw_w0, m_rw_w1, m_rw_w2, m_rw_a0, m_rw_a1, m_rw_a2, m_rw_g1, m_rw_g2, m_rw_k_k, m_rw_k_a, m_rw_r_k, m_rw_wr, m_rw_wk, m_rw_wv, m_rw_wo, m_rw_lnx_g, m_rw_lnx_b, m_s5_a_re, m_s5_a_im, m_s5_log_dt, m_s5_b_re, m_s5_b_im, m_s5_c_re, m_s5_c_im, m_s5_d, m_s5_w_glu, m_mlp_w1, m_mlp_w2, v_ln_g, v_ln_b, v_rw_mu, v_rw_w0, v_rw_w1, v_rw_w2, v_rw_a0, v_rw_a1, v_rw_a2, v_rw_g1, v_rw_g2, v_rw_k_k, v_rw_k_a, v_rw_r_k, v_rw_wr, v_rw_wk, v_rw_wv, v_rw_wo, v_rw_lnx_g, v_rw_lnx_b, v_s5_a_re, v_s5_a_im, v_s5_log_dt, v_s5_b_re, v_s5_b_im, v_s5_c_re, v_s5_c_im, v_s5_d, v_s5_w_glu, v_mlp_w1, v_mlp_w2):
    given = dict(locals())
    local_w = {n: given[n] for n in WEIGHTS}
    local_m = {n: given["m_" + n] for n in WEIGHTS}
    local_v = {n: given["v_" + n] for n in WEIGHTS}
    small = [n for n in WEIGHTS if n in SHARDED and n not in BIG]

    my_slot = 4 * lax.axis_index("x") + 2 * lax.axis_index("y") + lax.axis_index("c")
    as_bf16 = {n: local_w[n].astype(BF16) for n in BIG}

    late_started = _push_start("weights_late_start", [as_bf16[n] for n in LATE], [True] * len(LATE))
    early = [n for n in BIG if n not in LATE]
    gathered = _all_gather("weights_all_gather", [as_bf16[n] for n in early] + [local_w[n] for n in small])
    early_gathered = dict(zip(early, gathered))
    full = dict(local_w)
    for n, blk in zip(small, gathered[len(early):]):
        full[n] = jnp.moveaxis(blk, 0, SHARDED[n]).reshape(_full_shape(local_w[n].shape, SHARDED[n]))
    full['rw_w0'] = full['rw_w0'] + late_started['zero']

    def late_weights(after):
        _, lands = _push_wait("weights_late_wait", late_started, after)
        own = lambda n: as_bf16[n][None]
        return {n: lax.dynamic_update_slice(land, own(n), (my_slot,) + (0,) * as_bf16[n].ndim) for n, land in zip(LATE, lands)}

    exchanges = []

    def exchange(entries):
        started = _push_start(f"grads_start_{len(exchanges)}", [t for _, _, t in entries], [n in REPLICATED for n, _, _ in entries])
        exchanges.append((entries, started))
        return started['zero']

    loss_sq, dx, grads = _local_step(x[0], loss_target[0], full, early_gathered, late_weights, exchange)
    loss = (0.5 / D_MODEL) * lax.psum(loss_sq, MESH_AXES)

    wide = {'s5_b_re': (SSM_LANES, SSM_GROUP), 's5_b_im': (SSM_LANES, SSM_GROUP),
            's5_c_re': (D_MODEL, SSM_STATE), 's5_c_im': (D_MODEL, SSM_STATE)}
    by_name, many = {}, []
    landed = {n: {} for n in BIG}
    for e, (entries, started) in enumerate(exchanges):
        sources, lands = _push_wait(f"grads_wait_{e}", started, dx)
        for (n, i, _), src, slots in zip(entries, sources, lands):
            if n in wide:
                as_view = lambda t: t.reshape((1,) + wide[n])
                out = _adam_call(f"adamw_{n}", [slots.reshape((N_DEV,) + wide[n])], as_view(local_w[n]), as_view(local_m[n]),
                                 as_view(local_v[n]), own=[src.reshape(wide[n])])
                by_name[n] = [t.reshape(local_w[n].shape) for t in out]
            elif n in REPLICATED:
                many.append((n, (slots, local_w[n], local_m[n], local_v[n], src)))
            else:
                landed[n][i] = (slots, lax.dynamic_index_in_dim(src, my_slot, 0, keepdims=False))
    for n in BIG:
        layers = [landed[n][i] for i in sorted(landed[n])]
        by_name[n] = _adam_call(f"adamw_{n}", [s for s, _ in layers], local_w[n], local_m[n], local_v[n], own=[o for _, o in layers])
    split = lambda n: jnp.moveaxis(grads[n].reshape(_split_shape(local_w[n].shape, SHARDED[n])), SHARDED[n], 0)
    for n, slots in zip(small, _all_to_all("grads_all_to_all", [split(n) for n in small])):
        many.append((n, (slots, local_w[n], local_m[n], local_v[n], None)))
    last = [n for n in REPLICATED if n not in by_name and n not in dict(many)]
    for n, slots in zip(last, _all_gather("grads_all_gather", [grads[n] for n in last])):
        many.append((n, (slots, local_w[n], local_m[n], local_v[n], None)))
    for (n, _), out in zip(many, _adam_many("adamw_small", [item for _, item in many])):
        by_name[n] = out
    results = [by_name[n][k] for k in range(4) for n in WEIGHTS]
    return (loss, dx[None], *results)
```

```python
import functools
import math

import jax
import jax.numpy as jnp
from jax import lax
from jax.experimental import pallas as pl
from jax.experimental.pallas import tpu as pltpu

F32 = jnp.float32
BF16 = jnp.bfloat16
HI = lax.Precision.HIGHEST

D_MODEL = 1024
HEAD = 64
PAIR = 2 * HEAD
N_PAIR = D_MODEL // PAIR
CHUNK = 64
GN_EPS = 64e-5
LN_EPS = 1e-5
SSM_GROUP = 16
SSM_STATE = 64
SSM_GROUPS = D_MODEL // SSM_GROUP
SSM_LANES = SSM_GROUPS * SSM_STATE
SSM_BLK_IN = 128
SSM_BLK_ST = 512
N_SSM_BLK = D_MODEL // SSM_BLK_IN
DEPTH = 2
DN_ALPHA = (2.0 * DEPTH) ** 0.25
ADAM_LR, ADAM_B1, ADAM_B2, ADAM_EPS, ADAM_WD, ADAM_STEP = 0.001, 0.9, 0.999, 1e-08, 0.01, 10
N_DEV = 8
MESH_AXES = ("x", "y", "c")
VMEM_LIMIT = 56 * 1024 * 1024
TOKEN_TILE = 256
LN_TILE = 512

SHARDED = {
    "rw_mu": 2, "rw_w1": 1, "rw_w2": 2, "rw_a1": 1, "rw_a2": 2, "rw_g1": 1, "rw_g2": 2,
    "rw_wr": 1, "rw_wk": 1, "rw_wv": 1, "rw_wo": 1, "s5_d": 1, "s5_w_glu": 2, "mlp_w1": 2, "mlp_w2": 1,
}
WEIGHTS = ['ln_g', 'ln_b', 'rw_mu', 'rw_w0', 'rw_w1', 'rw_w2', 'rw_a0', 'rw_a1', 'rw_a2', 'rw_g1', 'rw_g2', 'rw_k_k',
           'rw_k_a', 'rw_r_k', 'rw_wr', 'rw_wk', 'rw_wv', 'rw_wo', 'rw_lnx_g', 'rw_lnx_b', 's5_a_re', 's5_a_im',
           's5_log_dt', 's5_b_re', 's5_b_im', 's5_c_re', 's5_c_im', 's5_d', 's5_w_glu', 'mlp_w1', 'mlp_w2']
REPLICATED = [n for n in WEIGHTS if n not in SHARDED]
BIG = ['rw_wr', 'rw_wk', 'rw_wv', 'rw_wo', 's5_w_glu', 'mlp_w1', 'mlp_w2']
LATE = ['s5_w_glu', 'mlp_w1', 'mlp_w2']
GRAD_DTYPE = BF16


def _cparams(n_grid):
    return pltpu.CompilerParams(dimension_semantics=("arbitrary",) * n_grid, vmem_limit_bytes=VMEM_LIMIT)


@jax.custom_vjp
def _mm(x, w):
    return jnp.dot(x.astype(BF16), w.astype(BF16), preferred_element_type=F32)


def _mm_fwd(x, w):
    return _mm(x, w), (x, w)


def _mm_bwd(res, dy):
    x, w = res
    dyb = dy.astype(BF16)
    dx = lax.dot_general(dyb, w.astype(BF16), (((1,), (1,)), ((), ())), preferred_element_type=F32)
    dw = lax.dot_general(x.astype(BF16), dyb, (((0,), (0,)), ((), ())), preferred_element_type=F32)
    return dx, dw


_mm.defvjp(_mm_fwd, _mm_bwd)


_DOT_DIMS = {'nn': (((1,), (0,)), ((), ())), 'nt': (((1,), (1,)), ((), ())), 'tn': (((0,), (0,)), ((), ()))}
_BATCH_DOT_DIMS = {'nn': (((2,), (1,)), ((0,), (0,))), 'nt': (((2,), (2,)), ((0,), (0,))), 'tn': (((1,), (1,)), ((0,), (0,)))}
CHUNK_PAIRS = 8
CHUNK_PASSES = 1


def _split_bf16(a):
    hi = a.astype(BF16)
    return hi, (a - hi.astype(F32)).astype(BF16)


def _dot_passes(a, b, mode, passes=None):
    dims = (_DOT_DIMS if a.ndim == 2 else _BATCH_DOT_DIMS)[mode]
    dot = lambda p, q: lax.dot_general(p, q, dims, preferred_element_type=F32)
    if (CHUNK_PASSES if passes is None else passes) == 1:
        return dot(a.astype(BF16), b.astype(BF16))
    (ah, al), (bh, bl) = _split_bf16(a), _split_bf16(b)
    return dot(ah, bh) + (dot(ah, bl) + dot(al, bh))


@functools.partial(jax.custom_vjp, nondiff_argnums=(2,))
def _pdot(a, b, mode):
    return _dot_passes(a, b, mode)


def _pdot_fwd(a, b, mode):
    return _dot_passes(a, b, mode), (a, b)


def _pdot_bwd(mode, res, dy):
    a, b = res
    if mode == 'nn':
        return _dot_passes(dy, b, 'nt'), _dot_passes(a, dy, 'tn')
    if mode == 'nt':
        return _dot_passes(dy, b, 'nn'), _dot_passes(dy, a, 'tn')
    return _dot_passes(b, dy, 'nt'), _dot_passes(a, dy, 'nn')


_pdot.defvjp(_pdot_fwd, _pdot_bwd)


def _tri_sum(x, upper):
    nb, n = x.shape[0], x.shape[1]
    ti = lax.broadcasted_iota(jnp.int32, (nb, n, n), 1)
    tj = lax.broadcasted_iota(jnp.int32, (nb, n, n), 2)
    tri = ((ti <= tj) if upper else (ti >= tj)).astype(BF16)
    hi = x.astype(BF16)
    rest = x - hi.astype(F32)
    mid = rest.astype(BF16)
    lo = (rest - mid.astype(F32)).astype(BF16)
    dot = lambda q: lax.dot_general(tri, q, _BATCH_DOT_DIMS['nn'], preferred_element_type=F32)
    return dot(hi) + (dot(mid) + dot(lo))


@jax.custom_vjp
def _cumsum_rows(x):
    return _tri_sum(x, False)


_cumsum_rows.defvjp(lambda x: (_tri_sum(x, False), None), lambda _, dy: (_tri_sum(dy, True),))


def _power_sum(a, order):
    m = a.shape[-1]
    ti = lax.broadcasted_iota(jnp.int32, (1, m, m), 1)
    tj = lax.broadcasted_iota(jnp.int32, (1, m, m), 2)
    pw, total = _dot_passes(a, a, 'nn'), (ti == tj).astype(F32) + a
    for _ in range(int(math.log2(order)) - 2):
        both = _dot_passes(pw, jnp.concatenate([pw, total], axis=2), 'nn')
        pw, total = both[:, :, :m], total + both[:, :, m:]
    return total + _dot_passes(pw, total, 'nn')


@functools.partial(jax.custom_vjp, nondiff_argnums=(1,))
def _nilpotent_inverse(a, order):
    return _power_sum(a, order)


def _nilpotent_inverse_fwd(a, order):
    inv = _power_sum(a, order)
    return inv, inv


def _nilpotent_inverse_bwd(order, inv, g):
    return (_dot_passes(_dot_passes(inv, g, 'tn'), inv, 'nt'),)


_nilpotent_inverse.defvjp(_nilpotent_inverse_fwd, _nilpotent_inverse_bwd)


@jax.custom_vjp
def _known_inverse(a, inv):
    return inv


_known_inverse.defvjp(lambda a, inv: (inv, inv),
                      lambda inv, g: (_dot_passes(_dot_passes(inv, g, 'tn'), inv, 'nt'), jnp.zeros_like(inv)))


@functools.partial(jax.custom_vjp, nondiff_argnums=(1,))
def _roll_rows(x, shift):
    return pltpu.roll(x, shift, 0)


def _roll_rows_fwd(x, shift):
    return pltpu.roll(x, shift, 0), None


def _roll_rows_bwd(shift, _, dy):
    n = dy.shape[0]
    return (pltpu.roll(dy, (n - shift) % n, 0),)


_roll_rows.defvjp(_roll_rows_fwd, _roll_rows_bwd)


def _shift_down(x, first_row):
    row = lax.broadcasted_iota(jnp.int32, x.shape, 0)
    return jnp.where(row == 0, first_row, _roll_rows(x, 1))


def _sigmoid(x):
    return 1.0 / (1.0 + jnp.exp(-x))


def _softplus(x):
    return jnp.maximum(x, 0.0) + jnp.log(1.0 + jnp.exp(-jnp.abs(x)))


def _gelu(x):
    return 0.5 * x * (1.0 + jnp.tanh(math.sqrt(2.0 / math.pi) * (x + 0.044715 * (x * x * x))))


def _layer_norm(x, g, b):
    mu = jnp.mean(x, axis=-1, keepdims=True)
    xc = x - mu
    var = jnp.mean(xc * xc, axis=-1, keepdims=True)
    return xc * lax.rsqrt(var + LN_EPS) * g + b


def _spec(block, index_map):
    return pl.BlockSpec(block, index_map)


def _tile(arr, tm):
    return (arr, (tm, arr.shape[1]), lambda i: (i, 0))


def _full(arr):
    nd = arr.ndim
    return (arr, arr.shape, lambda *_: (0,) * nd)


def _fwd_call(name, fn, grid, prims, outs):
    n_in = len(prims)

    def body(*refs):
        vals = [r[...] for r in refs[:n_in]]
        vals = [v.astype(F32) if v.dtype != F32 else v for v in vals]
        res = fn(*vals)
        for o, r in zip(refs[n_in:], res):
            o[...] = r.astype(o.dtype)

    return pl.pallas_call(
        body, name=name, grid=grid,
        in_specs=[_spec(b, m) for _, b, m in prims],
        out_specs=[_spec(b, m) for _, _, b, m in outs],
        out_shape=[jax.ShapeDtypeStruct(s, d) for s, d, _, _ in outs],
        compiler_params=_cparams(len(grid)),
    )(*[a for a, _, _ in prims])


def _bwd_call(name, fn, grid, prims, cts, wants, addends=None):
    addends = addends or {}
    n_in, n_ct = len(prims), len(cts)
    out_place = {}
    wants, out_dtype = list(wants), {}
    for i, w in enumerate(wants):
        if isinstance(w, tuple):
            wants[i], out_place[i] = w
        elif w == 'tile_bf16':
            wants[i], out_dtype[i] = 'tile', BF16
    diff = [i for i, w in enumerate(wants) if w]
    add_idx = [i for i in diff if i in addends]
    last_axis = len(grid) - 1

    def body(*refs):
        vals = [r[...] for r in refs[:n_in]]
        vals = [v.astype(F32) if v.dtype != F32 else v for v in vals]
        ct_vals = [r[...] for r in refs[n_in:n_in + n_ct]]
        ct_vals = [v.astype(F32) if v.dtype != F32 else v for v in ct_vals]
        add_refs = dict(zip(add_idx, refs[n_in + n_ct:n_in + n_ct + len(add_idx)]))
        out_refs = refs[n_in + n_ct + len(add_idx):]

        def f(*dargs):
            full = list(vals)
            for i, a in zip(diff, dargs):
                full[i] = a
            return tuple(fn(*full))

        _, vjp = jax.vjp(f, *[vals[i] for i in diff])
        grads = vjp(tuple(ct_vals))
        first = pl.program_id(last_axis) == 0
        for o, g, i in zip(out_refs, grads, diff):
            if wants[i] == 'tile':
                if i in add_refs:
                    g = g + add_refs[i][...]
                o[...] = g.astype(o.dtype)
            else:
                @pl.when(first)
                def _(o=o):
                    o[...] = jnp.zeros(o.shape, o.dtype)
                o[...] += g

    ins = list(prims) + list(cts) + [addends[i] for i in add_idx]
    places = [out_place.get(i, (prims[i][0].shape, prims[i][1], prims[i][2])) for i in diff]
    return pl.pallas_call(
        body, name=name, grid=grid,
        in_specs=[_spec(b, m) for _, b, m in ins],
        out_specs=[_spec(b, m) for _, b, m in places],
        out_shape=[jax.ShapeDtypeStruct(s, out_dtype.get(i, F32)) for i, (s, _, _) in zip(diff, places)],
        compiler_params=_cparams(len(grid)),
    )(*[a for a, _, _ in ins])


def _matmul(name, a, b, mode, *, out_dtype=F32, lhs_fn=None, epi=None, extra=None, tm=1024, tn=1024, tk=1024,
            b_cols=None, b_rows=None, out_cols=False):
    if mode == 'tn':
        kdim, m = a.shape
    else:
        m, kdim = a.shape
    if b_rows is not None:
        loc, cols = b.shape[-2], b.shape[-1]
        if mode == 'nn':
            n, k_shards = cols, max(1, tk // loc)
            tk = k_shards * loc
        else:
            n, tn = N_DEV * loc, loc
    elif b_cols is None:
        n = b.shape[0] if mode == 'nt' else b.shape[1]
    else:
        rows, loc = b.shape[-2], b.shape[-1]
        if mode == 'nn':
            n, n_shards = N_DEV * loc, max(1, tn // loc)
            tn = n_shards * loc
        else:
            n, k_shards = rows, max(1, tk // loc)
            tk = k_shards * loc
    if out_cols:
        o_shards = max(1, min(tn, n) // (n // N_DEV))
        tn = o_shards * (n // N_DEV)
    tm, tn, tk = min(tm, m), min(tn, n), min(tk, kdim)
    nk = kdim // tk
    grid = (m // tm, n // tn, nk)
    a_spec = _spec((tk, tm), lambda i, j, k: (k, i)) if mode == 'tn' else _spec((tm, tk), lambda i, j, k: (i, k))
    if b_rows is not None:
        lead = tuple(b_rows)
        skip = (None,) * (1 + len(lead))
        if mode == 'nn':
            b_spec = _spec((k_shards,) + skip[1:] + (tk // k_shards, tn), lambda i, j, k: (k,) + lead + (0, j))
        else:
            b_spec = _spec(skip + (tn, tk), lambda i, j, k: (j,) + lead + (0, k))
    elif b_cols is None:
        b_spec = _spec((tn, tk), lambda i, j, k: (j, k)) if mode == 'nt' else _spec((tk, tn), lambda i, j, k: (k, j))
    else:
        lead = tuple(b_cols)
        skip = (None,) * (1 + len(lead))
        if mode == 'nn':
            b_spec = _spec((n_shards,) + skip[1:] + (tk, tn // n_shards), lambda i, j, k: (j,) + lead + (k, 0))
        else:
            b_spec = _spec((k_shards,) + skip[1:] + (tn, tk // k_shards), lambda i, j, k: (k,) + lead + (j, 0))
    o_spec = _spec((tm, tn), lambda i, j, k: (i, j))
    if out_cols:
        o_place, o_shape = _spec((o_shards, tm, tn // o_shards), lambda i, j, k: (j, i, 0)), (N_DEV, m, n // N_DEV)
    else:
        o_place, o_shape = o_spec, (m, n)
    dims = _DOT_DIMS[mode]

    def body(*refs):
        if extra is None:
            a_ref, b_ref, o_ref, acc = refs
            x_ref = None
        else:
            a_ref, b_ref, x_ref, o_ref, acc = refs
        k = pl.program_id(2)
        av = a_ref[...]
        if lhs_fn is not None:
            av = lhs_fn(av.astype(F32))
        bv = b_ref[...]
        if bv.ndim == 3:
            bv = bv.reshape(tk, tn) if b_rows is not None else jnp.concatenate([bv[g] for g in range(bv.shape[0])], axis=1)
        part = lax.dot_general(av.astype(BF16), bv.astype(BF16), dims, preferred_element_type=F32)

        def finish(r):
            if epi is not None:
                r = epi(r, x_ref[...])
            if out_cols:
                width = tn // o_shards
                for g in range(o_shards):
                    o_ref[g] = r[:, g * width:(g + 1) * width].astype(o_ref.dtype)
            else:
                o_ref[...] = r.astype(o_ref.dtype)

        if nk == 1:
            finish(part)
            return

        @pl.when(k == 0)
        def _():
            acc[...] = part

        @pl.when((k > 0) & (k < nk - 1))
        def _():
            acc[...] += part

        @pl.when(k == nk - 1)
        def _():
            finish(acc[...] + part)

    ops, specs = [a, b], [a_spec, b_spec]
    if extra is not None:
        ops.append(extra)
        specs.append(o_spec)
    return pl.pallas_call(
        body, name=name, grid=grid, in_specs=specs, out_specs=o_place,
        out_shape=jax.ShapeDtypeStruct(o_shape, out_dtype),
        scratch_shapes=[pltpu.VMEM((tm, tn), F32)],
        compiler_params=_cparams(3),
    )(*ops)


def _relu2(x):
    r = jnp.maximum(x, 0.0)
    return r * r


def _mix_fn(x, halo, mu, w0, w1, w2, a0, a1, a2, g1, g2):
    prev_row = jnp.where(pl.program_id(0) == 0, 0.0, halo[7:8, :])
    xx = _shift_down(x, prev_row) - x
    xr, xw, xk, xv, xa, xg = (x + xx * mu[i:i + 1, :] for i in range(6))
    w_pre = w0 + _mm(jnp.tanh(_mm(xw, w1)), w2)
    log_decay = -jnp.exp(-_softplus(-w_pre) - 0.5)
    a = _sigmoid(a0 + _mm(_mm(xa, a1), a2))
    g = _mm(_sigmoid(_mm(xg, g1)), g2)
    return xr, xk, xv, log_decay, a, g


def _chunk_fn(r, lw, k, v, a, k_k, k_a, r_k, inverse=None, with_inverse=False):
    n, n_pair = r.shape[0], r.shape[1] // PAIR
    to_batch = lambda t: jnp.concatenate([t[None, :, p * PAIR:(p + 1) * PAIR] for p in range(n_pair)], axis=0)
    from_batch = lambda t: jnp.concatenate([t[p] for p in range(n_pair)], axis=1)
    r, lw, k, v, a, k_k, k_a, r_k = (to_batch(t) for t in (r, lw, k, v, a, k_k, k_a, r_k))
    lane = lax.broadcasted_iota(jnp.int32, (1, 1, PAIR), 2)
    m0 = lane < HEAD

    def head_sum(t):
        s0 = jnp.sum(jnp.where(m0, t, 0.0), axis=2, keepdims=True)
        s1 = jnp.sum(jnp.where(m0, 0.0, t), axis=2, keepdims=True)
        return jnp.where(m0, s0, s1)

    kk = k * k_k
    kk = kk / jnp.maximum(jnp.sqrt(head_sum(kk * kk)), 1e-12)
    k2 = k * (1.0 + (a - 1.0) * k_a)
    aa, bb = -kk, kk * a
    bonus = head_sum(r * k2 * r_k) * v

    cum = _cumsum_rows(lw)
    p_in, p_ex, p_inv = jnp.exp(cum), jnp.exp(cum - lw), jnp.exp(-cum)
    at, rt, kt, bt = aa * p_ex, r * p_in, k2 * p_inv, bb * p_inv

    def stack_masked(t):
        return jnp.concatenate([jnp.where(m0, t, 0.0), jnp.where(m0, 0.0, t)], axis=1)

    def unstack_sum(t):
        return t[:, :n] + t[:, n:]

    def unstack_select(t):
        return jnp.where(m0, t[:, :n], t[:, n:])

    ti = lax.broadcasted_iota(jnp.int32, (1, 2 * n, 2 * n), 1)
    tj = lax.broadcasted_iota(jnp.int32, (1, 2 * n, 2 * n), 2)
    same = (ti >= n) == (tj >= n)
    incl, strict = same & (ti >= tj), same & (ti > tj)
    m = 2 * n
    rows = lambda *ts: jnp.concatenate(ts, axis=1)
    cols = lambda *ts: jnp.concatenate(ts, axis=2)
    at_s, rt_s, kt_s, bt_s = stack_masked(at), stack_masked(rt), stack_masked(kt), stack_masked(bt)
    v_s, zeros = rows(v, v), jnp.zeros_like(at_s)
    bk_s = rows(bt_s, kt_s)
    gram = _pdot(rows(at_s, rt_s), bk_s, 'nt')
    a_ab = jnp.where(strict, gram[:, :m, :m], 0.0)
    a_ak = jnp.where(strict, gram[:, :m, m:], 0.0)
    a_rb = jnp.where(incl, gram[:, m:, :m], 0.0)
    a_rk = jnp.where(incl, gram[:, m:, m:], 0.0)
    inv = _nilpotent_inverse(a_ab, n) if inverse is None else _known_inverse(a_ab, inverse)
    wy_s = _pdot(inv, cols(at_s, _pdot(a_ak, v_s, 'nn')), 'nn')
    q_oi = _pdot(cols(a_rb, a_rk), rows(wy_s, cols(zeros, v_s)), 'nn')
    q_all = unstack_sum(rt_s + q_oi[:, :, :PAIR])
    oi_all = unstack_select(q_oi[:, :, PAIR:])
    own_head = (lax.broadcasted_iota(jnp.int32, (1, m, PAIR), 1) >= n) == (lax.broadcasted_iota(jnp.int32, (1, m, PAIR), 2) >= HEAD)
    wy_m = cols(wy_s[:, :, :PAIR], jnp.where(own_head, wy_s[:, :, PAIR:], 0.0))
    mm_nn = _pdot(bk_s, rows(wy_m, cols(zeros, stack_masked(v))), 'tn')
    mm, nn = mm_nn[:, :, :PAIR], mm_nn[:, :, PAIR:]
    ei = lax.broadcasted_iota(jnp.int32, (1, PAIR, PAIR), 1)
    ej = lax.broadcasted_iota(jnp.int32, (1, PAIR, PAIR), 2)
    eye_p = (ei == ej).astype(F32)
    decay_col = jnp.sum(eye_p * p_in[:, n - 1:n, :], axis=2, keepdims=True)
    maps = ((decay_col * (eye_p + mm))[None], (decay_col * nn)[None], from_batch(q_all), from_batch(oi_all),
            from_batch(bonus))
    return maps + (inv,) if with_inverse else maps


def _gate_fn(o, bonus, g, lnx_g, lnx_b):
    lane = lax.broadcasted_iota(jnp.int32, (1, PAIR), 1)
    m0 = lane < HEAD

    def head_mean(t):
        s0 = jnp.sum(jnp.where(m0, t, 0.0), axis=1, keepdims=True)
        s1 = jnp.sum(jnp.where(m0, 0.0, t), axis=1, keepdims=True)
        return jnp.where(m0, s0, s1) * (1.0 / HEAD)

    outs = []
    for p in range(N_PAIR):
        sl = slice(p * PAIR, (p + 1) * PAIR)
        oc = o[:, sl] - head_mean(o[:, sl])
        on = oc * lax.rsqrt(head_mean(oc * oc) + GN_EPS) * lnx_g[:, sl] + lnx_b[:, sl]
        outs.append((on + bonus[:, sl]) * g[:, sl])
    return (jnp.concatenate(outs, axis=1),)


def _res_ln_fn(h, y, g, b):
    return (_layer_norm(DN_ALPHA * h + y, g, b),)


def _glu_ln_fn(h, z, g, b):
    mix = z[:, :D_MODEL] * _sigmoid(z[:, D_MODEL:])
    return (_layer_norm(DN_ALPHA * h + mix, g, b),)


def _s5_param_fn(a_re, a_im, log_dt, b_re, b_im):
    dt = jnp.exp(log_dt)
    lam_re = jnp.minimum(a_re, -1e-4)
    lam_im = a_im
    mag = jnp.exp(dt * lam_re)
    abar_re = mag * jnp.cos(dt * lam_im)
    abar_im = mag * jnp.sin(dt * lam_im)
    den = lam_re * lam_re + lam_im * lam_im
    nr, ni = abar_re - 1.0, abar_im
    coef_re = (nr * lam_re + ni * lam_im) / den
    coef_im = (ni * lam_re - nr * lam_im) / den
    return abar_re, abar_im, coef_re, coef_im


def _s5_bbar_fn(coef_re, coef_im, b_re, b_im):
    return coef_re * b_re - coef_im * b_im, coef_re * b_im + coef_im * b_re


def _s5_in_fn(u, b_re, b_im):
    return _pdot(u, b_re[0], 'nn'), _pdot(u, b_im[0], 'nn')


def _s5_out_fn(s_re, s_im, u, c_re, c_im, d):
    y = _pdot(s_re, c_re[0], 'nn') - _pdot(s_im, c_im[0], 'nn') + u * d
    return (_gelu(y),)


def _pairs_to_batch(t):
    return jnp.concatenate([t[None, :, p * PAIR:(p + 1) * PAIR] for p in range(N_PAIR)], axis=0)


def _batch_to_pairs(t):
    return jnp.concatenate([t[p] for p in range(N_PAIR)], axis=1)


STATE_PASSES = 3


def _rwkv_scan_fwd(r, lw, k, v, a, k_k, k_a, r_k):
    n_tok = r.shape[0]
    n_chunk = n_tok // CHUNK

    def body(r_ref, lw_ref, k_ref, v_ref, a_ref, kk_ref, ka_ref, rk_ref, o_ref, bonus_ref, zs_ref, inv_ref, m_ref, q_ref, z):
        @pl.when(pl.program_id(0) == 0)
        def _():
            z[...] = jnp.zeros(z.shape, F32)

        m_all, n_all, q, oi, bonus, inv = _chunk_fn(r_ref[...], lw_ref[...], k_ref[...], v_ref[...], a_ref[...],
                                                    kk_ref[...], ka_ref[...], rk_ref[...], with_inverse=True)
        zb = z[...]
        zs_ref[0] = zb
        inv_ref[0] = inv.astype(inv_ref.dtype)
        m_ref[...] = m_all
        q_ref[...] = q
        bonus_ref[...] = bonus
        o_ref[...] = _batch_to_pairs(_dot_passes(_pairs_to_batch(q), zb, 'nn', STATE_PASSES)) + oi
        z[...] = _dot_passes(m_all[0], zb, 'nn', STATE_PASSES) + n_all[0]

    tok = _spec((CHUNK, D_MODEL), lambda c: (c, 0))
    par = _spec((1, D_MODEL), lambda c: (0, 0))
    st = _spec((1, N_PAIR, PAIR, PAIR), lambda c: (c, 0, 0, 0))
    return pl.pallas_call(
        body, name="rwkv_scan_fwd", grid=(n_chunk,),
        in_specs=[tok] * 5 + [par] * 3, out_specs=[tok, tok, st, st, st, tok],
        out_shape=[jax.ShapeDtypeStruct((n_tok, D_MODEL), F32)] * 2
        + [jax.ShapeDtypeStruct((n_chunk, N_PAIR, PAIR, PAIR), F32), jax.ShapeDtypeStruct((n_chunk, N_PAIR, PAIR, PAIR), BF16),
           jax.ShapeDtypeStruct((n_chunk, N_PAIR, PAIR, PAIR), F32), jax.ShapeDtypeStruct((n_tok, D_MODEL), F32)],
        scratch_shapes=[pltpu.VMEM((N_PAIR, PAIR, PAIR), F32)],
        compiler_params=_cparams(1),
    )(r, lw, k, v, a, k_k, k_a, r_k)


def _rwkv_scan_bwd(r, lw, k, v, a, k_k, k_a, r_k, zs, invs, ms, q, d_o, d_bonus):
    n_tok = r.shape[0]
    n_chunk = n_tok // CHUNK

    def body(r_ref, lw_ref, k_ref, v_ref, a_ref, kk_ref, ka_ref, rk_ref, zs_ref, inv_ref, m_ref, q_ref, do_ref, db_ref,
             dr_ref, dlw_ref, dk_ref, dv_ref, da_ref, dkk_ref, dka_ref, drk_ref, g):
        sums = (dkk_ref, dka_ref, drk_ref)

        @pl.when(pl.program_id(0) == 0)
        def _():
            g[...] = jnp.zeros(g.shape, F32)
            for s_ref in sums:
                s_ref[...] = jnp.zeros(s_ref.shape, F32)

        prim = (r_ref[...], lw_ref[...], k_ref[...], v_ref[...], a_ref[...], kk_ref[...], ka_ref[...], rk_ref[...])
        known = inv_ref[0].astype(F32)
        _, chunk_vjp = jax.vjp(lambda *p: _chunk_fn(*p, inverse=known), *prim)
        m_all, q = m_ref[...], q_ref[...]
        gb, zb, d_out = g[...], zs_ref[0], do_ref[...]
        dob = _pairs_to_batch(d_out)
        d_m = _dot_passes(gb, zb, 'nt', STATE_PASSES)
        d_q = _batch_to_pairs(_dot_passes(dob, zb, 'nt', STATE_PASSES))
        grads = chunk_vjp((d_m[None], gb[None], d_q, d_out, db_ref[...]))
        for out_ref, val in zip((dr_ref, dlw_ref, dk_ref, dv_ref, da_ref), grads[:5]):
            out_ref[...] = val.astype(out_ref.dtype)
        for s_ref, val in zip(sums, grads[5:]):
            s_ref[...] += val
        same_head = (lax.broadcasted_iota(jnp.int32, (1, PAIR, PAIR), 1) // HEAD
                     == lax.broadcasted_iota(jnp.int32, (1, PAIR, PAIR), 2) // HEAD)
        g[...] = (_dot_passes(m_all[0], gb, 'tn', STATE_PASSES)
                  + jnp.where(same_head, _dot_passes(_pairs_to_batch(q), dob, 'tn', STATE_PASSES), 0.0))

    tok = _spec((CHUNK, D_MODEL), lambda c: (n_chunk - 1 - c, 0))
    par = _spec((1, D_MODEL), lambda c: (0, 0))
    st = _spec((1, N_PAIR, PAIR, PAIR), lambda c: (n_chunk - 1 - c, 0, 0, 0))
    tok_shape = lambda dt: jax.ShapeDtypeStruct((n_tok, D_MODEL), dt)
    return pl.pallas_call(
        body, name="rwkv_scan_bwd", grid=(n_chunk,),
        in_specs=[tok] * 5 + [par] * 3 + [st, st, st, tok, tok, tok], out_specs=[tok] * 5 + [par] * 3,
        out_shape=[tok_shape(BF16), tok_shape(F32), tok_shape(BF16), tok_shape(BF16), tok_shape(F32)]
        + [jax.ShapeDtypeStruct((1, D_MODEL), F32)] * 3,
        scratch_shapes=[pltpu.VMEM((N_PAIR, PAIR, PAIR), F32)],
        compiler_params=_cparams(1),
    )(r, lw, k, v, a, k_k, k_a, r_k, zs, invs, ms, q, d_o, d_bonus)


S5_TIME_TILE = 1024
S5_GROUP = 8


def _scan_rows(re_ref, im_ref, a1, carry, reverse):
    lb = re_ref.shape[1]
    grp, n_grp = S5_GROUP, re_ref.shape[0] // S5_GROUP

    def cmul(xr, xi, yr, yi):
        return xr * yr - xi * yi, xr * yi + xi * yr

    a2 = cmul(*a1, *a1)
    a4 = cmul(*a2, *a2)
    a8 = cmul(*a4, *a4)
    row = lax.broadcasted_iota(jnp.int32, (grp, lb), 0)
    expo = (grp - row) if reverse else (row + 1)
    pw = (jnp.ones((grp, lb), F32), jnp.zeros((grp, lb), F32))
    for bit, ap in ((1, a1), (2, a2), (4, a4), (8, a8)):
        nxt = cmul(*pw, *ap)
        sel = (expo & bit) != 0
        pw = (jnp.where(sel, nxt[0], pw[0]), jnp.where(sel, nxt[1], pw[1]))

    steps = []
    for d, ad in ((1, a1), (2, a2), (4, a4)):
        keep = (row < grp - d) if reverse else (row >= d)
        steps.append(((grp - d) if reverse else d, jnp.where(keep, ad[0], 0.0), jnp.where(keep, ad[1], 0.0)))

    def group(i, c):
        gi = (n_grp - 1 - i) if reverse else i
        rows = pl.ds(pl.multiple_of(gi * grp, grp), grp)
        xr, xi = re_ref[rows, :], im_ref[rows, :]
        for shift, mr, mi in steps:
            pr, pi = cmul(mr, mi, pltpu.roll(xr, shift, 0), pltpu.roll(xi, shift, 0))
            xr, xi = xr + pr, xi + pi
        cr, ci = cmul(*pw, c[0], c[1])
        xr, xi = xr + cr, xi + ci
        re_ref[rows, :] = xr
        im_ref[rows, :] = xi
        edge = slice(0, 1) if reverse else slice(grp - 1, grp)
        return xr[edge, :], xi[edge, :]

    return lax.fori_loop(0, n_grp, group, carry)


def _s5_fwd(u, b_re, b_im, a_re, a_im, c_re, c_im, d_row):
    n_tok = u.shape[0]
    tt = min(S5_TIME_TILE, n_tok)

    def body(u_ref, bre_ref, bim_ref, are_ref, aim_ref, cre_ref, cim_ref, d_ref, sre_ref, sim_ref, yg_ref, carry):
        @pl.when(pl.program_id(1) == 0)
        def _():
            carry[...] = jnp.zeros(carry.shape, F32)

        uv = u_ref[...]
        sre_ref[...], sim_ref[...] = _s5_in_fn(uv, bre_ref[...], bim_ref[...])
        c = _scan_rows(sre_ref, sim_ref, (are_ref[...], aim_ref[...]), (carry[0:1, :], carry[1:2, :]), False)
        carry[0:1, :] = c[0]
        carry[1:2, :] = c[1]
        (yg,) = _s5_out_fn(sre_ref[...], sim_ref[...], uv, cre_ref[...], cim_ref[...], d_ref[...])
        yg_ref[...] = yg.astype(yg_ref.dtype)

    u_blk = _spec((tt, SSM_BLK_IN), lambda l, t: (t, l))
    s_blk = _spec((tt, SSM_BLK_ST), lambda l, t: (t, l))
    blk3 = lambda arr: _spec((1,) + arr.shape[1:], lambda l, t: (l, 0, 0))
    par = lambda width: _spec((1, width), lambda l, t: (0, l))
    return pl.pallas_call(
        body, name="s5_fwd", grid=(N_SSM_BLK, n_tok // tt),
        in_specs=[u_blk, blk3(b_re), blk3(b_im), par(SSM_BLK_ST), par(SSM_BLK_ST), blk3(c_re), blk3(c_im), par(SSM_BLK_IN)],
        out_specs=[s_blk, s_blk, u_blk],
        out_shape=[jax.ShapeDtypeStruct((n_tok, SSM_LANES), F32)] * 2 + [jax.ShapeDtypeStruct((n_tok, D_MODEL), BF16)],
        scratch_shapes=[pltpu.VMEM((2, SSM_BLK_ST), F32)],
        compiler_params=_cparams(2),
    )(u, b_re, b_im, a_re, a_im, c_re, c_im, d_row)


def _s5_bwd(dyg, u, s_re, s_im, dh_res, b_re, b_im, a_re, a_im, c_re, c_im, d_row):
    n_tok = u.shape[0]
    tt = min(S5_TIME_TILE, n_tok)
    n_t = n_tok // tt

    def body(dyg_ref, u_ref, sre_ref, sim_ref, hre_ref, him_ref, res_ref, bre_ref, bim_ref, are_ref, aim_ref,
             cre_ref, cim_ref, d_ref, du_ref, dbre_ref, dbim_ref, dcre_ref, dcim_ref, dd_ref, dare_ref, daim_ref,
             carry, gre, gim):
        i = pl.program_id(1)
        sums = (dbre_ref, dbim_ref, dcre_ref, dcim_ref, dd_ref, dare_ref, daim_ref)

        @pl.when(i == 0)
        def _():
            carry[...] = jnp.zeros(carry.shape, F32)
            for r in sums:
                r[...] = jnp.zeros(r.shape, F32)

        uv, sre, sim = u_ref[...], sre_ref[...], sim_ref[...]
        _, out_vjp = jax.vjp(_s5_out_fn, sre, sim, uv, cre_ref[...], cim_ref[...], d_ref[...])
        gre[...], gim[...], du_out, dcre, dcim, dd = out_vjp((dyg_ref[...],))
        c = _scan_rows(gre, gim, (are_ref[...], -aim_ref[...]), (carry[0:1, :], carry[1:2, :]), True)
        carry[0:1, :] = c[0]
        carry[1:2, :] = c[1]
        g_re, g_im = gre[...], gim[...]
        at_start = i == n_t - 1
        p_re = _shift_down(sre, jnp.where(at_start, 0.0, hre_ref[7:8, :]))
        p_im = _shift_down(sim, jnp.where(at_start, 0.0, him_ref[7:8, :]))
        _, in_vjp = jax.vjp(_s5_in_fn, uv, bre_ref[...], bim_ref[...])
        du_in, dbre, dbim = in_vjp((g_re, g_im))
        du_ref[...] = du_out + du_in + res_ref[...]
        dare = jnp.sum(g_re * p_re + g_im * p_im, axis=0, keepdims=True)
        daim = jnp.sum(g_im * p_re - g_re * p_im, axis=0, keepdims=True)
        for r, val in zip(sums, (dbre, dbim, dcre, dcim, dd, dare, daim)):
            r[...] += val

    u_blk = _spec((tt, SSM_BLK_IN), lambda l, t: (n_t - 1 - t, l))
    s_blk = _spec((tt, SSM_BLK_ST), lambda l, t: (n_t - 1 - t, l))
    halo = _spec((8, SSM_BLK_ST), lambda l, t: (jnp.maximum((n_t - 1 - t) * (tt // 8) - 1, 0), l))
    blk3 = lambda arr: _spec((1,) + arr.shape[1:], lambda l, t: (l, 0, 0))
    par = lambda width: _spec((1, width), lambda l, t: (0, l))
    params = [blk3(b_re), blk3(b_im), par(SSM_BLK_ST), par(SSM_BLK_ST), blk3(c_re), blk3(c_im), par(SSM_BLK_IN)]
    shape = lambda arr: jax.ShapeDtypeStruct(arr.shape, F32)
    return pl.pallas_call(
        body, name="s5_bwd", grid=(N_SSM_BLK, n_t),
        in_specs=[u_blk, u_blk, s_blk, s_blk, halo, halo, u_blk] + params,
        out_specs=[u_blk, blk3(b_re), blk3(b_im), blk3(c_re), blk3(c_im), par(SSM_BLK_IN), par(SSM_BLK_ST), par(SSM_BLK_ST)],
        out_shape=[shape(u), shape(b_re), shape(b_im), shape(c_re), shape(c_im), shape(d_row), shape(a_re), shape(a_im)],
        scratch_shapes=[pltpu.VMEM((2, SSM_BLK_ST), F32), pltpu.VMEM((tt, SSM_BLK_ST), F32), pltpu.VMEM((tt, SSM_BLK_ST), F32)],
        compiler_params=_cparams(2),
    )(dyg, u, s_re, s_im, s_re, s_im, dh_res, b_re, b_im, a_re, a_im, c_re, c_im, d_row)


def _loss_call(h, target, tm):
    n_tok, d = h.shape

    def body(h_ref, t_ref, acc_ref, dh_ref):
        @pl.when(pl.program_id(0) == 0)
        def _():
            acc_ref[...] = jnp.zeros(acc_ref.shape, F32)

        e = h_ref[...] - t_ref[...]
        dh_ref[...] = e * (1.0 / d)
        acc_ref[...] += jnp.sum(jnp.sum(e * e, axis=1, keepdims=True), axis=0, keepdims=True)

    tok = _spec((tm, d), lambda i: (i, 0))
    return pl.pallas_call(
        body, name="loss", grid=(n_tok // tm,),
        in_specs=[tok, tok], out_specs=[_spec((8, 128), lambda i: (0, 0)), tok],
        out_shape=[jax.ShapeDtypeStruct((8, 128), F32), jax.ShapeDtypeStruct(h.shape, F32)],
        compiler_params=_cparams(1),
    )(h, target)


def _block_diag(t):
    nb, ng, rows, cols = t.shape
    eye = jnp.eye(ng, dtype=t.dtype)
    return jnp.einsum('bgrc,gh->bgrhc', t, eye).reshape(nb, ng * rows, ng * cols)


def _block_diag_t(t, rows, cols):
    nb, ng = t.shape[0], t.shape[1] // rows
    t5 = t.reshape(nb, ng, rows, ng, cols)
    return jnp.einsum('bgrhc,gh->bgrc', t5, jnp.eye(ng, dtype=t.dtype))


def _mlp_fwd(layer, h, w1g, w2g):
    pre = _matmul(f"mlp{layer}_up", h, w1g, 'nn', b_cols=(layer,), out_dtype=BF16)
    return pre, _matmul(f"mlp{layer}_down", pre, w2g, 'nn', lhs_fn=_relu2, b_rows=(layer,))


def _mlp_bwd(layer, h, pre, dy, dh_res, w1g, w2g):
    dpre = _matmul(f"mlp{layer}_down_dx", dy, w2g, 'nt', epi=lambda acc, p: acc * (2.0 * jnp.maximum(p, 0.0)),
                   extra=pre, b_rows=(layer,), out_dtype=BF16)
    dw2 = _matmul(f"mlp{layer}_down_dw", pre, dy, 'tn', lhs_fn=_relu2, out_dtype=GRAD_DTYPE)
    dw1 = _matmul(f"mlp{layer}_up_dw", h, dpre, 'tn', out_cols=True, out_dtype=GRAD_DTYPE)
    dh = _matmul(f"mlp{layer}_up_dx", dpre, w1g, 'nt', epi=lambda acc, e: acc + e, extra=dh_res, b_cols=(layer,))
    return dh, dw1, dw2


def _local_step(x, target, w, wg, late_weights, exchange):
    n_tok = x.shape[0]
    tm = min(TOKEN_TILE, n_tok)
    n_tile, n_chunk = n_tok // tm, n_tok // CHUNK
    row = lambda v: v.reshape(1, -1)
    big = {n: wg[n].reshape(D_MODEL, D_MODEL) for n in ('rw_wr', 'rw_wk', 'rw_wv', 'rw_wo')}
    ln_g, ln_b = w['ln_g'], w['ln_b']
    grads = {}
    to_slots = lambda t: t.reshape(N_DEV, t.shape[0] // N_DEV, t.shape[1])

    halo_spec = (8, D_MODEL), lambda i: (jnp.maximum(i * (tm // 8) - 1, 0), 0)
    mix_p = [_tile(x, tm), (x,) + halo_spec, _full(w['rw_mu'][0]), _full(row(w['rw_w0'])), _full(w['rw_w1'][0]),
             _full(w['rw_w2'][0]), _full(row(w['rw_a0'])), _full(w['rw_a1'][0]), _full(w['rw_a2'][0]),
             _full(w['rw_g1'][0]), _full(w['rw_g2'][0])]
    tok_out = lambda dt: ((n_tok, D_MODEL), dt, (tm, D_MODEL), lambda i: (i, 0))
    xr, xk, xv, lw, a, g = _fwd_call("rwkv_mix", _mix_fn, (n_tile,), mix_p,
                                     [tok_out(BF16), tok_out(BF16), tok_out(BF16), tok_out(F32), tok_out(F32), tok_out(F32)])
    r = _matmul("rwkv_r", xr, big['rw_wr'], 'nn')
    k = _matmul("rwkv_k", xk, big['rw_wk'], 'nn')
    v = _matmul("rwkv_v", xv, big['rw_wv'], 'nn')

    scan_in = (r, lw, k, v, a, row(w['rw_k_k']), row(w['rw_k_a']), w['rw_r_k'].reshape(1, -1))
    o, bonus, *scan_saved = _rwkv_scan_fwd(*scan_in)

    gate_p = [_tile(o, tm), _tile(bonus, tm), _tile(g, tm), _full(row(w['rw_lnx_g'])), _full(row(w['rw_lnx_b']))]
    (og,) = _fwd_call("rwkv_gate", _gate_fn, (n_tile,), gate_p, [tok_out(BF16)])
    y0 = _matmul("rwkv_o", og, big['rw_wo'], 'nn')
    tl = min(LN_TILE, n_tok)
    n_ln = n_tok // tl
    ln_out = lambda dt: ((n_tok, D_MODEL), dt, (tl, D_MODEL), lambda i: (i, 0))
    ln0_p = [_tile(x, tl), _tile(y0, tl), _full(ln_g[0:1]), _full(ln_b[0:1])]
    twice = lambda fn: lambda *args: fn(*args) * 2
    h1, h1_b = _fwd_call("ln0", twice(_res_ln_fn), (n_ln,), ln0_p, [ln_out(F32), ln_out(BF16)])
    late = late_weights(h1)
    w1g, w2g, glu_g = late['mlp_w1'], late['mlp_w2'], late['s5_w_glu']
    pre0, y1 = _mlp_fwd(0, h1_b, w1g, w2g)
    ln1_p = [_tile(h1, tl), _tile(y1, tl), _full(ln_g[1:2]), _full(ln_b[1:2])]
    (h2,) = _fwd_call("ln1", _res_ln_fn, (n_ln,), ln1_p, [ln_out(F32)])

    gp = (SSM_GROUPS, SSM_STATE)
    par_p = [_full(w['s5_a_re'][0]), _full(w['s5_a_im'][0]), _full(w['s5_log_dt'].reshape(SSM_GROUPS, 1))]
    s5_par_fn = lambda ar, ai, ld: _s5_param_fn(ar, ai, ld, None, None)
    gp_out = (gp, F32, gp, lambda i: (0, 0))
    abar_re, abar_im, coef_re, coef_im = _fwd_call("s5_param", s5_par_fn, (1,), par_p, [gp_out] * 4)
    b_flat = [w[n][0].reshape(SSM_LANES, SSM_GROUP) for n in ('s5_b_re', 's5_b_im')]
    bbar_p = [_full(coef_re.reshape(SSM_LANES, 1)), _full(coef_im.reshape(SSM_LANES, 1)), _full(b_flat[0]), _full(b_flat[1])]
    bb_out = ((SSM_LANES, SSM_GROUP), F32, (SSM_LANES, SSM_GROUP), lambda i: (0, 0))
    bbar_re, bbar_im = _fwd_call("s5_bbar", _s5_bbar_fn, (1,), bbar_p, [bb_out] * 2)
    to_in = lambda t: _block_diag(t.reshape(N_SSM_BLK, 8, SSM_STATE, SSM_GROUP).transpose(0, 1, 3, 2))
    to_out = lambda t: _block_diag(t.reshape(N_SSM_BLK, 8, SSM_GROUP, SSM_STATE).transpose(0, 1, 3, 2))
    bblk_re, bblk_im = to_in(bbar_re), to_in(bbar_im)
    cblk_re, cblk_im = to_out(w['s5_c_re'][0]), to_out(w['s5_c_im'][0])
    a_row_re, a_row_im = abar_re.reshape(1, SSM_LANES), abar_im.reshape(1, SSM_LANES)

    d_row = row(w['s5_d'])
    s5_params = (bblk_re, bblk_im, a_row_re, a_row_im, cblk_re, cblk_im, d_row)
    s_re, s_im, yg = _s5_fwd(h2, *s5_params)
    z = _matmul("s5_glu", yg, glu_g, 'nn', b_cols=(0,))
    ln2_p = [_tile(h2, tl), _tile(z, tl), _full(ln_g[2:3]), _full(ln_b[2:3])]
    h3, h3_b = _fwd_call("ln2", twice(_glu_ln_fn), (n_ln,), ln2_p, [ln_out(F32), ln_out(BF16)])
    pre1, y3 = _mlp_fwd(1, h3_b, w1g, w2g)
    ln3_p = [_tile(h3, tl), _tile(y3, tl), _full(ln_g[3:4]), _full(ln_b[3:4])]
    (h4,) = _fwd_call("ln3", _res_ln_fn, (n_ln,), ln3_p, [ln_out(F32)])

    loss_acc, dh4 = _loss_call(h4, target, tl)

    ln_w = ['tile', 'tile_bf16', 'acc', 'acc']
    dh3_res, dy3, dg3, db3 = _bwd_call("ln3_bwd", _res_ln_fn, (n_ln,), ln3_p, [_tile(dh4, tl)], ln_w)
    dh3, dw1_1, dw2_1 = _mlp_bwd(1, h3_b, pre1, dy3, dh3_res, w1g, w2g)
    zero = exchange([('mlp_w1', 1, dw1_1), ('mlp_w2', 1, to_slots(dw2_1))])
    ln2_p[2] = _full(ln_g[2:3] + zero)
    dh2_res, dz, dg2, db2 = _bwd_call("ln2_bwd", _glu_ln_fn, (n_ln,), ln2_p, [_tile(dh3, tl)], ln_w)
    dyg = _matmul("s5_glu_dx", dz, glu_g, 'nt', b_cols=(0,))
    dw_glu = _matmul("s5_glu_dw", yg, dz, 'tn', out_cols=True, out_dtype=GRAD_DTYPE)
    dh2, dbb_re, dbb_im, dcb_re, dcb_im, dd, dabar_re, dabar_im = _s5_bwd(dyg, h2, s_re, s_im, dh2_res, *s5_params)
    from_in = lambda t: _block_diag_t(t, SSM_GROUP, SSM_STATE).transpose(0, 1, 3, 2).reshape(SSM_LANES, SSM_GROUP)
    from_out = lambda t: _block_diag_t(t, SSM_STATE, SSM_GROUP).transpose(0, 1, 3, 2).reshape(1, SSM_GROUPS, SSM_GROUP, SSM_STATE)
    grads['s5_c_re'], grads['s5_c_im'] = from_out(dcb_re), from_out(dcb_im)
    grads['s5_d'] = dd
    dcoef_re, dcoef_im, db_re, db_im = _bwd_call(
        "s5_bbar_bwd", _s5_bbar_fn, (1,), bbar_p, [_full(from_in(dbb_re)), _full(from_in(dbb_im))], ['acc'] * 4)
    grads['s5_b_re'] = db_re.reshape(w['s5_b_re'].shape)
    grads['s5_b_im'] = db_im.reshape(w['s5_b_im'].shape)
    par_ct = [_full(dabar_re.reshape(gp)), _full(dabar_im.reshape(gp)), _full(dcoef_re.reshape(gp)), _full(dcoef_im.reshape(gp))]
    da_re, da_im, dlog_dt = _bwd_call("s5_param_bwd", s5_par_fn, (1,), par_p, par_ct, ['acc'] * 3)
    grads['s5_a_re'], grads['s5_a_im'], grads['s5_log_dt'] = da_re[None], da_im[None], dlog_dt.reshape(1, SSM_GROUPS)
    zero = exchange([(n, 0, grads[n]) for n in ('s5_a_re', 's5_a_im', 's5_log_dt', 's5_b_re', 's5_b_im', 's5_c_re', 's5_c_im')])
    ln1_p[2] = _full(ln_g[1:2] + zero)

    dh1_res, dy1, dg1, db1 = _bwd_call("ln1_bwd", _res_ln_fn, (n_ln,), ln1_p, [_tile(dh2, tl)], ln_w)
    dh1, dw1_0, dw2_0 = _mlp_bwd(0, h1_b, pre0, dy1, dh1_res, w1g, w2g)
    dx_res, dy0, dg0, db0 = _bwd_call("ln0_bwd", _res_ln_fn, (n_ln,), ln0_p, [_tile(dh1, tl)], ln_w)
    grads['ln_g'] = jnp.concatenate([dg0, dg1, dg2, dg3])
    grads['ln_b'] = jnp.concatenate([db0, db1, db2, db3])
    dog = _matmul("rwkv_o_dx", dy0, big['rw_wo'], 'nt')
    dw_o = to_slots(_matmul("rwkv_o_dw", og, dy0, 'tn', out_dtype=GRAD_DTYPE))
    zero = exchange([('s5_w_glu', 0, dw_glu), ('mlp_w1', 0, dw1_0), ('mlp_w2', 0, to_slots(dw2_0)), ('rw_wo', 0, dw_o)])
    gate_p[3] = _full(row(w['rw_lnx_g']) + zero)
    d_o, dbonus, dgate, grads['rw_lnx_g'], grads['rw_lnx_b'] = _bwd_call(
        "rwkv_gate_bwd", _gate_fn, (n_tile,), gate_p, [_tile(dog, tm)], ['tile', 'tile', 'tile', 'acc', 'acc'])
    dr, dlw, dk, dv, da, grads['rw_k_k'], grads['rw_k_a'], dr_k = _rwkv_scan_bwd(*scan_in, *scan_saved, d_o, dbonus)
    grads['rw_r_k'] = dr_k.reshape(w['rw_r_k'].shape)
    dw_r = to_slots(_matmul("rwkv_r_dw", xr, dr, 'tn', out_dtype=GRAD_DTYPE))
    dw_k = to_slots(_matmul("rwkv_k_dw", xk, dk, 'tn', out_dtype=GRAD_DTYPE))
    dw_v = to_slots(_matmul("rwkv_v_dw", xv, dv, 'tn', out_dtype=GRAD_DTYPE))
    done = ('ln_g', 'ln_b', 'rw_lnx_g', 'rw_lnx_b', 'rw_k_k', 'rw_k_a', 'rw_r_k')
    zero = exchange([('rw_wr', 0, dw_r), ('rw_wk', 0, dw_k), ('rw_wv', 0, dw_v)] + [(n, 0, grads[n]) for n in done])
    mix_p[3] = _full(row(w['rw_w0']) + zero)
    dxr = _matmul("rwkv_r_dx", dr, big['rw_wr'], 'nt')
    dxk = _matmul("rwkv_k_dx", dk, big['rw_wk'], 'nt')
    dxv = _matmul("rwkv_v_dx", dv, big['rw_wv'], 'nt')
    mix_ct = [_tile(t, tm) for t in (dxr, dxk, dxv, dlw, da, dgate)]
    halo_grad = ('tile', ((n_tile * 8, D_MODEL), (8, D_MODEL), lambda i: (i, 0)))
    res = _bwd_call("rwkv_mix_bwd", _mix_fn, (n_tile,), mix_p, mix_ct, ['tile', halo_grad] + ['acc'] * 9,
                    addends={0: _tile(dx_res, tm)})
    corr = res[1].reshape(n_tile, 8, D_MODEL)[1:, 7:8, :]
    corr = jnp.pad(corr, ((0, 1), (tm - 1, 0), (0, 0)))
    dx = (res[0].reshape(n_tile, tm, D_MODEL) + corr).reshape(n_tok, D_MODEL)
    (grads['rw_mu'], grads['rw_w0'], grads['rw_w1'], grads['rw_w2'], grads['rw_a0'], grads['rw_a1'], grads['rw_a2'],
     grads['rw_g1'], grads['rw_g2']) = [t[None] if t.shape[0] != 1 else t for t in res[2:]]
    return loss_acc[0, 0], dx, grads


def _full_shape(local_shape, axis):
    return local_shape[:axis] + (N_DEV * local_shape[axis],) + local_shape[axis + 1:]


def _split_shape(local_shape, axis):
    return local_shape[:axis] + (N_DEV, local_shape[axis]) + local_shape[axis + 1:]


def _mesh_pos():
    return lax.axis_index("x"), lax.axis_index("y"), lax.axis_index("c")


def _flip(v, f):
    return 1 - v if f else v


def _hbm_call(name, body, arrays, out_shapes, n_sems):
    n_t = len(arrays)
    hbm = pl.BlockSpec(memory_space=pl.ANY)
    return pl.pallas_call(
        body, name=name, out_shape=out_shapes, in_specs=[hbm] * n_t, out_specs=[hbm] * n_t,
        scratch_shapes=[pltpu.SemaphoreType.DMA((n_t, n_sems)), pltpu.SemaphoreType.DMA((n_t, n_sems)),
                        pltpu.SemaphoreType.DMA((n_t,))],
    )(*arrays)


def _all_gather(name, locals_):
    n_t = len(locals_)

    def body(*refs):
        x_refs, out_refs = refs[:n_t], refs[n_t:2 * n_t]
        send_sems, recv_sems, local_sems = refs[2 * n_t:]
        x, y, c = _mesh_pos()
        me, sibling = (x, y, c), (x, y, 1 - c)
        chips = [(1 - x, y), (x, 1 - y), (1 - x, 1 - y)]

        def copy(t, k, block, to, own=False):
            slot = out_refs[t].at[4 * block[0] + 2 * block[1] + block[2]]
            return pltpu.make_async_remote_copy(
                src_ref=x_refs[t] if own else slot, dst_ref=slot,
                send_sem=send_sems.at[t, k], recv_sem=recv_sems.at[t, k],
                device_id=to, device_id_type=pl.DeviceIdType.MESH)

        mine = [pltpu.make_async_copy(x_refs[t], out_refs[t].at[4 * x + 2 * y + c], local_sems.at[t]) for t in range(n_t)]
        for cp in mine:
            cp.start()
        sent = []
        for t in range(n_t):
            sent.append(copy(t, 0, me, sibling, own=True))
            sent += [copy(t, 1 + j, me, (*chip, c), own=True) for j, chip in enumerate(chips)]
        for cp in sent:
            cp.start()
        for j, chip in enumerate(chips):
            for t in range(n_t):
                copy(t, 1 + j, (*chip, c), me).wait_recv()
                passed = copy(t, 4 + j, (*chip, c), sibling)
                passed.start()
                sent.append(passed)
        for t in range(n_t):
            copy(t, 0, sibling, me).wait_recv()
            for j, chip in enumerate(chips):
                copy(t, 4 + j, (*chip, 1 - c), me).wait_recv()
        for cp in sent:
            cp.wait_send()
        for cp in mine:
            cp.wait()

    outs = [jax.ShapeDtypeStruct((N_DEV,) + a.shape, a.dtype) for a in locals_]
    return _hbm_call(name, body, locals_, outs, 7)


def _all_to_all(name, blocks):
    n_t = len(blocks)

    def body(*refs):
        x_refs, out_refs = refs[:n_t], refs[n_t:2 * n_t]
        send_sems, recv_sems, local_sems = refs[2 * n_t:]
        x, y, c = _mesh_pos()
        my_slot = 4 * x + 2 * y + c
        mine = [pltpu.make_async_copy(x_refs[t].at[my_slot], out_refs[t].at[my_slot], local_sems.at[t]) for t in range(n_t)]
        for cp in mine:
            cp.start()
        copies = []
        for k in range(1, N_DEV):
            peer = (_flip(x, k & 4), _flip(y, k & 2), _flip(c, k & 1))
            peer_slot = 4 * peer[0] + 2 * peer[1] + peer[2]
            for t in range(n_t):
                sems = dict(send_sem=send_sems.at[t, k - 1], recv_sem=recv_sems.at[t, k - 1],
                            device_id=peer, device_id_type=pl.DeviceIdType.MESH)
                cp = pltpu.make_async_remote_copy(src_ref=x_refs[t].at[peer_slot], dst_ref=out_refs[t].at[my_slot], **sems)
                cp.start()
                landed = pltpu.make_async_remote_copy(src_ref=x_refs[t].at[my_slot], dst_ref=out_refs[t].at[peer_slot], **sems)
                copies.append((cp, landed))
        for sent, landed in copies:
            landed.wait_recv()
            sent.wait_send()
        for cp in mine:
            cp.wait()

    outs = [jax.ShapeDtypeStruct(a.shape, a.dtype) for a in blocks]
    return _hbm_call(name, body, blocks, outs, 7)


_HBM = pl.BlockSpec(memory_space=pltpu.HBM)
_SEM = pl.BlockSpec(memory_space=pltpu.SEMAPHORE)
_EFFECT = pltpu.SideEffectType.DATAFLOW_SIDE_EFFECTING


def _peer_copies(x_refs, land_refs, send_sems, recv_sems, gather, mine):
    x, y, c = _mesh_pos()
    my_slot = 4 * x + 2 * y + c
    copies = []
    for k in range(1, N_DEV):
        peer = (_flip(x, k & 4), _flip(y, k & 2), _flip(c, k & 1))
        peer_slot = 4 * peer[0] + 2 * peer[1] + peer[2]
        for t, (x_ref, land) in enumerate(zip(x_refs, land_refs)):
            sem = t * (N_DEV - 1) + k - 1
            copies.append(pltpu.make_async_remote_copy(
                src_ref=x_ref if gather[t] else x_ref.at[peer_slot], dst_ref=land.at[my_slot if mine else peer_slot],
                send_sem=send_sems.at[sem], recv_sem=recv_sems.at[sem], device_id=peer, device_id_type=pl.DeviceIdType.MESH))
    return copies


def _push_start(name, arrays, gather):
    n_t = len(arrays)
    lands = [lax.empty((N_DEV,) + a.shape if whole else a.shape, a.dtype) for a, whole in zip(arrays, gather)]

    def body(*refs):
        x_refs, land_refs = refs[:n_t], refs[n_t:2 * n_t]
        send_sems, recv_sems = refs[2 * n_t], refs[2 * n_t + 1]
        for sent in _peer_copies(x_refs, land_refs, send_sems, recv_sems, gather, True):
            sent.start()
        refs[-1][...] = jnp.zeros((8, 128), F32)

    held = list(arrays) + lands
    sems = pltpu.SemaphoreType.DMA((n_t * (N_DEV - 1),))
    out = pl.pallas_call(
        body, name=name,
        out_shape=(sems, sems, *[pltpu.HBM(a.shape, a.dtype) for a in held], jax.ShapeDtypeStruct((8, 128), F32)),
        in_specs=[_HBM] * (2 * n_t),
        out_specs=(_SEM, _SEM, *[_HBM] * (2 * n_t), pl.BlockSpec(memory_space=pltpu.VMEM)),
        input_output_aliases={i: 2 + i for i in range(2 * n_t)},
        compiler_params=pltpu.CompilerParams(has_side_effects=_EFFECT),
    )(*[pltpu.with_memory_space_constraint(a, pltpu.HBM) for a in held])
    return dict(send=out[0], recv=out[1], held=out[2:2 + 2 * n_t], zero=out[-1][0, 0], gather=gather)


def _push_wait(name, started, after):
    held, gather = started['held'], started['gather']
    n_t = len(held) // 2

    def body(*refs):
        x_refs, land_refs = refs[:n_t], refs[n_t:2 * n_t]
        send_sems, recv_sems = refs[2 * n_t], refs[2 * n_t + 1]
        for sent in _peer_copies(x_refs, land_refs, send_sems, recv_sems, gather, True):
            sent.wait_send()
        for landed in _peer_copies(x_refs, land_refs, send_sems, recv_sems, gather, False):
            landed.wait_recv()

    out = pl.pallas_call(
        body, name=name,
        out_shape=[pltpu.HBM(a.shape, a.dtype) for a in held],
        in_specs=[_HBM] * (2 * n_t) + [_SEM, _SEM, pl.BlockSpec(memory_space=pl.ANY)],
        out_specs=[_HBM] * (2 * n_t),
        input_output_aliases={i: i for i in range(2 * n_t)},
        compiler_params=pltpu.CompilerParams(has_side_effects=_EFFECT),
    )(*held, started['send'], started['recv'], after)
    return out[:n_t], out[n_t:]


ADAM_BLOCK = 128 * 1024


def _adam_update(part, w_ref, m_ref, v_ref, g_ref, d_ref, nm_ref, nv_ref):
    g = part(0)
    for i in range(1, N_DEV):
        g = g + part(i)
    m_new = ADAM_B1 * m_ref[...] + (1.0 - ADAM_B1) * g
    v_new = ADAM_B2 * v_ref[...] + (1.0 - ADAM_B2) * (g * g)
    m_hat = m_new * (1.0 / (1.0 - ADAM_B1 ** ADAM_STEP))
    v_hat = v_new * (1.0 / (1.0 - ADAM_B2 ** ADAM_STEP))
    g_ref[...] = g
    nm_ref[...] = m_new
    nv_ref[...] = v_new
    d_ref[...] = -ADAM_LR * (m_hat / (jnp.sqrt(v_hat) + ADAM_EPS) + ADAM_WD * w_ref[...])


def _adam_many(name, items):
    n_t = len(items)
    n_in = [4 if own is None else 5 for *_, own in items]
    first = [sum(n_in[:t]) for t in range(n_t)]

    def body(*refs):
        ins, outs = refs[:sum(n_in)], refs[sum(n_in):]
        x, y, c = _mesh_pos()
        my_slot = 4 * x + 2 * y + c
        for t in range(n_t):
            s_ref, w_ref, m_ref, v_ref = ins[first[t]:first[t] + 4]
            if n_in[t] == 4:
                part = lambda i: s_ref[i]
            else:
                own_ref = ins[first[t] + 4]
                part = lambda i: jnp.where(my_slot == i, own_ref[...], s_ref[i])
            _adam_update(part, w_ref, m_ref, v_ref, *outs[4 * t:4 * t + 4])

    vmem = pl.BlockSpec(memory_space=pltpu.VMEM)
    out = pl.pallas_call(
        body, name=name, in_specs=[vmem] * sum(n_in), out_specs=[vmem] * (4 * n_t),
        out_shape=[jax.ShapeDtypeStruct(item[1].shape, F32) for item in items for _ in range(4)],
        compiler_params=pltpu.CompilerParams(vmem_limit_bytes=VMEM_LIMIT),
    )(*[a for item in items for a in item if a is not None])
    return [out[4 * t:4 * t + 4] for t in range(n_t)]


def _adam_call(name, slots, w, m, v, own=None):
    n_layer, rows, width = w.shape
    tile = min(rows, ADAM_BLOCK // max(width, 128))

    def body(*refs):
        s_refs = refs[:n_layer]
        own_refs = refs[n_layer:2 * n_layer] if own is not None else None
        w_ref, m_ref, v_ref = refs[-7:-4]
        layer = pl.program_id(0)
        x, y, c = _mesh_pos()
        my_slot = 4 * x + 2 * y + c
        for l in range(n_layer):
            @pl.when(layer == l)
            def _(l=l):
                if own is None:
                    part = lambda i: s_refs[l][i].astype(F32)
                else:
                    part = lambda i: jnp.where(my_slot == i, own_refs[l][...], s_refs[l][i]).astype(F32)
                _adam_update(part, w_ref, m_ref, v_ref, *refs[-4:])

    slot_spec = lambda k: _spec((N_DEV, tile, width), lambda l, i: (0, jnp.where(l == k, i, 0), 0))
    own_spec = lambda k: _spec((tile, width), lambda l, i: (jnp.where(l == k, i, 0), 0))
    blk = _spec((None, tile, width), lambda l, i: (l, i, 0))
    specs = [slot_spec(k) for k in range(n_layer)] + ([own_spec(k) for k in range(n_layer)] if own is not None else [])
    return pl.pallas_call(
        body, name=name, grid=(n_layer, rows // tile),
        in_specs=specs + [blk] * 3, out_specs=[blk] * 4, out_shape=[jax.ShapeDtypeStruct(w.shape, F32)] * 4,
        compiler_params=_cparams(2),
    )(*slots, *(own or []), w, m, v)


def kernel(x, ln_g, ln_b, rw_mu, rw_w0, rw_w1, rw_w2, rw_a0, rw_a1, rw_a2, rw_g1, rw_g2, rw_k_k, rw_k_a, rw_r_k, rw_wr, rw_wk, rw_wv, rw_wo, rw_lnx_g, rw_lnx_b, s5_a_re, s5_a_im, s5_log_dt, s5_b_re, s5_b_im, s5_c_re, s5_c_im, s5_d, s5_w_glu, mlp_w1, mlp_w2, loss_target, m_ln_g, m_ln_b, m_rw_mu, m_rw_w0, m_rw_w1, m_rw_w2, m_rw_a0, m_rw_a1, m_rw_a2, m_rw_g1, m_rw_g2, m_rw_k_k, m_rw_k_a, m_rw_r_k, m_rw_wr, m_rw_wk, m_rw_wv, m_rw_wo, m_rw_lnx_g, m_rw_lnx_b, m_s5_a_re, m_s5_a_im, m_s5_log_dt, m_s5_b_re, m_s5_b_im, m_s5_c_re, m_s5_c_im, m_s5_d, m_s5_w_glu, m_mlp_w1, m_mlp_w2, v_ln_g, v_ln_b, v_rw_mu, v_rw_w0, v_rw_w1, v_rw_w2, v_rw_a0, v_rw_a1, v_rw_a2, v_rw_g1, v_rw_g2, v_rw_k_k, v_rw_k_a, v_rw_r_k, v_rw_wr, v_rw_wk, v_rw_wv, v_rw_wo, v_rw_lnx_g, v_rw_lnx_b, v_s5_a_re, v_s5_a_im, v_s5_log_dt, v_s5_b_re, v_s5_b_im, v_s5_c_re, v_s5_c_im, v_s5_d, v_s5_w_glu, v_mlp_w1, v_mlp_w2):
    given = dict(locals())
    local_w = {n: given[n] for n in WEIGHTS}
    local_m = {n: given["m_" + n] for n in WEIGHTS}
    local_v = {n: given["v_" + n] for n in WEIGHTS}
    small = [n for n in WEIGHTS if n in SHARDED and n not in BIG]

    my_slot = 4 * lax.axis_index("x") + 2 * lax.axis_index("y") + lax.axis_index("c")
    as_bf16 = {n: local_w[n].astype(BF16) for n in BIG}

    late_started = _push_start("weights_late_start", [as_bf16[n] for n in LATE], [True] * len(LATE))
    early = [n for n in BIG if n not in LATE]
    gathered = _all_gather("weights_all_gather", [as_bf16[n] for n in early] + [local_w[n] for n in small])
    early_gathered = dict(zip(early, gathered))
    full = dict(local_w)
    for n, blk in zip(small, gathered[len(early):]):
        full[n] = jnp.moveaxis(blk, 0, SHARDED[n]).reshape(_full_shape(local_w[n].shape, SHARDED[n]))
    full['rw_w0'] = full['rw_w0'] + late_started['zero']

    def late_weights(after):
        _, lands = _push_wait("weights_late_wait", late_started, after)
        own = lambda n: as_bf16[n][None]
        return {n: lax.dynamic_update_slice(land, own(n), (my_slot,) + (0,) * as_bf16[n].ndim) for n, land in zip(LATE, lands)}

    exchanges = []

    def exchange(entries):
        started = _push_start(f"grads_start_{len(exchanges)}", [t for _, _, t in entries], [n in REPLICATED for n, _, _ in entries])
        exchanges.append((entries, started))
        return started['zero']

    loss_sq, dx, grads = _local_step(x[0], loss_target[0], full, early_gathered, late_weights, exchange)
    loss = (0.5 / D_MODEL) * lax.psum(loss_sq, MESH_AXES)

    wide = {'s5_b_re': (SSM_LANES, SSM_GROUP), 's5_b_im': (SSM_LANES, SSM_GROUP),
            's5_c_re': (D_MODEL, SSM_STATE), 's5_c_im': (D_MODEL, SSM_STATE)}
    by_name, many = {}, []
    landed = {n: {} for n in BIG}
    for e, (entries, started) in enumerate(exchanges):
        sources, lands = _push_wait(f"grads_wait_{e}", started, dx)
        for (n, i, _), src, slots in zip(entries, sources, lands):
            if n in wide:
                as_view = lambda t: t.reshape((1,) + wide[n])
                out = _adam_call(f"adamw_{n}", [slots.reshape((N_DEV,) + wide[n])], as_view(local_w[n]), as_view(local_m[n]),
                                 as_view(local_v[n]), own=[src.reshape(wide[n])])
                by_name[n] = [t.reshape(local_w[n].shape) for t in out]
            elif n in REPLICATED:
                many.append((n, (slots, local_w[n], local_m[n], local_v[n], src)))
            else:
                landed[n][i] = (slots, lax.dynamic_index_in_dim(src, my_slot, 0, keepdims=False))
    for n in BIG:
        layers = [landed[n][i] for i in sorted(landed[n])]
        by_name[n] = _adam_call(f"adamw_{n}", [s for s, _ in layers], local_w[n], local_m[n], local_v[n], own=[o for _, o in layers])
    split = lambda n: jnp.moveaxis(grads[n].reshape(_split_shape(local_w[n].shape, SHARDED[n])), SHARDED[n], 0)
    for n, slots in zip(small, _all_to_all("grads_all_to_all", [split(n) for n in small])):
        many.append((n, (slots, local_w[n], local_m[n], local_v[n], None)))
    last = [n for n in REPLICATED if n not in by_name and n not in dict(many)]
    for n, slots in zip(last, _all_gather("grads_all_gather", [grads[n] for n in last])):
        many.append((n, (slots, local_w[n], local_m[n], local_v[n], None)))
    for (n, _), out in zip(many, _adam_many("adamw_small", [item for _, item in many])):
        by_name[n] = out
    results = [by_name[n][k] for k in range(4) for n in WEIGHTS]
    return (loss, dx[None], *results)
```

```python
import functools
import math

import jax
import jax.numpy as jnp
from jax import lax
from jax.experimental import pallas as pl
from jax.experimental.pallas import tpu as pltpu

F32 = jnp.float32
BF16 = jnp.bfloat16
HI = lax.Precision.HIGHEST

D_MODEL = 1024
HEAD = 64
PAIR = 2 * HEAD
N_PAIR = D_MODEL // PAIR
CHUNK = 64
GN_EPS = 64e-5
LN_EPS = 1e-5
SSM_GROUP = 16
SSM_STATE = 64
SSM_GROUPS = D_MODEL // SSM_GROUP
SSM_LANES = SSM_GROUPS * SSM_STATE
SSM_BLK_IN = 128
SSM_BLK_ST = 512
N_SSM_BLK = D_MODEL // SSM_BLK_IN
DEPTH = 2
DN_ALPHA = (2.0 * DEPTH) ** 0.25
ADAM_LR, ADAM_B1, ADAM_B2, ADAM_EPS, ADAM_WD, ADAM_STEP = 0.001, 0.9, 0.999, 1e-08, 0.01, 10
N_DEV = 8
MESH_AXES = ("x", "y", "c")
VMEM_LIMIT = 56 * 1024 * 1024
TOKEN_TILE = 256
LN_TILE = 512

SHARDED = {
    "rw_mu": 2, "rw_w1": 1, "rw_w2": 2, "rw_a1": 1, "rw_a2": 2, "rw_g1": 1, "rw_g2": 2,
    "rw_wr": 1, "rw_wk": 1, "rw_wv": 1, "rw_wo": 1, "s5_d": 1, "s5_w_glu": 2, "mlp_w1": 2, "mlp_w2": 1,
}
WEIGHTS = ['ln_g', 'ln_b', 'rw_mu', 'rw_w0', 'rw_w1', 'rw_w2', 'rw_a0', 'rw_a1', 'rw_a2', 'rw_g1', 'rw_g2', 'rw_k_k',
           'rw_k_a', 'rw_r_k', 'rw_wr', 'rw_wk', 'rw_wv', 'rw_wo', 'rw_lnx_g', 'rw_lnx_b', 's5_a_re', 's5_a_im',
           's5_log_dt', 's5_b_re', 's5_b_im', 's5_c_re', 's5_c_im', 's5_d', 's5_w_glu', 'mlp_w1', 'mlp_w2']
REPLICATED = [n for n in WEIGHTS if n not in SHARDED]
BIG = ['rw_wr', 'rw_wk', 'rw_wv', 'rw_wo', 's5_w_glu', 'mlp_w1', 'mlp_w2']
LATE = ['s5_w_glu', 'mlp_w1', 'mlp_w2']
GRAD_DTYPE = BF16


def _cparams(n_grid):
    return pltpu.CompilerParams(dimension_semantics=("arbitrary",) * n_grid, vmem_limit_bytes=VMEM_LIMIT)


@jax.custom_vjp
def _mm(x, w):
    return jnp.dot(x.astype(BF16), w.astype(BF16), preferred_element_type=F32)


def _mm_fwd(x, w):
    return _mm(x, w), (x, w)


def _mm_bwd(res, dy):
    x, w = res
    dyb = dy.astype(BF16)
    dx = lax.dot_general(dyb, w.astype(BF16), (((1,), (1,)), ((), ())), preferred_element_type=F32)
    dw = lax.dot_general(x.astype(BF16), dyb, (((0,), (0,)), ((), ())), preferred_element_type=F32)
    return dx, dw


_mm.defvjp(_mm_fwd, _mm_bwd)


_DOT_DIMS = {'nn': (((1,), (0,)), ((), ())), 'nt': (((1,), (1,)), ((), ())), 'tn': (((0,), (0,)), ((), ()))}
_BATCH_DOT_DIMS = {'nn': (((2,), (1,)), ((0,), (0,))), 'nt': (((2,), (2,)), ((0,), (0,))), 'tn': (((1,), (1,)), ((0,), (0,)))}
CHUNK_PAIRS = 8
CHUNK_PASSES = 1


def _split_bf16(a):
    hi = a.astype(BF16)
    return hi, (a - hi.astype(F32)).astype(BF16)


def _dot_passes(a, b, mode, passes=None):
    dims = (_DOT_DIMS if a.ndim == 2 else _BATCH_DOT_DIMS)[mode]
    dot = lambda p, q: lax.dot_general(p, q, dims, preferred_element_type=F32)
    if (CHUNK_PASSES if passes is None else passes) == 1:
        return dot(a.astype(BF16), b.astype(BF16))
    (ah, al), (bh, bl) = _split_bf16(a), _split_bf16(b)
    return dot(ah, bh) + (dot(ah, bl) + dot(al, bh))


@functools.partial(jax.custom_vjp, nondiff_argnums=(2,))
def _pdot(a, b, mode):
    return _dot_passes(a, b, mode)


def _pdot_fwd(a, b, mode):
    return _dot_passes(a, b, mode), (a, b)


def _pdot_bwd(mode, res, dy):
    a, b = res
    if mode == 'nn':
        return _dot_passes(dy, b, 'nt'), _dot_passes(a, dy, 'tn')
    if mode == 'nt':
        return _dot_passes(dy, b, 'nn'), _dot_passes(dy, a, 'tn')
    return _dot_passes(b, dy, 'nt'), _dot_passes(a, dy, 'nn')


_pdot.defvjp(_pdot_fwd, _pdot_bwd)


def _tri_sum(x, upper):
    nb, n = x.shape[0], x.shape[1]
    ti = lax.broadcasted_iota(jnp.int32, (nb, n, n), 1)
    tj = lax.broadcasted_iota(jnp.int32, (nb, n, n), 2)
    tri = ((ti <= tj) if upper else (ti >= tj)).astype(BF16)
    hi = x.astype(BF16)
    rest = x - hi.astype(F32)
    mid = rest.astype(BF16)
    lo = (rest - mid.astype(F32)).astype(BF16)
    dot = lambda q: lax.dot_general(tri, q, _BATCH_DOT_DIMS['nn'], preferred_element_type=F32)
    return dot(hi) + (dot(mid) + dot(lo))


@jax.custom_vjp
def _cumsum_rows(x):
    return _tri_sum(x, False)


_cumsum_rows.defvjp(lambda x: (_tri_sum(x, False), None), lambda _, dy: (_tri_sum(dy, True),))


def _power_sum(a, order):
    m = a.shape[-1]
    ti = lax.broadcasted_iota(jnp.int32, (1, m, m), 1)
    tj = lax.broadcasted_iota(jnp.int32, (1, m, m), 2)
    pw, total = _dot_passes(a, a, 'nn'), (ti == tj).astype(F32) + a
    for _ in range(int(math.log2(order)) - 2):
        both = _dot_passes(pw, jnp.concatenate([pw, total], axis=2), 'nn')
        pw, total = both[:, :, :m], total + both[:, :, m:]
    return total + _dot_passes(pw, total, 'nn')


@functools.partial(jax.custom_vjp, nondiff_argnums=(1,))
def _nilpotent_inverse(a, order):
    return _power_sum(a, order)


def _nilpotent_inverse_fwd(a, order):
    inv = _power_sum(a, order)
    return inv, inv


def _nilpotent_inverse_bwd(order, inv, g):
    return (_dot_passes(_dot_passes(inv, g, 'tn'), inv, 'nt'),)


_nilpotent_inverse.defvjp(_nilpotent_inverse_fwd, _nilpotent_inverse_bwd)


@jax.custom_vjp
def _known_inverse(a, inv):
    return inv


_known_inverse.defvjp(lambda a, inv: (inv, inv),
                      lambda inv, g: (_dot_passes(_dot_passes(inv, g, 'tn'), inv, 'nt'), jnp.zeros_like(inv)))


@functools.partial(jax.custom_vjp, nondiff_argnums=(1,))
def _roll_rows(x, shift):
    return pltpu.roll(x, shift, 0)


def _roll_rows_fwd(x, shift):
    return pltpu.roll(x, shift, 0), None


def _roll_rows_bwd(shift, _, dy):
    n = dy.shape[0]
    return (pltpu.roll(dy, (n - shift) % n, 0),)


_roll_rows.defvjp(_roll_rows_fwd, _roll_rows_bwd)


def _shift_down(x, first_row):
    row = lax.broadcasted_iota(jnp.int32, x.shape, 0)
    return jnp.where(row == 0, first_row, _roll_rows(x, 1))


def _sigmoid(x):
    return 1.0 / (1.0 + jnp.exp(-x))


def _softplus(x):
    return jnp.maximum(x, 0.0) + jnp.log(1.0 + jnp.exp(-jnp.abs(x)))


def _gelu(x):
    return 0.5 * x * (1.0 + jnp.tanh(math.sqrt(2.0 / math.pi) * (x + 0.044715 * (x * x * x))))


def _layer_norm(x, g, b):
    mu = jnp.mean(x, axis=-1, keepdims=True)
    xc = x - mu
    var = jnp.mean(xc * xc, axis=-1, keepdims=True)
    return xc * lax.rsqrt(var + LN_EPS) * g + b


def _spec(block, index_map):
    return pl.BlockSpec(block, index_map)


def _tile(arr, tm):
    return (arr, (tm, arr.shape[1]), lambda i: (i, 0))


def _full(arr):
    nd = arr.ndim
    return (arr, arr.shape, lambda *_: (0,) * nd)


def _fwd_call(name, fn, grid, prims, outs):
    n_in = len(prims)

    def body(*refs):
        vals = [r[...] for r in refs[:n_in]]
        vals = [v.astype(F32) if v.dtype != F32 else v for v in vals]
        res = fn(*vals)
        for o, r in zip(refs[n_in:], res):
            o[...] = r.astype(o.dtype)

    return pl.pallas_call(
        body, name=name, grid=grid,
        in_specs=[_spec(b, m) for _, b, m in prims],
        out_specs=[_spec(b, m) for _, _, b, m in outs],
        out_shape=[jax.ShapeDtypeStruct(s, d) for s, d, _, _ in outs],
        compiler_params=_cparams(len(grid)),
    )(*[a for a, _, _ in prims])


def _bwd_call(name, fn, grid, prims, cts, wants, addends=None):
    addends = addends or {}
    n_in, n_ct = len(prims), len(cts)
    out_place = {}
    wants, out_dtype = list(wants), {}
    for i, w in enumerate(wants):
        if isinstance(w, tuple):
            wants[i], out_place[i] = w
        elif w == 'tile_bf16':
            wants[i], out_dtype[i] = 'tile', BF16
    diff = [i for i, w in enumerate(wants) if w]
    add_idx = [i for i in diff if i in addends]
    last_axis = len(grid) - 1

    def body(*refs):
        vals = [r[...] for r in refs[:n_in]]
        vals = [v.astype(F32) if v.dtype != F32 else v for v in vals]
        ct_vals = [r[...] for r in refs[n_in:n_in + n_ct]]
        ct_vals = [v.astype(F32) if v.dtype != F32 else v for v in ct_vals]
        add_refs = dict(zip(add_idx, refs[n_in + n_ct:n_in + n_ct + len(add_idx)]))
        out_refs = refs[n_in + n_ct + len(add_idx):]

        def f(*dargs):
            full = list(vals)
            for i, a in zip(diff, dargs):
                full[i] = a
            return tuple(fn(*full))

        _, vjp = jax.vjp(f, *[vals[i] for i in diff])
        grads = vjp(tuple(ct_vals))
        first = pl.program_id(last_axis) == 0
        for o, g, i in zip(out_refs, grads, diff):
            if wants[i] == 'tile':
                if i in add_refs:
                    g = g + add_refs[i][...]
                o[...] = g.astype(o.dtype)
            else:
                @pl.when(first)
                def _(o=o):
                    o[...] = jnp.zeros(o.shape, o.dtype)
                o[...] += g

    ins = list(prims) + list(cts) + [addends[i] for i in add_idx]
    places = [out_place.get(i, (prims[i][0].shape, prims[i][1], prims[i][2])) for i in diff]
    return pl.pallas_call(
        body, name=name, grid=grid,
        in_specs=[_spec(b, m) for _, b, m in ins],
        out_specs=[_spec(b, m) for _, b, m in places],
        out_shape=[jax.ShapeDtypeStruct(s, out_dtype.get(i, F32)) for i, (s, _, _) in zip(diff, places)],
        compiler_params=_cparams(len(grid)),
    )(*[a for a, _, _ in ins])


def _matmul(name, a, b, mode, *, out_dtype=F32, lhs_fn=None, epi=None, extra=None, tm=1024, tn=1024, tk=1024,
            b_cols=None, b_rows=None, out_cols=False):
    if mode == 'tn':
        kdim, m = a.shape
    else:
        m, kdim = a.shape
    if b_rows is not None:
        loc, cols = b.shape[-2], b.shape[-1]
        if mode == 'nn':
            n, k_shards = cols, max(1, tk // loc)
            tk = k_shards * loc
        else:
            n, tn = N_DEV * loc, loc
    elif b_cols is None:
        n = b.shape[0] if mode == 'nt' else b.shape[1]
    else:
        rows, loc = b.shape[-2], b.shape[-1]
        if mode == 'nn':
            n, n_shards = N_DEV * loc, max(1, tn // loc)
            tn = n_shards * loc
        else:
            n, k_shards = rows, max(1, tk // loc)
            tk = k_shards * loc
    if out_cols:
        o_shards = max(1, min(tn, n) // (n // N_DEV))
        tn = o_shards * (n // N_DEV)
    tm, tn, tk = min(tm, m), min(tn, n), min(tk, kdim)
    nk = kdim // tk
    grid = (m // tm, n // tn, nk)
    a_spec = _spec((tk, tm), lambda i, j, k: (k, i)) if mode == 'tn' else _spec((tm, tk), lambda i, j, k: (i, k))
    if b_rows is not None:
        lead = tuple(b_rows)
        skip = (None,) * (1 + len(lead))
        if mode == 'nn':
            b_spec = _spec((k_shards,) + skip[1:] + (tk // k_shards, tn), lambda i, j, k: (k,) + lead + (0, j))
        else:
            b_spec = _spec(skip + (tn, tk), lambda i, j, k: (j,) + lead + (0, k))
    elif b_cols is None:
        b_spec = _spec((tn, tk), lambda i, j, k: (j, k)) if mode == 'nt' else _spec((tk, tn), lambda i, j, k: (k, j))
    else:
        lead = tuple(b_cols)
        skip = (None,) * (1 + len(lead))
        if mode == 'nn':
            b_spec = _spec((n_shards,) + skip[1:] + (tk, tn // n_shards), lambda i, j, k: (j,) + lead + (k, 0))
        else:
            b_spec = _spec((k_shards,) + skip[1:] + (tn, tk // k_shards), lambda i, j, k: (k,) + lead + (j, 0))
    o_spec = _spec((tm, tn), lambda i, j, k: (i, j))
    if out_cols:
        o_place, o_shape = _spec((o_shards, tm, tn // o_shards), lambda i, j, k: (j, i, 0)), (N_DEV, m, n // N_DEV)
    else:
        o_place, o_shape = o_spec, (m, n)
    dims = _DOT_DIMS[mode]

    def body(*refs):
        if extra is None:
            a_ref, b_ref, o_ref, acc = refs
            x_ref = None
        else:
            a_ref, b_ref, x_ref, o_ref, acc = refs
        k = pl.program_id(2)
        av = a_ref[...]
        if lhs_fn is not None:
            av = lhs_fn(av.astype(F32))
        bv = b_ref[...]
        if bv.ndim == 3:
            bv = bv.reshape(tk, tn) if b_rows is not None else jnp.concatenate([bv[g] for g in range(bv.shape[0])], axis=1)
        part = lax.dot_general(av.astype(BF16), bv.astype(BF16), dims, preferred_element_type=F32)

        def finish(r):
            if epi is not None:
                r = epi(r, x_ref[...])
            if out_cols:
                width = tn // o_shards
                for g in range(o_shards):
                    o_ref[g] = r[:, g * width:(g + 1) * width].astype(o_ref.dtype)
            else:
                o_ref[...] = r.astype(o_ref.dtype)

        if nk == 1:
            finish(part)
            return

        @pl.when(k == 0)
        def _():
            acc[...] = part

        @pl.when((k > 0) & (k < nk - 1))
        def _():
            acc[...] += part

        @pl.when(k == nk - 1)
        def _():
            finish(acc[...] + part)

    ops, specs = [a, b], [a_spec, b_spec]
    if extra is not None:
        ops.append(extra)
        specs.append(o_spec)
    return pl.pallas_call(
        body, name=name, grid=grid, in_specs=specs, out_specs=o_place,
        out_shape=jax.ShapeDtypeStruct(o_shape, out_dtype),
        scratch_shapes=[pltpu.VMEM((tm, tn), F32)],
        compiler_params=_cparams(3),
    )(*ops)


def _relu2(x):
    r = jnp.maximum(x, 0.0)
    return r * r


def _mix_fn(x, halo, mu, w0, w1, w2, a0, a1, a2, g1, g2):
    prev_row = jnp.where(pl.program_id(0) == 0, 0.0, halo[7:8, :])
    xx = _shift_down(x, prev_row) - x
    xr, xw, xk, xv, xa, xg = (x + xx * mu[i:i + 1, :] for i in range(6))
    w_pre = w0 + _mm(jnp.tanh(_mm(xw, w1)), w2)
    log_decay = -jnp.exp(-_softplus(-w_pre) - 0.5)
    a = _sigmoid(a0 + _mm(_mm(xa, a1), a2))
    g = _mm(_sigmoid(_mm(xg, g1)), g2)
    return xr, xk, xv, log_decay, a, g


def _chunk_fn(r, lw, k, v, a, k_k, k_a, r_k, inverse=None, with_inverse=False):
    n, n_pair = r.shape[0], r.shape[1] // PAIR
    to_batch = lambda t: jnp.concatenate([t[None, :, p * PAIR:(p + 1) * PAIR] for p in range(n_pair)], axis=0)
    from_batch = lambda t: jnp.concatenate([t[p] for p in range(n_pair)], axis=1)
    r, lw, k, v, a, k_k, k_a, r_k = (to_batch(t) for t in (r, lw, k, v, a, k_k, k_a, r_k))
    lane = lax.broadcasted_iota(jnp.int32, (1, 1, PAIR), 2)
    m0 = lane < HEAD

    def head_sum(t):
        s0 = jnp.sum(jnp.where(m0, t, 0.0), axis=2, keepdims=True)
        s1 = jnp.sum(jnp.where(m0, 0.0, t), axis=2, keepdims=True)
        return jnp.where(m0, s0, s1)

    kk = k * k_k
    kk = kk / jnp.maximum(jnp.sqrt(head_sum(kk * kk)), 1e-12)
    k2 = k * (1.0 + (a - 1.0) * k_a)
    aa, bb = -kk, kk * a
    bonus = head_sum(r * k2 * r_k) * v

    cum = _cumsum_rows(lw)
    p_in, p_ex, p_inv = jnp.exp(cum), jnp.exp(cum - lw), jnp.exp(-cum)
    at, rt, kt, bt = aa * p_ex, r * p_in, k2 * p_inv, bb * p_inv

    def stack_masked(t):
        return jnp.concatenate([jnp.where(m0, t, 0.0), jnp.where(m0, 0.0, t)], axis=1)

    def unstack_sum(t):
        return t[:, :n] + t[:, n:]

    def unstack_select(t):
        return jnp.where(m0, t[:, :n], t[:, n:])

    ti = lax.broadcasted_iota(jnp.int32, (1, 2 * n, 2 * n), 1)
    tj = lax.broadcasted_iota(jnp.int32, (1, 2 * n, 2 * n), 2)
    same = (ti >= n) == (tj >= n)
    incl, strict = same & (ti >= tj), same & (ti > tj)
    m = 2 * n
    rows = lambda *ts: jnp.concatenate(ts, axis=1)
    cols = lambda *ts: jnp.concatenate(ts, axis=2)
    at_s, rt_s, kt_s, bt_s = stack_masked(at), stack_masked(rt), stack_masked(kt), stack_masked(bt)
    v_s, zeros = rows(v, v), jnp.zeros_like(at_s)
    bk_s = rows(bt_s, kt_s)
    gram = _pdot(rows(at_s, rt_s), bk_s, 'nt')
    a_ab = jnp.where(strict, gram[:, :m, :m], 0.0)
    a_ak = jnp.where(strict, gram[:, :m, m:], 0.0)
    a_rb = jnp.where(incl, gram[:, m:, :m], 0.0)
    a_rk = jnp.where(incl, gram[:, m:, m:], 0.0)
    inv = _nilpotent_inverse(a_ab, n) if inverse is None else _known_inverse(a_ab, inverse)
    wy_s = _pdot(inv, cols(at_s, _pdot(a_ak, v_s, 'nn')), 'nn')
    q_oi = _pdot(cols(a_rb, a_rk), rows(wy_s, cols(zeros, v_s)), 'nn')
    q_all = unstack_sum(rt_s + q_oi[:, :, :PAIR])
    oi_all = unstack_select(q_oi[:, :, PAIR:])
    own_head = (lax.broadcasted_iota(jnp.int32, (1, m, PAIR), 1) >= n) == (lax.broadcasted_iota(jnp.int32, (1, m, PAIR), 2) >= HEAD)
    wy_m = cols(wy_s[:, :, :PAIR], jnp.where(own_head, wy_s[:, :, PAIR:], 0.0))
    mm_nn = _pdot(bk_s, rows(wy_m, cols(zeros, stack_masked(v))), 'tn')
    mm, nn = mm_nn[:, :, :PAIR], mm_nn[:, :, PAIR:]
    ei = lax.broadcasted_iota(jnp.int32, (1, PAIR, PAIR), 1)
    ej = lax.broadcasted_iota(jnp.int32, (1, PAIR, PAIR), 2)
    eye_p = (ei == ej).astype(F32)
    decay_col = jnp.sum(eye_p * p_in[:, n - 1:n, :], axis=2, keepdims=True)
    maps = ((decay_col * (eye_p + mm))[None], (decay_col * nn)[None], from_batch(q_all), from_batch(oi_all),
            from_batch(bonus))
    return maps + (inv,) if with_inverse else maps


def _gate_fn(o, bonus, g, lnx_g, lnx_b):
    lane = lax.broadcasted_iota(jnp.int32, (1, PAIR), 1)
    m0 = lane < HEAD

    def head_mean(t):
        s0 = jnp.sum(jnp.where(m0, t, 0.0), axis=1, keepdims=True)
        s1 = jnp.sum(jnp.where(m0, 0.0, t), axis=1, keepdims=True)
        return jnp.where(m0, s0, s1) * (1.0 / HEAD)

    outs = []
    for p in range(N_PAIR):
        sl = slice(p * PAIR, (p + 1) * PAIR)
        oc = o[:, sl] - head_mean(o[:, sl])
        on = oc * lax.rsqrt(head_mean(oc * oc) + GN_EPS) * lnx_g[:, sl] + lnx_b[:, sl]
        outs.append((on + bonus[:, sl]) * g[:, sl])
    return (jnp.concatenate(outs, axis=1),)


def _res_ln_fn(h, y, g, b):
    return (_layer_norm(DN_ALPHA * h + y, g, b),)


def _glu_ln_fn(h, z, g, b):
    mix = z[:, :D_MODEL] * _sigmoid(z[:, D_MODEL:])
    return (_layer_norm(DN_ALPHA * h + mix, g, b),)


def _s5_param_fn(a_re, a_im, log_dt, b_re, b_im):
    dt = jnp.exp(log_dt)
    lam_re = jnp.minimum(a_re, -1e-4)
    lam_im = a_im
    mag = jnp.exp(dt * lam_re)
    abar_re = mag * jnp.cos(dt * lam_im)
    abar_im = mag * jnp.sin(dt * lam_im)
    den = lam_re * lam_re + lam_im * lam_im
    nr, ni = abar_re - 1.0, abar_im
    coef_re = (nr * lam_re + ni * lam_im) / den
    coef_im = (ni * lam_re - nr * lam_im) / den
    return abar_re, abar_im, coef_re, coef_im


def _s5_bbar_fn(coef_re, coef_im, b_re, b_im):
    return coef_re * b_re - coef_im * b_im, coef_re * b_im + coef_im * b_re


def _s5_in_fn(u, b_re, b_im):
    return _pdot(u, b_re[0], 'nn'), _pdot(u, b_im[0], 'nn')


def _s5_out_fn(s_re, s_im, u, c_re, c_im, d):
    y = _pdot(s_re, c_re[0], 'nn') - _pdot(s_im, c_im[0], 'nn') + u * d
    return (_gelu(y),)


def _pairs_to_batch(t):
    return jnp.concatenate([t[None, :, p * PAIR:(p + 1) * PAIR] for p in range(N_PAIR)], axis=0)


def _batch_to_pairs(t):
    return jnp.concatenate([t[p] for p in range(N_PAIR)], axis=1)


STATE_PASSES = 3


def _rwkv_scan_fwd(r, lw, k, v, a, k_k, k_a, r_k):
    n_tok = r.shape[0]
    n_chunk = n_tok // CHUNK

    def body(r_ref, lw_ref, k_ref, v_ref, a_ref, kk_ref, ka_ref, rk_ref, o_ref, bonus_ref, zs_ref, inv_ref, m_ref, q_ref, z):
        @pl.when(pl.program_id(0) == 0)
        def _():
            z[...] = jnp.zeros(z.shape, F32)

        m_all, n_all, q, oi, bonus, inv = _chunk_fn(r_ref[...], lw_ref[...], k_ref[...], v_ref[...], a_ref[...],
                                                    kk_ref[...], ka_ref[...], rk_ref[...], with_inverse=True)
        zb = z[...]
        zs_ref[0] = zb
        inv_ref[0] = inv.astype(inv_ref.dtype)
        m_ref[...] = m_all
        q_ref[...] = q
        bonus_ref[...] = bonus
        o_ref[...] = _batch_to_pairs(_dot_passes(_pairs_to_batch(q), zb, 'nn', STATE_PASSES)) + oi
        z[...] = _dot_passes(m_all[0], zb, 'nn', STATE_PASSES) + n_all[0]

    tok = _spec((CHUNK, D_MODEL), lambda c: (c, 0))
    par = _spec((1, D_MODEL), lambda c: (0, 0))
    st = _spec((1, N_PAIR, PAIR, PAIR), lambda c: (c, 0, 0, 0))
    return pl.pallas_call(
        body, name="rwkv_scan_fwd", grid=(n_chunk,),
        in_specs=[tok] * 5 + [par] * 3, out_specs=[tok, tok, st, st, st, tok],
        out_shape=[jax.ShapeDtypeStruct((n_tok, D_MODEL), F32)] * 2
        + [jax.ShapeDtypeStruct((n_chunk, N_PAIR, PAIR, PAIR), F32), jax.ShapeDtypeStruct((n_chunk, N_PAIR, PAIR, PAIR), BF16),
           jax.ShapeDtypeStruct((n_chunk, N_PAIR, PAIR, PAIR), F32), jax.ShapeDtypeStruct((n_tok, D_MODEL), F32)],
        scratch_shapes=[pltpu.VMEM((N_PAIR, PAIR, PAIR), F32)],
        compiler_params=_cparams(1),
    )(r, lw, k, v, a, k_k, k_a, r_k)


def _rwkv_scan_bwd(r, lw, k, v, a, k_k, k_a, r_k, zs, invs, ms, q, d_o, d_bonus):
    n_tok = r.shape[0]
    n_chunk = n_tok // CHUNK

    def body(r_ref, lw_ref, k_ref, v_ref, a_ref, kk_ref, ka_ref, rk_ref, zs_ref, inv_ref, m_ref, q_ref, do_ref, db_ref,
             dr_ref, dlw_ref, dk_ref, dv_ref, da_ref, dkk_ref, dka_ref, drk_ref, g):
        sums = (dkk_ref, dka_ref, drk_ref)

        @pl.when(pl.program_id(0) == 0)
        def _():
            g[...] = jnp.zeros(g.shape, F32)
            for s_ref in sums:
                s_ref[...] = jnp.zeros(s_ref.shape, F32)

        prim = (r_ref[...], lw_ref[...], k_ref[...], v_ref[...], a_ref[...], kk_ref[...], ka_ref[...], rk_ref[...])
        known = inv_ref[0].astype(F32)
        _, chunk_vjp = jax.vjp(lambda *p: _chunk_fn(*p, inverse=known), *prim)
        m_all, q = m_ref[...], q_ref[...]
        gb, zb, d_out = g[...], zs_ref[0], do_ref[...]
        dob = _pairs_to_batch(d_out)
        d_m = _dot_passes(gb, zb, 'nt', STATE_PASSES)
        d_q = _batch_to_pairs(_dot_passes(dob, zb, 'nt', STATE_PASSES))
        grads = chunk_vjp((d_m[None], gb[None], d_q, d_out, db_ref[...]))
        for out_ref, val in zip((dr_ref, dlw_ref, dk_ref, dv_ref, da_ref), grads[:5]):
            out_ref[...] = val.astype(out_ref.dtype)
        for s_ref, val in zip(sums, grads[5:]):
            s_ref[...] += val
        same_head = (lax.broadcasted_iota(jnp.int32, (1, PAIR, PAIR), 1) // HEAD
                     == lax.broadcasted_iota(jnp.int32, (1, PAIR, PAIR), 2) // HEAD)
        g[...] = (_dot_passes(m_all[0], gb, 'tn', STATE_PASSES)
                  + jnp.where(same_head, _dot_passes(_pairs_to_batch(q), dob, 'tn', STATE_PASSES), 0.0))

    tok = _spec((CHUNK, D_MODEL), lambda c: (n_chunk - 1 - c, 0))
    par = _spec((1, D_MODEL), lambda c: (0, 0))
    st = _spec((1, N_PAIR, PAIR, PAIR), lambda c: (n_chunk - 1 - c, 0, 0, 0))
    tok_shape = lambda dt: jax.ShapeDtypeStruct((n_tok, D_MODEL), dt)
    return pl.pallas_call(
        body, name="rwkv_scan_bwd", grid=(n_chunk,),
        in_specs=[tok] * 5 + [par] * 3 + [st, st, st, tok, tok, tok], out_specs=[tok] * 5 + [par] * 3,
        out_shape=[tok_shape(BF16), tok_shape(F32), tok_shape(BF16), tok_shape(BF16), tok_shape(F32)]
        + [jax.ShapeDtypeStruct((1, D_MODEL), F32)] * 3,
        scratch_shapes=[pltpu.VMEM((N_PAIR, PAIR, PAIR), F32)],
        compiler_params=_cparams(1),
    )(r, lw, k, v, a, k_k, k_a, r_k, zs, invs, ms, q, d_o, d_bonus)


S5_TIME_TILE = 1024
S5_GROUP = 8


def _scan_rows(re_ref, im_ref, a1, carry, reverse):
    lb = re_ref.shape[1]
    grp, n_grp = S5_GROUP, re_ref.shape[0] // S5_GROUP

    def cmul(xr, xi, yr, yi):
        return xr * yr - xi * yi, xr * yi + xi * yr

    a2 = cmul(*a1, *a1)
    a4 = cmul(*a2, *a2)
    a8 = cmul(*a4, *a4)
    row = lax.broadcasted_iota(jnp.int32, (grp, lb), 0)
    expo = (grp - row) if reverse else (row + 1)
    pw = (jnp.ones((grp, lb), F32), jnp.zeros((grp, lb), F32))
    for bit, ap in ((1, a1), (2, a2), (4, a4), (8, a8)):
        nxt = cmul(*pw, *ap)
        sel = (expo & bit) != 0
        pw = (jnp.where(sel, nxt[0], pw[0]), jnp.where(sel, nxt[1], pw[1]))

    steps = []
    for d, ad in ((1, a1), (2, a2), (4, a4)):
        keep = (row < grp - d) if reverse else (row >= d)
        steps.append(((grp - d) if reverse else d, jnp.where(keep, ad[0], 0.0), jnp.where(keep, ad[1], 0.0)))

    def group(i, c):
        gi = (n_grp - 1 - i) if reverse else i
        rows = pl.ds(pl.multiple_of(gi * grp, grp), grp)
        xr, xi = re_ref[rows, :], im_ref[rows, :]
        for shift, mr, mi in steps:
            pr, pi = cmul(mr, mi, pltpu.roll(xr, shift, 0), pltpu.roll(xi, shift, 0))
            xr, xi = xr + pr, xi + pi
        cr, ci = cmul(*pw, c[0], c[1])
        xr, xi = xr + cr, xi + ci
        re_ref[rows, :] = xr
        im_ref[rows, :] = xi
        edge = slice(0, 1) if reverse else slice(grp - 1, grp)
        return xr[edge, :], xi[edge, :]

    return lax.fori_loop(0, n_grp, group, carry)


def _s5_fwd(u, b_re, b_im, a_re, a_im, c_re, c_im, d_row):
    n_tok = u.shape[0]
    tt = min(S5_TIME_TILE, n_tok)

    def body(u_ref, bre_ref, bim_ref, are_ref, aim_ref, cre_ref, cim_ref, d_ref, sre_ref, sim_ref, yg_ref, carry):
        @pl.when(pl.program_id(1) == 0)
        def _():
            carry[...] = jnp.zeros(carry.shape, F32)

        uv = u_ref[...]
        sre_ref[...], sim_ref[...] = _s5_in_fn(uv, bre_ref[...], bim_ref[...])
        c = _scan_rows(sre_ref, sim_ref, (are_ref[...], aim_ref[...]), (carry[0:1, :], carry[1:2, :]), False)
        carry[0:1, :] = c[0]
        carry[1:2, :] = c[1]
        (yg,) = _s5_out_fn(sre_ref[...], sim_ref[...], uv, cre_ref[...], cim_ref[...], d_ref[...])
        yg_ref[...] = yg.astype(yg_ref.dtype)

    u_blk = _spec((tt, SSM_BLK_IN), lambda l, t: (t, l))
    s_blk = _spec((tt, SSM_BLK_ST), lambda l, t: (t, l))
    blk3 = lambda arr: _spec((1,) + arr.shape[1:], lambda l, t: (l, 0, 0))
    par = lambda width: _spec((1, width), lambda l, t: (0, l))
    return pl.pallas_call(
        body, name="s5_fwd", grid=(N_SSM_BLK, n_tok // tt),
        in_specs=[u_blk, blk3(b_re), blk3(b_im), par(SSM_BLK_ST), par(SSM_BLK_ST), blk3(c_re), blk3(c_im), par(SSM_BLK_IN)],
        out_specs=[s_blk, s_blk, u_blk],
        out_shape=[jax.ShapeDtypeStruct((n_tok, SSM_LANES), F32)] * 2 + [jax.ShapeDtypeStruct((n_tok, D_MODEL), BF16)],
        scratch_shapes=[pltpu.VMEM((2, SSM_BLK_ST), F32)],
        compiler_params=_cparams(2),
    )(u, b_re, b_im, a_re, a_im, c_re, c_im, d_row)


def _s5_bwd(dyg, u, s_re, s_im, dh_res, b_re, b_im, a_re, a_im, c_re, c_im, d_row):
    n_tok = u.shape[0]
    tt = min(S5_TIME_TILE, n_tok)
    n_t = n_tok // tt

    def body(dyg_ref, u_ref, sre_ref, sim_ref, hre_ref, him_ref, res_ref, bre_ref, bim_ref, are_ref, aim_ref,
             cre_ref, cim_ref, d_ref, du_ref, dbre_ref, dbim_ref, dcre_ref, dcim_ref, dd_ref, dare_ref, daim_ref,
             carry, gre, gim):
        i = pl.program_id(1)
        sums = (dbre_ref, dbim_ref, dcre_ref, dcim_ref, dd_ref, dare_ref, daim_ref)

        @pl.when(i == 0)
        def _():
            carry[...] = jnp.zeros(carry.shape, F32)
            for r in sums:
                r[...] = jnp.zeros(r.shape, F32)

        uv, sre, sim = u_ref[...], sre_ref[...], sim_ref[...]
        _, out_vjp = jax.vjp(_s5_out_fn, sre, sim, uv, cre_ref[...], cim_ref[...], d_ref[...])
        gre[...], gim[...], du_out, dcre, dcim, dd = out_vjp((dyg_ref[...],))
        c = _scan_rows(gre, gim, (are_ref[...], -aim_ref[...]), (carry[0:1, :], carry[1:2, :]), True)
        carry[0:1, :] = c[0]
        carry[1:2, :] = c[1]
        g_re, g_im = gre[...], gim[...]
        at_start = i == n_t - 1
        p_re = _shift_down(sre, jnp.where(at_start, 0.0, hre_ref[7:8, :]))
        p_im = _shift_down(sim, jnp.where(at_start, 0.0, him_ref[7:8, :]))
        _, in_vjp = jax.vjp(_s5_in_fn, uv, bre_ref[...], bim_ref[...])
        du_in, dbre, dbim = in_vjp((g_re, g_im))
        du_ref[...] = du_out + du_in + res_ref[...]
        dare = jnp.sum(g_re * p_re + g_im * p_im, axis=0, keepdims=True)
        daim = jnp.sum(g_im * p_re - g_re * p_im, axis=0, keepdims=True)
        for r, val in zip(sums, (dbre, dbim, dcre, dcim, dd, dare, daim)):
            r[...] += val

    u_blk = _spec((tt, SSM_BLK_IN), lambda l, t: (n_t - 1 - t, l))
    s_blk = _spec((tt, SSM_BLK_ST), lambda l, t: (n_t - 1 - t, l))
    halo = _spec((8, SSM_BLK_ST), lambda l, t: (jnp.maximum((n_t - 1 - t) * (tt // 8) - 1, 0), l))
    blk3 = lambda arr: _spec((1,) + arr.shape[1:], lambda l, t: (l, 0, 0))
    par = lambda width: _spec((1, width), lambda l, t: (0, l))
    params = [blk3(b_re), blk3(b_im), par(SSM_BLK_ST), par(SSM_BLK_ST), blk3(c_re), blk3(c_im), par(SSM_BLK_IN)]
    shape = lambda arr: jax.ShapeDtypeStruct(arr.shape, F32)
    return pl.pallas_call(
        body, name="s5_bwd", grid=(N_SSM_BLK, n_t),
        in_specs=[u_blk, u_blk, s_blk, s_blk, halo, halo, u_blk] + params,
        out_specs=[u_blk, blk3(b_re), blk3(b_im), blk3(c_re), blk3(c_im), par(SSM_BLK_IN), par(SSM_BLK_ST), par(SSM_BLK_ST)],
        out_shape=[shape(u), shape(b_re), shape(b_im), shape(c_re), shape(c_im), shape(d_row), shape(a_re), shape(a_im)],
        scratch_shapes=[pltpu.VMEM((2, SSM_BLK_ST), F32), pltpu.VMEM((tt, SSM_BLK_ST), F32), pltpu.VMEM((tt, SSM_BLK_ST), F32)],
        compiler_params=_cparams(2),
    )(dyg, u, s_re, s_im, s_re, s_im, dh_res, b_re, b_im, a_re, a_im, c_re, c_im, d_row)


def _loss_call(h, target, tm):
    n_tok, d = h.shape

    def body(h_ref, t_ref, acc_ref, dh_ref):
        @pl.when(pl.program_id(0) == 0)
        def _():
            acc_ref[...] = jnp.zeros(acc_ref.shape, F32)

        e = h_ref[...] - t_ref[...]
        dh_ref[...] = e * (1.0 / d)
        acc_ref[...] += jnp.sum(jnp.sum(e * e, axis=1, keepdims=True), axis=0, keepdims=True)

    tok = _spec((tm, d), lambda i: (i, 0))
    return pl.pallas_call(
        body, name="loss", grid=(n_tok // tm,),
        in_specs=[tok, tok], out_specs=[_spec((8, 128), lambda i: (0, 0)), tok],
        out_shape=[jax.ShapeDtypeStruct((8, 128), F32), jax.ShapeDtypeStruct(h.shape, F32)],
        compiler_params=_cparams(1),
    )(h, target)


def _block_diag(t):
    nb, ng, rows, cols = t.shape
    eye = jnp.eye(ng, dtype=t.dtype)
    return jnp.einsum('bgrc,gh->bgrhc', t, eye).reshape(nb, ng * rows, ng * cols)


def _block_diag_t(t, rows, cols):
    nb, ng = t.shape[0], t.shape[1] // rows
    t5 = t.reshape(nb, ng, rows, ng, cols)
    return jnp.einsum('bgrhc,gh->bgrc', t5, jnp.eye(ng, dtype=t.dtype))


def _mlp_fwd(layer, h, w1g, w2g):
    pre = _matmul(f"mlp{layer}_up", h, w1g, 'nn', b_cols=(layer,), out_dtype=BF16)
    return pre, _matmul(f"mlp{layer}_down", pre, w2g, 'nn', lhs_fn=_relu2, b_rows=(layer,))


def _mlp_bwd(layer, h, pre, dy, dh_res, w1g, w2g):
    dpre = _matmul(f"mlp{layer}_down_dx", dy, w2g, 'nt', epi=lambda acc, p: acc * (2.0 * jnp.maximum(p, 0.0)),
                   extra=pre, b_rows=(layer,), out_dtype=BF16)
    dw2 = _matmul(f"mlp{layer}_down_dw", pre, dy, 'tn', lhs_fn=_relu2, out_dtype=GRAD_DTYPE)
    dw1 = _matmul(f"mlp{layer}_up_dw", h, dpre, 'tn', out_cols=True, out_dtype=GRAD_DTYPE)
    dh = _matmul(f"mlp{layer}_up_dx", dpre, w1g, 'nt', epi=lambda acc, e: acc + e, extra=dh_res, b_cols=(layer,))
    return dh, dw1, dw2


def _local_step(x, target, w, wg, late_forward, late_weights, exchange):
    n_tok = x.shape[0]
    tm = min(TOKEN_TILE, n_tok)
    n_tile, n_chunk = n_tok // tm, n_tok // CHUNK
    row = lambda v: v.reshape(1, -1)
    big = {n: wg[n].reshape(D_MODEL, D_MODEL) for n in ('rw_wr', 'rw_wk', 'rw_wv', 'rw_wo')}
    ln_g, ln_b = w['ln_g'], w['ln_b']
    grads = {}
    to_slots = lambda t: t.reshape(N_DEV, t.shape[0] // N_DEV, t.shape[1])

    halo_spec = (8, D_MODEL), lambda i: (jnp.maximum(i * (tm // 8) - 1, 0), 0)
    mix_p = [_tile(x, tm), (x,) + halo_spec, _full(w['rw_mu'][0]), _full(row(w['rw_w0'])), _full(w['rw_w1'][0]),
             _full(w['rw_w2'][0]), _full(row(w['rw_a0'])), _full(w['rw_a1'][0]), _full(w['rw_a2'][0]),
             _full(w['rw_g1'][0]), _full(w['rw_g2'][0])]
    tok_out = lambda dt: ((n_tok, D_MODEL), dt, (tm, D_MODEL), lambda i: (i, 0))
    xr, xk, xv, lw, a, g = _fwd_call("rwkv_mix", _mix_fn, (n_tile,), mix_p,
                                     [tok_out(BF16), tok_out(BF16), tok_out(BF16), tok_out(F32), tok_out(F32), tok_out(F32)])
    r = _matmul("rwkv_r", xr, big['rw_wr'], 'nn')
    k = _matmul("rwkv_k", xk, big['rw_wk'], 'nn')
    v = _matmul("rwkv_v", xv, big['rw_wv'], 'nn')

    scan_in = (r, lw, k, v, a, row(w['rw_k_k']), row(w['rw_k_a']), w['rw_r_k'].reshape(1, -1))
    o, bonus, *scan_saved = _rwkv_scan_fwd(*scan_in)

    zero = late_forward(o)
    gate_p = [_tile(o, tm), _tile(bonus, tm), _tile(g, tm), _full(row(w['rw_lnx_g']) + zero), _full(row(w['rw_lnx_b']))]
    (og,) = _fwd_call("rwkv_gate", _gate_fn, (n_tile,), gate_p, [tok_out(BF16)])
    y0 = _matmul("rwkv_o", og, big['rw_wo'], 'nn')
    tl = min(LN_TILE, n_tok)
    n_ln = n_tok // tl
    ln_out = lambda dt: ((n_tok, D_MODEL), dt, (tl, D_MODEL), lambda i: (i, 0))
    ln0_p = [_tile(x, tl), _tile(y0, tl), _full(ln_g[0:1]), _full(ln_b[0:1])]
    twice = lambda fn: lambda *args: fn(*args) * 2
    h1, h1_b = _fwd_call("ln0", twice(_res_ln_fn), (n_ln,), ln0_p, [ln_out(F32), ln_out(BF16)])
    late = late_weights(h1)
    w1g, w2g, glu_g = late['mlp_w1'], late['mlp_w2'], late['s5_w_glu']
    pre0, y1 = _mlp_fwd(0, h1_b, w1g, w2g)
    ln1_p = [_tile(h1, tl), _tile(y1, tl), _full(ln_g[1:2]), _full(ln_b[1:2])]
    (h2,) = _fwd_call("ln1", _res_ln_fn, (n_ln,), ln1_p, [ln_out(F32)])

    gp = (SSM_GROUPS, SSM_STATE)
    par_p = [_full(w['s5_a_re'][0]), _full(w['s5_a_im'][0]), _full(w['s5_log_dt'].reshape(SSM_GROUPS, 1))]
    s5_par_fn = lambda ar, ai, ld: _s5_param_fn(ar, ai, ld, None, None)
    gp_out = (gp, F32, gp, lambda i: (0, 0))
    abar_re, abar_im, coef_re, coef_im = _fwd_call("s5_param", s5_par_fn, (1,), par_p, [gp_out] * 4)
    b_flat = [w[n][0].reshape(SSM_LANES, SSM_GROUP) for n in ('s5_b_re', 's5_b_im')]
    bbar_p = [_full(coef_re.reshape(SSM_LANES, 1)), _full(coef_im.reshape(SSM_LANES, 1)), _full(b_flat[0]), _full(b_flat[1])]
    bb_out = ((SSM_LANES, SSM_GROUP), F32, (SSM_LANES, SSM_GROUP), lambda i: (0, 0))
    bbar_re, bbar_im = _fwd_call("s5_bbar", _s5_bbar_fn, (1,), bbar_p, [bb_out] * 2)
    to_in = lambda t: _block_diag(t.reshape(N_SSM_BLK, 8, SSM_STATE, SSM_GROUP).transpose(0, 1, 3, 2))
    to_out = lambda t: _block_diag(t.reshape(N_SSM_BLK, 8, SSM_GROUP, SSM_STATE).transpose(0, 1, 3, 2))
    bblk_re, bblk_im = to_in(bbar_re), to_in(bbar_im)
    cblk_re, cblk_im = to_out(w['s5_c_re'][0]), to_out(w['s5_c_im'][0])
    a_row_re, a_row_im = abar_re.reshape(1, SSM_LANES), abar_im.reshape(1, SSM_LANES)

    d_row = row(w['s5_d'])
    s5_params = (bblk_re, bblk_im, a_row_re, a_row_im, cblk_re, cblk_im, d_row)
    s_re, s_im, yg = _s5_fwd(h2, *s5_params)
    z = _matmul("s5_glu", yg, glu_g, 'nn', b_cols=(0,))
    ln2_p = [_tile(h2, tl), _tile(z, tl), _full(ln_g[2:3]), _full(ln_b[2:3])]
    h3, h3_b = _fwd_call("ln2", twice(_glu_ln_fn), (n_ln,), ln2_p, [ln_out(F32), ln_out(BF16)])
    pre1, y3 = _mlp_fwd(1, h3_b, w1g, w2g)
    ln3_p = [_tile(h3, tl), _tile(y3, tl), _full(ln_g[3:4]), _full(ln_b[3:4])]
    (h4,) = _fwd_call("ln3", _res_ln_fn, (n_ln,), ln3_p, [ln_out(F32)])

    loss_acc, dh4 = _loss_call(h4, target, tl)

    ln_w = ['tile', 'tile_bf16', 'acc', 'acc']
    dh3_res, dy3, dg3, db3 = _bwd_call("ln3_bwd", _res_ln_fn, (n_ln,), ln3_p, [_tile(dh4, tl)], ln_w)
    dh3, dw1_1, dw2_1 = _mlp_bwd(1, h3_b, pre1, dy3, dh3_res, w1g, w2g)
    zero = exchange([('mlp_w1', 1, dw1_1), ('mlp_w2', 1, to_slots(dw2_1))])
    ln2_p[2] = _full(ln_g[2:3] + zero)
    dh2_res, dz, dg2, db2 = _bwd_call("ln2_bwd", _glu_ln_fn, (n_ln,), ln2_p, [_tile(dh3, tl)], ln_w)
    dyg = _matmul("s5_glu_dx", dz, glu_g, 'nt', b_cols=(0,))
    dw_glu = _matmul("s5_glu_dw", yg, dz, 'tn', out_cols=True, out_dtype=GRAD_DTYPE)
    dh2, dbb_re, dbb_im, dcb_re, dcb_im, dd, dabar_re, dabar_im = _s5_bwd(dyg, h2, s_re, s_im, dh2_res, *s5_params)
    from_in = lambda t: _block_diag_t(t, SSM_GROUP, SSM_STATE).transpose(0, 1, 3, 2).reshape(SSM_LANES, SSM_GROUP)
    from_out = lambda t: _block_diag_t(t, SSM_STATE, SSM_GROUP).transpose(0, 1, 3, 2).reshape(1, SSM_GROUPS, SSM_GROUP, SSM_STATE)
    grads['s5_c_re'], grads['s5_c_im'] = from_out(dcb_re), from_out(dcb_im)
    grads['s5_d'] = dd
    dcoef_re, dcoef_im, db_re, db_im = _bwd_call(
        "s5_bbar_bwd", _s5_bbar_fn, (1,), bbar_p, [_full(from_in(dbb_re)), _full(from_in(dbb_im))], ['acc'] * 4)
    grads['s5_b_re'] = db_re.reshape(w['s5_b_re'].shape)
    grads['s5_b_im'] = db_im.reshape(w['s5_b_im'].shape)
    par_ct = [_full(dabar_re.reshape(gp)), _full(dabar_im.reshape(gp)), _full(dcoef_re.reshape(gp)), _full(dcoef_im.reshape(gp))]
    da_re, da_im, dlog_dt = _bwd_call("s5_param_bwd", s5_par_fn, (1,), par_p, par_ct, ['acc'] * 3)
    grads['s5_a_re'], grads['s5_a_im'], grads['s5_log_dt'] = da_re[None], da_im[None], dlog_dt.reshape(1, SSM_GROUPS)
    zero = exchange([(n, 0, grads[n]) for n in ('s5_a_re', 's5_a_im', 's5_log_dt', 's5_b_re', 's5_b_im', 's5_c_re', 's5_c_im')])
    ln1_p[2] = _full(ln_g[1:2] + zero)

    dh1_res, dy1, dg1, db1 = _bwd_call("ln1_bwd", _res_ln_fn, (n_ln,), ln1_p, [_tile(dh2, tl)], ln_w)
    dh1, dw1_0, dw2_0 = _mlp_bwd(0, h1_b, pre0, dy1, dh1_res, w1g, w2g)
    dx_res, dy0, dg0, db0 = _bwd_call("ln0_bwd", _res_ln_fn, (n_ln,), ln0_p, [_tile(dh1, tl)], ln_w)
    grads['ln_g'] = jnp.concatenate([dg0, dg1, dg2, dg3])
    grads['ln_b'] = jnp.concatenate([db0, db1, db2, db3])
    dog = _matmul("rwkv_o_dx", dy0, big['rw_wo'], 'nt')
    dw_o = to_slots(_matmul("rwkv_o_dw", og, dy0, 'tn', out_dtype=GRAD_DTYPE))
    zero = exchange([('s5_w_glu', 0, dw_glu), ('mlp_w1', 0, dw1_0), ('mlp_w2', 0, to_slots(dw2_0)), ('rw_wo', 0, dw_o)])
    gate_p[3] = _full(row(w['rw_lnx_g']) + zero)
    d_o, dbonus, dgate, grads['rw_lnx_g'], grads['rw_lnx_b'] = _bwd_call(
        "rwkv_gate_bwd", _gate_fn, (n_tile,), gate_p, [_tile(dog, tm)], ['tile', 'tile', 'tile', 'acc', 'acc'])
    dr, dlw, dk, dv, da, grads['rw_k_k'], grads['rw_k_a'], dr_k = _rwkv_scan_bwd(*scan_in, *scan_saved, d_o, dbonus)
    grads['rw_r_k'] = dr_k.reshape(w['rw_r_k'].shape)
    dw_r = to_slots(_matmul("rwkv_r_dw", xr, dr, 'tn', out_dtype=GRAD_DTYPE))
    dw_k = to_slots(_matmul("rwkv_k_dw", xk, dk, 'tn', out_dtype=GRAD_DTYPE))
    dw_v = to_slots(_matmul("rwkv_v_dw", xv, dv, 'tn', out_dtype=GRAD_DTYPE))
    done = ('ln_g', 'ln_b', 'rw_lnx_g', 'rw_lnx_b', 'rw_k_k', 'rw_k_a', 'rw_r_k')
    zero = exchange([('rw_wr', 0, dw_r), ('rw_wk', 0, dw_k), ('rw_wv', 0, dw_v)] + [(n, 0, grads[n]) for n in done])
    mix_p[3] = _full(row(w['rw_w0']) + zero)
    dxr = _matmul("rwkv_r_dx", dr, big['rw_wr'], 'nt')
    dxk = _matmul("rwkv_k_dx", dk, big['rw_wk'], 'nt')
    dxv = _matmul("rwkv_v_dx", dv, big['rw_wv'], 'nt')
    mix_ct = [_tile(t, tm) for t in (dxr, dxk, dxv, dlw, da, dgate)]
    halo_grad = ('tile', ((n_tile * 8, D_MODEL), (8, D_MODEL), lambda i: (i, 0)))
    res = _bwd_call("rwkv_mix_bwd", _mix_fn, (n_tile,), mix_p, mix_ct, ['tile', halo_grad] + ['acc'] * 9,
                    addends={0: _tile(dx_res, tm)})
    corr = res[1].reshape(n_tile, 8, D_MODEL)[1:, 7:8, :]
    corr = jnp.pad(corr, ((0, 1), (tm - 1, 0), (0, 0)))
    dx = (res[0].reshape(n_tile, tm, D_MODEL) + corr).reshape(n_tok, D_MODEL)
    (grads['rw_mu'], grads['rw_w0'], grads['rw_w1'], grads['rw_w2'], grads['rw_a0'], grads['rw_a1'], grads['rw_a2'],
     grads['rw_g1'], grads['rw_g2']) = [t[None] if t.shape[0] != 1 else t for t in res[2:]]
    return loss_acc[0, 0], dx, grads


def _full_shape(local_shape, axis):
    return local_shape[:axis] + (N_DEV * local_shape[axis],) + local_shape[axis + 1:]


def _split_shape(local_shape, axis):
    return local_shape[:axis] + (N_DEV, local_shape[axis]) + local_shape[axis + 1:]


def _mesh_pos():
    return lax.axis_index("x"), lax.axis_index("y"), lax.axis_index("c")


def _flip(v, f):
    return 1 - v if f else v


def _hbm_call(name, body, arrays, out_shapes, n_sems):
    n_t = len(arrays)
    hbm = pl.BlockSpec(memory_space=pl.ANY)
    return pl.pallas_call(
        body, name=name, out_shape=out_shapes, in_specs=[hbm] * n_t, out_specs=[hbm] * n_t,
        scratch_shapes=[pltpu.SemaphoreType.DMA((n_t, n_sems)), pltpu.SemaphoreType.DMA((n_t, n_sems)),
                        pltpu.SemaphoreType.DMA((n_t,))],
    )(*arrays)


def _all_gather(name, locals_):
    n_t = len(locals_)

    def body(*refs):
        x_refs, out_refs = refs[:n_t], refs[n_t:2 * n_t]
        send_sems, recv_sems, local_sems = refs[2 * n_t:]
        x, y, c = _mesh_pos()
        me, sibling = (x, y, c), (x, y, 1 - c)
        chips = [(1 - x, y), (x, 1 - y), (1 - x, 1 - y)]

        def copy(t, k, block, to, own=False):
            slot = out_refs[t].at[4 * block[0] + 2 * block[1] + block[2]]
            return pltpu.make_async_remote_copy(
                src_ref=x_refs[t] if own else slot, dst_ref=slot,
                send_sem=send_sems.at[t, k], recv_sem=recv_sems.at[t, k],
                device_id=to, device_id_type=pl.DeviceIdType.MESH)

        mine = [pltpu.make_async_copy(x_refs[t], out_refs[t].at[4 * x + 2 * y + c], local_sems.at[t]) for t in range(n_t)]
        for cp in mine:
            cp.start()
        sent = []
        for t in range(n_t):
            sent.append(copy(t, 0, me, sibling, own=True))
            sent += [copy(t, 1 + j, me, (*chip, c), own=True) for j, chip in enumerate(chips)]
        for cp in sent:
            cp.start()
        for j, chip in enumerate(chips):
            for t in range(n_t):
                copy(t, 1 + j, (*chip, c), me).wait_recv()
                passed = copy(t, 4 + j, (*chip, c), sibling)
                passed.start()
                sent.append(passed)
        for t in range(n_t):
            copy(t, 0, sibling, me).wait_recv()
            for j, chip in enumerate(chips):
                copy(t, 4 + j, (*chip, 1 - c), me).wait_recv()
        for cp in sent:
            cp.wait_send()
        for cp in mine:
            cp.wait()

    outs = [jax.ShapeDtypeStruct((N_DEV,) + a.shape, a.dtype) for a in locals_]
    return _hbm_call(name, body, locals_, outs, 7)


def _all_to_all(name, blocks):
    n_t = len(blocks)

    def body(*refs):
        x_refs, out_refs = refs[:n_t], refs[n_t:2 * n_t]
        send_sems, recv_sems, local_sems = refs[2 * n_t:]
        x, y, c = _mesh_pos()
        my_slot = 4 * x + 2 * y + c
        mine = [pltpu.make_async_copy(x_refs[t].at[my_slot], out_refs[t].at[my_slot], local_sems.at[t]) for t in range(n_t)]
        for cp in mine:
            cp.start()
        copies = []
        for k in range(1, N_DEV):
            peer = (_flip(x, k & 4), _flip(y, k & 2), _flip(c, k & 1))
            peer_slot = 4 * peer[0] + 2 * peer[1] + peer[2]
            for t in range(n_t):
                sems = dict(send_sem=send_sems.at[t, k - 1], recv_sem=recv_sems.at[t, k - 1],
                            device_id=peer, device_id_type=pl.DeviceIdType.MESH)
                cp = pltpu.make_async_remote_copy(src_ref=x_refs[t].at[peer_slot], dst_ref=out_refs[t].at[my_slot], **sems)
                cp.start()
                landed = pltpu.make_async_remote_copy(src_ref=x_refs[t].at[my_slot], dst_ref=out_refs[t].at[peer_slot], **sems)
                copies.append((cp, landed))
        for sent, landed in copies:
            landed.wait_recv()
            sent.wait_send()
        for cp in mine:
            cp.wait()

    outs = [jax.ShapeDtypeStruct(a.shape, a.dtype) for a in blocks]
    return _hbm_call(name, body, blocks, outs, 7)


_HBM = pl.BlockSpec(memory_space=pltpu.HBM)
_SEM = pl.BlockSpec(memory_space=pltpu.SEMAPHORE)
_EFFECT = pltpu.SideEffectType.DATAFLOW_SIDE_EFFECTING


def _peer_copies(x_refs, land_refs, send_sems, recv_sems, gather, mine):
    x, y, c = _mesh_pos()
    my_slot = 4 * x + 2 * y + c
    copies = []
    for k in range(1, N_DEV):
        peer = (_flip(x, k & 4), _flip(y, k & 2), _flip(c, k & 1))
        peer_slot = 4 * peer[0] + 2 * peer[1] + peer[2]
        for t, (x_ref, land) in enumerate(zip(x_refs, land_refs)):
            sem = t * (N_DEV - 1) + k - 1
            copies.append(pltpu.make_async_remote_copy(
                src_ref=x_ref if gather[t] else x_ref.at[peer_slot], dst_ref=land.at[my_slot if mine else peer_slot],
                send_sem=send_sems.at[sem], recv_sem=recv_sems.at[sem], device_id=peer, device_id_type=pl.DeviceIdType.MESH))
    return copies


def _split_start(name, held, plan, n_copies):
    n_h = len(held)

    def body(*refs):
        for sent in plan(refs[:n_h], refs[n_h], refs[n_h + 1], True):
            sent.start()
        refs[-1][...] = jnp.zeros((8, 128), F32)

    sems = pltpu.SemaphoreType.DMA((n_copies,))
    out = pl.pallas_call(
        body, name=name,
        out_shape=(sems, sems, *[pltpu.HBM(a.shape, a.dtype) for a in held], jax.ShapeDtypeStruct((8, 128), F32)),
        in_specs=[_HBM] * n_h,
        out_specs=(_SEM, _SEM, *[_HBM] * n_h, pl.BlockSpec(memory_space=pltpu.VMEM)),
        input_output_aliases={i: 2 + i for i in range(n_h)},
        compiler_params=pltpu.CompilerParams(has_side_effects=_EFFECT),
    )(*[pltpu.with_memory_space_constraint(a, pltpu.HBM) for a in held])
    return dict(send=out[0], recv=out[1], held=out[2:2 + n_h], zero=out[-1][0, 0], plan=plan)


def _split_wait(name, started, after):
    held, plan = started['held'], started['plan']
    n_h = len(held)

    def body(*refs):
        for sent in plan(refs[:n_h], refs[n_h], refs[n_h + 1], True):
            sent.wait_send()
        for landed in plan(refs[:n_h], refs[n_h], refs[n_h + 1], False):
            landed.wait_recv()

    return pl.pallas_call(
        body, name=name,
        out_shape=[pltpu.HBM(a.shape, a.dtype) for a in held],
        in_specs=[_HBM] * n_h + [_SEM, _SEM, pl.BlockSpec(memory_space=pl.ANY)],
        out_specs=[_HBM] * n_h,
        input_output_aliases={i: i for i in range(n_h)},
        compiler_params=pltpu.CompilerParams(has_side_effects=_EFFECT),
    )(*held, started['send'], started['recv'], after)


def _push_start(name, arrays, gather):
    n_t = len(arrays)
    lands = [lax.empty((N_DEV,) + a.shape if whole else a.shape, a.dtype) for a, whole in zip(arrays, gather)]
    plan = lambda refs, send, recv, mine: _peer_copies(refs[:n_t], refs[n_t:], send, recv, gather, mine)
    return _split_start(name, list(arrays) + lands, plan, n_t * (N_DEV - 1))


def _push_wait(name, started, after):
    held = _split_wait(name, started, after)
    return held[:len(held) // 2], held[len(held) // 2:]


def _chip_copies(refs, send_sems, recv_sems, mine):
    n_t = len(refs) // 2
    x, y, c = _mesh_pos()
    my_slot = 4 * x + 2 * y + c
    peers = [(x, y, 1 - c), (1 - x, y, c), (x, 1 - y, c), (1 - x, 1 - y, c)]
    copies = []
    for j, peer in enumerate(peers):
        peer_slot = 4 * peer[0] + 2 * peer[1] + peer[2]
        for t in range(n_t):
            copies.append(pltpu.make_async_remote_copy(
                src_ref=refs[t], dst_ref=refs[n_t + t].at[my_slot if mine else peer_slot],
                send_sem=send_sems.at[4 * t + j], recv_sem=recv_sems.at[4 * t + j],
                device_id=peer, device_id_type=pl.DeviceIdType.MESH))
    return copies


def _forward_copies(refs, send_sems, recv_sems, mine):
    x, y, c = _mesh_pos()
    sibling = (x, y, 1 - c)
    copies = []
    for j, (px, py) in enumerate([(1 - x, y), (x, 1 - y), (1 - x, 1 - y)]):
        slot = 4 * px + 2 * py + (c if mine else 1 - c)
        for t, land in enumerate(refs):
            copies.append(pltpu.make_async_remote_copy(
                src_ref=land.at[slot], dst_ref=land.at[slot], send_sem=send_sems.at[3 * t + j], recv_sem=recv_sems.at[3 * t + j],
                device_id=sibling, device_id_type=pl.DeviceIdType.MESH))
    return copies


ADAM_BLOCK = 128 * 1024


def _adam_update(part, w_ref, m_ref, v_ref, g_ref, d_ref, nm_ref, nv_ref):
    g = part(0)
    for i in range(1, N_DEV):
        g = g + part(i)
    m_new = ADAM_B1 * m_ref[...] + (1.0 - ADAM_B1) * g
    v_new = ADAM_B2 * v_ref[...] + (1.0 - ADAM_B2) * (g * g)
    m_hat = m_new * (1.0 / (1.0 - ADAM_B1 ** ADAM_STEP))
    v_hat = v_new * (1.0 / (1.0 - ADAM_B2 ** ADAM_STEP))
    g_ref[...] = g
    nm_ref[...] = m_new
    nv_ref[...] = v_new
    d_ref[...] = -ADAM_LR * (m_hat / (jnp.sqrt(v_hat) + ADAM_EPS) + ADAM_WD * w_ref[...])


def _adam_many(name, items):
    n_t = len(items)
    n_in = [4 if own is None else 5 for *_, own in items]
    first = [sum(n_in[:t]) for t in range(n_t)]

    def body(*refs):
        ins, outs = refs[:sum(n_in)], refs[sum(n_in):]
        x, y, c = _mesh_pos()
        my_slot = 4 * x + 2 * y + c
        for t in range(n_t):
            s_ref, w_ref, m_ref, v_ref = ins[first[t]:first[t] + 4]
            if n_in[t] == 4:
                part = lambda i: s_ref[i]
            else:
                own_ref = ins[first[t] + 4]
                part = lambda i: jnp.where(my_slot == i, own_ref[...], s_ref[i])
            _adam_update(part, w_ref, m_ref, v_ref, *outs[4 * t:4 * t + 4])

    vmem = pl.BlockSpec(memory_space=pltpu.VMEM)
    out = pl.pallas_call(
        body, name=name, in_specs=[vmem] * sum(n_in), out_specs=[vmem] * (4 * n_t),
        out_shape=[jax.ShapeDtypeStruct(item[1].shape, F32) for item in items for _ in range(4)],
        compiler_params=pltpu.CompilerParams(vmem_limit_bytes=VMEM_LIMIT),
    )(*[a for item in items for a in item if a is not None])
    return [out[4 * t:4 * t + 4] for t in range(n_t)]


def _adam_call(name, slots, w, m, v, own=None):
    n_layer, rows, width = w.shape
    tile = min(rows, ADAM_BLOCK // max(width, 128))

    def body(*refs):
        s_refs = refs[:n_layer]
        own_refs = refs[n_layer:2 * n_layer] if own is not None else None
        w_ref, m_ref, v_ref = refs[-7:-4]
        layer = pl.program_id(0)
        x, y, c = _mesh_pos()
        my_slot = 4 * x + 2 * y + c
        for l in range(n_layer):
            @pl.when(layer == l)
            def _(l=l):
                if own is None:
                    part = lambda i: s_refs[l][i].astype(F32)
                else:
                    part = lambda i: jnp.where(my_slot == i, own_refs[l][...], s_refs[l][i]).astype(F32)
                _adam_update(part, w_ref, m_ref, v_ref, *refs[-4:])

    slot_spec = lambda k: _spec((N_DEV, tile, width), lambda l, i: (0, jnp.where(l == k, i, 0), 0))
    own_spec = lambda k: _spec((tile, width), lambda l, i: (jnp.where(l == k, i, 0), 0))
    blk = _spec((None, tile, width), lambda l, i: (l, i, 0))
    specs = [slot_spec(k) for k in range(n_layer)] + ([own_spec(k) for k in range(n_layer)] if own is not None else [])
    return pl.pallas_call(
        body, name=name, grid=(n_layer, rows // tile),
        in_specs=specs + [blk] * 3, out_specs=[blk] * 4, out_shape=[jax.ShapeDtypeStruct(w.shape, F32)] * 4,
        compiler_params=_cparams(2),
    )(*slots, *(own or []), w, m, v)


def kernel(x, ln_g, ln_b, rw_mu, rw_w0, rw_w1, rw_w2, rw_a0, rw_a1, rw_a2, rw_g1, rw_g2, rw_k_k, rw_k_a, rw_r_k, rw_wr, rw_wk, rw_wv, rw_wo, rw_lnx_g, rw_lnx_b, s5_a_re, s5_a_im, s5_log_dt, s5_b_re, s5_b_im, s5_c_re, s5_c_im, s5_d, s5_w_glu, mlp_w1, mlp_w2, loss_target, m_ln_g, m_ln_b, m_rw_mu, m_rw_w0, m_rw_w1, m_rw_w2, m_rw_a0, m_rw_a1, m_rw_a2, m_rw_g1, m_rw_g2, m_rw_k_k, m_rw_k_a, m_rw_r_k, m_rw_wr, m_rw_wk, m_rw_wv, m_rw_wo, m_rw_lnx_g, m_rw_lnx_b, m_s5_a_re, m_s5_a_im, m_s5_log_dt, m_s5_b_re, m_s5_b_im, m_s5_c_re, m_s5_c_im, m_s5_d, m_s5_w_glu, m_mlp_w1, m_mlp_w2, v_ln_g, v_ln_b, v_rw_mu, v_rw_w0, v_rw_w1, v_rw_w2, v_rw_a0, v_rw_a1, v_rw_a2, v_rw_g1, v_rw_g2, v_rw_k_k, v_rw_k_a, v_rw_r_k, v_rw_wr, v_rw_wk, v_rw_wv, v_rw_wo, v_rw_lnx_g, v_rw_lnx_b, v_s5_a_re, v_s5_a_im, v_s5_log_dt, v_s5_b_re, v_s5_b_im, v_s5_c_re, v_s5_c_im, v_s5_d, v_s5_w_glu, v_mlp_w1, v_mlp_w2):
    given = dict(locals())
    local_w = {n: given[n] for n in WEIGHTS}
    local_m = {n: given["m_" + n] for n in WEIGHTS}
    local_v = {n: given["v_" + n] for n in WEIGHTS}
    small = [n for n in WEIGHTS if n in SHARDED and n not in BIG]

    my_slot = 4 * lax.axis_index("x") + 2 * lax.axis_index("y") + lax.axis_index("c")
    as_bf16 = {n: local_w[n].astype(BF16) for n in BIG}

    early = [n for n in BIG if n not in LATE]
    gathered = _all_gather("weights_all_gather", [as_bf16[n] for n in early] + [local_w[n] for n in small])
    early_gathered = dict(zip(early, gathered))
    full = dict(local_w)
    for n, blk in zip(small, gathered[len(early):]):
        full[n] = jnp.moveaxis(blk, 0, SHARDED[n]).reshape(_full_shape(local_w[n].shape, SHARDED[n]))
    late_lands = [lax.empty((N_DEV,) + as_bf16[n].shape, BF16) for n in LATE]
    late = {'stage': _split_start("weights_late_start", [as_bf16[n] for n in LATE] + late_lands, _chip_copies, 4 * len(LATE))}
    full['rw_w0'] = full['rw_w0'] + late['stage']['zero']

    def late_forward(after):
        held = _split_wait("weights_late_wait", late['stage'], after)
        late['stage'] = _split_start("weights_forward_start", held[len(LATE):], _forward_copies, 3 * len(LATE))
        return late['stage']['zero']

    def late_weights(after):
        lands = _split_wait("weights_forward_wait", late['stage'], after)
        own = lambda n: as_bf16[n][None]
        return {n: lax.dynamic_update_slice(land, own(n), (my_slot,) + (0,) * as_bf16[n].ndim) for n, land in zip(LATE, lands)}

    exchanges = []

    def exchange(entries):
        started = _push_start(f"grads_start_{len(exchanges)}", [t for _, _, t in entries], [n in REPLICATED for n, _, _ in entries])
        exchanges.append((entries, started))
        return started['zero']

    loss_sq, dx, grads = _local_step(x[0], loss_target[0], full, early_gathered, late_forward, late_weights, exchange)
    loss = (0.5 / D_MODEL) * lax.psum(loss_sq, MESH_AXES)

    wide = {'s5_b_re': (SSM_LANES, SSM_GROUP), 's5_b_im': (SSM_LANES, SSM_GROUP),
            's5_c_re': (D_MODEL, SSM_STATE), 's5_c_im': (D_MODEL, SSM_STATE)}
    by_name, many = {}, []
    landed = {n: {} for n in BIG}
    for e, (entries, started) in enumerate(exchanges):
        sources, lands = _push_wait(f"grads_wait_{e}", started, dx)
        for (n, i, _), src, slots in zip(entries, sources, lands):
            if n in wide:
                as_view = lambda t: t.reshape((1,) + wide[n])
                out = _adam_call(f"adamw_{n}", [slots.reshape((N_DEV,) + wide[n])], as_view(local_w[n]), as_view(local_m[n]),
                                 as_view(local_v[n]), own=[src.reshape(wide[n])])
                by_name[n] = [t.reshape(local_w[n].shape) for t in out]
            elif n in REPLICATED:
                many.append((n, (slots, local_w[n], local_m[n], local_v[n], src)))
            else:
                landed[n][i] = (slots, lax.dynamic_index_in_dim(src, my_slot, 0, keepdims=False))
    for n in BIG:
        layers = [landed[n][i] for i in sorted(landed[n])]
        by_name[n] = _adam_call(f"adamw_{n}", [s for s, _ in layers], local_w[n], local_m[n], local_v[n], own=[o for _, o in layers])
    split = lambda n: jnp.moveaxis(grads[n].reshape(_split_shape(local_w[n].shape, SHARDED[n])), SHARDED[n], 0)
    for n, slots in zip(small, _all_to_all("grads_all_to_all", [split(n) for n in small])):
        many.append((n, (slots, local_w[n], local_m[n], local_v[n], None)))
    last = [n for n in REPLICATED if n not in by_name and n not in dict(many)]
    for n, slots in zip(last, _all_gather("grads_all_gather", [grads[n] for n in last])):
        many.append((n, (slots, local_w[n], local_m[n], local_v[n], None)))
    for (n, _), out in zip(many, _adam_many("adamw_small", [item for _, item in many])):
        by_name[n] = out
    results = [by_name[n][k] for k in range(4) for n in WEIGHTS]
    return (loss, dx[None], *results)
```

```python
import functools
import math

import jax
import jax.numpy as jnp
from jax import lax
from jax.experimental import pallas as pl
from jax.experimental.pallas import tpu as pltpu

F32 = jnp.float32
BF16 = jnp.bfloat16

D_MODEL = 1024
HEAD = 64
PAIR = 2 * HEAD
N_PAIR = D_MODEL // PAIR
CHUNK = 64
GN_EPS = 64e-5
LN_EPS = 1e-5
SSM_GROUP = 16
SSM_STATE = 64
SSM_GROUPS = D_MODEL // SSM_GROUP
SSM_LANES = SSM_GROUPS * SSM_STATE
SSM_BLK_IN = 128
SSM_BLK_ST = 512
N_SSM_BLK = D_MODEL // SSM_BLK_IN
DEPTH = 2
DN_ALPHA = (2.0 * DEPTH) ** 0.25
ADAM_LR, ADAM_B1, ADAM_B2, ADAM_EPS, ADAM_WD, ADAM_STEP = 0.001, 0.9, 0.999, 1e-08, 0.01, 10
N_DEV = 8
MESH_AXES = ("x", "y", "c")
VMEM_LIMIT = 56 * 1024 * 1024
TOKEN_TILE = 256
LN_TILE = 512

SHARDED = {
    "rw_mu": 2, "rw_w1": 1, "rw_w2": 2, "rw_a1": 1, "rw_a2": 2, "rw_g1": 1, "rw_g2": 2,
    "rw_wr": 1, "rw_wk": 1, "rw_wv": 1, "rw_wo": 1, "s5_d": 1, "s5_w_glu": 2, "mlp_w1": 2, "mlp_w2": 1,
}
WEIGHTS = ['ln_g', 'ln_b', 'rw_mu', 'rw_w0', 'rw_w1', 'rw_w2', 'rw_a0', 'rw_a1', 'rw_a2', 'rw_g1', 'rw_g2', 'rw_k_k',
           'rw_k_a', 'rw_r_k', 'rw_wr', 'rw_wk', 'rw_wv', 'rw_wo', 'rw_lnx_g', 'rw_lnx_b', 's5_a_re', 's5_a_im',
           's5_log_dt', 's5_b_re', 's5_b_im', 's5_c_re', 's5_c_im', 's5_d', 's5_w_glu', 'mlp_w1', 'mlp_w2']
REPLICATED = [n for n in WEIGHTS if n not in SHARDED]
BIG = ['rw_wr', 'rw_wk', 'rw_wv', 'rw_wo', 's5_w_glu', 'mlp_w1', 'mlp_w2']
MID = ['rw_wr', 'rw_wk', 'rw_wv']
LATE = ['rw_wo', 's5_w_glu', 'mlp_w1', 'mlp_w2']
GRAD_DTYPE = BF16


def _cparams(n_grid):
    return pltpu.CompilerParams(dimension_semantics=("arbitrary",) * n_grid, vmem_limit_bytes=VMEM_LIMIT)


@jax.custom_vjp
def _mm(x, w):
    return jnp.dot(x.astype(BF16), w.astype(BF16), preferred_element_type=F32)


def _mm_fwd(x, w):
    return _mm(x, w), (x, w)


def _mm_bwd(res, dy):
    x, w = res
    dyb = dy.astype(BF16)
    dx = lax.dot_general(dyb, w.astype(BF16), (((1,), (1,)), ((), ())), preferred_element_type=F32)
    dw = lax.dot_general(x.astype(BF16), dyb, (((0,), (0,)), ((), ())), preferred_element_type=F32)
    return dx, dw


_mm.defvjp(_mm_fwd, _mm_bwd)


_DOT_DIMS = {'nn': (((1,), (0,)), ((), ())), 'nt': (((1,), (1,)), ((), ())), 'tn': (((0,), (0,)), ((), ()))}
_BATCH_DOT_DIMS = {'nn': (((2,), (1,)), ((0,), (0,))), 'nt': (((2,), (2,)), ((0,), (0,))), 'tn': (((1,), (1,)), ((0,), (0,)))}
CHUNK_PASSES = 1


def _split_bf16(a):
    hi = a.astype(BF16)
    return hi, (a - hi.astype(F32)).astype(BF16)


def _dot_passes(a, b, mode, passes=None):
    dims = (_DOT_DIMS if a.ndim == 2 else _BATCH_DOT_DIMS)[mode]
    dot = lambda p, q: lax.dot_general(p, q, dims, preferred_element_type=F32)
    if (CHUNK_PASSES if passes is None else passes) == 1:
        return dot(a.astype(BF16), b.astype(BF16))
    (ah, al), (bh, bl) = _split_bf16(a), _split_bf16(b)
    return dot(ah, bh) + (dot(ah, bl) + dot(al, bh))


@functools.partial(jax.custom_vjp, nondiff_argnums=(2,))
def _pdot(a, b, mode):
    return _dot_passes(a, b, mode)


def _pdot_fwd(a, b, mode):
    return _dot_passes(a, b, mode), (a, b)


def _pdot_bwd(mode, res, dy):
    a, b = res
    if mode == 'nn':
        return _dot_passes(dy, b, 'nt'), _dot_passes(a, dy, 'tn')
    if mode == 'nt':
        return _dot_passes(dy, b, 'nn'), _dot_passes(dy, a, 'tn')
    return _dot_passes(b, dy, 'nt'), _dot_passes(a, dy, 'nn')


_pdot.defvjp(_pdot_fwd, _pdot_bwd)


def _tri_sum(x, upper):
    nb, n = x.shape[0], x.shape[1]
    ti = lax.broadcasted_iota(jnp.int32, (nb, n, n), 1)
    tj = lax.broadcasted_iota(jnp.int32, (nb, n, n), 2)
    tri = ((ti <= tj) if upper else (ti >= tj)).astype(BF16)
    hi = x.astype(BF16)
    rest = x - hi.astype(F32)
    mid = rest.astype(BF16)
    lo = (rest - mid.astype(F32)).astype(BF16)
    dot = lambda q: lax.dot_general(tri, q, _BATCH_DOT_DIMS['nn'], preferred_element_type=F32)
    return dot(hi) + (dot(mid) + dot(lo))


@jax.custom_vjp
def _cumsum_rows(x):
    return _tri_sum(x, False)


_cumsum_rows.defvjp(lambda x: (_tri_sum(x, False), None), lambda _, dy: (_tri_sum(dy, True),))


def _power_sum(a, order):
    m = a.shape[-1]
    ti = lax.broadcasted_iota(jnp.int32, (1, m, m), 1)
    tj = lax.broadcasted_iota(jnp.int32, (1, m, m), 2)
    pw, total = _dot_passes(a, a, 'nn'), (ti == tj).astype(F32) + a
    for _ in range(int(math.log2(order)) - 2):
        both = _dot_passes(pw, jnp.concatenate([pw, total], axis=2), 'nn')
        pw, total = both[:, :, :m], total + both[:, :, m:]
    return total + _dot_passes(pw, total, 'nn')


@functools.partial(jax.custom_vjp, nondiff_argnums=(1,))
def _nilpotent_inverse(a, order):
    return _power_sum(a, order)


def _nilpotent_inverse_fwd(a, order):
    inv = _power_sum(a, order)
    return inv, inv


def _nilpotent_inverse_bwd(order, inv, g):
    return (_dot_passes(_dot_passes(inv, g, 'tn'), inv, 'nt'),)


_nilpotent_inverse.defvjp(_nilpotent_inverse_fwd, _nilpotent_inverse_bwd)


@jax.custom_vjp
def _known_inverse(a, inv):
    return inv


_known_inverse.defvjp(lambda a, inv: (inv, inv),
                      lambda inv, g: (_dot_passes(_dot_passes(inv, g, 'tn'), inv, 'nt'), jnp.zeros_like(inv)))


@functools.partial(jax.custom_vjp, nondiff_argnums=(1,))
def _roll_rows(x, shift):
    return pltpu.roll(x, shift, 0)


def _roll_rows_fwd(x, shift):
    return pltpu.roll(x, shift, 0), None


def _roll_rows_bwd(shift, _, dy):
    n = dy.shape[0]
    return (pltpu.roll(dy, (n - shift) % n, 0),)


_roll_rows.defvjp(_roll_rows_fwd, _roll_rows_bwd)


def _shift_down(x, first_row):
    row = lax.broadcasted_iota(jnp.int32, x.shape, 0)
    return jnp.where(row == 0, first_row, _roll_rows(x, 1))


def _sigmoid(x):
    return 1.0 / (1.0 + jnp.exp(-x))


def _softplus(x):
    return jnp.maximum(x, 0.0) + jnp.log(1.0 + jnp.exp(-jnp.abs(x)))


def _gelu(x):
    return 0.5 * x * (1.0 + jnp.tanh(math.sqrt(2.0 / math.pi) * (x + 0.044715 * (x * x * x))))


def _layer_norm(x, g, b):
    mu = jnp.mean(x, axis=-1, keepdims=True)
    xc = x - mu
    var = jnp.mean(xc * xc, axis=-1, keepdims=True)
    return xc * lax.rsqrt(var + LN_EPS) * g + b


def _spec(block, index_map):
    return pl.BlockSpec(block, index_map)


def _tile(arr, tm):
    return (arr, (tm, arr.shape[1]), lambda i: (i, 0))


def _full(arr):
    nd = arr.ndim
    return (arr, arr.shape, lambda *_: (0,) * nd)


def _fwd_call(name, fn, grid, prims, outs):
    n_in = len(prims)

    def body(*refs):
        vals = [r[...] for r in refs[:n_in]]
        vals = [v.astype(F32) if v.dtype != F32 else v for v in vals]
        res = fn(*vals)
        for o, r in zip(refs[n_in:], res):
            o[...] = r.astype(o.dtype)

    return pl.pallas_call(
        body, name=name, grid=grid,
        in_specs=[_spec(b, m) for _, b, m in prims],
        out_specs=[_spec(b, m) for _, _, b, m in outs],
        out_shape=[jax.ShapeDtypeStruct(s, d) for s, d, _, _ in outs],
        compiler_params=_cparams(len(grid)),
    )(*[a for a, _, _ in prims])


def _bwd_call(name, fn, grid, prims, cts, wants, addends=None):
    addends = addends or {}
    n_in, n_ct = len(prims), len(cts)
    out_place = {}
    wants, out_dtype = list(wants), {}
    for i, w in enumerate(wants):
        if isinstance(w, tuple):
            wants[i], out_place[i] = w
        elif w == 'tile_bf16':
            wants[i], out_dtype[i] = 'tile', BF16
    diff = [i for i, w in enumerate(wants) if w]
    add_idx = [i for i in diff if i in addends]
    last_axis = len(grid) - 1

    def body(*refs):
        vals = [r[...] for r in refs[:n_in]]
        vals = [v.astype(F32) if v.dtype != F32 else v for v in vals]
        ct_vals = [r[...] for r in refs[n_in:n_in + n_ct]]
        ct_vals = [v.astype(F32) if v.dtype != F32 else v for v in ct_vals]
        add_refs = dict(zip(add_idx, refs[n_in + n_ct:n_in + n_ct + len(add_idx)]))
        out_refs = refs[n_in + n_ct + len(add_idx):]

        def f(*dargs):
            full = list(vals)
            for i, a in zip(diff, dargs):
                full[i] = a
            return tuple(fn(*full))

        _, vjp = jax.vjp(f, *[vals[i] for i in diff])
        grads = vjp(tuple(ct_vals))
        first = pl.program_id(last_axis) == 0
        for o, g, i in zip(out_refs, grads, diff):
            if wants[i] == 'tile':
                if i in add_refs:
                    g = g + add_refs[i][...]
                o[...] = g.astype(o.dtype)
            else:
                @pl.when(first)
                def _(o=o):
                    o[...] = jnp.zeros(o.shape, o.dtype)
                o[...] += g

    ins = list(prims) + list(cts) + [addends[i] for i in add_idx]
    places = [out_place.get(i, (prims[i][0].shape, prims[i][1], prims[i][2])) for i in diff]
    return pl.pallas_call(
        body, name=name, grid=grid,
        in_specs=[_spec(b, m) for _, b, m in ins],
        out_specs=[_spec(b, m) for _, b, m in places],
        out_shape=[jax.ShapeDtypeStruct(s, out_dtype.get(i, F32)) for i, (s, _, _) in zip(diff, places)],
        compiler_params=_cparams(len(grid)),
    )(*[a for a, _, _ in ins])


def _matmul(name, a, b, mode, *, out_dtype=F32, lhs_fn=None, epi=None, extra=None, tm=1024, tn=1024, tk=1024,
            b_cols=None, b_rows=None, out_cols=False):
    if mode == 'tn':
        kdim, m = a.shape
    else:
        m, kdim = a.shape
    if b_rows is not None:
        loc, cols = b.shape[-2], b.shape[-1]
        if mode == 'nn':
            n, k_shards = cols, max(1, tk // loc)
            tk = k_shards * loc
        else:
            n, tn = N_DEV * loc, loc
    elif b_cols is None:
        n = b.shape[0] if mode == 'nt' else b.shape[1]
    else:
        rows, loc = b.shape[-2], b.shape[-1]
        if mode == 'nn':
            n, n_shards = N_DEV * loc, max(1, tn // loc)
            tn = n_shards * loc
        else:
            n, k_shards = rows, max(1, tk // loc)
            tk = k_shards * loc
    if out_cols:
        o_shards = max(1, min(tn, n) // (n // N_DEV))
        tn = o_shards * (n // N_DEV)
    tm, tn, tk = min(tm, m), min(tn, n), min(tk, kdim)
    nk = kdim // tk
    grid = (m // tm, n // tn, nk)
    a_spec = _spec((tk, tm), lambda i, j, k: (k, i)) if mode == 'tn' else _spec((tm, tk), lambda i, j, k: (i, k))
    if b_rows is not None:
        lead = tuple(b_rows)
        skip = (None,) * (1 + len(lead))
        if mode == 'nn':
            b_spec = _spec((k_shards,) + skip[1:] + (tk // k_shards, tn), lambda i, j, k: (k,) + lead + (0, j))
        else:
            b_spec = _spec(skip + (tn, tk), lambda i, j, k: (j,) + lead + (0, k))
    elif b_cols is None:
        b_spec = _spec((tn, tk), lambda i, j, k: (j, k)) if mode == 'nt' else _spec((tk, tn), lambda i, j, k: (k, j))
    else:
        lead = tuple(b_cols)
        skip = (None,) * (1 + len(lead))
        if mode == 'nn':
            b_spec = _spec((n_shards,) + skip[1:] + (tk, tn // n_shards), lambda i, j, k: (j,) + lead + (k, 0))
        else:
            b_spec = _spec((k_shards,) + skip[1:] + (tn, tk // k_shards), lambda i, j, k: (k,) + lead + (j, 0))
    o_spec = _spec((tm, tn), lambda i, j, k: (i, j))
    if out_cols:
        o_place, o_shape = _spec((o_shards, tm, tn // o_shards), lambda i, j, k: (j, i, 0)), (N_DEV, m, n // N_DEV)
    else:
        o_place, o_shape = o_spec, (m, n)
    dims = _DOT_DIMS[mode]

    def body(*refs):
        if extra is None:
            a_ref, b_ref, o_ref, acc = refs
            x_ref = None
        else:
            a_ref, b_ref, x_ref, o_ref, acc = refs
        k = pl.program_id(2)
        av = a_ref[...]
        if lhs_fn is not None:
            av = lhs_fn(av.astype(F32))
        bv = b_ref[...]
        if bv.ndim == 3:
            bv = bv.reshape(tk, tn) if b_rows is not None else jnp.concatenate([bv[g] for g in range(bv.shape[0])], axis=1)
        part = lax.dot_general(av.astype(BF16), bv.astype(BF16), dims, preferred_element_type=F32)

        def finish(r):
            if epi is not None:
                r = epi(r, x_ref[...])
            if out_cols:
                width = tn // o_shards
                for g in range(o_shards):
                    o_ref[g] = r[:, g * width:(g + 1) * width].astype(o_ref.dtype)
            else:
                o_ref[...] = r.astype(o_ref.dtype)

        if nk == 1:
            finish(part)
            return

        @pl.when(k == 0)
        def _():
            acc[...] = part

        @pl.when((k > 0) & (k < nk - 1))
        def _():
            acc[...] += part

        @pl.when(k == nk - 1)
        def _():
            finish(acc[...] + part)

    ops, specs = [a, b], [a_spec, b_spec]
    if extra is not None:
        ops.append(extra)
        specs.append(o_spec)
    return pl.pallas_call(
        body, name=name, grid=grid, in_specs=specs, out_specs=o_place,
        out_shape=jax.ShapeDtypeStruct(o_shape, out_dtype),
        scratch_shapes=[pltpu.VMEM((tm, tn), F32)],
        compiler_params=_cparams(3),
    )(*ops)


def _relu2(x):
    r = jnp.maximum(x, 0.0)
    return r * r


def _mix_fn(x, halo, mu, w0, w1, w2, a0, a1, a2, g1, g2):
    prev_row = jnp.where(pl.program_id(0) == 0, 0.0, halo[7:8, :])
    xx = _shift_down(x, prev_row) - x
    xr, xw, xk, xv, xa, xg = (x + xx * mu[i:i + 1, :] for i in range(6))
    w_pre = w0 + _mm(jnp.tanh(_mm(xw, w1)), w2)
    log_decay = -jnp.exp(-_softplus(-w_pre) - 0.5)
    a = _sigmoid(a0 + _mm(_mm(xa, a1), a2))
    g = _mm(_sigmoid(_mm(xg, g1)), g2)
    return xr, xk, xv, log_decay, a, g


def _chunk_fn(r, lw, k, v, a, k_k, k_a, r_k, inverse=None, with_inverse=False):
    n, n_pair = r.shape[0], r.shape[1] // PAIR
    to_batch = lambda t: jnp.concatenate([t[None, :, p * PAIR:(p + 1) * PAIR] for p in range(n_pair)], axis=0)
    from_batch = lambda t: jnp.concatenate([t[p] for p in range(n_pair)], axis=1)
    r, lw, k, v, a, k_k, k_a, r_k = (to_batch(t) for t in (r, lw, k, v, a, k_k, k_a, r_k))
    lane = lax.broadcasted_iota(jnp.int32, (1, 1, PAIR), 2)
    m0 = lane < HEAD

    def head_sum(t):
        s0 = jnp.sum(jnp.where(m0, t, 0.0), axis=2, keepdims=True)
        s1 = jnp.sum(jnp.where(m0, 0.0, t), axis=2, keepdims=True)
        return jnp.where(m0, s0, s1)

    kk = k * k_k
    kk = kk / jnp.maximum(jnp.sqrt(head_sum(kk * kk)), 1e-12)
    k2 = k * (1.0 + (a - 1.0) * k_a)
    aa, bb = -kk, kk * a
    bonus = head_sum(r * k2 * r_k) * v

    cum = _cumsum_rows(lw)
    p_in, p_ex, p_inv = jnp.exp(cum), jnp.exp(cum - lw), jnp.exp(-cum)
    at, rt, kt, bt = aa * p_ex, r * p_in, k2 * p_inv, bb * p_inv

    def stack_masked(t):
        return jnp.concatenate([jnp.where(m0, t, 0.0), jnp.where(m0, 0.0, t)], axis=1)

    def unstack_sum(t):
        return t[:, :n] + t[:, n:]

    def unstack_select(t):
        return jnp.where(m0, t[:, :n], t[:, n:])

    ti = lax.broadcasted_iota(jnp.int32, (1, 2 * n, 2 * n), 1)
    tj = lax.broadcasted_iota(jnp.int32, (1, 2 * n, 2 * n), 2)
    same = (ti >= n) == (tj >= n)
    incl, strict = same & (ti >= tj), same & (ti > tj)
    m = 2 * n
    rows = lambda *ts: jnp.concatenate(ts, axis=1)
    cols = lambda *ts: jnp.concatenate(ts, axis=2)
    at_s, rt_s, kt_s, bt_s = stack_masked(at), stack_masked(rt), stack_masked(kt), stack_masked(bt)
    v_s, zeros = rows(v, v), jnp.zeros_like(at_s)
    bk_s = rows(bt_s, kt_s)
    gram = _pdot(rows(at_s, rt_s), bk_s, 'nt')
    a_ab = jnp.where(strict, gram[:, :m, :m], 0.0)
    a_ak = jnp.where(strict, gram[:, :m, m:], 0.0)
    a_rb = jnp.where(incl, gram[:, m:, :m], 0.0)
    a_rk = jnp.where(incl, gram[:, m:, m:], 0.0)
    inv = _nilpotent_inverse(a_ab, n) if inverse is None else _known_inverse(a_ab, inverse)
    wy_s = _pdot(inv, cols(at_s, _pdot(a_ak, v_s, 'nn')), 'nn')
    q_oi = _pdot(cols(a_rb, a_rk), rows(wy_s, cols(zeros, v_s)), 'nn')
    q_all = unstack_sum(rt_s + q_oi[:, :, :PAIR])
    oi_all = unstack_select(q_oi[:, :, PAIR:])
    own_head = (lax.broadcasted_iota(jnp.int32, (1, m, PAIR), 1) >= n) == (lax.broadcasted_iota(jnp.int32, (1, m, PAIR), 2) >= HEAD)
    wy_m = cols(wy_s[:, :, :PAIR], jnp.where(own_head, wy_s[:, :, PAIR:], 0.0))
    mm_nn = _pdot(bk_s, rows(wy_m, cols(zeros, stack_masked(v))), 'tn')
    mm, nn = mm_nn[:, :, :PAIR], mm_nn[:, :, PAIR:]
    ei = lax.broadcasted_iota(jnp.int32, (1, PAIR, PAIR), 1)
    ej = lax.broadcasted_iota(jnp.int32, (1, PAIR, PAIR), 2)
    eye_p = (ei == ej).astype(F32)
    decay_col = jnp.sum(eye_p * p_in[:, n - 1:n, :], axis=2, keepdims=True)
    maps = ((decay_col * (eye_p + mm))[None], (decay_col * nn)[None], from_batch(q_all), from_batch(oi_all),
            from_batch(bonus))
    return maps + (inv,) if with_inverse else maps


def _gate_fn(o, bonus, g, lnx_g, lnx_b):
    lane = lax.broadcasted_iota(jnp.int32, (1, PAIR), 1)
    m0 = lane < HEAD

    def head_mean(t):
        s0 = jnp.sum(jnp.where(m0, t, 0.0), axis=1, keepdims=True)
        s1 = jnp.sum(jnp.where(m0, 0.0, t), axis=1, keepdims=True)
        return jnp.where(m0, s0, s1) * (1.0 / HEAD)

    outs = []
    for p in range(N_PAIR):
        sl = slice(p * PAIR, (p + 1) * PAIR)
        oc = o[:, sl] - head_mean(o[:, sl])
        on = oc * lax.rsqrt(head_mean(oc * oc) + GN_EPS) * lnx_g[:, sl] + lnx_b[:, sl]
        outs.append((on + bonus[:, sl]) * g[:, sl])
    return (jnp.concatenate(outs, axis=1),)


def _res_ln_fn(h, y, g, b):
    return (_layer_norm(DN_ALPHA * h + y, g, b),)


def _glu_ln_fn(h, z, g, b):
    mix = z[:, :D_MODEL] * _sigmoid(z[:, D_MODEL:])
    return (_layer_norm(DN_ALPHA * h + mix, g, b),)


def _s5_param_fn(a_re, a_im, log_dt, b_re, b_im):
    dt = jnp.exp(log_dt)
    lam_re = jnp.minimum(a_re, -1e-4)
    lam_im = a_im
    mag = jnp.exp(dt * lam_re)
    abar_re = mag * jnp.cos(dt * lam_im)
    abar_im = mag * jnp.sin(dt * lam_im)
    den = lam_re * lam_re + lam_im * lam_im
    nr, ni = abar_re - 1.0, abar_im
    coef_re = (nr * lam_re + ni * lam_im) / den
    coef_im = (ni * lam_re - nr * lam_im) / den
    return abar_re, abar_im, coef_re, coef_im


def _s5_bbar_fn(coef_re, coef_im, b_re, b_im):
    return coef_re * b_re - coef_im * b_im, coef_re * b_im + coef_im * b_re


def _s5_in_fn(u, b_re, b_im):
    return _pdot(u, b_re[0], 'nn'), _pdot(u, b_im[0], 'nn')


def _s5_out_fn(s_re, s_im, u, c_re, c_im, d):
    y = _pdot(s_re, c_re[0], 'nn') - _pdot(s_im, c_im[0], 'nn') + u * d
    return (_gelu(y),)


def _pairs_to_batch(t):
    return jnp.concatenate([t[None, :, p * PAIR:(p + 1) * PAIR] for p in range(N_PAIR)], axis=0)


def _batch_to_pairs(t):
    return jnp.concatenate([t[p] for p in range(N_PAIR)], axis=1)


STATE_PASSES = 3


def _rwkv_scan_fwd(r, lw, k, v, a, k_k, k_a, r_k):
    n_tok = r.shape[0]
    n_chunk = n_tok // CHUNK

    def body(r_ref, lw_ref, k_ref, v_ref, a_ref, kk_ref, ka_ref, rk_ref, o_ref, bonus_ref, zs_ref, inv_ref, m_ref, q_ref, z):
        @pl.when(pl.program_id(0) == 0)
        def _():
            z[...] = jnp.zeros(z.shape, F32)

        m_all, n_all, q, oi, bonus, inv = _chunk_fn(r_ref[...], lw_ref[...], k_ref[...], v_ref[...], a_ref[...],
                                                    kk_ref[...], ka_ref[...], rk_ref[...], with_inverse=True)
        zb = z[...]
        zs_ref[0] = zb
        inv_ref[0] = inv.astype(inv_ref.dtype)
        m_ref[...] = m_all
        q_ref[...] = q
        bonus_ref[...] = bonus
        o_ref[...] = _batch_to_pairs(_dot_passes(_pairs_to_batch(q), zb, 'nn', STATE_PASSES)) + oi
        z[...] = _dot_passes(m_all[0], zb, 'nn', STATE_PASSES) + n_all[0]

    tok = _spec((CHUNK, D_MODEL), lambda c: (c, 0))
    par = _spec((1, D_MODEL), lambda c: (0, 0))
    st = _spec((1, N_PAIR, PAIR, PAIR), lambda c: (c, 0, 0, 0))
    return pl.pallas_call(
        body, name="rwkv_scan_fwd", grid=(n_chunk,),
        in_specs=[tok] * 5 + [par] * 3, out_specs=[tok, tok, st, st, st, tok],
        out_shape=[jax.ShapeDtypeStruct((n_tok, D_MODEL), F32)] * 2
        + [jax.ShapeDtypeStruct((n_chunk, N_PAIR, PAIR, PAIR), F32), jax.ShapeDtypeStruct((n_chunk, N_PAIR, PAIR, PAIR), BF16),
           jax.ShapeDtypeStruct((n_chunk, N_PAIR, PAIR, PAIR), F32), jax.ShapeDtypeStruct((n_tok, D_MODEL), F32)],
        scratch_shapes=[pltpu.VMEM((N_PAIR, PAIR, PAIR), F32)],
        compiler_params=_cparams(1),
    )(r, lw, k, v, a, k_k, k_a, r_k)


def _rwkv_scan_bwd(r, lw, k, v, a, k_k, k_a, r_k, zs, invs, ms, q, d_o, d_bonus):
    n_tok = r.shape[0]
    n_chunk = n_tok // CHUNK

    def body(r_ref, lw_ref, k_ref, v_ref, a_ref, kk_ref, ka_ref, rk_ref, zs_ref, inv_ref, m_ref, q_ref, do_ref, db_ref,
             dr_ref, dlw_ref, dk_ref, dv_ref, da_ref, dkk_ref, dka_ref, drk_ref, g):
        sums = (dkk_ref, dka_ref, drk_ref)

        @pl.when(pl.program_id(0) == 0)
        def _():
            g[...] = jnp.zeros(g.shape, F32)
            for s_ref in sums:
                s_ref[...] = jnp.zeros(s_ref.shape, F32)

        prim = (r_ref[...], lw_ref[...], k_ref[...], v_ref[...], a_ref[...], kk_ref[...], ka_ref[...], rk_ref[...])
        known = inv_ref[0].astype(F32)
        _, chunk_vjp = jax.vjp(lambda *p: _chunk_fn(*p, inverse=known), *prim)
        m_all, q = m_ref[...], q_ref[...]
        gb, zb, d_out = g[...], zs_ref[0], do_ref[...]
        dob = _pairs_to_batch(d_out)
        d_m = _dot_passes(gb, zb, 'nt', STATE_PASSES)
        d_q = _batch_to_pairs(_dot_passes(dob, zb, 'nt', STATE_PASSES))
        grads = chunk_vjp((d_m[None], gb[None], d_q, d_out, db_ref[...]))
        for out_ref, val in zip((dr_ref, dlw_ref, dk_ref, dv_ref, da_ref), grads[:5]):
            out_ref[...] = val.astype(out_ref.dtype)
        for s_ref, val in zip(sums, grads[5:]):
            s_ref[...] += val
        same_head = (lax.broadcasted_iota(jnp.int32, (1, PAIR, PAIR), 1) // HEAD
                     == lax.broadcasted_iota(jnp.int32, (1, PAIR, PAIR), 2) // HEAD)
        g[...] = (_dot_passes(m_all[0], gb, 'tn', STATE_PASSES)
                  + jnp.where(same_head, _dot_passes(_pairs_to_batch(q), dob, 'tn', STATE_PASSES), 0.0))

    tok = _spec((CHUNK, D_MODEL), lambda c: (n_chunk - 1 - c, 0))
    par = _spec((1, D_MODEL), lambda c: (0, 0))
    st = _spec((1, N_PAIR, PAIR, PAIR), lambda c: (n_chunk - 1 - c, 0, 0, 0))
    tok_shape = lambda dt: jax.ShapeDtypeStruct((n_tok, D_MODEL), dt)
    return pl.pallas_call(
        body, name="rwkv_scan_bwd", grid=(n_chunk,),
        in_specs=[tok] * 5 + [par] * 3 + [st, st, st, tok, tok, tok], out_specs=[tok] * 5 + [par] * 3,
        out_shape=[tok_shape(BF16), tok_shape(F32), tok_shape(BF16), tok_shape(BF16), tok_shape(F32)]
        + [jax.ShapeDtypeStruct((1, D_MODEL), F32)] * 3,
        scratch_shapes=[pltpu.VMEM((N_PAIR, PAIR, PAIR), F32)],
        compiler_params=_cparams(1),
    )(r, lw, k, v, a, k_k, k_a, r_k, zs, invs, ms, q, d_o, d_bonus)


S5_TIME_TILE = 1024
S5_GROUP = 8


def _scan_rows(re_ref, im_ref, a1, carry, reverse):
    lb = re_ref.shape[1]
    grp, n_grp = S5_GROUP, re_ref.shape[0] // S5_GROUP

    def cmul(xr, xi, yr, yi):
        return xr * yr - xi * yi, xr * yi + xi * yr

    a2 = cmul(*a1, *a1)
    a4 = cmul(*a2, *a2)
    a8 = cmul(*a4, *a4)
    row = lax.broadcasted_iota(jnp.int32, (grp, lb), 0)
    expo = (grp - row) if reverse else (row + 1)
    pw = (jnp.ones((grp, lb), F32), jnp.zeros((grp, lb), F32))
    for bit, ap in ((1, a1), (2, a2), (4, a4), (8, a8)):
        nxt = cmul(*pw, *ap)
        sel = (expo & bit) != 0
        pw = (jnp.where(sel, nxt[0], pw[0]), jnp.where(sel, nxt[1], pw[1]))

    steps = []
    for d, ad in ((1, a1), (2, a2), (4, a4)):
        keep = (row < grp - d) if reverse else (row >= d)
        steps.append(((grp - d) if reverse else d, jnp.where(keep, ad[0], 0.0), jnp.where(keep, ad[1], 0.0)))

    def group(i, c):
        gi = (n_grp - 1 - i) if reverse else i
        rows = pl.ds(pl.multiple_of(gi * grp, grp), grp)
        xr, xi = re_ref[rows, :], im_ref[rows, :]
        for shift, mr, mi in steps:
            pr, pi = cmul(mr, mi, pltpu.roll(xr, shift, 0), pltpu.roll(xi, shift, 0))
            xr, xi = xr + pr, xi + pi
        cr, ci = cmul(*pw, c[0], c[1])
        xr, xi = xr + cr, xi + ci
        re_ref[rows, :] = xr
        im_ref[rows, :] = xi
        edge = slice(0, 1) if reverse else slice(grp - 1, grp)
        return xr[edge, :], xi[edge, :]

    return lax.fori_loop(0, n_grp, group, carry)


def _s5_fwd(u, b_re, b_im, a_re, a_im, c_re, c_im, d_row):
    n_tok = u.shape[0]
    tt = min(S5_TIME_TILE, n_tok)

    def body(u_ref, bre_ref, bim_ref, are_ref, aim_ref, cre_ref, cim_ref, d_ref, sre_ref, sim_ref, yg_ref, carry):
        @pl.when(pl.program_id(1) == 0)
        def _():
            carry[...] = jnp.zeros(carry.shape, F32)

        uv = u_ref[...]
        sre_ref[...], sim_ref[...] = _s5_in_fn(uv, bre_ref[...], bim_ref[...])
        c = _scan_rows(sre_ref, sim_ref, (are_ref[...], aim_ref[...]), (carry[0:1, :], carry[1:2, :]), False)
        carry[0:1, :] = c[0]
        carry[1:2, :] = c[1]
        (yg,) = _s5_out_fn(sre_ref[...], sim_ref[...], uv, cre_ref[...], cim_ref[...], d_ref[...])
        yg_ref[...] = yg.astype(yg_ref.dtype)

    u_blk = _spec((tt, SSM_BLK_IN), lambda l, t: (t, l))
    s_blk = _spec((tt, SSM_BLK_ST), lambda l, t: (t, l))
    blk3 = lambda arr: _spec((1,) + arr.shape[1:], lambda l, t: (l, 0, 0))
    par = lambda width: _spec((1, width), lambda l, t: (0, l))
    return pl.pallas_call(
        body, name="s5_fwd", grid=(N_SSM_BLK, n_tok // tt),
        in_specs=[u_blk, blk3(b_re), blk3(b_im), par(SSM_BLK_ST), par(SSM_BLK_ST), blk3(c_re), blk3(c_im), par(SSM_BLK_IN)],
        out_specs=[s_blk, s_blk, u_blk],
        out_shape=[jax.ShapeDtypeStruct((n_tok, SSM_LANES), F32)] * 2 + [jax.ShapeDtypeStruct((n_tok, D_MODEL), BF16)],
        scratch_shapes=[pltpu.VMEM((2, SSM_BLK_ST), F32)],
        compiler_params=_cparams(2),
    )(u, b_re, b_im, a_re, a_im, c_re, c_im, d_row)


def _s5_bwd(dyg, u, s_re, s_im, dh_res, b_re, b_im, a_re, a_im, c_re, c_im, d_row):
    n_tok = u.shape[0]
    tt = min(S5_TIME_TILE, n_tok)
    n_t = n_tok // tt

    def body(dyg_ref, u_ref, sre_ref, sim_ref, hre_ref, him_ref, res_ref, bre_ref, bim_ref, are_ref, aim_ref,
             cre_ref, cim_ref, d_ref, du_ref, dbre_ref, dbim_ref, dcre_ref, dcim_ref, dd_ref, dare_ref, daim_ref,
             carry, gre, gim):
        i = pl.program_id(1)
        sums = (dbre_ref, dbim_ref, dcre_ref, dcim_ref, dd_ref, dare_ref, daim_ref)

        @pl.when(i == 0)
        def _():
            carry[...] = jnp.zeros(carry.shape, F32)
            for r in sums:
                r[...] = jnp.zeros(r.shape, F32)

        uv, sre, sim = u_ref[...], sre_ref[...], sim_ref[...]
        _, out_vjp = jax.vjp(_s5_out_fn, sre, sim, uv, cre_ref[...], cim_ref[...], d_ref[...])
        gre[...], gim[...], du_out, dcre, dcim, dd = out_vjp((dyg_ref[...],))
        c = _scan_rows(gre, gim, (are_ref[...], -aim_ref[...]), (carry[0:1, :], carry[1:2, :]), True)
        carry[0:1, :] = c[0]
        carry[1:2, :] = c[1]
        g_re, g_im = gre[...], gim[...]
        at_start = i == n_t - 1
        p_re = _shift_down(sre, jnp.where(at_start, 0.0, hre_ref[7:8, :]))
        p_im = _shift_down(sim, jnp.where(at_start, 0.0, him_ref[7:8, :]))
        _, in_vjp = jax.vjp(_s5_in_fn, uv, bre_ref[...], bim_ref[...])
        du_in, dbre, dbim = in_vjp((g_re, g_im))
        du_ref[...] = du_out + du_in + res_ref[...]
        dare = jnp.sum(g_re * p_re + g_im * p_im, axis=0, keepdims=True)
        daim = jnp.sum(g_im * p_re - g_re * p_im, axis=0, keepdims=True)
        for r, val in zip(sums, (dbre, dbim, dcre, dcim, dd, dare, daim)):
            r[...] += val

    u_blk = _spec((tt, SSM_BLK_IN), lambda l, t: (n_t - 1 - t, l))
    s_blk = _spec((tt, SSM_BLK_ST), lambda l, t: (n_t - 1 - t, l))
    halo = _spec((8, SSM_BLK_ST), lambda l, t: (jnp.maximum((n_t - 1 - t) * (tt // 8) - 1, 0), l))
    blk3 = lambda arr: _spec((1,) + arr.shape[1:], lambda l, t: (l, 0, 0))
    par = lambda width: _spec((1, width), lambda l, t: (0, l))
    params = [blk3(b_re), blk3(b_im), par(SSM_BLK_ST), par(SSM_BLK_ST), blk3(c_re), blk3(c_im), par(SSM_BLK_IN)]
    shape = lambda arr: jax.ShapeDtypeStruct(arr.shape, F32)
    return pl.pallas_call(
        body, name="s5_bwd", grid=(N_SSM_BLK, n_t),
        in_specs=[u_blk, u_blk, s_blk, s_blk, halo, halo, u_blk] + params,
        out_specs=[u_blk, blk3(b_re), blk3(b_im), blk3(c_re), blk3(c_im), par(SSM_BLK_IN), par(SSM_BLK_ST), par(SSM_BLK_ST)],
        out_shape=[shape(u), shape(b_re), shape(b_im), shape(c_re), shape(c_im), shape(d_row), shape(a_re), shape(a_im)],
        scratch_shapes=[pltpu.VMEM((2, SSM_BLK_ST), F32), pltpu.VMEM((tt, SSM_BLK_ST), F32), pltpu.VMEM((tt, SSM_BLK_ST), F32)],
        compiler_params=_cparams(2),
    )(dyg, u, s_re, s_im, s_re, s_im, dh_res, b_re, b_im, a_re, a_im, c_re, c_im, d_row)


def _loss_call(h, target, tm):
    n_tok, d = h.shape

    def body(h_ref, t_ref, acc_ref, dh_ref):
        @pl.when(pl.program_id(0) == 0)
        def _():
            acc_ref[...] = jnp.zeros(acc_ref.shape, F32)

        e = h_ref[...] - t_ref[...]
        dh_ref[...] = e * (1.0 / d)
        acc_ref[...] += jnp.sum(jnp.sum(e * e, axis=1, keepdims=True), axis=0, keepdims=True)

    tok = _spec((tm, d), lambda i: (i, 0))
    return pl.pallas_call(
        body, name="loss", grid=(n_tok // tm,),
        in_specs=[tok, tok], out_specs=[_spec((8, 128), lambda i: (0, 0)), tok],
        out_shape=[jax.ShapeDtypeStruct((8, 128), F32), jax.ShapeDtypeStruct(h.shape, F32)],
        compiler_params=_cparams(1),
    )(h, target)


def _block_diag(t):
    nb, ng, rows, cols = t.shape
    eye = jnp.eye(ng, dtype=t.dtype)
    return jnp.einsum('bgrc,gh->bgrhc', t, eye).reshape(nb, ng * rows, ng * cols)


def _block_diag_t(t, rows, cols):
    nb, ng = t.shape[0], t.shape[1] // rows
    t5 = t.reshape(nb, ng, rows, ng, cols)
    return jnp.einsum('bgrhc,gh->bgrc', t5, jnp.eye(ng, dtype=t.dtype))


def _mlp_fwd(layer, h, w1g, w2g):
    pre = _matmul(f"mlp{layer}_up", h, w1g, 'nn', b_cols=(layer,), out_dtype=BF16)
    return pre, _matmul(f"mlp{layer}_down", pre, w2g, 'nn', lhs_fn=_relu2, b_rows=(layer,))


def _mlp_bwd(layer, h, pre, dy, dh_res, w1g, w2g):
    dpre = _matmul(f"mlp{layer}_down_dx", dy, w2g, 'nt', epi=lambda acc, p: acc * (2.0 * jnp.maximum(p, 0.0)),
                   extra=pre, b_rows=(layer,), out_dtype=BF16)
    dw2 = _matmul(f"mlp{layer}_down_dw", pre, dy, 'tn', lhs_fn=_relu2, out_dtype=GRAD_DTYPE)
    dw1 = _matmul(f"mlp{layer}_up_dw", h, dpre, 'tn', out_cols=True, out_dtype=GRAD_DTYPE)
    dh = _matmul(f"mlp{layer}_up_dx", dpre, w1g, 'nt', epi=lambda acc, e: acc + e, extra=dh_res, b_cols=(layer,))
    return dh, dw1, dw2


def _local_step(x, target, w, mid, late, exchange):
    n_tok = x.shape[0]
    tm = min(TOKEN_TILE, n_tok)
    n_tile, n_chunk = n_tok // tm, n_tok // CHUNK
    row = lambda v: v.reshape(1, -1)
    big = {}
    ln_g, ln_b = w['ln_g'], w['ln_b']
    grads = {}
    to_slots = lambda t: t.reshape(N_DEV, t.shape[0] // N_DEV, t.shape[1])

    halo_spec = (8, D_MODEL), lambda i: (jnp.maximum(i * (tm // 8) - 1, 0), 0)
    mix_p = [_tile(x, tm), (x,) + halo_spec, _full(w['rw_mu'][0]), _full(row(w['rw_w0'])), _full(w['rw_w1'][0]),
             _full(w['rw_w2'][0]), _full(row(w['rw_a0'])), _full(w['rw_a1'][0]), _full(w['rw_a2'][0]),
             _full(w['rw_g1'][0]), _full(w['rw_g2'][0])]
    tok_out = lambda dt: ((n_tok, D_MODEL), dt, (tm, D_MODEL), lambda i: (i, 0))
    xr, xk, xv, lw, a, g = _fwd_call("rwkv_mix", _mix_fn, (n_tile,), mix_p,
                                     [tok_out(BF16), tok_out(BF16), tok_out(BF16), tok_out(F32), tok_out(F32), tok_out(F32)])
    mid.forward(xr)
    square = lambda t: t.reshape(D_MODEL, D_MODEL)
    big.update({n: square(t) for n, t in mid.finish(xr).items()})
    r = _matmul("rwkv_r", xr, big['rw_wr'], 'nn')
    k = _matmul("rwkv_k", xk, big['rw_wk'], 'nn')
    v = _matmul("rwkv_v", xv, big['rw_wv'], 'nn')

    scan_in = (r, lw, k, v, a, row(w['rw_k_k']), row(w['rw_k_a']), w['rw_r_k'].reshape(1, -1))
    o, bonus, *scan_saved = _rwkv_scan_fwd(*scan_in)

    zero = late.forward(o)
    gate_p = [_tile(o, tm), _tile(bonus, tm), _tile(g, tm), _full(row(w['rw_lnx_g']) + zero), _full(row(w['rw_lnx_b']))]
    (og,) = _fwd_call("rwkv_gate", _gate_fn, (n_tile,), gate_p, [tok_out(BF16)])
    late_w = late.finish(og)
    big['rw_wo'] = square(late_w['rw_wo'])
    w1g, w2g, glu_g = late_w['mlp_w1'], late_w['mlp_w2'], late_w['s5_w_glu']
    y0 = _matmul("rwkv_o", og, big['rw_wo'], 'nn')
    tl = min(LN_TILE, n_tok)
    n_ln = n_tok // tl
    ln_out = lambda dt: ((n_tok, D_MODEL), dt, (tl, D_MODEL), lambda i: (i, 0))
    ln0_p = [_tile(x, tl), _tile(y0, tl), _full(ln_g[0:1]), _full(ln_b[0:1])]
    twice = lambda fn: lambda *args: fn(*args) * 2
    h1, h1_b = _fwd_call("ln0", twice(_res_ln_fn), (n_ln,), ln0_p, [ln_out(F32), ln_out(BF16)])
    pre0, y1 = _mlp_fwd(0, h1_b, w1g, w2g)
    ln1_p = [_tile(h1, tl), _tile(y1, tl), _full(ln_g[1:2]), _full(ln_b[1:2])]
    (h2,) = _fwd_call("ln1", _res_ln_fn, (n_ln,), ln1_p, [ln_out(F32)])

    gp = (SSM_GROUPS, SSM_STATE)
    par_p = [_full(w['s5_a_re'][0]), _full(w['s5_a_im'][0]), _full(w['s5_log_dt'].reshape(SSM_GROUPS, 1))]
    s5_par_fn = lambda ar, ai, ld: _s5_param_fn(ar, ai, ld, None, None)
    gp_out = (gp, F32, gp, lambda i: (0, 0))
    abar_re, abar_im, coef_re, coef_im = _fwd_call("s5_param", s5_par_fn, (1,), par_p, [gp_out] * 4)
    b_flat = [w[n][0].reshape(SSM_LANES, SSM_GROUP) for n in ('s5_b_re', 's5_b_im')]
    bbar_p = [_full(coef_re.reshape(SSM_LANES, 1)), _full(coef_im.reshape(SSM_LANES, 1)), _full(b_flat[0]), _full(b_flat[1])]
    bb_out = ((SSM_LANES, SSM_GROUP), F32, (SSM_LANES, SSM_GROUP), lambda i: (0, 0))
    bbar_re, bbar_im = _fwd_call("s5_bbar", _s5_bbar_fn, (1,), bbar_p, [bb_out] * 2)
    to_in = lambda t: _block_diag(t.reshape(N_SSM_BLK, 8, SSM_STATE, SSM_GROUP).transpose(0, 1, 3, 2))
    to_out = lambda t: _block_diag(t.reshape(N_SSM_BLK, 8, SSM_GROUP, SSM_STATE).transpose(0, 1, 3, 2))
    bblk_re, bblk_im = to_in(bbar_re), to_in(bbar_im)
    cblk_re, cblk_im = to_out(w['s5_c_re'][0]), to_out(w['s5_c_im'][0])
    a_row_re, a_row_im = abar_re.reshape(1, SSM_LANES), abar_im.reshape(1, SSM_LANES)

    d_row = row(w['s5_d'])
    s5_params = (bblk_re, bblk_im, a_row_re, a_row_im, cblk_re, cblk_im, d_row)
    s_re, s_im, yg = _s5_fwd(h2, *s5_params)
    z = _matmul("s5_glu", yg, glu_g, 'nn', b_cols=(0,))
    ln2_p = [_tile(h2, tl), _tile(z, tl), _full(ln_g[2:3]), _full(ln_b[2:3])]
    h3, h3_b = _fwd_call("ln2", twice(_glu_ln_fn), (n_ln,), ln2_p, [ln_out(F32), ln_out(BF16)])
    pre1, y3 = _mlp_fwd(1, h3_b, w1g, w2g)
    ln3_p = [_tile(h3, tl), _tile(y3, tl), _full(ln_g[3:4]), _full(ln_b[3:4])]
    (h4,) = _fwd_call("ln3", _res_ln_fn, (n_ln,), ln3_p, [ln_out(F32)])

    loss_acc, dh4 = _loss_call(h4, target, tl)

    ln_w = ['tile', 'tile_bf16', 'acc', 'acc']
    dh3_res, dy3, dg3, db3 = _bwd_call("ln3_bwd", _res_ln_fn, (n_ln,), ln3_p, [_tile(dh4, tl)], ln_w)
    dh3, dw1_1, dw2_1 = _mlp_bwd(1, h3_b, pre1, dy3, dh3_res, w1g, w2g)
    zero = exchange([('mlp_w1', 1, dw1_1), ('mlp_w2', 1, to_slots(dw2_1))])
    ln2_p[2] = _full(ln_g[2:3] + zero)
    dh2_res, dz, dg2, db2 = _bwd_call("ln2_bwd", _glu_ln_fn, (n_ln,), ln2_p, [_tile(dh3, tl)], ln_w)
    dyg = _matmul("s5_glu_dx", dz, glu_g, 'nt', b_cols=(0,))
    dw_glu = _matmul("s5_glu_dw", yg, dz, 'tn', out_cols=True, out_dtype=GRAD_DTYPE)
    dh2, dbb_re, dbb_im, dcb_re, dcb_im, dd, dabar_re, dabar_im = _s5_bwd(dyg, h2, s_re, s_im, dh2_res, *s5_params)
    from_in = lambda t: _block_diag_t(t, SSM_GROUP, SSM_STATE).transpose(0, 1, 3, 2).reshape(SSM_LANES, SSM_GROUP)
    from_out = lambda t: _block_diag_t(t, SSM_STATE, SSM_GROUP).transpose(0, 1, 3, 2).reshape(1, SSM_GROUPS, SSM_GROUP, SSM_STATE)
    grads['s5_c_re'], grads['s5_c_im'] = from_out(dcb_re), from_out(dcb_im)
    grads['s5_d'] = dd
    dcoef_re, dcoef_im, db_re, db_im = _bwd_call(
        "s5_bbar_bwd", _s5_bbar_fn, (1,), bbar_p, [_full(from_in(dbb_re)), _full(from_in(dbb_im))], ['acc'] * 4)
    grads['s5_b_re'] = db_re.reshape(w['s5_b_re'].shape)
    grads['s5_b_im'] = db_im.reshape(w['s5_b_im'].shape)
    par_ct = [_full(dabar_re.reshape(gp)), _full(dabar_im.reshape(gp)), _full(dcoef_re.reshape(gp)), _full(dcoef_im.reshape(gp))]
    da_re, da_im, dlog_dt = _bwd_call("s5_param_bwd", s5_par_fn, (1,), par_p, par_ct, ['acc'] * 3)
    grads['s5_a_re'], grads['s5_a_im'], grads['s5_log_dt'] = da_re[None], da_im[None], dlog_dt.reshape(1, SSM_GROUPS)
    zero = exchange([(n, 0, grads[n]) for n in ('s5_a_re', 's5_a_im', 's5_log_dt', 's5_b_re', 's5_b_im', 's5_c_re', 's5_c_im')])
    ln1_p[2] = _full(ln_g[1:2] + zero)

    dh1_res, dy1, dg1, db1 = _bwd_call("ln1_bwd", _res_ln_fn, (n_ln,), ln1_p, [_tile(dh2, tl)], ln_w)
    dh1, dw1_0, dw2_0 = _mlp_bwd(0, h1_b, pre0, dy1, dh1_res, w1g, w2g)
    dx_res, dy0, dg0, db0 = _bwd_call("ln0_bwd", _res_ln_fn, (n_ln,), ln0_p, [_tile(dh1, tl)], ln_w)
    grads['ln_g'] = jnp.concatenate([dg0, dg1, dg2, dg3])
    grads['ln_b'] = jnp.concatenate([db0, db1, db2, db3])
    dog = _matmul("rwkv_o_dx", dy0, big['rw_wo'], 'nt')
    dw_o = to_slots(_matmul("rwkv_o_dw", og, dy0, 'tn', out_dtype=GRAD_DTYPE))
    zero = exchange([('s5_w_glu', 0, dw_glu), ('mlp_w1', 0, dw1_0), ('mlp_w2', 0, to_slots(dw2_0)), ('rw_wo', 0, dw_o)])
    gate_p[3] = _full(row(w['rw_lnx_g']) + zero)
    d_o, dbonus, dgate, grads['rw_lnx_g'], grads['rw_lnx_b'] = _bwd_call(
        "rwkv_gate_bwd", _gate_fn, (n_tile,), gate_p, [_tile(dog, tm)], ['tile', 'tile', 'tile', 'acc', 'acc'])
    dr, dlw, dk, dv, da, grads['rw_k_k'], grads['rw_k_a'], dr_k = _rwkv_scan_bwd(*scan_in, *scan_saved, d_o, dbonus)
    grads['rw_r_k'] = dr_k.reshape(w['rw_r_k'].shape)
    dw_r = to_slots(_matmul("rwkv_r_dw", xr, dr, 'tn', out_dtype=GRAD_DTYPE))
    dw_k = to_slots(_matmul("rwkv_k_dw", xk, dk, 'tn', out_dtype=GRAD_DTYPE))
    dw_v = to_slots(_matmul("rwkv_v_dw", xv, dv, 'tn', out_dtype=GRAD_DTYPE))
    done = ('ln_g', 'ln_b', 'rw_lnx_g', 'rw_lnx_b', 'rw_k_k', 'rw_k_a', 'rw_r_k')
    zero = exchange([('rw_wr', 0, dw_r), ('rw_wk', 0, dw_k), ('rw_wv', 0, dw_v)] + [(n, 0, grads[n]) for n in done])
    mix_p[3] = _full(row(w['rw_w0']) + zero)
    dxr = _matmul("rwkv_r_dx", dr, big['rw_wr'], 'nt')
    dxk = _matmul("rwkv_k_dx", dk, big['rw_wk'], 'nt')
    dxv = _matmul("rwkv_v_dx", dv, big['rw_wv'], 'nt')
    mix_ct = [_tile(t, tm) for t in (dxr, dxk, dxv, dlw, da, dgate)]
    halo_grad = ('tile', ((n_tile * 8, D_MODEL), (8, D_MODEL), lambda i: (i, 0)))
    res = _bwd_call("rwkv_mix_bwd", _mix_fn, (n_tile,), mix_p, mix_ct, ['tile', halo_grad] + ['acc'] * 9,
                    addends={0: _tile(dx_res, tm)})
    corr = res[1].reshape(n_tile, 8, D_MODEL)[1:, 7:8, :]
    corr = jnp.pad(corr, ((0, 1), (tm - 1, 0), (0, 0)))
    dx = (res[0].reshape(n_tile, tm, D_MODEL) + corr).reshape(n_tok, D_MODEL)
    (grads['rw_mu'], grads['rw_w0'], grads['rw_w1'], grads['rw_w2'], grads['rw_a0'], grads['rw_a1'], grads['rw_a2'],
     grads['rw_g1'], grads['rw_g2']) = [t[None] if t.shape[0] != 1 else t for t in res[2:]]
    return loss_acc[0, 0], dx, grads


def _full_shape(local_shape, axis):
    return local_shape[:axis] + (N_DEV * local_shape[axis],) + local_shape[axis + 1:]


def _split_shape(local_shape, axis):
    return local_shape[:axis] + (N_DEV, local_shape[axis]) + local_shape[axis + 1:]


def _mesh_pos():
    return lax.axis_index("x"), lax.axis_index("y"), lax.axis_index("c")


def _flip(v, f):
    return 1 - v if f else v


def _hbm_call(name, body, arrays, out_shapes, n_sems):
    n_t = len(arrays)
    hbm = pl.BlockSpec(memory_space=pl.ANY)
    return pl.pallas_call(
        body, name=name, out_shape=out_shapes, in_specs=[hbm] * n_t, out_specs=[hbm] * n_t,
        scratch_shapes=[pltpu.SemaphoreType.DMA((n_t, n_sems)), pltpu.SemaphoreType.DMA((n_t, n_sems)),
                        pltpu.SemaphoreType.DMA((n_t,))],
    )(*arrays)


def _all_gather(name, locals_):
    n_t = len(locals_)

    def body(*refs):
        x_refs, out_refs = refs[:n_t], refs[n_t:2 * n_t]
        send_sems, recv_sems, local_sems = refs[2 * n_t:]
        x, y, c = _mesh_pos()
        me, sibling = (x, y, c), (x, y, 1 - c)
        chips = [(1 - x, y), (x, 1 - y), (1 - x, 1 - y)]

        def copy(t, k, block, to, own=False):
            slot = out_refs[t].at[4 * block[0] + 2 * block[1] + block[2]]
            return pltpu.make_async_remote_copy(
                src_ref=x_refs[t] if own else slot, dst_ref=slot,
                send_sem=send_sems.at[t, k], recv_sem=recv_sems.at[t, k],
                device_id=to, device_id_type=pl.DeviceIdType.MESH)

        mine = [pltpu.make_async_copy(x_refs[t], out_refs[t].at[4 * x + 2 * y + c], local_sems.at[t]) for t in range(n_t)]
        for cp in mine:
            cp.start()
        sent = []
        for t in range(n_t):
            sent.append(copy(t, 0, me, sibling, own=True))
            sent += [copy(t, 1 + j, me, (*chip, c), own=True) for j, chip in enumerate(chips)]
        for cp in sent:
            cp.start()
        for j, chip in enumerate(chips):
            for t in range(n_t):
                copy(t, 1 + j, (*chip, c), me).wait_recv()
                passed = copy(t, 4 + j, (*chip, c), sibling)
                passed.start()
                sent.append(passed)
        for t in range(n_t):
            copy(t, 0, sibling, me).wait_recv()
            for j, chip in enumerate(chips):
                copy(t, 4 + j, (*chip, 1 - c), me).wait_recv()
        for cp in sent:
            cp.wait_send()
        for cp in mine:
            cp.wait()

    outs = [jax.ShapeDtypeStruct((N_DEV,) + a.shape, a.dtype) for a in locals_]
    return _hbm_call(name, body, locals_, outs, 7)


def _all_to_all(name, blocks):
    n_t = len(blocks)

    def body(*refs):
        x_refs, out_refs = refs[:n_t], refs[n_t:2 * n_t]
        send_sems, recv_sems, local_sems = refs[2 * n_t:]
        x, y, c = _mesh_pos()
        my_slot = 4 * x + 2 * y + c
        mine = [pltpu.make_async_copy(x_refs[t].at[my_slot], out_refs[t].at[my_slot], local_sems.at[t]) for t in range(n_t)]
        for cp in mine:
            cp.start()
        copies = []
        for k in range(1, N_DEV):
            peer = (_flip(x, k & 4), _flip(y, k & 2), _flip(c, k & 1))
            peer_slot = 4 * peer[0] + 2 * peer[1] + peer[2]
            for t in range(n_t):
                sems = dict(send_sem=send_sems.at[t, k - 1], recv_sem=recv_sems.at[t, k - 1],
                            device_id=peer, device_id_type=pl.DeviceIdType.MESH)
                cp = pltpu.make_async_remote_copy(src_ref=x_refs[t].at[peer_slot], dst_ref=out_refs[t].at[my_slot], **sems)
                cp.start()
                landed = pltpu.make_async_remote_copy(src_ref=x_refs[t].at[my_slot], dst_ref=out_refs[t].at[peer_slot], **sems)
                copies.append((cp, landed))
        for sent, landed in copies:
            landed.wait_recv()
            sent.wait_send()
        for cp in mine:
            cp.wait()

    outs = [jax.ShapeDtypeStruct(a.shape, a.dtype) for a in blocks]
    return _hbm_call(name, body, blocks, outs, 7)


_HBM = pl.BlockSpec(memory_space=pltpu.HBM)
_SEM = pl.BlockSpec(memory_space=pltpu.SEMAPHORE)
_EFFECT = pltpu.SideEffectType.DATAFLOW_SIDE_EFFECTING


def _peer_copies(x_refs, land_refs, send_sems, recv_sems, gather, mine):
    x, y, c = _mesh_pos()
    my_slot = 4 * x + 2 * y + c
    copies = []
    for k in range(1, N_DEV):
        peer = (_flip(x, k & 4), _flip(y, k & 2), _flip(c, k & 1))
        peer_slot = 4 * peer[0] + 2 * peer[1] + peer[2]
        for t, (x_ref, land) in enumerate(zip(x_refs, land_refs)):
            sem = t * (N_DEV - 1) + k - 1
            copies.append(pltpu.make_async_remote_copy(
                src_ref=x_ref if gather[t] else x_ref.at[peer_slot], dst_ref=land.at[my_slot if mine else peer_slot],
                send_sem=send_sems.at[sem], recv_sem=recv_sems.at[sem], device_id=peer, device_id_type=pl.DeviceIdType.MESH))
    return copies


def _split_start(name, held, plan, n_copies):
    n_h = len(held)

    def body(*refs):
        for sent in plan(refs[:n_h], refs[n_h], refs[n_h + 1], True):
            sent.start()
        refs[-1][...] = jnp.zeros((8, 128), F32)

    sems = pltpu.SemaphoreType.DMA((n_copies,))
    out = pl.pallas_call(
        body, name=name,
        out_shape=(sems, sems, *[pltpu.HBM(a.shape, a.dtype) for a in held], jax.ShapeDtypeStruct((8, 128), F32)),
        in_specs=[_HBM] * n_h,
        out_specs=(_SEM, _SEM, *[_HBM] * n_h, pl.BlockSpec(memory_space=pltpu.VMEM)),
        input_output_aliases={i: 2 + i for i in range(n_h)},
        compiler_params=pltpu.CompilerParams(has_side_effects=_EFFECT),
    )(*[pltpu.with_memory_space_constraint(a, pltpu.HBM) for a in held])
    return dict(send=out[0], recv=out[1], held=out[2:2 + n_h], zero=out[-1][0, 0], plan=plan)


def _split_wait(name, started, after):
    held, plan = started['held'], started['plan']
    n_h = len(held)

    def body(*refs):
        for sent in plan(refs[:n_h], refs[n_h], refs[n_h + 1], True):
            sent.wait_send()
        for landed in plan(refs[:n_h], refs[n_h], refs[n_h + 1], False):
            landed.wait_recv()

    return pl.pallas_call(
        body, name=name,
        out_shape=[pltpu.HBM(a.shape, a.dtype) for a in held],
        in_specs=[_HBM] * n_h + [_SEM, _SEM, pl.BlockSpec(memory_space=pl.ANY)],
        out_specs=[_HBM] * n_h,
        input_output_aliases={i: i for i in range(n_h)},
        compiler_params=pltpu.CompilerParams(has_side_effects=_EFFECT),
    )(*held, started['send'], started['recv'], after)


def _push_start(name, arrays, gather):
    n_t = len(arrays)
    lands = [lax.empty((N_DEV,) + a.shape if whole else a.shape, a.dtype) for a, whole in zip(arrays, gather)]
    plan = lambda refs, send, recv, mine: _peer_copies(refs[:n_t], refs[n_t:], send, recv, gather, mine)
    return _split_start(name, list(arrays) + lands, plan, n_t * (N_DEV - 1))


def _push_wait(name, started, after):
    held = _split_wait(name, started, after)
    return held[:len(held) // 2], held[len(held) // 2:]


def _chip_copies(refs, send_sems, recv_sems, mine):
    n_t = len(refs) // 2
    x, y, c = _mesh_pos()
    my_slot = 4 * x + 2 * y + c
    peers = [(x, y, 1 - c), (1 - x, y, c), (x, 1 - y, c), (1 - x, 1 - y, c)]
    copies = []
    for j, peer in enumerate(peers):
        peer_slot = 4 * peer[0] + 2 * peer[1] + peer[2]
        for t in range(n_t):
            copies.append(pltpu.make_async_remote_copy(
                src_ref=refs[t], dst_ref=refs[n_t + t].at[my_slot if mine else peer_slot],
                send_sem=send_sems.at[4 * t + j], recv_sem=recv_sems.at[4 * t + j],
                device_id=peer, device_id_type=pl.DeviceIdType.MESH))
    return copies


def _forward_copies(refs, send_sems, recv_sems, mine):
    x, y, c = _mesh_pos()
    sibling = (x, y, 1 - c)
    copies = []
    for j, (px, py) in enumerate([(1 - x, y), (x, 1 - y), (1 - x, 1 - y)]):
        slot = 4 * px + 2 * py + (c if mine else 1 - c)
        for t, land in enumerate(refs):
            copies.append(pltpu.make_async_remote_copy(
                src_ref=land.at[slot], dst_ref=land.at[slot], send_sem=send_sems.at[3 * t + j], recv_sem=recv_sems.at[3 * t + j],
                device_id=sibling, device_id_type=pl.DeviceIdType.MESH))
    return copies


class _TwoStageGather:
    def __init__(self, name, arrays, my_slot):
        self.name, self.arrays, self.my_slot = name, arrays, my_slot
        values = list(arrays.values())
        lands = [lax.empty((N_DEV,) + a.shape, a.dtype) for a in values]
        self.stage = _split_start(name + "_start", values + lands, _chip_copies, 4 * len(values))
        self.zero = self.stage['zero']

    def forward(self, after):
        held = _split_wait(self.name + "_wait", self.stage, after)
        self.stage = _split_start(self.name + "_forward_start", held[len(self.arrays):], _forward_copies, 3 * len(self.arrays))
        return self.stage['zero']

    def finish(self, after):
        lands = _split_wait(self.name + "_forward_wait", self.stage, after)
        return {n: lax.dynamic_update_slice(land, a[None], (self.my_slot,) + (0,) * a.ndim)
                for (n, a), land in zip(self.arrays.items(), lands)}


ADAM_BLOCK = 128 * 1024


def _adam_update(part, w_ref, m_ref, v_ref, g_ref, d_ref, nm_ref, nv_ref):
    g = part(0)
    for i in range(1, N_DEV):
        g = g + part(i)
    m_new = ADAM_B1 * m_ref[...] + (1.0 - ADAM_B1) * g
    v_new = ADAM_B2 * v_ref[...] + (1.0 - ADAM_B2) * (g * g)
    m_hat = m_new * (1.0 / (1.0 - ADAM_B1 ** ADAM_STEP))
    v_hat = v_new * (1.0 / (1.0 - ADAM_B2 ** ADAM_STEP))
    g_ref[...] = g
    nm_ref[...] = m_new
    nv_ref[...] = v_new
    d_ref[...] = -ADAM_LR * (m_hat / (jnp.sqrt(v_hat) + ADAM_EPS) + ADAM_WD * w_ref[...])


def _adam_many(name, items):
    n_t = len(items)
    n_in = [4 if own is None else 5 for *_, own in items]
    first = [sum(n_in[:t]) for t in range(n_t)]

    def body(*refs):
        ins, outs = refs[:sum(n_in)], refs[sum(n_in):]
        x, y, c = _mesh_pos()
        my_slot = 4 * x + 2 * y + c
        for t in range(n_t):
            s_ref, w_ref, m_ref, v_ref = ins[first[t]:first[t] + 4]
            if n_in[t] == 4:
                part = lambda i: s_ref[i]
            else:
                own_ref = ins[first[t] + 4]
                part = lambda i: jnp.where(my_slot == i, own_ref[...], s_ref[i])
            _adam_update(part, w_ref, m_ref, v_ref, *outs[4 * t:4 * t + 4])

    vmem = pl.BlockSpec(memory_space=pltpu.VMEM)
    out = pl.pallas_call(
        body, name=name, in_specs=[vmem] * sum(n_in), out_specs=[vmem] * (4 * n_t),
        out_shape=[jax.ShapeDtypeStruct(item[1].shape, F32) for item in items for _ in range(4)],
        compiler_params=pltpu.CompilerParams(vmem_limit_bytes=VMEM_LIMIT),
    )(*[a for item in items for a in item if a is not None])
    return [out[4 * t:4 * t + 4] for t in range(n_t)]


def _adam_call(name, slots, w, m, v, own=None):
    n_layer, rows, width = w.shape
    tile = min(rows, ADAM_BLOCK // max(width, 128))

    def body(*refs):
        s_refs = refs[:n_layer]
        own_refs = refs[n_layer:2 * n_layer] if own is not None else None
        w_ref, m_ref, v_ref = refs[-7:-4]
        layer = pl.program_id(0)
        x, y, c = _mesh_pos()
        my_slot = 4 * x + 2 * y + c
        for l in range(n_layer):
            @pl.when(layer == l)
            def _(l=l):
                if own is None:
                    part = lambda i: s_refs[l][i].astype(F32)
                else:
                    part = lambda i: jnp.where(my_slot == i, own_refs[l][...], s_refs[l][i]).astype(F32)
                _adam_update(part, w_ref, m_ref, v_ref, *refs[-4:])

    slot_spec = lambda k: _spec((N_DEV, tile, width), lambda l, i: (0, jnp.where(l == k, i, 0), 0))
    own_spec = lambda k: _spec((tile, width), lambda l, i: (jnp.where(l == k, i, 0), 0))
    blk = _spec((None, tile, width), lambda l, i: (l, i, 0))
    specs = [slot_spec(k) for k in range(n_layer)] + ([own_spec(k) for k in range(n_layer)] if own is not None else [])
    return pl.pallas_call(
        body, name=name, grid=(n_layer, rows // tile),
        in_specs=specs + [blk] * 3, out_specs=[blk] * 4, out_shape=[jax.ShapeDtypeStruct(w.shape, F32)] * 4,
        compiler_params=_cparams(2),
    )(*slots, *(own or []), w, m, v)


def kernel(x, ln_g, ln_b, rw_mu, rw_w0, rw_w1, rw_w2, rw_a0, rw_a1, rw_a2, rw_g1, rw_g2, rw_k_k, rw_k_a, rw_r_k, rw_wr, rw_wk, rw_wv, rw_wo, rw_lnx_g, rw_lnx_b, s5_a_re, s5_a_im, s5_log_dt, s5_b_re, s5_b_im, s5_c_re, s5_c_im, s5_d, s5_w_glu, mlp_w1, mlp_w2, loss_target, m_ln_g, m_ln_b, m_rw_mu, m_rw_w0, m_rw_w1, m_rw_w2, m_rw_a0, m_rw_a1, m_rw_a2, m_rw_g1, m_rw_g2, m_rw_k_k, m_rw_k_a, m_rw_r_k, m_rw_wr, m_rw_wk, m_rw_wv, m_rw_wo, m_rw_lnx_g, m_rw_lnx_b, m_s5_a_re, m_s5_a_im, m_s5_log_dt, m_s5_b_re, m_s5_b_im, m_s5_c_re, m_s5_c_im, m_s5_d, m_s5_w_glu, m_mlp_w1, m_mlp_w2, v_ln_g, v_ln_b, v_rw_mu, v_rw_w0, v_rw_w1, v_rw_w2, v_rw_a0, v_rw_a1, v_rw_a2, v_rw_g1, v_rw_g2, v_rw_k_k, v_rw_k_a, v_rw_r_k, v_rw_wr, v_rw_wk, v_rw_wv, v_rw_wo, v_rw_lnx_g, v_rw_lnx_b, v_s5_a_re, v_s5_a_im, v_s5_log_dt, v_s5_b_re, v_s5_b_im, v_s5_c_re, v_s5_c_im, v_s5_d, v_s5_w_glu, v_mlp_w1, v_mlp_w2):
    given = dict(locals())
    local_w = {n: given[n] for n in WEIGHTS}
    local_m = {n: given["m_" + n] for n in WEIGHTS}
    local_v = {n: given["v_" + n] for n in WEIGHTS}
    small = [n for n in WEIGHTS if n in SHARDED and n not in BIG]

    my_slot = 4 * lax.axis_index("x") + 2 * lax.axis_index("y") + lax.axis_index("c")
    as_bf16 = {n: local_w[n].astype(BF16) for n in BIG}

    gathered = _all_gather("weights_all_gather", [local_w[n] for n in small])
    full = dict(local_w)
    for n, blk in zip(small, gathered):
        full[n] = jnp.moveaxis(blk, 0, SHARDED[n]).reshape(_full_shape(local_w[n].shape, SHARDED[n]))
    mid = _TwoStageGather("weights_mid", {n: as_bf16[n] for n in MID}, my_slot)
    late = _TwoStageGather("weights_late", {n: as_bf16[n] for n in LATE}, my_slot)
    full['rw_w0'] = full['rw_w0'] + (mid.zero + late.zero)

    exchanges = []

    def exchange(entries):
        started = _push_start(f"grads_start_{len(exchanges)}", [t for _, _, t in entries], [n in REPLICATED for n, _, _ in entries])
        exchanges.append((entries, started))
        return started['zero']

    loss_sq, dx, grads = _local_step(x[0], loss_target[0], full, mid, late, exchange)
    loss = (0.5 / D_MODEL) * lax.psum(loss_sq, MESH_AXES)

    wide = {'s5_b_re': (SSM_LANES, SSM_GROUP), 's5_b_im': (SSM_LANES, SSM_GROUP),
            's5_c_re': (D_MODEL, SSM_STATE), 's5_c_im': (D_MODEL, SSM_STATE)}
    by_name, many = {}, []
    landed = {n: {} for n in BIG}
    for e, (entries, started) in enumerate(exchanges):
        sources, lands = _push_wait(f"grads_wait_{e}", started, dx)
        for (n, i, _), src, slots in zip(entries, sources, lands):
            if n in wide:
                as_view = lambda t: t.reshape((1,) + wide[n])
                out = _adam_call(f"adamw_{n}", [slots.reshape((N_DEV,) + wide[n])], as_view(local_w[n]), as_view(local_m[n]),
                                 as_view(local_v[n]), own=[src.reshape(wide[n])])
                by_name[n] = [t.reshape(local_w[n].shape) for t in out]
            elif n in REPLICATED:
                many.append((n, (slots, local_w[n], local_m[n], local_v[n], src)))
            else:
                landed[n][i] = (slots, lax.dynamic_index_in_dim(src, my_slot, 0, keepdims=False))
    for n in BIG:
        layers = [landed[n][i] for i in sorted(landed[n])]
        by_name[n] = _adam_call(f"adamw_{n}", [s for s, _ in layers], local_w[n], local_m[n], local_v[n], own=[o for _, o in layers])
    split = lambda n: jnp.moveaxis(grads[n].reshape(_split_shape(local_w[n].shape, SHARDED[n])), SHARDED[n], 0)
    for n, slots in zip(small, _all_to_all("grads_all_to_all", [split(n) for n in small])):
        many.append((n, (slots, local_w[n], local_m[n], local_v[n], None)))
    last = [n for n in REPLICATED if n not in by_name and n not in dict(many)]
    for n, slots in zip(last, _all_gather("grads_all_gather", [grads[n] for n in last])):
        many.append((n, (slots, local_w[n], local_m[n], local_v[n], None)))
    for (n, _), out in zip(many, _adam_many("adamw_small", [item for _, item in many])):
        by_name[n] = out
    results = [by_name[n][k] for k in range(4) for n in WEIGHTS]
    return (loss, dx[None], *results)
```

```python
import functools
import math

import jax
import jax.numpy as jnp
from jax import lax
from jax.experimental import pallas as pl
from jax.experimental.pallas import tpu as pltpu

F32 = jnp.float32
BF16 = jnp.bfloat16

D_MODEL = 1024
HEAD = 64
PAIR = 2 * HEAD
N_PAIR = D_MODEL // PAIR
CHUNK = 64
GN_EPS = 64e-5
LN_EPS = 1e-5
SSM_GROUP = 16
SSM_STATE = 64
SSM_GROUPS = D_MODEL // SSM_GROUP
SSM_LANES = SSM_GROUPS * SSM_STATE
SSM_BLK_IN = 128
SSM_BLK_ST = 512
N_SSM_BLK = D_MODEL // SSM_BLK_IN
DEPTH = 2
DN_ALPHA = (2.0 * DEPTH) ** 0.25
ADAM_LR, ADAM_B1, ADAM_B2, ADAM_EPS, ADAM_WD, ADAM_STEP = 0.001, 0.9, 0.999, 1e-08, 0.01, 10
N_DEV = 8
MESH_AXES = ("x", "y", "c")
VMEM_LIMIT = 56 * 1024 * 1024
TOKEN_TILE = 256
LN_TILE = 512

SHARDED = {
    "rw_mu": 2, "rw_w1": 1, "rw_w2": 2, "rw_a1": 1, "rw_a2": 2, "rw_g1": 1, "rw_g2": 2,
    "rw_wr": 1, "rw_wk": 1, "rw_wv": 1, "rw_wo": 1, "s5_d": 1, "s5_w_glu": 2, "mlp_w1": 2, "mlp_w2": 1,
}
WEIGHTS = ['ln_g', 'ln_b', 'rw_mu', 'rw_w0', 'rw_w1', 'rw_w2', 'rw_a0', 'rw_a1', 'rw_a2', 'rw_g1', 'rw_g2', 'rw_k_k',
           'rw_k_a', 'rw_r_k', 'rw_wr', 'rw_wk', 'rw_wv', 'rw_wo', 'rw_lnx_g', 'rw_lnx_b', 's5_a_re', 's5_a_im',
           's5_log_dt', 's5_b_re', 's5_b_im', 's5_c_re', 's5_c_im', 's5_d', 's5_w_glu', 'mlp_w1', 'mlp_w2']
REPLICATED = [n for n in WEIGHTS if n not in SHARDED]
BIG = ['rw_wr', 'rw_wk', 'rw_wv', 'rw_wo', 's5_w_glu', 'mlp_w1', 'mlp_w2']
MID = ['rw_wr', 'rw_wk', 'rw_wv']
LATE = ['rw_wo', 's5_w_glu', 'mlp_w1', 'mlp_w2']
GRAD_DTYPE = BF16


def _cparams(n_grid):
    return pltpu.CompilerParams(dimension_semantics=("arbitrary",) * n_grid, vmem_limit_bytes=VMEM_LIMIT)


@jax.custom_vjp
def _mm(x, w):
    return jnp.dot(x.astype(BF16), w.astype(BF16), preferred_element_type=F32)


def _mm_fwd(x, w):
    return _mm(x, w), (x, w)


def _mm_bwd(res, dy):
    x, w = res
    dyb = dy.astype(BF16)
    dx = lax.dot_general(dyb, w.astype(BF16), (((1,), (1,)), ((), ())), preferred_element_type=F32)
    dw = lax.dot_general(x.astype(BF16), dyb, (((0,), (0,)), ((), ())), preferred_element_type=F32)
    return dx, dw


_mm.defvjp(_mm_fwd, _mm_bwd)


_DOT_DIMS = {'nn': (((1,), (0,)), ((), ())), 'nt': (((1,), (1,)), ((), ())), 'tn': (((0,), (0,)), ((), ()))}
_BATCH_DOT_DIMS = {'nn': (((2,), (1,)), ((0,), (0,))), 'nt': (((2,), (2,)), ((0,), (0,))), 'tn': (((1,), (1,)), ((0,), (0,)))}
CHUNK_PASSES = 1


def _split_bf16(a):
    hi = a.astype(BF16)
    return hi, (a - hi.astype(F32)).astype(BF16)


def _dot_passes(a, b, mode, passes=None):
    dims = (_DOT_DIMS if a.ndim == 2 else _BATCH_DOT_DIMS)[mode]
    dot = lambda p, q: lax.dot_general(p, q, dims, preferred_element_type=F32)
    if (CHUNK_PASSES if passes is None else passes) == 1:
        return dot(a.astype(BF16), b.astype(BF16))
    (ah, al), (bh, bl) = _split_bf16(a), _split_bf16(b)
    return dot(ah, bh) + (dot(ah, bl) + dot(al, bh))


@functools.partial(jax.custom_vjp, nondiff_argnums=(2,))
def _pdot(a, b, mode):
    return _dot_passes(a, b, mode)


def _pdot_fwd(a, b, mode):
    return _dot_passes(a, b, mode), (a, b)


def _pdot_bwd(mode, res, dy):
    a, b = res
    if mode == 'nn':
        return _dot_passes(dy, b, 'nt'), _dot_passes(a, dy, 'tn')
    if mode == 'nt':
        return _dot_passes(dy, b, 'nn'), _dot_passes(dy, a, 'tn')
    return _dot_passes(b, dy, 'nt'), _dot_passes(a, dy, 'nn')


_pdot.defvjp(_pdot_fwd, _pdot_bwd)


def _tri_sum(x, upper):
    nb, n = x.shape[0], x.shape[1]
    ti = lax.broadcasted_iota(jnp.int32, (nb, n, n), 1)
    tj = lax.broadcasted_iota(jnp.int32, (nb, n, n), 2)
    tri = ((ti <= tj) if upper else (ti >= tj)).astype(BF16)
    hi = x.astype(BF16)
    rest = x - hi.astype(F32)
    mid = rest.astype(BF16)
    lo = (rest - mid.astype(F32)).astype(BF16)
    dot = lambda q: lax.dot_general(tri, q, _BATCH_DOT_DIMS['nn'], preferred_element_type=F32)
    return dot(hi) + (dot(mid) + dot(lo))


@jax.custom_vjp
def _cumsum_rows(x):
    return _tri_sum(x, False)


_cumsum_rows.defvjp(lambda x: (_tri_sum(x, False), None), lambda _, dy: (_tri_sum(dy, True),))


def _power_sum(a, order):
    m = a.shape[-1]
    ti = lax.broadcasted_iota(jnp.int32, (1, m, m), 1)
    tj = lax.broadcasted_iota(jnp.int32, (1, m, m), 2)
    pw, total = _dot_passes(a, a, 'nn'), (ti == tj).astype(F32) + a
    for _ in range(int(math.log2(order)) - 2):
        both = _dot_passes(pw, jnp.concatenate([pw, total], axis=2), 'nn')
        pw, total = both[:, :, :m], total + both[:, :, m:]
    return total + _dot_passes(pw, total, 'nn')


@functools.partial(jax.custom_vjp, nondiff_argnums=(1,))
def _nilpotent_inverse(a, order):
    return _power_sum(a, order)


def _nilpotent_inverse_fwd(a, order):
    inv = _power_sum(a, order)
    return inv, inv


def _nilpotent_inverse_bwd(order, inv, g):
    return (_dot_passes(_dot_passes(inv, g, 'tn'), inv, 'nt'),)


_nilpotent_inverse.defvjp(_nilpotent_inverse_fwd, _nilpotent_inverse_bwd)


@jax.custom_vjp
def _known_inverse(a, inv):
    return inv


_known_inverse.defvjp(lambda a, inv: (inv, inv),
                      lambda inv, g: (_dot_passes(_dot_passes(inv, g, 'tn'), inv, 'nt'), jnp.zeros_like(inv)))


@functools.partial(jax.custom_vjp, nondiff_argnums=(1,))
def _roll_rows(x, shift):
    return pltpu.roll(x, shift, 0)


def _roll_rows_fwd(x, shift):
    return pltpu.roll(x, shift, 0), None


def _roll_rows_bwd(shift, _, dy):
    n = dy.shape[0]
    return (pltpu.roll(dy, (n - shift) % n, 0),)


_roll_rows.defvjp(_roll_rows_fwd, _roll_rows_bwd)


def _shift_down(x, first_row):
    row = lax.broadcasted_iota(jnp.int32, x.shape, 0)
    return jnp.where(row == 0, first_row, _roll_rows(x, 1))


def _sigmoid(x):
    return 1.0 / (1.0 + jnp.exp(-x))


def _softplus(x):
    return jnp.maximum(x, 0.0) + jnp.log(1.0 + jnp.exp(-jnp.abs(x)))


def _gelu(x):
    return 0.5 * x * (1.0 + jnp.tanh(math.sqrt(2.0 / math.pi) * (x + 0.044715 * (x * x * x))))


def _layer_norm(x, g, b):
    mu = jnp.mean(x, axis=-1, keepdims=True)
    xc = x - mu
    var = jnp.mean(xc * xc, axis=-1, keepdims=True)
    return xc * lax.rsqrt(var + LN_EPS) * g + b


def _spec(block, index_map):
    return pl.BlockSpec(block, index_map)


def _tile(arr, tm):
    return (arr, (tm, arr.shape[1]), lambda i: (i, 0))


def _full(arr):
    nd = arr.ndim
    return (arr, arr.shape, lambda *_: (0,) * nd)


def _fwd_call(name, fn, grid, prims, outs):
    n_in = len(prims)

    def body(*refs):
        vals = [r[...] for r in refs[:n_in]]
        vals = [v.astype(F32) if v.dtype != F32 else v for v in vals]
        res = fn(*vals)
        for o, r in zip(refs[n_in:], res):
            o[...] = r.astype(o.dtype)

    return pl.pallas_call(
        body, name=name, grid=grid,
        in_specs=[_spec(b, m) for _, b, m in prims],
        out_specs=[_spec(b, m) for _, _, b, m in outs],
        out_shape=[jax.ShapeDtypeStruct(s, d) for s, d, _, _ in outs],
        compiler_params=_cparams(len(grid)),
    )(*[a for a, _, _ in prims])


def _bwd_call(name, fn, grid, prims, cts, wants, addends=None):
    addends = addends or {}
    n_in, n_ct = len(prims), len(cts)
    out_place = {}
    wants, out_dtype = list(wants), {}
    for i, w in enumerate(wants):
        if isinstance(w, tuple):
            wants[i], out_place[i] = w
        elif w == 'tile_bf16':
            wants[i], out_dtype[i] = 'tile', BF16
    diff = [i for i, w in enumerate(wants) if w]
    add_idx = [i for i in diff if i in addends]
    last_axis = len(grid) - 1

    def body(*refs):
        vals = [r[...] for r in refs[:n_in]]
        vals = [v.astype(F32) if v.dtype != F32 else v for v in vals]
        ct_vals = [r[...] for r in refs[n_in:n_in + n_ct]]
        ct_vals = [v.astype(F32) if v.dtype != F32 else v for v in ct_vals]
        add_refs = dict(zip(add_idx, refs[n_in + n_ct:n_in + n_ct + len(add_idx)]))
        out_refs = refs[n_in + n_ct + len(add_idx):]

        def f(*dargs):
            full = list(vals)
            for i, a in zip(diff, dargs):
                full[i] = a
            return tuple(fn(*full))

        _, vjp = jax.vjp(f, *[vals[i] for i in diff])
        grads = vjp(tuple(ct_vals))
        first = pl.program_id(last_axis) == 0
        for o, g, i in zip(out_refs, grads, diff):
            if wants[i] == 'tile':
                if i in add_refs:
                    g = g + add_refs[i][...]
                o[...] = g.astype(o.dtype)
            else:
                @pl.when(first)
                def _(o=o):
                    o[...] = jnp.zeros(o.shape, o.dtype)
                o[...] += g

    ins = list(prims) + list(cts) + [addends[i] for i in add_idx]
    places = [out_place.get(i, (prims[i][0].shape, prims[i][1], prims[i][2])) for i in diff]
    return pl.pallas_call(
        body, name=name, grid=grid,
        in_specs=[_spec(b, m) for _, b, m in ins],
        out_specs=[_spec(b, m) for _, b, m in places],
        out_shape=[jax.ShapeDtypeStruct(s, out_dtype.get(i, F32)) for i, (s, _, _) in zip(diff, places)],
        compiler_params=_cparams(len(grid)),
    )(*[a for a, _, _ in ins])


def _matmul(name, a, b, mode, *, out_dtype=F32, lhs_fn=None, epi=None, extra=None, tm=1024, tn=1024, tk=1024,
            b_cols=None, b_rows=None, out_cols=False):
    if mode == 'tn':
        kdim, m = a.shape
    else:
        m, kdim = a.shape
    if b_rows is not None:
        loc, cols = b.shape[-2], b.shape[-1]
        if mode == 'nn':
            n, k_shards = cols, max(1, tk // loc)
            tk = k_shards * loc
        else:
            n, tn = N_DEV * loc, loc
    elif b_cols is None:
        n = b.shape[0] if mode == 'nt' else b.shape[1]
    else:
        rows, loc = b.shape[-2], b.shape[-1]
        if mode == 'nn':
            n, n_shards = N_DEV * loc, max(1, tn // loc)
            tn = n_shards * loc
        else:
            n, k_shards = rows, max(1, tk // loc)
            tk = k_shards * loc
    if out_cols:
        o_shards = max(1, min(tn, n) // (n // N_DEV))
        tn = o_shards * (n // N_DEV)
    tm, tn, tk = min(tm, m), min(tn, n), min(tk, kdim)
    nk = kdim // tk
    grid = (m // tm, n // tn, nk)
    a_spec = _spec((tk, tm), lambda i, j, k: (k, i)) if mode == 'tn' else _spec((tm, tk), lambda i, j, k: (i, k))
    if b_rows is not None:
        lead = tuple(b_rows)
        skip = (None,) * (1 + len(lead))
        if mode == 'nn':
            b_spec = _spec((k_shards,) + skip[1:] + (tk // k_shards, tn), lambda i, j, k: (k,) + lead + (0, j))
        else:
            b_spec = _spec(skip + (tn, tk), lambda i, j, k: (j,) + lead + (0, k))
    elif b_cols is None:
        b_spec = _spec((tn, tk), lambda i, j, k: (j, k)) if mode == 'nt' else _spec((tk, tn), lambda i, j, k: (k, j))
    else:
        lead = tuple(b_cols)
        skip = (None,) * (1 + len(lead))
        if mode == 'nn':
            b_spec = _spec((n_shards,) + skip[1:] + (tk, tn // n_shards), lambda i, j, k: (j,) + lead + (k, 0))
        else:
            b_spec = _spec((k_shards,) + skip[1:] + (tn, tk // k_shards), lambda i, j, k: (k,) + lead + (j, 0))
    o_spec = _spec((tm, tn), lambda i, j, k: (i, j))
    if out_cols:
        o_place, o_shape = _spec((o_shards, tm, tn // o_shards), lambda i, j, k: (j, i, 0)), (N_DEV, m, n // N_DEV)
    else:
        o_place, o_shape = o_spec, (m, n)
    dims = _DOT_DIMS[mode]

    def body(*refs):
        if extra is None:
            a_ref, b_ref, o_ref, acc = refs
            x_ref = None
        else:
            a_ref, b_ref, x_ref, o_ref, acc = refs
        k = pl.program_id(2)
        av = a_ref[...]
        if lhs_fn is not None:
            av = lhs_fn(av.astype(F32))
        bv = b_ref[...]
        if bv.ndim == 3:
            bv = bv.reshape(tk, tn) if b_rows is not None else jnp.concatenate([bv[g] for g in range(bv.shape[0])], axis=1)
        part = lax.dot_general(av.astype(BF16), bv.astype(BF16), dims, preferred_element_type=F32)

        def finish(r):
            if epi is not None:
                r = epi(r, x_ref[...])
            if out_cols:
                width = tn // o_shards
                for g in range(o_shards):
                    o_ref[g] = r[:, g * width:(g + 1) * width].astype(o_ref.dtype)
            else:
                o_ref[...] = r.astype(o_ref.dtype)

        if nk == 1:
            finish(part)
            return

        @pl.when(k == 0)
        def _():
            acc[...] = part

        @pl.when((k > 0) & (k < nk - 1))
        def _():
            acc[...] += part

        @pl.when(k == nk - 1)
        def _():
            finish(acc[...] + part)

    ops, specs = [a, b], [a_spec, b_spec]
    if extra is not None:
        ops.append(extra)
        specs.append(o_spec)
    return pl.pallas_call(
        body, name=name, grid=grid, in_specs=specs, out_specs=o_place,
        out_shape=jax.ShapeDtypeStruct(o_shape, out_dtype),
        scratch_shapes=[pltpu.VMEM((tm, tn), F32)],
        compiler_params=_cparams(3),
    )(*ops)


def _relu2(x):
    r = jnp.maximum(x, 0.0)
    return r * r


def _mix_fn(x, halo, mu, w0, w1, w2, a0, a1, a2, g1, g2):
    prev_row = jnp.where(pl.program_id(0) == 0, 0.0, halo[7:8, :])
    xx = _shift_down(x, prev_row) - x
    xr, xw, xk, xv, xa, xg = (x + xx * mu[i:i + 1, :] for i in range(6))
    w_pre = w0 + _mm(jnp.tanh(_mm(xw, w1)), w2)
    log_decay = -jnp.exp(-_softplus(-w_pre) - 0.5)
    a = _sigmoid(a0 + _mm(_mm(xa, a1), a2))
    g = _mm(_sigmoid(_mm(xg, g1)), g2)
    return xr, xk, xv, log_decay, a, g


def _chunk_fn(r, lw, k, v, a, k_k, k_a, r_k, inverse=None, with_inverse=False):
    n, n_pair = r.shape[0], r.shape[1] // PAIR
    to_batch = lambda t: jnp.concatenate([t[None, :, p * PAIR:(p + 1) * PAIR] for p in range(n_pair)], axis=0)
    from_batch = lambda t: jnp.concatenate([t[p] for p in range(n_pair)], axis=1)
    r, lw, k, v, a, k_k, k_a, r_k = (to_batch(t) for t in (r, lw, k, v, a, k_k, k_a, r_k))
    lane = lax.broadcasted_iota(jnp.int32, (1, 1, PAIR), 2)
    m0 = lane < HEAD

    def head_sum(t):
        s0 = jnp.sum(jnp.where(m0, t, 0.0), axis=2, keepdims=True)
        s1 = jnp.sum(jnp.where(m0, 0.0, t), axis=2, keepdims=True)
        return jnp.where(m0, s0, s1)

    kk = k * k_k
    kk = kk / jnp.maximum(jnp.sqrt(head_sum(kk * kk)), 1e-12)
    k2 = k * (1.0 + (a - 1.0) * k_a)
    aa, bb = -kk, kk * a
    bonus = head_sum(r * k2 * r_k) * v

    cum = _cumsum_rows(lw)
    p_in, p_ex, p_inv = jnp.exp(cum), jnp.exp(cum - lw), jnp.exp(-cum)
    at, rt, kt, bt = aa * p_ex, r * p_in, k2 * p_inv, bb * p_inv

    def stack_masked(t):
        return jnp.concatenate([jnp.where(m0, t, 0.0), jnp.where(m0, 0.0, t)], axis=1)

    def unstack_sum(t):
        return t[:, :n] + t[:, n:]

    def unstack_select(t):
        return jnp.where(m0, t[:, :n], t[:, n:])

    ti = lax.broadcasted_iota(jnp.int32, (1, 2 * n, 2 * n), 1)
    tj = lax.broadcasted_iota(jnp.int32, (1, 2 * n, 2 * n), 2)
    same = (ti >= n) == (tj >= n)
    incl, strict = same & (ti >= tj), same & (ti > tj)
    m = 2 * n
    rows = lambda *ts: jnp.concatenate(ts, axis=1)
    cols = lambda *ts: jnp.concatenate(ts, axis=2)
    at_s, rt_s, kt_s, bt_s = stack_masked(at), stack_masked(rt), stack_masked(kt), stack_masked(bt)
    v_s, zeros = rows(v, v), jnp.zeros_like(at_s)
    bk_s = rows(bt_s, kt_s)
    gram = _pdot(rows(at_s, rt_s), bk_s, 'nt')
    a_ab = jnp.where(strict, gram[:, :m, :m], 0.0)
    a_ak = jnp.where(strict, gram[:, :m, m:], 0.0)
    a_rb = jnp.where(incl, gram[:, m:, :m], 0.0)
    a_rk = jnp.where(incl, gram[:, m:, m:], 0.0)
    inv = _nilpotent_inverse(a_ab, n) if inverse is None else _known_inverse(a_ab, inverse)
    wy_s = _pdot(inv, cols(at_s, _pdot(a_ak, v_s, 'nn')), 'nn')
    q_oi = _pdot(cols(a_rb, a_rk), rows(wy_s, cols(zeros, v_s)), 'nn')
    q_all = unstack_sum(rt_s + q_oi[:, :, :PAIR])
    oi_all = unstack_select(q_oi[:, :, PAIR:])
    own_head = (lax.broadcasted_iota(jnp.int32, (1, m, PAIR), 1) >= n) == (lax.broadcasted_iota(jnp.int32, (1, m, PAIR), 2) >= HEAD)
    wy_m = cols(wy_s[:, :, :PAIR], jnp.where(own_head, wy_s[:, :, PAIR:], 0.0))
    mm_nn = _pdot(bk_s, rows(wy_m, cols(zeros, stack_masked(v))), 'tn')
    mm, nn = mm_nn[:, :, :PAIR], mm_nn[:, :, PAIR:]
    ei = lax.broadcasted_iota(jnp.int32, (1, PAIR, PAIR), 1)
    ej = lax.broadcasted_iota(jnp.int32, (1, PAIR, PAIR), 2)
    eye_p = (ei == ej).astype(F32)
    decay_col = jnp.sum(eye_p * p_in[:, n - 1:n, :], axis=2, keepdims=True)
    maps = ((decay_col * (eye_p + mm))[None], (decay_col * nn)[None], from_batch(q_all), from_batch(oi_all),
            from_batch(bonus))
    return maps + (inv,) if with_inverse else maps


def _gate_fn(o, bonus, g, lnx_g, lnx_b):
    lane = lax.broadcasted_iota(jnp.int32, (1, PAIR), 1)
    m0 = lane < HEAD

    def head_mean(t):
        s0 = jnp.sum(jnp.where(m0, t, 0.0), axis=1, keepdims=True)
        s1 = jnp.sum(jnp.where(m0, 0.0, t), axis=1, keepdims=True)
        return jnp.where(m0, s0, s1) * (1.0 / HEAD)

    outs = []
    for p in range(N_PAIR):
        sl = slice(p * PAIR, (p + 1) * PAIR)
        oc = o[:, sl] - head_mean(o[:, sl])
        on = oc * lax.rsqrt(head_mean(oc * oc) + GN_EPS) * lnx_g[:, sl] + lnx_b[:, sl]
        outs.append((on + bonus[:, sl]) * g[:, sl])
    return (jnp.concatenate(outs, axis=1),)


def _res_ln_fn(h, y, g, b):
    return (_layer_norm(DN_ALPHA * h + y, g, b),)


def _glu_ln_fn(h, z, g, b):
    mix = z[:, :D_MODEL] * _sigmoid(z[:, D_MODEL:])
    return (_layer_norm(DN_ALPHA * h + mix, g, b),)


def _s5_param_fn(a_re, a_im, log_dt, b_re, b_im):
    dt = jnp.exp(log_dt)
    lam_re = jnp.minimum(a_re, -1e-4)
    lam_im = a_im
    mag = jnp.exp(dt * lam_re)
    abar_re = mag * jnp.cos(dt * lam_im)
    abar_im = mag * jnp.sin(dt * lam_im)
    den = lam_re * lam_re + lam_im * lam_im
    nr, ni = abar_re - 1.0, abar_im
    coef_re = (nr * lam_re + ni * lam_im) / den
    coef_im = (ni * lam_re - nr * lam_im) / den
    return abar_re, abar_im, coef_re, coef_im


def _s5_bbar_fn(coef_re, coef_im, b_re, b_im):
    return coef_re * b_re - coef_im * b_im, coef_re * b_im + coef_im * b_re


def _s5_in_fn(u, b_re, b_im):
    return _pdot(u, b_re[0], 'nn'), _pdot(u, b_im[0], 'nn')


def _s5_out_fn(s_re, s_im, u, c_re, c_im, d):
    y = _pdot(s_re, c_re[0], 'nn') - _pdot(s_im, c_im[0], 'nn') + u * d
    return (_gelu(y),)


def _pairs_to_batch(t):
    return jnp.concatenate([t[None, :, p * PAIR:(p + 1) * PAIR] for p in range(N_PAIR)], axis=0)


def _batch_to_pairs(t):
    return jnp.concatenate([t[p] for p in range(N_PAIR)], axis=1)


STATE_PASSES = 3


def _rwkv_scan_fwd(r, lw, k, v, a, k_k, k_a, r_k):
    n_tok = r.shape[0]
    n_chunk = n_tok // CHUNK

    def body(r_ref, lw_ref, k_ref, v_ref, a_ref, kk_ref, ka_ref, rk_ref, o_ref, bonus_ref, zs_ref, inv_ref, m_ref, q_ref, z):
        @pl.when(pl.program_id(0) == 0)
        def _():
            z[...] = jnp.zeros(z.shape, F32)

        m_all, n_all, q, oi, bonus, inv = _chunk_fn(r_ref[...], lw_ref[...], k_ref[...], v_ref[...], a_ref[...],
                                                    kk_ref[...], ka_ref[...], rk_ref[...], with_inverse=True)
        zb = z[...]
        zs_ref[0] = zb
        inv_ref[0] = inv.astype(inv_ref.dtype)
        m_ref[...] = m_all
        q_ref[...] = q
        bonus_ref[...] = bonus
        o_ref[...] = _batch_to_pairs(_dot_passes(_pairs_to_batch(q), zb, 'nn', STATE_PASSES)) + oi
        z[...] = _dot_passes(m_all[0], zb, 'nn', STATE_PASSES) + n_all[0]

    tok = _spec((CHUNK, D_MODEL), lambda c: (c, 0))
    par = _spec((1, D_MODEL), lambda c: (0, 0))
    st = _spec((1, N_PAIR, PAIR, PAIR), lambda c: (c, 0, 0, 0))
    return pl.pallas_call(
        body, name="rwkv_scan_fwd", grid=(n_chunk,),
        in_specs=[tok] * 5 + [par] * 3, out_specs=[tok, tok, st, st, st, tok],
        out_shape=[jax.ShapeDtypeStruct((n_tok, D_MODEL), F32)] * 2
        + [jax.ShapeDtypeStruct((n_chunk, N_PAIR, PAIR, PAIR), F32), jax.ShapeDtypeStruct((n_chunk, N_PAIR, PAIR, PAIR), BF16),
           jax.ShapeDtypeStruct((n_chunk, N_PAIR, PAIR, PAIR), F32), jax.ShapeDtypeStruct((n_tok, D_MODEL), F32)],
        scratch_shapes=[pltpu.VMEM((N_PAIR, PAIR, PAIR), F32)],
        compiler_params=_cparams(1),
    )(r, lw, k, v, a, k_k, k_a, r_k)


def _rwkv_scan_bwd(r, lw, k, v, a, k_k, k_a, r_k, zs, invs, ms, q, d_o, d_bonus):
    n_tok = r.shape[0]
    n_chunk = n_tok // CHUNK

    def body(r_ref, lw_ref, k_ref, v_ref, a_ref, kk_ref, ka_ref, rk_ref, zs_ref, inv_ref, m_ref, q_ref, do_ref, db_ref,
             dr_ref, dlw_ref, dk_ref, dv_ref, da_ref, dkk_ref, dka_ref, drk_ref, g):
        sums = (dkk_ref, dka_ref, drk_ref)

        @pl.when(pl.program_id(0) == 0)
        def _():
            g[...] = jnp.zeros(g.shape, F32)
            for s_ref in sums:
                s_ref[...] = jnp.zeros(s_ref.shape, F32)

        prim = (r_ref[...], lw_ref[...], k_ref[...], v_ref[...], a_ref[...], kk_ref[...], ka_ref[...], rk_ref[...])
        known = inv_ref[0].astype(F32)
        _, chunk_vjp = jax.vjp(lambda *p: _chunk_fn(*p, inverse=known), *prim)
        m_all, q = m_ref[...], q_ref[...]
        gb, zb, d_out = g[...], zs_ref[0], do_ref[...]
        dob = _pairs_to_batch(d_out)
        d_m = _dot_passes(gb, zb, 'nt', STATE_PASSES)
        d_q = _batch_to_pairs(_dot_passes(dob, zb, 'nt', STATE_PASSES))
        grads = chunk_vjp((d_m[None], gb[None], d_q, d_out, db_ref[...]))
        for out_ref, val in zip((dr_ref, dlw_ref, dk_ref, dv_ref, da_ref), grads[:5]):
            out_ref[...] = val.astype(out_ref.dtype)
        for s_ref, val in zip(sums, grads[5:]):
            s_ref[...] += val
        same_head = (lax.broadcasted_iota(jnp.int32, (1, PAIR, PAIR), 1) // HEAD
                     == lax.broadcasted_iota(jnp.int32, (1, PAIR, PAIR), 2) // HEAD)
        g[...] = (_dot_passes(m_all[0], gb, 'tn', STATE_PASSES)
                  + jnp.where(same_head, _dot_passes(_pairs_to_batch(q), dob, 'tn', STATE_PASSES), 0.0))

    tok = _spec((CHUNK, D_MODEL), lambda c: (n_chunk - 1 - c, 0))
    par = _spec((1, D_MODEL), lambda c: (0, 0))
    st = _spec((1, N_PAIR, PAIR, PAIR), lambda c: (n_chunk - 1 - c, 0, 0, 0))
    tok_shape = lambda dt: jax.ShapeDtypeStruct((n_tok, D_MODEL), dt)
    return pl.pallas_call(
        body, name="rwkv_scan_bwd", grid=(n_chunk,),
        in_specs=[tok] * 5 + [par] * 3 + [st, st, st, tok, tok, tok], out_specs=[tok] * 5 + [par] * 3,
        out_shape=[tok_shape(BF16), tok_shape(F32), tok_shape(BF16), tok_shape(BF16), tok_shape(F32)]
        + [jax.ShapeDtypeStruct((1, D_MODEL), F32)] * 3,
        scratch_shapes=[pltpu.VMEM((N_PAIR, PAIR, PAIR), F32)],
        compiler_params=_cparams(1),
    )(r, lw, k, v, a, k_k, k_a, r_k, zs, invs, ms, q, d_o, d_bonus)


S5_TIME_TILE = 1024
S5_GROUP = 8


def _scan_rows(re_ref, im_ref, a1, carry, reverse):
    lb = re_ref.shape[1]
    grp, n_grp = S5_GROUP, re_ref.shape[0] // S5_GROUP

    def cmul(xr, xi, yr, yi):
        return xr * yr - xi * yi, xr * yi + xi * yr

    a2 = cmul(*a1, *a1)
    a4 = cmul(*a2, *a2)
    a8 = cmul(*a4, *a4)
    row = lax.broadcasted_iota(jnp.int32, (grp, lb), 0)
    expo = (grp - row) if reverse else (row + 1)
    pw = (jnp.ones((grp, lb), F32), jnp.zeros((grp, lb), F32))
    for bit, ap in ((1, a1), (2, a2), (4, a4), (8, a8)):
        nxt = cmul(*pw, *ap)
        sel = (expo & bit) != 0
        pw = (jnp.where(sel, nxt[0], pw[0]), jnp.where(sel, nxt[1], pw[1]))

    steps = []
    for d, ad in ((1, a1), (2, a2), (4, a4)):
        keep = (row < grp - d) if reverse else (row >= d)
        steps.append(((grp - d) if reverse else d, jnp.where(keep, ad[0], 0.0), jnp.where(keep, ad[1], 0.0)))

    def group(i, c):
        gi = (n_grp - 1 - i) if reverse else i
        rows = pl.ds(pl.multiple_of(gi * grp, grp), grp)
        xr, xi = re_ref[rows, :], im_ref[rows, :]
        for shift, mr, mi in steps:
            pr, pi = cmul(mr, mi, pltpu.roll(xr, shift, 0), pltpu.roll(xi, shift, 0))
            xr, xi = xr + pr, xi + pi
        cr, ci = cmul(*pw, c[0], c[1])
        xr, xi = xr + cr, xi + ci
        re_ref[rows, :] = xr
        im_ref[rows, :] = xi
        edge = slice(0, 1) if reverse else slice(grp - 1, grp)
        return xr[edge, :], xi[edge, :]

    return lax.fori_loop(0, n_grp, group, carry)


def _s5_fwd(u, b_re, b_im, a_re, a_im, c_re, c_im, d_row):
    n_tok = u.shape[0]
    tt = min(S5_TIME_TILE, n_tok)

    def body(u_ref, bre_ref, bim_ref, are_ref, aim_ref, cre_ref, cim_ref, d_ref, sre_ref, sim_ref, yg_ref, carry):
        @pl.when(pl.program_id(1) == 0)
        def _():
            carry[...] = jnp.zeros(carry.shape, F32)

        uv = u_ref[...]
        sre_ref[...], sim_ref[...] = _s5_in_fn(uv, bre_ref[...], bim_ref[...])
        c = _scan_rows(sre_ref, sim_ref, (are_ref[...], aim_ref[...]), (carry[0:1, :], carry[1:2, :]), False)
        carry[0:1, :] = c[0]
        carry[1:2, :] = c[1]
        (yg,) = _s5_out_fn(sre_ref[...], sim_ref[...], uv, cre_ref[...], cim_ref[...], d_ref[...])
        yg_ref[...] = yg.astype(yg_ref.dtype)

    u_blk = _spec((tt, SSM_BLK_IN), lambda l, t: (t, l))
    s_blk = _spec((tt, SSM_BLK_ST), lambda l, t: (t, l))
    blk3 = lambda arr: _spec((1,) + arr.shape[1:], lambda l, t: (l, 0, 0))
    par = lambda width: _spec((1, width), lambda l, t: (0, l))
    return pl.pallas_call(
        body, name="s5_fwd", grid=(N_SSM_BLK, n_tok // tt),
        in_specs=[u_blk, blk3(b_re), blk3(b_im), par(SSM_BLK_ST), par(SSM_BLK_ST), blk3(c_re), blk3(c_im), par(SSM_BLK_IN)],
        out_specs=[s_blk, s_blk, u_blk],
        out_shape=[jax.ShapeDtypeStruct((n_tok, SSM_LANES), F32)] * 2 + [jax.ShapeDtypeStruct((n_tok, D_MODEL), BF16)],
        scratch_shapes=[pltpu.VMEM((2, SSM_BLK_ST), F32)],
        compiler_params=_cparams(2),
    )(u, b_re, b_im, a_re, a_im, c_re, c_im, d_row)


def _s5_bwd(dyg, u, s_re, s_im, dh_res, b_re, b_im, a_re, a_im, c_re, c_im, d_row):
    n_tok = u.shape[0]
    tt = min(S5_TIME_TILE, n_tok)
    n_t = n_tok // tt

    def body(dyg_ref, u_ref, sre_ref, sim_ref, hre_ref, him_ref, res_ref, bre_ref, bim_ref, are_ref, aim_ref,
             cre_ref, cim_ref, d_ref, du_ref, dbre_ref, dbim_ref, dcre_ref, dcim_ref, dd_ref, dare_ref, daim_ref,
             carry, gre, gim):
        i = pl.program_id(1)
        sums = (dbre_ref, dbim_ref, dcre_ref, dcim_ref, dd_ref, dare_ref, daim_ref)

        @pl.when(i == 0)
        def _():
            carry[...] = jnp.zeros(carry.shape, F32)
            for r in sums:
                r[...] = jnp.zeros(r.shape, F32)

        uv, sre, sim = u_ref[...], sre_ref[...], sim_ref[...]
        _, out_vjp = jax.vjp(_s5_out_fn, sre, sim, uv, cre_ref[...], cim_ref[...], d_ref[...])
        gre[...], gim[...], du_out, dcre, dcim, dd = out_vjp((dyg_ref[...],))
        c = _scan_rows(gre, gim, (are_ref[...], -aim_ref[...]), (carry[0:1, :], carry[1:2, :]), True)
        carry[0:1, :] = c[0]
        carry[1:2, :] = c[1]
        g_re, g_im = gre[...], gim[...]
        at_start = i == n_t - 1
        p_re = _shift_down(sre, jnp.where(at_start, 0.0, hre_ref[7:8, :]))
        p_im = _shift_down(sim, jnp.where(at_start, 0.0, him_ref[7:8, :]))
        _, in_vjp = jax.vjp(_s5_in_fn, uv, bre_ref[...], bim_ref[...])
        du_in, dbre, dbim = in_vjp((g_re, g_im))
        du_ref[...] = du_out + du_in + res_ref[...]
        dare = jnp.sum(g_re * p_re + g_im * p_im, axis=0, keepdims=True)
        daim = jnp.sum(g_im * p_re - g_re * p_im, axis=0, keepdims=True)
        for r, val in zip(sums, (dbre, dbim, dcre, dcim, dd, dare, daim)):
            r[...] += val

    u_blk = _spec((tt, SSM_BLK_IN), lambda l, t: (n_t - 1 - t, l))
    s_blk = _spec((tt, SSM_BLK_ST), lambda l, t: (n_t - 1 - t, l))
    halo = _spec((8, SSM_BLK_ST), lambda l, t: (jnp.maximum((n_t - 1 - t) * (tt // 8) - 1, 0), l))
    blk3 = lambda arr: _spec((1,) + arr.shape[1:], lambda l, t: (l, 0, 0))
    par = lambda width: _spec((1, width), lambda l, t: (0, l))
    params = [blk3(b_re), blk3(b_im), par(SSM_BLK_ST), par(SSM_BLK_ST), blk3(c_re), blk3(c_im), par(SSM_BLK_IN)]
    shape = lambda arr: jax.ShapeDtypeStruct(arr.shape, F32)
    return pl.pallas_call(
        body, name="s5_bwd", grid=(N_SSM_BLK, n_t),
        in_specs=[u_blk, u_blk, s_blk, s_blk, halo, halo, u_blk] + params,
        out_specs=[u_blk, blk3(b_re), blk3(b_im), blk3(c_re), blk3(c_im), par(SSM_BLK_IN), par(SSM_BLK_ST), par(SSM_BLK_ST)],
        out_shape=[shape(u), shape(b_re), shape(b_im), shape(c_re), shape(c_im), shape(d_row), shape(a_re), shape(a_im)],
        scratch_shapes=[pltpu.VMEM((2, SSM_BLK_ST), F32), pltpu.VMEM((tt, SSM_BLK_ST), F32), pltpu.VMEM((tt, SSM_BLK_ST), F32)],
        compiler_params=_cparams(2),
    )(dyg, u, s_re, s_im, s_re, s_im, dh_res, b_re, b_im, a_re, a_im, c_re, c_im, d_row)


def _loss_call(h, target, tm):
    n_tok, d = h.shape

    def body(h_ref, t_ref, acc_ref, dh_ref):
        @pl.when(pl.program_id(0) == 0)
        def _():
            acc_ref[...] = jnp.zeros(acc_ref.shape, F32)

        e = h_ref[...] - t_ref[...]
        dh_ref[...] = e * (1.0 / d)
        acc_ref[...] += jnp.sum(jnp.sum(e * e, axis=1, keepdims=True), axis=0, keepdims=True)

    tok = _spec((tm, d), lambda i: (i, 0))
    return pl.pallas_call(
        body, name="loss", grid=(n_tok // tm,),
        in_specs=[tok, tok], out_specs=[_spec((8, 128), lambda i: (0, 0)), tok],
        out_shape=[jax.ShapeDtypeStruct((8, 128), F32), jax.ShapeDtypeStruct(h.shape, F32)],
        compiler_params=_cparams(1),
    )(h, target)


def _block_diag(t):
    nb, ng, rows, cols = t.shape
    eye = jnp.eye(ng, dtype=t.dtype)
    return jnp.einsum('bgrc,gh->bgrhc', t, eye).reshape(nb, ng * rows, ng * cols)


def _block_diag_t(t, rows, cols):
    nb, ng = t.shape[0], t.shape[1] // rows
    t5 = t.reshape(nb, ng, rows, ng, cols)
    return jnp.einsum('bgrhc,gh->bgrc', t5, jnp.eye(ng, dtype=t.dtype))


def _mlp_fwd(layer, h, w1g, w2g):
    pre = _matmul(f"mlp{layer}_up", h, w1g, 'nn', b_cols=(layer,), out_dtype=BF16)
    return pre, _matmul(f"mlp{layer}_down", pre, w2g, 'nn', lhs_fn=_relu2, b_rows=(layer,))


def _mlp_bwd(layer, h, pre, dy, dh_res, w1g, w2g):
    dpre = _matmul(f"mlp{layer}_down_dx", dy, w2g, 'nt', epi=lambda acc, p: acc * (2.0 * jnp.maximum(p, 0.0)),
                   extra=pre, b_rows=(layer,), out_dtype=BF16)
    dw2 = _matmul(f"mlp{layer}_down_dw", pre, dy, 'tn', lhs_fn=_relu2, out_dtype=GRAD_DTYPE)
    dw1 = _matmul(f"mlp{layer}_up_dw", h, dpre, 'tn', out_cols=True, out_dtype=GRAD_DTYPE)
    dh = _matmul(f"mlp{layer}_up_dx", dpre, w1g, 'nt', epi=lambda acc, e: acc + e, extra=dh_res, b_cols=(layer,))
    return dh, dw1, dw2


def _local_step(x, target, w, mid, start_late, exchange):
    n_tok = x.shape[0]
    tm = min(TOKEN_TILE, n_tok)
    n_tile, n_chunk = n_tok // tm, n_tok // CHUNK
    row = lambda v: v.reshape(1, -1)
    big = {}
    ln_g, ln_b = w['ln_g'], w['ln_b']
    grads = {}
    to_slots = lambda t: t.reshape(N_DEV, t.shape[0] // N_DEV, t.shape[1])

    halo_spec = (8, D_MODEL), lambda i: (jnp.maximum(i * (tm // 8) - 1, 0), 0)
    mix_p = [_tile(x, tm), (x,) + halo_spec, _full(w['rw_mu'][0]), _full(row(w['rw_w0'])), _full(w['rw_w1'][0]),
             _full(w['rw_w2'][0]), _full(row(w['rw_a0'])), _full(w['rw_a1'][0]), _full(w['rw_a2'][0]),
             _full(w['rw_g1'][0]), _full(w['rw_g2'][0])]
    tok_out = lambda dt: ((n_tok, D_MODEL), dt, (tm, D_MODEL), lambda i: (i, 0))
    xr, xk, xv, lw, a, g = _fwd_call("rwkv_mix", _mix_fn, (n_tile,), mix_p,
                                     [tok_out(BF16), tok_out(BF16), tok_out(BF16), tok_out(F32), tok_out(F32), tok_out(F32)])
    mid.forward(xr)
    square = lambda t: t.reshape(D_MODEL, D_MODEL)
    big.update({n: square(t) for n, t in mid.finish(xr).items()})
    late = start_late(big['rw_wr'])
    r = _matmul("rwkv_r", xr, big['rw_wr'], 'nn')
    k = _matmul("rwkv_k", xk, big['rw_wk'], 'nn')
    v = _matmul("rwkv_v", xv, big['rw_wv'], 'nn')

    scan_in = (r, lw, k, v, a, row(w['rw_k_k']) + late.zero, row(w['rw_k_a']), w['rw_r_k'].reshape(1, -1))
    o, bonus, *scan_saved = _rwkv_scan_fwd(*scan_in)

    zero = late.forward(o)
    gate_p = [_tile(o, tm), _tile(bonus, tm), _tile(g, tm), _full(row(w['rw_lnx_g']) + zero), _full(row(w['rw_lnx_b']))]
    (og,) = _fwd_call("rwkv_gate", _gate_fn, (n_tile,), gate_p, [tok_out(BF16)])
    late_w = late.finish(og)
    big['rw_wo'] = square(late_w['rw_wo'])
    w1g, w2g, glu_g = late_w['mlp_w1'], late_w['mlp_w2'], late_w['s5_w_glu']
    y0 = _matmul("rwkv_o", og, big['rw_wo'], 'nn')
    tl = min(LN_TILE, n_tok)
    n_ln = n_tok // tl
    ln_out = lambda dt: ((n_tok, D_MODEL), dt, (tl, D_MODEL), lambda i: (i, 0))
    ln0_p = [_tile(x, tl), _tile(y0, tl), _full(ln_g[0:1]), _full(ln_b[0:1])]
    twice = lambda fn: lambda *args: fn(*args) * 2
    h1, h1_b = _fwd_call("ln0", twice(_res_ln_fn), (n_ln,), ln0_p, [ln_out(F32), ln_out(BF16)])
    pre0, y1 = _mlp_fwd(0, h1_b, w1g, w2g)
    ln1_p = [_tile(h1, tl), _tile(y1, tl), _full(ln_g[1:2]), _full(ln_b[1:2])]
    (h2,) = _fwd_call("ln1", _res_ln_fn, (n_ln,), ln1_p, [ln_out(F32)])

    gp = (SSM_GROUPS, SSM_STATE)
    par_p = [_full(w['s5_a_re'][0]), _full(w['s5_a_im'][0]), _full(w['s5_log_dt'].reshape(SSM_GROUPS, 1))]
    s5_par_fn = lambda ar, ai, ld: _s5_param_fn(ar, ai, ld, None, None)
    gp_out = (gp, F32, gp, lambda i: (0, 0))
    abar_re, abar_im, coef_re, coef_im = _fwd_call("s5_param", s5_par_fn, (1,), par_p, [gp_out] * 4)
    b_flat = [w[n][0].reshape(SSM_LANES, SSM_GROUP) for n in ('s5_b_re', 's5_b_im')]
    bbar_p = [_full(coef_re.reshape(SSM_LANES, 1)), _full(coef_im.reshape(SSM_LANES, 1)), _full(b_flat[0]), _full(b_flat[1])]
    bb_out = ((SSM_LANES, SSM_GROUP), F32, (SSM_LANES, SSM_GROUP), lambda i: (0, 0))
    bbar_re, bbar_im = _fwd_call("s5_bbar", _s5_bbar_fn, (1,), bbar_p, [bb_out] * 2)
    to_in = lambda t: _block_diag(t.reshape(N_SSM_BLK, 8, SSM_STATE, SSM_GROUP).transpose(0, 1, 3, 2))
    to_out = lambda t: _block_diag(t.reshape(N_SSM_BLK, 8, SSM_GROUP, SSM_STATE).transpose(0, 1, 3, 2))
    bblk_re, bblk_im = to_in(bbar_re), to_in(bbar_im)
    cblk_re, cblk_im = to_out(w['s5_c_re'][0]), to_out(w['s5_c_im'][0])
    a_row_re, a_row_im = abar_re.reshape(1, SSM_LANES), abar_im.reshape(1, SSM_LANES)

    d_row = row(w['s5_d'])
    s5_params = (bblk_re, bblk_im, a_row_re, a_row_im, cblk_re, cblk_im, d_row)
    s_re, s_im, yg = _s5_fwd(h2, *s5_params)
    z = _matmul("s5_glu", yg, glu_g, 'nn', b_cols=(0,))
    ln2_p = [_tile(h2, tl), _tile(z, tl), _full(ln_g[2:3]), _full(ln_b[2:3])]
    h3, h3_b = _fwd_call("ln2", twice(_glu_ln_fn), (n_ln,), ln2_p, [ln_out(F32), ln_out(BF16)])
    pre1, y3 = _mlp_fwd(1, h3_b, w1g, w2g)
    ln3_p = [_tile(h3, tl), _tile(y3, tl), _full(ln_g[3:4]), _full(ln_b[3:4])]
    (h4,) = _fwd_call("ln3", _res_ln_fn, (n_ln,), ln3_p, [ln_out(F32)])

    loss_acc, dh4 = _loss_call(h4, target, tl)

    ln_w = ['tile', 'tile_bf16', 'acc', 'acc']
    dh3_res, dy3, dg3, db3 = _bwd_call("ln3_bwd", _res_ln_fn, (n_ln,), ln3_p, [_tile(dh4, tl)], ln_w)
    dh3, dw1_1, dw2_1 = _mlp_bwd(1, h3_b, pre1, dy3, dh3_res, w1g, w2g)
    zero = exchange([('mlp_w1', 1, dw1_1), ('mlp_w2', 1, to_slots(dw2_1))])
    ln2_p[2] = _full(ln_g[2:3] + zero)
    dh2_res, dz, dg2, db2 = _bwd_call("ln2_bwd", _glu_ln_fn, (n_ln,), ln2_p, [_tile(dh3, tl)], ln_w)
    dyg = _matmul("s5_glu_dx", dz, glu_g, 'nt', b_cols=(0,))
    dw_glu = _matmul("s5_glu_dw", yg, dz, 'tn', out_cols=True, out_dtype=GRAD_DTYPE)
    dh2, dbb_re, dbb_im, dcb_re, dcb_im, dd, dabar_re, dabar_im = _s5_bwd(dyg, h2, s_re, s_im, dh2_res, *s5_params)
    from_in = lambda t: _block_diag_t(t, SSM_GROUP, SSM_STATE).transpose(0, 1, 3, 2).reshape(SSM_LANES, SSM_GROUP)
    from_out = lambda t: _block_diag_t(t, SSM_STATE, SSM_GROUP).transpose(0, 1, 3, 2).reshape(1, SSM_GROUPS, SSM_GROUP, SSM_STATE)
    grads['s5_c_re'], grads['s5_c_im'] = from_out(dcb_re), from_out(dcb_im)
    grads['s5_d'] = dd
    dcoef_re, dcoef_im, db_re, db_im = _bwd_call(
        "s5_bbar_bwd", _s5_bbar_fn, (1,), bbar_p, [_full(from_in(dbb_re)), _full(from_in(dbb_im))], ['acc'] * 4)
    grads['s5_b_re'] = db_re.reshape(w['s5_b_re'].shape)
    grads['s5_b_im'] = db_im.reshape(w['s5_b_im'].shape)
    par_ct = [_full(dabar_re.reshape(gp)), _full(dabar_im.reshape(gp)), _full(dcoef_re.reshape(gp)), _full(dcoef_im.reshape(gp))]
    da_re, da_im, dlog_dt = _bwd_call("s5_param_bwd", s5_par_fn, (1,), par_p, par_ct, ['acc'] * 3)
    grads['s5_a_re'], grads['s5_a_im'], grads['s5_log_dt'] = da_re[None], da_im[None], dlog_dt.reshape(1, SSM_GROUPS)
    zero = exchange([(n, 0, grads[n]) for n in ('s5_a_re', 's5_a_im', 's5_log_dt', 's5_b_re', 's5_b_im', 's5_c_re', 's5_c_im')])
    ln1_p[2] = _full(ln_g[1:2] + zero)

    dh1_res, dy1, dg1, db1 = _bwd_call("ln1_bwd", _res_ln_fn, (n_ln,), ln1_p, [_tile(dh2, tl)], ln_w)
    dh1, dw1_0, dw2_0 = _mlp_bwd(0, h1_b, pre0, dy1, dh1_res, w1g, w2g)
    dx_res, dy0, dg0, db0 = _bwd_call("ln0_bwd", _res_ln_fn, (n_ln,), ln0_p, [_tile(dh1, tl)], ln_w)
    grads['ln_g'] = jnp.concatenate([dg0, dg1, dg2, dg3])
    grads['ln_b'] = jnp.concatenate([db0, db1, db2, db3])
    dog = _matmul("rwkv_o_dx", dy0, big['rw_wo'], 'nt')
    dw_o = to_slots(_matmul("rwkv_o_dw", og, dy0, 'tn', out_dtype=GRAD_DTYPE))
    zero = exchange([('s5_w_glu', 0, dw_glu), ('mlp_w1', 0, dw1_0), ('mlp_w2', 0, to_slots(dw2_0)), ('rw_wo', 0, dw_o)])
    gate_p[3] = _full(row(w['rw_lnx_g']) + zero)
    d_o, dbonus, dgate, grads['rw_lnx_g'], grads['rw_lnx_b'] = _bwd_call(
        "rwkv_gate_bwd", _gate_fn, (n_tile,), gate_p, [_tile(dog, tm)], ['tile', 'tile', 'tile', 'acc', 'acc'])
    dr, dlw, dk, dv, da, grads['rw_k_k'], grads['rw_k_a'], dr_k = _rwkv_scan_bwd(*scan_in, *scan_saved, d_o, dbonus)
    grads['rw_r_k'] = dr_k.reshape(w['rw_r_k'].shape)
    dw_r = to_slots(_matmul("rwkv_r_dw", xr, dr, 'tn', out_dtype=GRAD_DTYPE))
    dw_k = to_slots(_matmul("rwkv_k_dw", xk, dk, 'tn', out_dtype=GRAD_DTYPE))
    dw_v = to_slots(_matmul("rwkv_v_dw", xv, dv, 'tn', out_dtype=GRAD_DTYPE))
    done = ('ln_g', 'ln_b', 'rw_lnx_g', 'rw_lnx_b', 'rw_k_k', 'rw_k_a', 'rw_r_k')
    zero = exchange([('rw_wr', 0, dw_r), ('rw_wk', 0, dw_k), ('rw_wv', 0, dw_v)] + [(n, 0, grads[n]) for n in done])
    mix_p[3] = _full(row(w['rw_w0']) + zero)
    dxr = _matmul("rwkv_r_dx", dr, big['rw_wr'], 'nt')
    dxk = _matmul("rwkv_k_dx", dk, big['rw_wk'], 'nt')
    dxv = _matmul("rwkv_v_dx", dv, big['rw_wv'], 'nt')
    mix_ct = [_tile(t, tm) for t in (dxr, dxk, dxv, dlw, da, dgate)]
    halo_grad = ('tile', ((n_tile * 8, D_MODEL), (8, D_MODEL), lambda i: (i, 0)))
    res = _bwd_call("rwkv_mix_bwd", _mix_fn, (n_tile,), mix_p, mix_ct, ['tile', halo_grad] + ['acc'] * 9,
                    addends={0: _tile(dx_res, tm)})
    corr = res[1].reshape(n_tile, 8, D_MODEL)[1:, 7:8, :]
    corr = jnp.pad(corr, ((0, 1), (tm - 1, 0), (0, 0)))
    dx = (res[0].reshape(n_tile, tm, D_MODEL) + corr).reshape(n_tok, D_MODEL)
    (grads['rw_mu'], grads['rw_w0'], grads['rw_w1'], grads['rw_w2'], grads['rw_a0'], grads['rw_a1'], grads['rw_a2'],
     grads['rw_g1'], grads['rw_g2']) = [t[None] if t.shape[0] != 1 else t for t in res[2:]]
    return loss_acc[0, 0], dx, grads


def _full_shape(local_shape, axis):
    return local_shape[:axis] + (N_DEV * local_shape[axis],) + local_shape[axis + 1:]


def _split_shape(local_shape, axis):
    return local_shape[:axis] + (N_DEV, local_shape[axis]) + local_shape[axis + 1:]


def _mesh_pos():
    return lax.axis_index("x"), lax.axis_index("y"), lax.axis_index("c")


def _flip(v, f):
    return 1 - v if f else v


def _hbm_call(name, body, arrays, out_shapes, n_sems):
    n_t = len(arrays)
    hbm = pl.BlockSpec(memory_space=pl.ANY)
    return pl.pallas_call(
        body, name=name, out_shape=out_shapes, in_specs=[hbm] * n_t, out_specs=[hbm] * n_t,
        scratch_shapes=[pltpu.SemaphoreType.DMA((n_t, n_sems)), pltpu.SemaphoreType.DMA((n_t, n_sems)),
                        pltpu.SemaphoreType.DMA((n_t,))],
    )(*arrays)


def _all_gather(name, locals_):
    n_t = len(locals_)

    def body(*refs):
        x_refs, out_refs = refs[:n_t], refs[n_t:2 * n_t]
        send_sems, recv_sems, local_sems = refs[2 * n_t:]
        x, y, c = _mesh_pos()
        me, sibling = (x, y, c), (x, y, 1 - c)
        chips = [(1 - x, y), (x, 1 - y), (1 - x, 1 - y)]

        def copy(t, k, block, to, own=False):
            slot = out_refs[t].at[4 * block[0] + 2 * block[1] + block[2]]
            return pltpu.make_async_remote_copy(
                src_ref=x_refs[t] if own else slot, dst_ref=slot,
                send_sem=send_sems.at[t, k], recv_sem=recv_sems.at[t, k],
                device_id=to, device_id_type=pl.DeviceIdType.MESH)

        mine = [pltpu.make_async_copy(x_refs[t], out_refs[t].at[4 * x + 2 * y + c], local_sems.at[t]) for t in range(n_t)]
        for cp in mine:
            cp.start()
        sent = []
        for t in range(n_t):
            sent.append(copy(t, 0, me, sibling, own=True))
            sent += [copy(t, 1 + j, me, (*chip, c), own=True) for j, chip in enumerate(chips)]
        for cp in sent:
            cp.start()
        for j, chip in enumerate(chips):
            for t in range(n_t):
                copy(t, 1 + j, (*chip, c), me).wait_recv()
                passed = copy(t, 4 + j, (*chip, c), sibling)
                passed.start()
                sent.append(passed)
        for t in range(n_t):
            copy(t, 0, sibling, me).wait_recv()
            for j, chip in enumerate(chips):
                copy(t, 4 + j, (*chip, 1 - c), me).wait_recv()
        for cp in sent:
            cp.wait_send()
        for cp in mine:
            cp.wait()

    outs = [jax.ShapeDtypeStruct((N_DEV,) + a.shape, a.dtype) for a in locals_]
    return _hbm_call(name, body, locals_, outs, 7)


def _all_to_all(name, blocks):
    n_t = len(blocks)

    def body(*refs):
        x_refs, out_refs = refs[:n_t], refs[n_t:2 * n_t]
        send_sems, recv_sems, local_sems = refs[2 * n_t:]
        x, y, c = _mesh_pos()
        my_slot = 4 * x + 2 * y + c
        mine = [pltpu.make_async_copy(x_refs[t].at[my_slot], out_refs[t].at[my_slot], local_sems.at[t]) for t in range(n_t)]
        for cp in mine:
            cp.start()
        copies = []
        for k in range(1, N_DEV):
            peer = (_flip(x, k & 4), _flip(y, k & 2), _flip(c, k & 1))
            peer_slot = 4 * peer[0] + 2 * peer[1] + peer[2]
            for t in range(n_t):
                sems = dict(send_sem=send_sems.at[t, k - 1], recv_sem=recv_sems.at[t, k - 1],
                            device_id=peer, device_id_type=pl.DeviceIdType.MESH)
                cp = pltpu.make_async_remote_copy(src_ref=x_refs[t].at[peer_slot], dst_ref=out_refs[t].at[my_slot], **sems)
                cp.start()
                landed = pltpu.make_async_remote_copy(src_ref=x_refs[t].at[my_slot], dst_ref=out_refs[t].at[peer_slot], **sems)
                copies.append((cp, landed))
        for sent, landed in copies:
            landed.wait_recv()
            sent.wait_send()
        for cp in mine:
            cp.wait()

    outs = [jax.ShapeDtypeStruct(a.shape, a.dtype) for a in blocks]
    return _hbm_call(name, body, blocks, outs, 7)


_HBM = pl.BlockSpec(memory_space=pltpu.HBM)
_SEM = pl.BlockSpec(memory_space=pltpu.SEMAPHORE)
_EFFECT = pltpu.SideEffectType.DATAFLOW_SIDE_EFFECTING


def _peer_copies(x_refs, land_refs, send_sems, recv_sems, gather, mine):
    x, y, c = _mesh_pos()
    my_slot = 4 * x + 2 * y + c
    copies = []
    for k in range(1, N_DEV):
        peer = (_flip(x, k & 4), _flip(y, k & 2), _flip(c, k & 1))
        peer_slot = 4 * peer[0] + 2 * peer[1] + peer[2]
        for t, (x_ref, land) in enumerate(zip(x_refs, land_refs)):
            sem = t * (N_DEV - 1) + k - 1
            copies.append(pltpu.make_async_remote_copy(
                src_ref=x_ref if gather[t] else x_ref.at[peer_slot], dst_ref=land.at[my_slot if mine else peer_slot],
                send_sem=send_sems.at[sem], recv_sem=recv_sems.at[sem], device_id=peer, device_id_type=pl.DeviceIdType.MESH))
    return copies


def _split_start(name, held, plan, n_copies, after=None):
    n_h = len(held)

    n_in = n_h + (after is not None)

    def body(*refs):
        for sent in plan(refs[:n_h], refs[n_in], refs[n_in + 1], True):
            sent.start()
        refs[-1][...] = jnp.zeros((8, 128), F32)

    sems = pltpu.SemaphoreType.DMA((n_copies,))
    out = pl.pallas_call(
        body, name=name,
        out_shape=(sems, sems, *[pltpu.HBM(a.shape, a.dtype) for a in held], jax.ShapeDtypeStruct((8, 128), F32)),
        in_specs=[_HBM] * n_h + [pl.BlockSpec(memory_space=pl.ANY)] * (after is not None),
        out_specs=(_SEM, _SEM, *[_HBM] * n_h, pl.BlockSpec(memory_space=pltpu.VMEM)),
        input_output_aliases={i: 2 + i for i in range(n_h)},
        compiler_params=pltpu.CompilerParams(has_side_effects=_EFFECT),
    )(*[pltpu.with_memory_space_constraint(a, pltpu.HBM) for a in held], *([after] if after is not None else []))
    return dict(send=out[0], recv=out[1], held=out[2:2 + n_h], zero=out[-1][0, 0], plan=plan)


def _split_wait(name, started, after):
    held, plan = started['held'], started['plan']
    n_h = len(held)

    def body(*refs):
        for sent in plan(refs[:n_h], refs[n_h], refs[n_h + 1], True):
            sent.wait_send()
        for landed in plan(refs[:n_h], refs[n_h], refs[n_h + 1], False):
            landed.wait_recv()

    return pl.pallas_call(
        body, name=name,
        out_shape=[pltpu.HBM(a.shape, a.dtype) for a in held],
        in_specs=[_HBM] * n_h + [_SEM, _SEM, pl.BlockSpec(memory_space=pl.ANY)],
        out_specs=[_HBM] * n_h,
        input_output_aliases={i: i for i in range(n_h)},
        compiler_params=pltpu.CompilerParams(has_side_effects=_EFFECT),
    )(*held, started['send'], started['recv'], after)


def _push_start(name, arrays, gather):
    n_t = len(arrays)
    lands = [lax.empty((N_DEV,) + a.shape if whole else a.shape, a.dtype) for a, whole in zip(arrays, gather)]
    plan = lambda refs, send, recv, mine: _peer_copies(refs[:n_t], refs[n_t:], send, recv, gather, mine)
    return _split_start(name, list(arrays) + lands, plan, n_t * (N_DEV - 1))


def _push_wait(name, started, after):
    held = _split_wait(name, started, after)
    return held[:len(held) // 2], held[len(held) // 2:]


def _chip_copies(refs, send_sems, recv_sems, mine):
    n_t = len(refs) // 2
    x, y, c = _mesh_pos()
    my_slot = 4 * x + 2 * y + c
    peers = [(x, y, 1 - c), (1 - x, y, c), (x, 1 - y, c), (1 - x, 1 - y, c)]
    copies = []
    for j, peer in enumerate(peers):
        peer_slot = 4 * peer[0] + 2 * peer[1] + peer[2]
        for t in range(n_t):
            copies.append(pltpu.make_async_remote_copy(
                src_ref=refs[t], dst_ref=refs[n_t + t].at[my_slot if mine else peer_slot],
                send_sem=send_sems.at[4 * t + j], recv_sem=recv_sems.at[4 * t + j],
                device_id=peer, device_id_type=pl.DeviceIdType.MESH))
    return copies


def _forward_copies(refs, send_sems, recv_sems, mine):
    x, y, c = _mesh_pos()
    sibling = (x, y, 1 - c)
    copies = []
    for j, (px, py) in enumerate([(1 - x, y), (x, 1 - y), (1 - x, 1 - y)]):
        slot = 4 * px + 2 * py + (c if mine else 1 - c)
        for t, land in enumerate(refs):
            copies.append(pltpu.make_async_remote_copy(
                src_ref=land.at[slot], dst_ref=land.at[slot], send_sem=send_sems.at[3 * t + j], recv_sem=recv_sems.at[3 * t + j],
                device_id=sibling, device_id_type=pl.DeviceIdType.MESH))
    return copies


class _TwoStageGather:
    def __init__(self, name, arrays, my_slot, after=None):
        self.name, self.arrays, self.my_slot = name, arrays, my_slot
        values = list(arrays.values())
        lands = [lax.empty((N_DEV,) + a.shape, a.dtype) for a in values]
        self.stage = _split_start(name + "_start", values + lands, _chip_copies, 4 * len(values), after)
        self.zero = self.stage['zero']

    def forward(self, after):
        held = _split_wait(self.name + "_wait", self.stage, after)
        self.stage = _split_start(self.name + "_forward_start", held[len(self.arrays):], _forward_copies, 3 * len(self.arrays))
        return self.stage['zero']

    def finish(self, after):
        lands = _split_wait(self.name + "_forward_wait", self.stage, after)
        return {n: lax.dynamic_update_slice(land, a[None], (self.my_slot,) + (0,) * a.ndim)
                for (n, a), land in zip(self.arrays.items(), lands)}


ADAM_BLOCK = 128 * 1024


def _adam_update(part, w_ref, m_ref, v_ref, g_ref, d_ref, nm_ref, nv_ref):
    g = part(0)
    for i in range(1, N_DEV):
        g = g + part(i)
    m_new = ADAM_B1 * m_ref[...] + (1.0 - ADAM_B1) * g
    v_new = ADAM_B2 * v_ref[...] + (1.0 - ADAM_B2) * (g * g)
    m_hat = m_new * (1.0 / (1.0 - ADAM_B1 ** ADAM_STEP))
    v_hat = v_new * (1.0 / (1.0 - ADAM_B2 ** ADAM_STEP))
    g_ref[...] = g
    nm_ref[...] = m_new
    nv_ref[...] = v_new
    d_ref[...] = -ADAM_LR * (m_hat / (jnp.sqrt(v_hat) + ADAM_EPS) + ADAM_WD * w_ref[...])


def _adam_many(name, items):
    n_t = len(items)
    n_in = [4 if own is None else 5 for *_, own in items]
    first = [sum(n_in[:t]) for t in range(n_t)]

    def body(*refs):
        ins, outs = refs[:sum(n_in)], refs[sum(n_in):]
        x, y, c = _mesh_pos()
        my_slot = 4 * x + 2 * y + c
        for t in range(n_t):
            s_ref, w_ref, m_ref, v_ref = ins[first[t]:first[t] + 4]
            if n_in[t] == 4:
                part = lambda i: s_ref[i]
            else:
                own_ref = ins[first[t] + 4]
                part = lambda i: jnp.where(my_slot == i, own_ref[...], s_ref[i])
            _adam_update(part, w_ref, m_ref, v_ref, *outs[4 * t:4 * t + 4])

    vmem = pl.BlockSpec(memory_space=pltpu.VMEM)
    out = pl.pallas_call(
        body, name=name, in_specs=[vmem] * sum(n_in), out_specs=[vmem] * (4 * n_t),
        out_shape=[jax.ShapeDtypeStruct(item[1].shape, F32) for item in items for _ in range(4)],
        compiler_params=pltpu.CompilerParams(vmem_limit_bytes=VMEM_LIMIT),
    )(*[a for item in items for a in item if a is not None])
    return [out[4 * t:4 * t + 4] for t in range(n_t)]


def _adam_call(name, slots, w, m, v, own=None):
    n_layer, rows, width = w.shape
    tile = min(rows, ADAM_BLOCK // max(width, 128))

    def body(*refs):
        s_refs = refs[:n_layer]
        own_refs = refs[n_layer:2 * n_layer] if own is not None else None
        w_ref, m_ref, v_ref = refs[-7:-4]
        layer = pl.program_id(0)
        x, y, c = _mesh_pos()
        my_slot = 4 * x + 2 * y + c
        for l in range(n_layer):
            @pl.when(layer == l)
            def _(l=l):
                if own is None:
                    part = lambda i: s_refs[l][i].astype(F32)
                else:
                    part = lambda i: jnp.where(my_slot == i, own_refs[l][...], s_refs[l][i]).astype(F32)
                _adam_update(part, w_ref, m_ref, v_ref, *refs[-4:])

    slot_spec = lambda k: _spec((N_DEV, tile, width), lambda l, i: (0, jnp.where(l == k, i, 0), 0))
    own_spec = lambda k: _spec((tile, width), lambda l, i: (jnp.where(l == k, i, 0), 0))
    blk = _spec((None, tile, width), lambda l, i: (l, i, 0))
    specs = [slot_spec(k) for k in range(n_layer)] + ([own_spec(k) for k in range(n_layer)] if own is not None else [])
    return pl.pallas_call(
        body, name=name, grid=(n_layer, rows // tile),
        in_specs=specs + [blk] * 3, out_specs=[blk] * 4, out_shape=[jax.ShapeDtypeStruct(w.shape, F32)] * 4,
        compiler_params=_cparams(2),
    )(*slots, *(own or []), w, m, v)


def kernel(x, ln_g, ln_b, rw_mu, rw_w0, rw_w1, rw_w2, rw_a0, rw_a1, rw_a2, rw_g1, rw_g2, rw_k_k, rw_k_a, rw_r_k, rw_wr, rw_wk, rw_wv, rw_wo, rw_lnx_g, rw_lnx_b, s5_a_re, s5_a_im, s5_log_dt, s5_b_re, s5_b_im, s5_c_re, s5_c_im, s5_d, s5_w_glu, mlp_w1, mlp_w2, loss_target, m_ln_g, m_ln_b, m_rw_mu, m_rw_w0, m_rw_w1, m_rw_w2, m_rw_a0, m_rw_a1, m_rw_a2, m_rw_g1, m_rw_g2, m_rw_k_k, m_rw_k_a, m_rw_r_k, m_rw_wr, m_rw_wk, m_rw_wv, m_rw_wo, m_rw_lnx_g, m_rw_lnx_b, m_s5_a_re, m_s5_a_im, m_s5_log_dt, m_s5_b_re, m_s5_b_im, m_s5_c_re, m_s5_c_im, m_s5_d, m_s5_w_glu, m_mlp_w1, m_mlp_w2, v_ln_g, v_ln_b, v_rw_mu, v_rw_w0, v_rw_w1, v_rw_w2, v_rw_a0, v_rw_a1, v_rw_a2, v_rw_g1, v_rw_g2, v_rw_k_k, v_rw_k_a, v_rw_r_k, v_rw_wr, v_rw_wk, v_rw_wv, v_rw_wo, v_rw_lnx_g, v_rw_lnx_b, v_s5_a_re, v_s5_a_im, v_s5_log_dt, v_s5_b_re, v_s5_b_im, v_s5_c_re, v_s5_c_im, v_s5_d, v_s5_w_glu, v_mlp_w1, v_mlp_w2):
    given = dict(locals())
    local_w = {n: given[n] for n in WEIGHTS}
    local_m = {n: given["m_" + n] for n in WEIGHTS}
    local_v = {n: given["v_" + n] for n in WEIGHTS}
    small = [n for n in WEIGHTS if n in SHARDED and n not in BIG]

    my_slot = 4 * lax.axis_index("x") + 2 * lax.axis_index("y") + lax.axis_index("c")
    as_bf16 = {n: local_w[n].astype(BF16) for n in BIG}

    gathered = _all_gather("weights_all_gather", [local_w[n] for n in small])
    full = dict(local_w)
    for n, blk in zip(small, gathered):
        full[n] = jnp.moveaxis(blk, 0, SHARDED[n]).reshape(_full_shape(local_w[n].shape, SHARDED[n]))
    mid = _TwoStageGather("weights_mid", {n: as_bf16[n] for n in MID}, my_slot)
    late = lambda after: _TwoStageGather("weights_late", {n: as_bf16[n] for n in LATE}, my_slot, after)
    full['rw_w0'] = full['rw_w0'] + mid.zero

    exchanges = []

    def exchange(entries):
        started = _push_start(f"grads_start_{len(exchanges)}", [t for _, _, t in entries], [n in REPLICATED for n, _, _ in entries])
        exchanges.append((entries, started))
        return started['zero']

    loss_sq, dx, grads = _local_step(x[0], loss_target[0], full, mid, late, exchange)
    loss = (0.5 / D_MODEL) * lax.psum(loss_sq, MESH_AXES)

    wide = {'s5_b_re': (SSM_LANES, SSM_GROUP), 's5_b_im': (SSM_LANES, SSM_GROUP),
            's5_c_re': (D_MODEL, SSM_STATE), 's5_c_im': (D_MODEL, SSM_STATE)}
    by_name, many = {}, []
    landed = {n: {} for n in BIG}
    for e, (entries, started) in enumerate(exchanges):
        sources, lands = _push_wait(f"grads_wait_{e}", started, dx)
        for (n, i, _), src, slots in zip(entries, sources, lands):
            if n in wide:
                as_view = lambda t: t.reshape((1,) + wide[n])
                out = _adam_call(f"adamw_{n}", [slots.reshape((N_DEV,) + wide[n])], as_view(local_w[n]), as_view(local_m[n]),
                                 as_view(local_v[n]), own=[src.reshape(wide[n])])
                by_name[n] = [t.reshape(local_w[n].shape) for t in out]
            elif n in REPLICATED:
                many.append((n, (slots, local_w[n], local_m[n], local_v[n], src)))
            else:
                landed[n][i] = (slots, lax.dynamic_index_in_dim(src, my_slot, 0, keepdims=False))
    for n in BIG:
        layers = [landed[n][i] for i in sorted(landed[n])]
        by_name[n] = _adam_call(f"adamw_{n}", [s for s, _ in layers], local_w[n], local_m[n], local_v[n], own=[o for _, o in layers])
    split = lambda n: jnp.moveaxis(grads[n].reshape(_split_shape(local_w[n].shape, SHARDED[n])), SHARDED[n], 0)
    for n, slots in zip(small, _all_to_all("grads_all_to_all", [split(n) for n in small])):
        many.append((n, (slots, local_w[n], local_m[n], local_v[n], None)))
    last = [n for n in REPLICATED if n not in by_name and n not in dict(many)]
    for n, slots in zip(last, _all_gather("grads_all_gather", [grads[n] for n in last])):
        many.append((n, (slots, local_w[n], local_m[n], local_v[n], None)))
    for (n, _), out in zip(many, _adam_many("adamw_small", [item for _, item in many])):
        by_name[n] = out
    results = [by_name[n][k] for k in range(4) for n in WEIGHTS]
    return (loss, dx[None], *results)
```

```python
import functools
import math

import jax
import jax.numpy as jnp
from jax import lax
from jax.experimental import pallas as pl
from jax.experimental.pallas import tpu as pltpu

F32 = jnp.float32
BF16 = jnp.bfloat16

D_MODEL = 1024
HEAD = 64
PAIR = 2 * HEAD
N_PAIR = D_MODEL // PAIR
CHUNK = 64
GN_EPS = 64e-5
LN_EPS = 1e-5
SSM_GROUP = 16
SSM_STATE = 64
SSM_GROUPS = D_MODEL // SSM_GROUP
SSM_LANES = SSM_GROUPS * SSM_STATE
SSM_BLK_IN = 128
SSM_BLK_ST = 512
N_SSM_BLK = D_MODEL // SSM_BLK_IN
DEPTH = 2
DN_ALPHA = (2.0 * DEPTH) ** 0.25
ADAM_LR, ADAM_B1, ADAM_B2, ADAM_EPS, ADAM_WD, ADAM_STEP = 0.001, 0.9, 0.999, 1e-08, 0.01, 10
N_DEV = 8
MESH_AXES = ("x", "y", "c")
VMEM_LIMIT = 56 * 1024 * 1024
TOKEN_TILE = 256
LN_TILE = 512

SHARDED = {
    "rw_mu": 2, "rw_w1": 1, "rw_w2": 2, "rw_a1": 1, "rw_a2": 2, "rw_g1": 1, "rw_g2": 2,
    "rw_wr": 1, "rw_wk": 1, "rw_wv": 1, "rw_wo": 1, "s5_d": 1, "s5_w_glu": 2, "mlp_w1": 2, "mlp_w2": 1,
}
WEIGHTS = ['ln_g', 'ln_b', 'rw_mu', 'rw_w0', 'rw_w1', 'rw_w2', 'rw_a0', 'rw_a1', 'rw_a2', 'rw_g1', 'rw_g2', 'rw_k_k',
           'rw_k_a', 'rw_r_k', 'rw_wr', 'rw_wk', 'rw_wv', 'rw_wo', 'rw_lnx_g', 'rw_lnx_b', 's5_a_re', 's5_a_im',
           's5_log_dt', 's5_b_re', 's5_b_im', 's5_c_re', 's5_c_im', 's5_d', 's5_w_glu', 'mlp_w1', 'mlp_w2']
REPLICATED = [n for n in WEIGHTS if n not in SHARDED]
BIG = ['rw_wr', 'rw_wk', 'rw_wv', 'rw_wo', 's5_w_glu', 'mlp_w1', 'mlp_w2']
MID = ['rw_wr', 'rw_wk', 'rw_wv']
LATE = ['rw_wo', 's5_w_glu', 'mlp_w1', 'mlp_w2']
GRAD_DTYPE = BF16


def _cparams(n_grid):
    return pltpu.CompilerParams(dimension_semantics=("arbitrary",) * n_grid, vmem_limit_bytes=VMEM_LIMIT)


@jax.custom_vjp
def _mm(x, w):
    return jnp.dot(x.astype(BF16), w.astype(BF16), preferred_element_type=F32)


def _mm_fwd(x, w):
    return _mm(x, w), (x, w)


def _mm_bwd(res, dy):
    x, w = res
    dyb = dy.astype(BF16)
    dx = lax.dot_general(dyb, w.astype(BF16), (((1,), (1,)), ((), ())), preferred_element_type=F32)
    dw = lax.dot_general(x.astype(BF16), dyb, (((0,), (0,)), ((), ())), preferred_element_type=F32)
    return dx, dw


_mm.defvjp(_mm_fwd, _mm_bwd)


_DOT_DIMS = {'nn': (((1,), (0,)), ((), ())), 'nt': (((1,), (1,)), ((), ())), 'tn': (((0,), (0,)), ((), ()))}
_BATCH_DOT_DIMS = {'nn': (((2,), (1,)), ((0,), (0,))), 'nt': (((2,), (2,)), ((0,), (0,))), 'tn': (((1,), (1,)), ((0,), (0,)))}
CHUNK_PASSES = 1


def _split_bf16(a):
    hi = a.astype(BF16)
    return hi, (a - hi.astype(F32)).astype(BF16)


def _dot_passes(a, b, mode, passes=None):
    dims = (_DOT_DIMS if a.ndim == 2 else _BATCH_DOT_DIMS)[mode]
    dot = lambda p, q: lax.dot_general(p, q, dims, preferred_element_type=F32)
    if (CHUNK_PASSES if passes is None else passes) == 1:
        return dot(a.astype(BF16), b.astype(BF16))
    (ah, al), (bh, bl) = _split_bf16(a), _split_bf16(b)
    return dot(ah, bh) + (dot(ah, bl) + dot(al, bh))


@functools.partial(jax.custom_vjp, nondiff_argnums=(2,))
def _pdot(a, b, mode):
    return _dot_passes(a, b, mode)


def _pdot_fwd(a, b, mode):
    return _dot_passes(a, b, mode), (a, b)


def _pdot_bwd(mode, res, dy):
    a, b = res
    if mode == 'nn':
        return _dot_passes(dy, b, 'nt'), _dot_passes(a, dy, 'tn')
    if mode == 'nt':
        return _dot_passes(dy, b, 'nn'), _dot_passes(dy, a, 'tn')
    return _dot_passes(b, dy, 'nt'), _dot_passes(a, dy, 'nn')


_pdot.defvjp(_pdot_fwd, _pdot_bwd)


def _tri_sum(x, upper):
    nb, n = x.shape[0], x.shape[1]
    ti = lax.broadcasted_iota(jnp.int32, (nb, n, n), 1)
    tj = lax.broadcasted_iota(jnp.int32, (nb, n, n), 2)
    tri = ((ti <= tj) if upper else (ti >= tj)).astype(BF16)
    hi = x.astype(BF16)
    rest = x - hi.astype(F32)
    mid = rest.astype(BF16)
    lo = (rest - mid.astype(F32)).astype(BF16)
    dot = lambda q: lax.dot_general(tri, q, _BATCH_DOT_DIMS['nn'], preferred_element_type=F32)
    return dot(hi) + (dot(mid) + dot(lo))


@jax.custom_vjp
def _cumsum_rows(x):
    return _tri_sum(x, False)


_cumsum_rows.defvjp(lambda x: (_tri_sum(x, False), None), lambda _, dy: (_tri_sum(dy, True),))


def _power_sum(a, order):
    m = a.shape[-1]
    ti = lax.broadcasted_iota(jnp.int32, (1, m, m), 1)
    tj = lax.broadcasted_iota(jnp.int32, (1, m, m), 2)
    pw, total = _dot_passes(a, a, 'nn'), (ti == tj).astype(F32) + a
    for _ in range(int(math.log2(order)) - 2):
        both = _dot_passes(pw, jnp.concatenate([pw, total], axis=2), 'nn')
        pw, total = both[:, :, :m], total + both[:, :, m:]
    return total + _dot_passes(pw, total, 'nn')


@functools.partial(jax.custom_vjp, nondiff_argnums=(1,))
def _nilpotent_inverse(a, order):
    return _power_sum(a, order)


def _nilpotent_inverse_fwd(a, order):
    inv = _power_sum(a, order)
    return inv, inv


def _nilpotent_inverse_bwd(order, inv, g):
    return (_dot_passes(_dot_passes(inv, g, 'tn'), inv, 'nt'),)


_nilpotent_inverse.defvjp(_nilpotent_inverse_fwd, _nilpotent_inverse_bwd)


@jax.custom_vjp
def _known_inverse(a, inv):
    return inv


_known_inverse.defvjp(lambda a, inv: (inv, inv),
                      lambda inv, g: (_dot_passes(_dot_passes(inv, g, 'tn'), inv, 'nt'), jnp.zeros_like(inv)))


@functools.partial(jax.custom_vjp, nondiff_argnums=(1,))
def _roll_rows(x, shift):
    return pltpu.roll(x, shift, 0)


def _roll_rows_fwd(x, shift):
    return pltpu.roll(x, shift, 0), None


def _roll_rows_bwd(shift, _, dy):
    n = dy.shape[0]
    return (pltpu.roll(dy, (n - shift) % n, 0),)


_roll_rows.defvjp(_roll_rows_fwd, _roll_rows_bwd)


def _shift_down(x, first_row):
    row = lax.broadcasted_iota(jnp.int32, x.shape, 0)
    return jnp.where(row == 0, first_row, _roll_rows(x, 1))


def _sigmoid(x):
    return 1.0 / (1.0 + jnp.exp(-x))


def _softplus(x):
    return jnp.maximum(x, 0.0) + jnp.log(1.0 + jnp.exp(-jnp.abs(x)))


def _gelu(x):
    return 0.5 * x * (1.0 + jnp.tanh(math.sqrt(2.0 / math.pi) * (x + 0.044715 * (x * x * x))))


def _layer_norm(x, g, b):
    mu = jnp.mean(x, axis=-1, keepdims=True)
    xc = x - mu
    var = jnp.mean(xc * xc, axis=-1, keepdims=True)
    return xc * lax.rsqrt(var + LN_EPS) * g + b


def _spec(block, index_map):
    return pl.BlockSpec(block, index_map)


def _tile(arr, tm):
    return (arr, (tm, arr.shape[1]), lambda i: (i, 0))


def _full(arr):
    nd = arr.ndim
    return (arr, arr.shape, lambda *_: (0,) * nd)


def _fwd_call(name, fn, grid, prims, outs):
    n_in = len(prims)

    def body(*refs):
        vals = [r[...] for r in refs[:n_in]]
        vals = [v.astype(F32) if v.dtype != F32 else v for v in vals]
        res = fn(*vals)
        for o, r in zip(refs[n_in:], res):
            o[...] = r.astype(o.dtype)

    return pl.pallas_call(
        body, name=name, grid=grid,
        in_specs=[_spec(b, m) for _, b, m in prims],
        out_specs=[_spec(b, m) for _, _, b, m in outs],
        out_shape=[jax.ShapeDtypeStruct(s, d) for s, d, _, _ in outs],
        compiler_params=_cparams(len(grid)),
    )(*[a for a, _, _ in prims])


def _bwd_call(name, fn, grid, prims, cts, wants, addends=None):
    addends = addends or {}
    n_in, n_ct = len(prims), len(cts)
    out_place = {}
    wants, out_dtype = list(wants), {}
    for i, w in enumerate(wants):
        if isinstance(w, tuple):
            wants[i], out_place[i] = w
        elif w == 'tile_bf16':
            wants[i], out_dtype[i] = 'tile', BF16
    diff = [i for i, w in enumerate(wants) if w]
    add_idx = [i for i in diff if i in addends]
    last_axis = len(grid) - 1

    def body(*refs):
        vals = [r[...] for r in refs[:n_in]]
        vals = [v.astype(F32) if v.dtype != F32 else v for v in vals]
        ct_vals = [r[...] for r in refs[n_in:n_in + n_ct]]
        ct_vals = [v.astype(F32) if v.dtype != F32 else v for v in ct_vals]
        add_refs = dict(zip(add_idx, refs[n_in + n_ct:n_in + n_ct + len(add_idx)]))
        out_refs = refs[n_in + n_ct + len(add_idx):]

        def f(*dargs):
            full = list(vals)
            for i, a in zip(diff, dargs):
                full[i] = a
            return tuple(fn(*full))

        _, vjp = jax.vjp(f, *[vals[i] for i in diff])
        grads = vjp(tuple(ct_vals))
        first = pl.program_id(last_axis) == 0
        for o, g, i in zip(out_refs, grads, diff):
            if wants[i] == 'tile':
                if i in add_refs:
                    g = g + add_refs[i][...]
                o[...] = g.astype(o.dtype)
            else:
                @pl.when(first)
                def _(o=o):
                    o[...] = jnp.zeros(o.shape, o.dtype)
                o[...] += g

    ins = list(prims) + list(cts) + [addends[i] for i in add_idx]
    places = [out_place.get(i, (prims[i][0].shape, prims[i][1], prims[i][2])) for i in diff]
    return pl.pallas_call(
        body, name=name, grid=grid,
        in_specs=[_spec(b, m) for _, b, m in ins],
        out_specs=[_spec(b, m) for _, b, m in places],
        out_shape=[jax.ShapeDtypeStruct(s, out_dtype.get(i, F32)) for i, (s, _, _) in zip(diff, places)],
        compiler_params=_cparams(len(grid)),
    )(*[a for a, _, _ in ins])


def _matmul(name, a, b, mode, *, out_dtype=F32, lhs_fn=None, epi=None, extra=None, tm=1024, tn=1024, tk=1024,
            b_cols=None, b_rows=None, out_cols=False):
    if mode == 'tn':
        kdim, m = a.shape
    else:
        m, kdim = a.shape
    if b_rows is not None:
        loc, cols = b.shape[-2], b.shape[-1]
        if mode == 'nn':
            n, k_shards = cols, max(1, tk // loc)
            tk = k_shards * loc
        else:
            n, tn = N_DEV * loc, loc
    elif b_cols is None:
        n = b.shape[0] if mode == 'nt' else b.shape[1]
    else:
        rows, loc = b.shape[-2], b.shape[-1]
        if mode == 'nn':
            n, n_shards = N_DEV * loc, max(1, tn // loc)
            tn = n_shards * loc
        else:
            n, k_shards = rows, max(1, tk // loc)
            tk = k_shards * loc
    if out_cols:
        o_shards = max(1, min(tn, n) // (n // N_DEV))
        tn = o_shards * (n // N_DEV)
    tm, tn, tk = min(tm, m), min(tn, n), min(tk, kdim)
    nk = kdim // tk
    grid = (m // tm, n // tn, nk)
    a_spec = _spec((tk, tm), lambda i, j, k: (k, i)) if mode == 'tn' else _spec((tm, tk), lambda i, j, k: (i, k))
    if b_rows is not None:
        lead = tuple(b_rows)
        skip = (None,) * (1 + len(lead))
        if mode == 'nn':
            b_spec = _spec((k_shards,) + skip[1:] + (tk // k_shards, tn), lambda i, j, k: (k,) + lead + (0, j))
        else:
            b_spec = _spec(skip + (tn, tk), lambda i, j, k: (j,) + lead + (0, k))
    elif b_cols is None:
        b_spec = _spec((tn, tk), lambda i, j, k: (j, k)) if mode == 'nt' else _spec((tk, tn), lambda i, j, k: (k, j))
    else:
        lead = tuple(b_cols)
        skip = (None,) * (1 + len(lead))
        if mode == 'nn':
            b_spec = _spec((n_shards,) + skip[1:] + (tk, tn // n_shards), lambda i, j, k: (j,) + lead + (k, 0))
        else:
            b_spec = _spec((k_shards,) + skip[1:] + (tn, tk // k_shards), lambda i, j, k: (k,) + lead + (j, 0))
    o_spec = _spec((tm, tn), lambda i, j, k: (i, j))
    if out_cols:
        o_place, o_shape = _spec((o_shards, tm, tn // o_shards), lambda i, j, k: (j, i, 0)), (N_DEV, m, n // N_DEV)
    else:
        o_place, o_shape = o_spec, (m, n)
    dims = _DOT_DIMS[mode]

    def body(*refs):
        if extra is None:
            a_ref, b_ref, o_ref, acc = refs
            x_ref = None
        else:
            a_ref, b_ref, x_ref, o_ref, acc = refs
        k = pl.program_id(2)
        av = a_ref[...]
        if lhs_fn is not None:
            av = lhs_fn(av.astype(F32))
        bv = b_ref[...]
        if bv.ndim == 3:
            bv = bv.reshape(tk, tn) if b_rows is not None else jnp.concatenate([bv[g] for g in range(bv.shape[0])], axis=1)
        part = lax.dot_general(av.astype(BF16), bv.astype(BF16), dims, preferred_element_type=F32)

        def finish(r):
            if epi is not None:
                r = epi(r, x_ref[...])
            if out_cols:
                width = tn // o_shards
                for g in range(o_shards):
                    o_ref[g] = r[:, g * width:(g + 1) * width].astype(o_ref.dtype)
            else:
                o_ref[...] = r.astype(o_ref.dtype)

        if nk == 1:
            finish(part)
            return

        @pl.when(k == 0)
        def _():
            acc[...] = part

        @pl.when((k > 0) & (k < nk - 1))
        def _():
            acc[...] += part

        @pl.when(k == nk - 1)
        def _():
            finish(acc[...] + part)

    ops, specs = [a, b], [a_spec, b_spec]
    if extra is not None:
        ops.append(extra)
        specs.append(o_spec)
    return pl.pallas_call(
        body, name=name, grid=grid, in_specs=specs, out_specs=o_place,
        out_shape=jax.ShapeDtypeStruct(o_shape, out_dtype),
        scratch_shapes=[pltpu.VMEM((tm, tn), F32)],
        compiler_params=_cparams(3),
    )(*ops)


def _relu2(x):
    r = jnp.maximum(x, 0.0)
    return r * r


def _mix_fn(x, halo, mu, w0, w1, w2, a0, a1, a2, g1, g2):
    prev_row = jnp.where(pl.program_id(0) == 0, 0.0, halo[7:8, :])
    xx = _shift_down(x, prev_row) - x
    xr, xw, xk, xv, xa, xg = (x + xx * mu[i:i + 1, :] for i in range(6))
    w_pre = w0 + _mm(jnp.tanh(_mm(xw, w1)), w2)
    log_decay = -jnp.exp(-_softplus(-w_pre) - 0.5)
    a = _sigmoid(a0 + _mm(_mm(xa, a1), a2))
    g = _mm(_sigmoid(_mm(xg, g1)), g2)
    return xr, xk, xv, log_decay, a, g


def _chunk_fn(r, lw, k, v, a, k_k, k_a, r_k, inverse=None, with_inverse=False):
    n, n_pair = r.shape[0], r.shape[1] // PAIR
    to_batch = lambda t: jnp.concatenate([t[None, :, p * PAIR:(p + 1) * PAIR] for p in range(n_pair)], axis=0)
    from_batch = lambda t: jnp.concatenate([t[p] for p in range(n_pair)], axis=1)
    r, lw, k, v, a, k_k, k_a, r_k = (to_batch(t) for t in (r, lw, k, v, a, k_k, k_a, r_k))
    lane = lax.broadcasted_iota(jnp.int32, (1, 1, PAIR), 2)
    m0 = lane < HEAD

    def head_sum(t):
        s0 = jnp.sum(jnp.where(m0, t, 0.0), axis=2, keepdims=True)
        s1 = jnp.sum(jnp.where(m0, 0.0, t), axis=2, keepdims=True)
        return jnp.where(m0, s0, s1)

    kk = k * k_k
    kk = kk / jnp.maximum(jnp.sqrt(head_sum(kk * kk)), 1e-12)
    k2 = k * (1.0 + (a - 1.0) * k_a)
    aa, bb = -kk, kk * a
    bonus = head_sum(r * k2 * r_k) * v

    cum = _cumsum_rows(lw)
    p_in, p_ex, p_inv = jnp.exp(cum), jnp.exp(cum - lw), jnp.exp(-cum)
    at, rt, kt, bt = aa * p_ex, r * p_in, k2 * p_inv, bb * p_inv

    def stack_masked(t):
        return jnp.concatenate([jnp.where(m0, t, 0.0), jnp.where(m0, 0.0, t)], axis=1)

    def unstack_sum(t):
        return t[:, :n] + t[:, n:]

    def unstack_select(t):
        return jnp.where(m0, t[:, :n], t[:, n:])

    ti = lax.broadcasted_iota(jnp.int32, (1, 2 * n, 2 * n), 1)
    tj = lax.broadcasted_iota(jnp.int32, (1, 2 * n, 2 * n), 2)
    same = (ti >= n) == (tj >= n)
    incl, strict = same & (ti >= tj), same & (ti > tj)
    m = 2 * n
    rows = lambda *ts: jnp.concatenate(ts, axis=1)
    cols = lambda *ts: jnp.concatenate(ts, axis=2)
    at_s, rt_s, kt_s, bt_s = stack_masked(at), stack_masked(rt), stack_masked(kt), stack_masked(bt)
    v_s, zeros = rows(v, v), jnp.zeros_like(at_s)
    bk_s = rows(bt_s, kt_s)
    gram = _pdot(rows(at_s, rt_s), bk_s, 'nt')
    a_ab = jnp.where(strict, gram[:, :m, :m], 0.0)
    a_ak = jnp.where(strict, gram[:, :m, m:], 0.0)
    a_rb = jnp.where(incl, gram[:, m:, :m], 0.0)
    a_rk = jnp.where(incl, gram[:, m:, m:], 0.0)
    inv = _nilpotent_inverse(a_ab, n) if inverse is None else _known_inverse(a_ab, inverse)
    wy_s = _pdot(inv, cols(at_s, _pdot(a_ak, v_s, 'nn')), 'nn')
    q_oi = _pdot(cols(a_rb, a_rk), rows(wy_s, cols(zeros, v_s)), 'nn')
    q_all = unstack_sum(rt_s + q_oi[:, :, :PAIR])
    oi_all = unstack_select(q_oi[:, :, PAIR:])
    own_head = (lax.broadcasted_iota(jnp.int32, (1, m, PAIR), 1) >= n) == (lax.broadcasted_iota(jnp.int32, (1, m, PAIR), 2) >= HEAD)
    wy_m = cols(wy_s[:, :, :PAIR], jnp.where(own_head, wy_s[:, :, PAIR:], 0.0))
    mm_nn = _pdot(bk_s, rows(wy_m, cols(zeros, stack_masked(v))), 'tn')
    mm, nn = mm_nn[:, :, :PAIR], mm_nn[:, :, PAIR:]
    ei = lax.broadcasted_iota(jnp.int32, (1, PAIR, PAIR), 1)
    ej = lax.broadcasted_iota(jnp.int32, (1, PAIR, PAIR), 2)
    eye_p = (ei == ej).astype(F32)
    decay_col = jnp.sum(eye_p * p_in[:, n - 1:n, :], axis=2, keepdims=True)
    maps = ((decay_col * (eye_p + mm))[None], (decay_col * nn)[None], from_batch(q_all), from_batch(oi_all),
            from_batch(bonus))
    return maps + (inv,) if with_inverse else maps


def _gate_fn(o, bonus, g, lnx_g, lnx_b):
    lane = lax.broadcasted_iota(jnp.int32, (1, PAIR), 1)
    m0 = lane < HEAD

    def head_mean(t):
        s0 = jnp.sum(jnp.where(m0, t, 0.0), axis=1, keepdims=True)
        s1 = jnp.sum(jnp.where(m0, 0.0, t), axis=1, keepdims=True)
        return jnp.where(m0, s0, s1) * (1.0 / HEAD)

    outs = []
    for p in range(N_PAIR):
        sl = slice(p * PAIR, (p + 1) * PAIR)
        oc = o[:, sl] - head_mean(o[:, sl])
        on = oc * lax.rsqrt(head_mean(oc * oc) + GN_EPS) * lnx_g[:, sl] + lnx_b[:, sl]
        outs.append((on + bonus[:, sl]) * g[:, sl])
    return (jnp.concatenate(outs, axis=1),)


def _res_ln_fn(h, y, g, b):
    return (_layer_norm(DN_ALPHA * h + y, g, b),)


def _glu_ln_fn(h, z, g, b):
    mix = z[:, :D_MODEL] * _sigmoid(z[:, D_MODEL:])
    return (_layer_norm(DN_ALPHA * h + mix, g, b),)


def _s5_param_fn(a_re, a_im, log_dt, b_re, b_im):
    dt = jnp.exp(log_dt)
    lam_re = jnp.minimum(a_re, -1e-4)
    lam_im = a_im
    mag = jnp.exp(dt * lam_re)
    abar_re = mag * jnp.cos(dt * lam_im)
    abar_im = mag * jnp.sin(dt * lam_im)
    den = lam_re * lam_re + lam_im * lam_im
    nr, ni = abar_re - 1.0, abar_im
    coef_re = (nr * lam_re + ni * lam_im) / den
    coef_im = (ni * lam_re - nr * lam_im) / den
    return abar_re, abar_im, coef_re, coef_im


def _s5_bbar_fn(coef_re, coef_im, b_re, b_im):
    return coef_re * b_re - coef_im * b_im, coef_re * b_im + coef_im * b_re


def _s5_in_fn(u, b_re, b_im):
    return _pdot(u, b_re[0], 'nn'), _pdot(u, b_im[0], 'nn')


def _s5_out_fn(s_re, s_im, u, c_re, c_im, d):
    y = _pdot(s_re, c_re[0], 'nn') - _pdot(s_im, c_im[0], 'nn') + u * d
    return (_gelu(y),)


def _pairs_to_batch(t):
    return jnp.concatenate([t[None, :, p * PAIR:(p + 1) * PAIR] for p in range(N_PAIR)], axis=0)


def _batch_to_pairs(t):
    return jnp.concatenate([t[p] for p in range(N_PAIR)], axis=1)


STATE_PASSES = 1


def _rwkv_scan_fwd(r, lw, k, v, a, k_k, k_a, r_k):
    n_tok = r.shape[0]
    n_chunk = n_tok // CHUNK

    def body(r_ref, lw_ref, k_ref, v_ref, a_ref, kk_ref, ka_ref, rk_ref, o_ref, bonus_ref, zs_ref, inv_ref, m_ref, q_ref, z):
        @pl.when(pl.program_id(0) == 0)
        def _():
            z[...] = jnp.zeros(z.shape, F32)

        m_all, n_all, q, oi, bonus, inv = _chunk_fn(r_ref[...], lw_ref[...], k_ref[...], v_ref[...], a_ref[...],
                                                    kk_ref[...], ka_ref[...], rk_ref[...], with_inverse=True)
        zb = z[...]
        zs_ref[0] = zb
        inv_ref[0] = inv.astype(inv_ref.dtype)
        m_ref[...] = m_all
        q_ref[...] = q
        bonus_ref[...] = bonus
        o_ref[...] = _batch_to_pairs(_dot_passes(_pairs_to_batch(q), zb, 'nn', STATE_PASSES)) + oi
        z[...] = _dot_passes(m_all[0], zb, 'nn', STATE_PASSES) + n_all[0]

    tok = _spec((CHUNK, D_MODEL), lambda c: (c, 0))
    par = _spec((1, D_MODEL), lambda c: (0, 0))
    st = _spec((1, N_PAIR, PAIR, PAIR), lambda c: (c, 0, 0, 0))
    return pl.pallas_call(
        body, name="rwkv_scan_fwd", grid=(n_chunk,),
        in_specs=[tok] * 5 + [par] * 3, out_specs=[tok, tok, st, st, st, tok],
        out_shape=[jax.ShapeDtypeStruct((n_tok, D_MODEL), F32)] * 2
        + [jax.ShapeDtypeStruct((n_chunk, N_PAIR, PAIR, PAIR), F32), jax.ShapeDtypeStruct((n_chunk, N_PAIR, PAIR, PAIR), BF16),
           jax.ShapeDtypeStruct((n_chunk, N_PAIR, PAIR, PAIR), F32), jax.ShapeDtypeStruct((n_tok, D_MODEL), F32)],
        scratch_shapes=[pltpu.VMEM((N_PAIR, PAIR, PAIR), F32)],
        compiler_params=_cparams(1),
    )(r, lw, k, v, a, k_k, k_a, r_k)


def _rwkv_scan_bwd(r, lw, k, v, a, k_k, k_a, r_k, zs, invs, ms, q, d_o, d_bonus):
    n_tok = r.shape[0]
    n_chunk = n_tok // CHUNK

    def body(r_ref, lw_ref, k_ref, v_ref, a_ref, kk_ref, ka_ref, rk_ref, zs_ref, inv_ref, m_ref, q_ref, do_ref, db_ref,
             dr_ref, dlw_ref, dk_ref, dv_ref, da_ref, dkk_ref, dka_ref, drk_ref, g):
        sums = (dkk_ref, dka_ref, drk_ref)

        @pl.when(pl.program_id(0) == 0)
        def _():
            g[...] = jnp.zeros(g.shape, F32)
            for s_ref in sums:
                s_ref[...] = jnp.zeros(s_ref.shape, F32)

        prim = (r_ref[...], lw_ref[...], k_ref[...], v_ref[...], a_ref[...], kk_ref[...], ka_ref[...], rk_ref[...])
        known = inv_ref[0].astype(F32)
        _, chunk_vjp = jax.vjp(lambda *p: _chunk_fn(*p, inverse=known), *prim)
        m_all, q = m_ref[...], q_ref[...]
        gb, zb, d_out = g[...], zs_ref[0], do_ref[...]
        dob = _pairs_to_batch(d_out)
        d_m = _dot_passes(gb, zb, 'nt', STATE_PASSES)
        d_q = _batch_to_pairs(_dot_passes(dob, zb, 'nt', STATE_PASSES))
        grads = chunk_vjp((d_m[None], gb[None], d_q, d_out, db_ref[...]))
        for out_ref, val in zip((dr_ref, dlw_ref, dk_ref, dv_ref, da_ref), grads[:5]):
            out_ref[...] = val.astype(out_ref.dtype)
        for s_ref, val in zip(sums, grads[5:]):
            s_ref[...] += val
        same_head = (lax.broadcasted_iota(jnp.int32, (1, PAIR, PAIR), 1) // HEAD
                     == lax.broadcasted_iota(jnp.int32, (1, PAIR, PAIR), 2) // HEAD)
        g[...] = (_dot_passes(m_all[0], gb, 'tn', STATE_PASSES)
                  + jnp.where(same_head, _dot_passes(_pairs_to_batch(q), dob, 'tn', STATE_PASSES), 0.0))

    tok = _spec((CHUNK, D_MODEL), lambda c: (n_chunk - 1 - c, 0))
    par = _spec((1, D_MODEL), lambda c: (0, 0))
    st = _spec((1, N_PAIR, PAIR, PAIR), lambda c: (n_chunk - 1 - c, 0, 0, 0))
    tok_shape = lambda dt: jax.ShapeDtypeStruct((n_tok, D_MODEL), dt)
    return pl.pallas_call(
        body, name="rwkv_scan_bwd", grid=(n_chunk,),
        in_specs=[tok] * 5 + [par] * 3 + [st, st, st, tok, tok, tok], out_specs=[tok] * 5 + [par] * 3,
        out_shape=[tok_shape(BF16), tok_shape(F32), tok_shape(BF16), tok_shape(BF16), tok_shape(F32)]
        + [jax.ShapeDtypeStruct((1, D_MODEL), F32)] * 3,
        scratch_shapes=[pltpu.VMEM((N_PAIR, PAIR, PAIR), F32)],
        compiler_params=_cparams(1),
    )(r, lw, k, v, a, k_k, k_a, r_k, zs, invs, ms, q, d_o, d_bonus)


S5_TIME_TILE = 1024
S5_GROUP = 8


def _scan_rows(re_ref, im_ref, a1, carry, reverse):
    lb = re_ref.shape[1]
    grp, n_grp = S5_GROUP, re_ref.shape[0] // S5_GROUP

    def cmul(xr, xi, yr, yi):
        return xr * yr - xi * yi, xr * yi + xi * yr

    a2 = cmul(*a1, *a1)
    a4 = cmul(*a2, *a2)
    a8 = cmul(*a4, *a4)
    row = lax.broadcasted_iota(jnp.int32, (grp, lb), 0)
    expo = (grp - row) if reverse else (row + 1)
    pw = (jnp.ones((grp, lb), F32), jnp.zeros((grp, lb), F32))
    for bit, ap in ((1, a1), (2, a2), (4, a4), (8, a8)):
        nxt = cmul(*pw, *ap)
        sel = (expo & bit) != 0
        pw = (jnp.where(sel, nxt[0], pw[0]), jnp.where(sel, nxt[1], pw[1]))

    steps = []
    for d, ad in ((1, a1), (2, a2), (4, a4)):
        keep = (row < grp - d) if reverse else (row >= d)
        steps.append(((grp - d) if reverse else d, jnp.where(keep, ad[0], 0.0), jnp.where(keep, ad[1], 0.0)))

    def group(i, c):
        gi = (n_grp - 1 - i) if reverse else i
        rows = pl.ds(pl.multiple_of(gi * grp, grp), grp)
        xr, xi = re_ref[rows, :], im_ref[rows, :]
        for shift, mr, mi in steps:
            pr, pi = cmul(mr, mi, pltpu.roll(xr, shift, 0), pltpu.roll(xi, shift, 0))
            xr, xi = xr + pr, xi + pi
        cr, ci = cmul(*pw, c[0], c[1])
        xr, xi = xr + cr, xi + ci
        re_ref[rows, :] = xr
        im_ref[rows, :] = xi
        edge = slice(0, 1) if reverse else slice(grp - 1, grp)
        return xr[edge, :], xi[edge, :]

    return lax.fori_loop(0, n_grp, group, carry)


def _s5_fwd(u, b_re, b_im, a_re, a_im, c_re, c_im, d_row):
    n_tok = u.shape[0]
    tt = min(S5_TIME_TILE, n_tok)

    def body(u_ref, bre_ref, bim_ref, are_ref, aim_ref, cre_ref, cim_ref, d_ref, sre_ref, sim_ref, yg_ref, carry):
        @pl.when(pl.program_id(1) == 0)
        def _():
            carry[...] = jnp.zeros(carry.shape, F32)

        uv = u_ref[...]
        sre_ref[...], sim_ref[...] = _s5_in_fn(uv, bre_ref[...], bim_ref[...])
        c = _scan_rows(sre_ref, sim_ref, (are_ref[...], aim_ref[...]), (carry[0:1, :], carry[1:2, :]), False)
        carry[0:1, :] = c[0]
        carry[1:2, :] = c[1]
        (yg,) = _s5_out_fn(sre_ref[...], sim_ref[...], uv, cre_ref[...], cim_ref[...], d_ref[...])
        yg_ref[...] = yg.astype(yg_ref.dtype)

    u_blk = _spec((tt, SSM_BLK_IN), lambda l, t: (t, l))
    s_blk = _spec((tt, SSM_BLK_ST), lambda l, t: (t, l))
    blk3 = lambda arr: _spec((1,) + arr.shape[1:], lambda l, t: (l, 0, 0))
    par = lambda width: _spec((1, width), lambda l, t: (0, l))
    return pl.pallas_call(
        body, name="s5_fwd", grid=(N_SSM_BLK, n_tok // tt),
        in_specs=[u_blk, blk3(b_re), blk3(b_im), par(SSM_BLK_ST), par(SSM_BLK_ST), blk3(c_re), blk3(c_im), par(SSM_BLK_IN)],
        out_specs=[s_blk, s_blk, u_blk],
        out_shape=[jax.ShapeDtypeStruct((n_tok, SSM_LANES), F32)] * 2 + [jax.ShapeDtypeStruct((n_tok, D_MODEL), BF16)],
        scratch_shapes=[pltpu.VMEM((2, SSM_BLK_ST), F32)],
        compiler_params=_cparams(2),
    )(u, b_re, b_im, a_re, a_im, c_re, c_im, d_row)


def _s5_bwd(dyg, u, s_re, s_im, dh_res, b_re, b_im, a_re, a_im, c_re, c_im, d_row):
    n_tok = u.shape[0]
    tt = min(S5_TIME_TILE, n_tok)
    n_t = n_tok // tt

    def body(dyg_ref, u_ref, sre_ref, sim_ref, hre_ref, him_ref, res_ref, bre_ref, bim_ref, are_ref, aim_ref,
             cre_ref, cim_ref, d_ref, du_ref, dbre_ref, dbim_ref, dcre_ref, dcim_ref, dd_ref, dare_ref, daim_ref,
             carry, gre, gim):
        i = pl.program_id(1)
        sums = (dbre_ref, dbim_ref, dcre_ref, dcim_ref, dd_ref, dare_ref, daim_ref)

        @pl.when(i == 0)
        def _():
            carry[...] = jnp.zeros(carry.shape, F32)
            for r in sums:
                r[...] = jnp.zeros(r.shape, F32)

        uv, sre, sim = u_ref[...], sre_ref[...], sim_ref[...]
        _, out_vjp = jax.vjp(_s5_out_fn, sre, sim, uv, cre_ref[...], cim_ref[...], d_ref[...])
        gre[...], gim[...], du_out, dcre, dcim, dd = out_vjp((dyg_ref[...],))
        c = _scan_rows(gre, gim, (are_ref[...], -aim_ref[...]), (carry[0:1, :], carry[1:2, :]), True)
        carry[0:1, :] = c[0]
        carry[1:2, :] = c[1]
        g_re, g_im = gre[...], gim[...]
        at_start = i == n_t - 1
        p_re = _shift_down(sre, jnp.where(at_start, 0.0, hre_ref[7:8, :]))
        p_im = _shift_down(sim, jnp.where(at_start, 0.0, him_ref[7:8, :]))
        _, in_vjp = jax.vjp(_s5_in_fn, uv, bre_ref[...], bim_ref[...])
        du_in, dbre, dbim = in_vjp((g_re, g_im))
        du_ref[...] = du_out + du_in + res_ref[...]
        dare = jnp.sum(g_re * p_re + g_im * p_im, axis=0, keepdims=True)
        daim = jnp.sum(g_im * p_re - g_re * p_im, axis=0, keepdims=True)
        for r, val in zip(sums, (dbre, dbim, dcre, dcim, dd, dare, daim)):
            r[...] += val

    u_blk = _spec((tt, SSM_BLK_IN), lambda l, t: (n_t - 1 - t, l))
    s_blk = _spec((tt, SSM_BLK_ST), lambda l, t: (n_t - 1 - t, l))
    halo = _spec((8, SSM_BLK_ST), lambda l, t: (jnp.maximum((n_t - 1 - t) * (tt // 8) - 1, 0), l))
    blk3 = lambda arr: _spec((1,) + arr.shape[1:], lambda l, t: (l, 0, 0))
    par = lambda width: _spec((1, width), lambda l, t: (0, l))
    params = [blk3(b_re), blk3(b_im), par(SSM_BLK_ST), par(SSM_BLK_ST), blk3(c_re), blk3(c_im), par(SSM_BLK_IN)]
    shape = lambda arr: jax.ShapeDtypeStruct(arr.shape, F32)
    return pl.pallas_call(
        body, name="s5_bwd", grid=(N_SSM_BLK, n_t),
        in_specs=[u_blk, u_blk, s_blk, s_blk, halo, halo, u_blk] + params,
        out_specs=[u_blk, blk3(b_re), blk3(b_im), blk3(c_re), blk3(c_im), par(SSM_BLK_IN), par(SSM_BLK_ST), par(SSM_BLK_ST)],
        out_shape=[shape(u), shape(b_re), shape(b_im), shape(c_re), shape(c_im), shape(d_row), shape(a_re), shape(a_im)],
        scratch_shapes=[pltpu.VMEM((2, SSM_BLK_ST), F32), pltpu.VMEM((tt, SSM_BLK_ST), F32), pltpu.VMEM((tt, SSM_BLK_ST), F32)],
        compiler_params=_cparams(2),
    )(dyg, u, s_re, s_im, s_re, s_im, dh_res, b_re, b_im, a_re, a_im, c_re, c_im, d_row)


def _loss_call(h, target, tm):
    n_tok, d = h.shape

    def body(h_ref, t_ref, acc_ref, dh_ref):
        @pl.when(pl.program_id(0) == 0)
        def _():
            acc_ref[...] = jnp.zeros(acc_ref.shape, F32)

        e = h_ref[...] - t_ref[...]
        dh_ref[...] = e * (1.0 / d)
        acc_ref[...] += jnp.sum(jnp.sum(e * e, axis=1, keepdims=True), axis=0, keepdims=True)

    tok = _spec((tm, d), lambda i: (i, 0))
    return pl.pallas_call(
        body, name="loss", grid=(n_tok // tm,),
        in_specs=[tok, tok], out_specs=[_spec((8, 128), lambda i: (0, 0)), tok],
        out_shape=[jax.ShapeDtypeStruct((8, 128), F32), jax.ShapeDtypeStruct(h.shape, F32)],
        compiler_params=_cparams(1),
    )(h, target)


def _block_diag(t):
    nb, ng, rows, cols = t.shape
    eye = jnp.eye(ng, dtype=t.dtype)
    return jnp.einsum('bgrc,gh->bgrhc', t, eye).reshape(nb, ng * rows, ng * cols)


def _block_diag_t(t, rows, cols):
    nb, ng = t.shape[0], t.shape[1] // rows
    t5 = t.reshape(nb, ng, rows, ng, cols)
    return jnp.einsum('bgrhc,gh->bgrc', t5, jnp.eye(ng, dtype=t.dtype))


def _mlp_fwd(layer, h, w1g, w2g):
    pre = _matmul(f"mlp{layer}_up", h, w1g, 'nn', b_cols=(layer,), out_dtype=BF16)
    return pre, _matmul(f"mlp{layer}_down", pre, w2g, 'nn', lhs_fn=_relu2, b_rows=(layer,))


def _mlp_bwd(layer, h, pre, dy, dh_res, w1g, w2g):
    dpre = _matmul(f"mlp{layer}_down_dx", dy, w2g, 'nt', epi=lambda acc, p: acc * (2.0 * jnp.maximum(p, 0.0)),
                   extra=pre, b_rows=(layer,), out_dtype=BF16)
    dw2 = _matmul(f"mlp{layer}_down_dw", pre, dy, 'tn', lhs_fn=_relu2, out_dtype=GRAD_DTYPE)
    dw1 = _matmul(f"mlp{layer}_up_dw", h, dpre, 'tn', out_cols=True, out_dtype=GRAD_DTYPE)
    dh = _matmul(f"mlp{layer}_up_dx", dpre, w1g, 'nt', epi=lambda acc, e: acc + e, extra=dh_res, b_cols=(layer,))
    return dh, dw1, dw2


def _local_step(x, target, w, mid, start_late, exchange):
    n_tok = x.shape[0]
    tm = min(TOKEN_TILE, n_tok)
    n_tile, n_chunk = n_tok // tm, n_tok // CHUNK
    row = lambda v: v.reshape(1, -1)
    big = {}
    ln_g, ln_b = w['ln_g'], w['ln_b']
    grads = {}
    to_slots = lambda t: t.reshape(N_DEV, t.shape[0] // N_DEV, t.shape[1])

    halo_spec = (8, D_MODEL), lambda i: (jnp.maximum(i * (tm // 8) - 1, 0), 0)
    mix_p = [_tile(x, tm), (x,) + halo_spec, _full(w['rw_mu'][0]), _full(row(w['rw_w0'])), _full(w['rw_w1'][0]),
             _full(w['rw_w2'][0]), _full(row(w['rw_a0'])), _full(w['rw_a1'][0]), _full(w['rw_a2'][0]),
             _full(w['rw_g1'][0]), _full(w['rw_g2'][0])]
    tok_out = lambda dt: ((n_tok, D_MODEL), dt, (tm, D_MODEL), lambda i: (i, 0))
    xr, xk, xv, lw, a, g = _fwd_call("rwkv_mix", _mix_fn, (n_tile,), mix_p,
                                     [tok_out(BF16), tok_out(BF16), tok_out(BF16), tok_out(F32), tok_out(F32), tok_out(F32)])
    mid.forward(xr)
    square = lambda t: t.reshape(D_MODEL, D_MODEL)
    big.update({n: square(t) for n, t in mid.finish(xr).items()})
    late = start_late(big['rw_wr'])
    r = _matmul("rwkv_r", xr, big['rw_wr'], 'nn')
    k = _matmul("rwkv_k", xk, big['rw_wk'], 'nn')
    v = _matmul("rwkv_v", xv, big['rw_wv'], 'nn')

    scan_in = (r, lw, k, v, a, row(w['rw_k_k']) + late.zero, row(w['rw_k_a']), w['rw_r_k'].reshape(1, -1))
    o, bonus, *scan_saved = _rwkv_scan_fwd(*scan_in)

    zero = late.forward(o)
    gate_p = [_tile(o, tm), _tile(bonus, tm), _tile(g, tm), _full(row(w['rw_lnx_g']) + zero), _full(row(w['rw_lnx_b']))]
    (og,) = _fwd_call("rwkv_gate", _gate_fn, (n_tile,), gate_p, [tok_out(BF16)])
    late_w = late.finish(og)
    big['rw_wo'] = square(late_w['rw_wo'])
    w1g, w2g, glu_g = late_w['mlp_w1'], late_w['mlp_w2'], late_w['s5_w_glu']
    y0 = _matmul("rwkv_o", og, big['rw_wo'], 'nn')
    tl = min(LN_TILE, n_tok)
    n_ln = n_tok // tl
    ln_out = lambda dt: ((n_tok, D_MODEL), dt, (tl, D_MODEL), lambda i: (i, 0))
    ln0_p = [_tile(x, tl), _tile(y0, tl), _full(ln_g[0:1]), _full(ln_b[0:1])]
    twice = lambda fn: lambda *args: fn(*args) * 2
    h1, h1_b = _fwd_call("ln0", twice(_res_ln_fn), (n_ln,), ln0_p, [ln_out(F32), ln_out(BF16)])
    pre0, y1 = _mlp_fwd(0, h1_b, w1g, w2g)
    ln1_p = [_tile(h1, tl), _tile(y1, tl), _full(ln_g[1:2]), _full(ln_b[1:2])]
    (h2,) = _fwd_call("ln1", _res_ln_fn, (n_ln,), ln1_p, [ln_out(F32)])

    gp = (SSM_GROUPS, SSM_STATE)
    par_p = [_full(w['s5_a_re'][0]), _full(w['s5_a_im'][0]), _full(w['s5_log_dt'].reshape(SSM_GROUPS, 1))]
    s5_par_fn = lambda ar, ai, ld: _s5_param_fn(ar, ai, ld, None, None)
    gp_out = (gp, F32, gp, lambda i: (0, 0))
    abar_re, abar_im, coef_re, coef_im = _fwd_call("s5_param", s5_par_fn, (1,), par_p, [gp_out] * 4)
    b_flat = [w[n][0].reshape(SSM_LANES, SSM_GROUP) for n in ('s5_b_re', 's5_b_im')]
    bbar_p = [_full(coef_re.reshape(SSM_LANES, 1)), _full(coef_im.reshape(SSM_LANES, 1)), _full(b_flat[0]), _full(b_flat[1])]
    bb_out = ((SSM_LANES, SSM_GROUP), F32, (SSM_LANES, SSM_GROUP), lambda i: (0, 0))
    bbar_re, bbar_im = _fwd_call("s5_bbar", _s5_bbar_fn, (1,), bbar_p, [bb_out] * 2)
    to_in = lambda t: _block_diag(t.reshape(N_SSM_BLK, 8, SSM_STATE, SSM_GROUP).transpose(0, 1, 3, 2))
    to_out = lambda t: _block_diag(t.reshape(N_SSM_BLK, 8, SSM_GROUP, SSM_STATE).transpose(0, 1, 3, 2))
    bblk_re, bblk_im = to_in(bbar_re), to_in(bbar_im)
    cblk_re, cblk_im = to_out(w['s5_c_re'][0]), to_out(w['s5_c_im'][0])
    a_row_re, a_row_im = abar_re.reshape(1, SSM_LANES), abar_im.reshape(1, SSM_LANES)

    d_row = row(w['s5_d'])
    s5_params = (bblk_re, bblk_im, a_row_re, a_row_im, cblk_re, cblk_im, d_row)
    s_re, s_im, yg = _s5_fwd(h2, *s5_params)
    z = _matmul("s5_glu", yg, glu_g, 'nn', b_cols=(0,))
    ln2_p = [_tile(h2, tl), _tile(z, tl), _full(ln_g[2:3]), _full(ln_b[2:3])]
    h3, h3_b = _fwd_call("ln2", twice(_glu_ln_fn), (n_ln,), ln2_p, [ln_out(F32), ln_out(BF16)])
    pre1, y3 = _mlp_fwd(1, h3_b, w1g, w2g)
    ln3_p = [_tile(h3, tl), _tile(y3, tl), _full(ln_g[3:4]), _full(ln_b[3:4])]
    (h4,) = _fwd_call("ln3", _res_ln_fn, (n_ln,), ln3_p, [ln_out(F32)])

    loss_acc, dh4 = _loss_call(h4, target, tl)

    ln_w = ['tile', 'tile_bf16', 'acc', 'acc']
    dh3_res, dy3, dg3, db3 = _bwd_call("ln3_bwd", _res_ln_fn, (n_ln,), ln3_p, [_tile(dh4, tl)], ln_w)
    dh3, dw1_1, dw2_1 = _mlp_bwd(1, h3_b, pre1, dy3, dh3_res, w1g, w2g)
    zero = exchange([('mlp_w1', 1, dw1_1), ('mlp_w2', 1, to_slots(dw2_1))])
    ln2_p[2] = _full(ln_g[2:3] + zero)
    dh2_res, dz, dg2, db2 = _bwd_call("ln2_bwd", _glu_ln_fn, (n_ln,), ln2_p, [_tile(dh3, tl)], ln_w)
    dyg = _matmul("s5_glu_dx", dz, glu_g, 'nt', b_cols=(0,))
    dw_glu = _matmul("s5_glu_dw", yg, dz, 'tn', out_cols=True, out_dtype=GRAD_DTYPE)
    dh2, dbb_re, dbb_im, dcb_re, dcb_im, dd, dabar_re, dabar_im = _s5_bwd(dyg, h2, s_re, s_im, dh2_res, *s5_params)
    from_in = lambda t: _block_diag_t(t, SSM_GROUP, SSM_STATE).transpose(0, 1, 3, 2).reshape(SSM_LANES, SSM_GROUP)
    from_out = lambda t: _block_diag_t(t, SSM_STATE, SSM_GROUP).transpose(0, 1, 3, 2).reshape(1, SSM_GROUPS, SSM_GROUP, SSM_STATE)
    grads['s5_c_re'], grads['s5_c_im'] = from_out(dcb_re), from_out(dcb_im)
    grads['s5_d'] = dd
    dcoef_re, dcoef_im, db_re, db_im = _bwd_call(
        "s5_bbar_bwd", _s5_bbar_fn, (1,), bbar_p, [_full(from_in(dbb_re)), _full(from_in(dbb_im))], ['acc'] * 4)
    grads['s5_b_re'] = db_re.reshape(w['s5_b_re'].shape)
    grads['s5_b_im'] = db_im.reshape(w['s5_b_im'].shape)
    par_ct = [_full(dabar_re.reshape(gp)), _full(dabar_im.reshape(gp)), _full(dcoef_re.reshape(gp)), _full(dcoef_im.reshape(gp))]
    da_re, da_im, dlog_dt = _bwd_call("s5_param_bwd", s5_par_fn, (1,), par_p, par_ct, ['acc'] * 3)
    grads['s5_a_re'], grads['s5_a_im'], grads['s5_log_dt'] = da_re[None], da_im[None], dlog_dt.reshape(1, SSM_GROUPS)
    zero = exchange([(n, 0, grads[n]) for n in ('s5_a_re', 's5_a_im', 's5_log_dt', 's5_b_re', 's5_b_im', 's5_c_re', 's5_c_im')])
    ln1_p[2] = _full(ln_g[1:2] + zero)

    dh1_res, dy1, dg1, db1 = _bwd_call("ln1_bwd", _res_ln_fn, (n_ln,), ln1_p, [_tile(dh2, tl)], ln_w)
    dh1, dw1_0, dw2_0 = _mlp_bwd(0, h1_b, pre0, dy1, dh1_res, w1g, w2g)
    dx_res, dy0, dg0, db0 = _bwd_call("ln0_bwd", _res_ln_fn, (n_ln,), ln0_p, [_tile(dh1, tl)], ln_w)
    grads['ln_g'] = jnp.concatenate([dg0, dg1, dg2, dg3])
    grads['ln_b'] = jnp.concatenate([db0, db1, db2, db3])
    dog = _matmul("rwkv_o_dx", dy0, big['rw_wo'], 'nt')
    dw_o = to_slots(_matmul("rwkv_o_dw", og, dy0, 'tn', out_dtype=GRAD_DTYPE))
    zero = exchange([('s5_w_glu', 0, dw_glu), ('mlp_w1', 0, dw1_0), ('mlp_w2', 0, to_slots(dw2_0)), ('rw_wo', 0, dw_o)])
    gate_p[3] = _full(row(w['rw_lnx_g']) + zero)
    d_o, dbonus, dgate, grads['rw_lnx_g'], grads['rw_lnx_b'] = _bwd_call(
        "rwkv_gate_bwd", _gate_fn, (n_tile,), gate_p, [_tile(dog, tm)], ['tile', 'tile', 'tile', 'acc', 'acc'])
    dr, dlw, dk, dv, da, grads['rw_k_k'], grads['rw_k_a'], dr_k = _rwkv_scan_bwd(*scan_in, *scan_saved, d_o, dbonus)
    grads['rw_r_k'] = dr_k.reshape(w['rw_r_k'].shape)
    dw_r = to_slots(_matmul("rwkv_r_dw", xr, dr, 'tn', out_dtype=GRAD_DTYPE))
    dw_k = to_slots(_matmul("rwkv_k_dw", xk, dk, 'tn', out_dtype=GRAD_DTYPE))
    dw_v = to_slots(_matmul("rwkv_v_dw", xv, dv, 'tn', out_dtype=GRAD_DTYPE))
    done = ('ln_g', 'ln_b', 'rw_lnx_g', 'rw_lnx_b', 'rw_k_k', 'rw_k_a', 'rw_r_k')
    zero = exchange([('rw_wr', 0, dw_r), ('rw_wk', 0, dw_k), ('rw_wv', 0, dw_v)] + [(n, 0, grads[n]) for n in done])
    mix_p[3] = _full(row(w['rw_w0']) + zero)
    dxr = _matmul("rwkv_r_dx", dr, big['rw_wr'], 'nt')
    dxk = _matmul("rwkv_k_dx", dk, big['rw_wk'], 'nt')
    dxv = _matmul("rwkv_v_dx", dv, big['rw_wv'], 'nt')
    mix_ct = [_tile(t, tm) for t in (dxr, dxk, dxv, dlw, da, dgate)]
    halo_grad = ('tile', ((n_tile * 8, D_MODEL), (8, D_MODEL), lambda i: (i, 0)))
    res = _bwd_call("rwkv_mix_bwd", _mix_fn, (n_tile,), mix_p, mix_ct, ['tile', halo_grad] + ['acc'] * 9,
                    addends={0: _tile(dx_res, tm)})
    corr = res[1].reshape(n_tile, 8, D_MODEL)[1:, 7:8, :]
    corr = jnp.pad(corr, ((0, 1), (tm - 1, 0), (0, 0)))
    dx = (res[0].reshape(n_tile, tm, D_MODEL) + corr).reshape(n_tok, D_MODEL)
    (grads['rw_mu'], grads['rw_w0'], grads['rw_w1'], grads['rw_w2'], grads['rw_a0'], grads['rw_a1'], grads['rw_a2'],
     grads['rw_g1'], grads['rw_g2']) = [t[None] if t.shape[0] != 1 else t for t in res[2:]]
    return loss_acc[0, 0], dx, grads


def _full_shape(local_shape, axis):
    return local_shape[:axis] + (N_DEV * local_shape[axis],) + local_shape[axis + 1:]


def _split_shape(local_shape, axis):
    return local_shape[:axis] + (N_DEV, local_shape[axis]) + local_shape[axis + 1:]


def _mesh_pos():
    return lax.axis_index("x"), lax.axis_index("y"), lax.axis_index("c")


def _flip(v, f):
    return 1 - v if f else v


def _hbm_call(name, body, arrays, out_shapes, n_sems):
    n_t = len(arrays)
    hbm = pl.BlockSpec(memory_space=pl.ANY)
    return pl.pallas_call(
        body, name=name, out_shape=out_shapes, in_specs=[hbm] * n_t, out_specs=[hbm] * n_t,
        scratch_shapes=[pltpu.SemaphoreType.DMA((n_t, n_sems)), pltpu.SemaphoreType.DMA((n_t, n_sems)),
                        pltpu.SemaphoreType.DMA((n_t,))],
    )(*arrays)


def _all_gather(name, locals_):
    n_t = len(locals_)

    def body(*refs):
        x_refs, out_refs = refs[:n_t], refs[n_t:2 * n_t]
        send_sems, recv_sems, local_sems = refs[2 * n_t:]
        x, y, c = _mesh_pos()
        me, sibling = (x, y, c), (x, y, 1 - c)
        chips = [(1 - x, y), (x, 1 - y), (1 - x, 1 - y)]

        def copy(t, k, block, to, own=False):
            slot = out_refs[t].at[4 * block[0] + 2 * block[1] + block[2]]
            return pltpu.make_async_remote_copy(
                src_ref=x_refs[t] if own else slot, dst_ref=slot,
                send_sem=send_sems.at[t, k], recv_sem=recv_sems.at[t, k],
                device_id=to, device_id_type=pl.DeviceIdType.MESH)

        mine = [pltpu.make_async_copy(x_refs[t], out_refs[t].at[4 * x + 2 * y + c], local_sems.at[t]) for t in range(n_t)]
        for cp in mine:
            cp.start()
        sent = []
        for t in range(n_t):
            sent.append(copy(t, 0, me, sibling, own=True))
            sent += [copy(t, 1 + j, me, (*chip, c), own=True) for j, chip in enumerate(chips)]
        for cp in sent:
            cp.start()
        for j, chip in enumerate(chips):
            for t in range(n_t):
                copy(t, 1 + j, (*chip, c), me).wait_recv()
                passed = copy(t, 4 + j, (*chip, c), sibling)
                passed.start()
                sent.append(passed)
        for t in range(n_t):
            copy(t, 0, sibling, me).wait_recv()
            for j, chip in enumerate(chips):
                copy(t, 4 + j, (*chip, 1 - c), me).wait_recv()
        for cp in sent:
            cp.wait_send()
        for cp in mine:
            cp.wait()

    outs = [jax.ShapeDtypeStruct((N_DEV,) + a.shape, a.dtype) for a in locals_]
    return _hbm_call(name, body, locals_, outs, 7)


def _all_to_all(name, blocks):
    n_t = len(blocks)

    def body(*refs):
        x_refs, out_refs = refs[:n_t], refs[n_t:2 * n_t]
        send_sems, recv_sems, local_sems = refs[2 * n_t:]
        x, y, c = _mesh_pos()
        my_slot = 4 * x + 2 * y + c
        mine = [pltpu.make_async_copy(x_refs[t].at[my_slot], out_refs[t].at[my_slot], local_sems.at[t]) for t in range(n_t)]
        for cp in mine:
            cp.start()
        copies = []
        for k in range(1, N_DEV):
            peer = (_flip(x, k & 4), _flip(y, k & 2), _flip(c, k & 1))
            peer_slot = 4 * peer[0] + 2 * peer[1] + peer[2]
            for t in range(n_t):
                sems = dict(send_sem=send_sems.at[t, k - 1], recv_sem=recv_sems.at[t, k - 1],
                            device_id=peer, device_id_type=pl.DeviceIdType.MESH)
                cp = pltpu.make_async_remote_copy(src_ref=x_refs[t].at[peer_slot], dst_ref=out_refs[t].at[my_slot], **sems)
                cp.start()
                landed = pltpu.make_async_remote_copy(src_ref=x_refs[t].at[my_slot], dst_ref=out_refs[t].at[peer_slot], **sems)
                copies.append((cp, landed))
        for sent, landed in copies:
            landed.wait_recv()
            sent.wait_send()
        for cp in mine:
            cp.wait()

    outs = [jax.ShapeDtypeStruct(a.shape, a.dtype) for a in blocks]
    return _hbm_call(name, body, blocks, outs, 7)


_HBM = pl.BlockSpec(memory_space=pltpu.HBM)
_SEM = pl.BlockSpec(memory_space=pltpu.SEMAPHORE)
_EFFECT = pltpu.SideEffectType.DATAFLOW_SIDE_EFFECTING


def _peer_copies(x_refs, land_refs, send_sems, recv_sems, gather, mine):
    x, y, c = _mesh_pos()
    my_slot = 4 * x + 2 * y + c
    copies = []
    for k in range(1, N_DEV):
        peer = (_flip(x, k & 4), _flip(y, k & 2), _flip(c, k & 1))
        peer_slot = 4 * peer[0] + 2 * peer[1] + peer[2]
        for t, (x_ref, land) in enumerate(zip(x_refs, land_refs)):
            sem = t * (N_DEV - 1) + k - 1
            copies.append(pltpu.make_async_remote_copy(
                src_ref=x_ref if gather[t] else x_ref.at[peer_slot], dst_ref=land.at[my_slot if mine else peer_slot],
                send_sem=send_sems.at[sem], recv_sem=recv_sems.at[sem], device_id=peer, device_id_type=pl.DeviceIdType.MESH))
    return copies


def _split_start(name, held, plan, n_copies, after=None):
    n_h = len(held)

    n_in = n_h + (after is not None)

    def body(*refs):
        for sent in plan(refs[:n_h], refs[n_in], refs[n_in + 1], True):
            sent.start()
        refs[-1][...] = jnp.zeros((8, 128), F32)

    sems = pltpu.SemaphoreType.DMA((n_copies,))
    out = pl.pallas_call(
        body, name=name,
        out_shape=(sems, sems, *[pltpu.HBM(a.shape, a.dtype) for a in held], jax.ShapeDtypeStruct((8, 128), F32)),
        in_specs=[_HBM] * n_h + [pl.BlockSpec(memory_space=pl.ANY)] * (after is not None),
        out_specs=(_SEM, _SEM, *[_HBM] * n_h, pl.BlockSpec(memory_space=pltpu.VMEM)),
        input_output_aliases={i: 2 + i for i in range(n_h)},
        compiler_params=pltpu.CompilerParams(has_side_effects=_EFFECT),
    )(*[pltpu.with_memory_space_constraint(a, pltpu.HBM) for a in held], *([after] if after is not None else []))
    return dict(send=out[0], recv=out[1], held=out[2:2 + n_h], zero=out[-1][0, 0], plan=plan)


def _split_wait(name, started, after):
    held, plan = started['held'], started['plan']
    n_h = len(held)

    def body(*refs):
        for sent in plan(refs[:n_h], refs[n_h], refs[n_h + 1], True):
            sent.wait_send()
        for landed in plan(refs[:n_h], refs[n_h], refs[n_h + 1], False):
            landed.wait_recv()

    return pl.pallas_call(
        body, name=name,
        out_shape=[pltpu.HBM(a.shape, a.dtype) for a in held],
        in_specs=[_HBM] * n_h + [_SEM, _SEM, pl.BlockSpec(memory_space=pl.ANY)],
        out_specs=[_HBM] * n_h,
        input_output_aliases={i: i for i in range(n_h)},
        compiler_params=pltpu.CompilerParams(has_side_effects=_EFFECT),
    )(*held, started['send'], started['recv'], after)


def _push_start(name, arrays, gather):
    n_t = len(arrays)
    lands = [lax.empty((N_DEV,) + a.shape if whole else a.shape, a.dtype) for a, whole in zip(arrays, gather)]
    plan = lambda refs, send, recv, mine: _peer_copies(refs[:n_t], refs[n_t:], send, recv, gather, mine)
    return _split_start(name, list(arrays) + lands, plan, n_t * (N_DEV - 1))


def _push_wait(name, started, after):
    held = _split_wait(name, started, after)
    return held[:len(held) // 2], held[len(held) // 2:]


def _chip_copies(refs, send_sems, recv_sems, mine):
    n_t = len(refs) // 2
    x, y, c = _mesh_pos()
    my_slot = 4 * x + 2 * y + c
    peers = [(x, y, 1 - c), (1 - x, y, c), (x, 1 - y, c), (1 - x, 1 - y, c)]
    copies = []
    for j, peer in enumerate(peers):
        peer_slot = 4 * peer[0] + 2 * peer[1] + peer[2]
        for t in range(n_t):
            copies.append(pltpu.make_async_remote_copy(
                src_ref=refs[t], dst_ref=refs[n_t + t].at[my_slot if mine else peer_slot],
                send_sem=send_sems.at[4 * t + j], recv_sem=recv_sems.at[4 * t + j],
                device_id=peer, device_id_type=pl.DeviceIdType.MESH))
    return copies


def _forward_copies(refs, send_sems, recv_sems, mine):
    x, y, c = _mesh_pos()
    sibling = (x, y, 1 - c)
    copies = []
    for j, (px, py) in enumerate([(1 - x, y), (x, 1 - y), (1 - x, 1 - y)]):
        slot = 4 * px + 2 * py + (c if mine else 1 - c)
        for t, land in enumerate(refs):
            copies.append(pltpu.make_async_remote_copy(
                src_ref=land.at[slot], dst_ref=land.at[slot], send_sem=send_sems.at[3 * t + j], recv_sem=recv_sems.at[3 * t + j],
                device_id=sibling, device_id_type=pl.DeviceIdType.MESH))
    return copies


class _TwoStageGather:
    def __init__(self, name, arrays, my_slot, after=None):
        self.name, self.arrays, self.my_slot = name, arrays, my_slot
        values = list(arrays.values())
        lands = [lax.empty((N_DEV,) + a.shape, a.dtype) for a in values]
        self.stage = _split_start(name + "_start", values + lands, _chip_copies, 4 * len(values), after)
        self.zero = self.stage['zero']

    def forward(self, after):
        held = _split_wait(self.name + "_wait", self.stage, after)
        self.stage = _split_start(self.name + "_forward_start", held[len(self.arrays):], _forward_copies, 3 * len(self.arrays))
        return self.stage['zero']

    def finish(self, after):
        lands = _split_wait(self.name + "_forward_wait", self.stage, after)
        return {n: lax.dynamic_update_slice(land, a[None], (self.my_slot,) + (0,) * a.ndim)
                for (n, a), land in zip(self.arrays.items(), lands)}


ADAM_BLOCK = 128 * 1024


def _adam_update(part, w_ref, m_ref, v_ref, g_ref, d_ref, nm_ref, nv_ref):
    g = part(0)
    for i in range(1, N_DEV):
        g = g + part(i)
    m_new = ADAM_B1 * m_ref[...] + (1.0 - ADAM_B1) * g
    v_new = ADAM_B2 * v_ref[...] + (1.0 - ADAM_B2) * (g * g)
    m_hat = m_new * (1.0 / (1.0 - ADAM_B1 ** ADAM_STEP))
    v_hat = v_new * (1.0 / (1.0 - ADAM_B2 ** ADAM_STEP))
    g_ref[...] = g
    nm_ref[...] = m_new
    nv_ref[...] = v_new
    d_ref[...] = -ADAM_LR * (m_hat / (jnp.sqrt(v_hat) + ADAM_EPS) + ADAM_WD * w_ref[...])


def _adam_many(name, items):
    n_t = len(items)
    n_in = [4 if own is None else 5 for *_, own in items]
    first = [sum(n_in[:t]) for t in range(n_t)]

    def body(*refs):
        ins, outs = refs[:sum(n_in)], refs[sum(n_in):]
        x, y, c = _mesh_pos()
        my_slot = 4 * x + 2 * y + c
        for t in range(n_t):
            s_ref, w_ref, m_ref, v_ref = ins[first[t]:first[t] + 4]
            if n_in[t] == 4:
                part = lambda i: s_ref[i]
            else:
                own_ref = ins[first[t] + 4]
                part = lambda i: jnp.where(my_slot == i, own_ref[...], s_ref[i])
            _adam_update(part, w_ref, m_ref, v_ref, *outs[4 * t:4 * t + 4])

    vmem = pl.BlockSpec(memory_space=pltpu.VMEM)
    out = pl.pallas_call(
        body, name=name, in_specs=[vmem] * sum(n_in), out_specs=[vmem] * (4 * n_t),
        out_shape=[jax.ShapeDtypeStruct(item[1].shape, F32) for item in items for _ in range(4)],
        compiler_params=pltpu.CompilerParams(vmem_limit_bytes=VMEM_LIMIT),
    )(*[a for item in items for a in item if a is not None])
    return [out[4 * t:4 * t + 4] for t in range(n_t)]


def _adam_call(name, slots, w, m, v, own=None):
    n_layer, rows, width = w.shape
    tile = min(rows, ADAM_BLOCK // max(width, 128))

    def body(*refs):
        s_refs = refs[:n_layer]
        own_refs = refs[n_layer:2 * n_layer] if own is not None else None
        w_ref, m_ref, v_ref = refs[-7:-4]
        layer = pl.program_id(0)
        x, y, c = _mesh_pos()
        my_slot = 4 * x + 2 * y + c
        for l in range(n_layer):
            @pl.when(layer == l)
            def _(l=l):
                if own is None:
                    part = lambda i: s_refs[l][i].astype(F32)
                else:
                    part = lambda i: jnp.where(my_slot == i, own_refs[l][...], s_refs[l][i]).astype(F32)
                _adam_update(part, w_ref, m_ref, v_ref, *refs[-4:])

    slot_spec = lambda k: _spec((N_DEV, tile, width), lambda l, i: (0, jnp.where(l == k, i, 0), 0))
    own_spec = lambda k: _spec((tile, width), lambda l, i: (jnp.where(l == k, i, 0), 0))
    blk = _spec((None, tile, width), lambda l, i: (l, i, 0))
    specs = [slot_spec(k) for k in range(n_layer)] + ([own_spec(k) for k in range(n_layer)] if own is not None else [])
    return pl.pallas_call(
        body, name=name, grid=(n_layer, rows // tile),
        in_specs=specs + [blk] * 3, out_specs=[blk] * 4, out_shape=[jax.ShapeDtypeStruct(w.shape, F32)] * 4,
        compiler_params=_cparams(2),
    )(*slots, *(own or []), w, m, v)


def kernel(x, ln_g, ln_b, rw_mu, rw_w0, rw_w1, rw_w2, rw_a0, rw_a1, rw_a2, rw_g1, rw_g2, rw_k_k, rw_k_a, rw_r_k, rw_wr, rw_wk, rw_wv, rw_wo, rw_lnx_g, rw_lnx_b, s5_a_re, s5_a_im, s5_log_dt, s5_b_re, s5_b_im, s5_c_re, s5_c_im, s5_d, s5_w_glu, mlp_w1, mlp_w2, loss_target, m_ln_g, m_ln_b, m_rw_mu, m_rw_w0, m_rw_w1, m_rw_w2, m_rw_a0, m_rw_a1, m_rw_a2, m_rw_g1, m_rw_g2, m_rw_k_k, m_rw_k_a, m_rw_r_k, m_rw_wr, m_rw_wk, m_rw_wv, m_rw_wo, m_rw_lnx_g, m_rw_lnx_b, m_s5_a_re, m_s5_a_im, m_s5_log_dt, m_s5_b_re, m_s5_b_im, m_s5_c_re, m_s5_c_im, m_s5_d, m_s5_w_glu, m_mlp_w1, m_mlp_w2, v_ln_g, v_ln_b, v_rw_mu, v_rw_w0, v_rw_w1, v_rw_w2, v_rw_a0, v_rw_a1, v_rw_a2, v_rw_g1, v_rw_g2, v_rw_k_k, v_rw_k_a, v_rw_r_k, v_rw_wr, v_rw_wk, v_rw_wv, v_rw_wo, v_rw_lnx_g, v_rw_lnx_b, v_s5_a_re, v_s5_a_im, v_s5_log_dt, v_s5_b_re, v_s5_b_im, v_s5_c_re, v_s5_c_im, v_s5_d, v_s5_w_glu, v_mlp_w1, v_mlp_w2):
    given = dict(locals())
    local_w = {n: given[n] for n in WEIGHTS}
    local_m = {n: given["m_" + n] for n in WEIGHTS}
    local_v = {n: given["v_" + n] for n in WEIGHTS}
    small = [n for n in WEIGHTS if n in SHARDED and n not in BIG]

    my_slot = 4 * lax.axis_index("x") + 2 * lax.axis_index("y") + lax.axis_index("c")
    as_bf16 = {n: local_w[n].astype(BF16) for n in BIG}

    gathered = _all_gather("weights_all_gather", [local_w[n] for n in small])
    full = dict(local_w)
    for n, blk in zip(small, gathered):
        full[n] = jnp.moveaxis(blk, 0, SHARDED[n]).reshape(_full_shape(local_w[n].shape, SHARDED[n]))
    mid = _TwoStageGather("weights_mid", {n: as_bf16[n] for n in MID}, my_slot)
    late = lambda after: _TwoStageGather("weights_late", {n: as_bf16[n] for n in LATE}, my_slot, after)
    full['rw_w0'] = full['rw_w0'] + mid.zero

    exchanges = []

    def exchange(entries):
        started = _push_start(f"grads_start_{len(exchanges)}", [t for _, _, t in entries], [n in REPLICATED for n, _, _ in entries])
        exchanges.append((entries, started))
        return started['zero']

    loss_sq, dx, grads = _local_step(x[0], loss_target[0], full, mid, late, exchange)
    loss = (0.5 / D_MODEL) * lax.psum(loss_sq, MESH_AXES)

    wide = {'s5_b_re': (SSM_LANES, SSM_GROUP), 's5_b_im': (SSM_LANES, SSM_GROUP),
            's5_c_re': (D_MODEL, SSM_STATE), 's5_c_im': (D_MODEL, SSM_STATE)}
    by_name, many = {}, []
    landed = {n: {} for n in BIG}
    for e, (entries, started) in enumerate(exchanges):
        sources, lands = _push_wait(f"grads_wait_{e}", started, dx)
        for (n, i, _), src, slots in zip(entries, sources, lands):
            if n in wide:
                as_view = lambda t: t.reshape((1,) + wide[n])
                out = _adam_call(f"adamw_{n}", [slots.reshape((N_DEV,) + wide[n])], as_view(local_w[n]), as_view(local_m[n]),
                                 as_view(local_v[n]), own=[src.reshape(wide[n])])
                by_name[n] = [t.reshape(local_w[n].shape) for t in out]
            elif n in REPLICATED:
                many.append((n, (slots, local_w[n], local_m[n], local_v[n], src)))
            else:
                landed[n][i] = (slots, lax.dynamic_index_in_dim(src, my_slot, 0, keepdims=False))
    for n in BIG:
        layers = [landed[n][i] for i in sorted(landed[n])]
        by_name[n] = _adam_call(f"adamw_{n}", [s for s, _ in layers], local_w[n], local_m[n], local_v[n], own=[o for _, o in layers])
    split = lambda n: jnp.moveaxis(grads[n].reshape(_split_shape(local_w[n].shape, SHARDED[n])), SHARDED[n], 0)
    for n, slots in zip(small, _all_to_all("grads_all_to_all", [split(n) for n in small])):
        many.append((n, (slots, local_w[n], local_m[n], local_v[n], None)))
    last = [n for n in REPLICATED if n not in by_name and n not in dict(many)]
    for n, slots in zip(last, _all_gather("grads_all_gather", [grads[n] for n in last])):
        many.append((n, (slots, local_w[n], local_m[n], local_v[n], None)))
    for (n, _), out in zip(many, _adam_many("adamw_small", [item for _, item in many])):
        by_name[n] = out
    results = [by_name[n][k] for k in range(4) for n in WEIGHTS]
    return (loss, dx[None], *results)
```

```python
import functools
import math

import jax
import jax.numpy as jnp
from jax import lax
from jax.experimental import pallas as pl
from jax.experimental.pallas import tpu as pltpu

F32 = jnp.float32
BF16 = jnp.bfloat16

D_MODEL = 1024
HEAD = 64
PAIR = 2 * HEAD
N_PAIR = D_MODEL // PAIR
CHUNK = 64
GN_EPS = 64e-5
LN_EPS = 1e-5
SSM_GROUP = 16
SSM_STATE = 64
SSM_GROUPS = D_MODEL // SSM_GROUP
SSM_LANES = SSM_GROUPS * SSM_STATE
SSM_BLK_IN = 128
SSM_BLK_ST = 512
N_SSM_BLK = D_MODEL // SSM_BLK_IN
DEPTH = 2
DN_ALPHA = (2.0 * DEPTH) ** 0.25
ADAM_LR, ADAM_B1, ADAM_B2, ADAM_EPS, ADAM_WD, ADAM_STEP = 0.001, 0.9, 0.999, 1e-08, 0.01, 10
N_DEV = 8
MESH_AXES = ("x", "y", "c")
VMEM_LIMIT = 56 * 1024 * 1024
TOKEN_TILE = 256
LN_TILE = 512
WIDE_ROW_TILE = 2048

SHARDED = {
    "rw_mu": 2, "rw_w1": 1, "rw_w2": 2, "rw_a1": 1, "rw_a2": 2, "rw_g1": 1, "rw_g2": 2,
    "rw_wr": 1, "rw_wk": 1, "rw_wv": 1, "rw_wo": 1, "s5_d": 1, "s5_w_glu": 2, "mlp_w1": 2, "mlp_w2": 1,
}
WEIGHTS = ['ln_g', 'ln_b', 'rw_mu', 'rw_w0', 'rw_w1', 'rw_w2', 'rw_a0', 'rw_a1', 'rw_a2', 'rw_g1', 'rw_g2', 'rw_k_k',
           'rw_k_a', 'rw_r_k', 'rw_wr', 'rw_wk', 'rw_wv', 'rw_wo', 'rw_lnx_g', 'rw_lnx_b', 's5_a_re', 's5_a_im',
           's5_log_dt', 's5_b_re', 's5_b_im', 's5_c_re', 's5_c_im', 's5_d', 's5_w_glu', 'mlp_w1', 'mlp_w2']
REPLICATED = [n for n in WEIGHTS if n not in SHARDED]
BIG = ['rw_wr', 'rw_wk', 'rw_wv', 'rw_wo', 's5_w_glu', 'mlp_w1', 'mlp_w2']
MID = ['rw_wr', 'rw_wk', 'rw_wv']
LATE = ['rw_wo', 's5_w_glu', 'mlp_w1', 'mlp_w2']
GRAD_DTYPE = BF16


def _cparams(n_grid):
    return pltpu.CompilerParams(dimension_semantics=("arbitrary",) * n_grid, vmem_limit_bytes=VMEM_LIMIT)


@jax.custom_vjp
def _mm(x, w):
    return jnp.dot(x.astype(BF16), w.astype(BF16), preferred_element_type=F32)


def _mm_fwd(x, w):
    return _mm(x, w), (x, w)


def _mm_bwd(res, dy):
    x, w = res
    dyb = dy.astype(BF16)
    dx = lax.dot_general(dyb, w.astype(BF16), (((1,), (1,)), ((), ())), preferred_element_type=F32)
    dw = lax.dot_general(x.astype(BF16), dyb, (((0,), (0,)), ((), ())), preferred_element_type=F32)
    return dx, dw


_mm.defvjp(_mm_fwd, _mm_bwd)


_DOT_DIMS = {'nn': (((1,), (0,)), ((), ())), 'nt': (((1,), (1,)), ((), ())), 'tn': (((0,), (0,)), ((), ()))}
_BATCH_DOT_DIMS = {'nn': (((2,), (1,)), ((0,), (0,))), 'nt': (((2,), (2,)), ((0,), (0,))), 'tn': (((1,), (1,)), ((0,), (0,)))}
CHUNK_PASSES = 1


def _split_bf16(a):
    hi = a.astype(BF16)
    return hi, (a - hi.astype(F32)).astype(BF16)


def _dot_passes(a, b, mode, passes=None):
    dims = (_DOT_DIMS if a.ndim == 2 else _BATCH_DOT_DIMS)[mode]
    dot = lambda p, q: lax.dot_general(p, q, dims, preferred_element_type=F32)
    if (CHUNK_PASSES if passes is None else passes) == 1:
        return dot(a.astype(BF16), b.astype(BF16))
    (ah, al), (bh, bl) = _split_bf16(a), _split_bf16(b)
    return dot(ah, bh) + (dot(ah, bl) + dot(al, bh))


@functools.partial(jax.custom_vjp, nondiff_argnums=(2,))
def _pdot(a, b, mode):
    return _dot_passes(a, b, mode)


def _pdot_fwd(a, b, mode):
    return _dot_passes(a, b, mode), (a, b)


def _pdot_bwd(mode, res, dy):
    a, b = res
    if mode == 'nn':
        return _dot_passes(dy, b, 'nt'), _dot_passes(a, dy, 'tn')
    if mode == 'nt':
        return _dot_passes(dy, b, 'nn'), _dot_passes(dy, a, 'tn')
    return _dot_passes(b, dy, 'nt'), _dot_passes(a, dy, 'nn')


_pdot.defvjp(_pdot_fwd, _pdot_bwd)


def _tri_sum(x, upper):
    nb, n = x.shape[0], x.shape[1]
    ti = lax.broadcasted_iota(jnp.int32, (nb, n, n), 1)
    tj = lax.broadcasted_iota(jnp.int32, (nb, n, n), 2)
    tri = ((ti <= tj) if upper else (ti >= tj)).astype(BF16)
    hi = x.astype(BF16)
    rest = x - hi.astype(F32)
    mid = rest.astype(BF16)
    lo = (rest - mid.astype(F32)).astype(BF16)
    dot = lambda q: lax.dot_general(tri, q, _BATCH_DOT_DIMS['nn'], preferred_element_type=F32)
    return dot(hi) + (dot(mid) + dot(lo))


@jax.custom_vjp
def _cumsum_rows(x):
    return _tri_sum(x, False)


_cumsum_rows.defvjp(lambda x: (_tri_sum(x, False), None), lambda _, dy: (_tri_sum(dy, True),))


def _power_sum(a, order):
    m = a.shape[-1]
    ti = lax.broadcasted_iota(jnp.int32, (1, m, m), 1)
    tj = lax.broadcasted_iota(jnp.int32, (1, m, m), 2)
    pw, total = _dot_passes(a, a, 'nn'), (ti == tj).astype(F32) + a
    for _ in range(int(math.log2(order)) - 2):
        both = _dot_passes(pw, jnp.concatenate([pw, total], axis=2), 'nn')
        pw, total = both[:, :, :m], total + both[:, :, m:]
    return total + _dot_passes(pw, total, 'nn')


@functools.partial(jax.custom_vjp, nondiff_argnums=(1,))
def _nilpotent_inverse(a, order):
    return _power_sum(a, order)


def _nilpotent_inverse_fwd(a, order):
    inv = _power_sum(a, order)
    return inv, inv


def _nilpotent_inverse_bwd(order, inv, g):
    return (_dot_passes(_dot_passes(inv, g, 'tn'), inv, 'nt'),)


_nilpotent_inverse.defvjp(_nilpotent_inverse_fwd, _nilpotent_inverse_bwd)


@jax.custom_vjp
def _known_inverse(a, inv):
    return inv


_known_inverse.defvjp(lambda a, inv: (inv, inv),
                      lambda inv, g: (_dot_passes(_dot_passes(inv, g, 'tn'), inv, 'nt'), jnp.zeros_like(inv)))


@functools.partial(jax.custom_vjp, nondiff_argnums=(1,))
def _roll_rows(x, shift):
    return pltpu.roll(x, shift, 0)


def _roll_rows_fwd(x, shift):
    return pltpu.roll(x, shift, 0), None


def _roll_rows_bwd(shift, _, dy):
    n = dy.shape[0]
    return (pltpu.roll(dy, (n - shift) % n, 0),)


_roll_rows.defvjp(_roll_rows_fwd, _roll_rows_bwd)


def _shift_down(x, first_row):
    row = lax.broadcasted_iota(jnp.int32, x.shape, 0)
    return jnp.where(row == 0, first_row, _roll_rows(x, 1))


def _sigmoid(x):
    return 1.0 / (1.0 + jnp.exp(-x))


def _softplus(x):
    return jnp.maximum(x, 0.0) + jnp.log(1.0 + jnp.exp(-jnp.abs(x)))


def _gelu(x):
    return 0.5 * x * (1.0 + jnp.tanh(math.sqrt(2.0 / math.pi) * (x + 0.044715 * (x * x * x))))


def _layer_norm(x, g, b):
    mu = jnp.mean(x, axis=-1, keepdims=True)
    xc = x - mu
    var = jnp.mean(xc * xc, axis=-1, keepdims=True)
    return xc * lax.rsqrt(var + LN_EPS) * g + b


def _spec(block, index_map):
    return pl.BlockSpec(block, index_map)


def _tile(arr, tm):
    return (arr, (tm, arr.shape[1]), lambda i: (i, 0))


def _full(arr):
    nd = arr.ndim
    return (arr, arr.shape, lambda *_: (0,) * nd)


def _fwd_call(name, fn, grid, prims, outs):
    n_in = len(prims)

    def body(*refs):
        vals = [r[...] for r in refs[:n_in]]
        vals = [v.astype(F32) if v.dtype != F32 else v for v in vals]
        res = fn(*vals)
        for o, r in zip(refs[n_in:], res):
            o[...] = r.astype(o.dtype)

    return pl.pallas_call(
        body, name=name, grid=grid,
        in_specs=[_spec(b, m) for _, b, m in prims],
        out_specs=[_spec(b, m) for _, _, b, m in outs],
        out_shape=[jax.ShapeDtypeStruct(s, d) for s, d, _, _ in outs],
        compiler_params=_cparams(len(grid)),
    )(*[a for a, _, _ in prims])


def _bwd_call(name, fn, grid, prims, cts, wants, addends=None):
    addends = addends or {}
    n_in, n_ct = len(prims), len(cts)
    out_place = {}
    wants, out_dtype = list(wants), {}
    for i, w in enumerate(wants):
        if isinstance(w, tuple):
            wants[i], out_place[i] = w
        elif w == 'tile_bf16':
            wants[i], out_dtype[i] = 'tile', BF16
    diff = [i for i, w in enumerate(wants) if w]
    add_idx = [i for i in diff if i in addends]
    last_axis = len(grid) - 1

    def body(*refs):
        vals = [r[...] for r in refs[:n_in]]
        vals = [v.astype(F32) if v.dtype != F32 else v for v in vals]
        ct_vals = [r[...] for r in refs[n_in:n_in + n_ct]]
        ct_vals = [v.astype(F32) if v.dtype != F32 else v for v in ct_vals]
        add_refs = dict(zip(add_idx, refs[n_in + n_ct:n_in + n_ct + len(add_idx)]))
        out_refs = refs[n_in + n_ct + len(add_idx):]

        def f(*dargs):
            full = list(vals)
            for i, a in zip(diff, dargs):
                full[i] = a
            return tuple(fn(*full))

        _, vjp = jax.vjp(f, *[vals[i] for i in diff])
        grads = vjp(tuple(ct_vals))
        first = pl.program_id(last_axis) == 0
        for o, g, i in zip(out_refs, grads, diff):
            if wants[i] == 'tile':
                if i in add_refs:
                    g = g + add_refs[i][...]
                o[...] = g.astype(o.dtype)
            else:
                @pl.when(first)
                def _(o=o):
                    o[...] = jnp.zeros(o.shape, o.dtype)
                o[...] += g

    ins = list(prims) + list(cts) + [addends[i] for i in add_idx]
    places = [out_place.get(i, (prims[i][0].shape, prims[i][1], prims[i][2])) for i in diff]
    return pl.pallas_call(
        body, name=name, grid=grid,
        in_specs=[_spec(b, m) for _, b, m in ins],
        out_specs=[_spec(b, m) for _, b, m in places],
        out_shape=[jax.ShapeDtypeStruct(s, out_dtype.get(i, F32)) for i, (s, _, _) in zip(diff, places)],
        compiler_params=_cparams(len(grid)),
    )(*[a for a, _, _ in ins])


def _matmul(name, a, b, mode, *, out_dtype=F32, lhs_fn=None, epi=None, extra=None, tm=1024, tn=1024, tk=1024,
            b_cols=None, b_rows=None, out_cols=False):
    if mode == 'tn':
        kdim, m = a.shape
    else:
        m, kdim = a.shape
    if b_rows is not None:
        loc, cols = b.shape[-2], b.shape[-1]
        if mode == 'nn':
            n, k_shards = cols, max(1, tk // loc)
            tk = k_shards * loc
        else:
            n, tn = N_DEV * loc, loc
    elif b_cols is None:
        n = b.shape[0] if mode == 'nt' else b.shape[1]
    else:
        rows, loc = b.shape[-2], b.shape[-1]
        if mode == 'nn':
            n, n_shards = N_DEV * loc, max(1, tn // loc)
            tn = n_shards * loc
        else:
            n, k_shards = rows, max(1, tk // loc)
            tk = k_shards * loc
    if out_cols:
        o_shards = max(1, min(tn, n) // (n // N_DEV))
        tn = o_shards * (n // N_DEV)
    tm, tn, tk = min(tm, m), min(tn, n), min(tk, kdim)
    nk = kdim // tk
    grid = (m // tm, n // tn, nk)
    a_spec = _spec((tk, tm), lambda i, j, k: (k, i)) if mode == 'tn' else _spec((tm, tk), lambda i, j, k: (i, k))
    if b_rows is not None:
        lead = tuple(b_rows)
        skip = (None,) * (1 + len(lead))
        if mode == 'nn':
            b_spec = _spec((k_shards,) + skip[1:] + (tk // k_shards, tn), lambda i, j, k: (k,) + lead + (0, j))
        else:
            b_spec = _spec(skip + (tn, tk), lambda i, j, k: (j,) + lead + (0, k))
    elif b_cols is None:
        b_spec = _spec((tn, tk), lambda i, j, k: (j, k)) if mode == 'nt' else _spec((tk, tn), lambda i, j, k: (k, j))
    else:
        lead = tuple(b_cols)
        skip = (None,) * (1 + len(lead))
        if mode == 'nn':
            b_spec = _spec((n_shards,) + skip[1:] + (tk, tn // n_shards), lambda i, j, k: (j,) + lead + (k, 0))
        else:
            b_spec = _spec((k_shards,) + skip[1:] + (tn, tk // k_shards), lambda i, j, k: (k,) + lead + (j, 0))
    o_spec = _spec((tm, tn), lambda i, j, k: (i, j))
    if out_cols:
        o_place, o_shape = _spec((o_shards, tm, tn // o_shards), lambda i, j, k: (j, i, 0)), (N_DEV, m, n // N_DEV)
    else:
        o_place, o_shape = o_spec, (m, n)
    dims = _DOT_DIMS[mode]

    def body(*refs):
        if extra is None:
            a_ref, b_ref, o_ref, acc = refs
            x_ref = None
        else:
            a_ref, b_ref, x_ref, o_ref, acc = refs
        k = pl.program_id(2)
        av = a_ref[...]
        if lhs_fn is not None:
            av = lhs_fn(av.astype(F32))
        bv = b_ref[...]
        if bv.ndim == 3:
            bv = bv.reshape(tk, tn) if b_rows is not None else jnp.concatenate([bv[g] for g in range(bv.shape[0])], axis=1)
        part = lax.dot_general(av.astype(BF16), bv.astype(BF16), dims, preferred_element_type=F32)

        def finish(r):
            if epi is not None:
                r = epi(r, x_ref[...])
            if out_cols:
                width = tn // o_shards
                for g in range(o_shards):
                    o_ref[g] = r[:, g * width:(g + 1) * width].astype(o_ref.dtype)
            else:
                o_ref[...] = r.astype(o_ref.dtype)

        if nk == 1:
            finish(part)
            return

        @pl.when(k == 0)
        def _():
            acc[...] = part

        @pl.when((k > 0) & (k < nk - 1))
        def _():
            acc[...] += part

        @pl.when(k == nk - 1)
        def _():
            finish(acc[...] + part)

    ops, specs = [a, b], [a_spec, b_spec]
    if extra is not None:
        ops.append(extra)
        specs.append(o_spec)
    return pl.pallas_call(
        body, name=name, grid=grid, in_specs=specs, out_specs=o_place,
        out_shape=jax.ShapeDtypeStruct(o_shape, out_dtype),
        scratch_shapes=[pltpu.VMEM((tm, tn), F32)],
        compiler_params=_cparams(3),
    )(*ops)


def _relu2(x):
    r = jnp.maximum(x, 0.0)
    return r * r


def _mix_fn(x, halo, mu, w0, w1, w2, a0, a1, a2, g1, g2):
    prev_row = jnp.where(pl.program_id(0) == 0, 0.0, halo[7:8, :])
    xx = _shift_down(x, prev_row) - x
    xr, xw, xk, xv, xa, xg = (x + xx * mu[i:i + 1, :] for i in range(6))
    w_pre = w0 + _mm(jnp.tanh(_mm(xw, w1)), w2)
    log_decay = -jnp.exp(-_softplus(-w_pre) - 0.5)
    a = _sigmoid(a0 + _mm(_mm(xa, a1), a2))
    g = _mm(_sigmoid(_mm(xg, g1)), g2)
    return xr, xk, xv, log_decay, a, g


def _chunk_fn(r, lw, k, v, a, k_k, k_a, r_k, inverse=None, with_inverse=False):
    n, n_pair = r.shape[0], r.shape[1] // PAIR
    to_batch = lambda t: jnp.concatenate([t[None, :, p * PAIR:(p + 1) * PAIR] for p in range(n_pair)], axis=0)
    from_batch = lambda t: jnp.concatenate([t[p] for p in range(n_pair)], axis=1)
    r, lw, k, v, a, k_k, k_a, r_k = (to_batch(t) for t in (r, lw, k, v, a, k_k, k_a, r_k))
    lane = lax.broadcasted_iota(jnp.int32, (1, 1, PAIR), 2)
    m0 = lane < HEAD

    def head_sum(t):
        s0 = jnp.sum(jnp.where(m0, t, 0.0), axis=2, keepdims=True)
        s1 = jnp.sum(jnp.where(m0, 0.0, t), axis=2, keepdims=True)
        return jnp.where(m0, s0, s1)

    kk = k * k_k
    kk = kk / jnp.maximum(jnp.sqrt(head_sum(kk * kk)), 1e-12)
    k2 = k * (1.0 + (a - 1.0) * k_a)
    aa, bb = -kk, kk * a
    bonus = head_sum(r * k2 * r_k) * v

    cum = _cumsum_rows(lw)
    p_in, p_ex, p_inv = jnp.exp(cum), jnp.exp(cum - lw), jnp.exp(-cum)
    at, rt, kt, bt = aa * p_ex, r * p_in, k2 * p_inv, bb * p_inv

    def stack_masked(t):
        return jnp.concatenate([jnp.where(m0, t, 0.0), jnp.where(m0, 0.0, t)], axis=1)

    def unstack_sum(t):
        return t[:, :n] + t[:, n:]

    def unstack_select(t):
        return jnp.where(m0, t[:, :n], t[:, n:])

    ti = lax.broadcasted_iota(jnp.int32, (1, 2 * n, 2 * n), 1)
    tj = lax.broadcasted_iota(jnp.int32, (1, 2 * n, 2 * n), 2)
    same = (ti >= n) == (tj >= n)
    incl, strict = same & (ti >= tj), same & (ti > tj)
    m = 2 * n
    rows = lambda *ts: jnp.concatenate(ts, axis=1)
    cols = lambda *ts: jnp.concatenate(ts, axis=2)
    at_s, rt_s, kt_s, bt_s = stack_masked(at), stack_masked(rt), stack_masked(kt), stack_masked(bt)
    v_s, zeros = rows(v, v), jnp.zeros_like(at_s)
    bk_s = rows(bt_s, kt_s)
    gram = _pdot(rows(at_s, rt_s), bk_s, 'nt')
    a_ab = jnp.where(strict, gram[:, :m, :m], 0.0)
    a_ak = jnp.where(strict, gram[:, :m, m:], 0.0)
    a_rb = jnp.where(incl, gram[:, m:, :m], 0.0)
    a_rk = jnp.where(incl, gram[:, m:, m:], 0.0)
    inv = _nilpotent_inverse(a_ab, n) if inverse is None else _known_inverse(a_ab, inverse)
    wy_s = _pdot(inv, cols(at_s, _pdot(a_ak, v_s, 'nn')), 'nn')
    q_oi = _pdot(cols(a_rb, a_rk), rows(wy_s, cols(zeros, v_s)), 'nn')
    q_all = unstack_sum(rt_s + q_oi[:, :, :PAIR])
    oi_all = unstack_select(q_oi[:, :, PAIR:])
    own_head = (lax.broadcasted_iota(jnp.int32, (1, m, PAIR), 1) >= n) == (lax.broadcasted_iota(jnp.int32, (1, m, PAIR), 2) >= HEAD)
    wy_m = cols(wy_s[:, :, :PAIR], jnp.where(own_head, wy_s[:, :, PAIR:], 0.0))
    mm_nn = _pdot(bk_s, rows(wy_m, cols(zeros, stack_masked(v))), 'tn')
    mm, nn = mm_nn[:, :, :PAIR], mm_nn[:, :, PAIR:]
    ei = lax.broadcasted_iota(jnp.int32, (1, PAIR, PAIR), 1)
    ej = lax.broadcasted_iota(jnp.int32, (1, PAIR, PAIR), 2)
    eye_p = (ei == ej).astype(F32)
    decay_col = jnp.sum(eye_p * p_in[:, n - 1:n, :], axis=2, keepdims=True)
    maps = ((decay_col * (eye_p + mm))[None], (decay_col * nn)[None], from_batch(q_all), from_batch(oi_all),
            from_batch(bonus))
    return maps + (inv,) if with_inverse else maps


def _gate_fn(o, bonus, g, lnx_g, lnx_b):
    lane = lax.broadcasted_iota(jnp.int32, (1, PAIR), 1)
    m0 = lane < HEAD

    def head_mean(t):
        s0 = jnp.sum(jnp.where(m0, t, 0.0), axis=1, keepdims=True)
        s1 = jnp.sum(jnp.where(m0, 0.0, t), axis=1, keepdims=True)
        return jnp.where(m0, s0, s1) * (1.0 / HEAD)

    outs = []
    for p in range(N_PAIR):
        sl = slice(p * PAIR, (p + 1) * PAIR)
        oc = o[:, sl] - head_mean(o[:, sl])
        on = oc * lax.rsqrt(head_mean(oc * oc) + GN_EPS) * lnx_g[:, sl] + lnx_b[:, sl]
        outs.append((on + bonus[:, sl]) * g[:, sl])
    return (jnp.concatenate(outs, axis=1),)


def _res_ln_fn(h, y, g, b):
    return (_layer_norm(DN_ALPHA * h + y, g, b),)


def _glu_ln_fn(h, z, g, b):
    mix = z[:, :D_MODEL] * _sigmoid(z[:, D_MODEL:])
    return (_layer_norm(DN_ALPHA * h + mix, g, b),)


def _s5_param_fn(a_re, a_im, log_dt, b_re, b_im):
    dt = jnp.exp(log_dt)
    lam_re = jnp.minimum(a_re, -1e-4)
    lam_im = a_im
    mag = jnp.exp(dt * lam_re)
    abar_re = mag * jnp.cos(dt * lam_im)
    abar_im = mag * jnp.sin(dt * lam_im)
    den = lam_re * lam_re + lam_im * lam_im
    nr, ni = abar_re - 1.0, abar_im
    coef_re = (nr * lam_re + ni * lam_im) / den
    coef_im = (ni * lam_re - nr * lam_im) / den
    return abar_re, abar_im, coef_re, coef_im


def _s5_bbar_fn(coef_re, coef_im, b_re, b_im):
    return coef_re * b_re - coef_im * b_im, coef_re * b_im + coef_im * b_re


def _s5_in_fn(u, b_re, b_im):
    return _pdot(u, b_re[0], 'nn'), _pdot(u, b_im[0], 'nn')


def _s5_out_fn(s_re, s_im, u, c_re, c_im, d):
    y = _pdot(s_re, c_re[0], 'nn') - _pdot(s_im, c_im[0], 'nn') + u * d
    return (_gelu(y),)


def _pairs_to_batch(t):
    return jnp.concatenate([t[None, :, p * PAIR:(p + 1) * PAIR] for p in range(N_PAIR)], axis=0)


def _batch_to_pairs(t):
    return jnp.concatenate([t[p] for p in range(N_PAIR)], axis=1)


STATE_PASSES = 1


def _rwkv_scan_fwd(r, lw, k, v, a, k_k, k_a, r_k):
    n_tok = r.shape[0]
    n_chunk = n_tok // CHUNK

    def body(r_ref, lw_ref, k_ref, v_ref, a_ref, kk_ref, ka_ref, rk_ref, o_ref, bonus_ref, zs_ref, inv_ref, m_ref, q_ref, z):
        @pl.when(pl.program_id(0) == 0)
        def _():
            z[...] = jnp.zeros(z.shape, F32)

        m_all, n_all, q, oi, bonus, inv = _chunk_fn(r_ref[...], lw_ref[...], k_ref[...], v_ref[...], a_ref[...],
                                                    kk_ref[...], ka_ref[...], rk_ref[...], with_inverse=True)
        zb = z[...]
        zs_ref[0] = zb
        inv_ref[0] = inv.astype(inv_ref.dtype)
        m_ref[...] = m_all
        q_ref[...] = q
        bonus_ref[...] = bonus
        o_ref[...] = _batch_to_pairs(_dot_passes(_pairs_to_batch(q), zb, 'nn', STATE_PASSES)) + oi
        z[...] = _dot_passes(m_all[0], zb, 'nn', STATE_PASSES) + n_all[0]

    tok = _spec((CHUNK, D_MODEL), lambda c: (c, 0))
    par = _spec((1, D_MODEL), lambda c: (0, 0))
    st = _spec((1, N_PAIR, PAIR, PAIR), lambda c: (c, 0, 0, 0))
    return pl.pallas_call(
        body, name="rwkv_scan_fwd", grid=(n_chunk,),
        in_specs=[tok] * 5 + [par] * 3, out_specs=[tok, tok, st, st, st, tok],
        out_shape=[jax.ShapeDtypeStruct((n_tok, D_MODEL), F32)] * 2
        + [jax.ShapeDtypeStruct((n_chunk, N_PAIR, PAIR, PAIR), F32), jax.ShapeDtypeStruct((n_chunk, N_PAIR, PAIR, PAIR), BF16),
           jax.ShapeDtypeStruct((n_chunk, N_PAIR, PAIR, PAIR), F32), jax.ShapeDtypeStruct((n_tok, D_MODEL), F32)],
        scratch_shapes=[pltpu.VMEM((N_PAIR, PAIR, PAIR), F32)],
        compiler_params=_cparams(1),
    )(r, lw, k, v, a, k_k, k_a, r_k)


def _rwkv_scan_bwd(r, lw, k, v, a, k_k, k_a, r_k, zs, invs, ms, q, d_o, d_bonus):
    n_tok = r.shape[0]
    n_chunk = n_tok // CHUNK

    def body(r_ref, lw_ref, k_ref, v_ref, a_ref, kk_ref, ka_ref, rk_ref, zs_ref, inv_ref, m_ref, q_ref, do_ref, db_ref,
             dr_ref, dlw_ref, dk_ref, dv_ref, da_ref, dkk_ref, dka_ref, drk_ref, g):
        sums = (dkk_ref, dka_ref, drk_ref)

        @pl.when(pl.program_id(0) == 0)
        def _():
            g[...] = jnp.zeros(g.shape, F32)
            for s_ref in sums:
                s_ref[...] = jnp.zeros(s_ref.shape, F32)

        prim = (r_ref[...], lw_ref[...], k_ref[...], v_ref[...], a_ref[...], kk_ref[...], ka_ref[...], rk_ref[...])
        known = inv_ref[0].astype(F32)
        _, chunk_vjp = jax.vjp(lambda *p: _chunk_fn(*p, inverse=known), *prim)
        m_all, q = m_ref[...], q_ref[...]
        gb, zb, d_out = g[...], zs_ref[0], do_ref[...]
        dob = _pairs_to_batch(d_out)
        d_m = _dot_passes(gb, zb, 'nt', STATE_PASSES)
        d_q = _batch_to_pairs(_dot_passes(dob, zb, 'nt', STATE_PASSES))
        grads = chunk_vjp((d_m[None], gb[None], d_q, d_out, db_ref[...]))
        for out_ref, val in zip((dr_ref, dlw_ref, dk_ref, dv_ref, da_ref), grads[:5]):
            out_ref[...] = val.astype(out_ref.dtype)
        for s_ref, val in zip(sums, grads[5:]):
            s_ref[...] += val
        same_head = (lax.broadcasted_iota(jnp.int32, (1, PAIR, PAIR), 1) // HEAD
                     == lax.broadcasted_iota(jnp.int32, (1, PAIR, PAIR), 2) // HEAD)
        g[...] = (_dot_passes(m_all[0], gb, 'tn', STATE_PASSES)
                  + jnp.where(same_head, _dot_passes(_pairs_to_batch(q), dob, 'tn', STATE_PASSES), 0.0))

    tok = _spec((CHUNK, D_MODEL), lambda c: (n_chunk - 1 - c, 0))
    par = _spec((1, D_MODEL), lambda c: (0, 0))
    st = _spec((1, N_PAIR, PAIR, PAIR), lambda c: (n_chunk - 1 - c, 0, 0, 0))
    tok_shape = lambda dt: jax.ShapeDtypeStruct((n_tok, D_MODEL), dt)
    return pl.pallas_call(
        body, name="rwkv_scan_bwd", grid=(n_chunk,),
        in_specs=[tok] * 5 + [par] * 3 + [st, st, st, tok, tok, tok], out_specs=[tok] * 5 + [par] * 3,
        out_shape=[tok_shape(BF16), tok_shape(F32), tok_shape(BF16), tok_shape(BF16), tok_shape(F32)]
        + [jax.ShapeDtypeStruct((1, D_MODEL), F32)] * 3,
        scratch_shapes=[pltpu.VMEM((N_PAIR, PAIR, PAIR), F32)],
        compiler_params=_cparams(1),
    )(r, lw, k, v, a, k_k, k_a, r_k, zs, invs, ms, q, d_o, d_bonus)


S5_TIME_TILE = 1024
S5_GROUP = 8


def _scan_rows(re_ref, im_ref, a1, carry, reverse):
    lb = re_ref.shape[1]
    grp, n_grp = S5_GROUP, re_ref.shape[0] // S5_GROUP

    def cmul(xr, xi, yr, yi):
        return xr * yr - xi * yi, xr * yi + xi * yr

    a2 = cmul(*a1, *a1)
    a4 = cmul(*a2, *a2)
    a8 = cmul(*a4, *a4)
    row = lax.broadcasted_iota(jnp.int32, (grp, lb), 0)
    expo = (grp - row) if reverse else (row + 1)
    pw = (jnp.ones((grp, lb), F32), jnp.zeros((grp, lb), F32))
    for bit, ap in ((1, a1), (2, a2), (4, a4), (8, a8)):
        nxt = cmul(*pw, *ap)
        sel = (expo & bit) != 0
        pw = (jnp.where(sel, nxt[0], pw[0]), jnp.where(sel, nxt[1], pw[1]))

    steps = []
    for d, ad in ((1, a1), (2, a2), (4, a4)):
        keep = (row < grp - d) if reverse else (row >= d)
        steps.append(((grp - d) if reverse else d, jnp.where(keep, ad[0], 0.0), jnp.where(keep, ad[1], 0.0)))

    def group(i, c):
        gi = (n_grp - 1 - i) if reverse else i
        rows = pl.ds(pl.multiple_of(gi * grp, grp), grp)
        xr, xi = re_ref[rows, :], im_ref[rows, :]
        for shift, mr, mi in steps:
            pr, pi = cmul(mr, mi, pltpu.roll(xr, shift, 0), pltpu.roll(xi, shift, 0))
            xr, xi = xr + pr, xi + pi
        cr, ci = cmul(*pw, c[0], c[1])
        xr, xi = xr + cr, xi + ci
        re_ref[rows, :] = xr
        im_ref[rows, :] = xi
        edge = slice(0, 1) if reverse else slice(grp - 1, grp)
        return xr[edge, :], xi[edge, :]

    return lax.fori_loop(0, n_grp, group, carry)


def _s5_fwd(u, b_re, b_im, a_re, a_im, c_re, c_im, d_row):
    n_tok = u.shape[0]
    tt = min(S5_TIME_TILE, n_tok)

    def body(u_ref, bre_ref, bim_ref, are_ref, aim_ref, cre_ref, cim_ref, d_ref, sre_ref, sim_ref, yg_ref, carry):
        @pl.when(pl.program_id(1) == 0)
        def _():
            carry[...] = jnp.zeros(carry.shape, F32)

        uv = u_ref[...]
        sre_ref[...], sim_ref[...] = _s5_in_fn(uv, bre_ref[...], bim_ref[...])
        c = _scan_rows(sre_ref, sim_ref, (are_ref[...], aim_ref[...]), (carry[0:1, :], carry[1:2, :]), False)
        carry[0:1, :] = c[0]
        carry[1:2, :] = c[1]
        (yg,) = _s5_out_fn(sre_ref[...], sim_ref[...], uv, cre_ref[...], cim_ref[...], d_ref[...])
        yg_ref[...] = yg.astype(yg_ref.dtype)

    u_blk = _spec((tt, SSM_BLK_IN), lambda l, t: (t, l))
    s_blk = _spec((tt, SSM_BLK_ST), lambda l, t: (t, l))
    blk3 = lambda arr: _spec((1,) + arr.shape[1:], lambda l, t: (l, 0, 0))
    par = lambda width: _spec((1, width), lambda l, t: (0, l))
    return pl.pallas_call(
        body, name="s5_fwd", grid=(N_SSM_BLK, n_tok // tt),
        in_specs=[u_blk, blk3(b_re), blk3(b_im), par(SSM_BLK_ST), par(SSM_BLK_ST), blk3(c_re), blk3(c_im), par(SSM_BLK_IN)],
        out_specs=[s_blk, s_blk, u_blk],
        out_shape=[jax.ShapeDtypeStruct((n_tok, SSM_LANES), F32)] * 2 + [jax.ShapeDtypeStruct((n_tok, D_MODEL), BF16)],
        scratch_shapes=[pltpu.VMEM((2, SSM_BLK_ST), F32)],
        compiler_params=_cparams(2),
    )(u, b_re, b_im, a_re, a_im, c_re, c_im, d_row)


def _s5_bwd(dyg, u, s_re, s_im, dh_res, b_re, b_im, a_re, a_im, c_re, c_im, d_row):
    n_tok = u.shape[0]
    tt = min(S5_TIME_TILE, n_tok)
    n_t = n_tok // tt

    def body(dyg_ref, u_ref, sre_ref, sim_ref, hre_ref, him_ref, res_ref, bre_ref, bim_ref, are_ref, aim_ref,
             cre_ref, cim_ref, d_ref, du_ref, dbre_ref, dbim_ref, dcre_ref, dcim_ref, dd_ref, dare_ref, daim_ref,
             carry, gre, gim):
        i = pl.program_id(1)
        sums = (dbre_ref, dbim_ref, dcre_ref, dcim_ref, dd_ref, dare_ref, daim_ref)

        @pl.when(i == 0)
        def _():
            carry[...] = jnp.zeros(carry.shape, F32)
            for r in sums:
                r[...] = jnp.zeros(r.shape, F32)

        uv, sre, sim = u_ref[...], sre_ref[...], sim_ref[...]
        _, out_vjp = jax.vjp(_s5_out_fn, sre, sim, uv, cre_ref[...], cim_ref[...], d_ref[...])
        gre[...], gim[...], du_out, dcre, dcim, dd = out_vjp((dyg_ref[...],))
        c = _scan_rows(gre, gim, (are_ref[...], -aim_ref[...]), (carry[0:1, :], carry[1:2, :]), True)
        carry[0:1, :] = c[0]
        carry[1:2, :] = c[1]
        g_re, g_im = gre[...], gim[...]
        at_start = i == n_t - 1
        p_re = _shift_down(sre, jnp.where(at_start, 0.0, hre_ref[7:8, :]))
        p_im = _shift_down(sim, jnp.where(at_start, 0.0, him_ref[7:8, :]))
        _, in_vjp = jax.vjp(_s5_in_fn, uv, bre_ref[...], bim_ref[...])
        du_in, dbre, dbim = in_vjp((g_re, g_im))
        du_ref[...] = du_out + du_in + res_ref[...]
        dare = jnp.sum(g_re * p_re + g_im * p_im, axis=0, keepdims=True)
        daim = jnp.sum(g_im * p_re - g_re * p_im, axis=0, keepdims=True)
        for r, val in zip(sums, (dbre, dbim, dcre, dcim, dd, dare, daim)):
            r[...] += val

    u_blk = _spec((tt, SSM_BLK_IN), lambda l, t: (n_t - 1 - t, l))
    s_blk = _spec((tt, SSM_BLK_ST), lambda l, t: (n_t - 1 - t, l))
    halo = _spec((8, SSM_BLK_ST), lambda l, t: (jnp.maximum((n_t - 1 - t) * (tt // 8) - 1, 0), l))
    blk3 = lambda arr: _spec((1,) + arr.shape[1:], lambda l, t: (l, 0, 0))
    par = lambda width: _spec((1, width), lambda l, t: (0, l))
    params = [blk3(b_re), blk3(b_im), par(SSM_BLK_ST), par(SSM_BLK_ST), blk3(c_re), blk3(c_im), par(SSM_BLK_IN)]
    shape = lambda arr: jax.ShapeDtypeStruct(arr.shape, F32)
    return pl.pallas_call(
        body, name="s5_bwd", grid=(N_SSM_BLK, n_t),
        in_specs=[u_blk, u_blk, s_blk, s_blk, halo, halo, u_blk] + params,
        out_specs=[u_blk, blk3(b_re), blk3(b_im), blk3(c_re), blk3(c_im), par(SSM_BLK_IN), par(SSM_BLK_ST), par(SSM_BLK_ST)],
        out_shape=[shape(u), shape(b_re), shape(b_im), shape(c_re), shape(c_im), shape(d_row), shape(a_re), shape(a_im)],
        scratch_shapes=[pltpu.VMEM((2, SSM_BLK_ST), F32), pltpu.VMEM((tt, SSM_BLK_ST), F32), pltpu.VMEM((tt, SSM_BLK_ST), F32)],
        compiler_params=_cparams(2),
    )(dyg, u, s_re, s_im, s_re, s_im, dh_res, b_re, b_im, a_re, a_im, c_re, c_im, d_row)


def _loss_call(h, target, tm):
    n_tok, d = h.shape

    def body(h_ref, t_ref, acc_ref, dh_ref):
        @pl.when(pl.program_id(0) == 0)
        def _():
            acc_ref[...] = jnp.zeros(acc_ref.shape, F32)

        e = h_ref[...] - t_ref[...]
        dh_ref[...] = e * (1.0 / d)
        acc_ref[...] += jnp.sum(jnp.sum(e * e, axis=1, keepdims=True), axis=0, keepdims=True)

    tok = _spec((tm, d), lambda i: (i, 0))
    return pl.pallas_call(
        body, name="loss", grid=(n_tok // tm,),
        in_specs=[tok, tok], out_specs=[_spec((8, 128), lambda i: (0, 0)), tok],
        out_shape=[jax.ShapeDtypeStruct((8, 128), F32), jax.ShapeDtypeStruct(h.shape, F32)],
        compiler_params=_cparams(1),
    )(h, target)


def _block_diag(t):
    nb, ng, rows, cols = t.shape
    eye = jnp.eye(ng, dtype=t.dtype)
    return jnp.einsum('bgrc,gh->bgrhc', t, eye).reshape(nb, ng * rows, ng * cols)


def _block_diag_t(t, rows, cols):
    nb, ng = t.shape[0], t.shape[1] // rows
    t5 = t.reshape(nb, ng, rows, ng, cols)
    return jnp.einsum('bgrhc,gh->bgrc', t5, jnp.eye(ng, dtype=t.dtype))


def _mlp_fwd(layer, h, w1g, w2g):
    pre = _matmul(f"mlp{layer}_up", h, w1g, 'nn', b_cols=(layer,), out_dtype=BF16, tm=WIDE_ROW_TILE)
    return pre, _matmul(f"mlp{layer}_down", pre, w2g, 'nn', lhs_fn=_relu2, b_rows=(layer,))


def _mlp_bwd(layer, h, pre, dy, dh_res, w1g, w2g):
    dpre = _matmul(f"mlp{layer}_down_dx", dy, w2g, 'nt', epi=lambda acc, p: acc * (2.0 * jnp.maximum(p, 0.0)),
                   extra=pre, b_rows=(layer,), out_dtype=BF16, tm=WIDE_ROW_TILE)
    dw2 = _matmul(f"mlp{layer}_down_dw", pre, dy, 'tn', lhs_fn=_relu2, out_dtype=GRAD_DTYPE)
    dw1 = _matmul(f"mlp{layer}_up_dw", h, dpre, 'tn', out_cols=True, out_dtype=GRAD_DTYPE)
    dh = _matmul(f"mlp{layer}_up_dx", dpre, w1g, 'nt', epi=lambda acc, e: acc + e, extra=dh_res, b_cols=(layer,))
    return dh, dw1, dw2


def _local_step(x, target, w, mid, start_late, exchange):
    n_tok = x.shape[0]
    tm = min(TOKEN_TILE, n_tok)
    n_tile, n_chunk = n_tok // tm, n_tok // CHUNK
    row = lambda v: v.reshape(1, -1)
    big = {}
    ln_g, ln_b = w['ln_g'], w['ln_b']
    grads = {}
    to_slots = lambda t: t.reshape(N_DEV, t.shape[0] // N_DEV, t.shape[1])

    halo_spec = (8, D_MODEL), lambda i: (jnp.maximum(i * (tm // 8) - 1, 0), 0)
    mix_p = [_tile(x, tm), (x,) + halo_spec, _full(w['rw_mu'][0]), _full(row(w['rw_w0'])), _full(w['rw_w1'][0]),
             _full(w['rw_w2'][0]), _full(row(w['rw_a0'])), _full(w['rw_a1'][0]), _full(w['rw_a2'][0]),
             _full(w['rw_g1'][0]), _full(w['rw_g2'][0])]
    tok_out = lambda dt: ((n_tok, D_MODEL), dt, (tm, D_MODEL), lambda i: (i, 0))
    xr, xk, xv, lw, a, g = _fwd_call("rwkv_mix", _mix_fn, (n_tile,), mix_p,
                                     [tok_out(BF16), tok_out(BF16), tok_out(BF16), tok_out(F32), tok_out(F32), tok_out(F32)])
    mid.forward(xr)
    square = lambda t: t.reshape(D_MODEL, D_MODEL)
    big.update({n: square(t) for n, t in mid.finish(xr).items()})
    late = start_late(big['rw_wr'])
    r = _matmul("rwkv_r", xr, big['rw_wr'], 'nn')
    k = _matmul("rwkv_k", xk, big['rw_wk'], 'nn')
    v = _matmul("rwkv_v", xv, big['rw_wv'], 'nn')

    scan_in = (r, lw, k, v, a, row(w['rw_k_k']) + late.zero, row(w['rw_k_a']), w['rw_r_k'].reshape(1, -1))
    o, bonus, *scan_saved = _rwkv_scan_fwd(*scan_in)

    zero = late.forward(o)
    tl = min(LN_TILE, n_tok)
    n_ln = n_tok // tl
    ln_out = lambda dt: ((n_tok, D_MODEL), dt, (tl, D_MODEL), lambda i: (i, 0))
    gate_p = [_tile(o, tl), _tile(bonus, tl), _tile(g, tl), _full(row(w['rw_lnx_g']) + zero), _full(row(w['rw_lnx_b']))]
    (og,) = _fwd_call("rwkv_gate", _gate_fn, (n_ln,), gate_p, [ln_out(BF16)])
    late_w = late.finish(og)
    big['rw_wo'] = square(late_w['rw_wo'])
    w1g, w2g, glu_g = late_w['mlp_w1'], late_w['mlp_w2'], late_w['s5_w_glu']
    y0 = _matmul("rwkv_o", og, big['rw_wo'], 'nn')
    ln0_p = [_tile(x, tl), _tile(y0, tl), _full(ln_g[0:1]), _full(ln_b[0:1])]
    twice = lambda fn: lambda *args: fn(*args) * 2
    h1, h1_b = _fwd_call("ln0", twice(_res_ln_fn), (n_ln,), ln0_p, [ln_out(F32), ln_out(BF16)])
    pre0, y1 = _mlp_fwd(0, h1_b, w1g, w2g)
    ln1_p = [_tile(h1, tl), _tile(y1, tl), _full(ln_g[1:2]), _full(ln_b[1:2])]
    (h2,) = _fwd_call("ln1", _res_ln_fn, (n_ln,), ln1_p, [ln_out(F32)])

    gp = (SSM_GROUPS, SSM_STATE)
    par_p = [_full(w['s5_a_re'][0]), _full(w['s5_a_im'][0]), _full(w['s5_log_dt'].reshape(SSM_GROUPS, 1))]
    s5_par_fn = lambda ar, ai, ld: _s5_param_fn(ar, ai, ld, None, None)
    gp_out = (gp, F32, gp, lambda i: (0, 0))
    abar_re, abar_im, coef_re, coef_im = _fwd_call("s5_param", s5_par_fn, (1,), par_p, [gp_out] * 4)
    b_flat = [w[n][0].reshape(SSM_LANES, SSM_GROUP) for n in ('s5_b_re', 's5_b_im')]
    bbar_p = [_full(coef_re.reshape(SSM_LANES, 1)), _full(coef_im.reshape(SSM_LANES, 1)), _full(b_flat[0]), _full(b_flat[1])]
    bb_out = ((SSM_LANES, SSM_GROUP), F32, (SSM_LANES, SSM_GROUP), lambda i: (0, 0))
    bbar_re, bbar_im = _fwd_call("s5_bbar", _s5_bbar_fn, (1,), bbar_p, [bb_out] * 2)
    to_in = lambda t: _block_diag(t.reshape(N_SSM_BLK, 8, SSM_STATE, SSM_GROUP).transpose(0, 1, 3, 2))
    to_out = lambda t: _block_diag(t.reshape(N_SSM_BLK, 8, SSM_GROUP, SSM_STATE).transpose(0, 1, 3, 2))
    bblk_re, bblk_im = to_in(bbar_re), to_in(bbar_im)
    cblk_re, cblk_im = to_out(w['s5_c_re'][0]), to_out(w['s5_c_im'][0])
    a_row_re, a_row_im = abar_re.reshape(1, SSM_LANES), abar_im.reshape(1, SSM_LANES)

    d_row = row(w['s5_d'])
    s5_params = (bblk_re, bblk_im, a_row_re, a_row_im, cblk_re, cblk_im, d_row)
    s_re, s_im, yg = _s5_fwd(h2, *s5_params)
    z = _matmul("s5_glu", yg, glu_g, 'nn', b_cols=(0,))
    ln2_p = [_tile(h2, tl), _tile(z, tl), _full(ln_g[2:3]), _full(ln_b[2:3])]
    h3, h3_b = _fwd_call("ln2", twice(_glu_ln_fn), (n_ln,), ln2_p, [ln_out(F32), ln_out(BF16)])
    pre1, y3 = _mlp_fwd(1, h3_b, w1g, w2g)
    ln3_p = [_tile(h3, tl), _tile(y3, tl), _full(ln_g[3:4]), _full(ln_b[3:4])]
    (h4,) = _fwd_call("ln3", _res_ln_fn, (n_ln,), ln3_p, [ln_out(F32)])

    loss_acc, dh4 = _loss_call(h4, target, tl)

    ln_w = ['tile', 'tile_bf16', 'acc', 'acc']
    dh3_res, dy3, dg3, db3 = _bwd_call("ln3_bwd", _res_ln_fn, (n_ln,), ln3_p, [_tile(dh4, tl)], ln_w)
    dh3, dw1_1, dw2_1 = _mlp_bwd(1, h3_b, pre1, dy3, dh3_res, w1g, w2g)
    zero = exchange([('mlp_w1', 1, dw1_1), ('mlp_w2', 1, to_slots(dw2_1))])
    ln2_p[2] = _full(ln_g[2:3] + zero)
    dh2_res, dz, dg2, db2 = _bwd_call("ln2_bwd", _glu_ln_fn, (n_ln,), ln2_p, [_tile(dh3, tl)], ln_w)
    dyg = _matmul("s5_glu_dx", dz, glu_g, 'nt', b_cols=(0,))
    dw_glu = _matmul("s5_glu_dw", yg, dz, 'tn', out_cols=True, out_dtype=GRAD_DTYPE)
    dh2, dbb_re, dbb_im, dcb_re, dcb_im, dd, dabar_re, dabar_im = _s5_bwd(dyg, h2, s_re, s_im, dh2_res, *s5_params)
    from_in = lambda t: _block_diag_t(t, SSM_GROUP, SSM_STATE).transpose(0, 1, 3, 2).reshape(SSM_LANES, SSM_GROUP)
    from_out = lambda t: _block_diag_t(t, SSM_STATE, SSM_GROUP).transpose(0, 1, 3, 2).reshape(1, SSM_GROUPS, SSM_GROUP, SSM_STATE)
    grads['s5_c_re'], grads['s5_c_im'] = from_out(dcb_re), from_out(dcb_im)
    grads['s5_d'] = dd
    dcoef_re, dcoef_im, db_re, db_im = _bwd_call(
        "s5_bbar_bwd", _s5_bbar_fn, (1,), bbar_p, [_full(from_in(dbb_re)), _full(from_in(dbb_im))], ['acc'] * 4)
    grads['s5_b_re'] = db_re.reshape(w['s5_b_re'].shape)
    grads['s5_b_im'] = db_im.reshape(w['s5_b_im'].shape)
    par_ct = [_full(dabar_re.reshape(gp)), _full(dabar_im.reshape(gp)), _full(dcoef_re.reshape(gp)), _full(dcoef_im.reshape(gp))]
    da_re, da_im, dlog_dt = _bwd_call("s5_param_bwd", s5_par_fn, (1,), par_p, par_ct, ['acc'] * 3)
    grads['s5_a_re'], grads['s5_a_im'], grads['s5_log_dt'] = da_re[None], da_im[None], dlog_dt.reshape(1, SSM_GROUPS)
    zero = exchange([(n, 0, grads[n]) for n in ('s5_a_re', 's5_a_im', 's5_log_dt', 's5_b_re', 's5_b_im', 's5_c_re', 's5_c_im')])
    ln1_p[2] = _full(ln_g[1:2] + zero)

    dh1_res, dy1, dg1, db1 = _bwd_call("ln1_bwd", _res_ln_fn, (n_ln,), ln1_p, [_tile(dh2, tl)], ln_w)
    dh1, dw1_0, dw2_0 = _mlp_bwd(0, h1_b, pre0, dy1, dh1_res, w1g, w2g)
    dx_res, dy0, dg0, db0 = _bwd_call("ln0_bwd", _res_ln_fn, (n_ln,), ln0_p, [_tile(dh1, tl)], ln_w)
    grads['ln_g'] = jnp.concatenate([dg0, dg1, dg2, dg3])
    grads['ln_b'] = jnp.concatenate([db0, db1, db2, db3])
    dog = _matmul("rwkv_o_dx", dy0, big['rw_wo'], 'nt')
    dw_o = to_slots(_matmul("rwkv_o_dw", og, dy0, 'tn', out_dtype=GRAD_DTYPE))
    zero = exchange([('s5_w_glu', 0, dw_glu), ('mlp_w1', 0, dw1_0), ('mlp_w2', 0, to_slots(dw2_0)), ('rw_wo', 0, dw_o)])
    gate_p[3] = _full(row(w['rw_lnx_g']) + zero)
    d_o, dbonus, dgate, grads['rw_lnx_g'], grads['rw_lnx_b'] = _bwd_call(
        "rwkv_gate_bwd", _gate_fn, (n_ln,), gate_p, [_tile(dog, tl)], ['tile', 'tile', 'tile', 'acc', 'acc'])
    dr, dlw, dk, dv, da, grads['rw_k_k'], grads['rw_k_a'], dr_k = _rwkv_scan_bwd(*scan_in, *scan_saved, d_o, dbonus)
    grads['rw_r_k'] = dr_k.reshape(w['rw_r_k'].shape)
    dw_r = to_slots(_matmul("rwkv_r_dw", xr, dr, 'tn', out_dtype=GRAD_DTYPE))
    dw_k = to_slots(_matmul("rwkv_k_dw", xk, dk, 'tn', out_dtype=GRAD_DTYPE))
    dw_v = to_slots(_matmul("rwkv_v_dw", xv, dv, 'tn', out_dtype=GRAD_DTYPE))
    done = ('ln_g', 'ln_b', 'rw_lnx_g', 'rw_lnx_b', 'rw_k_k', 'rw_k_a', 'rw_r_k')
    zero = exchange([('rw_wr', 0, dw_r), ('rw_wk', 0, dw_k), ('rw_wv', 0, dw_v)] + [(n, 0, grads[n]) for n in done])
    mix_p[3] = _full(row(w['rw_w0']) + zero)
    dxr = _matmul("rwkv_r_dx", dr, big['rw_wr'], 'nt')
    dxk = _matmul("rwkv_k_dx", dk, big['rw_wk'], 'nt')
    dxv = _matmul("rwkv_v_dx", dv, big['rw_wv'], 'nt')
    mix_ct = [_tile(t, tm) for t in (dxr, dxk, dxv, dlw, da, dgate)]
    halo_grad = ('tile', ((n_tile * 8, D_MODEL), (8, D_MODEL), lambda i: (i, 0)))
    res = _bwd_call("rwkv_mix_bwd", _mix_fn, (n_tile,), mix_p, mix_ct, ['tile', halo_grad] + ['acc'] * 9,
                    addends={0: _tile(dx_res, tm)})
    corr = res[1].reshape(n_tile, 8, D_MODEL)[1:, 7:8, :]
    corr = jnp.pad(corr, ((0, 1), (tm - 1, 0), (0, 0)))
    dx = (res[0].reshape(n_tile, tm, D_MODEL) + corr).reshape(n_tok, D_MODEL)
    (grads['rw_mu'], grads['rw_w0'], grads['rw_w1'], grads['rw_w2'], grads['rw_a0'], grads['rw_a1'], grads['rw_a2'],
     grads['rw_g1'], grads['rw_g2']) = [t[None] if t.shape[0] != 1 else t for t in res[2:]]
    return loss_acc[0, 0], dx, grads


def _full_shape(local_shape, axis):
    return local_shape[:axis] + (N_DEV * local_shape[axis],) + local_shape[axis + 1:]


def _split_shape(local_shape, axis):
    return local_shape[:axis] + (N_DEV, local_shape[axis]) + local_shape[axis + 1:]


def _mesh_pos():
    return lax.axis_index("x"), lax.axis_index("y"), lax.axis_index("c")


def _flip(v, f):
    return 1 - v if f else v


def _hbm_call(name, body, arrays, out_shapes, n_sems):
    n_t = len(arrays)
    hbm = pl.BlockSpec(memory_space=pl.ANY)
    return pl.pallas_call(
        body, name=name, out_shape=out_shapes, in_specs=[hbm] * n_t, out_specs=[hbm] * n_t,
        scratch_shapes=[pltpu.SemaphoreType.DMA((n_t, n_sems)), pltpu.SemaphoreType.DMA((n_t, n_sems)),
                        pltpu.SemaphoreType.DMA((n_t,))],
    )(*arrays)


def _all_gather(name, locals_):
    n_t = len(locals_)

    def body(*refs):
        x_refs, out_refs = refs[:n_t], refs[n_t:2 * n_t]
        send_sems, recv_sems, local_sems = refs[2 * n_t:]
        x, y, c = _mesh_pos()
        me, sibling = (x, y, c), (x, y, 1 - c)
        chips = [(1 - x, y), (x, 1 - y), (1 - x, 1 - y)]

        def copy(t, k, block, to, own=False):
            slot = out_refs[t].at[4 * block[0] + 2 * block[1] + block[2]]
            return pltpu.make_async_remote_copy(
                src_ref=x_refs[t] if own else slot, dst_ref=slot,
                send_sem=send_sems.at[t, k], recv_sem=recv_sems.at[t, k],
                device_id=to, device_id_type=pl.DeviceIdType.MESH)

        mine = [pltpu.make_async_copy(x_refs[t], out_refs[t].at[4 * x + 2 * y + c], local_sems.at[t]) for t in range(n_t)]
        for cp in mine:
            cp.start()
        sent = []
        for t in range(n_t):
            sent.append(copy(t, 0, me, sibling, own=True))
            sent += [copy(t, 1 + j, me, (*chip, c), own=True) for j, chip in enumerate(chips)]
        for cp in sent:
            cp.start()
        for j, chip in enumerate(chips):
            for t in range(n_t):
                copy(t, 1 + j, (*chip, c), me).wait_recv()
                passed = copy(t, 4 + j, (*chip, c), sibling)
                passed.start()
                sent.append(passed)
        for t in range(n_t):
            copy(t, 0, sibling, me).wait_recv()
            for j, chip in enumerate(chips):
                copy(t, 4 + j, (*chip, 1 - c), me).wait_recv()
        for cp in sent:
            cp.wait_send()
        for cp in mine:
            cp.wait()

    outs = [jax.ShapeDtypeStruct((N_DEV,) + a.shape, a.dtype) for a in locals_]
    return _hbm_call(name, body, locals_, outs, 7)


def _all_to_all(name, blocks):
    n_t = len(blocks)

    def body(*refs):
        x_refs, out_refs = refs[:n_t], refs[n_t:2 * n_t]
        send_sems, recv_sems, local_sems = refs[2 * n_t:]
        x, y, c = _mesh_pos()
        my_slot = 4 * x + 2 * y + c
        mine = [pltpu.make_async_copy(x_refs[t].at[my_slot], out_refs[t].at[my_slot], local_sems.at[t]) for t in range(n_t)]
        for cp in mine:
            cp.start()
        copies = []
        for k in range(1, N_DEV):
            peer = (_flip(x, k & 4), _flip(y, k & 2), _flip(c, k & 1))
            peer_slot = 4 * peer[0] + 2 * peer[1] + peer[2]
            for t in range(n_t):
                sems = dict(send_sem=send_sems.at[t, k - 1], recv_sem=recv_sems.at[t, k - 1],
                            device_id=peer, device_id_type=pl.DeviceIdType.MESH)
                cp = pltpu.make_async_remote_copy(src_ref=x_refs[t].at[peer_slot], dst_ref=out_refs[t].at[my_slot], **sems)
                cp.start()
                landed = pltpu.make_async_remote_copy(src_ref=x_refs[t].at[my_slot], dst_ref=out_refs[t].at[peer_slot], **sems)
                copies.append((cp, landed))
        for sent, landed in copies:
            landed.wait_recv()
            sent.wait_send()
        for cp in mine:
            cp.wait()

    outs = [jax.ShapeDtypeStruct(a.shape, a.dtype) for a in blocks]
    return _hbm_call(name, body, blocks, outs, 7)


_HBM = pl.BlockSpec(memory_space=pltpu.HBM)
_SEM = pl.BlockSpec(memory_space=pltpu.SEMAPHORE)
_EFFECT = pltpu.SideEffectType.DATAFLOW_SIDE_EFFECTING


def _peer_copies(x_refs, land_refs, send_sems, recv_sems, gather, mine):
    x, y, c = _mesh_pos()
    my_slot = 4 * x + 2 * y + c
    copies = []
    for k in range(1, N_DEV):
        peer = (_flip(x, k & 4), _flip(y, k & 2), _flip(c, k & 1))
        peer_slot = 4 * peer[0] + 2 * peer[1] + peer[2]
        for t, (x_ref, land) in enumerate(zip(x_refs, land_refs)):
            sem = t * (N_DEV - 1) + k - 1
            copies.append(pltpu.make_async_remote_copy(
                src_ref=x_ref if gather[t] else x_ref.at[peer_slot], dst_ref=land.at[my_slot if mine else peer_slot],
                send_sem=send_sems.at[sem], recv_sem=recv_sems.at[sem], device_id=peer, device_id_type=pl.DeviceIdType.MESH))
    return copies


def _split_start(name, held, plan, n_copies, after=None):
    n_h = len(held)

    n_in = n_h + (after is not None)

    def body(*refs):
        for sent in plan(refs[:n_h], refs[n_in], refs[n_in + 1], True):
            sent.start()
        refs[-1][...] = jnp.zeros((8, 128), F32)

    sems = pltpu.SemaphoreType.DMA((n_copies,))
    out = pl.pallas_call(
        body, name=name,
        out_shape=(sems, sems, *[pltpu.HBM(a.shape, a.dtype) for a in held], jax.ShapeDtypeStruct((8, 128), F32)),
        in_specs=[_HBM] * n_h + [pl.BlockSpec(memory_space=pl.ANY)] * (after is not None),
        out_specs=(_SEM, _SEM, *[_HBM] * n_h, pl.BlockSpec(memory_space=pltpu.VMEM)),
        input_output_aliases={i: 2 + i for i in range(n_h)},
        compiler_params=pltpu.CompilerParams(has_side_effects=_EFFECT),
    )(*[pltpu.with_memory_space_constraint(a, pltpu.HBM) for a in held], *([after] if after is not None else []))
    return dict(send=out[0], recv=out[1], held=out[2:2 + n_h], zero=out[-1][0, 0], plan=plan)


def _split_wait(name, started, after):
    held, plan = started['held'], started['plan']
    n_h = len(held)

    def body(*refs):
        for sent in plan(refs[:n_h], refs[n_h], refs[n_h + 1], True):
            sent.wait_send()
        for landed in plan(refs[:n_h], refs[n_h], refs[n_h + 1], False):
            landed.wait_recv()

    return pl.pallas_call(
        body, name=name,
        out_shape=[pltpu.HBM(a.shape, a.dtype) for a in held],
        in_specs=[_HBM] * n_h + [_SEM, _SEM, pl.BlockSpec(memory_space=pl.ANY)],
        out_specs=[_HBM] * n_h,
        input_output_aliases={i: i for i in range(n_h)},
        compiler_params=pltpu.CompilerParams(has_side_effects=_EFFECT),
    )(*held, started['send'], started['recv'], after)


def _push_start(name, arrays, gather):
    n_t = len(arrays)
    lands = [lax.empty((N_DEV,) + a.shape if whole else a.shape, a.dtype) for a, whole in zip(arrays, gather)]
    plan = lambda refs, send, recv, mine: _peer_copies(refs[:n_t], refs[n_t:], send, recv, gather, mine)
    return _split_start(name, list(arrays) + lands, plan, n_t * (N_DEV - 1))


def _push_wait(name, started, after):
    held = _split_wait(name, started, after)
    return held[:len(held) // 2], held[len(held) // 2:]


def _chip_copies(refs, send_sems, recv_sems, mine):
    n_t = len(refs) // 2
    x, y, c = _mesh_pos()
    my_slot = 4 * x + 2 * y + c
    peers = [(x, y, 1 - c), (1 - x, y, c), (x, 1 - y, c), (1 - x, 1 - y, c)]
    copies = []
    for j, peer in enumerate(peers):
        peer_slot = 4 * peer[0] + 2 * peer[1] + peer[2]
        for t in range(n_t):
            copies.append(pltpu.make_async_remote_copy(
                src_ref=refs[t], dst_ref=refs[n_t + t].at[my_slot if mine else peer_slot],
                send_sem=send_sems.at[4 * t + j], recv_sem=recv_sems.at[4 * t + j],
                device_id=peer, device_id_type=pl.DeviceIdType.MESH))
    return copies


def _forward_copies(refs, send_sems, recv_sems, mine):
    x, y, c = _mesh_pos()
    sibling = (x, y, 1 - c)
    copies = []
    for j, (px, py) in enumerate([(1 - x, y), (x, 1 - y), (1 - x, 1 - y)]):
        slot = 4 * px + 2 * py + (c if mine else 1 - c)
        for t, land in enumerate(refs):
            copies.append(pltpu.make_async_remote_copy(
                src_ref=land.at[slot], dst_ref=land.at[slot], send_sem=send_sems.at[3 * t + j], recv_sem=recv_sems.at[3 * t + j],
                device_id=sibling, device_id_type=pl.DeviceIdType.MESH))
    return copies


class _TwoStageGather:
    def __init__(self, name, arrays, my_slot, after=None):
        self.name, self.arrays, self.my_slot = name, arrays, my_slot
        values = list(arrays.values())
        lands = [lax.empty((N_DEV,) + a.shape, a.dtype) for a in values]
        self.stage = _split_start(name + "_start", values + lands, _chip_copies, 4 * len(values), after)
        self.zero = self.stage['zero']

    def forward(self, after):
        held = _split_wait(self.name + "_wait", self.stage, after)
        self.stage = _split_start(self.name + "_forward_start", held[len(self.arrays):], _forward_copies, 3 * len(self.arrays))
        return self.stage['zero']

    def finish(self, after):
        lands = _split_wait(self.name + "_forward_wait", self.stage, after)
        return {n: lax.dynamic_update_slice(land, a[None], (self.my_slot,) + (0,) * a.ndim)
                for (n, a), land in zip(self.arrays.items(), lands)}


ADAM_BLOCK = 128 * 1024


def _adam_update(part, w_ref, m_ref, v_ref, g_ref, d_ref, nm_ref, nv_ref):
    g = part(0)
    for i in range(1, N_DEV):
        g = g + part(i)
    m_new = ADAM_B1 * m_ref[...] + (1.0 - ADAM_B1) * g
    v_new = ADAM_B2 * v_ref[...] + (1.0 - ADAM_B2) * (g * g)
    m_hat = m_new * (1.0 / (1.0 - ADAM_B1 ** ADAM_STEP))
    v_hat = v_new * (1.0 / (1.0 - ADAM_B2 ** ADAM_STEP))
    g_ref[...] = g
    nm_ref[...] = m_new
    nv_ref[...] = v_new
    d_ref[...] = -ADAM_LR * (m_hat / (jnp.sqrt(v_hat) + ADAM_EPS) + ADAM_WD * w_ref[...])


def _adam_many(name, items):
    n_t = len(items)
    n_in = [4 if own is None else 5 for *_, own in items]
    first = [sum(n_in[:t]) for t in range(n_t)]

    def body(*refs):
        ins, outs = refs[:sum(n_in)], refs[sum(n_in):]
        x, y, c = _mesh_pos()
        my_slot = 4 * x + 2 * y + c
        for t in range(n_t):
            s_ref, w_ref, m_ref, v_ref = ins[first[t]:first[t] + 4]
            if n_in[t] == 4:
                part = lambda i: s_ref[i]
            else:
                own_ref = ins[first[t] + 4]
                part = lambda i: jnp.where(my_slot == i, own_ref[...], s_ref[i])
            _adam_update(part, w_ref, m_ref, v_ref, *outs[4 * t:4 * t + 4])

    vmem = pl.BlockSpec(memory_space=pltpu.VMEM)
    out = pl.pallas_call(
        body, name=name, in_specs=[vmem] * sum(n_in), out_specs=[vmem] * (4 * n_t),
        out_shape=[jax.ShapeDtypeStruct(item[1].shape, F32) for item in items for _ in range(4)],
        compiler_params=pltpu.CompilerParams(vmem_limit_bytes=VMEM_LIMIT),
    )(*[a for item in items for a in item if a is not None])
    return [out[4 * t:4 * t + 4] for t in range(n_t)]


def _adam_call(name, slots, w, m, v, own=None):
    n_layer, rows, width = w.shape
    tile = min(rows, ADAM_BLOCK // max(width, 128))

    def body(*refs):
        s_refs = refs[:n_layer]
        own_refs = refs[n_layer:2 * n_layer] if own is not None else None
        w_ref, m_ref, v_ref = refs[-7:-4]
        layer = pl.program_id(0)
        x, y, c = _mesh_pos()
        my_slot = 4 * x + 2 * y + c
        for l in range(n_layer):
            @pl.when(layer == l)
            def _(l=l):
                if own is None:
                    part = lambda i: s_refs[l][i].astype(F32)
                else:
                    part = lambda i: jnp.where(my_slot == i, own_refs[l][...], s_refs[l][i]).astype(F32)
                _adam_update(part, w_ref, m_ref, v_ref, *refs[-4:])

    slot_spec = lambda k: _spec((N_DEV, tile, width), lambda l, i: (0, jnp.where(l == k, i, 0), 0))
    own_spec = lambda k: _spec((tile, width), lambda l, i: (jnp.where(l == k, i, 0), 0))
    blk = _spec((None, tile, width), lambda l, i: (l, i, 0))
    specs = [slot_spec(k) for k in range(n_layer)] + ([own_spec(k) for k in range(n_layer)] if own is not None else [])
    return pl.pallas_call(
        body, name=name, grid=(n_layer, rows // tile),
        in_specs=specs + [blk] * 3, out_specs=[blk] * 4, out_shape=[jax.ShapeDtypeStruct(w.shape, F32)] * 4,
        compiler_params=_cparams(2),
    )(*slots, *(own or []), w, m, v)


def kernel(x, ln_g, ln_b, rw_mu, rw_w0, rw_w1, rw_w2, rw_a0, rw_a1, rw_a2, rw_g1, rw_g2, rw_k_k, rw_k_a, rw_r_k, rw_wr, rw_wk, rw_wv, rw_wo, rw_lnx_g, rw_lnx_b, s5_a_re, s5_a_im, s5_log_dt, s5_b_re, s5_b_im, s5_c_re, s5_c_im, s5_d, s5_w_glu, mlp_w1, mlp_w2, loss_target, m_ln_g, m_ln_b, m_rw_mu, m_rw_w0, m_rw_w1, m_rw_w2, m_rw_a0, m_rw_a1, m_rw_a2, m_rw_g1, m_rw_g2, m_rw_k_k, m_rw_k_a, m_rw_r_k, m_rw_wr, m_rw_wk, m_rw_wv, m_rw_wo, m_rw_lnx_g, m_rw_lnx_b, m_s5_a_re, m_s5_a_im, m_s5_log_dt, m_s5_b_re, m_s5_b_im, m_s5_c_re, m_s5_c_im, m_s5_d, m_s5_w_glu, m_mlp_w1, m_mlp_w2, v_ln_g, v_ln_b, v_rw_mu, v_rw_w0, v_rw_w1, v_rw_w2, v_rw_a0, v_rw_a1, v_rw_a2, v_rw_g1, v_rw_g2, v_rw_k_k, v_rw_k_a, v_rw_r_k, v_rw_wr, v_rw_wk, v_rw_wv, v_rw_wo, v_rw_lnx_g, v_rw_lnx_b, v_s5_a_re, v_s5_a_im, v_s5_log_dt, v_s5_b_re, v_s5_b_im, v_s5_c_re, v_s5_c_im, v_s5_d, v_s5_w_glu, v_mlp_w1, v_mlp_w2):
    given = dict(locals())
    local_w = {n: given[n] for n in WEIGHTS}
    local_m = {n: given["m_" + n] for n in WEIGHTS}
    local_v = {n: given["v_" + n] for n in WEIGHTS}
    small = [n for n in WEIGHTS if n in SHARDED and n not in BIG]

    my_slot = 4 * lax.axis_index("x") + 2 * lax.axis_index("y") + lax.axis_index("c")
    as_bf16 = {n: local_w[n].astype(BF16) for n in BIG}

    gathered = _all_gather("weights_all_gather", [local_w[n] for n in small])
    full = dict(local_w)
    for n, blk in zip(small, gathered):
        full[n] = jnp.moveaxis(blk, 0, SHARDED[n]).reshape(_full_shape(local_w[n].shape, SHARDED[n]))
    mid = _TwoStageGather("weights_mid", {n: as_bf16[n] for n in MID}, my_slot)
    late = lambda after: _TwoStageGather("weights_late", {n: as_bf16[n] for n in LATE}, my_slot, after)
    full['rw_w0'] = full['rw_w0'] + mid.zero

    exchanges = []

    def exchange(entries):
        started = _push_start(f"grads_start_{len(exchanges)}", [t for _, _, t in entries], [n in REPLICATED for n, _, _ in entries])
        exchanges.append((entries, started))
        return started['zero']

    loss_sq, dx, grads = _local_step(x[0], loss_target[0], full, mid, late, exchange)
    loss = (0.5 / D_MODEL) * lax.psum(loss_sq, MESH_AXES)

    wide = {'s5_b_re': (SSM_LANES, SSM_GROUP), 's5_b_im': (SSM_LANES, SSM_GROUP),
            's5_c_re': (D_MODEL, SSM_STATE), 's5_c_im': (D_MODEL, SSM_STATE)}
    by_name, many = {}, []
    landed = {n: {} for n in BIG}
    for e, (entries, started) in enumerate(exchanges):
        sources, lands = _push_wait(f"grads_wait_{e}", started, dx)
        for (n, i, _), src, slots in zip(entries, sources, lands):
            if n in wide:
                as_view = lambda t: t.reshape((1,) + wide[n])
                out = _adam_call(f"adamw_{n}", [slots.reshape((N_DEV,) + wide[n])], as_view(local_w[n]), as_view(local_m[n]),
                                 as_view(local_v[n]), own=[src.reshape(wide[n])])
                by_name[n] = [t.reshape(local_w[n].shape) for t in out]
            elif n in REPLICATED:
                many.append((n, (slots, local_w[n], local_m[n], local_v[n], src)))
            else:
                landed[n][i] = (slots, lax.dynamic_index_in_dim(src, my_slot, 0, keepdims=False))
    for n in BIG:
        layers = [landed[n][i] for i in sorted(landed[n])]
        by_name[n] = _adam_call(f"adamw_{n}", [s for s, _ in layers], local_w[n], local_m[n], local_v[n], own=[o for _, o in layers])
    split = lambda n: jnp.moveaxis(grads[n].reshape(_split_shape(local_w[n].shape, SHARDED[n])), SHARDED[n], 0)
    for n, slots in zip(small, _all_to_all("grads_all_to_all", [split(n) for n in small])):
        many.append((n, (slots, local_w[n], local_m[n], local_v[n], None)))
    last = [n for n in REPLICATED if n not in by_name and n not in dict(many)]
    for n, slots in zip(last, _all_gather("grads_all_gather", [grads[n] for n in last])):
        many.append((n, (slots, local_w[n], local_m[n], local_v[n], None)))
    for (n, _), out in zip(many, _adam_many("adamw_small", [item for _, item in many])):
        by_name[n] = out
    results = [by_name[n][k] for k in range(4) for n in WEIGHTS]
    return (loss, dx[None], *results)
```

```python
import functools
import math

import jax
import jax.numpy as jnp
from jax import lax
from jax.experimental import pallas as pl
from jax.experimental.pallas import tpu as pltpu

F32 = jnp.float32
BF16 = jnp.bfloat16

D_MODEL = 1024
HEAD = 64
PAIR = 2 * HEAD
N_PAIR = D_MODEL // PAIR
CHUNK = 64
GN_EPS = 64e-5
LN_EPS = 1e-5
SSM_GROUP = 16
SSM_STATE = 64
SSM_GROUPS = D_MODEL // SSM_GROUP
SSM_LANES = SSM_GROUPS * SSM_STATE
SSM_BLK_IN = 128
SSM_BLK_ST = 512
N_SSM_BLK = D_MODEL // SSM_BLK_IN
DEPTH = 2
DN_ALPHA = (2.0 * DEPTH) ** 0.25
ADAM_LR, ADAM_B1, ADAM_B2, ADAM_EPS, ADAM_WD, ADAM_STEP = 0.001, 0.9, 0.999, 1e-08, 0.01, 10
N_DEV = 8
MESH_AXES = ("x", "y", "c")
VMEM_LIMIT = 56 * 1024 * 1024
TOKEN_TILE = 256
LN_TILE = 512
WIDE_ROW_TILE = 2048

SHARDED = {
    "rw_mu": 2, "rw_w1": 1, "rw_w2": 2, "rw_a1": 1, "rw_a2": 2, "rw_g1": 1, "rw_g2": 2,
    "rw_wr": 1, "rw_wk": 1, "rw_wv": 1, "rw_wo": 1, "s5_d": 1, "s5_w_glu": 2, "mlp_w1": 2, "mlp_w2": 1,
}
WEIGHTS = ['ln_g', 'ln_b', 'rw_mu', 'rw_w0', 'rw_w1', 'rw_w2', 'rw_a0', 'rw_a1', 'rw_a2', 'rw_g1', 'rw_g2', 'rw_k_k',
           'rw_k_a', 'rw_r_k', 'rw_wr', 'rw_wk', 'rw_wv', 'rw_wo', 'rw_lnx_g', 'rw_lnx_b', 's5_a_re', 's5_a_im',
           's5_log_dt', 's5_b_re', 's5_b_im', 's5_c_re', 's5_c_im', 's5_d', 's5_w_glu', 'mlp_w1', 'mlp_w2']
REPLICATED = [n for n in WEIGHTS if n not in SHARDED]
BIG = ['rw_wr', 'rw_wk', 'rw_wv', 'rw_wo', 's5_w_glu', 'mlp_w1', 'mlp_w2']
MID = ['rw_wr', 'rw_wk', 'rw_wv']
LATE = ['rw_wo', 's5_w_glu', 'mlp_w1', 'mlp_w2']
GRAD_DTYPE = BF16


def _cparams(n_grid):
    return pltpu.CompilerParams(dimension_semantics=("arbitrary",) * n_grid, vmem_limit_bytes=VMEM_LIMIT)


@jax.custom_vjp
def _mm(x, w):
    return jnp.dot(x.astype(BF16), w.astype(BF16), preferred_element_type=F32)


def _mm_fwd(x, w):
    return _mm(x, w), (x, w)


def _mm_bwd(res, dy):
    x, w = res
    dyb = dy.astype(BF16)
    dx = lax.dot_general(dyb, w.astype(BF16), (((1,), (1,)), ((), ())), preferred_element_type=F32)
    dw = lax.dot_general(x.astype(BF16), dyb, (((0,), (0,)), ((), ())), preferred_element_type=F32)
    return dx, dw


_mm.defvjp(_mm_fwd, _mm_bwd)


_DOT_DIMS = {'nn': (((1,), (0,)), ((), ())), 'nt': (((1,), (1,)), ((), ())), 'tn': (((0,), (0,)), ((), ()))}
_BATCH_DOT_DIMS = {'nn': (((2,), (1,)), ((0,), (0,))), 'nt': (((2,), (2,)), ((0,), (0,))), 'tn': (((1,), (1,)), ((0,), (0,)))}
CHUNK_PASSES = 1


def _split_bf16(a):
    hi = a.astype(BF16)
    return hi, (a - hi.astype(F32)).astype(BF16)


def _dot_passes(a, b, mode, passes=None):
    dims = (_DOT_DIMS if a.ndim == 2 else _BATCH_DOT_DIMS)[mode]
    dot = lambda p, q: lax.dot_general(p, q, dims, preferred_element_type=F32)
    if (CHUNK_PASSES if passes is None else passes) == 1:
        return dot(a.astype(BF16), b.astype(BF16))
    (ah, al), (bh, bl) = _split_bf16(a), _split_bf16(b)
    return dot(ah, bh) + (dot(ah, bl) + dot(al, bh))


@functools.partial(jax.custom_vjp, nondiff_argnums=(2,))
def _pdot(a, b, mode):
    return _dot_passes(a, b, mode)


def _pdot_fwd(a, b, mode):
    return _dot_passes(a, b, mode), (a, b)


def _pdot_bwd(mode, res, dy):
    a, b = res
    if mode == 'nn':
        return _dot_passes(dy, b, 'nt'), _dot_passes(a, dy, 'tn')
    if mode == 'nt':
        return _dot_passes(dy, b, 'nn'), _dot_passes(dy, a, 'tn')
    return _dot_passes(b, dy, 'nt'), _dot_passes(a, dy, 'nn')


_pdot.defvjp(_pdot_fwd, _pdot_bwd)


def _tri_sum(x, upper):
    nb, n = x.shape[0], x.shape[1]
    ti = lax.broadcasted_iota(jnp.int32, (nb, n, n), 1)
    tj = lax.broadcasted_iota(jnp.int32, (nb, n, n), 2)
    tri = ((ti <= tj) if upper else (ti >= tj)).astype(BF16)
    hi = x.astype(BF16)
    rest = x - hi.astype(F32)
    mid = rest.astype(BF16)
    lo = (rest - mid.astype(F32)).astype(BF16)
    dot = lambda q: lax.dot_general(tri, q, _BATCH_DOT_DIMS['nn'], preferred_element_type=F32)
    return dot(hi) + (dot(mid) + dot(lo))


@jax.custom_vjp
def _cumsum_rows(x):
    return _tri_sum(x, False)


_cumsum_rows.defvjp(lambda x: (_tri_sum(x, False), None), lambda _, dy: (_tri_sum(dy, True),))


def _power_sum(a, order):
    m = a.shape[-1]
    ti = lax.broadcasted_iota(jnp.int32, (1, m, m), 1)
    tj = lax.broadcasted_iota(jnp.int32, (1, m, m), 2)
    pw, total = _dot_passes(a, a, 'nn'), (ti == tj).astype(F32) + a
    for _ in range(int(math.log2(order)) - 2):
        both = _dot_passes(pw, jnp.concatenate([pw, total], axis=2), 'nn')
        pw, total = both[:, :, :m], total + both[:, :, m:]
    return total + _dot_passes(pw, total, 'nn')


@functools.partial(jax.custom_vjp, nondiff_argnums=(1,))
def _nilpotent_inverse(a, order):
    return _power_sum(a, order)


def _nilpotent_inverse_fwd(a, order):
    inv = _power_sum(a, order)
    return inv, inv


def _nilpotent_inverse_bwd(order, inv, g):
    return (_dot_passes(_dot_passes(inv, g, 'tn'), inv, 'nt'),)


_nilpotent_inverse.defvjp(_nilpotent_inverse_fwd, _nilpotent_inverse_bwd)


@jax.custom_vjp
def _known_inverse(a, inv):
    return inv


_known_inverse.defvjp(lambda a, inv: (inv, inv),
                      lambda inv, g: (_dot_passes(_dot_passes(inv, g, 'tn'), inv, 'nt'), jnp.zeros_like(inv)))


@functools.partial(jax.custom_vjp, nondiff_argnums=(1,))
def _roll_rows(x, shift):
    return pltpu.roll(x, shift, 0)


def _roll_rows_fwd(x, shift):
    return pltpu.roll(x, shift, 0), None


def _roll_rows_bwd(shift, _, dy):
    n = dy.shape[0]
    return (pltpu.roll(dy, (n - shift) % n, 0),)


_roll_rows.defvjp(_roll_rows_fwd, _roll_rows_bwd)


def _shift_down(x, first_row):
    row = lax.broadcasted_iota(jnp.int32, x.shape, 0)
    return jnp.where(row == 0, first_row, _roll_rows(x, 1))


def _sigmoid(x):
    return 1.0 / (1.0 + jnp.exp(-x))


def _softplus(x):
    return jnp.maximum(x, 0.0) + jnp.log(1.0 + jnp.exp(-jnp.abs(x)))


def _gelu(x):
    return 0.5 * x * (1.0 + jnp.tanh(math.sqrt(2.0 / math.pi) * (x + 0.044715 * (x * x * x))))


def _layer_norm(x, g, b):
    mu = jnp.mean(x, axis=-1, keepdims=True)
    xc = x - mu
    var = jnp.mean(xc * xc, axis=-1, keepdims=True)
    return xc * lax.rsqrt(var + LN_EPS) * g + b


def _spec(block, index_map):
    return pl.BlockSpec(block, index_map)


def _tile(arr, tm):
    return (arr, (tm, arr.shape[1]), lambda i: (i, 0))


def _full(arr):
    nd = arr.ndim
    return (arr, arr.shape, lambda *_: (0,) * nd)


def _fwd_call(name, fn, grid, prims, outs):
    n_in = len(prims)

    def body(*refs):
        vals = [r[...] for r in refs[:n_in]]
        vals = [v.astype(F32) if v.dtype != F32 else v for v in vals]
        res = fn(*vals)
        for o, r in zip(refs[n_in:], res):
            o[...] = r.astype(o.dtype)

    return pl.pallas_call(
        body, name=name, grid=grid,
        in_specs=[_spec(b, m) for _, b, m in prims],
        out_specs=[_spec(b, m) for _, _, b, m in outs],
        out_shape=[jax.ShapeDtypeStruct(s, d) for s, d, _, _ in outs],
        compiler_params=_cparams(len(grid)),
    )(*[a for a, _, _ in prims])


def _bwd_call(name, fn, grid, prims, cts, wants, addends=None):
    addends = addends or {}
    n_in, n_ct = len(prims), len(cts)
    out_place = {}
    wants, out_dtype = list(wants), {}
    for i, w in enumerate(wants):
        if isinstance(w, tuple):
            wants[i], out_place[i] = w
        elif w == 'tile_bf16':
            wants[i], out_dtype[i] = 'tile', BF16
    diff = [i for i, w in enumerate(wants) if w]
    add_idx = [i for i in diff if i in addends]
    last_axis = len(grid) - 1

    def body(*refs):
        vals = [r[...] for r in refs[:n_in]]
        vals = [v.astype(F32) if v.dtype != F32 else v for v in vals]
        ct_vals = [r[...] for r in refs[n_in:n_in + n_ct]]
        ct_vals = [v.astype(F32) if v.dtype != F32 else v for v in ct_vals]
        add_refs = dict(zip(add_idx, refs[n_in + n_ct:n_in + n_ct + len(add_idx)]))
        out_refs = refs[n_in + n_ct + len(add_idx):]

        def f(*dargs):
            full = list(vals)
            for i, a in zip(diff, dargs):
                full[i] = a
            return tuple(fn(*full))

        _, vjp = jax.vjp(f, *[vals[i] for i in diff])
        grads = vjp(tuple(ct_vals))
        first = pl.program_id(last_axis) == 0
        for o, g, i in zip(out_refs, grads, diff):
            if wants[i] == 'tile':
                if i in add_refs:
                    g = g + add_refs[i][...]
                o[...] = g.astype(o.dtype)
            else:
                @pl.when(first)
                def _(o=o):
                    o[...] = jnp.zeros(o.shape, o.dtype)
                o[...] += g

    ins = list(prims) + list(cts) + [addends[i] for i in add_idx]
    places = [out_place.get(i, (prims[i][0].shape, prims[i][1], prims[i][2])) for i in diff]
    return pl.pallas_call(
        body, name=name, grid=grid,
        in_specs=[_spec(b, m) for _, b, m in ins],
        out_specs=[_spec(b, m) for _, b, m in places],
        out_shape=[jax.ShapeDtypeStruct(s, out_dtype.get(i, F32)) for i, (s, _, _) in zip(diff, places)],
        compiler_params=_cparams(len(grid)),
    )(*[a for a, _, _ in ins])


def _matmul(name, a, b, mode, *, out_dtype=F32, lhs_fn=None, epi=None, extra=None, tm=1024, tn=1024, tk=1024,
            b_cols=None, b_rows=None, out_cols=False):
    if mode == 'tn':
        kdim, m = a.shape
    else:
        m, kdim = a.shape
    if b_rows is not None:
        loc, cols = b.shape[-2], b.shape[-1]
        if mode == 'nn':
            n, k_shards = cols, max(1, tk // loc)
            tk = k_shards * loc
        else:
            n, tn = N_DEV * loc, loc
    elif b_cols is None:
        n = b.shape[0] if mode == 'nt' else b.shape[1]
    else:
        rows, loc = b.shape[-2], b.shape[-1]
        if mode == 'nn':
            n, n_shards = N_DEV * loc, max(1, tn // loc)
            tn = n_shards * loc
        else:
            n, k_shards = rows, max(1, tk // loc)
            tk = k_shards * loc
    if out_cols:
        o_shards = max(1, min(tn, n) // (n // N_DEV))
        tn = o_shards * (n // N_DEV)
    tm, tn, tk = min(tm, m), min(tn, n), min(tk, kdim)
    nk = kdim // tk
    grid = (m // tm, n // tn, nk)
    a_spec = _spec((tk, tm), lambda i, j, k: (k, i)) if mode == 'tn' else _spec((tm, tk), lambda i, j, k: (i, k))
    if b_rows is not None:
        lead = tuple(b_rows)
        skip = (None,) * (1 + len(lead))
        if mode == 'nn':
            b_spec = _spec((k_shards,) + skip[1:] + (tk // k_shards, tn), lambda i, j, k: (k,) + lead + (0, j))
        else:
            b_spec = _spec(skip + (tn, tk), lambda i, j, k: (j,) + lead + (0, k))
    elif b_cols is None:
        b_spec = _spec((tn, tk), lambda i, j, k: (j, k)) if mode == 'nt' else _spec((tk, tn), lambda i, j, k: (k, j))
    else:
        lead = tuple(b_cols)
        skip = (None,) * (1 + len(lead))
        if mode == 'nn':
            b_spec = _spec((n_shards,) + skip[1:] + (tk, tn // n_shards), lambda i, j, k: (j,) + lead + (k, 0))
        else:
            b_spec = _spec((k_shards,) + skip[1:] + (tn, tk // k_shards), lambda i, j, k: (k,) + lead + (j, 0))
    o_spec = _spec((tm, tn), lambda i, j, k: (i, j))
    if out_cols:
        o_place, o_shape = _spec((o_shards, tm, tn // o_shards), lambda i, j, k: (j, i, 0)), (N_DEV, m, n // N_DEV)
    else:
        o_place, o_shape = o_spec, (m, n)
    dims = _DOT_DIMS[mode]

    def body(*refs):
        if extra is None:
            a_ref, b_ref, o_ref, acc = refs
            x_ref = None
        else:
            a_ref, b_ref, x_ref, o_ref, acc = refs
        k = pl.program_id(2)
        av = a_ref[...]
        if lhs_fn is not None:
            av = lhs_fn(av.astype(F32))
        bv = b_ref[...]
        if bv.ndim == 3:
            bv = bv.reshape(tk, tn) if b_rows is not None else jnp.concatenate([bv[g] for g in range(bv.shape[0])], axis=1)
        part = lax.dot_general(av.astype(BF16), bv.astype(BF16), dims, preferred_element_type=F32)

        def finish(r):
            if epi is not None:
                r = epi(r, x_ref[...])
            if out_cols:
                width = tn // o_shards
                for g in range(o_shards):
                    o_ref[g] = r[:, g * width:(g + 1) * width].astype(o_ref.dtype)
            else:
                o_ref[...] = r.astype(o_ref.dtype)

        if nk == 1:
            finish(part)
            return

        @pl.when(k == 0)
        def _():
            acc[...] = part

        @pl.when((k > 0) & (k < nk - 1))
        def _():
            acc[...] += part

        @pl.when(k == nk - 1)
        def _():
            finish(acc[...] + part)

    ops, specs = [a, b], [a_spec, b_spec]
    if extra is not None:
        ops.append(extra)
        specs.append(o_spec)
    return pl.pallas_call(
        body, name=name, grid=grid, in_specs=specs, out_specs=o_place,
        out_shape=jax.ShapeDtypeStruct(o_shape, out_dtype),
        scratch_shapes=[pltpu.VMEM((tm, tn), F32)],
        compiler_params=_cparams(3),
    )(*ops)


def _relu2(x):
    r = jnp.maximum(x, 0.0)
    return r * r


def _mix_fn(x, halo, mu, w0, w1, w2, a0, a1, a2, g1, g2):
    prev_row = jnp.where(pl.program_id(0) == 0, 0.0, halo[7:8, :])
    xx = _shift_down(x, prev_row) - x
    xr, xw, xk, xv, xa, xg = (x + xx * mu[i:i + 1, :] for i in range(6))
    w_pre = w0 + _mm(jnp.tanh(_mm(xw, w1)), w2)
    log_decay = -jnp.exp(-_softplus(-w_pre) - 0.5)
    a = _sigmoid(a0 + _mm(_mm(xa, a1), a2))
    g = _mm(_sigmoid(_mm(xg, g1)), g2)
    return xr, xk, xv, log_decay, a, g


def _chunk_fn(r, lw, k, v, a, k_k, k_a, r_k, inverse=None, with_inverse=False):
    n, n_pair = r.shape[0], r.shape[1] // PAIR
    to_batch = lambda t: jnp.concatenate([t[None, :, p * PAIR:(p + 1) * PAIR] for p in range(n_pair)], axis=0)
    from_batch = lambda t: jnp.concatenate([t[p] for p in range(n_pair)], axis=1)
    r, lw, k, v, a, k_k, k_a, r_k = (to_batch(t) for t in (r, lw, k, v, a, k_k, k_a, r_k))
    lane = lax.broadcasted_iota(jnp.int32, (1, 1, PAIR), 2)
    m0 = lane < HEAD

    def head_sum(t):
        s0 = jnp.sum(jnp.where(m0, t, 0.0), axis=2, keepdims=True)
        s1 = jnp.sum(jnp.where(m0, 0.0, t), axis=2, keepdims=True)
        return jnp.where(m0, s0, s1)

    kk = k * k_k
    kk = kk / jnp.maximum(jnp.sqrt(head_sum(kk * kk)), 1e-12)
    k2 = k * (1.0 + (a - 1.0) * k_a)
    aa, bb = -kk, kk * a
    bonus = head_sum(r * k2 * r_k) * v

    cum = _cumsum_rows(lw)
    p_in, p_ex, p_inv = jnp.exp(cum), jnp.exp(cum - lw), jnp.exp(-cum)
    at, rt, kt, bt = aa * p_ex, r * p_in, k2 * p_inv, bb * p_inv

    def stack_masked(t):
        return jnp.concatenate([jnp.where(m0, t, 0.0), jnp.where(m0, 0.0, t)], axis=1)

    def unstack_sum(t):
        return t[:, :n] + t[:, n:]

    def unstack_select(t):
        return jnp.where(m0, t[:, :n], t[:, n:])

    ti = lax.broadcasted_iota(jnp.int32, (1, 2 * n, 2 * n), 1)
    tj = lax.broadcasted_iota(jnp.int32, (1, 2 * n, 2 * n), 2)
    same = (ti >= n) == (tj >= n)
    incl, strict = same & (ti >= tj), same & (ti > tj)
    m = 2 * n
    rows = lambda *ts: jnp.concatenate(ts, axis=1)
    cols = lambda *ts: jnp.concatenate(ts, axis=2)
    at_s, rt_s, kt_s, bt_s = stack_masked(at), stack_masked(rt), stack_masked(kt), stack_masked(bt)
    v_s, zeros = rows(v, v), jnp.zeros_like(at_s)
    bk_s = rows(bt_s, kt_s)
    gram = _pdot(rows(at_s, rt_s), bk_s, 'nt')
    a_ab = jnp.where(strict, gram[:, :m, :m], 0.0)
    a_ak = jnp.where(strict, gram[:, :m, m:], 0.0)
    a_rb = jnp.where(incl, gram[:, m:, :m], 0.0)
    a_rk = jnp.where(incl, gram[:, m:, m:], 0.0)
    inv = _nilpotent_inverse(a_ab, n) if inverse is None else _known_inverse(a_ab, inverse)
    wy_s = _pdot(inv, cols(at_s, _pdot(a_ak, v_s, 'nn')), 'nn')
    q_oi = _pdot(cols(a_rb, a_rk), rows(wy_s, cols(zeros, v_s)), 'nn')
    q_all = unstack_sum(rt_s + q_oi[:, :, :PAIR])
    oi_all = unstack_select(q_oi[:, :, PAIR:])
    own_head = (lax.broadcasted_iota(jnp.int32, (1, m, PAIR), 1) >= n) == (lax.broadcasted_iota(jnp.int32, (1, m, PAIR), 2) >= HEAD)
    wy_m = cols(wy_s[:, :, :PAIR], jnp.where(own_head, wy_s[:, :, PAIR:], 0.0))
    mm_nn = _pdot(bk_s, rows(wy_m, cols(zeros, stack_masked(v))), 'tn')
    mm, nn = mm_nn[:, :, :PAIR], mm_nn[:, :, PAIR:]
    ei = lax.broadcasted_iota(jnp.int32, (1, PAIR, PAIR), 1)
    ej = lax.broadcasted_iota(jnp.int32, (1, PAIR, PAIR), 2)
    eye_p = (ei == ej).astype(F32)
    decay_col = jnp.sum(eye_p * p_in[:, n - 1:n, :], axis=2, keepdims=True)
    maps = ((decay_col * (eye_p + mm))[None], (decay_col * nn)[None], from_batch(q_all), from_batch(oi_all),
            from_batch(bonus))
    return maps + (inv,) if with_inverse else maps


def _gate_fn(o, bonus, g, lnx_g, lnx_b):
    lane = lax.broadcasted_iota(jnp.int32, (1, PAIR), 1)
    m0 = lane < HEAD

    def head_mean(t):
        s0 = jnp.sum(jnp.where(m0, t, 0.0), axis=1, keepdims=True)
        s1 = jnp.sum(jnp.where(m0, 0.0, t), axis=1, keepdims=True)
        return jnp.where(m0, s0, s1) * (1.0 / HEAD)

    outs = []
    for p in range(N_PAIR):
        sl = slice(p * PAIR, (p + 1) * PAIR)
        oc = o[:, sl] - head_mean(o[:, sl])
        on = oc * lax.rsqrt(head_mean(oc * oc) + GN_EPS) * lnx_g[:, sl] + lnx_b[:, sl]
        outs.append((on + bonus[:, sl]) * g[:, sl])
    return (jnp.concatenate(outs, axis=1),)


def _res_ln_fn(h, y, g, b):
    return (_layer_norm(DN_ALPHA * h + y, g, b),)


def _glu_ln_fn(h, z, g, b):
    mix = z[:, :D_MODEL] * _sigmoid(z[:, D_MODEL:])
    return (_layer_norm(DN_ALPHA * h + mix, g, b),)


def _s5_param_fn(a_re, a_im, log_dt, b_re, b_im):
    dt = jnp.exp(log_dt)
    lam_re = jnp.minimum(a_re, -1e-4)
    lam_im = a_im
    mag = jnp.exp(dt * lam_re)
    abar_re = mag * jnp.cos(dt * lam_im)
    abar_im = mag * jnp.sin(dt * lam_im)
    den = lam_re * lam_re + lam_im * lam_im
    nr, ni = abar_re - 1.0, abar_im
    coef_re = (nr * lam_re + ni * lam_im) / den
    coef_im = (ni * lam_re - nr * lam_im) / den
    return abar_re, abar_im, coef_re, coef_im


def _s5_bbar_fn(coef_re, coef_im, b_re, b_im):
    return coef_re * b_re - coef_im * b_im, coef_re * b_im + coef_im * b_re


def _s5_in_fn(u, b_re, b_im):
    return _pdot(u, b_re[0], 'nn'), _pdot(u, b_im[0], 'nn')


def _s5_out_fn(s_re, s_im, u, c_re, c_im, d):
    y = _pdot(s_re, c_re[0], 'nn') - _pdot(s_im, c_im[0], 'nn') + u * d
    return (_gelu(y),)


def _pairs_to_batch(t):
    return jnp.concatenate([t[None, :, p * PAIR:(p + 1) * PAIR] for p in range(N_PAIR)], axis=0)


def _batch_to_pairs(t):
    return jnp.concatenate([t[p] for p in range(N_PAIR)], axis=1)


STATE_PASSES = 1


def _rwkv_scan_fwd(r, lw, k, v, a, k_k, k_a, r_k):
    n_tok = r.shape[0]
    n_chunk = n_tok // CHUNK

    def body(r_ref, lw_ref, k_ref, v_ref, a_ref, kk_ref, ka_ref, rk_ref, o_ref, bonus_ref, zs_ref, inv_ref, m_ref, q_ref, z):
        @pl.when(pl.program_id(0) == 0)
        def _():
            z[...] = jnp.zeros(z.shape, F32)

        m_all, n_all, q, oi, bonus, inv = _chunk_fn(r_ref[...], lw_ref[...], k_ref[...], v_ref[...], a_ref[...],
                                                    kk_ref[...], ka_ref[...], rk_ref[...], with_inverse=True)
        zb = z[...]
        zs_ref[0] = zb
        inv_ref[0] = inv.astype(inv_ref.dtype)
        m_ref[...] = m_all
        q_ref[...] = q
        bonus_ref[...] = bonus
        o_ref[...] = _batch_to_pairs(_dot_passes(_pairs_to_batch(q), zb, 'nn', STATE_PASSES)) + oi
        z[...] = _dot_passes(m_all[0], zb, 'nn', STATE_PASSES) + n_all[0]

    tok = _spec((CHUNK, D_MODEL), lambda c: (c, 0))
    par = _spec((1, D_MODEL), lambda c: (0, 0))
    st = _spec((1, N_PAIR, PAIR, PAIR), lambda c: (c, 0, 0, 0))
    return pl.pallas_call(
        body, name="rwkv_scan_fwd", grid=(n_chunk,),
        in_specs=[tok] * 5 + [par] * 3, out_specs=[tok, tok, st, st, st, tok],
        out_shape=[jax.ShapeDtypeStruct((n_tok, D_MODEL), F32)] * 2
        + [jax.ShapeDtypeStruct((n_chunk, N_PAIR, PAIR, PAIR), F32), jax.ShapeDtypeStruct((n_chunk, N_PAIR, PAIR, PAIR), BF16),
           jax.ShapeDtypeStruct((n_chunk, N_PAIR, PAIR, PAIR), F32), jax.ShapeDtypeStruct((n_tok, D_MODEL), F32)],
        scratch_shapes=[pltpu.VMEM((N_PAIR, PAIR, PAIR), F32)],
        compiler_params=_cparams(1),
    )(r, lw, k, v, a, k_k, k_a, r_k)


def _rwkv_scan_bwd(r, lw, k, v, a, k_k, k_a, r_k, zs, invs, ms, q, d_o, d_bonus):
    n_tok = r.shape[0]
    n_chunk = n_tok // CHUNK

    def body(r_ref, lw_ref, k_ref, v_ref, a_ref, kk_ref, ka_ref, rk_ref, zs_ref, inv_ref, m_ref, q_ref, do_ref, db_ref,
             dr_ref, dlw_ref, dk_ref, dv_ref, da_ref, dkk_ref, dka_ref, drk_ref, g):
        sums = (dkk_ref, dka_ref, drk_ref)

        @pl.when(pl.program_id(0) == 0)
        def _():
            g[...] = jnp.zeros(g.shape, F32)
            for s_ref in sums:
                s_ref[...] = jnp.zeros(s_ref.shape, F32)

        prim = (r_ref[...], lw_ref[...], k_ref[...], v_ref[...], a_ref[...], kk_ref[...], ka_ref[...], rk_ref[...])
        known = inv_ref[0].astype(F32)
        _, chunk_vjp = jax.vjp(lambda *p: _chunk_fn(*p, inverse=known), *prim)
        m_all, q = m_ref[...], q_ref[...]
        gb, zb, d_out = g[...], zs_ref[0], do_ref[...]
        dob = _pairs_to_batch(d_out)
        d_m = _dot_passes(gb, zb, 'nt', STATE_PASSES)
        d_q = _batch_to_pairs(_dot_passes(dob, zb, 'nt', STATE_PASSES))
        grads = chunk_vjp((d_m[None], gb[None], d_q, d_out, db_ref[...]))
        for out_ref, val in zip((dr_ref, dlw_ref, dk_ref, dv_ref, da_ref), grads[:5]):
            out_ref[...] = val.astype(out_ref.dtype)
        for s_ref, val in zip(sums, grads[5:]):
            s_ref[...] += val
        same_head = (lax.broadcasted_iota(jnp.int32, (1, PAIR, PAIR), 1) // HEAD
                     == lax.broadcasted_iota(jnp.int32, (1, PAIR, PAIR), 2) // HEAD)
        g[...] = (_dot_passes(m_all[0], gb, 'tn', STATE_PASSES)
                  + jnp.where(same_head, _dot_passes(_pairs_to_batch(q), dob, 'tn', STATE_PASSES), 0.0))

    tok = _spec((CHUNK, D_MODEL), lambda c: (n_chunk - 1 - c, 0))
    par = _spec((1, D_MODEL), lambda c: (0, 0))
    st = _spec((1, N_PAIR, PAIR, PAIR), lambda c: (n_chunk - 1 - c, 0, 0, 0))
    tok_shape = lambda dt: jax.ShapeDtypeStruct((n_tok, D_MODEL), dt)
    return pl.pallas_call(
        body, name="rwkv_scan_bwd", grid=(n_chunk,),
        in_specs=[tok] * 5 + [par] * 3 + [st, st, st, tok, tok, tok], out_specs=[tok] * 5 + [par] * 3,
        out_shape=[tok_shape(BF16), tok_shape(F32), tok_shape(BF16), tok_shape(BF16), tok_shape(F32)]
        + [jax.ShapeDtypeStruct((1, D_MODEL), F32)] * 3,
        scratch_shapes=[pltpu.VMEM((N_PAIR, PAIR, PAIR), F32)],
        compiler_params=_cparams(1),
    )(r, lw, k, v, a, k_k, k_a, r_k, zs, invs, ms, q, d_o, d_bonus)


S5_TIME_TILE = 1024
S5_GROUP = 8


def _scan_rows(re_ref, im_ref, a1, carry, reverse):
    lb = re_ref.shape[1]
    grp, n_grp = S5_GROUP, re_ref.shape[0] // S5_GROUP

    def cmul(xr, xi, yr, yi):
        return xr * yr - xi * yi, xr * yi + xi * yr

    a2 = cmul(*a1, *a1)
    a4 = cmul(*a2, *a2)
    a8 = cmul(*a4, *a4)
    row = lax.broadcasted_iota(jnp.int32, (grp, lb), 0)
    expo = (grp - row) if reverse else (row + 1)
    pw = (jnp.ones((grp, lb), F32), jnp.zeros((grp, lb), F32))
    for bit, ap in ((1, a1), (2, a2), (4, a4), (8, a8)):
        nxt = cmul(*pw, *ap)
        sel = (expo & bit) != 0
        pw = (jnp.where(sel, nxt[0], pw[0]), jnp.where(sel, nxt[1], pw[1]))

    steps = []
    for d, ad in ((1, a1), (2, a2), (4, a4)):
        keep = (row < grp - d) if reverse else (row >= d)
        steps.append(((grp - d) if reverse else d, jnp.where(keep, ad[0], 0.0), jnp.where(keep, ad[1], 0.0)))

    def group(i, c):
        gi = (n_grp - 1 - i) if reverse else i
        rows = pl.ds(pl.multiple_of(gi * grp, grp), grp)
        xr, xi = re_ref[rows, :], im_ref[rows, :]
        for shift, mr, mi in steps:
            pr, pi = cmul(mr, mi, pltpu.roll(xr, shift, 0), pltpu.roll(xi, shift, 0))
            xr, xi = xr + pr, xi + pi
        cr, ci = cmul(*pw, c[0], c[1])
        xr, xi = xr + cr, xi + ci
        re_ref[rows, :] = xr
        im_ref[rows, :] = xi
        edge = slice(0, 1) if reverse else slice(grp - 1, grp)
        return xr[edge, :], xi[edge, :]

    return lax.fori_loop(0, n_grp, group, carry)


def _s5_fwd(u, b_re, b_im, a_re, a_im, c_re, c_im, d_row):
    n_tok = u.shape[0]
    tt = min(S5_TIME_TILE, n_tok)

    def body(u_ref, bre_ref, bim_ref, are_ref, aim_ref, cre_ref, cim_ref, d_ref, sre_ref, sim_ref, yg_ref, carry):
        @pl.when(pl.program_id(1) == 0)
        def _():
            carry[...] = jnp.zeros(carry.shape, F32)

        uv = u_ref[...]
        sre_ref[...], sim_ref[...] = _s5_in_fn(uv, bre_ref[...], bim_ref[...])
        c = _scan_rows(sre_ref, sim_ref, (are_ref[...], aim_ref[...]), (carry[0:1, :], carry[1:2, :]), False)
        carry[0:1, :] = c[0]
        carry[1:2, :] = c[1]
        (yg,) = _s5_out_fn(sre_ref[...], sim_ref[...], uv, cre_ref[...], cim_ref[...], d_ref[...])
        yg_ref[...] = yg.astype(yg_ref.dtype)

    u_blk = _spec((tt, SSM_BLK_IN), lambda l, t: (t, l))
    s_blk = _spec((tt, SSM_BLK_ST), lambda l, t: (t, l))
    blk3 = lambda arr: _spec((1,) + arr.shape[1:], lambda l, t: (l, 0, 0))
    par = lambda width: _spec((1, width), lambda l, t: (0, l))
    return pl.pallas_call(
        body, name="s5_fwd", grid=(N_SSM_BLK, n_tok // tt),
        in_specs=[u_blk, blk3(b_re), blk3(b_im), par(SSM_BLK_ST), par(SSM_BLK_ST), blk3(c_re), blk3(c_im), par(SSM_BLK_IN)],
        out_specs=[s_blk, s_blk, u_blk],
        out_shape=[jax.ShapeDtypeStruct((n_tok, SSM_LANES), F32)] * 2 + [jax.ShapeDtypeStruct((n_tok, D_MODEL), BF16)],
        scratch_shapes=[pltpu.VMEM((2, SSM_BLK_ST), F32)],
        compiler_params=_cparams(2),
    )(u, b_re, b_im, a_re, a_im, c_re, c_im, d_row)


def _s5_bwd(dyg, u, s_re, s_im, dh_res, b_re, b_im, a_re, a_im, c_re, c_im, d_row):
    n_tok = u.shape[0]
    tt = min(S5_TIME_TILE, n_tok)
    n_t = n_tok // tt

    def body(dyg_ref, u_ref, sre_ref, sim_ref, hre_ref, him_ref, res_ref, bre_ref, bim_ref, are_ref, aim_ref,
             cre_ref, cim_ref, d_ref, du_ref, dbre_ref, dbim_ref, dcre_ref, dcim_ref, dd_ref, dare_ref, daim_ref,
             carry, gre, gim):
        i = pl.program_id(1)
        sums = (dbre_ref, dbim_ref, dcre_ref, dcim_ref, dd_ref, dare_ref, daim_ref)

        @pl.when(i == 0)
        def _():
            carry[...] = jnp.zeros(carry.shape, F32)
            for r in sums:
                r[...] = jnp.zeros(r.shape, F32)

        uv, sre, sim = u_ref[...], sre_ref[...], sim_ref[...]
        _, out_vjp = jax.vjp(_s5_out_fn, sre, sim, uv, cre_ref[...], cim_ref[...], d_ref[...])
        gre[...], gim[...], du_out, dcre, dcim, dd = out_vjp((dyg_ref[...],))
        c = _scan_rows(gre, gim, (are_ref[...], -aim_ref[...]), (carry[0:1, :], carry[1:2, :]), True)
        carry[0:1, :] = c[0]
        carry[1:2, :] = c[1]
        g_re, g_im = gre[...], gim[...]
        at_start = i == n_t - 1
        p_re = _shift_down(sre, jnp.where(at_start, 0.0, hre_ref[7:8, :]))
        p_im = _shift_down(sim, jnp.where(at_start, 0.0, him_ref[7:8, :]))
        _, in_vjp = jax.vjp(_s5_in_fn, uv, bre_ref[...], bim_ref[...])
        du_in, dbre, dbim = in_vjp((g_re, g_im))
        du_ref[...] = du_out + du_in + res_ref[...]
        dare = jnp.sum(g_re * p_re + g_im * p_im, axis=0, keepdims=True)
        daim = jnp.sum(g_im * p_re - g_re * p_im, axis=0, keepdims=True)
        for r, val in zip(sums, (dbre, dbim, dcre, dcim, dd, dare, daim)):
            r[...] += val

    u_blk = _spec((tt, SSM_BLK_IN), lambda l, t: (n_t - 1 - t, l))
    s_blk = _spec((tt, SSM_BLK_ST), lambda l, t: (n_t - 1 - t, l))
    halo = _spec((8, SSM_BLK_ST), lambda l, t: (jnp.maximum((n_t - 1 - t) * (tt // 8) - 1, 0), l))
    blk3 = lambda arr: _spec((1,) + arr.shape[1:], lambda l, t: (l, 0, 0))
    par = lambda width: _spec((1, width), lambda l, t: (0, l))
    params = [blk3(b_re), blk3(b_im), par(SSM_BLK_ST), par(SSM_BLK_ST), blk3(c_re), blk3(c_im), par(SSM_BLK_IN)]
    shape = lambda arr: jax.ShapeDtypeStruct(arr.shape, F32)
    return pl.pallas_call(
        body, name="s5_bwd", grid=(N_SSM_BLK, n_t),
        in_specs=[u_blk, u_blk, s_blk, s_blk, halo, halo, u_blk] + params,
        out_specs=[u_blk, blk3(b_re), blk3(b_im), blk3(c_re), blk3(c_im), par(SSM_BLK_IN), par(SSM_BLK_ST), par(SSM_BLK_ST)],
        out_shape=[shape(u), shape(b_re), shape(b_im), shape(c_re), shape(c_im), shape(d_row), shape(a_re), shape(a_im)],
        scratch_shapes=[pltpu.VMEM((2, SSM_BLK_ST), F32), pltpu.VMEM((tt, SSM_BLK_ST), F32), pltpu.VMEM((tt, SSM_BLK_ST), F32)],
        compiler_params=_cparams(2),
    )(dyg, u, s_re, s_im, s_re, s_im, dh_res, b_re, b_im, a_re, a_im, c_re, c_im, d_row)


def _loss_call(h, target, tm):
    n_tok, d = h.shape

    def body(h_ref, t_ref, acc_ref, dh_ref):
        @pl.when(pl.program_id(0) == 0)
        def _():
            acc_ref[...] = jnp.zeros(acc_ref.shape, F32)

        e = h_ref[...] - t_ref[...]
        dh_ref[...] = e * (1.0 / d)
        acc_ref[...] += jnp.sum(jnp.sum(e * e, axis=1, keepdims=True), axis=0, keepdims=True)

    tok = _spec((tm, d), lambda i: (i, 0))
    return pl.pallas_call(
        body, name="loss", grid=(n_tok // tm,),
        in_specs=[tok, tok], out_specs=[_spec((8, 128), lambda i: (0, 0)), tok],
        out_shape=[jax.ShapeDtypeStruct((8, 128), F32), jax.ShapeDtypeStruct(h.shape, F32)],
        compiler_params=_cparams(1),
    )(h, target)


def _block_diag(t):
    nb, ng, rows, cols = t.shape
    eye = jnp.eye(ng, dtype=t.dtype)
    return jnp.einsum('bgrc,gh->bgrhc', t, eye).reshape(nb, ng * rows, ng * cols)


def _block_diag_t(t, rows, cols):
    nb, ng = t.shape[0], t.shape[1] // rows
    t5 = t.reshape(nb, ng, rows, ng, cols)
    return jnp.einsum('bgrhc,gh->bgrc', t5, jnp.eye(ng, dtype=t.dtype))


def _mlp_fwd(layer, h, w1g, w2g):
    pre = _matmul(f"mlp{layer}_up", h, w1g, 'nn', b_cols=(layer,), out_dtype=BF16, tm=WIDE_ROW_TILE)
    return pre, _matmul(f"mlp{layer}_down", pre, w2g, 'nn', lhs_fn=_relu2, b_rows=(layer,))


def _mlp_bwd(layer, h, pre, dy, dh_res, w1g, w2g):
    dpre = _matmul(f"mlp{layer}_down_dx", dy, w2g, 'nt', epi=lambda acc, p: acc * (2.0 * jnp.maximum(p, 0.0)),
                   extra=pre, b_rows=(layer,), out_dtype=BF16, tm=WIDE_ROW_TILE)
    dw2 = _matmul(f"mlp{layer}_down_dw", pre, dy, 'tn', lhs_fn=_relu2, out_dtype=GRAD_DTYPE, tm=WIDE_ROW_TILE)
    dw1 = _matmul(f"mlp{layer}_up_dw", h, dpre, 'tn', out_cols=True, out_dtype=GRAD_DTYPE)
    dh = _matmul(f"mlp{layer}_up_dx", dpre, w1g, 'nt', epi=lambda acc, e: acc + e, extra=dh_res, b_cols=(layer,))
    return dh, dw1, dw2


def _local_step(x, target, w, mid, start_late, exchange):
    n_tok = x.shape[0]
    tm = min(TOKEN_TILE, n_tok)
    n_tile, n_chunk = n_tok // tm, n_tok // CHUNK
    row = lambda v: v.reshape(1, -1)
    big = {}
    ln_g, ln_b = w['ln_g'], w['ln_b']
    grads = {}
    to_slots = lambda t: t.reshape(N_DEV, t.shape[0] // N_DEV, t.shape[1])

    halo_spec = (8, D_MODEL), lambda i: (jnp.maximum(i * (tm // 8) - 1, 0), 0)
    mix_p = [_tile(x, tm), (x,) + halo_spec, _full(w['rw_mu'][0]), _full(row(w['rw_w0'])), _full(w['rw_w1'][0]),
             _full(w['rw_w2'][0]), _full(row(w['rw_a0'])), _full(w['rw_a1'][0]), _full(w['rw_a2'][0]),
             _full(w['rw_g1'][0]), _full(w['rw_g2'][0])]
    tok_out = lambda dt: ((n_tok, D_MODEL), dt, (tm, D_MODEL), lambda i: (i, 0))
    xr, xk, xv, lw, a, g = _fwd_call("rwkv_mix", _mix_fn, (n_tile,), mix_p,
                                     [tok_out(BF16), tok_out(BF16), tok_out(BF16), tok_out(F32), tok_out(F32), tok_out(F32)])
    mid.forward(xr)
    square = lambda t: t.reshape(D_MODEL, D_MODEL)
    big.update({n: square(t) for n, t in mid.finish(xr).items()})
    late = start_late(big['rw_wr'])
    r = _matmul("rwkv_r", xr, big['rw_wr'], 'nn')
    k = _matmul("rwkv_k", xk, big['rw_wk'], 'nn')
    v = _matmul("rwkv_v", xv, big['rw_wv'], 'nn')

    scan_in = (r, lw, k, v, a, row(w['rw_k_k']) + late.zero, row(w['rw_k_a']), w['rw_r_k'].reshape(1, -1))
    o, bonus, *scan_saved = _rwkv_scan_fwd(*scan_in)

    zero = late.forward(o)
    tl = min(LN_TILE, n_tok)
    n_ln = n_tok // tl
    ln_out = lambda dt: ((n_tok, D_MODEL), dt, (tl, D_MODEL), lambda i: (i, 0))
    gate_p = [_tile(o, tl), _tile(bonus, tl), _tile(g, tl), _full(row(w['rw_lnx_g']) + zero), _full(row(w['rw_lnx_b']))]
    (og,) = _fwd_call("rwkv_gate", _gate_fn, (n_ln,), gate_p, [ln_out(BF16)])
    late_w = late.finish(og)
    big['rw_wo'] = square(late_w['rw_wo'])
    w1g, w2g, glu_g = late_w['mlp_w1'], late_w['mlp_w2'], late_w['s5_w_glu']
    y0 = _matmul("rwkv_o", og, big['rw_wo'], 'nn')
    ln0_p = [_tile(x, tl), _tile(y0, tl), _full(ln_g[0:1]), _full(ln_b[0:1])]
    twice = lambda fn: lambda *args: fn(*args) * 2
    h1, h1_b = _fwd_call("ln0", twice(_res_ln_fn), (n_ln,), ln0_p, [ln_out(F32), ln_out(BF16)])
    pre0, y1 = _mlp_fwd(0, h1_b, w1g, w2g)
    ln1_p = [_tile(h1, tl), _tile(y1, tl), _full(ln_g[1:2]), _full(ln_b[1:2])]
    (h2,) = _fwd_call("ln1", _res_ln_fn, (n_ln,), ln1_p, [ln_out(F32)])

    gp = (SSM_GROUPS, SSM_STATE)
    par_p = [_full(w['s5_a_re'][0]), _full(w['s5_a_im'][0]), _full(w['s5_log_dt'].reshape(SSM_GROUPS, 1))]
    s5_par_fn = lambda ar, ai, ld: _s5_param_fn(ar, ai, ld, None, None)
    gp_out = (gp, F32, gp, lambda i: (0, 0))
    abar_re, abar_im, coef_re, coef_im = _fwd_call("s5_param", s5_par_fn, (1,), par_p, [gp_out] * 4)
    b_flat = [w[n][0].reshape(SSM_LANES, SSM_GROUP) for n in ('s5_b_re', 's5_b_im')]
    bbar_p = [_full(coef_re.reshape(SSM_LANES, 1)), _full(coef_im.reshape(SSM_LANES, 1)), _full(b_flat[0]), _full(b_flat[1])]
    bb_out = ((SSM_LANES, SSM_GROUP), F32, (SSM_LANES, SSM_GROUP), lambda i: (0, 0))
    bbar_re, bbar_im = _fwd_call("s5_bbar", _s5_bbar_fn, (1,), bbar_p, [bb_out] * 2)
    to_in = lambda t: _block_diag(t.reshape(N_SSM_BLK, 8, SSM_STATE, SSM_GROUP).transpose(0, 1, 3, 2))
    to_out = lambda t: _block_diag(t.reshape(N_SSM_BLK, 8, SSM_GROUP, SSM_STATE).transpose(0, 1, 3, 2))
    bblk_re, bblk_im = to_in(bbar_re), to_in(bbar_im)
    cblk_re, cblk_im = to_out(w['s5_c_re'][0]), to_out(w['s5_c_im'][0])
    a_row_re, a_row_im = abar_re.reshape(1, SSM_LANES), abar_im.reshape(1, SSM_LANES)

    d_row = row(w['s5_d'])
    s5_params = (bblk_re, bblk_im, a_row_re, a_row_im, cblk_re, cblk_im, d_row)
    s_re, s_im, yg = _s5_fwd(h2, *s5_params)
    z = _matmul("s5_glu", yg, glu_g, 'nn', b_cols=(0,))
    ln2_p = [_tile(h2, tl), _tile(z, tl), _full(ln_g[2:3]), _full(ln_b[2:3])]
    h3, h3_b = _fwd_call("ln2", twice(_glu_ln_fn), (n_ln,), ln2_p, [ln_out(F32), ln_out(BF16)])
    pre1, y3 = _mlp_fwd(1, h3_b, w1g, w2g)
    ln3_p = [_tile(h3, tl), _tile(y3, tl), _full(ln_g[3:4]), _full(ln_b[3:4])]
    (h4,) = _fwd_call("ln3", _res_ln_fn, (n_ln,), ln3_p, [ln_out(F32)])

    loss_acc, dh4 = _loss_call(h4, target, tl)

    ln_w = ['tile', 'tile_bf16', 'acc', 'acc']
    dh3_res, dy3, dg3, db3 = _bwd_call("ln3_bwd", _res_ln_fn, (n_ln,), ln3_p, [_tile(dh4, tl)], ln_w)
    dh3, dw1_1, dw2_1 = _mlp_bwd(1, h3_b, pre1, dy3, dh3_res, w1g, w2g)
    zero = exchange([('mlp_w1', 1, dw1_1), ('mlp_w2', 1, to_slots(dw2_1))])
    ln2_p[2] = _full(ln_g[2:3] + zero)
    dh2_res, dz, dg2, db2 = _bwd_call("ln2_bwd", _glu_ln_fn, (n_ln,), ln2_p, [_tile(dh3, tl)], ln_w)
    dyg = _matmul("s5_glu_dx", dz, glu_g, 'nt', b_cols=(0,))
    dw_glu = _matmul("s5_glu_dw", yg, dz, 'tn', out_cols=True, out_dtype=GRAD_DTYPE)
    dh2, dbb_re, dbb_im, dcb_re, dcb_im, dd, dabar_re, dabar_im = _s5_bwd(dyg, h2, s_re, s_im, dh2_res, *s5_params)
    from_in = lambda t: _block_diag_t(t, SSM_GROUP, SSM_STATE).transpose(0, 1, 3, 2).reshape(SSM_LANES, SSM_GROUP)
    from_out = lambda t: _block_diag_t(t, SSM_STATE, SSM_GROUP).transpose(0, 1, 3, 2).reshape(1, SSM_GROUPS, SSM_GROUP, SSM_STATE)
    grads['s5_c_re'], grads['s5_c_im'] = from_out(dcb_re), from_out(dcb_im)
    grads['s5_d'] = dd
    dcoef_re, dcoef_im, db_re, db_im = _bwd_call(
        "s5_bbar_bwd", _s5_bbar_fn, (1,), bbar_p, [_full(from_in(dbb_re)), _full(from_in(dbb_im))], ['acc'] * 4)
    grads['s5_b_re'] = db_re.reshape(w['s5_b_re'].shape)
    grads['s5_b_im'] = db_im.reshape(w['s5_b_im'].shape)
    par_ct = [_full(dabar_re.reshape(gp)), _full(dabar_im.reshape(gp)), _full(dcoef_re.reshape(gp)), _full(dcoef_im.reshape(gp))]
    da_re, da_im, dlog_dt = _bwd_call("s5_param_bwd", s5_par_fn, (1,), par_p, par_ct, ['acc'] * 3)
    grads['s5_a_re'], grads['s5_a_im'], grads['s5_log_dt'] = da_re[None], da_im[None], dlog_dt.reshape(1, SSM_GROUPS)
    zero = exchange([(n, 0, grads[n]) for n in ('s5_a_re', 's5_a_im', 's5_log_dt', 's5_b_re', 's5_b_im', 's5_c_re', 's5_c_im')])
    ln1_p[2] = _full(ln_g[1:2] + zero)

    dh1_res, dy1, dg1, db1 = _bwd_call("ln1_bwd", _res_ln_fn, (n_ln,), ln1_p, [_tile(dh2, tl)], ln_w)
    dh1, dw1_0, dw2_0 = _mlp_bwd(0, h1_b, pre0, dy1, dh1_res, w1g, w2g)
    dx_res, dy0, dg0, db0 = _bwd_call("ln0_bwd", _res_ln_fn, (n_ln,), ln0_p, [_tile(dh1, tl)], ln_w)
    grads['ln_g'] = jnp.concatenate([dg0, dg1, dg2, dg3])
    grads['ln_b'] = jnp.concatenate([db0, db1, db2, db3])
    dog = _matmul("rwkv_o_dx", dy0, big['rw_wo'], 'nt')
    dw_o = to_slots(_matmul("rwkv_o_dw", og, dy0, 'tn', out_dtype=GRAD_DTYPE))
    zero = exchange([('s5_w_glu', 0, dw_glu), ('mlp_w1', 0, dw1_0), ('mlp_w2', 0, to_slots(dw2_0)), ('rw_wo', 0, dw_o)])
    gate_p[3] = _full(row(w['rw_lnx_g']) + zero)
    d_o, dbonus, dgate, grads['rw_lnx_g'], grads['rw_lnx_b'] = _bwd_call(
        "rwkv_gate_bwd", _gate_fn, (n_ln,), gate_p, [_tile(dog, tl)], ['tile', 'tile', 'tile', 'acc', 'acc'])
    dr, dlw, dk, dv, da, grads['rw_k_k'], grads['rw_k_a'], dr_k = _rwkv_scan_bwd(*scan_in, *scan_saved, d_o, dbonus)
    grads['rw_r_k'] = dr_k.reshape(w['rw_r_k'].shape)
    dw_r = to_slots(_matmul("rwkv_r_dw", xr, dr, 'tn', out_dtype=GRAD_DTYPE))
    dw_k = to_slots(_matmul("rwkv_k_dw", xk, dk, 'tn', out_dtype=GRAD_DTYPE))
    dw_v = to_slots(_matmul("rwkv_v_dw", xv, dv, 'tn', out_dtype=GRAD_DTYPE))
    done = ('ln_g', 'ln_b', 'rw_lnx_g', 'rw_lnx_b', 'rw_k_k', 'rw_k_a', 'rw_r_k')
    zero = exchange([('rw_wr', 0, dw_r), ('rw_wk', 0, dw_k), ('rw_wv', 0, dw_v)] + [(n, 0, grads[n]) for n in done])
    mix_p[3] = _full(row(w['rw_w0']) + zero)
    dxr = _matmul("rwkv_r_dx", dr, big['rw_wr'], 'nt')
    dxk = _matmul("rwkv_k_dx", dk, big['rw_wk'], 'nt')
    dxv = _matmul("rwkv_v_dx", dv, big['rw_wv'], 'nt')
    mix_ct = [_tile(t, tm) for t in (dxr, dxk, dxv, dlw, da, dgate)]
    halo_grad = ('tile', ((n_tile * 8, D_MODEL), (8, D_MODEL), lambda i: (i, 0)))
    res = _bwd_call("rwkv_mix_bwd", _mix_fn, (n_tile,), mix_p, mix_ct, ['tile', halo_grad] + ['acc'] * 9,
                    addends={0: _tile(dx_res, tm)})
    corr = res[1].reshape(n_tile, 8, D_MODEL)[1:, 7:8, :]
    corr = jnp.pad(corr, ((0, 1), (tm - 1, 0), (0, 0)))
    dx = (res[0].reshape(n_tile, tm, D_MODEL) + corr).reshape(n_tok, D_MODEL)
    (grads['rw_mu'], grads['rw_w0'], grads['rw_w1'], grads['rw_w2'], grads['rw_a0'], grads['rw_a1'], grads['rw_a2'],
     grads['rw_g1'], grads['rw_g2']) = [t[None] if t.shape[0] != 1 else t for t in res[2:]]
    return loss_acc[0, 0], dx, grads


def _full_shape(local_shape, axis):
    return local_shape[:axis] + (N_DEV * local_shape[axis],) + local_shape[axis + 1:]


def _split_shape(local_shape, axis):
    return local_shape[:axis] + (N_DEV, local_shape[axis]) + local_shape[axis + 1:]


def _mesh_pos():
    return lax.axis_index("x"), lax.axis_index("y"), lax.axis_index("c")


def _flip(v, f):
    return 1 - v if f else v


def _hbm_call(name, body, arrays, out_shapes, n_sems):
    n_t = len(arrays)
    hbm = pl.BlockSpec(memory_space=pl.ANY)
    return pl.pallas_call(
        body, name=name, out_shape=out_shapes, in_specs=[hbm] * n_t, out_specs=[hbm] * n_t,
        scratch_shapes=[pltpu.SemaphoreType.DMA((n_t, n_sems)), pltpu.SemaphoreType.DMA((n_t, n_sems)),
                        pltpu.SemaphoreType.DMA((n_t,))],
    )(*arrays)


def _all_gather(name, locals_):
    n_t = len(locals_)

    def body(*refs):
        x_refs, out_refs = refs[:n_t], refs[n_t:2 * n_t]
        send_sems, recv_sems, local_sems = refs[2 * n_t:]
        x, y, c = _mesh_pos()
        me, sibling = (x, y, c), (x, y, 1 - c)
        chips = [(1 - x, y), (x, 1 - y), (1 - x, 1 - y)]

        def copy(t, k, block, to, own=False):
            slot = out_refs[t].at[4 * block[0] + 2 * block[1] + block[2]]
            return pltpu.make_async_remote_copy(
                src_ref=x_refs[t] if own else slot, dst_ref=slot,
                send_sem=send_sems.at[t, k], recv_sem=recv_sems.at[t, k],
                device_id=to, device_id_type=pl.DeviceIdType.MESH)

        mine = [pltpu.make_async_copy(x_refs[t], out_refs[t].at[4 * x + 2 * y + c], local_sems.at[t]) for t in range(n_t)]
        for cp in mine:
            cp.start()
        sent = []
        for t in range(n_t):
            sent.append(copy(t, 0, me, sibling, own=True))
            sent += [copy(t, 1 + j, me, (*chip, c), own=True) for j, chip in enumerate(chips)]
        for cp in sent:
            cp.start()
        for j, chip in enumerate(chips):
            for t in range(n_t):
                copy(t, 1 + j, (*chip, c), me).wait_recv()
                passed = copy(t, 4 + j, (*chip, c), sibling)
                passed.start()
                sent.append(passed)
        for t in range(n_t):
            copy(t, 0, sibling, me).wait_recv()
            for j, chip in enumerate(chips):
                copy(t, 4 + j, (*chip, 1 - c), me).wait_recv()
        for cp in sent:
            cp.wait_send()
        for cp in mine:
            cp.wait()

    outs = [jax.ShapeDtypeStruct((N_DEV,) + a.shape, a.dtype) for a in locals_]
    return _hbm_call(name, body, locals_, outs, 7)


def _all_to_all(name, blocks):
    n_t = len(blocks)

    def body(*refs):
        x_refs, out_refs = refs[:n_t], refs[n_t:2 * n_t]
        send_sems, recv_sems, local_sems = refs[2 * n_t:]
        x, y, c = _mesh_pos()
        my_slot = 4 * x + 2 * y + c
        mine = [pltpu.make_async_copy(x_refs[t].at[my_slot], out_refs[t].at[my_slot], local_sems.at[t]) for t in range(n_t)]
        for cp in mine:
            cp.start()
        copies = []
        for k in range(1, N_DEV):
            peer = (_flip(x, k & 4), _flip(y, k & 2), _flip(c, k & 1))
            peer_slot = 4 * peer[0] + 2 * peer[1] + peer[2]
            for t in range(n_t):
                sems = dict(send_sem=send_sems.at[t, k - 1], recv_sem=recv_sems.at[t, k - 1],
                            device_id=peer, device_id_type=pl.DeviceIdType.MESH)
                cp = pltpu.make_async_remote_copy(src_ref=x_refs[t].at[peer_slot], dst_ref=out_refs[t].at[my_slot], **sems)
                cp.start()
                landed = pltpu.make_async_remote_copy(src_ref=x_refs[t].at[my_slot], dst_ref=out_refs[t].at[peer_slot], **sems)
                copies.append((cp, landed))
        for sent, landed in copies:
            landed.wait_recv()
            sent.wait_send()
        for cp in mine:
            cp.wait()

    outs = [jax.ShapeDtypeStruct(a.shape, a.dtype) for a in blocks]
    return _hbm_call(name, body, blocks, outs, 7)


_HBM = pl.BlockSpec(memory_space=pltpu.HBM)
_SEM = pl.BlockSpec(memory_space=pltpu.SEMAPHORE)
_EFFECT = pltpu.SideEffectType.DATAFLOW_SIDE_EFFECTING


def _peer_copies(x_refs, land_refs, send_sems, recv_sems, gather, mine):
    x, y, c = _mesh_pos()
    my_slot = 4 * x + 2 * y + c
    copies = []
    for k in range(1, N_DEV):
        peer = (_flip(x, k & 4), _flip(y, k & 2), _flip(c, k & 1))
        peer_slot = 4 * peer[0] + 2 * peer[1] + peer[2]
        for t, (x_ref, land) in enumerate(zip(x_refs, land_refs)):
            sem = t * (N_DEV - 1) + k - 1
            copies.append(pltpu.make_async_remote_copy(
                src_ref=x_ref if gather[t] else x_ref.at[peer_slot], dst_ref=land.at[my_slot if mine else peer_slot],
                send_sem=send_sems.at[sem], recv_sem=recv_sems.at[sem], device_id=peer, device_id_type=pl.DeviceIdType.MESH))
    return copies


def _split_start(name, held, plan, n_copies, after=None):
    n_h = len(held)

    n_in = n_h + (after is not None)

    def body(*refs):
        for sent in plan(refs[:n_h], refs[n_in], refs[n_in + 1], True):
            sent.start()
        refs[-1][...] = jnp.zeros((8, 128), F32)

    sems = pltpu.SemaphoreType.DMA((n_copies,))
    out = pl.pallas_call(
        body, name=name,
        out_shape=(sems, sems, *[pltpu.HBM(a.shape, a.dtype) for a in held], jax.ShapeDtypeStruct((8, 128), F32)),
        in_specs=[_HBM] * n_h + [pl.BlockSpec(memory_space=pl.ANY)] * (after is not None),
        out_specs=(_SEM, _SEM, *[_HBM] * n_h, pl.BlockSpec(memory_space=pltpu.VMEM)),
        input_output_aliases={i: 2 + i for i in range(n_h)},
        compiler_params=pltpu.CompilerParams(has_side_effects=_EFFECT),
    )(*[pltpu.with_memory_space_constraint(a, pltpu.HBM) for a in held], *([after] if after is not None else []))
    return dict(send=out[0], recv=out[1], held=out[2:2 + n_h], zero=out[-1][0, 0], plan=plan)


def _split_wait(name, started, after):
    held, plan = started['held'], started['plan']
    n_h = len(held)

    def body(*refs):
        for sent in plan(refs[:n_h], refs[n_h], refs[n_h + 1], True):
            sent.wait_send()
        for landed in plan(refs[:n_h], refs[n_h], refs[n_h + 1], False):
            landed.wait_recv()

    return pl.pallas_call(
        body, name=name,
        out_shape=[pltpu.HBM(a.shape, a.dtype) for a in held],
        in_specs=[_HBM] * n_h + [_SEM, _SEM, pl.BlockSpec(memory_space=pl.ANY)],
        out_specs=[_HBM] * n_h,
        input_output_aliases={i: i for i in range(n_h)},
        compiler_params=pltpu.CompilerParams(has_side_effects=_EFFECT),
    )(*held, started['send'], started['recv'], after)


def _push_start(name, arrays, gather):
    n_t = len(arrays)
    lands = [lax.empty((N_DEV,) + a.shape if whole else a.shape, a.dtype) for a, whole in zip(arrays, gather)]
    plan = lambda refs, send, recv, mine: _peer_copies(refs[:n_t], refs[n_t:], send, recv, gather, mine)
    return _split_start(name, list(arrays) + lands, plan, n_t * (N_DEV - 1))


def _push_wait(name, started, after):
    held = _split_wait(name, started, after)
    return held[:len(held) // 2], held[len(held) // 2:]


def _chip_copies(refs, send_sems, recv_sems, mine):
    n_t = len(refs) // 2
    x, y, c = _mesh_pos()
    my_slot = 4 * x + 2 * y + c
    peers = [(x, y, 1 - c), (1 - x, y, c), (x, 1 - y, c), (1 - x, 1 - y, c)]
    copies = []
    for j, peer in enumerate(peers):
        peer_slot = 4 * peer[0] + 2 * peer[1] + peer[2]
        for t in range(n_t):
            copies.append(pltpu.make_async_remote_copy(
                src_ref=refs[t], dst_ref=refs[n_t + t].at[my_slot if mine else peer_slot],
                send_sem=send_sems.at[4 * t + j], recv_sem=recv_sems.at[4 * t + j],
                device_id=peer, device_id_type=pl.DeviceIdType.MESH))
    return copies


def _forward_copies(refs, send_sems, recv_sems, mine):
    x, y, c = _mesh_pos()
    sibling = (x, y, 1 - c)
    copies = []
    for j, (px, py) in enumerate([(1 - x, y), (x, 1 - y), (1 - x, 1 - y)]):
        slot = 4 * px + 2 * py + (c if mine else 1 - c)
        for t, land in enumerate(refs):
            copies.append(pltpu.make_async_remote_copy(
                src_ref=land.at[slot], dst_ref=land.at[slot], send_sem=send_sems.at[3 * t + j], recv_sem=recv_sems.at[3 * t + j],
                device_id=sibling, device_id_type=pl.DeviceIdType.MESH))
    return copies


class _TwoStageGather:
    def __init__(self, name, arrays, my_slot, after=None):
        self.name, self.arrays, self.my_slot = name, arrays, my_slot
        values = list(arrays.values())
        lands = [lax.empty((N_DEV,) + a.shape, a.dtype) for a in values]
        self.stage = _split_start(name + "_start", values + lands, _chip_copies, 4 * len(values), after)
        self.zero = self.stage['zero']

    def forward(self, after):
        held = _split_wait(self.name + "_wait", self.stage, after)
        self.stage = _split_start(self.name + "_forward_start", held[len(self.arrays):], _forward_copies, 3 * len(self.arrays))
        return self.stage['zero']

    def finish(self, after):
        lands = _split_wait(self.name + "_forward_wait", self.stage, after)
        return {n: lax.dynamic_update_slice(land, a[None], (self.my_slot,) + (0,) * a.ndim)
                for (n, a), land in zip(self.arrays.items(), lands)}


ADAM_BLOCK = 256 * 1024


def _adam_update(part, w_ref, m_ref, v_ref, g_ref, d_ref, nm_ref, nv_ref):
    g = part(0)
    for i in range(1, N_DEV):
        g = g + part(i)
    m_new = ADAM_B1 * m_ref[...] + (1.0 - ADAM_B1) * g
    v_new = ADAM_B2 * v_ref[...] + (1.0 - ADAM_B2) * (g * g)
    m_hat = m_new * (1.0 / (1.0 - ADAM_B1 ** ADAM_STEP))
    v_hat = v_new * (1.0 / (1.0 - ADAM_B2 ** ADAM_STEP))
    g_ref[...] = g
    nm_ref[...] = m_new
    nv_ref[...] = v_new
    d_ref[...] = -ADAM_LR * (m_hat / (jnp.sqrt(v_hat) + ADAM_EPS) + ADAM_WD * w_ref[...])


def _adam_many(name, items):
    n_t = len(items)
    n_in = [4 if own is None else 5 for *_, own in items]
    first = [sum(n_in[:t]) for t in range(n_t)]

    def body(*refs):
        ins, outs = refs[:sum(n_in)], refs[sum(n_in):]
        x, y, c = _mesh_pos()
        my_slot = 4 * x + 2 * y + c
        for t in range(n_t):
            s_ref, w_ref, m_ref, v_ref = ins[first[t]:first[t] + 4]
            if n_in[t] == 4:
                part = lambda i: s_ref[i]
            else:
                own_ref = ins[first[t] + 4]
                part = lambda i: jnp.where(my_slot == i, own_ref[...], s_ref[i])
            _adam_update(part, w_ref, m_ref, v_ref, *outs[4 * t:4 * t + 4])

    vmem = pl.BlockSpec(memory_space=pltpu.VMEM)
    out = pl.pallas_call(
        body, name=name, in_specs=[vmem] * sum(n_in), out_specs=[vmem] * (4 * n_t),
        out_shape=[jax.ShapeDtypeStruct(item[1].shape, F32) for item in items for _ in range(4)],
        compiler_params=pltpu.CompilerParams(vmem_limit_bytes=VMEM_LIMIT),
    )(*[a for item in items for a in item if a is not None])
    return [out[4 * t:4 * t + 4] for t in range(n_t)]


def _adam_call(name, slots, w, m, v, own=None):
    n_layer, rows, width = w.shape
    tile = min(rows, ADAM_BLOCK // max(width, 128))

    def body(*refs):
        s_refs = refs[:n_layer]
        own_refs = refs[n_layer:2 * n_layer] if own is not None else None
        w_ref, m_ref, v_ref = refs[-7:-4]
        layer = pl.program_id(0)
        x, y, c = _mesh_pos()
        my_slot = 4 * x + 2 * y + c
        for l in range(n_layer):
            @pl.when(layer == l)
            def _(l=l):
                if own is None:
                    part = lambda i: s_refs[l][i].astype(F32)
                else:
                    part = lambda i: jnp.where(my_slot == i, own_refs[l][...], s_refs[l][i]).astype(F32)
                _adam_update(part, w_ref, m_ref, v_ref, *refs[-4:])

    slot_spec = lambda k: _spec((N_DEV, tile, width), lambda l, i: (0, jnp.where(l == k, i, 0), 0))
    own_spec = lambda k: _spec((tile, width), lambda l, i: (jnp.where(l == k, i, 0), 0))
    blk = _spec((None, tile, width), lambda l, i: (l, i, 0))
    specs = [slot_spec(k) for k in range(n_layer)] + ([own_spec(k) for k in range(n_layer)] if own is not None else [])
    return pl.pallas_call(
        body, name=name, grid=(n_layer, rows // tile),
        in_specs=specs + [blk] * 3, out_specs=[blk] * 4, out_shape=[jax.ShapeDtypeStruct(w.shape, F32)] * 4,
        compiler_params=_cparams(2),
    )(*slots, *(own or []), w, m, v)


def kernel(x, ln_g, ln_b, rw_mu, rw_w0, rw_w1, rw_w2, rw_a0, rw_a1, rw_a2, rw_g1, rw_g2, rw_k_k, rw_k_a, rw_r_k, rw_wr, rw_wk, rw_wv, rw_wo, rw_lnx_g, rw_lnx_b, s5_a_re, s5_a_im, s5_log_dt, s5_b_re, s5_b_im, s5_c_re, s5_c_im, s5_d, s5_w_glu, mlp_w1, mlp_w2, loss_target, m_ln_g, m_ln_b, m_rw_mu, m_rw_w0, m_rw_w1, m_rw_w2, m_rw_a0, m_rw_a1, m_rw_a2, m_rw_g1, m_rw_g2, m_rw_k_k, m_rw_k_a, m_rw_r_k, m_rw_wr, m_rw_wk, m_rw_wv, m_rw_wo, m_rw_lnx_g, m_rw_lnx_b, m_s5_a_re, m_s5_a_im, m_s5_log_dt, m_s5_b_re, m_s5_b_im, m_s5_c_re, m_s5_c_im, m_s5_d, m_s5_w_glu, m_mlp_w1, m_mlp_w2, v_ln_g, v_ln_b, v_rw_mu, v_rw_w0, v_rw_w1, v_rw_w2, v_rw_a0, v_rw_a1, v_rw_a2, v_rw_g1, v_rw_g2, v_rw_k_k, v_rw_k_a, v_rw_r_k, v_rw_wr, v_rw_wk, v_rw_wv, v_rw_wo, v_rw_lnx_g, v_rw_lnx_b, v_s5_a_re, v_s5_a_im, v_s5_log_dt, v_s5_b_re, v_s5_b_im, v_s5_c_re, v_s5_c_im, v_s5_d, v_s5_w_glu, v_mlp_w1, v_mlp_w2):
    given = dict(locals())
    local_w = {n: given[n] for n in WEIGHTS}
    local_m = {n: given["m_" + n] for n in WEIGHTS}
    local_v = {n: given["v_" + n] for n in WEIGHTS}
    small = [n for n in WEIGHTS if n in SHARDED and n not in BIG]

    my_slot = 4 * lax.axis_index("x") + 2 * lax.axis_index("y") + lax.axis_index("c")
    as_bf16 = {n: local_w[n].astype(BF16) for n in BIG}

    gathered = _all_gather("weights_all_gather", [local_w[n] for n in small])
    full = dict(local_w)
    for n, blk in zip(small, gathered):
        full[n] = jnp.moveaxis(blk, 0, SHARDED[n]).reshape(_full_shape(local_w[n].shape, SHARDED[n]))
    mid = _TwoStageGather("weights_mid", {n: as_bf16[n] for n in MID}, my_slot)
    late = lambda after: _TwoStageGather("weights_late", {n: as_bf16[n] for n in LATE}, my_slot, after)
    full['rw_w0'] = full['rw_w0'] + mid.zero

    exchanges = []

    def exchange(entries):
        started = _push_start(f"grads_start_{len(exchanges)}", [t for _, _, t in entries], [n in REPLICATED for n, _, _ in entries])
        exchanges.append((entries, started))
        return started['zero']

    loss_sq, dx, grads = _local_step(x[0], loss_target[0], full, mid, late, exchange)
    loss = (0.5 / D_MODEL) * lax.psum(loss_sq, MESH_AXES)

    wide = {'s5_b_re': (SSM_LANES, SSM_GROUP), 's5_b_im': (SSM_LANES, SSM_GROUP),
            's5_c_re': (D_MODEL, SSM_STATE), 's5_c_im': (D_MODEL, SSM_STATE)}
    by_name, many = {}, []
    landed = {n: {} for n in BIG}
    for e, (entries, started) in enumerate(exchanges):
        sources, lands = _push_wait(f"grads_wait_{e}", started, dx)
        for (n, i, _), src, slots in zip(entries, sources, lands):
            if n in wide:
                as_view = lambda t: t.reshape((1,) + wide[n])
                out = _adam_call(f"adamw_{n}", [slots.reshape((N_DEV,) + wide[n])], as_view(local_w[n]), as_view(local_m[n]),
                                 as_view(local_v[n]), own=[src.reshape(wide[n])])
                by_name[n] = [t.reshape(local_w[n].shape) for t in out]
            elif n in REPLICATED:
                many.append((n, (slots, local_w[n], local_m[n], local_v[n], src)))
            else:
                landed[n][i] = (slots, lax.dynamic_index_in_dim(src, my_slot, 0, keepdims=False))
    for n in BIG:
        layers = [landed[n][i] for i in sorted(landed[n])]
        by_name[n] = _adam_call(f"adamw_{n}", [s for s, _ in layers], local_w[n], local_m[n], local_v[n], own=[o for _, o in layers])
    split = lambda n: jnp.moveaxis(grads[n].reshape(_split_shape(local_w[n].shape, SHARDED[n])), SHARDED[n], 0)
    for n, slots in zip(small, _all_to_all("grads_all_to_all", [split(n) for n in small])):
        many.append((n, (slots, local_w[n], local_m[n], local_v[n], None)))
    last = [n for n in REPLICATED if n not in by_name and n not in dict(many)]
    for n, slots in zip(last, _all_gather("grads_all_gather", [grads[n] for n in last])):
        many.append((n, (slots, local_w[n], local_m[n], local_v[n], None)))
    for (n, _), out in zip(many, _adam_many("adamw_small", [item for _, item in many])):
        by_name[n] = out
    results = [by_name[n][k] for k in range(4) for n in WEIGHTS]
    return (loss, dx[None], *results)
```

```python
import functools
import math

import jax
import jax.numpy as jnp
from jax import lax
from jax.experimental import pallas as pl
from jax.experimental.pallas import tpu as pltpu

F32 = jnp.float32
BF16 = jnp.bfloat16

D_MODEL = 1024
HEAD = 64
PAIR = 2 * HEAD
N_PAIR = D_MODEL // PAIR
CHUNK = 64
GN_EPS = 64e-5
LN_EPS = 1e-5
SSM_GROUP = 16
SSM_STATE = 64
SSM_GROUPS = D_MODEL // SSM_GROUP
SSM_LANES = SSM_GROUPS * SSM_STATE
SSM_BLK_IN = 128
SSM_BLK_ST = 512
N_SSM_BLK = D_MODEL // SSM_BLK_IN
DEPTH = 2
DN_ALPHA = (2.0 * DEPTH) ** 0.25
ADAM_LR, ADAM_B1, ADAM_B2, ADAM_EPS, ADAM_WD, ADAM_STEP = 0.001, 0.9, 0.999, 1e-08, 0.01, 10
N_DEV = 8
MESH_AXES = ("x", "y", "c")
VMEM_LIMIT = 56 * 1024 * 1024
TOKEN_TILE = 256
LN_TILE = 512
WIDE_ROW_TILE = 2048

SHARDED = {
    "rw_mu": 2, "rw_w1": 1, "rw_w2": 2, "rw_a1": 1, "rw_a2": 2, "rw_g1": 1, "rw_g2": 2,
    "rw_wr": 1, "rw_wk": 1, "rw_wv": 1, "rw_wo": 1, "s5_d": 1, "s5_w_glu": 2, "mlp_w1": 2, "mlp_w2": 1,
}
WEIGHTS = ['ln_g', 'ln_b', 'rw_mu', 'rw_w0', 'rw_w1', 'rw_w2', 'rw_a0', 'rw_a1', 'rw_a2', 'rw_g1', 'rw_g2', 'rw_k_k',
           'rw_k_a', 'rw_r_k', 'rw_wr', 'rw_wk', 'rw_wv', 'rw_wo', 'rw_lnx_g', 'rw_lnx_b', 's5_a_re', 's5_a_im',
           's5_log_dt', 's5_b_re', 's5_b_im', 's5_c_re', 's5_c_im', 's5_d', 's5_w_glu', 'mlp_w1', 'mlp_w2']
REPLICATED = [n for n in WEIGHTS if n not in SHARDED]
BIG = ['rw_wr', 'rw_wk', 'rw_wv', 'rw_wo', 's5_w_glu', 'mlp_w1', 'mlp_w2']
MID = ['rw_wr', 'rw_wk', 'rw_wv']
LATE = ['rw_wo', 's5_w_glu', 'mlp_w1', 'mlp_w2']
BRANCH_DTYPE = BF16
GRAD_DTYPE = BF16


def _cparams(n_grid):
    return pltpu.CompilerParams(dimension_semantics=("arbitrary",) * n_grid, vmem_limit_bytes=VMEM_LIMIT)


@jax.custom_vjp
def _mm(x, w):
    return jnp.dot(x.astype(BF16), w.astype(BF16), preferred_element_type=F32)


def _mm_fwd(x, w):
    return _mm(x, w), (x, w)


def _mm_bwd(res, dy):
    x, w = res
    dyb = dy.astype(BF16)
    dx = lax.dot_general(dyb, w.astype(BF16), (((1,), (1,)), ((), ())), preferred_element_type=F32)
    dw = lax.dot_general(x.astype(BF16), dyb, (((0,), (0,)), ((), ())), preferred_element_type=F32)
    return dx, dw


_mm.defvjp(_mm_fwd, _mm_bwd)


_DOT_DIMS = {'nn': (((1,), (0,)), ((), ())), 'nt': (((1,), (1,)), ((), ())), 'tn': (((0,), (0,)), ((), ()))}
_BATCH_DOT_DIMS = {'nn': (((2,), (1,)), ((0,), (0,))), 'nt': (((2,), (2,)), ((0,), (0,))), 'tn': (((1,), (1,)), ((0,), (0,)))}
CHUNK_PASSES = 1


def _split_bf16(a):
    hi = a.astype(BF16)
    return hi, (a - hi.astype(F32)).astype(BF16)


def _dot_passes(a, b, mode, passes=None):
    dims = (_DOT_DIMS if a.ndim == 2 else _BATCH_DOT_DIMS)[mode]
    dot = lambda p, q: lax.dot_general(p, q, dims, preferred_element_type=F32)
    if (CHUNK_PASSES if passes is None else passes) == 1:
        return dot(a.astype(BF16), b.astype(BF16))
    (ah, al), (bh, bl) = _split_bf16(a), _split_bf16(b)
    return dot(ah, bh) + (dot(ah, bl) + dot(al, bh))


@functools.partial(jax.custom_vjp, nondiff_argnums=(2,))
def _pdot(a, b, mode):
    return _dot_passes(a, b, mode)


def _pdot_fwd(a, b, mode):
    return _dot_passes(a, b, mode), (a, b)


def _pdot_bwd(mode, res, dy):
    a, b = res
    if mode == 'nn':
        return _dot_passes(dy, b, 'nt'), _dot_passes(a, dy, 'tn')
    if mode == 'nt':
        return _dot_passes(dy, b, 'nn'), _dot_passes(dy, a, 'tn')
    return _dot_passes(b, dy, 'nt'), _dot_passes(a, dy, 'nn')


_pdot.defvjp(_pdot_fwd, _pdot_bwd)


def _tri_sum(x, upper):
    nb, n = x.shape[0], x.shape[1]
    ti = lax.broadcasted_iota(jnp.int32, (nb, n, n), 1)
    tj = lax.broadcasted_iota(jnp.int32, (nb, n, n), 2)
    tri = ((ti <= tj) if upper else (ti >= tj)).astype(BF16)
    hi = x.astype(BF16)
    rest = x - hi.astype(F32)
    mid = rest.astype(BF16)
    lo = (rest - mid.astype(F32)).astype(BF16)
    dot = lambda q: lax.dot_general(tri, q, _BATCH_DOT_DIMS['nn'], preferred_element_type=F32)
    return dot(hi) + (dot(mid) + dot(lo))


@jax.custom_vjp
def _cumsum_rows(x):
    return _tri_sum(x, False)


_cumsum_rows.defvjp(lambda x: (_tri_sum(x, False), None), lambda _, dy: (_tri_sum(dy, True),))


def _power_sum(a, order):
    m = a.shape[-1]
    ti = lax.broadcasted_iota(jnp.int32, (1, m, m), 1)
    tj = lax.broadcasted_iota(jnp.int32, (1, m, m), 2)
    pw, total = _dot_passes(a, a, 'nn'), (ti == tj).astype(F32) + a
    for _ in range(int(math.log2(order)) - 2):
        both = _dot_passes(pw, jnp.concatenate([pw, total], axis=2), 'nn')
        pw, total = both[:, :, :m], total + both[:, :, m:]
    return total + _dot_passes(pw, total, 'nn')


@functools.partial(jax.custom_vjp, nondiff_argnums=(1,))
def _nilpotent_inverse(a, order):
    return _power_sum(a, order)


def _nilpotent_inverse_fwd(a, order):
    inv = _power_sum(a, order)
    return inv, inv


def _nilpotent_inverse_bwd(order, inv, g):
    return (_dot_passes(_dot_passes(inv, g, 'tn'), inv, 'nt'),)


_nilpotent_inverse.defvjp(_nilpotent_inverse_fwd, _nilpotent_inverse_bwd)


@jax.custom_vjp
def _known_inverse(a, inv):
    return inv


_known_inverse.defvjp(lambda a, inv: (inv, inv),
                      lambda inv, g: (_dot_passes(_dot_passes(inv, g, 'tn'), inv, 'nt'), jnp.zeros_like(inv)))


@functools.partial(jax.custom_vjp, nondiff_argnums=(1,))
def _roll_rows(x, shift):
    return pltpu.roll(x, shift, 0)


def _roll_rows_fwd(x, shift):
    return pltpu.roll(x, shift, 0), None


def _roll_rows_bwd(shift, _, dy):
    n = dy.shape[0]
    return (pltpu.roll(dy, (n - shift) % n, 0),)


_roll_rows.defvjp(_roll_rows_fwd, _roll_rows_bwd)


def _shift_down(x, first_row):
    row = lax.broadcasted_iota(jnp.int32, x.shape, 0)
    return jnp.where(row == 0, first_row, _roll_rows(x, 1))


def _sigmoid(x):
    return 1.0 / (1.0 + jnp.exp(-x))


def _softplus(x):
    return jnp.maximum(x, 0.0) + jnp.log(1.0 + jnp.exp(-jnp.abs(x)))


def _gelu(x):
    return 0.5 * x * (1.0 + jnp.tanh(math.sqrt(2.0 / math.pi) * (x + 0.044715 * (x * x * x))))


def _layer_norm(x, g, b):
    mu = jnp.mean(x, axis=-1, keepdims=True)
    xc = x - mu
    var = jnp.mean(xc * xc, axis=-1, keepdims=True)
    return xc * lax.rsqrt(var + LN_EPS) * g + b


def _spec(block, index_map):
    return pl.BlockSpec(block, index_map)


def _tile(arr, tm):
    return (arr, (tm, arr.shape[1]), lambda i: (i, 0))


def _full(arr):
    nd = arr.ndim
    return (arr, arr.shape, lambda *_: (0,) * nd)


def _fwd_call(name, fn, grid, prims, outs):
    n_in = len(prims)

    def body(*refs):
        vals = [r[...] for r in refs[:n_in]]
        vals = [v.astype(F32) if v.dtype != F32 else v for v in vals]
        res = fn(*vals)
        for o, r in zip(refs[n_in:], res):
            o[...] = r.astype(o.dtype)

    return pl.pallas_call(
        body, name=name, grid=grid,
        in_specs=[_spec(b, m) for _, b, m in prims],
        out_specs=[_spec(b, m) for _, _, b, m in outs],
        out_shape=[jax.ShapeDtypeStruct(s, d) for s, d, _, _ in outs],
        compiler_params=_cparams(len(grid)),
    )(*[a for a, _, _ in prims])


def _bwd_call(name, fn, grid, prims, cts, wants, addends=None):
    addends = addends or {}
    n_in, n_ct = len(prims), len(cts)
    out_place = {}
    wants, out_dtype = list(wants), {}
    for i, w in enumerate(wants):
        if isinstance(w, tuple):
            wants[i], out_place[i] = w
        elif w == 'tile_bf16':
            wants[i], out_dtype[i] = 'tile', BF16
    diff = [i for i, w in enumerate(wants) if w]
    add_idx = [i for i in diff if i in addends]
    last_axis = len(grid) - 1

    def body(*refs):
        vals = [r[...] for r in refs[:n_in]]
        vals = [v.astype(F32) if v.dtype != F32 else v for v in vals]
        ct_vals = [r[...] for r in refs[n_in:n_in + n_ct]]
        ct_vals = [v.astype(F32) if v.dtype != F32 else v for v in ct_vals]
        add_refs = dict(zip(add_idx, refs[n_in + n_ct:n_in + n_ct + len(add_idx)]))
        out_refs = refs[n_in + n_ct + len(add_idx):]

        def f(*dargs):
            full = list(vals)
            for i, a in zip(diff, dargs):
                full[i] = a
            return tuple(fn(*full))

        _, vjp = jax.vjp(f, *[vals[i] for i in diff])
        grads = vjp(tuple(ct_vals))
        first = pl.program_id(last_axis) == 0
        for o, g, i in zip(out_refs, grads, diff):
            if wants[i] == 'tile':
                if i in add_refs:
                    g = g + add_refs[i][...]
                o[...] = g.astype(o.dtype)
            else:
                @pl.when(first)
                def _(o=o):
                    o[...] = jnp.zeros(o.shape, o.dtype)
                o[...] += g

    ins = list(prims) + list(cts) + [addends[i] for i in add_idx]
    places = [out_place.get(i, (prims[i][0].shape, prims[i][1], prims[i][2])) for i in diff]
    return pl.pallas_call(
        body, name=name, grid=grid,
        in_specs=[_spec(b, m) for _, b, m in ins],
        out_specs=[_spec(b, m) for _, b, m in places],
        out_shape=[jax.ShapeDtypeStruct(s, out_dtype.get(i, F32)) for i, (s, _, _) in zip(diff, places)],
        compiler_params=_cparams(len(grid)),
    )(*[a for a, _, _ in ins])


def _matmul(name, a, b, mode, *, out_dtype=F32, lhs_fn=None, epi=None, extra=None, tm=1024, tn=1024, tk=1024,
            b_cols=None, b_rows=None, out_cols=False):
    if mode == 'tn':
        kdim, m = a.shape
    else:
        m, kdim = a.shape
    if b_rows is not None:
        loc, cols = b.shape[-2], b.shape[-1]
        if mode == 'nn':
            n, k_shards = cols, max(1, tk // loc)
            tk = k_shards * loc
        else:
            n, tn = N_DEV * loc, loc
    elif b_cols is None:
        n = b.shape[0] if mode == 'nt' else b.shape[1]
    else:
        rows, loc = b.shape[-2], b.shape[-1]
        if mode == 'nn':
            n, n_shards = N_DEV * loc, max(1, tn // loc)
            tn = n_shards * loc
        else:
            n, k_shards = rows, max(1, tk // loc)
            tk = k_shards * loc
    if out_cols:
        o_shards = max(1, min(tn, n) // (n // N_DEV))
        tn = o_shards * (n // N_DEV)
    tm, tn, tk = min(tm, m), min(tn, n), min(tk, kdim)
    nk = kdim // tk
    grid = (m // tm, n // tn, nk)
    a_spec = _spec((tk, tm), lambda i, j, k: (k, i)) if mode == 'tn' else _spec((tm, tk), lambda i, j, k: (i, k))
    if b_rows is not None:
        lead = tuple(b_rows)
        skip = (None,) * (1 + len(lead))
        if mode == 'nn':
            b_spec = _spec((k_shards,) + skip[1:] + (tk // k_shards, tn), lambda i, j, k: (k,) + lead + (0, j))
        else:
            b_spec = _spec(skip + (tn, tk), lambda i, j, k: (j,) + lead + (0, k))
    elif b_cols is None:
        b_spec = _spec((tn, tk), lambda i, j, k: (j, k)) if mode == 'nt' else _spec((tk, tn), lambda i, j, k: (k, j))
    else:
        lead = tuple(b_cols)
        skip = (None,) * (1 + len(lead))
        if mode == 'nn':
            b_spec = _spec((n_shards,) + skip[1:] + (tk, tn // n_shards), lambda i, j, k: (j,) + lead + (k, 0))
        else:
            b_spec = _spec((k_shards,) + skip[1:] + (tn, tk // k_shards), lambda i, j, k: (k,) + lead + (j, 0))
    o_spec = _spec((tm, tn), lambda i, j, k: (i, j))
    if out_cols:
        o_place, o_shape = _spec((o_shards, tm, tn // o_shards), lambda i, j, k: (j, i, 0)), (N_DEV, m, n // N_DEV)
    else:
        o_place, o_shape = o_spec, (m, n)
    dims = _DOT_DIMS[mode]

    def body(*refs):
        if extra is None:
            a_ref, b_ref, o_ref, acc = refs
            x_ref = None
        else:
            a_ref, b_ref, x_ref, o_ref, acc = refs
        k = pl.program_id(2)
        av = a_ref[...]
        if lhs_fn is not None:
            av = lhs_fn(av.astype(F32))
        bv = b_ref[...]
        if bv.ndim == 3:
            bv = bv.reshape(tk, tn) if b_rows is not None else jnp.concatenate([bv[g] for g in range(bv.shape[0])], axis=1)
        part = lax.dot_general(av.astype(BF16), bv.astype(BF16), dims, preferred_element_type=F32)

        def finish(r):
            if epi is not None:
                r = epi(r, x_ref[...])
            if out_cols:
                width = tn // o_shards
                for g in range(o_shards):
                    o_ref[g] = r[:, g * width:(g + 1) * width].astype(o_ref.dtype)
            else:
                o_ref[...] = r.astype(o_ref.dtype)

        if nk == 1:
            finish(part)
            return

        @pl.when(k == 0)
        def _():
            acc[...] = part

        @pl.when((k > 0) & (k < nk - 1))
        def _():
            acc[...] += part

        @pl.when(k == nk - 1)
        def _():
            finish(acc[...] + part)

    ops, specs = [a, b], [a_spec, b_spec]
    if extra is not None:
        ops.append(extra)
        specs.append(o_spec)
    return pl.pallas_call(
        body, name=name, grid=grid, in_specs=specs, out_specs=o_place,
        out_shape=jax.ShapeDtypeStruct(o_shape, out_dtype),
        scratch_shapes=[pltpu.VMEM((tm, tn), F32)],
        compiler_params=_cparams(3),
    )(*ops)


def _relu2(x):
    r = jnp.maximum(x, 0.0)
    return r * r


def _mix_fn(x, halo, mu, w0, w1, w2, a0, a1, a2, g1, g2):
    prev_row = jnp.where(pl.program_id(0) == 0, 0.0, halo[7:8, :])
    xx = _shift_down(x, prev_row) - x
    xr, xw, xk, xv, xa, xg = (x + xx * mu[i:i + 1, :] for i in range(6))
    w_pre = w0 + _mm(jnp.tanh(_mm(xw, w1)), w2)
    log_decay = -jnp.exp(-_softplus(-w_pre) - 0.5)
    a = _sigmoid(a0 + _mm(_mm(xa, a1), a2))
    g = _mm(_sigmoid(_mm(xg, g1)), g2)
    return xr, xk, xv, log_decay, a, g


def _chunk_fn(r, lw, k, v, a, k_k, k_a, r_k, inverse=None, with_inverse=False):
    n, n_pair = r.shape[0], r.shape[1] // PAIR
    to_batch = lambda t: jnp.concatenate([t[None, :, p * PAIR:(p + 1) * PAIR] for p in range(n_pair)], axis=0)
    from_batch = lambda t: jnp.concatenate([t[p] for p in range(n_pair)], axis=1)
    r, lw, k, v, a, k_k, k_a, r_k = (to_batch(t) for t in (r, lw, k, v, a, k_k, k_a, r_k))
    lane = lax.broadcasted_iota(jnp.int32, (1, 1, PAIR), 2)
    m0 = lane < HEAD

    def head_sum(t):
        s0 = jnp.sum(jnp.where(m0, t, 0.0), axis=2, keepdims=True)
        s1 = jnp.sum(jnp.where(m0, 0.0, t), axis=2, keepdims=True)
        return jnp.where(m0, s0, s1)

    kk = k * k_k
    kk = kk / jnp.maximum(jnp.sqrt(head_sum(kk * kk)), 1e-12)
    k2 = k * (1.0 + (a - 1.0) * k_a)
    aa, bb = -kk, kk * a
    bonus = head_sum(r * k2 * r_k) * v

    cum = _cumsum_rows(lw)
    p_in, p_ex, p_inv = jnp.exp(cum), jnp.exp(cum - lw), jnp.exp(-cum)
    at, rt, kt, bt = aa * p_ex, r * p_in, k2 * p_inv, bb * p_inv

    def stack_masked(t):
        return jnp.concatenate([jnp.where(m0, t, 0.0), jnp.where(m0, 0.0, t)], axis=1)

    def unstack_sum(t):
        return t[:, :n] + t[:, n:]

    def unstack_select(t):
        return jnp.where(m0, t[:, :n], t[:, n:])

    ti = lax.broadcasted_iota(jnp.int32, (1, 2 * n, 2 * n), 1)
    tj = lax.broadcasted_iota(jnp.int32, (1, 2 * n, 2 * n), 2)
    same = (ti >= n) == (tj >= n)
    incl, strict = same & (ti >= tj), same & (ti > tj)
    m = 2 * n
    rows = lambda *ts: jnp.concatenate(ts, axis=1)
    cols = lambda *ts: jnp.concatenate(ts, axis=2)
    at_s, rt_s, kt_s, bt_s = stack_masked(at), stack_masked(rt), stack_masked(kt), stack_masked(bt)
    v_s, zeros = rows(v, v), jnp.zeros_like(at_s)
    bk_s = rows(bt_s, kt_s)
    gram = _pdot(rows(at_s, rt_s), bk_s, 'nt')
    a_ab = jnp.where(strict, gram[:, :m, :m], 0.0)
    a_ak = jnp.where(strict, gram[:, :m, m:], 0.0)
    a_rb = jnp.where(incl, gram[:, m:, :m], 0.0)
    a_rk = jnp.where(incl, gram[:, m:, m:], 0.0)
    inv = _nilpotent_inverse(a_ab, n) if inverse is None else _known_inverse(a_ab, inverse)
    wy_s = _pdot(inv, cols(at_s, _pdot(a_ak, v_s, 'nn')), 'nn')
    q_oi = _pdot(cols(a_rb, a_rk), rows(wy_s, cols(zeros, v_s)), 'nn')
    q_all = unstack_sum(rt_s + q_oi[:, :, :PAIR])
    oi_all = unstack_select(q_oi[:, :, PAIR:])
    own_head = (lax.broadcasted_iota(jnp.int32, (1, m, PAIR), 1) >= n) == (lax.broadcasted_iota(jnp.int32, (1, m, PAIR), 2) >= HEAD)
    wy_m = cols(wy_s[:, :, :PAIR], jnp.where(own_head, wy_s[:, :, PAIR:], 0.0))
    mm_nn = _pdot(bk_s, rows(wy_m, cols(zeros, stack_masked(v))), 'tn')
    mm, nn = mm_nn[:, :, :PAIR], mm_nn[:, :, PAIR:]
    ei = lax.broadcasted_iota(jnp.int32, (1, PAIR, PAIR), 1)
    ej = lax.broadcasted_iota(jnp.int32, (1, PAIR, PAIR), 2)
    eye_p = (ei == ej).astype(F32)
    decay_col = jnp.sum(eye_p * p_in[:, n - 1:n, :], axis=2, keepdims=True)
    maps = ((decay_col * (eye_p + mm))[None], (decay_col * nn)[None], from_batch(q_all), from_batch(oi_all),
            from_batch(bonus))
    return maps + (inv,) if with_inverse else maps


def _gate_fn(o, bonus, g, lnx_g, lnx_b):
    lane = lax.broadcasted_iota(jnp.int32, (1, PAIR), 1)
    m0 = lane < HEAD

    def head_mean(t):
        s0 = jnp.sum(jnp.where(m0, t, 0.0), axis=1, keepdims=True)
        s1 = jnp.sum(jnp.where(m0, 0.0, t), axis=1, keepdims=True)
        return jnp.where(m0, s0, s1) * (1.0 / HEAD)

    outs = []
    for p in range(N_PAIR):
        sl = slice(p * PAIR, (p + 1) * PAIR)
        oc = o[:, sl] - head_mean(o[:, sl])
        on = oc * lax.rsqrt(head_mean(oc * oc) + GN_EPS) * lnx_g[:, sl] + lnx_b[:, sl]
        outs.append((on + bonus[:, sl]) * g[:, sl])
    return (jnp.concatenate(outs, axis=1),)


def _res_ln_fn(h, y, g, b):
    return (_layer_norm(DN_ALPHA * h + y, g, b),)


def _glu_ln_fn(h, z, g, b):
    mix = z[:, :D_MODEL] * _sigmoid(z[:, D_MODEL:])
    return (_layer_norm(DN_ALPHA * h + mix, g, b),)


def _s5_param_fn(a_re, a_im, log_dt, b_re, b_im):
    dt = jnp.exp(log_dt)
    lam_re = jnp.minimum(a_re, -1e-4)
    lam_im = a_im
    mag = jnp.exp(dt * lam_re)
    abar_re = mag * jnp.cos(dt * lam_im)
    abar_im = mag * jnp.sin(dt * lam_im)
    den = lam_re * lam_re + lam_im * lam_im
    nr, ni = abar_re - 1.0, abar_im
    coef_re = (nr * lam_re + ni * lam_im) / den
    coef_im = (ni * lam_re - nr * lam_im) / den
    return abar_re, abar_im, coef_re, coef_im


def _s5_bbar_fn(coef_re, coef_im, b_re, b_im):
    return coef_re * b_re - coef_im * b_im, coef_re * b_im + coef_im * b_re


def _s5_in_fn(u, b_re, b_im):
    return _pdot(u, b_re[0], 'nn'), _pdot(u, b_im[0], 'nn')


def _s5_out_fn(s_re, s_im, u, c_re, c_im, d):
    y = _pdot(s_re, c_re[0], 'nn') - _pdot(s_im, c_im[0], 'nn') + u * d
    return (_gelu(y),)


def _pairs_to_batch(t):
    return jnp.concatenate([t[None, :, p * PAIR:(p + 1) * PAIR] for p in range(N_PAIR)], axis=0)


def _batch_to_pairs(t):
    return jnp.concatenate([t[p] for p in range(N_PAIR)], axis=1)


STATE_PASSES = 1


def _rwkv_scan_fwd(r, lw, k, v, a, k_k, k_a, r_k):
    n_tok = r.shape[0]
    n_chunk = n_tok // CHUNK

    def body(r_ref, lw_ref, k_ref, v_ref, a_ref, kk_ref, ka_ref, rk_ref, o_ref, bonus_ref, zs_ref, inv_ref, m_ref, q_ref, z):
        @pl.when(pl.program_id(0) == 0)
        def _():
            z[...] = jnp.zeros(z.shape, F32)

        m_all, n_all, q, oi, bonus, inv = _chunk_fn(r_ref[...], lw_ref[...], k_ref[...], v_ref[...], a_ref[...],
                                                    kk_ref[...], ka_ref[...], rk_ref[...], with_inverse=True)
        zb = z[...]
        zs_ref[0] = zb
        inv_ref[0] = inv.astype(inv_ref.dtype)
        m_ref[...] = m_all
        q_ref[...] = q
        bonus_ref[...] = bonus
        o_ref[...] = _batch_to_pairs(_dot_passes(_pairs_to_batch(q), zb, 'nn', STATE_PASSES)) + oi
        z[...] = _dot_passes(m_all[0], zb, 'nn', STATE_PASSES) + n_all[0]

    tok = _spec((CHUNK, D_MODEL), lambda c: (c, 0))
    par = _spec((1, D_MODEL), lambda c: (0, 0))
    st = _spec((1, N_PAIR, PAIR, PAIR), lambda c: (c, 0, 0, 0))
    return pl.pallas_call(
        body, name="rwkv_scan_fwd", grid=(n_chunk,),
        in_specs=[tok] * 5 + [par] * 3, out_specs=[tok, tok, st, st, st, tok],
        out_shape=[jax.ShapeDtypeStruct((n_tok, D_MODEL), F32)] * 2
        + [jax.ShapeDtypeStruct((n_chunk, N_PAIR, PAIR, PAIR), F32), jax.ShapeDtypeStruct((n_chunk, N_PAIR, PAIR, PAIR), BF16),
           jax.ShapeDtypeStruct((n_chunk, N_PAIR, PAIR, PAIR), F32), jax.ShapeDtypeStruct((n_tok, D_MODEL), F32)],
        scratch_shapes=[pltpu.VMEM((N_PAIR, PAIR, PAIR), F32)],
        compiler_params=_cparams(1),
    )(r, lw, k, v, a, k_k, k_a, r_k)


def _rwkv_scan_bwd(r, lw, k, v, a, k_k, k_a, r_k, zs, invs, ms, q, d_o, d_bonus):
    n_tok = r.shape[0]
    n_chunk = n_tok // CHUNK

    def body(r_ref, lw_ref, k_ref, v_ref, a_ref, kk_ref, ka_ref, rk_ref, zs_ref, inv_ref, m_ref, q_ref, do_ref, db_ref,
             dr_ref, dlw_ref, dk_ref, dv_ref, da_ref, dkk_ref, dka_ref, drk_ref, g):
        sums = (dkk_ref, dka_ref, drk_ref)

        @pl.when(pl.program_id(0) == 0)
        def _():
            g[...] = jnp.zeros(g.shape, F32)
            for s_ref in sums:
                s_ref[...] = jnp.zeros(s_ref.shape, F32)

        prim = (r_ref[...], lw_ref[...], k_ref[...], v_ref[...], a_ref[...], kk_ref[...], ka_ref[...], rk_ref[...])
        known = inv_ref[0].astype(F32)
        _, chunk_vjp = jax.vjp(lambda *p: _chunk_fn(*p, inverse=known), *prim)
        m_all, q = m_ref[...], q_ref[...]
        gb, zb, d_out = g[...], zs_ref[0], do_ref[...]
        dob = _pairs_to_batch(d_out)
        d_m = _dot_passes(gb, zb, 'nt', STATE_PASSES)
        d_q = _batch_to_pairs(_dot_passes(dob, zb, 'nt', STATE_PASSES))
        grads = chunk_vjp((d_m[None], gb[None], d_q, d_out, db_ref[...]))
        for out_ref, val in zip((dr_ref, dlw_ref, dk_ref, dv_ref, da_ref), grads[:5]):
            out_ref[...] = val.astype(out_ref.dtype)
        for s_ref, val in zip(sums, grads[5:]):
            s_ref[...] += val
        same_head = (lax.broadcasted_iota(jnp.int32, (1, PAIR, PAIR), 1) // HEAD
                     == lax.broadcasted_iota(jnp.int32, (1, PAIR, PAIR), 2) // HEAD)
        g[...] = (_dot_passes(m_all[0], gb, 'tn', STATE_PASSES)
                  + jnp.where(same_head, _dot_passes(_pairs_to_batch(q), dob, 'tn', STATE_PASSES), 0.0))

    tok = _spec((CHUNK, D_MODEL), lambda c: (n_chunk - 1 - c, 0))
    par = _spec((1, D_MODEL), lambda c: (0, 0))
    st = _spec((1, N_PAIR, PAIR, PAIR), lambda c: (n_chunk - 1 - c, 0, 0, 0))
    tok_shape = lambda dt: jax.ShapeDtypeStruct((n_tok, D_MODEL), dt)
    return pl.pallas_call(
        body, name="rwkv_scan_bwd", grid=(n_chunk,),
        in_specs=[tok] * 5 + [par] * 3 + [st, st, st, tok, tok, tok], out_specs=[tok] * 5 + [par] * 3,
        out_shape=[tok_shape(BF16), tok_shape(F32), tok_shape(BF16), tok_shape(BF16), tok_shape(F32)]
        + [jax.ShapeDtypeStruct((1, D_MODEL), F32)] * 3,
        scratch_shapes=[pltpu.VMEM((N_PAIR, PAIR, PAIR), F32)],
        compiler_params=_cparams(1),
    )(r, lw, k, v, a, k_k, k_a, r_k, zs, invs, ms, q, d_o, d_bonus)


S5_TIME_TILE = 1024
S5_GROUP = 8


def _scan_rows(re_ref, im_ref, a1, carry, reverse):
    lb = re_ref.shape[1]
    grp, n_grp = S5_GROUP, re_ref.shape[0] // S5_GROUP

    def cmul(xr, xi, yr, yi):
        return xr * yr - xi * yi, xr * yi + xi * yr

    a2 = cmul(*a1, *a1)
    a4 = cmul(*a2, *a2)
    a8 = cmul(*a4, *a4)
    row = lax.broadcasted_iota(jnp.int32, (grp, lb), 0)
    expo = (grp - row) if reverse else (row + 1)
    pw = (jnp.ones((grp, lb), F32), jnp.zeros((grp, lb), F32))
    for bit, ap in ((1, a1), (2, a2), (4, a4), (8, a8)):
        nxt = cmul(*pw, *ap)
        sel = (expo & bit) != 0
        pw = (jnp.where(sel, nxt[0], pw[0]), jnp.where(sel, nxt[1], pw[1]))

    steps = []
    for d, ad in ((1, a1), (2, a2), (4, a4)):
        keep = (row < grp - d) if reverse else (row >= d)
        steps.append(((grp - d) if reverse else d, jnp.where(keep, ad[0], 0.0), jnp.where(keep, ad[1], 0.0)))

    def group(i, c):
        gi = (n_grp - 1 - i) if reverse else i
        rows = pl.ds(pl.multiple_of(gi * grp, grp), grp)
        xr, xi = re_ref[rows, :], im_ref[rows, :]
        for shift, mr, mi in steps:
            pr, pi = cmul(mr, mi, pltpu.roll(xr, shift, 0), pltpu.roll(xi, shift, 0))
            xr, xi = xr + pr, xi + pi
        cr, ci = cmul(*pw, c[0], c[1])
        xr, xi = xr + cr, xi + ci
        re_ref[rows, :] = xr
        im_ref[rows, :] = xi
        edge = slice(0, 1) if reverse else slice(grp - 1, grp)
        return xr[edge, :], xi[edge, :]

    return lax.fori_loop(0, n_grp, group, carry)


def _s5_fwd(u, b_re, b_im, a_re, a_im, c_re, c_im, d_row):
    n_tok = u.shape[0]
    tt = min(S5_TIME_TILE, n_tok)

    def body(u_ref, bre_ref, bim_ref, are_ref, aim_ref, cre_ref, cim_ref, d_ref, sre_ref, sim_ref, yg_ref, carry):
        @pl.when(pl.program_id(1) == 0)
        def _():
            carry[...] = jnp.zeros(carry.shape, F32)

        uv = u_ref[...]
        sre_ref[...], sim_ref[...] = _s5_in_fn(uv, bre_ref[...], bim_ref[...])
        c = _scan_rows(sre_ref, sim_ref, (are_ref[...], aim_ref[...]), (carry[0:1, :], carry[1:2, :]), False)
        carry[0:1, :] = c[0]
        carry[1:2, :] = c[1]
        (yg,) = _s5_out_fn(sre_ref[...], sim_ref[...], uv, cre_ref[...], cim_ref[...], d_ref[...])
        yg_ref[...] = yg.astype(yg_ref.dtype)

    u_blk = _spec((tt, SSM_BLK_IN), lambda l, t: (t, l))
    s_blk = _spec((tt, SSM_BLK_ST), lambda l, t: (t, l))
    blk3 = lambda arr: _spec((1,) + arr.shape[1:], lambda l, t: (l, 0, 0))
    par = lambda width: _spec((1, width), lambda l, t: (0, l))
    return pl.pallas_call(
        body, name="s5_fwd", grid=(N_SSM_BLK, n_tok // tt),
        in_specs=[u_blk, blk3(b_re), blk3(b_im), par(SSM_BLK_ST), par(SSM_BLK_ST), blk3(c_re), blk3(c_im), par(SSM_BLK_IN)],
        out_specs=[s_blk, s_blk, u_blk],
        out_shape=[jax.ShapeDtypeStruct((n_tok, SSM_LANES), F32)] * 2 + [jax.ShapeDtypeStruct((n_tok, D_MODEL), BF16)],
        scratch_shapes=[pltpu.VMEM((2, SSM_BLK_ST), F32)],
        compiler_params=_cparams(2),
    )(u, b_re, b_im, a_re, a_im, c_re, c_im, d_row)


def _s5_bwd(dyg, u, s_re, s_im, dh_res, b_re, b_im, a_re, a_im, c_re, c_im, d_row):
    n_tok = u.shape[0]
    tt = min(S5_TIME_TILE, n_tok)
    n_t = n_tok // tt

    def body(dyg_ref, u_ref, sre_ref, sim_ref, hre_ref, him_ref, res_ref, bre_ref, bim_ref, are_ref, aim_ref,
             cre_ref, cim_ref, d_ref, du_ref, dbre_ref, dbim_ref, dcre_ref, dcim_ref, dd_ref, dare_ref, daim_ref,
             carry, gre, gim):
        i = pl.program_id(1)
        sums = (dbre_ref, dbim_ref, dcre_ref, dcim_ref, dd_ref, dare_ref, daim_ref)

        @pl.when(i == 0)
        def _():
            carry[...] = jnp.zeros(carry.shape, F32)
            for r in sums:
                r[...] = jnp.zeros(r.shape, F32)

        uv, sre, sim = u_ref[...], sre_ref[...], sim_ref[...]
        _, out_vjp = jax.vjp(_s5_out_fn, sre, sim, uv, cre_ref[...], cim_ref[...], d_ref[...])
        gre[...], gim[...], du_out, dcre, dcim, dd = out_vjp((dyg_ref[...],))
        c = _scan_rows(gre, gim, (are_ref[...], -aim_ref[...]), (carry[0:1, :], carry[1:2, :]), True)
        carry[0:1, :] = c[0]
        carry[1:2, :] = c[1]
        g_re, g_im = gre[...], gim[...]
        at_start = i == n_t - 1
        p_re = _shift_down(sre, jnp.where(at_start, 0.0, hre_ref[7:8, :]))
        p_im = _shift_down(sim, jnp.where(at_start, 0.0, him_ref[7:8, :]))
        _, in_vjp = jax.vjp(_s5_in_fn, uv, bre_ref[...], bim_ref[...])
        du_in, dbre, dbim = in_vjp((g_re, g_im))
        du_ref[...] = du_out + du_in + res_ref[...]
        dare = jnp.sum(g_re * p_re + g_im * p_im, axis=0, keepdims=True)
        daim = jnp.sum(g_im * p_re - g_re * p_im, axis=0, keepdims=True)
        for r, val in zip(sums, (dbre, dbim, dcre, dcim, dd, dare, daim)):
            r[...] += val

    u_blk = _spec((tt, SSM_BLK_IN), lambda l, t: (n_t - 1 - t, l))
    s_blk = _spec((tt, SSM_BLK_ST), lambda l, t: (n_t - 1 - t, l))
    halo = _spec((8, SSM_BLK_ST), lambda l, t: (jnp.maximum((n_t - 1 - t) * (tt // 8) - 1, 0), l))
    blk3 = lambda arr: _spec((1,) + arr.shape[1:], lambda l, t: (l, 0, 0))
    par = lambda width: _spec((1, width), lambda l, t: (0, l))
    params = [blk3(b_re), blk3(b_im), par(SSM_BLK_ST), par(SSM_BLK_ST), blk3(c_re), blk3(c_im), par(SSM_BLK_IN)]
    shape = lambda arr: jax.ShapeDtypeStruct(arr.shape, F32)
    return pl.pallas_call(
        body, name="s5_bwd", grid=(N_SSM_BLK, n_t),
        in_specs=[u_blk, u_blk, s_blk, s_blk, halo, halo, u_blk] + params,
        out_specs=[u_blk, blk3(b_re), blk3(b_im), blk3(c_re), blk3(c_im), par(SSM_BLK_IN), par(SSM_BLK_ST), par(SSM_BLK_ST)],
        out_shape=[shape(u), shape(b_re), shape(b_im), shape(c_re), shape(c_im), shape(d_row), shape(a_re), shape(a_im)],
        scratch_shapes=[pltpu.VMEM((2, SSM_BLK_ST), F32), pltpu.VMEM((tt, SSM_BLK_ST), F32), pltpu.VMEM((tt, SSM_BLK_ST), F32)],
        compiler_params=_cparams(2),
    )(dyg, u, s_re, s_im, s_re, s_im, dh_res, b_re, b_im, a_re, a_im, c_re, c_im, d_row)


def _loss_call(h, target, tm):
    n_tok, d = h.shape

    def body(h_ref, t_ref, acc_ref, dh_ref):
        @pl.when(pl.program_id(0) == 0)
        def _():
            acc_ref[...] = jnp.zeros(acc_ref.shape, F32)

        e = h_ref[...] - t_ref[...]
        dh_ref[...] = e * (1.0 / d)
        acc_ref[...] += jnp.sum(jnp.sum(e * e, axis=1, keepdims=True), axis=0, keepdims=True)

    tok = _spec((tm, d), lambda i: (i, 0))
    return pl.pallas_call(
        body, name="loss", grid=(n_tok // tm,),
        in_specs=[tok, tok], out_specs=[_spec((8, 128), lambda i: (0, 0)), tok],
        out_shape=[jax.ShapeDtypeStruct((8, 128), F32), jax.ShapeDtypeStruct(h.shape, F32)],
        compiler_params=_cparams(1),
    )(h, target)


def _block_diag(t):
    nb, ng, rows, cols = t.shape
    eye = jnp.eye(ng, dtype=t.dtype)
    return jnp.einsum('bgrc,gh->bgrhc', t, eye).reshape(nb, ng * rows, ng * cols)


def _block_diag_t(t, rows, cols):
    nb, ng = t.shape[0], t.shape[1] // rows
    t5 = t.reshape(nb, ng, rows, ng, cols)
    return jnp.einsum('bgrhc,gh->bgrc', t5, jnp.eye(ng, dtype=t.dtype))


def _mlp_fwd(layer, h, w1g, w2g):
    pre = _matmul(f"mlp{layer}_up", h, w1g, 'nn', b_cols=(layer,), out_dtype=BF16, tm=WIDE_ROW_TILE)
    return pre, _matmul(f"mlp{layer}_down", pre, w2g, 'nn', lhs_fn=_relu2, b_rows=(layer,), out_dtype=BRANCH_DTYPE)


def _mlp_bwd(layer, h, pre, dy, dh_res, w1g, w2g):
    dpre = _matmul(f"mlp{layer}_down_dx", dy, w2g, 'nt', epi=lambda acc, p: acc * (2.0 * jnp.maximum(p, 0.0)),
                   extra=pre, b_rows=(layer,), out_dtype=BF16, tm=WIDE_ROW_TILE)
    dw2 = _matmul(f"mlp{layer}_down_dw", pre, dy, 'tn', lhs_fn=_relu2, out_dtype=GRAD_DTYPE)
    dw1 = _matmul(f"mlp{layer}_up_dw", h, dpre, 'tn', out_cols=True, out_dtype=GRAD_DTYPE)
    dh = _matmul(f"mlp{layer}_up_dx", dpre, w1g, 'nt', epi=lambda acc, e: acc + e, extra=dh_res, b_cols=(layer,))
    return dh, dw1, dw2


def _local_step(x, target, w, mid, start_late, exchange):
    n_tok = x.shape[0]
    tm = min(TOKEN_TILE, n_tok)
    n_tile, n_chunk = n_tok // tm, n_tok // CHUNK
    row = lambda v: v.reshape(1, -1)
    big = {}
    ln_g, ln_b = w['ln_g'], w['ln_b']
    grads = {}
    to_slots = lambda t: t.reshape(N_DEV, t.shape[0] // N_DEV, t.shape[1])

    halo_spec = (8, D_MODEL), lambda i: (jnp.maximum(i * (tm // 8) - 1, 0), 0)
    mix_p = [_tile(x, tm), (x,) + halo_spec, _full(w['rw_mu'][0]), _full(row(w['rw_w0'])), _full(w['rw_w1'][0]),
             _full(w['rw_w2'][0]), _full(row(w['rw_a0'])), _full(w['rw_a1'][0]), _full(w['rw_a2'][0]),
             _full(w['rw_g1'][0]), _full(w['rw_g2'][0])]
    tok_out = lambda dt: ((n_tok, D_MODEL), dt, (tm, D_MODEL), lambda i: (i, 0))
    xr, xk, xv, lw, a, g = _fwd_call("rwkv_mix", _mix_fn, (n_tile,), mix_p,
                                     [tok_out(BF16), tok_out(BF16), tok_out(BF16), tok_out(F32), tok_out(F32), tok_out(F32)])
    mid.forward(xr)
    square = lambda t: t.reshape(D_MODEL, D_MODEL)
    big.update({n: square(t) for n, t in mid.finish(xr).items()})
    late = start_late(big['rw_wr'])
    r = _matmul("rwkv_r", xr, big['rw_wr'], 'nn')
    k = _matmul("rwkv_k", xk, big['rw_wk'], 'nn')
    v = _matmul("rwkv_v", xv, big['rw_wv'], 'nn')

    scan_in = (r, lw, k, v, a, row(w['rw_k_k']) + late.zero, row(w['rw_k_a']), w['rw_r_k'].reshape(1, -1))
    o, bonus, *scan_saved = _rwkv_scan_fwd(*scan_in)

    zero = late.forward(o)
    tl = min(LN_TILE, n_tok)
    n_ln = n_tok // tl
    ln_out = lambda dt: ((n_tok, D_MODEL), dt, (tl, D_MODEL), lambda i: (i, 0))
    gate_p = [_tile(o, tl), _tile(bonus, tl), _tile(g, tl), _full(row(w['rw_lnx_g']) + zero), _full(row(w['rw_lnx_b']))]
    (og,) = _fwd_call("rwkv_gate", _gate_fn, (n_ln,), gate_p, [ln_out(BF16)])
    late_w = late.finish(og)
    big['rw_wo'] = square(late_w['rw_wo'])
    w1g, w2g, glu_g = late_w['mlp_w1'], late_w['mlp_w2'], late_w['s5_w_glu']
    y0 = _matmul("rwkv_o", og, big['rw_wo'], 'nn', out_dtype=BRANCH_DTYPE)
    ln0_p = [_tile(x, tl), _tile(y0, tl), _full(ln_g[0:1]), _full(ln_b[0:1])]
    twice = lambda fn: lambda *args: fn(*args) * 2
    h1, h1_b = _fwd_call("ln0", twice(_res_ln_fn), (n_ln,), ln0_p, [ln_out(F32), ln_out(BF16)])
    pre0, y1 = _mlp_fwd(0, h1_b, w1g, w2g)
    ln1_p = [_tile(h1, tl), _tile(y1, tl), _full(ln_g[1:2]), _full(ln_b[1:2])]
    (h2,) = _fwd_call("ln1", _res_ln_fn, (n_ln,), ln1_p, [ln_out(F32)])

    gp = (SSM_GROUPS, SSM_STATE)
    par_p = [_full(w['s5_a_re'][0]), _full(w['s5_a_im'][0]), _full(w['s5_log_dt'].reshape(SSM_GROUPS, 1))]
    s5_par_fn = lambda ar, ai, ld: _s5_param_fn(ar, ai, ld, None, None)
    gp_out = (gp, F32, gp, lambda i: (0, 0))
    abar_re, abar_im, coef_re, coef_im = _fwd_call("s5_param", s5_par_fn, (1,), par_p, [gp_out] * 4)
    b_flat = [w[n][0].reshape(SSM_LANES, SSM_GROUP) for n in ('s5_b_re', 's5_b_im')]
    bbar_p = [_full(coef_re.reshape(SSM_LANES, 1)), _full(coef_im.reshape(SSM_LANES, 1)), _full(b_flat[0]), _full(b_flat[1])]
    bb_out = ((SSM_LANES, SSM_GROUP), F32, (SSM_LANES, SSM_GROUP), lambda i: (0, 0))
    bbar_re, bbar_im = _fwd_call("s5_bbar", _s5_bbar_fn, (1,), bbar_p, [bb_out] * 2)
    to_in = lambda t: _block_diag(t.reshape(N_SSM_BLK, 8, SSM_STATE, SSM_GROUP).transpose(0, 1, 3, 2))
    to_out = lambda t: _block_diag(t.reshape(N_SSM_BLK, 8, SSM_GROUP, SSM_STATE).transpose(0, 1, 3, 2))
    bblk_re, bblk_im = to_in(bbar_re), to_in(bbar_im)
    cblk_re, cblk_im = to_out(w['s5_c_re'][0]), to_out(w['s5_c_im'][0])
    a_row_re, a_row_im = abar_re.reshape(1, SSM_LANES), abar_im.reshape(1, SSM_LANES)

    d_row = row(w['s5_d'])
    s5_params = (bblk_re, bblk_im, a_row_re, a_row_im, cblk_re, cblk_im, d_row)
    s_re, s_im, yg = _s5_fwd(h2, *s5_params)
    z = _matmul("s5_glu", yg, glu_g, 'nn', b_cols=(0,), out_dtype=BRANCH_DTYPE)
    ln2_p = [_tile(h2, tl), _tile(z, tl), _full(ln_g[2:3]), _full(ln_b[2:3])]
    h3, h3_b = _fwd_call("ln2", twice(_glu_ln_fn), (n_ln,), ln2_p, [ln_out(F32), ln_out(BF16)])
    pre1, y3 = _mlp_fwd(1, h3_b, w1g, w2g)
    ln3_p = [_tile(h3, tl), _tile(y3, tl), _full(ln_g[3:4]), _full(ln_b[3:4])]
    (h4,) = _fwd_call("ln3", _res_ln_fn, (n_ln,), ln3_p, [ln_out(F32)])

    loss_acc, dh4 = _loss_call(h4, target, tl)

    ln_w = ['tile', 'tile_bf16', 'acc', 'acc']
    dh3_res, dy3, dg3, db3 = _bwd_call("ln3_bwd", _res_ln_fn, (n_ln,), ln3_p, [_tile(dh4, tl)], ln_w)
    dh3, dw1_1, dw2_1 = _mlp_bwd(1, h3_b, pre1, dy3, dh3_res, w1g, w2g)
    zero = exchange([('mlp_w1', 1, dw1_1), ('mlp_w2', 1, to_slots(dw2_1))])
    ln2_p[2] = _full(ln_g[2:3] + zero)
    dh2_res, dz, dg2, db2 = _bwd_call("ln2_bwd", _glu_ln_fn, (n_ln,), ln2_p, [_tile(dh3, tl)], ln_w)
    dyg = _matmul("s5_glu_dx", dz, glu_g, 'nt', b_cols=(0,))
    dw_glu = _matmul("s5_glu_dw", yg, dz, 'tn', out_cols=True, out_dtype=GRAD_DTYPE)
    dh2, dbb_re, dbb_im, dcb_re, dcb_im, dd, dabar_re, dabar_im = _s5_bwd(dyg, h2, s_re, s_im, dh2_res, *s5_params)
    from_in = lambda t: _block_diag_t(t, SSM_GROUP, SSM_STATE).transpose(0, 1, 3, 2).reshape(SSM_LANES, SSM_GROUP)
    from_out = lambda t: _block_diag_t(t, SSM_STATE, SSM_GROUP).transpose(0, 1, 3, 2).reshape(1, SSM_GROUPS, SSM_GROUP, SSM_STATE)
    grads['s5_c_re'], grads['s5_c_im'] = from_out(dcb_re), from_out(dcb_im)
    grads['s5_d'] = dd
    dcoef_re, dcoef_im, db_re, db_im = _bwd_call(
        "s5_bbar_bwd", _s5_bbar_fn, (1,), bbar_p, [_full(from_in(dbb_re)), _full(from_in(dbb_im))], ['acc'] * 4)
    grads['s5_b_re'] = db_re.reshape(w['s5_b_re'].shape)
    grads['s5_b_im'] = db_im.reshape(w['s5_b_im'].shape)
    par_ct = [_full(dabar_re.reshape(gp)), _full(dabar_im.reshape(gp)), _full(dcoef_re.reshape(gp)), _full(dcoef_im.reshape(gp))]
    da_re, da_im, dlog_dt = _bwd_call("s5_param_bwd", s5_par_fn, (1,), par_p, par_ct, ['acc'] * 3)
    grads['s5_a_re'], grads['s5_a_im'], grads['s5_log_dt'] = da_re[None], da_im[None], dlog_dt.reshape(1, SSM_GROUPS)
    zero = exchange([(n, 0, grads[n]) for n in ('s5_a_re', 's5_a_im', 's5_log_dt', 's5_b_re', 's5_b_im', 's5_c_re', 's5_c_im')])
    ln1_p[2] = _full(ln_g[1:2] + zero)

    dh1_res, dy1, dg1, db1 = _bwd_call("ln1_bwd", _res_ln_fn, (n_ln,), ln1_p, [_tile(dh2, tl)], ln_w)
    dh1, dw1_0, dw2_0 = _mlp_bwd(0, h1_b, pre0, dy1, dh1_res, w1g, w2g)
    dx_res, dy0, dg0, db0 = _bwd_call("ln0_bwd", _res_ln_fn, (n_ln,), ln0_p, [_tile(dh1, tl)], ln_w)
    grads['ln_g'] = jnp.concatenate([dg0, dg1, dg2, dg3])
    grads['ln_b'] = jnp.concatenate([db0, db1, db2, db3])
    dog = _matmul("rwkv_o_dx", dy0, big['rw_wo'], 'nt')
    dw_o = to_slots(_matmul("rwkv_o_dw", og, dy0, 'tn', out_dtype=GRAD_DTYPE))
    zero = exchange([('s5_w_glu', 0, dw_glu), ('mlp_w1', 0, dw1_0), ('mlp_w2', 0, to_slots(dw2_0)), ('rw_wo', 0, dw_o)])
    gate_p[3] = _full(row(w['rw_lnx_g']) + zero)
    d_o, dbonus, dgate, grads['rw_lnx_g'], grads['rw_lnx_b'] = _bwd_call(
        "rwkv_gate_bwd", _gate_fn, (n_ln,), gate_p, [_tile(dog, tl)], ['tile', 'tile', 'tile', 'acc', 'acc'])
    dr, dlw, dk, dv, da, grads['rw_k_k'], grads['rw_k_a'], dr_k = _rwkv_scan_bwd(*scan_in, *scan_saved, d_o, dbonus)
    grads['rw_r_k'] = dr_k.reshape(w['rw_r_k'].shape)
    dw_r = to_slots(_matmul("rwkv_r_dw", xr, dr, 'tn', out_dtype=GRAD_DTYPE))
    dw_k = to_slots(_matmul("rwkv_k_dw", xk, dk, 'tn', out_dtype=GRAD_DTYPE))
    dw_v = to_slots(_matmul("rwkv_v_dw", xv, dv, 'tn', out_dtype=GRAD_DTYPE))
    done = ('ln_g', 'ln_b', 'rw_lnx_g', 'rw_lnx_b', 'rw_k_k', 'rw_k_a', 'rw_r_k')
    zero = exchange([('rw_wr', 0, dw_r), ('rw_wk', 0, dw_k), ('rw_wv', 0, dw_v)] + [(n, 0, grads[n]) for n in done])
    mix_p[3] = _full(row(w['rw_w0']) + zero)
    dxr = _matmul("rwkv_r_dx", dr, big['rw_wr'], 'nt')
    dxk = _matmul("rwkv_k_dx", dk, big['rw_wk'], 'nt')
    dxv = _matmul("rwkv_v_dx", dv, big['rw_wv'], 'nt')
    mix_ct = [_tile(t, tm) for t in (dxr, dxk, dxv, dlw, da, dgate)]
    halo_grad = ('tile', ((n_tile * 8, D_MODEL), (8, D_MODEL), lambda i: (i, 0)))
    res = _bwd_call("rwkv_mix_bwd", _mix_fn, (n_tile,), mix_p, mix_ct, ['tile', halo_grad] + ['acc'] * 9,
                    addends={0: _tile(dx_res, tm)})
    corr = res[1].reshape(n_tile, 8, D_MODEL)[1:, 7:8, :]
    corr = jnp.pad(corr, ((0, 1), (tm - 1, 0), (0, 0)))
    dx = (res[0].reshape(n_tile, tm, D_MODEL) + corr).reshape(n_tok, D_MODEL)
    (grads['rw_mu'], grads['rw_w0'], grads['rw_w1'], grads['rw_w2'], grads['rw_a0'], grads['rw_a1'], grads['rw_a2'],
     grads['rw_g1'], grads['rw_g2']) = [t[None] if t.shape[0] != 1 else t for t in res[2:]]
    return loss_acc[0, 0], dx, grads


def _full_shape(local_shape, axis):
    return local_shape[:axis] + (N_DEV * local_shape[axis],) + local_shape[axis + 1:]


def _split_shape(local_shape, axis):
    return local_shape[:axis] + (N_DEV, local_shape[axis]) + local_shape[axis + 1:]


def _mesh_pos():
    return lax.axis_index("x"), lax.axis_index("y"), lax.axis_index("c")


def _flip(v, f):
    return 1 - v if f else v


def _hbm_call(name, body, arrays, out_shapes, n_sems):
    n_t = len(arrays)
    hbm = pl.BlockSpec(memory_space=pl.ANY)
    return pl.pallas_call(
        body, name=name, out_shape=out_shapes, in_specs=[hbm] * n_t, out_specs=[hbm] * n_t,
        scratch_shapes=[pltpu.SemaphoreType.DMA((n_t, n_sems)), pltpu.SemaphoreType.DMA((n_t, n_sems)),
                        pltpu.SemaphoreType.DMA((n_t,))],
    )(*arrays)


def _all_gather(name, locals_):
    n_t = len(locals_)

    def body(*refs):
        x_refs, out_refs = refs[:n_t], refs[n_t:2 * n_t]
        send_sems, recv_sems, local_sems = refs[2 * n_t:]
        x, y, c = _mesh_pos()
        me, sibling = (x, y, c), (x, y, 1 - c)
        chips = [(1 - x, y), (x, 1 - y), (1 - x, 1 - y)]

        def copy(t, k, block, to, own=False):
            slot = out_refs[t].at[4 * block[0] + 2 * block[1] + block[2]]
            return pltpu.make_async_remote_copy(
                src_ref=x_refs[t] if own else slot, dst_ref=slot,
                send_sem=send_sems.at[t, k], recv_sem=recv_sems.at[t, k],
                device_id=to, device_id_type=pl.DeviceIdType.MESH)

        mine = [pltpu.make_async_copy(x_refs[t], out_refs[t].at[4 * x + 2 * y + c], local_sems.at[t]) for t in range(n_t)]
        for cp in mine:
            cp.start()
        sent = []
        for t in range(n_t):
            sent.append(copy(t, 0, me, sibling, own=True))
            sent += [copy(t, 1 + j, me, (*chip, c), own=True) for j, chip in enumerate(chips)]
        for cp in sent:
            cp.start()
        for j, chip in enumerate(chips):
            for t in range(n_t):
                copy(t, 1 + j, (*chip, c), me).wait_recv()
                passed = copy(t, 4 + j, (*chip, c), sibling)
                passed.start()
                sent.append(passed)
        for t in range(n_t):
            copy(t, 0, sibling, me).wait_recv()
            for j, chip in enumerate(chips):
                copy(t, 4 + j, (*chip, 1 - c), me).wait_recv()
        for cp in sent:
            cp.wait_send()
        for cp in mine:
            cp.wait()

    outs = [jax.ShapeDtypeStruct((N_DEV,) + a.shape, a.dtype) for a in locals_]
    return _hbm_call(name, body, locals_, outs, 7)


def _all_to_all(name, blocks):
    n_t = len(blocks)

    def body(*refs):
        x_refs, out_refs = refs[:n_t], refs[n_t:2 * n_t]
        send_sems, recv_sems, local_sems = refs[2 * n_t:]
        x, y, c = _mesh_pos()
        my_slot = 4 * x + 2 * y + c
        mine = [pltpu.make_async_copy(x_refs[t].at[my_slot], out_refs[t].at[my_slot], local_sems.at[t]) for t in range(n_t)]
        for cp in mine:
            cp.start()
        copies = []
        for k in range(1, N_DEV):
            peer = (_flip(x, k & 4), _flip(y, k & 2), _flip(c, k & 1))
            peer_slot = 4 * peer[0] + 2 * peer[1] + peer[2]
            for t in range(n_t):
                sems = dict(send_sem=send_sems.at[t, k - 1], recv_sem=recv_sems.at[t, k - 1],
                            device_id=peer, device_id_type=pl.DeviceIdType.MESH)
                cp = pltpu.make_async_remote_copy(src_ref=x_refs[t].at[peer_slot], dst_ref=out_refs[t].at[my_slot], **sems)
                cp.start()
                landed = pltpu.make_async_remote_copy(src_ref=x_refs[t].at[my_slot], dst_ref=out_refs[t].at[peer_slot], **sems)
                copies.append((cp, landed))
        for sent, landed in copies:
            landed.wait_recv()
            sent.wait_send()
        for cp in mine:
            cp.wait()

    outs = [jax.ShapeDtypeStruct(a.shape, a.dtype) for a in blocks]
    return _hbm_call(name, body, blocks, outs, 7)


_HBM = pl.BlockSpec(memory_space=pltpu.HBM)
_SEM = pl.BlockSpec(memory_space=pltpu.SEMAPHORE)
_EFFECT = pltpu.SideEffectType.DATAFLOW_SIDE_EFFECTING


def _peer_copies(x_refs, land_refs, send_sems, recv_sems, gather, mine):
    x, y, c = _mesh_pos()
    my_slot = 4 * x + 2 * y + c
    copies = []
    for k in range(1, N_DEV):
        peer = (_flip(x, k & 4), _flip(y, k & 2), _flip(c, k & 1))
        peer_slot = 4 * peer[0] + 2 * peer[1] + peer[2]
        for t, (x_ref, land) in enumerate(zip(x_refs, land_refs)):
            sem = t * (N_DEV - 1) + k - 1
            copies.append(pltpu.make_async_remote_copy(
                src_ref=x_ref if gather[t] else x_ref.at[peer_slot], dst_ref=land.at[my_slot if mine else peer_slot],
                send_sem=send_sems.at[sem], recv_sem=recv_sems.at[sem], device_id=peer, device_id_type=pl.DeviceIdType.MESH))
    return copies


def _split_start(name, held, plan, n_copies, after=None):
    n_h = len(held)

    n_in = n_h + (after is not None)

    def body(*refs):
        for sent in plan(refs[:n_h], refs[n_in], refs[n_in + 1], True):
            sent.start()
        refs[-1][...] = jnp.zeros((8, 128), F32)

    sems = pltpu.SemaphoreType.DMA((n_copies,))
    out = pl.pallas_call(
        body, name=name,
        out_shape=(sems, sems, *[pltpu.HBM(a.shape, a.dtype) for a in held], jax.ShapeDtypeStruct((8, 128), F32)),
        in_specs=[_HBM] * n_h + [pl.BlockSpec(memory_space=pl.ANY)] * (after is not None),
        out_specs=(_SEM, _SEM, *[_HBM] * n_h, pl.BlockSpec(memory_space=pltpu.VMEM)),
        input_output_aliases={i: 2 + i for i in range(n_h)},
        compiler_params=pltpu.CompilerParams(has_side_effects=_EFFECT),
    )(*[pltpu.with_memory_space_constraint(a, pltpu.HBM) for a in held], *([after] if after is not None else []))
    return dict(send=out[0], recv=out[1], held=out[2:2 + n_h], zero=out[-1][0, 0], plan=plan)


def _split_wait(name, started, after):
    held, plan = started['held'], started['plan']
    n_h = len(held)

    def body(*refs):
        for sent in plan(refs[:n_h], refs[n_h], refs[n_h + 1], True):
            sent.wait_send()
        for landed in plan(refs[:n_h], refs[n_h], refs[n_h + 1], False):
            landed.wait_recv()

    return pl.pallas_call(
        body, name=name,
        out_shape=[pltpu.HBM(a.shape, a.dtype) for a in held],
        in_specs=[_HBM] * n_h + [_SEM, _SEM, pl.BlockSpec(memory_space=pl.ANY)],
        out_specs=[_HBM] * n_h,
        input_output_aliases={i: i for i in range(n_h)},
        compiler_params=pltpu.CompilerParams(has_side_effects=_EFFECT),
    )(*held, started['send'], started['recv'], after)


def _push_start(name, arrays, gather):
    n_t = len(arrays)
    lands = [lax.empty((N_DEV,) + a.shape if whole else a.shape, a.dtype) for a, whole in zip(arrays, gather)]
    plan = lambda refs, send, recv, mine: _peer_copies(refs[:n_t], refs[n_t:], send, recv, gather, mine)
    return _split_start(name, list(arrays) + lands, plan, n_t * (N_DEV - 1))


def _push_wait(name, started, after):
    held = _split_wait(name, started, after)
    return held[:len(held) // 2], held[len(held) // 2:]


def _chip_copies(refs, send_sems, recv_sems, mine):
    n_t = len(refs) // 2
    x, y, c = _mesh_pos()
    my_slot = 4 * x + 2 * y + c
    peers = [(x, y, 1 - c), (1 - x, y, c), (x, 1 - y, c), (1 - x, 1 - y, c)]
    copies = []
    for j, peer in enumerate(peers):
        peer_slot = 4 * peer[0] + 2 * peer[1] + peer[2]
        for t in range(n_t):
            copies.append(pltpu.make_async_remote_copy(
                src_ref=refs[t], dst_ref=refs[n_t + t].at[my_slot if mine else peer_slot],
                send_sem=send_sems.at[4 * t + j], recv_sem=recv_sems.at[4 * t + j],
                device_id=peer, device_id_type=pl.DeviceIdType.MESH))
    return copies


def _forward_copies(refs, send_sems, recv_sems, mine):
    x, y, c = _mesh_pos()
    sibling = (x, y, 1 - c)
    copies = []
    for j, (px, py) in enumerate([(1 - x, y), (x, 1 - y), (1 - x, 1 - y)]):
        slot = 4 * px + 2 * py + (c if mine else 1 - c)
        for t, land in enumerate(refs):
            copies.append(pltpu.make_async_remote_copy(
                src_ref=land.at[slot], dst_ref=land.at[slot], send_sem=send_sems.at[3 * t + j], recv_sem=recv_sems.at[3 * t + j],
                device_id=sibling, device_id_type=pl.DeviceIdType.MESH))
    return copies


class _TwoStageGather:
    def __init__(self, name, arrays, my_slot, after=None):
        self.name, self.arrays, self.my_slot = name, arrays, my_slot
        values = list(arrays.values())
        lands = [lax.empty((N_DEV,) + a.shape, a.dtype) for a in values]
        self.stage = _split_start(name + "_start", values + lands, _chip_copies, 4 * len(values), after)
        self.zero = self.stage['zero']

    def forward(self, after):
        held = _split_wait(self.name + "_wait", self.stage, after)
        self.stage = _split_start(self.name + "_forward_start", held[len(self.arrays):], _forward_copies, 3 * len(self.arrays))
        return self.stage['zero']

    def finish(self, after):
        lands = _split_wait(self.name + "_forward_wait", self.stage, after)
        return {n: lax.dynamic_update_slice(land, a[None], (self.my_slot,) + (0,) * a.ndim)
                for (n, a), land in zip(self.arrays.items(), lands)}


ADAM_BLOCK = 128 * 1024


def _adam_update(part, w_ref, m_ref, v_ref, g_ref, d_ref, nm_ref, nv_ref):
    g = part(0)
    for i in range(1, N_DEV):
        g = g + part(i)
    m_new = ADAM_B1 * m_ref[...] + (1.0 - ADAM_B1) * g
    v_new = ADAM_B2 * v_ref[...] + (1.0 - ADAM_B2) * (g * g)
    m_hat = m_new * (1.0 / (1.0 - ADAM_B1 ** ADAM_STEP))
    v_hat = v_new * (1.0 / (1.0 - ADAM_B2 ** ADAM_STEP))
    g_ref[...] = g
    nm_ref[...] = m_new
    nv_ref[...] = v_new
    d_ref[...] = -ADAM_LR * (m_hat / (jnp.sqrt(v_hat) + ADAM_EPS) + ADAM_WD * w_ref[...])


def _adam_many(name, items):
    n_t = len(items)
    n_in = [4 if own is None else 5 for *_, own in items]
    first = [sum(n_in[:t]) for t in range(n_t)]

    def body(*refs):
        ins, outs = refs[:sum(n_in)], refs[sum(n_in):]
        x, y, c = _mesh_pos()
        my_slot = 4 * x + 2 * y + c
        for t in range(n_t):
            s_ref, w_ref, m_ref, v_ref = ins[first[t]:first[t] + 4]
            if n_in[t] == 4:
                part = lambda i: s_ref[i]
            else:
                own_ref = ins[first[t] + 4]
                part = lambda i: jnp.where(my_slot == i, own_ref[...], s_ref[i])
            _adam_update(part, w_ref, m_ref, v_ref, *outs[4 * t:4 * t + 4])

    vmem = pl.BlockSpec(memory_space=pltpu.VMEM)
    out = pl.pallas_call(
        body, name=name, in_specs=[vmem] * sum(n_in), out_specs=[vmem] * (4 * n_t),
        out_shape=[jax.ShapeDtypeStruct(item[1].shape, F32) for item in items for _ in range(4)],
        compiler_params=pltpu.CompilerParams(vmem_limit_bytes=VMEM_LIMIT),
    )(*[a for item in items for a in item if a is not None])
    return [out[4 * t:4 * t + 4] for t in range(n_t)]


def _adam_call(name, slots, w, m, v, own=None):
    n_layer, rows, width = w.shape
    tile = min(rows, ADAM_BLOCK // max(width, 128))

    def body(*refs):
        s_refs = refs[:n_layer]
        own_refs = refs[n_layer:2 * n_layer] if own is not None else None
        w_ref, m_ref, v_ref = refs[-7:-4]
        layer = pl.program_id(0)
        x, y, c = _mesh_pos()
        my_slot = 4 * x + 2 * y + c
        for l in range(n_layer):
            @pl.when(layer == l)
            def _(l=l):
                if own is None:
                    part = lambda i: s_refs[l][i].astype(F32)
                else:
                    part = lambda i: jnp.where(my_slot == i, own_refs[l][...], s_refs[l][i]).astype(F32)
                _adam_update(part, w_ref, m_ref, v_ref, *refs[-4:])

    slot_spec = lambda k: _spec((N_DEV, tile, width), lambda l, i: (0, jnp.where(l == k, i, 0), 0))
    own_spec = lambda k: _spec((tile, width), lambda l, i: (jnp.where(l == k, i, 0), 0))
    blk = _spec((None, tile, width), lambda l, i: (l, i, 0))
    specs = [slot_spec(k) for k in range(n_layer)] + ([own_spec(k) for k in range(n_layer)] if own is not None else [])
    return pl.pallas_call(
        body, name=name, grid=(n_layer, rows // tile),
        in_specs=specs + [blk] * 3, out_specs=[blk] * 4, out_shape=[jax.ShapeDtypeStruct(w.shape, F32)] * 4,
        compiler_params=_cparams(2),
    )(*slots, *(own or []), w, m, v)


def kernel(x, ln_g, ln_b, rw_mu, rw_w0, rw_w1, rw_w2, rw_a0, rw_a1, rw_a2, rw_g1, rw_g2, rw_k_k, rw_k_a, rw_r_k, rw_wr, rw_wk, rw_wv, rw_wo, rw_lnx_g, rw_lnx_b, s5_a_re, s5_a_im, s5_log_dt, s5_b_re, s5_b_im, s5_c_re, s5_c_im, s5_d, s5_w_glu, mlp_w1, mlp_w2, loss_target, m_ln_g, m_ln_b, m_rw_mu, m_rw_w0, m_rw_w1, m_rw_w2, m_rw_a0, m_rw_a1, m_rw_a2, m_rw_g1, m_rw_g2, m_rw_k_k, m_rw_k_a, m_rw_r_k, m_rw_wr, m_rw_wk, m_rw_wv, m_rw_wo, m_rw_lnx_g, m_rw_lnx_b, m_s5_a_re, m_s5_a_im, m_s5_log_dt, m_s5_b_re, m_s5_b_im, m_s5_c_re, m_s5_c_im, m_s5_d, m_s5_w_glu, m_mlp_w1, m_mlp_w2, v_ln_g, v_ln_b, v_rw_mu, v_rw_w0, v_rw_w1, v_rw_w2, v_rw_a0, v_rw_a1, v_rw_a2, v_rw_g1, v_rw_g2, v_rw_k_k, v_rw_k_a, v_rw_r_k, v_rw_wr, v_rw_wk, v_rw_wv, v_rw_wo, v_rw_lnx_g, v_rw_lnx_b, v_s5_a_re, v_s5_a_im, v_s5_log_dt, v_s5_b_re, v_s5_b_im, v_s5_c_re, v_s5_c_im, v_s5_d, v_s5_w_glu, v_mlp_w1, v_mlp_w2):
    given = dict(locals())
    local_w = {n: given[n] for n in WEIGHTS}
    local_m = {n: given["m_" + n] for n in WEIGHTS}
    local_v = {n: given["v_" + n] for n in WEIGHTS}
    small = [n for n in WEIGHTS if n in SHARDED and n not in BIG]

    my_slot = 4 * lax.axis_index("x") + 2 * lax.axis_index("y") + lax.axis_index("c")
    as_bf16 = {n: local_w[n].astype(BF16) for n in BIG}

    gathered = _all_gather("weights_all_gather", [local_w[n] for n in small])
    full = dict(local_w)
    for n, blk in zip(small, gathered):
        full[n] = jnp.moveaxis(blk, 0, SHARDED[n]).reshape(_full_shape(local_w[n].shape, SHARDED[n]))
    mid = _TwoStageGather("weights_mid", {n: as_bf16[n] for n in MID}, my_slot)
    late = lambda after: _TwoStageGather("weights_late", {n: as_bf16[n] for n in LATE}, my_slot, after)
    full['rw_w0'] = full['rw_w0'] + mid.zero

    exchanges = []

    def exchange(entries):
        started = _push_start(f"grads_start_{len(exchanges)}", [t for _, _, t in entries], [n in REPLICATED for n, _, _ in entries])
        exchanges.append((entries, started))
        return started['zero']

    loss_sq, dx, grads = _local_step(x[0], loss_target[0], full, mid, late, exchange)
    loss = (0.5 / D_MODEL) * lax.psum(loss_sq, MESH_AXES)

    wide = {'s5_b_re': (SSM_LANES, SSM_GROUP), 's5_b_im': (SSM_LANES, SSM_GROUP),
            's5_c_re': (D_MODEL, SSM_STATE), 's5_c_im': (D_MODEL, SSM_STATE)}
    by_name, many = {}, []
    landed = {n: {} for n in BIG}
    for e, (entries, started) in enumerate(exchanges):
        sources, lands = _push_wait(f"grads_wait_{e}", started, dx)
        for (n, i, _), src, slots in zip(entries, sources, lands):
            if n in wide:
                as_view = lambda t: t.reshape((1,) + wide[n])
                out = _adam_call(f"adamw_{n}", [slots.reshape((N_DEV,) + wide[n])], as_view(local_w[n]), as_view(local_m[n]),
                                 as_view(local_v[n]), own=[src.reshape(wide[n])])
                by_name[n] = [t.reshape(local_w[n].shape) for t in out]
            elif n in REPLICATED:
                many.append((n, (slots, local_w[n], local_m[n], local_v[n], src)))
            else:
                landed[n][i] = (slots, lax.dynamic_index_in_dim(src, my_slot, 0, keepdims=False))
    for n in BIG:
        layers = [landed[n][i] for i in sorted(landed[n])]
        by_name[n] = _adam_call(f"adamw_{n}", [s for s, _ in layers], local_w[n], local_m[n], local_v[n], own=[o for _, o in layers])
    split = lambda n: jnp.moveaxis(grads[n].reshape(_split_shape(local_w[n].shape, SHARDED[n])), SHARDED[n], 0)
    for n, slots in zip(small, _all_to_all("grads_all_to_all", [split(n) for n in small])):
        many.append((n, (slots, local_w[n], local_m[n], local_v[n], None)))
    last = [n for n in REPLICATED if n not in by_name and n not in dict(many)]
    for n, slots in zip(last, _all_gather("grads_all_gather", [grads[n] for n in last])):
        many.append((n, (slots, local_w[n], local_m[n], local_v[n], None)))
    for (n, _), out in zip(many, _adam_many("adamw_small", [item for _, item in many])):
        by_name[n] = out
    results = [by_name[n][k] for k in range(4) for n in WEIGHTS]
    return (loss, dx[None], *results)
```

```python
import functools
import math

import jax
import jax.numpy as jnp
from jax import lax
from jax.experimental import pallas as pl
from jax.experimental.pallas import tpu as pltpu

F32 = jnp.float32
BF16 = jnp.bfloat16

D_MODEL = 1024
HEAD = 64
PAIR = 2 * HEAD
N_PAIR = D_MODEL // PAIR
CHUNK = 64
GN_EPS = 64e-5
LN_EPS = 1e-5
SSM_GROUP = 16
SSM_STATE = 64
SSM_GROUPS = D_MODEL // SSM_GROUP
SSM_LANES = SSM_GROUPS * SSM_STATE
SSM_BLK_IN = 128
SSM_BLK_ST = 512
N_SSM_BLK = D_MODEL // SSM_BLK_IN
DEPTH = 2
DN_ALPHA = (2.0 * DEPTH) ** 0.25
ADAM_LR, ADAM_B1, ADAM_B2, ADAM_EPS, ADAM_WD, ADAM_STEP = 0.001, 0.9, 0.999, 1e-08, 0.01, 10
N_DEV = 8
MESH_AXES = ("x", "y", "c")
VMEM_LIMIT = 56 * 1024 * 1024
TOKEN_TILE = 256
LN_TILE = 512
WIDE_ROW_TILE = 2048

SHARDED = {
    "rw_mu": 2, "rw_w1": 1, "rw_w2": 2, "rw_a1": 1, "rw_a2": 2, "rw_g1": 1, "rw_g2": 2,
    "rw_wr": 1, "rw_wk": 1, "rw_wv": 1, "rw_wo": 1, "s5_d": 1, "s5_w_glu": 2, "mlp_w1": 2, "mlp_w2": 1,
}
WEIGHTS = ['ln_g', 'ln_b', 'rw_mu', 'rw_w0', 'rw_w1', 'rw_w2', 'rw_a0', 'rw_a1', 'rw_a2', 'rw_g1', 'rw_g2', 'rw_k_k',
           'rw_k_a', 'rw_r_k', 'rw_wr', 'rw_wk', 'rw_wv', 'rw_wo', 'rw_lnx_g', 'rw_lnx_b', 's5_a_re', 's5_a_im',
           's5_log_dt', 's5_b_re', 's5_b_im', 's5_c_re', 's5_c_im', 's5_d', 's5_w_glu', 'mlp_w1', 'mlp_w2']
REPLICATED = [n for n in WEIGHTS if n not in SHARDED]
BIG = ['rw_wr', 'rw_wk', 'rw_wv', 'rw_wo', 's5_w_glu', 'mlp_w1', 'mlp_w2']
MID = ['rw_wr', 'rw_wk', 'rw_wv']
LATE = ['rw_wo', 's5_w_glu', 'mlp_w1', 'mlp_w2']
BRANCH_DTYPE = BF16
GRAD_DTYPE = BF16


def _cparams(n_grid):
    return pltpu.CompilerParams(dimension_semantics=("arbitrary",) * n_grid, vmem_limit_bytes=VMEM_LIMIT)


@jax.custom_vjp
def _mm(x, w):
    return jnp.dot(x.astype(BF16), w.astype(BF16), preferred_element_type=F32)


def _mm_fwd(x, w):
    return _mm(x, w), (x, w)


def _mm_bwd(res, dy):
    x, w = res
    dyb = dy.astype(BF16)
    dx = lax.dot_general(dyb, w.astype(BF16), (((1,), (1,)), ((), ())), preferred_element_type=F32)
    dw = lax.dot_general(x.astype(BF16), dyb, (((0,), (0,)), ((), ())), preferred_element_type=F32)
    return dx, dw


_mm.defvjp(_mm_fwd, _mm_bwd)


_DOT_DIMS = {'nn': (((1,), (0,)), ((), ())), 'nt': (((1,), (1,)), ((), ())), 'tn': (((0,), (0,)), ((), ()))}
_BATCH_DOT_DIMS = {'nn': (((2,), (1,)), ((0,), (0,))), 'nt': (((2,), (2,)), ((0,), (0,))), 'tn': (((1,), (1,)), ((0,), (0,)))}
CHUNK_PASSES = 1


def _split_bf16(a):
    hi = a.astype(BF16)
    return hi, (a - hi.astype(F32)).astype(BF16)


def _dot_passes(a, b, mode, passes=None):
    dims = (_DOT_DIMS if a.ndim == 2 else _BATCH_DOT_DIMS)[mode]
    dot = lambda p, q: lax.dot_general(p, q, dims, preferred_element_type=F32)
    if (CHUNK_PASSES if passes is None else passes) == 1:
        return dot(a.astype(BF16), b.astype(BF16))
    (ah, al), (bh, bl) = _split_bf16(a), _split_bf16(b)
    return dot(ah, bh) + (dot(ah, bl) + dot(al, bh))


@functools.partial(jax.custom_vjp, nondiff_argnums=(2,))
def _pdot(a, b, mode):
    return _dot_passes(a, b, mode)


def _pdot_fwd(a, b, mode):
    return _dot_passes(a, b, mode), (a, b)


def _pdot_bwd(mode, res, dy):
    a, b = res
    if mode == 'nn':
        return _dot_passes(dy, b, 'nt'), _dot_passes(a, dy, 'tn')
    if mode == 'nt':
        return _dot_passes(dy, b, 'nn'), _dot_passes(dy, a, 'tn')
    return _dot_passes(b, dy, 'nt'), _dot_passes(a, dy, 'nn')


_pdot.defvjp(_pdot_fwd, _pdot_bwd)


def _tri_sum(x, upper):
    nb, n = x.shape[0], x.shape[1]
    ti = lax.broadcasted_iota(jnp.int32, (nb, n, n), 1)
    tj = lax.broadcasted_iota(jnp.int32, (nb, n, n), 2)
    tri = ((ti <= tj) if upper else (ti >= tj)).astype(BF16)
    hi = x.astype(BF16)
    rest = x - hi.astype(F32)
    mid = rest.astype(BF16)
    lo = (rest - mid.astype(F32)).astype(BF16)
    dot = lambda q: lax.dot_general(tri, q, _BATCH_DOT_DIMS['nn'], preferred_element_type=F32)
    return dot(hi) + (dot(mid) + dot(lo))


@jax.custom_vjp
def _cumsum_rows(x):
    return _tri_sum(x, False)


_cumsum_rows.defvjp(lambda x: (_tri_sum(x, False), None), lambda _, dy: (_tri_sum(dy, True),))


def _power_sum(a, order):
    m = a.shape[-1]
    ti = lax.broadcasted_iota(jnp.int32, (1, m, m), 1)
    tj = lax.broadcasted_iota(jnp.int32, (1, m, m), 2)
    pw, total = _dot_passes(a, a, 'nn'), (ti == tj).astype(F32) + a
    for _ in range(int(math.log2(order)) - 2):
        both = _dot_passes(pw, jnp.concatenate([pw, total], axis=2), 'nn')
        pw, total = both[:, :, :m], total + both[:, :, m:]
    return total + _dot_passes(pw, total, 'nn')


@functools.partial(jax.custom_vjp, nondiff_argnums=(1,))
def _nilpotent_inverse(a, order):
    return _power_sum(a, order)


def _nilpotent_inverse_fwd(a, order):
    inv = _power_sum(a, order)
    return inv, inv


def _nilpotent_inverse_bwd(order, inv, g):
    return (_dot_passes(_dot_passes(inv, g, 'tn'), inv, 'nt'),)


_nilpotent_inverse.defvjp(_nilpotent_inverse_fwd, _nilpotent_inverse_bwd)


@jax.custom_vjp
def _known_inverse(a, inv):
    return inv


_known_inverse.defvjp(lambda a, inv: (inv, inv),
                      lambda inv, g: (_dot_passes(_dot_passes(inv, g, 'tn'), inv, 'nt'), jnp.zeros_like(inv)))


@functools.partial(jax.custom_vjp, nondiff_argnums=(1,))
def _roll_rows(x, shift):
    return pltpu.roll(x, shift, 0)


def _roll_rows_fwd(x, shift):
    return pltpu.roll(x, shift, 0), None


def _roll_rows_bwd(shift, _, dy):
    n = dy.shape[0]
    return (pltpu.roll(dy, (n - shift) % n, 0),)


_roll_rows.defvjp(_roll_rows_fwd, _roll_rows_bwd)


def _shift_down(x, first_row):
    row = lax.broadcasted_iota(jnp.int32, x.shape, 0)
    return jnp.where(row == 0, first_row, _roll_rows(x, 1))


def _sigmoid(x):
    return 1.0 / (1.0 + jnp.exp(-x))


def _softplus(x):
    return jnp.maximum(x, 0.0) + jnp.log(1.0 + jnp.exp(-jnp.abs(x)))


def _gelu(x):
    return 0.5 * x * (1.0 + jnp.tanh(math.sqrt(2.0 / math.pi) * (x + 0.044715 * (x * x * x))))


def _layer_norm(x, g, b):
    mu = jnp.mean(x, axis=-1, keepdims=True)
    xc = x - mu
    var = jnp.mean(xc * xc, axis=-1, keepdims=True)
    return xc * lax.rsqrt(var + LN_EPS) * g + b


def _spec(block, index_map):
    return pl.BlockSpec(block, index_map)


def _tile(arr, tm):
    return (arr, (tm, arr.shape[1]), lambda i: (i, 0))


def _full(arr):
    nd = arr.ndim
    return (arr, arr.shape, lambda *_: (0,) * nd)


def _fwd_call(name, fn, grid, prims, outs):
    n_in = len(prims)

    def body(*refs):
        vals = [r[...] for r in refs[:n_in]]
        vals = [v.astype(F32) if v.dtype != F32 else v for v in vals]
        res = fn(*vals)
        for o, r in zip(refs[n_in:], res):
            o[...] = r.astype(o.dtype)

    return pl.pallas_call(
        body, name=name, grid=grid,
        in_specs=[_spec(b, m) for _, b, m in prims],
        out_specs=[_spec(b, m) for _, _, b, m in outs],
        out_shape=[jax.ShapeDtypeStruct(s, d) for s, d, _, _ in outs],
        compiler_params=_cparams(len(grid)),
    )(*[a for a, _, _ in prims])


def _bwd_call(name, fn, grid, prims, cts, wants, addends=None):
    addends = addends or {}
    n_in, n_ct = len(prims), len(cts)
    out_place = {}
    wants, out_dtype = list(wants), {}
    for i, w in enumerate(wants):
        if isinstance(w, tuple):
            wants[i], out_place[i] = w
        elif w == 'tile_bf16':
            wants[i], out_dtype[i] = 'tile', BF16
    diff = [i for i, w in enumerate(wants) if w]
    add_idx = [i for i in diff if i in addends]
    last_axis = len(grid) - 1

    def body(*refs):
        vals = [r[...] for r in refs[:n_in]]
        vals = [v.astype(F32) if v.dtype != F32 else v for v in vals]
        ct_vals = [r[...] for r in refs[n_in:n_in + n_ct]]
        ct_vals = [v.astype(F32) if v.dtype != F32 else v for v in ct_vals]
        add_refs = dict(zip(add_idx, refs[n_in + n_ct:n_in + n_ct + len(add_idx)]))
        out_refs = refs[n_in + n_ct + len(add_idx):]

        def f(*dargs):
            full = list(vals)
            for i, a in zip(diff, dargs):
                full[i] = a
            return tuple(fn(*full))

        _, vjp = jax.vjp(f, *[vals[i] for i in diff])
        grads = vjp(tuple(ct_vals))
        first = pl.program_id(last_axis) == 0
        for o, g, i in zip(out_refs, grads, diff):
            if wants[i] == 'tile':
                if i in add_refs:
                    g = g + add_refs[i][...]
                o[...] = g.astype(o.dtype)
            else:
                @pl.when(first)
                def _(o=o):
                    o[...] = jnp.zeros(o.shape, o.dtype)
                o[...] += g

    ins = list(prims) + list(cts) + [addends[i] for i in add_idx]
    places = [out_place.get(i, (prims[i][0].shape, prims[i][1], prims[i][2])) for i in diff]
    return pl.pallas_call(
        body, name=name, grid=grid,
        in_specs=[_spec(b, m) for _, b, m in ins],
        out_specs=[_spec(b, m) for _, b, m in places],
        out_shape=[jax.ShapeDtypeStruct(s, out_dtype.get(i, F32)) for i, (s, _, _) in zip(diff, places)],
        compiler_params=_cparams(len(grid)),
    )(*[a for a, _, _ in ins])


def _matmul(name, a, b, mode, *, out_dtype=F32, lhs_fn=None, epi=None, extra=None, tm=1024, tn=1024, tk=1024,
            b_cols=None, b_rows=None, out_cols=False):
    if mode == 'tn':
        kdim, m = a.shape
    else:
        m, kdim = a.shape
    if b_rows is not None:
        loc, cols = b.shape[-2], b.shape[-1]
        if mode == 'nn':
            n, k_shards = cols, max(1, tk // loc)
            tk = k_shards * loc
        else:
            n, tn = N_DEV * loc, loc
    elif b_cols is None:
        n = b.shape[0] if mode == 'nt' else b.shape[1]
    else:
        rows, loc = b.shape[-2], b.shape[-1]
        if mode == 'nn':
            n, n_shards = N_DEV * loc, max(1, tn // loc)
            tn = n_shards * loc
        else:
            n, k_shards = rows, max(1, tk // loc)
            tk = k_shards * loc
    if out_cols:
        o_shards = max(1, min(tn, n) // (n // N_DEV))
        tn = o_shards * (n // N_DEV)
    tm, tn, tk = min(tm, m), min(tn, n), min(tk, kdim)
    nk = kdim // tk
    grid = (m // tm, n // tn, nk)
    a_spec = _spec((tk, tm), lambda i, j, k: (k, i)) if mode == 'tn' else _spec((tm, tk), lambda i, j, k: (i, k))
    if b_rows is not None:
        lead = tuple(b_rows)
        skip = (None,) * (1 + len(lead))
        if mode == 'nn':
            b_spec = _spec((k_shards,) + skip[1:] + (tk // k_shards, tn), lambda i, j, k: (k,) + lead + (0, j))
        else:
            b_spec = _spec(skip + (tn, tk), lambda i, j, k: (j,) + lead + (0, k))
    elif b_cols is None:
        b_spec = _spec((tn, tk), lambda i, j, k: (j, k)) if mode == 'nt' else _spec((tk, tn), lambda i, j, k: (k, j))
    else:
        lead = tuple(b_cols)
        skip = (None,) * (1 + len(lead))
        if mode == 'nn':
            b_spec = _spec((n_shards,) + skip[1:] + (tk, tn // n_shards), lambda i, j, k: (j,) + lead + (k, 0))
        else:
            b_spec = _spec((k_shards,) + skip[1:] + (tn, tk // k_shards), lambda i, j, k: (k,) + lead + (j, 0))
    o_spec = _spec((tm, tn), lambda i, j, k: (i, j))
    if out_cols:
        o_place, o_shape = _spec((o_shards, tm, tn // o_shards), lambda i, j, k: (j, i, 0)), (N_DEV, m, n // N_DEV)
    else:
        o_place, o_shape = o_spec, (m, n)
    dims = _DOT_DIMS[mode]

    def body(*refs):
        if extra is None:
            a_ref, b_ref, o_ref, acc = refs
            x_ref = None
        else:
            a_ref, b_ref, x_ref, o_ref, acc = refs
        k = pl.program_id(2)
        av = a_ref[...]
        if lhs_fn is not None:
            av = lhs_fn(av.astype(F32))
        bv = b_ref[...]
        if bv.ndim == 3:
            bv = bv.reshape(tk, tn) if b_rows is not None else jnp.concatenate([bv[g] for g in range(bv.shape[0])], axis=1)
        part = lax.dot_general(av.astype(BF16), bv.astype(BF16), dims, preferred_element_type=F32)

        def finish(r):
            if epi is not None:
                r = epi(r, x_ref[...])
            if out_cols:
                width = tn // o_shards
                for g in range(o_shards):
                    o_ref[g] = r[:, g * width:(g + 1) * width].astype(o_ref.dtype)
            else:
                o_ref[...] = r.astype(o_ref.dtype)

        if nk == 1:
            finish(part)
            return

        @pl.when(k == 0)
        def _():
            acc[...] = part

        @pl.when((k > 0) & (k < nk - 1))
        def _():
            acc[...] += part

        @pl.when(k == nk - 1)
        def _():
            finish(acc[...] + part)

    ops, specs = [a, b], [a_spec, b_spec]
    if extra is not None:
        ops.append(extra)
        specs.append(o_spec)
    return pl.pallas_call(
        body, name=name, grid=grid, in_specs=specs, out_specs=o_place,
        out_shape=jax.ShapeDtypeStruct(o_shape, out_dtype),
        scratch_shapes=[pltpu.VMEM((tm, tn), F32)],
        compiler_params=_cparams(3),
    )(*ops)


def _relu2(x):
    r = jnp.maximum(x, 0.0)
    return r * r


def _mix_fn(x, halo, mu, w0, w1, w2, a0, a1, a2, g1, g2):
    prev_row = jnp.where(pl.program_id(0) == 0, 0.0, halo[7:8, :])
    xx = _shift_down(x, prev_row) - x
    xr, xw, xk, xv, xa, xg = (x + xx * mu[i:i + 1, :] for i in range(6))
    w_pre = w0 + _mm(jnp.tanh(_mm(xw, w1)), w2)
    log_decay = -jnp.exp(-_softplus(-w_pre) - 0.5)
    a = _sigmoid(a0 + _mm(_mm(xa, a1), a2))
    g = _mm(_sigmoid(_mm(xg, g1)), g2)
    return xr, xk, xv, log_decay, a, g


def _chunk_fn(r, lw, k, v, a, k_k, k_a, r_k, inverse=None, with_inverse=False):
    n, n_pair = r.shape[0], r.shape[1] // PAIR
    to_batch = lambda t: jnp.concatenate([t[None, :, p * PAIR:(p + 1) * PAIR] for p in range(n_pair)], axis=0)
    from_batch = lambda t: jnp.concatenate([t[p] for p in range(n_pair)], axis=1)
    r, lw, k, v, a, k_k, k_a, r_k = (to_batch(t) for t in (r, lw, k, v, a, k_k, k_a, r_k))
    lane = lax.broadcasted_iota(jnp.int32, (1, 1, PAIR), 2)
    m0 = lane < HEAD

    def head_sum(t):
        s0 = jnp.sum(jnp.where(m0, t, 0.0), axis=2, keepdims=True)
        s1 = jnp.sum(jnp.where(m0, 0.0, t), axis=2, keepdims=True)
        return jnp.where(m0, s0, s1)

    kk = k * k_k
    kk = kk / jnp.maximum(jnp.sqrt(head_sum(kk * kk)), 1e-12)
    k2 = k * (1.0 + (a - 1.0) * k_a)
    aa, bb = -kk, kk * a
    bonus = head_sum(r * k2 * r_k) * v

    cum = _cumsum_rows(lw)
    p_in, p_ex, p_inv = jnp.exp(cum), jnp.exp(cum - lw), jnp.exp(-cum)
    at, rt, kt, bt = aa * p_ex, r * p_in, k2 * p_inv, bb * p_inv

    def stack_masked(t):
        return jnp.concatenate([jnp.where(m0, t, 0.0), jnp.where(m0, 0.0, t)], axis=1)

    def unstack_sum(t):
        return t[:, :n] + t[:, n:]

    def unstack_select(t):
        return jnp.where(m0, t[:, :n], t[:, n:])

    ti = lax.broadcasted_iota(jnp.int32, (1, 2 * n, 2 * n), 1)
    tj = lax.broadcasted_iota(jnp.int32, (1, 2 * n, 2 * n), 2)
    same = (ti >= n) == (tj >= n)
    incl, strict = same & (ti >= tj), same & (ti > tj)
    m = 2 * n
    rows = lambda *ts: jnp.concatenate(ts, axis=1)
    cols = lambda *ts: jnp.concatenate(ts, axis=2)
    at_s, rt_s, kt_s, bt_s = stack_masked(at), stack_masked(rt), stack_masked(kt), stack_masked(bt)
    v_s, zeros = rows(v, v), jnp.zeros_like(at_s)
    bk_s = rows(bt_s, kt_s)
    gram = _pdot(rows(at_s, rt_s), bk_s, 'nt')
    a_ab = jnp.where(strict, gram[:, :m, :m], 0.0)
    a_ak = jnp.where(strict, gram[:, :m, m:], 0.0)
    a_rb = jnp.where(incl, gram[:, m:, :m], 0.0)
    a_rk = jnp.where(incl, gram[:, m:, m:], 0.0)
    inv = _nilpotent_inverse(a_ab, n) if inverse is None else _known_inverse(a_ab, inverse)
    wy_s = _pdot(inv, cols(at_s, _pdot(a_ak, v_s, 'nn')), 'nn')
    q_oi = _pdot(cols(a_rb, a_rk), rows(wy_s, cols(zeros, v_s)), 'nn')
    q_all = unstack_sum(rt_s + q_oi[:, :, :PAIR])
    oi_all = unstack_select(q_oi[:, :, PAIR:])
    own_head = (lax.broadcasted_iota(jnp.int32, (1, m, PAIR), 1) >= n) == (lax.broadcasted_iota(jnp.int32, (1, m, PAIR), 2) >= HEAD)
    wy_m = cols(wy_s[:, :, :PAIR], jnp.where(own_head, wy_s[:, :, PAIR:], 0.0))
    mm_nn = _pdot(bk_s, rows(wy_m, cols(zeros, stack_masked(v))), 'tn')
    mm, nn = mm_nn[:, :, :PAIR], mm_nn[:, :, PAIR:]
    ei = lax.broadcasted_iota(jnp.int32, (1, PAIR, PAIR), 1)
    ej = lax.broadcasted_iota(jnp.int32, (1, PAIR, PAIR), 2)
    eye_p = (ei == ej).astype(F32)
    decay_col = jnp.sum(eye_p * p_in[:, n - 1:n, :], axis=2, keepdims=True)
    maps = ((decay_col * (eye_p + mm))[None], (decay_col * nn)[None], from_batch(q_all), from_batch(oi_all),
            from_batch(bonus))
    return maps + (inv,) if with_inverse else maps


def _gate_fn(o, bonus, g, lnx_g, lnx_b):
    lane = lax.broadcasted_iota(jnp.int32, (1, PAIR), 1)
    m0 = lane < HEAD

    def head_mean(t):
        s0 = jnp.sum(jnp.where(m0, t, 0.0), axis=1, keepdims=True)
        s1 = jnp.sum(jnp.where(m0, 0.0, t), axis=1, keepdims=True)
        return jnp.where(m0, s0, s1) * (1.0 / HEAD)

    outs = []
    for p in range(N_PAIR):
        sl = slice(p * PAIR, (p + 1) * PAIR)
        oc = o[:, sl] - head_mean(o[:, sl])
        on = oc * lax.rsqrt(head_mean(oc * oc) + GN_EPS) * lnx_g[:, sl] + lnx_b[:, sl]
        outs.append((on + bonus[:, sl]) * g[:, sl])
    return (jnp.concatenate(outs, axis=1),)


def _res_ln_fn(h, y, g, b):
    return (_layer_norm(DN_ALPHA * h + y, g, b),)


def _glu_ln_fn(h, z, g, b):
    mix = z[:, :D_MODEL] * _sigmoid(z[:, D_MODEL:])
    return (_layer_norm(DN_ALPHA * h + mix, g, b),)


def _s5_param_fn(a_re, a_im, log_dt, b_re, b_im):
    dt = jnp.exp(log_dt)
    lam_re = jnp.minimum(a_re, -1e-4)
    lam_im = a_im
    mag = jnp.exp(dt * lam_re)
    abar_re = mag * jnp.cos(dt * lam_im)
    abar_im = mag * jnp.sin(dt * lam_im)
    den = lam_re * lam_re + lam_im * lam_im
    nr, ni = abar_re - 1.0, abar_im
    coef_re = (nr * lam_re + ni * lam_im) / den
    coef_im = (ni * lam_re - nr * lam_im) / den
    return abar_re, abar_im, coef_re, coef_im


def _s5_bbar_fn(coef_re, coef_im, b_re, b_im):
    return coef_re * b_re - coef_im * b_im, coef_re * b_im + coef_im * b_re


def _s5_in_fn(u, b_re, b_im):
    return _pdot(u, b_re[0], 'nn'), _pdot(u, b_im[0], 'nn')


def _s5_out_fn(s_re, s_im, u, c_re, c_im, d):
    y = _pdot(s_re, c_re[0], 'nn') - _pdot(s_im, c_im[0], 'nn') + u * d
    return (_gelu(y),)


def _pairs_to_batch(t):
    return jnp.concatenate([t[None, :, p * PAIR:(p + 1) * PAIR] for p in range(N_PAIR)], axis=0)


def _batch_to_pairs(t):
    return jnp.concatenate([t[p] for p in range(N_PAIR)], axis=1)


STATE_PASSES = 1


def _rwkv_scan_fwd(r, lw, k, v, a, k_k, k_a, r_k):
    n_tok = r.shape[0]
    n_chunk = n_tok // CHUNK

    def body(r_ref, lw_ref, k_ref, v_ref, a_ref, kk_ref, ka_ref, rk_ref, o_ref, bonus_ref, zs_ref, inv_ref, m_ref, q_ref, z):
        @pl.when(pl.program_id(0) == 0)
        def _():
            z[...] = jnp.zeros(z.shape, F32)

        m_all, n_all, q, oi, bonus, inv = _chunk_fn(r_ref[...], lw_ref[...], k_ref[...], v_ref[...], a_ref[...],
                                                    kk_ref[...], ka_ref[...], rk_ref[...], with_inverse=True)
        zb = z[...]
        zs_ref[0] = zb
        inv_ref[0] = inv.astype(inv_ref.dtype)
        m_ref[...] = m_all
        q_ref[...] = q
        bonus_ref[...] = bonus
        o_ref[...] = _batch_to_pairs(_dot_passes(_pairs_to_batch(q), zb, 'nn', STATE_PASSES)) + oi
        z[...] = _dot_passes(m_all[0], zb, 'nn', STATE_PASSES) + n_all[0]

    tok = _spec((CHUNK, D_MODEL), lambda c: (c, 0))
    par = _spec((1, D_MODEL), lambda c: (0, 0))
    st = _spec((1, N_PAIR, PAIR, PAIR), lambda c: (c, 0, 0, 0))
    return pl.pallas_call(
        body, name="rwkv_scan_fwd", grid=(n_chunk,),
        in_specs=[tok] * 5 + [par] * 3, out_specs=[tok, tok, st, st, st, tok],
        out_shape=[jax.ShapeDtypeStruct((n_tok, D_MODEL), F32)] * 2
        + [jax.ShapeDtypeStruct((n_chunk, N_PAIR, PAIR, PAIR), F32), jax.ShapeDtypeStruct((n_chunk, N_PAIR, PAIR, PAIR), BF16),
           jax.ShapeDtypeStruct((n_chunk, N_PAIR, PAIR, PAIR), F32), jax.ShapeDtypeStruct((n_tok, D_MODEL), F32)],
        scratch_shapes=[pltpu.VMEM((N_PAIR, PAIR, PAIR), F32)],
        compiler_params=_cparams(1),
    )(r, lw, k, v, a, k_k, k_a, r_k)


def _rwkv_scan_bwd(r, lw, k, v, a, k_k, k_a, r_k, zs, invs, ms, q, d_o, d_bonus):
    n_tok = r.shape[0]
    n_chunk = n_tok // CHUNK

    def body(r_ref, lw_ref, k_ref, v_ref, a_ref, kk_ref, ka_ref, rk_ref, zs_ref, inv_ref, m_ref, q_ref, do_ref, db_ref,
             dr_ref, dlw_ref, dk_ref, dv_ref, da_ref, dkk_ref, dka_ref, drk_ref, g):
        sums = (dkk_ref, dka_ref, drk_ref)

        @pl.when(pl.program_id(0) == 0)
        def _():
            g[...] = jnp.zeros(g.shape, F32)
            for s_ref in sums:
                s_ref[...] = jnp.zeros(s_ref.shape, F32)

        prim = (r_ref[...], lw_ref[...], k_ref[...], v_ref[...], a_ref[...], kk_ref[...], ka_ref[...], rk_ref[...])
        known = inv_ref[0].astype(F32)
        _, chunk_vjp = jax.vjp(lambda *p: _chunk_fn(*p, inverse=known), *prim)
        m_all, q = m_ref[...], q_ref[...]
        gb, zb, d_out = g[...], zs_ref[0], do_ref[...]
        dob = _pairs_to_batch(d_out)
        d_m = _dot_passes(gb, zb, 'nt', STATE_PASSES)
        d_q = _batch_to_pairs(_dot_passes(dob, zb, 'nt', STATE_PASSES))
        grads = chunk_vjp((d_m[None], gb[None], d_q, d_out, db_ref[...]))
        for out_ref, val in zip((dr_ref, dlw_ref, dk_ref, dv_ref, da_ref), grads[:5]):
            out_ref[...] = val.astype(out_ref.dtype)
        for s_ref, val in zip(sums, grads[5:]):
            s_ref[...] += val
        same_head = (lax.broadcasted_iota(jnp.int32, (1, PAIR, PAIR), 1) // HEAD
                     == lax.broadcasted_iota(jnp.int32, (1, PAIR, PAIR), 2) // HEAD)
        g[...] = (_dot_passes(m_all[0], gb, 'tn', STATE_PASSES)
                  + jnp.where(same_head, _dot_passes(_pairs_to_batch(q), dob, 'tn', STATE_PASSES), 0.0))

    tok = _spec((CHUNK, D_MODEL), lambda c: (n_chunk - 1 - c, 0))
    par = _spec((1, D_MODEL), lambda c: (0, 0))
    st = _spec((1, N_PAIR, PAIR, PAIR), lambda c: (n_chunk - 1 - c, 0, 0, 0))
    tok_shape = lambda dt: jax.ShapeDtypeStruct((n_tok, D_MODEL), dt)
    return pl.pallas_call(
        body, name="rwkv_scan_bwd", grid=(n_chunk,),
        in_specs=[tok] * 5 + [par] * 3 + [st, st, st, tok, tok, tok], out_specs=[tok] * 5 + [par] * 3,
        out_shape=[tok_shape(BF16), tok_shape(F32), tok_shape(BF16), tok_shape(BF16), tok_shape(F32)]
        + [jax.ShapeDtypeStruct((1, D_MODEL), F32)] * 3,
        scratch_shapes=[pltpu.VMEM((N_PAIR, PAIR, PAIR), F32)],
        compiler_params=_cparams(1),
    )(r, lw, k, v, a, k_k, k_a, r_k, zs, invs, ms, q, d_o, d_bonus)


S5_TIME_TILE = 1024
S5_GROUP = 8
STATE_DTYPE = BF16
HALO_ROWS = 16


def _scan_rows(re_ref, im_ref, a1, carry, reverse):
    lb = re_ref.shape[1]
    grp, n_grp = S5_GROUP, re_ref.shape[0] // S5_GROUP

    def cmul(xr, xi, yr, yi):
        return xr * yr - xi * yi, xr * yi + xi * yr

    a2 = cmul(*a1, *a1)
    a4 = cmul(*a2, *a2)
    a8 = cmul(*a4, *a4)
    row = lax.broadcasted_iota(jnp.int32, (grp, lb), 0)
    expo = (grp - row) if reverse else (row + 1)
    pw = (jnp.ones((grp, lb), F32), jnp.zeros((grp, lb), F32))
    for bit, ap in ((1, a1), (2, a2), (4, a4), (8, a8)):
        nxt = cmul(*pw, *ap)
        sel = (expo & bit) != 0
        pw = (jnp.where(sel, nxt[0], pw[0]), jnp.where(sel, nxt[1], pw[1]))

    steps = []
    for d, ad in ((1, a1), (2, a2), (4, a4)):
        keep = (row < grp - d) if reverse else (row >= d)
        steps.append(((grp - d) if reverse else d, jnp.where(keep, ad[0], 0.0), jnp.where(keep, ad[1], 0.0)))

    def group(i, c):
        gi = (n_grp - 1 - i) if reverse else i
        rows = pl.ds(pl.multiple_of(gi * grp, grp), grp)
        xr, xi = re_ref[rows, :], im_ref[rows, :]
        for shift, mr, mi in steps:
            pr, pi = cmul(mr, mi, pltpu.roll(xr, shift, 0), pltpu.roll(xi, shift, 0))
            xr, xi = xr + pr, xi + pi
        cr, ci = cmul(*pw, c[0], c[1])
        xr, xi = xr + cr, xi + ci
        re_ref[rows, :] = xr
        im_ref[rows, :] = xi
        edge = slice(0, 1) if reverse else slice(grp - 1, grp)
        return xr[edge, :], xi[edge, :]

    return lax.fori_loop(0, n_grp, group, carry)


def _s5_fwd(u, b_re, b_im, a_re, a_im, c_re, c_im, d_row):
    n_tok = u.shape[0]
    tt = min(S5_TIME_TILE, n_tok)

    def body(u_ref, bre_ref, bim_ref, are_ref, aim_ref, cre_ref, cim_ref, d_ref, sre_ref, sim_ref, yg_ref, carry, wre, wim):
        @pl.when(pl.program_id(1) == 0)
        def _():
            carry[...] = jnp.zeros(carry.shape, F32)

        uv = u_ref[...]
        wre[...], wim[...] = _s5_in_fn(uv, bre_ref[...], bim_ref[...])
        c = _scan_rows(wre, wim, (are_ref[...], aim_ref[...]), (carry[0:1, :], carry[1:2, :]), False)
        carry[0:1, :] = c[0]
        carry[1:2, :] = c[1]
        s_re, s_im = wre[...], wim[...]
        sre_ref[...] = s_re.astype(sre_ref.dtype)
        sim_ref[...] = s_im.astype(sim_ref.dtype)
        (yg,) = _s5_out_fn(s_re, s_im, uv, cre_ref[...], cim_ref[...], d_ref[...])
        yg_ref[...] = yg.astype(yg_ref.dtype)

    u_blk = _spec((tt, SSM_BLK_IN), lambda l, t: (t, l))
    s_blk = _spec((tt, SSM_BLK_ST), lambda l, t: (t, l))
    blk3 = lambda arr: _spec((1,) + arr.shape[1:], lambda l, t: (l, 0, 0))
    par = lambda width: _spec((1, width), lambda l, t: (0, l))
    return pl.pallas_call(
        body, name="s5_fwd", grid=(N_SSM_BLK, n_tok // tt),
        in_specs=[u_blk, blk3(b_re), blk3(b_im), par(SSM_BLK_ST), par(SSM_BLK_ST), blk3(c_re), blk3(c_im), par(SSM_BLK_IN)],
        out_specs=[s_blk, s_blk, u_blk],
        out_shape=[jax.ShapeDtypeStruct((n_tok, SSM_LANES), STATE_DTYPE)] * 2 + [jax.ShapeDtypeStruct((n_tok, D_MODEL), BF16)],
        scratch_shapes=[pltpu.VMEM((2, SSM_BLK_ST), F32), pltpu.VMEM((tt, SSM_BLK_ST), F32), pltpu.VMEM((tt, SSM_BLK_ST), F32)],
        compiler_params=_cparams(2),
    )(u, b_re, b_im, a_re, a_im, c_re, c_im, d_row)


def _s5_bwd(dyg, u, s_re, s_im, dh_res, b_re, b_im, a_re, a_im, c_re, c_im, d_row):
    n_tok = u.shape[0]
    tt = min(S5_TIME_TILE, n_tok)
    n_t = n_tok // tt

    def body(dyg_ref, u_ref, sre_ref, sim_ref, hre_ref, him_ref, res_ref, bre_ref, bim_ref, are_ref, aim_ref,
             cre_ref, cim_ref, d_ref, du_ref, dbre_ref, dbim_ref, dcre_ref, dcim_ref, dd_ref, dare_ref, daim_ref,
             carry, gre, gim):
        i = pl.program_id(1)
        sums = (dbre_ref, dbim_ref, dcre_ref, dcim_ref, dd_ref, dare_ref, daim_ref)

        @pl.when(i == 0)
        def _():
            carry[...] = jnp.zeros(carry.shape, F32)
            for r in sums:
                r[...] = jnp.zeros(r.shape, F32)

        uv, sre, sim = u_ref[...], sre_ref[...].astype(F32), sim_ref[...].astype(F32)
        _, out_vjp = jax.vjp(_s5_out_fn, sre, sim, uv, cre_ref[...], cim_ref[...], d_ref[...])
        gre[...], gim[...], du_out, dcre, dcim, dd = out_vjp((dyg_ref[...],))
        c = _scan_rows(gre, gim, (are_ref[...], -aim_ref[...]), (carry[0:1, :], carry[1:2, :]), True)
        carry[0:1, :] = c[0]
        carry[1:2, :] = c[1]
        g_re, g_im = gre[...], gim[...]
        at_start = i == n_t - 1
        last = HALO_ROWS - 1
        p_re = _shift_down(sre, jnp.where(at_start, 0.0, hre_ref[last:last + 1, :].astype(F32)))
        p_im = _shift_down(sim, jnp.where(at_start, 0.0, him_ref[last:last + 1, :].astype(F32)))
        _, in_vjp = jax.vjp(_s5_in_fn, uv, bre_ref[...], bim_ref[...])
        du_in, dbre, dbim = in_vjp((g_re, g_im))
        du_ref[...] = du_out + du_in + res_ref[...]
        dare = jnp.sum(g_re * p_re + g_im * p_im, axis=0, keepdims=True)
        daim = jnp.sum(g_im * p_re - g_re * p_im, axis=0, keepdims=True)
        for r, val in zip(sums, (dbre, dbim, dcre, dcim, dd, dare, daim)):
            r[...] += val

    u_blk = _spec((tt, SSM_BLK_IN), lambda l, t: (n_t - 1 - t, l))
    s_blk = _spec((tt, SSM_BLK_ST), lambda l, t: (n_t - 1 - t, l))
    halo = _spec((HALO_ROWS, SSM_BLK_ST), lambda l, t: (jnp.maximum((n_t - 1 - t) * (tt // HALO_ROWS) - 1, 0), l))
    blk3 = lambda arr: _spec((1,) + arr.shape[1:], lambda l, t: (l, 0, 0))
    par = lambda width: _spec((1, width), lambda l, t: (0, l))
    params = [blk3(b_re), blk3(b_im), par(SSM_BLK_ST), par(SSM_BLK_ST), blk3(c_re), blk3(c_im), par(SSM_BLK_IN)]
    shape = lambda arr: jax.ShapeDtypeStruct(arr.shape, F32)
    return pl.pallas_call(
        body, name="s5_bwd", grid=(N_SSM_BLK, n_t),
        in_specs=[u_blk, u_blk, s_blk, s_blk, halo, halo, u_blk] + params,
        out_specs=[u_blk, blk3(b_re), blk3(b_im), blk3(c_re), blk3(c_im), par(SSM_BLK_IN), par(SSM_BLK_ST), par(SSM_BLK_ST)],
        out_shape=[shape(u), shape(b_re), shape(b_im), shape(c_re), shape(c_im), shape(d_row), shape(a_re), shape(a_im)],
        scratch_shapes=[pltpu.VMEM((2, SSM_BLK_ST), F32), pltpu.VMEM((tt, SSM_BLK_ST), F32), pltpu.VMEM((tt, SSM_BLK_ST), F32)],
        compiler_params=_cparams(2),
    )(dyg, u, s_re, s_im, s_re, s_im, dh_res, b_re, b_im, a_re, a_im, c_re, c_im, d_row)


def _loss_call(h, target, tm):
    n_tok, d = h.shape

    def body(h_ref, t_ref, acc_ref, dh_ref):
        @pl.when(pl.program_id(0) == 0)
        def _():
            acc_ref[...] = jnp.zeros(acc_ref.shape, F32)

        e = h_ref[...] - t_ref[...]
        dh_ref[...] = e * (1.0 / d)
        acc_ref[...] += jnp.sum(jnp.sum(e * e, axis=1, keepdims=True), axis=0, keepdims=True)

    tok = _spec((tm, d), lambda i: (i, 0))
    return pl.pallas_call(
        body, name="loss", grid=(n_tok // tm,),
        in_specs=[tok, tok], out_specs=[_spec((8, 128), lambda i: (0, 0)), tok],
        out_shape=[jax.ShapeDtypeStruct((8, 128), F32), jax.ShapeDtypeStruct(h.shape, F32)],
        compiler_params=_cparams(1),
    )(h, target)


def _block_diag(t):
    nb, ng, rows, cols = t.shape
    eye = jnp.eye(ng, dtype=t.dtype)
    return jnp.einsum('bgrc,gh->bgrhc', t, eye).reshape(nb, ng * rows, ng * cols)


def _block_diag_t(t, rows, cols):
    nb, ng = t.shape[0], t.shape[1] // rows
    t5 = t.reshape(nb, ng, rows, ng, cols)
    return jnp.einsum('bgrhc,gh->bgrc', t5, jnp.eye(ng, dtype=t.dtype))


def _mlp_fwd(layer, h, w1g, w2g):
    pre = _matmul(f"mlp{layer}_up", h, w1g, 'nn', b_cols=(layer,), out_dtype=BF16, tm=WIDE_ROW_TILE)
    return pre, _matmul(f"mlp{layer}_down", pre, w2g, 'nn', lhs_fn=_relu2, b_rows=(layer,), out_dtype=BRANCH_DTYPE)


def _mlp_bwd(layer, h, pre, dy, dh_res, w1g, w2g):
    dpre = _matmul(f"mlp{layer}_down_dx", dy, w2g, 'nt', epi=lambda acc, p: acc * (2.0 * jnp.maximum(p, 0.0)),
                   extra=pre, b_rows=(layer,), out_dtype=BF16, tm=WIDE_ROW_TILE)
    dw2 = _matmul(f"mlp{layer}_down_dw", pre, dy, 'tn', lhs_fn=_relu2, out_dtype=GRAD_DTYPE)
    dw1 = _matmul(f"mlp{layer}_up_dw", h, dpre, 'tn', out_cols=True, out_dtype=GRAD_DTYPE)
    dh = _matmul(f"mlp{layer}_up_dx", dpre, w1g, 'nt', epi=lambda acc, e: acc + e, extra=dh_res, b_cols=(layer,))
    return dh, dw1, dw2


def _local_step(x, target, w, mid, start_late, exchange):
    n_tok = x.shape[0]
    tm = min(TOKEN_TILE, n_tok)
    n_tile, n_chunk = n_tok // tm, n_tok // CHUNK
    row = lambda v: v.reshape(1, -1)
    big = {}
    ln_g, ln_b = w['ln_g'], w['ln_b']
    grads = {}
    to_slots = lambda t: t.reshape(N_DEV, t.shape[0] // N_DEV, t.shape[1])

    halo_spec = (8, D_MODEL), lambda i: (jnp.maximum(i * (tm // 8) - 1, 0), 0)
    mix_p = [_tile(x, tm), (x,) + halo_spec, _full(w['rw_mu'][0]), _full(row(w['rw_w0'])), _full(w['rw_w1'][0]),
             _full(w['rw_w2'][0]), _full(row(w['rw_a0'])), _full(w['rw_a1'][0]), _full(w['rw_a2'][0]),
             _full(w['rw_g1'][0]), _full(w['rw_g2'][0])]
    tok_out = lambda dt: ((n_tok, D_MODEL), dt, (tm, D_MODEL), lambda i: (i, 0))
    xr, xk, xv, lw, a, g = _fwd_call("rwkv_mix", _mix_fn, (n_tile,), mix_p,
                                     [tok_out(BF16), tok_out(BF16), tok_out(BF16), tok_out(F32), tok_out(F32), tok_out(F32)])
    mid.forward(xr)
    square = lambda t: t.reshape(D_MODEL, D_MODEL)
    big.update({n: square(t) for n, t in mid.finish(xr).items()})
    late = start_late(big['rw_wr'])
    r = _matmul("rwkv_r", xr, big['rw_wr'], 'nn')
    k = _matmul("rwkv_k", xk, big['rw_wk'], 'nn')
    v = _matmul("rwkv_v", xv, big['rw_wv'], 'nn')

    scan_in = (r, lw, k, v, a, row(w['rw_k_k']) + late.zero, row(w['rw_k_a']), w['rw_r_k'].reshape(1, -1))
    o, bonus, *scan_saved = _rwkv_scan_fwd(*scan_in)

    zero = late.forward(o)
    tl = min(LN_TILE, n_tok)
    n_ln = n_tok // tl
    ln_out = lambda dt: ((n_tok, D_MODEL), dt, (tl, D_MODEL), lambda i: (i, 0))
    gate_p = [_tile(o, tl), _tile(bonus, tl), _tile(g, tl), _full(row(w['rw_lnx_g']) + zero), _full(row(w['rw_lnx_b']))]
    (og,) = _fwd_call("rwkv_gate", _gate_fn, (n_ln,), gate_p, [ln_out(BF16)])
    late_w = late.finish(og)
    big['rw_wo'] = square(late_w['rw_wo'])
    w1g, w2g, glu_g = late_w['mlp_w1'], late_w['mlp_w2'], late_w['s5_w_glu']
    y0 = _matmul("rwkv_o", og, big['rw_wo'], 'nn', out_dtype=BRANCH_DTYPE)
    ln0_p = [_tile(x, tl), _tile(y0, tl), _full(ln_g[0:1]), _full(ln_b[0:1])]
    twice = lambda fn: lambda *args: fn(*args) * 2
    h1, h1_b = _fwd_call("ln0", twice(_res_ln_fn), (n_ln,), ln0_p, [ln_out(F32), ln_out(BF16)])
    pre0, y1 = _mlp_fwd(0, h1_b, w1g, w2g)
    ln1_p = [_tile(h1, tl), _tile(y1, tl), _full(ln_g[1:2]), _full(ln_b[1:2])]
    (h2,) = _fwd_call("ln1", _res_ln_fn, (n_ln,), ln1_p, [ln_out(F32)])

    gp = (SSM_GROUPS, SSM_STATE)
    par_p = [_full(w['s5_a_re'][0]), _full(w['s5_a_im'][0]), _full(w['s5_log_dt'].reshape(SSM_GROUPS, 1))]
    s5_par_fn = lambda ar, ai, ld: _s5_param_fn(ar, ai, ld, None, None)
    gp_out = (gp, F32, gp, lambda i: (0, 0))
    abar_re, abar_im, coef_re, coef_im = _fwd_call("s5_param", s5_par_fn, (1,), par_p, [gp_out] * 4)
    b_flat = [w[n][0].reshape(SSM_LANES, SSM_GROUP) for n in ('s5_b_re', 's5_b_im')]
    bbar_p = [_full(coef_re.reshape(SSM_LANES, 1)), _full(coef_im.reshape(SSM_LANES, 1)), _full(b_flat[0]), _full(b_flat[1])]
    bb_out = ((SSM_LANES, SSM_GROUP), F32, (SSM_LANES, SSM_GROUP), lambda i: (0, 0))
    bbar_re, bbar_im = _fwd_call("s5_bbar", _s5_bbar_fn, (1,), bbar_p, [bb_out] * 2)
    to_in = lambda t: _block_diag(t.reshape(N_SSM_BLK, 8, SSM_STATE, SSM_GROUP).transpose(0, 1, 3, 2))
    to_out = lambda t: _block_diag(t.reshape(N_SSM_BLK, 8, SSM_GROUP, SSM_STATE).transpose(0, 1, 3, 2))
    bblk_re, bblk_im = to_in(bbar_re), to_in(bbar_im)
    cblk_re, cblk_im = to_out(w['s5_c_re'][0]), to_out(w['s5_c_im'][0])
    a_row_re, a_row_im = abar_re.reshape(1, SSM_LANES), abar_im.reshape(1, SSM_LANES)

    d_row = row(w['s5_d'])
    s5_params = (bblk_re, bblk_im, a_row_re, a_row_im, cblk_re, cblk_im, d_row)
    s_re, s_im, yg = _s5_fwd(h2, *s5_params)
    z = _matmul("s5_glu", yg, glu_g, 'nn', b_cols=(0,), out_dtype=BRANCH_DTYPE)
    ln2_p = [_tile(h2, tl), _tile(z, tl), _full(ln_g[2:3]), _full(ln_b[2:3])]
    h3, h3_b = _fwd_call("ln2", twice(_glu_ln_fn), (n_ln,), ln2_p, [ln_out(F32), ln_out(BF16)])
    pre1, y3 = _mlp_fwd(1, h3_b, w1g, w2g)
    ln3_p = [_tile(h3, tl), _tile(y3, tl), _full(ln_g[3:4]), _full(ln_b[3:4])]
    (h4,) = _fwd_call("ln3", _res_ln_fn, (n_ln,), ln3_p, [ln_out(F32)])

    loss_acc, dh4 = _loss_call(h4, target, tl)

    ln_w = ['tile', 'tile_bf16', 'acc', 'acc']
    dh3_res, dy3, dg3, db3 = _bwd_call("ln3_bwd", _res_ln_fn, (n_ln,), ln3_p, [_tile(dh4, tl)], ln_w)
    dh3, dw1_1, dw2_1 = _mlp_bwd(1, h3_b, pre1, dy3, dh3_res, w1g, w2g)
    zero = exchange([('mlp_w1', 1, dw1_1), ('mlp_w2', 1, to_slots(dw2_1))])
    ln2_p[2] = _full(ln_g[2:3] + zero)
    dh2_res, dz, dg2, db2 = _bwd_call("ln2_bwd", _glu_ln_fn, (n_ln,), ln2_p, [_tile(dh3, tl)], ln_w)
    dyg = _matmul("s5_glu_dx", dz, glu_g, 'nt', b_cols=(0,))
    dw_glu = _matmul("s5_glu_dw", yg, dz, 'tn', out_cols=True, out_dtype=GRAD_DTYPE)
    dh2, dbb_re, dbb_im, dcb_re, dcb_im, dd, dabar_re, dabar_im = _s5_bwd(dyg, h2, s_re, s_im, dh2_res, *s5_params)
    from_in = lambda t: _block_diag_t(t, SSM_GROUP, SSM_STATE).transpose(0, 1, 3, 2).reshape(SSM_LANES, SSM_GROUP)
    from_out = lambda t: _block_diag_t(t, SSM_STATE, SSM_GROUP).transpose(0, 1, 3, 2).reshape(1, SSM_GROUPS, SSM_GROUP, SSM_STATE)
    grads['s5_c_re'], grads['s5_c_im'] = from_out(dcb_re), from_out(dcb_im)
    grads['s5_d'] = dd
    dcoef_re, dcoef_im, db_re, db_im = _bwd_call(
        "s5_bbar_bwd", _s5_bbar_fn, (1,), bbar_p, [_full(from_in(dbb_re)), _full(from_in(dbb_im))], ['acc'] * 4)
    grads['s5_b_re'] = db_re.reshape(w['s5_b_re'].shape)
    grads['s5_b_im'] = db_im.reshape(w['s5_b_im'].shape)
    par_ct = [_full(dabar_re.reshape(gp)), _full(dabar_im.reshape(gp)), _full(dcoef_re.reshape(gp)), _full(dcoef_im.reshape(gp))]
    da_re, da_im, dlog_dt = _bwd_call("s5_param_bwd", s5_par_fn, (1,), par_p, par_ct, ['acc'] * 3)
    grads['s5_a_re'], grads['s5_a_im'], grads['s5_log_dt'] = da_re[None], da_im[None], dlog_dt.reshape(1, SSM_GROUPS)
    zero = exchange([(n, 0, grads[n]) for n in ('s5_a_re', 's5_a_im', 's5_log_dt', 's5_b_re', 's5_b_im', 's5_c_re', 's5_c_im')])
    ln1_p[2] = _full(ln_g[1:2] + zero)

    dh1_res, dy1, dg1, db1 = _bwd_call("ln1_bwd", _res_ln_fn, (n_ln,), ln1_p, [_tile(dh2, tl)], ln_w)
    dh1, dw1_0, dw2_0 = _mlp_bwd(0, h1_b, pre0, dy1, dh1_res, w1g, w2g)
    dx_res, dy0, dg0, db0 = _bwd_call("ln0_bwd", _res_ln_fn, (n_ln,), ln0_p, [_tile(dh1, tl)], ln_w)
    grads['ln_g'] = jnp.concatenate([dg0, dg1, dg2, dg3])
    grads['ln_b'] = jnp.concatenate([db0, db1, db2, db3])
    dog = _matmul("rwkv_o_dx", dy0, big['rw_wo'], 'nt')
    dw_o = to_slots(_matmul("rwkv_o_dw", og, dy0, 'tn', out_dtype=GRAD_DTYPE))
    zero = exchange([('s5_w_glu', 0, dw_glu), ('mlp_w1', 0, dw1_0), ('mlp_w2', 0, to_slots(dw2_0)), ('rw_wo', 0, dw_o)])
    gate_p[3] = _full(row(w['rw_lnx_g']) + zero)
    d_o, dbonus, dgate, grads['rw_lnx_g'], grads['rw_lnx_b'] = _bwd_call(
        "rwkv_gate_bwd", _gate_fn, (n_ln,), gate_p, [_tile(dog, tl)], ['tile', 'tile', 'tile', 'acc', 'acc'])
    dr, dlw, dk, dv, da, grads['rw_k_k'], grads['rw_k_a'], dr_k = _rwkv_scan_bwd(*scan_in, *scan_saved, d_o, dbonus)
    grads['rw_r_k'] = dr_k.reshape(w['rw_r_k'].shape)
    dw_r = to_slots(_matmul("rwkv_r_dw", xr, dr, 'tn', out_dtype=GRAD_DTYPE))
    dw_k = to_slots(_matmul("rwkv_k_dw", xk, dk, 'tn', out_dtype=GRAD_DTYPE))
    dw_v = to_slots(_matmul("rwkv_v_dw", xv, dv, 'tn', out_dtype=GRAD_DTYPE))
    done = ('ln_g', 'ln_b', 'rw_lnx_g', 'rw_lnx_b', 'rw_k_k', 'rw_k_a', 'rw_r_k')
    zero = exchange([('rw_wr', 0, dw_r), ('rw_wk', 0, dw_k), ('rw_wv', 0, dw_v)] + [(n, 0, grads[n]) for n in done])
    mix_p[3] = _full(row(w['rw_w0']) + zero)
    dxr = _matmul("rwkv_r_dx", dr, big['rw_wr'], 'nt')
    dxk = _matmul("rwkv_k_dx", dk, big['rw_wk'], 'nt')
    dxv = _matmul("rwkv_v_dx", dv, big['rw_wv'], 'nt')
    mix_ct = [_tile(t, tm) for t in (dxr, dxk, dxv, dlw, da, dgate)]
    halo_grad = ('tile', ((n_tile * 8, D_MODEL), (8, D_MODEL), lambda i: (i, 0)))
    res = _bwd_call("rwkv_mix_bwd", _mix_fn, (n_tile,), mix_p, mix_ct, ['tile', halo_grad] + ['acc'] * 9,
                    addends={0: _tile(dx_res, tm)})
    corr = res[1].reshape(n_tile, 8, D_MODEL)[1:, 7:8, :]
    corr = jnp.pad(corr, ((0, 1), (tm - 1, 0), (0, 0)))
    dx = (res[0].reshape(n_tile, tm, D_MODEL) + corr).reshape(n_tok, D_MODEL)
    (grads['rw_mu'], grads['rw_w0'], grads['rw_w1'], grads['rw_w2'], grads['rw_a0'], grads['rw_a1'], grads['rw_a2'],
     grads['rw_g1'], grads['rw_g2']) = [t[None] if t.shape[0] != 1 else t for t in res[2:]]
    return loss_acc[0, 0], dx, grads


def _full_shape(local_shape, axis):
    return local_shape[:axis] + (N_DEV * local_shape[axis],) + local_shape[axis + 1:]


def _split_shape(local_shape, axis):
    return local_shape[:axis] + (N_DEV, local_shape[axis]) + local_shape[axis + 1:]


def _mesh_pos():
    return lax.axis_index("x"), lax.axis_index("y"), lax.axis_index("c")


def _flip(v, f):
    return 1 - v if f else v


def _hbm_call(name, body, arrays, out_shapes, n_sems):
    n_t = len(arrays)
    hbm = pl.BlockSpec(memory_space=pl.ANY)
    return pl.pallas_call(
        body, name=name, out_shape=out_shapes, in_specs=[hbm] * n_t, out_specs=[hbm] * n_t,
        scratch_shapes=[pltpu.SemaphoreType.DMA((n_t, n_sems)), pltpu.SemaphoreType.DMA((n_t, n_sems)),
                        pltpu.SemaphoreType.DMA((n_t,))],
    )(*arrays)


def _all_gather(name, locals_):
    n_t = len(locals_)

    def body(*refs):
        x_refs, out_refs = refs[:n_t], refs[n_t:2 * n_t]
        send_sems, recv_sems, local_sems = refs[2 * n_t:]
        x, y, c = _mesh_pos()
        me, sibling = (x, y, c), (x, y, 1 - c)
        chips = [(1 - x, y), (x, 1 - y), (1 - x, 1 - y)]

        def copy(t, k, block, to, own=False):
            slot = out_refs[t].at[4 * block[0] + 2 * block[1] + block[2]]
            return pltpu.make_async_remote_copy(
                src_ref=x_refs[t] if own else slot, dst_ref=slot,
                send_sem=send_sems.at[t, k], recv_sem=recv_sems.at[t, k],
                device_id=to, device_id_type=pl.DeviceIdType.MESH)

        mine = [pltpu.make_async_copy(x_refs[t], out_refs[t].at[4 * x + 2 * y + c], local_sems.at[t]) for t in range(n_t)]
        for cp in mine:
            cp.start()
        sent = []
        for t in range(n_t):
            sent.append(copy(t, 0, me, sibling, own=True))
            sent += [copy(t, 1 + j, me, (*chip, c), own=True) for j, chip in enumerate(chips)]
        for cp in sent:
            cp.start()
        for j, chip in enumerate(chips):
            for t in range(n_t):
                copy(t, 1 + j, (*chip, c), me).wait_recv()
                passed = copy(t, 4 + j, (*chip, c), sibling)
                passed.start()
                sent.append(passed)
        for t in range(n_t):
            copy(t, 0, sibling, me).wait_recv()
            for j, chip in enumerate(chips):
                copy(t, 4 + j, (*chip, 1 - c), me).wait_recv()
        for cp in sent:
            cp.wait_send()
        for cp in mine:
            cp.wait()

    outs = [jax.ShapeDtypeStruct((N_DEV,) + a.shape, a.dtype) for a in locals_]
    return _hbm_call(name, body, locals_, outs, 7)


def _all_to_all(name, blocks):
    n_t = len(blocks)

    def body(*refs):
        x_refs, out_refs = refs[:n_t], refs[n_t:2 * n_t]
        send_sems, recv_sems, local_sems = refs[2 * n_t:]
        x, y, c = _mesh_pos()
        my_slot = 4 * x + 2 * y + c
        mine = [pltpu.make_async_copy(x_refs[t].at[my_slot], out_refs[t].at[my_slot], local_sems.at[t]) for t in range(n_t)]
        for cp in mine:
            cp.start()
        copies = []
        for k in range(1, N_DEV):
            peer = (_flip(x, k & 4), _flip(y, k & 2), _flip(c, k & 1))
            peer_slot = 4 * peer[0] + 2 * peer[1] + peer[2]
            for t in range(n_t):
                sems = dict(send_sem=send_sems.at[t, k - 1], recv_sem=recv_sems.at[t, k - 1],
                            device_id=peer, device_id_type=pl.DeviceIdType.MESH)
                cp = pltpu.make_async_remote_copy(src_ref=x_refs[t].at[peer_slot], dst_ref=out_refs[t].at[my_slot], **sems)
                cp.start()
                landed = pltpu.make_async_remote_copy(src_ref=x_refs[t].at[my_slot], dst_ref=out_refs[t].at[peer_slot], **sems)
                copies.append((cp, landed))
        for sent, landed in copies:
            landed.wait_recv()
            sent.wait_send()
        for cp in mine:
            cp.wait()

    outs = [jax.ShapeDtypeStruct(a.shape, a.dtype) for a in blocks]
    return _hbm_call(name, body, blocks, outs, 7)


_HBM = pl.BlockSpec(memory_space=pltpu.HBM)
_SEM = pl.BlockSpec(memory_space=pltpu.SEMAPHORE)
_EFFECT = pltpu.SideEffectType.DATAFLOW_SIDE_EFFECTING


def _peer_copies(x_refs, land_refs, send_sems, recv_sems, gather, mine):
    x, y, c = _mesh_pos()
    my_slot = 4 * x + 2 * y + c
    copies = []
    for k in range(1, N_DEV):
        peer = (_flip(x, k & 4), _flip(y, k & 2), _flip(c, k & 1))
        peer_slot = 4 * peer[0] + 2 * peer[1] + peer[2]
        for t, (x_ref, land) in enumerate(zip(x_refs, land_refs)):
            sem = t * (N_DEV - 1) + k - 1
            copies.append(pltpu.make_async_remote_copy(
                src_ref=x_ref if gather[t] else x_ref.at[peer_slot], dst_ref=land.at[my_slot if mine else peer_slot],
                send_sem=send_sems.at[sem], recv_sem=recv_sems.at[sem], device_id=peer, device_id_type=pl.DeviceIdType.MESH))
    return copies


def _split_start(name, held, plan, n_copies, after=None):
    n_h = len(held)

    n_in = n_h + (after is not None)

    def body(*refs):
        for sent in plan(refs[:n_h], refs[n_in], refs[n_in + 1], True):
            sent.start()
        refs[-1][...] = jnp.zeros((8, 128), F32)

    sems = pltpu.SemaphoreType.DMA((n_copies,))
    out = pl.pallas_call(
        body, name=name,
        out_shape=(sems, sems, *[pltpu.HBM(a.shape, a.dtype) for a in held], jax.ShapeDtypeStruct((8, 128), F32)),
        in_specs=[_HBM] * n_h + [pl.BlockSpec(memory_space=pl.ANY)] * (after is not None),
        out_specs=(_SEM, _SEM, *[_HBM] * n_h, pl.BlockSpec(memory_space=pltpu.VMEM)),
        input_output_aliases={i: 2 + i for i in range(n_h)},
        compiler_params=pltpu.CompilerParams(has_side_effects=_EFFECT),
    )(*[pltpu.with_memory_space_constraint(a, pltpu.HBM) for a in held], *([after] if after is not None else []))
    return dict(send=out[0], recv=out[1], held=out[2:2 + n_h], zero=out[-1][0, 0], plan=plan)


def _split_wait(name, started, after):
    held, plan = started['held'], started['plan']
    n_h = len(held)

    def body(*refs):
        for sent in plan(refs[:n_h], refs[n_h], refs[n_h + 1], True):
            sent.wait_send()
        for landed in plan(refs[:n_h], refs[n_h], refs[n_h + 1], False):
            landed.wait_recv()

    return pl.pallas_call(
        body, name=name,
        out_shape=[pltpu.HBM(a.shape, a.dtype) for a in held],
        in_specs=[_HBM] * n_h + [_SEM, _SEM, pl.BlockSpec(memory_space=pl.ANY)],
        out_specs=[_HBM] * n_h,
        input_output_aliases={i: i for i in range(n_h)},
        compiler_params=pltpu.CompilerParams(has_side_effects=_EFFECT),
    )(*held, started['send'], started['recv'], after)


def _push_start(name, arrays, gather):
    n_t = len(arrays)
    lands = [lax.empty((N_DEV,) + a.shape if whole else a.shape, a.dtype) for a, whole in zip(arrays, gather)]
    plan = lambda refs, send, recv, mine: _peer_copies(refs[:n_t], refs[n_t:], send, recv, gather, mine)
    return _split_start(name, list(arrays) + lands, plan, n_t * (N_DEV - 1))


def _push_wait(name, started, after):
    held = _split_wait(name, started, after)
    return held[:len(held) // 2], held[len(held) // 2:]


def _chip_copies(refs, send_sems, recv_sems, mine):
    n_t = len(refs) // 2
    x, y, c = _mesh_pos()
    my_slot = 4 * x + 2 * y + c
    peers = [(x, y, 1 - c), (1 - x, y, c), (x, 1 - y, c), (1 - x, 1 - y, c)]
    copies = []
    for j, peer in enumerate(peers):
        peer_slot = 4 * peer[0] + 2 * peer[1] + peer[2]
        for t in range(n_t):
            copies.append(pltpu.make_async_remote_copy(
                src_ref=refs[t], dst_ref=refs[n_t + t].at[my_slot if mine else peer_slot],
                send_sem=send_sems.at[4 * t + j], recv_sem=recv_sems.at[4 * t + j],
                device_id=peer, device_id_type=pl.DeviceIdType.MESH))
    return copies


def _forward_copies(refs, send_sems, recv_sems, mine):
    x, y, c = _mesh_pos()
    sibling = (x, y, 1 - c)
    copies = []
    for j, (px, py) in enumerate([(1 - x, y), (x, 1 - y), (1 - x, 1 - y)]):
        slot = 4 * px + 2 * py + (c if mine else 1 - c)
        for t, land in enumerate(refs):
            copies.append(pltpu.make_async_remote_copy(
                src_ref=land.at[slot], dst_ref=land.at[slot], send_sem=send_sems.at[3 * t + j], recv_sem=recv_sems.at[3 * t + j],
                device_id=sibling, device_id_type=pl.DeviceIdType.MESH))
    return copies


class _TwoStageGather:
    def __init__(self, name, arrays, my_slot, after=None):
        self.name, self.arrays, self.my_slot = name, arrays, my_slot
        values = list(arrays.values())
        lands = [lax.empty((N_DEV,) + a.shape, a.dtype) for a in values]
        self.stage = _split_start(name + "_start", values + lands, _chip_copies, 4 * len(values), after)
        self.zero = self.stage['zero']

    def forward(self, after):
        held = _split_wait(self.name + "_wait", self.stage, after)
        self.stage = _split_start(self.name + "_forward_start", held[len(self.arrays):], _forward_copies, 3 * len(self.arrays))
        return self.stage['zero']

    def finish(self, after):
        lands = _split_wait(self.name + "_forward_wait", self.stage, after)
        return {n: lax.dynamic_update_slice(land, a[None], (self.my_slot,) + (0,) * a.ndim)
                for (n, a), land in zip(self.arrays.items(), lands)}


ADAM_BLOCK = 128 * 1024


def _adam_update(part, w_ref, m_ref, v_ref, g_ref, d_ref, nm_ref, nv_ref):
    g = part(0)
    for i in range(1, N_DEV):
        g = g + part(i)
    m_new = ADAM_B1 * m_ref[...] + (1.0 - ADAM_B1) * g
    v_new = ADAM_B2 * v_ref[...] + (1.0 - ADAM_B2) * (g * g)
    m_hat = m_new * (1.0 / (1.0 - ADAM_B1 ** ADAM_STEP))
    v_hat = v_new * (1.0 / (1.0 - ADAM_B2 ** ADAM_STEP))
    g_ref[...] = g
    nm_ref[...] = m_new
    nv_ref[...] = v_new
    d_ref[...] = -ADAM_LR * (m_hat / (jnp.sqrt(v_hat) + ADAM_EPS) + ADAM_WD * w_ref[...])


def _adam_many(name, items):
    n_t = len(items)
    n_in = [4 if own is None else 5 for *_, own in items]
    first = [sum(n_in[:t]) for t in range(n_t)]

    def body(*refs):
        ins, outs = refs[:sum(n_in)], refs[sum(n_in):]
        x, y, c = _mesh_pos()
        my_slot = 4 * x + 2 * y + c
        for t in range(n_t):
            s_ref, w_ref, m_ref, v_ref = ins[first[t]:first[t] + 4]
            if n_in[t] == 4:
                part = lambda i: s_ref[i]
            else:
                own_ref = ins[first[t] + 4]
                part = lambda i: jnp.where(my_slot == i, own_ref[...], s_ref[i])
            _adam_update(part, w_ref, m_ref, v_ref, *outs[4 * t:4 * t + 4])

    vmem = pl.BlockSpec(memory_space=pltpu.VMEM)
    out = pl.pallas_call(
        body, name=name, in_specs=[vmem] * sum(n_in), out_specs=[vmem] * (4 * n_t),
        out_shape=[jax.ShapeDtypeStruct(item[1].shape, F32) for item in items for _ in range(4)],
        compiler_params=pltpu.CompilerParams(vmem_limit_bytes=VMEM_LIMIT),
    )(*[a for item in items for a in item if a is not None])
    return [out[4 * t:4 * t + 4] for t in range(n_t)]


def _adam_call(name, slots, w, m, v, own=None):
    n_layer, rows, width = w.shape
    tile = min(rows, ADAM_BLOCK // max(width, 128))

    def body(*refs):
        s_refs = refs[:n_layer]
        own_refs = refs[n_layer:2 * n_layer] if own is not None else None
        w_ref, m_ref, v_ref = refs[-7:-4]
        layer = pl.program_id(0)
        x, y, c = _mesh_pos()
        my_slot = 4 * x + 2 * y + c
        for l in range(n_layer):
            @pl.when(layer == l)
            def _(l=l):
                if own is None:
                    part = lambda i: s_refs[l][i].astype(F32)
                else:
                    part = lambda i: jnp.where(my_slot == i, own_refs[l][...], s_refs[l][i]).astype(F32)
                _adam_update(part, w_ref, m_ref, v_ref, *refs[-4:])

    slot_spec = lambda k: _spec((N_DEV, tile, width), lambda l, i: (0, jnp.where(l == k, i, 0), 0))
    own_spec = lambda k: _spec((tile, width), lambda l, i: (jnp.where(l == k, i, 0), 0))
    blk = _spec((None, tile, width), lambda l, i: (l, i, 0))
    specs = [slot_spec(k) for k in range(n_layer)] + ([own_spec(k) for k in range(n_layer)] if own is not None else [])
    return pl.pallas_call(
        body, name=name, grid=(n_layer, rows // tile),
        in_specs=specs + [blk] * 3, out_specs=[blk] * 4, out_shape=[jax.ShapeDtypeStruct(w.shape, F32)] * 4,
        compiler_params=_cparams(2),
    )(*slots, *(own or []), w, m, v)


def kernel(x, ln_g, ln_b, rw_mu, rw_w0, rw_w1, rw_w2, rw_a0, rw_a1, rw_a2, rw_g1, rw_g2, rw_k_k, rw_k_a, rw_r_k, rw_wr, rw_wk, rw_wv, rw_wo, rw_lnx_g, rw_lnx_b, s5_a_re, s5_a_im, s5_log_dt, s5_b_re, s5_b_im, s5_c_re, s5_c_im, s5_d, s5_w_glu, mlp_w1, mlp_w2, loss_target, m_ln_g, m_ln_b, m_rw_mu, m_rw_w0, m_rw_w1, m_rw_w2, m_rw_a0, m_rw_a1, m_rw_a2, m_rw_g1, m_rw_g2, m_rw_k_k, m_rw_k_a, m_rw_r_k, m_rw_wr, m_rw_wk, m_rw_wv, m_rw_wo, m_rw_lnx_g, m_rw_lnx_b, m_s5_a_re, m_s5_a_im, m_s5_log_dt, m_s5_b_re, m_s5_b_im, m_s5_c_re, m_s5_c_im, m_s5_d, m_s5_w_glu, m_mlp_w1, m_mlp_w2, v_ln_g, v_ln_b, v_rw_mu, v_rw_w0, v_rw_w1, v_rw_w2, v_rw_a0, v_rw_a1, v_rw_a2, v_rw_g1, v_rw_g2, v_rw_k_k, v_rw_k_a, v_rw_r_k, v_rw_wr, v_rw_wk, v_rw_wv, v_rw_wo, v_rw_lnx_g, v_rw_lnx_b, v_s5_a_re, v_s5_a_im, v_s5_log_dt, v_s5_b_re, v_s5_b_im, v_s5_c_re, v_s5_c_im, v_s5_d, v_s5_w_glu, v_mlp_w1, v_mlp_w2):
    given = dict(locals())
    local_w = {n: given[n] for n in WEIGHTS}
    local_m = {n: given["m_" + n] for n in WEIGHTS}
    local_v = {n: given["v_" + n] for n in WEIGHTS}
    small = [n for n in WEIGHTS if n in SHARDED and n not in BIG]

    my_slot = 4 * lax.axis_index("x") + 2 * lax.axis_index("y") + lax.axis_index("c")
    as_bf16 = {n: local_w[n].astype(BF16) for n in BIG}

    gathered = _all_gather("weights_all_gather", [local_w[n] for n in small])
    full = dict(local_w)
    for n, blk in zip(small, gathered):
        full[n] = jnp.moveaxis(blk, 0, SHARDED[n]).reshape(_full_shape(local_w[n].shape, SHARDED[n]))
    mid = _TwoStageGather("weights_mid", {n: as_bf16[n] for n in MID}, my_slot)
    late = lambda after: _TwoStageGather("weights_late", {n: as_bf16[n] for n in LATE}, my_slot, after)
    full['rw_w0'] = full['rw_w0'] + mid.zero

    exchanges = []

    def exchange(entries):
        started = _push_start(f"grads_start_{len(exchanges)}", [t for _, _, t in entries], [n in REPLICATED for n, _, _ in entries])
        exchanges.append((entries, started))
        return started['zero']

    loss_sq, dx, grads = _local_step(x[0], loss_target[0], full, mid, late, exchange)
    loss = (0.5 / D_MODEL) * lax.psum(loss_sq, MESH_AXES)

    wide = {'s5_b_re': (SSM_LANES, SSM_GROUP), 's5_b_im': (SSM_LANES, SSM_GROUP),
            's5_c_re': (D_MODEL, SSM_STATE), 's5_c_im': (D_MODEL, SSM_STATE)}
    by_name, many = {}, []
    landed = {n: {} for n in BIG}
    for e, (entries, started) in enumerate(exchanges):
        sources, lands = _push_wait(f"grads_wait_{e}", started, dx)
        for (n, i, _), src, slots in zip(entries, sources, lands):
            if n in wide:
                as_view = lambda t: t.reshape((1,) + wide[n])
                out = _adam_call(f"adamw_{n}", [slots.reshape((N_DEV,) + wide[n])], as_view(local_w[n]), as_view(local_m[n]),
                                 as_view(local_v[n]), own=[src.reshape(wide[n])])
                by_name[n] = [t.reshape(local_w[n].shape) for t in out]
            elif n in REPLICATED:
                many.append((n, (slots, local_w[n], local_m[n], local_v[n], src)))
            else:
                landed[n][i] = (slots, lax.dynamic_index_in_dim(src, my_slot, 0, keepdims=False))
    for n in BIG:
        layers = [landed[n][i] for i in sorted(landed[n])]
        by_name[n] = _adam_call(f"adamw_{n}", [s for s, _ in layers], local_w[n], local_m[n], local_v[n], own=[o for _, o in layers])
    split = lambda n: jnp.moveaxis(grads[n].reshape(_split_shape(local_w[n].shape, SHARDED[n])), SHARDED[n], 0)
    for n, slots in zip(small, _all_to_all("grads_all_to_all", [split(n) for n in small])):
        many.append((n, (slots, local_w[n], local_m[n], local_v[n], None)))
    last = [n for n in REPLICATED if n not in by_name and n not in dict(many)]
    for n, slots in zip(last, _all_gather("grads_all_gather", [grads[n] for n in last])):
        many.append((n, (slots, local_w[n], local_m[n], local_v[n], None)))
    for (n, _), out in zip(many, _adam_many("adamw_small", [item for _, item in many])):
        by_name[n] = out
    results = [by_name[n][k] for k in range(4) for n in WEIGHTS]
    return (loss, dx[None], *results)
```
